```python
import jax, jax.numpy as jnp
from jax import lax
import numpy as np

D_MODEL = 1024
BATCH = 8
SEQ = 4096
DEPTH = 2

N_MIXERS = 2
N_POOL_GROUPS = 4
POOL_GROUP = D_MODEL // N_POOL_GROUPS
POOL_WINDOWS = (2, 4, 8, 16)
N_HEADS = 16
QK_NOPE = 64
QK_ROPE = 32
V_HEAD = 64
Q_LORA = D_MODEL // 4
KV_LORA = D_MODEL // 8
ROPE_THETA = 10000.0
D_FF = 11 * D_MODEL // 4
Q_BLOCK = 128
EPS = 1e-6
N_MOD = 9
N_POOL_LAYERS = (DEPTH + 1) // 2
N_MLA_LAYERS = DEPTH // 2
ATTN_SCALE = (QK_NOPE + QK_ROPE) ** -0.5

kernel_name = "hybrid_pool_mla_macaron_encoder"


def rmsnorm(x, g):
    xf = x.astype(jnp.float32)
    y = xf * lax.rsqrt(jnp.mean(xf * xf, axis=-1, keepdims=True) + EPS)
    return (y * g.astype(jnp.float32)).astype(x.dtype)


def swiglu(h, w_in, w_out):
    gate, up = jnp.split(h @ w_in, 2, axis=-1)
    return (jax.nn.silu(gate) * up) @ w_out


def centred_mean(x, window):
    s = x.shape[1]
    cs = lax.cumsum(x.astype(jnp.float32), axis=1)
    cs = jnp.pad(cs, ((0, 0), (1, 0), (0, 0)))
    t = jnp.arange(s)
    hi = jnp.clip(t + window // 2, 0, s)
    lo = jnp.clip(t - window // 2, 0, s)
    tot = jnp.take(cs, hi, axis=1) - jnp.take(cs, lo, axis=1)
    cnt = (hi - lo).astype(jnp.float32)[None, :, None]
    return (tot / cnt).astype(x.dtype)


def pool_mixer(h, w, b, scale):
    B, S, _ = h.shape
    hg = h.reshape(B, S, N_POOL_GROUPS, POOL_GROUP)
    pooled = jnp.stack([centred_mean(hg[:, :, g], POOL_WINDOWS[g]) for g in range(N_POOL_GROUPS)], axis=2)
    y = jnp.einsum('bsgc,gcd->bsgd', pooled - hg, w) + b
    return y.reshape(B, S, D_MODEL) * scale


def rope_tables(s, dtype):
    inv = 1.0 / (ROPE_THETA ** (jnp.arange(0, QK_ROPE, 2, dtype=jnp.float32) / QK_ROPE))
    ang = jnp.arange(s, dtype=jnp.float32)[:, None] * inv[None, :]
    return jnp.cos(ang).astype(dtype), jnp.sin(ang).astype(dtype)


def apply_rope(x, cos, sin):
    x1, x2 = jnp.split(x, 2, axis=-1)
    return jnp.concatenate([x1 * cos - x2 * sin, x2 * cos + x1 * sin], axis=-1)


def mla_mixer(h, w_in, q_norm, kv_norm, w_uq, w_uk, w_uv, w_o, cos, sin):
    B, S, _ = h.shape
    lat = h @ w_in
    c_q, c_kv, k_r = jnp.split(lat, [Q_LORA, Q_LORA + KV_LORA], axis=-1)
    c_q = rmsnorm(c_q, q_norm)
    c_kv = rmsnorm(c_kv, kv_norm)
    q = jnp.einsum('bsc,chd->bshd', c_q, w_uq)
    q_nope, q_rope = q[..., :QK_NOPE], q[..., QK_NOPE:]
    q_rope = apply_rope(q_rope, cos[:, None, :], sin[:, None, :]) * ATTN_SCALE
    k_rope = apply_rope(k_r, cos, sin)
    q_lat = jnp.einsum('bshn,chn->bshc', q_nope, w_uk) * ATTN_SCALE
    nb = S // Q_BLOCK
    qlb = q_lat.reshape(B, nb, Q_BLOCK, N_HEADS, KV_LORA).transpose(1, 0, 2, 3, 4)
    qrb = q_rope.reshape(B, nb, Q_BLOCK, N_HEADS, QK_ROPE).transpose(1, 0, 2, 3, 4)

    def block(args):
        ql, qr = args
        s = (jnp.einsum('bqhc,bkc->bhqk', ql, c_kv)
             + jnp.einsum('bqhr,bkr->bhqk', qr, k_rope))
        p = jax.nn.softmax(s.astype(jnp.float32), axis=-1).astype(c_kv.dtype)
        return jnp.einsum('bhqk,bkc->bqhc', p, c_kv)

    o_lat = lax.map(block, (qlb, qrb))
    o_lat = o_lat.transpose(1, 0, 2, 3, 4).reshape(B, S, N_HEADS, KV_LORA)
    o = jnp.einsum('bshc,chv->bshv', o_lat, w_uv)
    return o.reshape(B, S, N_HEADS * V_HEAD) @ w_o


def modulated_sublayer(x, mod, g_pre, g_post, fn, weight):
    shift, scale, gate = mod[:, 0], mod[:, 1], mod[:, 2]
    h = rmsnorm(x, g_pre) * (1.0 + scale) + shift
    y = rmsnorm(fn(h), g_post)
    return x + weight * (1.0 + gate) * y


def _fwd_setup_inputs(seed: int = 0) -> dict:
    key = jax.random.key(seed)
    ks = jax.random.split(key, 20)
    n = jax.random.normal
    f32 = jnp.float32
    return {
        "x": n(ks[0], (BATCH, SEQ, D_MODEL), f32),
        "c": n(ks[1], (BATCH, D_MODEL), f32),
        "ada_w": n(ks[2], (DEPTH, D_MODEL, N_MOD * D_MODEL), f32) * (0.5 * D_MODEL ** -0.5),
        "ada_b": n(ks[3], (DEPTH, N_MOD * D_MODEL), f32) * 0.01,
        "norm_g": 1.0 + 0.05 * n(ks[4], (DEPTH, 6, D_MODEL), f32),
        "ffn_w_in": n(ks[5], (DEPTH, 2, D_MODEL, 2 * D_FF), f32) * D_MODEL ** -0.5,
        "ffn_w_out": n(ks[6], (DEPTH, 2, D_FF, D_MODEL), f32) * D_FF ** -0.5,
        "pool_w": n(ks[7], (N_POOL_LAYERS, N_POOL_GROUPS, POOL_GROUP, POOL_GROUP), f32) * POOL_GROUP ** -0.5,
        "pool_b": n(ks[8], (N_POOL_LAYERS, N_POOL_GROUPS, POOL_GROUP), f32) * 0.01,
        "pool_scale": 1.0 + 0.05 * n(ks[9], (N_POOL_LAYERS, D_MODEL), f32),
        "mla_w_in": n(ks[10], (N_MLA_LAYERS, D_MODEL, Q_LORA + KV_LORA + QK_ROPE), f32) * D_MODEL ** -0.5,
        "mla_q_norm": 1.0 + 0.05 * n(ks[11], (N_MLA_LAYERS, Q_LORA), f32),
        "mla_kv_norm": 1.0 + 0.05 * n(ks[12], (N_MLA_LAYERS, KV_LORA), f32),
        "mla_w_uq": n(ks[13], (N_MLA_LAYERS, Q_LORA, N_HEADS, QK_NOPE + QK_ROPE), f32) * Q_LORA ** -0.5,
        "mla_w_uk": n(ks[14], (N_MLA_LAYERS, KV_LORA, N_HEADS, QK_NOPE), f32) * KV_LORA ** -0.5,
        "mla_w_uv": n(ks[15], (N_MLA_LAYERS, KV_LORA, N_HEADS, V_HEAD), f32) * KV_LORA ** -0.5,
        "mla_w_o": n(ks[16], (N_MLA_LAYERS, N_HEADS * V_HEAD, D_MODEL), f32) * (N_HEADS * V_HEAD) ** -0.5,
    }


def _fwd_reference(x, c, ada_w, ada_b, norm_g, ffn_w_in, ffn_w_out, pool_w, pool_b, pool_scale,
              mla_w_in, mla_q_norm, mla_kv_norm, mla_w_uq, mla_w_uk, mla_w_uv, mla_w_o):
    B = x.shape[0]
    cos, sin = rope_tables(x.shape[1], x.dtype)
    sc = jax.nn.silu(c)
    for i in range(DEPTH):
        mod = (sc @ ada_w[i] + ada_b[i]).reshape(B, N_MOD, D_MODEL)[:, :, None, :]
        g = norm_g[i]
        x = modulated_sublayer(x, mod[:, 0:3], g[0], g[1],
                               lambda h: swiglu(h, ffn_w_in[i, 0], ffn_w_out[i, 0]), 0.5)
        if i % N_MIXERS == 0:
            li = i // N_MIXERS
            mixer = lambda h: pool_mixer(h, pool_w[li], pool_b[li], pool_scale[li])
        else:
            li = i // N_MIXERS
            mixer = lambda h: mla_mixer(h, mla_w_in[li], mla_q_norm[li], mla_kv_norm[li],
                                        mla_w_uq[li], mla_w_uk[li], mla_w_uv[li], mla_w_o[li],
                                        cos, sin)
        x = modulated_sublayer(x, mod[:, 3:6], g[2], g[3], mixer, 1.0)
        x = modulated_sublayer(x, mod[:, 6:9], g[4], g[5],
                               lambda h: swiglu(h, ffn_w_in[i, 1], ffn_w_out[i, 1]), 0.5)
    return x


import jax as _jax
import jax.numpy as _jnp

TWIN_FORMAT = 'train_step'
FWD_PARAMS = ['x', 'c', 'ada_w', 'ada_b', 'norm_g', 'ffn_w_in', 'ffn_w_out', 'pool_w', 'pool_b', 'pool_scale', 'mla_w_in', 'mla_q_norm', 'mla_kv_norm', 'mla_w_uq', 'mla_w_uk', 'mla_w_uv', 'mla_w_o']
TWIN_WEIGHTS = ['ada_w', 'ada_b', 'norm_g', 'ffn_w_in', 'ffn_w_out', 'pool_w', 'pool_b', 'pool_scale', 'mla_w_in', 'mla_q_norm', 'mla_kv_norm', 'mla_w_uq', 'mla_w_uk', 'mla_w_uv', 'mla_w_o']
TWIN_DIFF_INPUT = 'x'
TWIN_INPUTS = ['x', 'c', 'ada_w', 'ada_b', 'norm_g', 'ffn_w_in', 'ffn_w_out', 'pool_w', 'pool_b', 'pool_scale', 'mla_w_in', 'mla_q_norm', 'mla_kv_norm', 'mla_w_uq', 'mla_w_uk', 'mla_w_uv', 'mla_w_o', 'loss_target', 'm_ada_w', 'm_ada_b', 'm_norm_g', 'm_ffn_w_in', 'm_ffn_w_out', 'm_pool_w', 'm_pool_b', 'm_pool_scale', 'm_mla_w_in', 'm_mla_q_norm', 'm_mla_kv_norm', 'm_mla_w_uq', 'm_mla_w_uk', 'm_mla_w_uv', 'm_mla_w_o', 'v_ada_w', 'v_ada_b', 'v_norm_g', 'v_ffn_w_in', 'v_ffn_w_out', 'v_pool_w', 'v_pool_b', 'v_pool_scale', 'v_mla_w_in', 'v_mla_q_norm', 'v_mla_kv_norm', 'v_mla_w_uq', 'v_mla_w_uk', 'v_mla_w_uv', 'v_mla_w_o']
TWIN_OUTPUTS = ['loss', 'grad_x', 'grad_ada_w', 'grad_ada_b', 'grad_norm_g', 'grad_ffn_w_in', 'grad_ffn_w_out', 'grad_pool_w', 'grad_pool_b', 'grad_pool_scale', 'grad_mla_w_in', 'grad_mla_q_norm', 'grad_mla_kv_norm', 'grad_mla_w_uq', 'grad_mla_w_uk', 'grad_mla_w_uv', 'grad_mla_w_o', 'delta_ada_w', 'delta_ada_b', 'delta_norm_g', 'delta_ffn_w_in', 'delta_ffn_w_out', 'delta_pool_w', 'delta_pool_b', 'delta_pool_scale', 'delta_mla_w_in', 'delta_mla_q_norm', 'delta_mla_kv_norm', 'delta_mla_w_uq', 'delta_mla_w_uk', 'delta_mla_w_uv', 'delta_mla_w_o', 'new_m_ada_w', 'new_m_ada_b', 'new_m_norm_g', 'new_m_ffn_w_in', 'new_m_ffn_w_out', 'new_m_pool_w', 'new_m_pool_b', 'new_m_pool_scale', 'new_m_mla_w_in', 'new_m_mla_q_norm', 'new_m_mla_kv_norm', 'new_m_mla_w_uq', 'new_m_mla_w_uk', 'new_m_mla_w_uv', 'new_m_mla_w_o', 'new_v_ada_w', 'new_v_ada_b', 'new_v_norm_g', 'new_v_ffn_w_in', 'new_v_ffn_w_out', 'new_v_pool_w', 'new_v_pool_b', 'new_v_pool_scale', 'new_v_mla_w_in', 'new_v_mla_q_norm', 'new_v_mla_kv_norm', 'new_v_mla_w_uq', 'new_v_mla_w_uk', 'new_v_mla_w_uv', 'new_v_mla_w_o']
TWIN_LEAF_KINDS = {'loss': 'loss', 'grad_x': 'grad_x', 'grad_ada_w': 'grad_w', 'grad_ada_b': 'grad_w', 'grad_norm_g': 'grad_w', 'grad_ffn_w_in': 'grad_w', 'grad_ffn_w_out': 'grad_w', 'grad_pool_w': 'grad_w', 'grad_pool_b': 'grad_w', 'grad_pool_scale': 'grad_w', 'grad_mla_w_in': 'grad_w', 'grad_mla_q_norm': 'grad_w', 'grad_mla_kv_norm': 'grad_w', 'grad_mla_w_uq': 'grad_w', 'grad_mla_w_uk': 'grad_w', 'grad_mla_w_uv': 'grad_w', 'grad_mla_w_o': 'grad_w', 'delta_ada_w': 'delta_w', 'delta_ada_b': 'delta_w', 'delta_norm_g': 'delta_w', 'delta_ffn_w_in': 'delta_w', 'delta_ffn_w_out': 'delta_w', 'delta_pool_w': 'delta_w', 'delta_pool_b': 'delta_w', 'delta_pool_scale': 'delta_w', 'delta_mla_w_in': 'delta_w', 'delta_mla_q_norm': 'delta_w', 'delta_mla_kv_norm': 'delta_w', 'delta_mla_w_uq': 'delta_w', 'delta_mla_w_uk': 'delta_w', 'delta_mla_w_uv': 'delta_w', 'delta_mla_w_o': 'delta_w', 'new_m_ada_w': 'new_m', 'new_m_ada_b': 'new_m', 'new_m_norm_g': 'new_m', 'new_m_ffn_w_in': 'new_m', 'new_m_ffn_w_out': 'new_m', 'new_m_pool_w': 'new_m', 'new_m_pool_b': 'new_m', 'new_m_pool_scale': 'new_m', 'new_m_mla_w_in': 'new_m', 'new_m_mla_q_norm': 'new_m', 'new_m_mla_kv_norm': 'new_m', 'new_m_mla_w_uq': 'new_m', 'new_m_mla_w_uk': 'new_m', 'new_m_mla_w_uv': 'new_m', 'new_m_mla_w_o': 'new_m', 'new_v_ada_w': 'new_v', 'new_v_ada_b': 'new_v', 'new_v_norm_g': 'new_v', 'new_v_ffn_w_in': 'new_v', 'new_v_ffn_w_out': 'new_v', 'new_v_pool_w': 'new_v', 'new_v_pool_b': 'new_v', 'new_v_pool_scale': 'new_v', 'new_v_mla_w_in': 'new_v', 'new_v_mla_q_norm': 'new_v', 'new_v_mla_kv_norm': 'new_v', 'new_v_mla_w_uq': 'new_v', 'new_v_mla_w_uk': 'new_v', 'new_v_mla_w_uv': 'new_v', 'new_v_mla_w_o': 'new_v'}


def _forward(args):
    return _fwd_reference(*[args[k] for k in FWD_PARAMS])


def _output_shape():
    out = _jax.eval_shape(lambda: _forward(_fwd_setup_inputs(0)))
    return out.shape, out.dtype

N_MICROBATCH = 1
ADAM_LR = 0.001
ADAM_B1 = 0.9
ADAM_B2 = 0.999
ADAM_EPS = 1e-08
ADAM_WD = 0.01
ADAM_STEP = 10
PER_EXAMPLE_BATCH_AXIS = {'x': 0, 'c': 0, 'loss_target': 0}
SHARED_INPUTS = []
_WEIGHT_DTYPES = {'ada_w': _jnp.float32, 'ada_b': _jnp.float32, 'norm_g': _jnp.float32, 'ffn_w_in': _jnp.float32, 'ffn_w_out': _jnp.float32, 'pool_w': _jnp.float32, 'pool_b': _jnp.float32, 'pool_scale': _jnp.float32, 'mla_w_in': _jnp.float32, 'mla_q_norm': _jnp.float32, 'mla_kv_norm': _jnp.float32, 'mla_w_uq': _jnp.float32, 'mla_w_uk': _jnp.float32, 'mla_w_uv': _jnp.float32, 'mla_w_o': _jnp.float32}
MOMENT_SCALE = {'ada_w': 7.594844e+00, 'ada_b': 1.567260e+01, 'norm_g': 1.822600e+01, 'ffn_w_in': 9.612163e-01, 'ffn_w_out': 1.921135e+00, 'pool_w': 1.716628e+00, 'pool_b': 3.311775e+01, 'pool_scale': 1.237265e+01, 'mla_w_in': 1.646176e+01, 'mla_q_norm': 1.257268e+00, 'mla_kv_norm': 3.325036e+01, 'mla_w_uq': 5.655244e-01, 'mla_w_uk': 7.463165e-01, 'mla_w_uv': 1.021475e+01, 'mla_w_o': 1.042553e+01}


def _to_microbatches(a, axis):
    t = _jnp.moveaxis(a, axis, 0)
    t = t.reshape((N_MICROBATCH, t.shape[0] // N_MICROBATCH) + t.shape[1:])
    return _jnp.moveaxis(t, 1, axis + 1)


def setup_inputs(seed: int = 0) -> dict:
    inp = _fwd_setup_inputs(seed)
    key = _jax.random.fold_in(_jax.random.key(seed), 7919)
    shape, _ = _output_shape()
    out = dict(inp)
    out["loss_target"] = _jax.random.normal(_jax.random.fold_in(key, 0), shape, _jnp.float32)
    for i, name in enumerate(TWIN_WEIGHTS):
        w = inp[name].astype(_jnp.float32)
        if MOMENT_SCALE is None:
            s = _jnp.sqrt(_jnp.mean(_jnp.square(w)) + 1e-30)
        else:
            s = MOMENT_SCALE[name]
        km, kv = _jax.random.split(_jax.random.fold_in(key, i + 1))
        out[name] = w
        out["m_" + name] = s * _jax.random.normal(km, w.shape, _jnp.float32)
        out["v_" + name] = (s * s) * _jax.random.uniform(kv, w.shape, _jnp.float32, 0.5, 1.5)
    if N_MICROBATCH > 1:
        for name, axis in PER_EXAMPLE_BATCH_AXIS.items():
            out[name] = _to_microbatches(out[name], axis)
    return {'x': out['x'], 'c': out['c'], 'ada_w': out['ada_w'], 'ada_b': out['ada_b'], 'norm_g': out['norm_g'], 'ffn_w_in': out['ffn_w_in'], 'ffn_w_out': out['ffn_w_out'], 'pool_w': out['pool_w'], 'pool_b': out['pool_b'], 'pool_scale': out['pool_scale'], 'mla_w_in': out['mla_w_in'], 'mla_q_norm': out['mla_q_norm'], 'mla_kv_norm': out['mla_kv_norm'], 'mla_w_uq': out['mla_w_uq'], 'mla_w_uk': out['mla_w_uk'], 'mla_w_uv': out['mla_w_uv'], 'mla_w_o': out['mla_w_o'], 'loss_target': out['loss_target'], 'm_ada_w': out['m_ada_w'], 'm_ada_b': out['m_ada_b'], 'm_norm_g': out['m_norm_g'], 'm_ffn_w_in': out['m_ffn_w_in'], 'm_ffn_w_out': out['m_ffn_w_out'], 'm_pool_w': out['m_pool_w'], 'm_pool_b': out['m_pool_b'], 'm_pool_scale': out['m_pool_scale'], 'm_mla_w_in': out['m_mla_w_in'], 'm_mla_q_norm': out['m_mla_q_norm'], 'm_mla_kv_norm': out['m_mla_kv_norm'], 'm_mla_w_uq': out['m_mla_w_uq'], 'm_mla_w_uk': out['m_mla_w_uk'], 'm_mla_w_uv': out['m_mla_w_uv'], 'm_mla_w_o': out['m_mla_w_o'], 'v_ada_w': out['v_ada_w'], 'v_ada_b': out['v_ada_b'], 'v_norm_g': out['v_norm_g'], 'v_ffn_w_in': out['v_ffn_w_in'], 'v_ffn_w_out': out['v_ffn_w_out'], 'v_pool_w': out['v_pool_w'], 'v_pool_b': out['v_pool_b'], 'v_pool_scale': out['v_pool_scale'], 'v_mla_w_in': out['v_mla_w_in'], 'v_mla_q_norm': out['v_mla_q_norm'], 'v_mla_kv_norm': out['v_mla_kv_norm'], 'v_mla_w_uq': out['v_mla_w_uq'], 'v_mla_w_uk': out['v_mla_w_uk'], 'v_mla_w_uv': out['v_mla_w_uv'], 'v_mla_w_o': out['v_mla_w_o']}


def _loss(weights, diff, rest, loss_target):
    with _jax.named_scope("forward"):
        args = {**rest, TWIN_DIFF_INPUT: diff, **{k: w.astype(_WEIGHT_DTYPES[k]) for k, w in weights.items()}}
        y = _forward(args)
    with _jax.named_scope("loss_head"):
        err = _jnp.square(y.astype(_jnp.float32) - loss_target)
        return 0.5 * _jnp.sum(_jnp.mean(err, axis=-1)) if err.ndim else 0.5 * err


def _adamw(w, g, m, v):
    m = ADAM_B1 * m + (1.0 - ADAM_B1) * g
    v = ADAM_B2 * v + (1.0 - ADAM_B2) * _jnp.square(g)
    m_hat = m / (1.0 - ADAM_B1 ** ADAM_STEP)
    v_hat = v / (1.0 - ADAM_B2 ** ADAM_STEP)
    delta = -ADAM_LR * (m_hat / (_jnp.sqrt(v_hat) + ADAM_EPS) + ADAM_WD * w)
    return delta, m, v


def reference(x, c, ada_w, ada_b, norm_g, ffn_w_in, ffn_w_out, pool_w, pool_b, pool_scale, mla_w_in, mla_q_norm, mla_kv_norm, mla_w_uq, mla_w_uk, mla_w_uv, mla_w_o, loss_target, m_ada_w, m_ada_b, m_norm_g, m_ffn_w_in, m_ffn_w_out, m_pool_w, m_pool_b, m_pool_scale, m_mla_w_in, m_mla_q_norm, m_mla_kv_norm, m_mla_w_uq, m_mla_w_uk, m_mla_w_uv, m_mla_w_o, v_ada_w, v_ada_b, v_norm_g, v_ffn_w_in, v_ffn_w_out, v_pool_w, v_pool_b, v_pool_scale, v_mla_w_in, v_mla_q_norm, v_mla_kv_norm, v_mla_w_uq, v_mla_w_uk, v_mla_w_uv, v_mla_w_o):
    given = dict(x=x, c=c, ada_w=ada_w, ada_b=ada_b, norm_g=norm_g, ffn_w_in=ffn_w_in, ffn_w_out=ffn_w_out, pool_w=pool_w, pool_b=pool_b, pool_scale=pool_scale, mla_w_in=mla_w_in, mla_q_norm=mla_q_norm, mla_kv_norm=mla_kv_norm, mla_w_uq=mla_w_uq, mla_w_uk=mla_w_uk, mla_w_uv=mla_w_uv, mla_w_o=mla_w_o, loss_target=loss_target, m_ada_w=m_ada_w, m_ada_b=m_ada_b, m_norm_g=m_norm_g, m_ffn_w_in=m_ffn_w_in, m_ffn_w_out=m_ffn_w_out, m_pool_w=m_pool_w, m_pool_b=m_pool_b, m_pool_scale=m_pool_scale, m_mla_w_in=m_mla_w_in, m_mla_q_norm=m_mla_q_norm, m_mla_kv_norm=m_mla_kv_norm, m_mla_w_uq=m_mla_w_uq, m_mla_w_uk=m_mla_w_uk, m_mla_w_uv=m_mla_w_uv, m_mla_w_o=m_mla_w_o, v_ada_w=v_ada_w, v_ada_b=v_ada_b, v_norm_g=v_norm_g, v_ffn_w_in=v_ffn_w_in, v_ffn_w_out=v_ffn_w_out, v_pool_w=v_pool_w, v_pool_b=v_pool_b, v_pool_scale=v_pool_scale, v_mla_w_in=v_mla_w_in, v_mla_q_norm=v_mla_q_norm, v_mla_kv_norm=v_mla_kv_norm, v_mla_w_uq=v_mla_w_uq, v_mla_w_uk=v_mla_w_uk, v_mla_w_uv=v_mla_w_uv, v_mla_w_o=v_mla_w_o)
    weights = {n: given[n] for n in TWIN_WEIGHTS}
    shared = {n: given[n] for n in SHARED_INPUTS}
    per_example = {n: given[n] for n in ['x', 'c']}
    grad_fn = _jax.value_and_grad(_loss, argnums=(0, 1))

    def one_microbatch(ex, loss_target):
        ex = dict(ex)
        diff = ex.pop(TWIN_DIFF_INPUT)
        return grad_fn(weights, diff, {**shared, **ex}, loss_target)

    if N_MICROBATCH == 1:
        loss, (grad_w, grad_x) = one_microbatch(per_example, given["loss_target"])
    else:
        def body(carry, xs):
            loss_sum, grad_sum = carry
            l_k, (gw_k, gx_k) = one_microbatch(xs[0], xs[1])
            with _jax.named_scope("update"):
                return (loss_sum + l_k, _jax.tree.map(_jnp.add, grad_sum, gw_k)), gx_k

        init = (_jnp.zeros((), _jnp.float32), _jax.tree.map(_jnp.zeros_like, weights))
        (loss, grad_w), grad_x = _jax.lax.scan(body, init, (per_example, given["loss_target"]))
    with _jax.named_scope("update"):
        delta_w, new_m, new_v = {}, {}, {}
        for n in TWIN_WEIGHTS:
            delta_w[n], new_m[n], new_v[n] = _adamw(weights[n], grad_w[n], given["m_" + n], given["v_" + n])
    return (loss, grad_x, *[grad_w[n] for n in TWIN_WEIGHTS], *[delta_w[n] for n in TWIN_WEIGHTS],
            *[new_m[n] for n in TWIN_WEIGHTS], *[new_v[n] for n in TWIN_WEIGHTS])
```

```python
import functools

import jax
import jax.numpy as jnp
from jax import lax
from jax.experimental import pallas as pl
from jax.experimental.pallas import tpu as pltpu

F32 = jnp.float32
BF16 = jnp.bfloat16
N_DEV = 8
AXES = ("x", "y", "c")
MESH = pl.DeviceIdType.MESH

D_MODEL = 1024
N_HEADS = 16
QK_NOPE = 64
QK_ROPE = 32
V_HEAD = 64
Q_LORA = 256
KV_LORA = 128
LAT_PAD = 512
QK_PAD = 256
D_FF = 2816
FF_BLK = 2 * D_FF // N_DEV
POOL_WINDOWS = (2, 4, 8, 16)
POOL_GROUP = 256
ROPE_THETA = 10000.0
EPS = 1e-6
ATTN_SCALE = (QK_NOPE + QK_ROPE) ** -0.5
ADAM_LR, ADAM_B1, ADAM_B2, ADAM_EPS, ADAM_WD, ADAM_STEP = 0.001, 0.9, 0.999, 1e-08, 0.01, 10
VMEM_LIMIT = 56 * 1024 * 1024

NN = ((1,), (0,))
NT = ((1,), (1,))
TN = ((0,), (0,))


def _params(**kw):
    return pltpu.CompilerParams(vmem_limit_bytes=VMEM_LIMIT, **kw)


def _dot(a, b, contract):
    return lax.dot_general(a, b, (contract, ((), ())), preferred_element_type=F32)


def _matmul(name, a, b, *, a_blk, a_map, b_blk, b_map, o_shape, o_blk, o_map, grid, contract, out_dtype,
            bias=None, bias_blk=None, bias_map=None):
    n_k = grid[-1]
    k_axis = len(grid) - 1
    acc_shape = tuple(d for d in o_blk if d is not None)

    def body(*refs):
        if bias is None:
            a_ref, b_ref, o_ref, acc = refs
        else:
            a_ref, b_ref, bias_ref, o_ref, acc = refs
        k = pl.program_id(k_axis)

        @pl.when(k == 0)
        def _():
            acc[...] = jnp.zeros_like(acc)

        acc[...] += _dot(a_ref[...].astype(BF16), b_ref[...].astype(BF16), contract)

        @pl.when(k == n_k - 1)
        def _():
            r = acc[...]
            if bias is not None:
                r = r + bias_ref[...]
            o_ref[...] = r.astype(o_ref.dtype)

    in_specs = [pl.BlockSpec(a_blk, a_map), pl.BlockSpec(b_blk, b_map)]
    args = [a, b]
    if bias is not None:
        in_specs.append(pl.BlockSpec(bias_blk, bias_map))
        args.append(bias)
    return pl.pallas_call(
        body, name=name, grid=grid, in_specs=in_specs, out_specs=pl.BlockSpec(o_blk, o_map),
        out_shape=jax.ShapeDtypeStruct(o_shape, out_dtype), scratch_shapes=[pltpu.VMEM(acc_shape, F32)],
        compiler_params=_params(),
    )(*args)


def _rowmap(name, fn, ins, outs, reds, grid):
    n_in, n_out, n_red = len(ins), len(outs), len(reds)

    def body(*refs):
        in_refs = refs[:n_in]
        out_refs = refs[n_in:n_in + n_out]
        red_refs = refs[n_in + n_out:]
        out_vals, red_vals = fn(*[r[...] for r in in_refs])
        for r, v in zip(out_refs, out_vals):
            r[...] = v.astype(r.dtype)
        if n_red:
            first = pl.program_id(0) == 0
            for ax in range(1, len(grid)):
                first = jnp.logical_and(first, pl.program_id(ax) == 0)

            @pl.when(first)
            def _():
                for r in red_refs:
                    r[...] = jnp.zeros_like(r)

            for r, v in zip(red_refs, red_vals):
                r[...] += v

    res = pl.pallas_call(
        body, name=name, grid=grid,
        in_specs=[pl.BlockSpec(blk, imap) for _, blk, imap in ins],
        out_specs=[pl.BlockSpec(blk, imap) for _, blk, imap in list(outs) + list(reds)],
        out_shape=[sds for sds, _, _ in list(outs) + list(reds)],
        compiler_params=_params(),
    )(*[a for a, _, _ in ins])
    return res[:n_out], res[n_out:]


def _sds(shape, dtype):
    return jax.ShapeDtypeStruct(shape, dtype)


def _tile(a, tm):
    return (a, (tm, a.shape[1]), lambda i: (i, 0))


def _row(a):
    return (a, (1, a.shape[1]), lambda i: (0, 0))


def _otile(n, c, dtype, tm):
    return (_sds((n, c), dtype), (tm, c), lambda i: (i, 0))


def _ored(c):
    return (_sds((1, c), F32), (1, c), lambda i: (0, 0))


def _colsum(v):
    return jnp.sum(v, axis=0, keepdims=True)


def _rstd(v):
    return lax.rsqrt(jnp.mean(v * v, axis=-1, keepdims=True) + EPS)


def _prenorm(name, x, g_pre, scale, shift, out_dtype, tm):
    n, d = x.shape

    def fn(xv, g, sc, sh):
        return (xv * _rstd(xv) * g * (1.0 + sc) + sh,), ()

    (h,), _ = _rowmap(name, fn, [_tile(x, tm), _row(g_pre), _row(scale), _row(shift)], [_otile(n, d, out_dtype, tm)], [],
                      (n // tm,))
    return h


def _postnorm(name, x, u, g_post, gate, weight, tm):
    n, d = x.shape

    def fn(xv, uv, g, gt):
        return (xv + weight * (1.0 + gt) * (uv * _rstd(uv) * g),), ()

    (y,), _ = _rowmap(name, fn, [_tile(x, tm), _tile(u, tm), _row(g_post), _row(gate)], [_otile(n, d, F32, tm)], [], (n // tm,))
    return y


def _postnorm_bwd(name, dout, u, g_post, gate, weight, out_dtype, tm):
    n, d = u.shape

    def fn(dv, uv, g, gt):
        r = _rstd(uv)
        un = uv * r
        dy = dv * (weight * (1.0 + gt))
        a = dy * g
        du = r * (a - un * jnp.mean(a * un, axis=-1, keepdims=True))
        return (du,), (_colsum(dv * (weight * (un * g))), _colsum(dy * un))

    (du,), reds = _rowmap(name, fn, [_tile(dout, tm), _tile(u, tm), _row(g_post), _row(gate)], [_otile(n, d, out_dtype, tm)],
                          [_ored(d), _ored(d)], (n // tm,))
    return du, reds


def _prenorm_bwd(name, dh, x, dout, g_pre, scale, tm):
    n, d = x.shape

    def fn(dhv, xv, dv, g, sc):
        dhv = dhv.astype(F32)
        r = _rstd(xv)
        xn = xv * r
        b = dhv * (g * (1.0 + sc))
        dx = dv + r * (b - xn * jnp.mean(b * xn, axis=-1, keepdims=True))
        return (dx,), (_colsum(dhv), _colsum(dhv * (xn * g)), _colsum(dhv * ((1.0 + sc) * xn)))

    (dx,), reds = _rowmap(name, fn, [_tile(dh, tm), _tile(x, tm), _tile(dout, tm), _row(g_pre), _row(scale)],
                          [_otile(n, d, F32, tm)], [_ored(d), _ored(d), _ored(d)], (n // tm,))
    return dx, reds


def _ffn_fwd(tag, h, w_in8, w_out4, tm):
    s, d = h.shape
    nt = s // tm
    gu = _matmul(f"ffn_up_{tag}", h, w_in8, a_blk=(tm, d), a_map=lambda g, i, k: (i, 0), b_blk=(None, d, FF_BLK),
                 b_map=lambda g, i, k: (g, 0, 0), o_shape=(8, s, FF_BLK), o_blk=(None, tm, FF_BLK), o_map=lambda g, i, k: (g, i, 0),
                 grid=(8, nt, 1), contract=NN, out_dtype=BF16)

    def act_fn(gv, uv):
        gv = gv.astype(F32)
        return (gv * jax.nn.sigmoid(gv) * uv.astype(F32),), ()

    (act,), _ = _rowmap(f"ffn_act_{tag}", act_fn,
                        [(gu, (None, tm, FF_BLK), lambda j, i: (j, i, 0)), (gu, (None, tm, FF_BLK), lambda j, i: (j + 4, i, 0))],
                        [(_sds((4, s, FF_BLK), BF16), (None, tm, FF_BLK), lambda j, i: (j, i, 0))], [], (4, nt))
    u = _matmul(f"ffn_down_{tag}", act, w_out4, a_blk=(None, tm, FF_BLK), a_map=lambda i, k: (k, i, 0), b_blk=(None, FF_BLK, d),
                b_map=lambda i, k: (k, 0, 0), o_shape=(s, d), o_blk=(tm, d), o_map=lambda i, k: (i, 0), grid=(nt, 4),
                contract=NN, out_dtype=F32)
    return u, gu


def _ffn_bwd(tag, du, h, gu, w_in8, w_out4, tm):
    s, d = h.shape
    nt = s // tm
    dact = _matmul(f"ffn_dact_{tag}", du, w_out4, a_blk=(tm, d), a_map=lambda j, i, k: (i, 0), b_blk=(None, FF_BLK, d),
                   b_map=lambda j, i, k: (j, 0, 0), o_shape=(4, s, FF_BLK), o_blk=(None, tm, FF_BLK), o_map=lambda j, i, k: (j, i, 0),
                   grid=(4, nt, 1), contract=NT, out_dtype=BF16)

    def bwd_fn(gv, uv, dav):
        gv, uv, dav = gv.astype(F32), uv.astype(F32), dav.astype(F32)
        sg = jax.nn.sigmoid(gv)
        silu = gv * sg
        return (jnp.stack([dav * uv * (sg * (1.0 + gv * (1.0 - sg))), dav * silu]), silu * uv), ()

    blk = (None, tm, FF_BLK)
    (dgu, act), _ = _rowmap(
        f"ffn_dgu_{tag}", bwd_fn,
        [(gu, blk, lambda j, i: (j, i, 0)), (gu, blk, lambda j, i: (j + 4, i, 0)), (dact, blk, lambda j, i: (j, i, 0))],
        [(_sds((2, 4, s, FF_BLK), BF16), (2, None, tm, FF_BLK), lambda j, i: (0, j, i, 0)),
         (_sds((4, s, FF_BLK), BF16), blk, lambda j, i: (j, i, 0))], [], (4, nt))
    dgu = dgu.reshape(8, s, FF_BLK)
    dh = _matmul(f"ffn_dh_{tag}", dgu, w_in8, a_blk=(None, tm, FF_BLK), a_map=lambda i, k: (k, i, 0), b_blk=(None, d, FF_BLK),
                 b_map=lambda i, k: (k, 0, 0), o_shape=(s, d), o_blk=(tm, d), o_map=lambda i, k: (i, 0), grid=(nt, 8),
                 contract=NT, out_dtype=F32)
    dw_in8 = _matmul(f"ffn_dwin_{tag}", h, dgu, a_blk=(tm, d), a_map=lambda g, k: (k, 0), b_blk=(None, tm, FF_BLK),
                     b_map=lambda g, k: (g, k, 0), o_shape=(8, d, FF_BLK), o_blk=(None, d, FF_BLK), o_map=lambda g, k: (g, 0, 0),
                     grid=(8, nt), contract=TN, out_dtype=BF16)
    dw_out4 = _matmul(f"ffn_dwout_{tag}", act, du, a_blk=(None, tm, FF_BLK), a_map=lambda j, k: (j, k, 0), b_blk=(tm, d),
                      b_map=lambda j, k: (k, 0), o_shape=(4, FF_BLK, d), o_blk=(None, FF_BLK, d), o_map=lambda j, k: (j, 0, 0),
                      grid=(4, nt), contract=TN, out_dtype=BF16)
    return dh, dw_in8, dw_out4


def _window_sum(x, window, transpose):
    s = x.shape[0]
    t = lax.broadcasted_iota(jnp.int32, (s, 1), 0)
    half = window // 2
    cnt = jnp.minimum(t + half, s) - jnp.maximum(t - half, 0)
    inv = 1.0 / cnt.astype(F32)
    if transpose:
        x = x * inv
        offsets = range(-half + 1, half + 1)
    else:
        offsets = range(-half, half)
    acc = jnp.zeros_like(x)
    for o in offsets:
        shifted = x if o == 0 else pltpu.roll(x, (-o) % s, 0)
        valid = jnp.logical_and(t + o >= 0, t + o < s)
        acc = acc + jnp.where(valid, shifted, 0.0)
    return acc if transpose else acc * inv


def _pool_mix(name, x, transpose, out_dtype):
    s, d = x.shape

    def body(x_ref, o_ref):
        g = pl.program_id(0)
        for gi, window in enumerate(POOL_WINDOWS):
            @pl.when(g == gi)
            def _(window=window):
                xv = x_ref[...].astype(F32)
                o_ref[...] = (_window_sum(xv, window, transpose) - xv).astype(o_ref.dtype)

    return pl.pallas_call(
        body, name=name, grid=(len(POOL_WINDOWS),), in_specs=[pl.BlockSpec((s, POOL_GROUP), lambda g: (0, g))],
        out_specs=pl.BlockSpec((s, POOL_GROUP), lambda g: (0, g)), out_shape=_sds((s, d), out_dtype), compiler_params=_params(),
    )(x)


def _pool_fwd(h, w4, bias, pscale, tm):
    s, d = h.shape
    nt = s // tm
    z = _pool_mix("pool_mix", h, False, BF16)
    v = _matmul("pool_proj", z, w4, a_blk=(tm, POOL_GROUP), a_map=lambda i, g, k: (i, g), b_blk=(None, POOL_GROUP, POOL_GROUP),
                b_map=lambda i, g, k: (g, 0, 0), o_shape=(s, d), o_blk=(tm, POOL_GROUP), o_map=lambda i, g, k: (i, g),
                grid=(nt, 4, 1), contract=NN, out_dtype=F32, bias=bias, bias_blk=(1, POOL_GROUP), bias_map=lambda i, g, k: (0, g))
    (u,), _ = _rowmap("pool_scale", lambda vv, ps: ((vv * ps,), ()), [_tile(v, tm), _row(pscale)], [_otile(s, d, F32, tm)], [],
                      (nt,))
    return u, z, v


def _pool_bwd(du, z, v, w4, pscale, tm):
    s, d = du.shape
    nt = s // tm

    def fn(duv, vv, ps):
        dv = duv * ps
        return (dv,), (_colsum(duv * vv), _colsum(dv))

    (dv,), reds = _rowmap("pool_dscale", fn, [_tile(du, tm), _tile(v, tm), _row(pscale)], [_otile(s, d, BF16, tm)],
                          [_ored(d), _ored(d)], (nt,))
    dw4 = _matmul("pool_dw", z, dv, a_blk=(tm, POOL_GROUP), a_map=lambda g, k: (k, g), b_blk=(tm, POOL_GROUP),
                  b_map=lambda g, k: (k, g), o_shape=(4, POOL_GROUP, POOL_GROUP), o_blk=(None, POOL_GROUP, POOL_GROUP),
                  o_map=lambda g, k: (g, 0, 0), grid=(4, nt), contract=TN, out_dtype=F32)
    dz = _matmul("pool_dz", dv, w4, a_blk=(tm, POOL_GROUP), a_map=lambda i, g, k: (i, g), b_blk=(None, POOL_GROUP, POOL_GROUP),
                 b_map=lambda i, g, k: (g, 0, 0), o_shape=(s, d), o_blk=(tm, POOL_GROUP), o_map=lambda i, g, k: (i, g),
                 grid=(nt, 4, 1), contract=NT, out_dtype=F32)
    dh = _pool_mix("pool_mix_t", dz, True, F32)
    return dh, dw4, reds


def _lane(shape):
    return lax.broadcasted_iota(jnp.int32, shape, 1)


def _rope_swap(v, transpose):
    half = QK_ROPE // 2
    lane = _lane(v.shape)
    up = pltpu.roll(v, v.shape[1] - half, 1)
    down = pltpu.roll(v, half, 1)
    if transpose:
        return jnp.where(lane < half, up, jnp.where(lane < QK_ROPE, -down, 0.0))
    return jnp.where(lane < half, -up, jnp.where(lane < QK_ROPE, down, 0.0))


def _rope(v, cos, sin):
    return v * cos + _rope_swap(v, False) * sin


def _rope_t(g, cos, sin):
    return g * cos + _rope_swap(g * sin, True)


def _mla_mid(lat, q_norm, kv_norm, cos_k, sin_k, tm):
    s = lat.shape[0]

    def fn(lv, qn, kn, cs, sn):
        cq = lv[:, :Q_LORA]
        ckv = lv[:, Q_LORA:Q_LORA + KV_LORA]
        kr = lv[:, Q_LORA + KV_LORA:]
        cq = cq * _rstd(cq) * qn
        ckv = ckv * _rstd(ckv) * kn
        return (cq, jnp.concatenate([ckv, _rope(kr, cs, sn)], axis=1)), ()

    (cq, kcat), _ = _rowmap("mla_mid", fn, [_tile(lat, tm), _row(q_norm), _row(kv_norm), _tile(cos_k, tm), _tile(sin_k, tm)],
                            [_otile(s, Q_LORA, BF16, tm), _otile(s, QK_PAD, BF16, tm)], [], (s // tm,))
    return cq, kcat


def _mla_mid_bwd(lat, dcq, dkcat, dv, q_norm, kv_norm, cos_k, sin_k, tm):
    s = lat.shape[0]

    def fn(lv, dq, dk, dvv, qn, kn, cs, sn):
        cq = lv[:, :Q_LORA]
        ckv = lv[:, Q_LORA:Q_LORA + KV_LORA]
        rq, rk = _rstd(cq), _rstd(ckv)
        cqn, ckn = cq * rq, ckv * rk
        a = dq * qn
        d_cq = rq * (a - cqn * jnp.mean(a * cqn, axis=-1, keepdims=True))
        dckv = dk[:, :KV_LORA] + dvv
        a2 = dckv * kn
        d_ckv = rk * (a2 - ckn * jnp.mean(a2 * ckn, axis=-1, keepdims=True))
        d_kr = _rope_t(dk[:, KV_LORA:], cs, sn)
        return (jnp.concatenate([d_cq, d_ckv, d_kr], axis=1),), (_colsum(dq * cqn), _colsum(dckv * ckn))

    (dlat,), reds = _rowmap(
        "mla_mid_bwd", fn,
        [_tile(lat, tm), _tile(dcq, tm), _tile(dkcat, tm), _tile(dv, tm), _row(q_norm), _row(kv_norm), _tile(cos_k, tm),
         _tile(sin_k, tm)],
        [_otile(s, LAT_PAD, BF16, tm)], [_ored(Q_LORA), _ored(KV_LORA)], (s // tm,))
    return dlat, reds


def _mla_q(cq, wq, wukp, cos_k, sin_k, tm):
    s = cq.shape[0]

    def body(cq_ref, wq_ref, wuk_ref, cos_ref, sin_ref, o_ref):
        aq = _dot(cq_ref[...], wq_ref[...], NN)
        qlat = _dot(aq.astype(BF16), wuk_ref[...], NN)
        roped = _rope(aq[:, KV_LORA:], cos_ref[...], sin_ref[...])
        o_ref[...] = (jnp.concatenate([qlat[:, :KV_LORA], roped], axis=1) * ATTN_SCALE).astype(o_ref.dtype)

    wblk = pl.BlockSpec((None, QK_PAD, QK_PAD), lambda h, i: (h, 0, 0))
    tblk = pl.BlockSpec((tm, KV_LORA), lambda h, i: (i, 0))
    return pl.pallas_call(
        body, name="mla_q", grid=(N_HEADS, s // tm),
        in_specs=[pl.BlockSpec((tm, Q_LORA), lambda h, i: (i, 0)), wblk, wblk, tblk, tblk],
        out_specs=pl.BlockSpec((None, tm, QK_PAD), lambda h, i: (h, i, 0)), out_shape=_sds((N_HEADS, s, QK_PAD), BF16),
        compiler_params=_params(),
    )(cq, wq, wukp, cos_k, sin_k)


def _mla_q_bwd(cq, wq, wukp, cos_k, sin_k, dqcat, tm):
    s = cq.shape[0]

    def body(cq_ref, wq_ref, wuk_ref, cos_ref, sin_ref, dq_ref, dcq_ref, dwq_ref, dwuk_ref):
        h, i = pl.program_id(0), pl.program_id(1)
        cqv = cq_ref[...]
        aq = _dot(cqv, wq_ref[...], NN).astype(BF16)
        g = dq_ref[...].astype(F32) * ATTN_SCALE
        gl, gr = g[:, :KV_LORA], g[:, KV_LORA:]
        dqlat = jnp.concatenate([gl, jnp.zeros_like(gl)], axis=1).astype(BF16)
        d_rope = _rope_t(gr, cos_ref[...], sin_ref[...])
        daq = _dot(dqlat, wuk_ref[...], NT) + jnp.concatenate([jnp.zeros_like(d_rope), d_rope], axis=1)
        daq_b = daq.astype(BF16)
        d_wuk = _dot(aq, dqlat, TN)
        d_wq = _dot(cqv, daq_b, TN)
        d_cq = _dot(daq_b, wq_ref[...], NT)
        rows = pl.ds(pl.multiple_of(i * tm, tm), tm)

        @pl.when(i == 0)
        def _():
            dwq_ref[...] = d_wq
            dwuk_ref[...] = d_wuk

        @pl.when(i != 0)
        def _():
            dwq_ref[...] += d_wq
            dwuk_ref[...] += d_wuk

        @pl.when(h == 0)
        def _():
            dcq_ref[rows, :] = d_cq

        @pl.when(h != 0)
        def _():
            dcq_ref[rows, :] += d_cq

    wblk = pl.BlockSpec((None, QK_PAD, QK_PAD), lambda h, i: (h, 0, 0))
    tblk = pl.BlockSpec((tm, KV_LORA), lambda h, i: (i, 0))
    return pl.pallas_call(
        body, name="mla_q_bwd", grid=(N_HEADS, s // tm),
        in_specs=[pl.BlockSpec((tm, Q_LORA), lambda h, i: (i, 0)), wblk, wblk, tblk, tblk,
                  pl.BlockSpec((None, tm, QK_PAD), lambda h, i: (h, i, 0))],
        out_specs=[pl.BlockSpec((s, Q_LORA), lambda h, i: (0, 0)), wblk, wblk],
        out_shape=[_sds((s, Q_LORA), F32), _sds((N_HEADS, QK_PAD, QK_PAD), F32), _sds((N_HEADS, QK_PAD, QK_PAD), F32)],
        compiler_params=_params(),
    )(cq, wq, wukp, cos_k, sin_k, dqcat)


def _flash_fwd(qcat, kcat, tq, tk):
    n_h, s, _ = qcat.shape
    n_k = s // tk

    def body(q_ref, k_ref, v_ref, o_ref, lse_ref):
        q = q_ref[...]

        def step(kk, carry):
            m, l, acc = carry
            rows = pl.ds(pl.multiple_of(kk * tk, tk), tk)
            sc = _dot(q, k_ref[rows, :], NT)
            m_new = jnp.maximum(m, jnp.max(sc, axis=1, keepdims=True))
            alpha = jnp.exp(m - m_new)
            p = jnp.exp(sc - m_new)
            l = alpha * l + jnp.sum(p, axis=1, keepdims=True)
            acc = alpha * acc + _dot(p.astype(BF16), v_ref[rows, :], NN)
            return m_new, l, acc

        init = (jnp.full((tq, 1), -1e30, F32), jnp.zeros((tq, 1), F32), jnp.zeros((tq, KV_LORA), F32))
        m, l, acc = lax.fori_loop(0, n_k, step, init)
        o_ref[...] = (acc / l).astype(o_ref.dtype)
        lse_ref[...] = m + jnp.log(l)

    return pl.pallas_call(
        body, name="mla_attn", grid=(n_h, s // tq),
        in_specs=[pl.BlockSpec((None, tq, QK_PAD), lambda h, i: (h, i, 0)), pl.BlockSpec((s, QK_PAD), lambda h, i: (0, 0)),
                  pl.BlockSpec((s, KV_LORA), lambda h, i: (0, 0))],
        out_specs=[pl.BlockSpec((None, tq, KV_LORA), lambda h, i: (h, i, 0)), pl.BlockSpec((None, tq, 1), lambda h, i: (h, i, 0))],
        out_shape=[_sds((n_h, s, KV_LORA), BF16), _sds((n_h, s, 1), F32)], compiler_params=_params(),
    )(qcat, kcat, kcat)


def _flash_bwd(qcat, kcat, o, do, lse, tq, tk):
    n_h, s, _ = qcat.shape
    n_k = s // tk

    def body(q_ref, k_ref, v_ref, o_ref, do_ref, lse_ref, dq_ref, dk_ref, dv_ref, dq_acc):
        h, i = pl.program_id(0), pl.program_id(1)

        @pl.when(jnp.logical_and(h == 0, i == 0))
        def _():
            dk_ref[...] = jnp.zeros_like(dk_ref)
            dv_ref[...] = jnp.zeros_like(dv_ref)

        q = q_ref[...]
        dov = do_ref[...]
        lse_v = lse_ref[...]
        delta = jnp.sum(dov.astype(F32) * o_ref[...].astype(F32), axis=1, keepdims=True)
        dq_acc[...] = jnp.zeros_like(dq_acc)

        def step(kk, carry):
            rows = pl.ds(pl.multiple_of(kk * tk, tk), tk)
            k = k_ref[rows, :]
            p = jnp.exp(_dot(q, k, NT) - lse_v)
            dp = _dot(dov, v_ref[rows, :], NT)
            ds = (p * (dp - delta)).astype(BF16)
            dq_acc[...] += _dot(ds, k, NN)
            dv_ref[rows, :] += _dot(p.astype(BF16), dov, TN)
            dk_ref[rows, :] += _dot(ds, q, TN)
            return carry

        lax.fori_loop(0, n_k, step, 0)
        dq_ref[...] = dq_acc[...].astype(dq_ref.dtype)

    qblk = pl.BlockSpec((None, tq, QK_PAD), lambda h, i: (h, i, 0))
    oblk = pl.BlockSpec((None, tq, KV_LORA), lambda h, i: (h, i, 0))
    return pl.pallas_call(
        body, name="mla_attn_bwd", grid=(n_h, s // tq),
        in_specs=[qblk, pl.BlockSpec((s, QK_PAD), lambda h, i: (0, 0)), pl.BlockSpec((s, KV_LORA), lambda h, i: (0, 0)), oblk, oblk,
                  pl.BlockSpec((None, tq, 1), lambda h, i: (h, i, 0))],
        out_specs=[qblk, pl.BlockSpec((s, QK_PAD), lambda h, i: (0, 0)), pl.BlockSpec((s, KV_LORA), lambda h, i: (0, 0))],
        out_shape=[_sds((n_h, s, QK_PAD), BF16), _sds((s, QK_PAD), F32), _sds((s, KV_LORA), F32)],
        scratch_shapes=[pltpu.VMEM((tq, QK_PAD), F32)], compiler_params=_params(),
    )(qcat, kcat, kcat, o, do, lse)


def _mla_fwd(h, wts, cos_k, sin_k, tm):
    s, d = h.shape
    nt = s // tm
    lat = _matmul("mla_lat", h, wts["w_in"], a_blk=(tm, d), a_map=lambda i, k: (i, 0), b_blk=(d, LAT_PAD), b_map=lambda i, k: (0, 0),
                  o_shape=(s, LAT_PAD), o_blk=(tm, LAT_PAD), o_map=lambda i, k: (i, 0), grid=(nt, 1), contract=NN, out_dtype=F32)
    cq, kcat = _mla_mid(lat, wts["q_norm"], wts["kv_norm"], cos_k, sin_k, tm)
    qcat = _mla_q(cq, wts["wq"], wts["wukp"], cos_k, sin_k, tm)
    o_lat, lse = _flash_fwd(qcat, kcat, tm, tm)
    o = _matmul("mla_uv", o_lat, wts["wuv2"], a_blk=(None, tm, KV_LORA), a_map=lambda i, p, r: (2 * p + r, i, 0),
                b_blk=(None, KV_LORA, 2 * V_HEAD), b_map=lambda i, p, r: (2 * p + r, 0, 0), o_shape=(s, d), o_blk=(tm, 2 * V_HEAD),
                o_map=lambda i, p, r: (i, p), grid=(nt, N_HEADS // 2, 2), contract=NN, out_dtype=BF16)
    u = _matmul("mla_out", o, wts["w_o"], a_blk=(tm, d), a_map=lambda i, k: (i, 0), b_blk=(d, d), b_map=lambda i, k: (0, 0),
                o_shape=(s, d), o_blk=(tm, d), o_map=lambda i, k: (i, 0), grid=(nt, 1), contract=NN, out_dtype=F32)
    return u, (lat, cq, kcat, qcat, o_lat, lse, o)


def _mla_bwd(du, h, saved, wts, cos_k, sin_k, tm):
    lat, cq, kcat, qcat, o_lat, lse, o = saved
    s, d = h.shape
    nt = s // tm
    do = _matmul("mla_do", du, wts["w_o"], a_blk=(tm, d), a_map=lambda i, k: (i, 0), b_blk=(d, d), b_map=lambda i, k: (0, 0),
                 o_shape=(s, d), o_blk=(tm, d), o_map=lambda i, k: (i, 0), grid=(nt, 1), contract=NT, out_dtype=BF16)
    dw_o = _matmul("mla_dwo", o, du, a_blk=(tm, d), a_map=lambda k: (k, 0), b_blk=(tm, d), b_map=lambda k: (k, 0),
                   o_shape=(d, d), o_blk=(d, d), o_map=lambda k: (0, 0), grid=(nt,), contract=TN, out_dtype=F32)
    do_lat = _matmul("mla_dolat", do, wts["wuv2"], a_blk=(tm, 2 * V_HEAD), a_map=lambda hh, i, k: (i, hh // 2),
                     b_blk=(None, KV_LORA, 2 * V_HEAD), b_map=lambda hh, i, k: (hh, 0, 0), o_shape=(N_HEADS, s, KV_LORA),
                     o_blk=(None, tm, KV_LORA), o_map=lambda hh, i, k: (hh, i, 0), grid=(N_HEADS, nt, 1), contract=NT, out_dtype=BF16)
    dwuv2 = _matmul("mla_dwuv", o_lat, do, a_blk=(None, tm, KV_LORA), a_map=lambda hh, k: (hh, k, 0), b_blk=(tm, 2 * V_HEAD),
                    b_map=lambda hh, k: (k, hh // 2), o_shape=(N_HEADS, KV_LORA, 2 * V_HEAD), o_blk=(None, KV_LORA, 2 * V_HEAD),
                    o_map=lambda hh, k: (hh, 0, 0), grid=(N_HEADS, nt), contract=TN, out_dtype=F32)
    dqcat, dkcat, dv = _flash_bwd(qcat, kcat, o_lat, do_lat, lse, tm, tm)
    dcq, dwq, dwukp = _mla_q_bwd(cq, wts["wq"], wts["wukp"], cos_k, sin_k, dqcat, tm)
    dlat, (dqn, dkn) = _mla_mid_bwd(lat, dcq, dkcat, dv, wts["q_norm"], wts["kv_norm"], cos_k, sin_k, tm)
    dh = _matmul("mla_dh", dlat, wts["w_in"], a_blk=(tm, LAT_PAD), a_map=lambda i, k: (i, 0), b_blk=(d, LAT_PAD),
                 b_map=lambda i, k: (0, 0), o_shape=(s, d), o_blk=(tm, d), o_map=lambda i, k: (i, 0), grid=(nt, 1), contract=NT,
                 out_dtype=F32)
    dw_in = _matmul("mla_dwin", h, dlat, a_blk=(tm, d), a_map=lambda k: (k, 0), b_blk=(tm, LAT_PAD), b_map=lambda k: (k, 0),
                    o_shape=(d, LAT_PAD), o_blk=(d, LAT_PAD), o_map=lambda k: (0, 0), grid=(nt,), contract=TN, out_dtype=F32)
    return dh, dict(w_in=dw_in, wq=dwq, wukp=dwukp, wuv2=dwuv2, w_o=dw_o, q_norm=dqn, kv_norm=dkn)


def _adamw(name, parts, w, m, v):
    n_parts, r, c = parts.shape
    tr = r
    for cand in (256, 128, 64, 32, 16, 8):
        if r > cand and r % cand == 0:
            tr = cand
            break

    def body(p_ref, w_ref, m_ref, v_ref, g_ref, d_ref, nm_ref, nv_ref):
        g = p_ref[0].astype(F32)
        for k in range(1, n_parts):
            g = g + p_ref[k].astype(F32)
        nm = ADAM_B1 * m_ref[...] + (1.0 - ADAM_B1) * g
        nv = ADAM_B2 * v_ref[...] + (1.0 - ADAM_B2) * (g * g)
        m_hat = nm / (1.0 - ADAM_B1 ** ADAM_STEP)
        v_hat = nv / (1.0 - ADAM_B2 ** ADAM_STEP)
        g_ref[...] = g
        d_ref[...] = -ADAM_LR * (m_hat / (jnp.sqrt(v_hat) + ADAM_EPS) + ADAM_WD * w_ref[...])
        nm_ref[...] = nm
        nv_ref[...] = nv

    blk = pl.BlockSpec((tr, c), lambda i: (i, 0))
    return pl.pallas_call(
        body, name=name, grid=(r // tr,), in_specs=[pl.BlockSpec((n_parts, tr, c), lambda i: (0, i, 0)), blk, blk, blk],
        out_specs=[blk] * 4, out_shape=[_sds((r, c), F32)] * 4, compiler_params=_params(),
    )(parts, w, m, v)


def _mesh_pos():
    return lax.axis_index("x"), lax.axis_index("y"), lax.axis_index("c")


def _flip(pos, mask):
    return tuple(1 - p if (mask >> (2 - b)) & 1 else p for b, p in enumerate(pos))


def _index(pos):
    return 4 * pos[0] + 2 * pos[1] + pos[2]


def _all_gather(name, xs):
    n = len(xs)

    def body(*refs):
        x_refs, o_refs = refs[:n], refs[n:2 * n]
        send_sems, recv_sems, local_sems = refs[2 * n:]
        me = _mesh_pos()
        sibling = _flip(me, 1)
        others = [_flip(me, 4), _flip(me, 2), _flip(me, 6)]

        def copy(k, j, block, to, src=None):
            dst = o_refs[k].at[_index(block)]
            return pltpu.make_async_remote_copy(
                src_ref=dst if src is None else src, dst_ref=dst, send_sem=send_sems.at[k, j], recv_sem=recv_sems.at[k, j],
                device_id=to, device_id_type=MESH)

        local = [pltpu.make_async_copy(x_refs[k], o_refs[k].at[_index(me)], local_sems.at[k]) for k in range(n)]
        for cp in local:
            cp.start()
        first = []
        for k in range(n):
            first.append(copy(k, 0, me, sibling, src=x_refs[k]))
            first += [copy(k, 1 + j, me, other, src=x_refs[k]) for j, other in enumerate(others)]
        for cp in first:
            cp.start()
        passed = []
        for j, other in enumerate(others):
            for k in range(n):
                copy(k, 1 + j, other, me).wait_recv()
                cp = copy(k, 4 + j, other, sibling)
                cp.start()
                passed.append(cp)
        for k in range(n):
            copy(k, 0, sibling, me).wait_recv()
        for j, other in enumerate(others):
            for k in range(n):
                copy(k, 4 + j, _flip(other, 1), me).wait_recv()
        for cp in first + passed:
            cp.wait_send()
        for cp in local:
            cp.wait()

    any_spec = pl.BlockSpec(memory_space=pl.ANY)
    return pl.pallas_call(
        body, name=name, in_specs=[any_spec] * n, out_specs=[any_spec] * n,
        out_shape=[_sds((N_DEV,) + x.shape, x.dtype) for x in xs],
        scratch_shapes=[pltpu.SemaphoreType.DMA((n, 7)), pltpu.SemaphoreType.DMA((n, 7)), pltpu.SemaphoreType.DMA((n,))],
    )(*xs)


def _all_to_all(name, groups):
    flat = [(gi, f) for gi, grp in enumerate(groups) for f in range(len(grp))]
    n = len(flat)
    n_groups = len(groups)

    def body(*refs):
        x_refs, o_refs = refs[:n], refs[n:n + n_groups]
        send_sems, recv_sems, local_sems = refs[n + n_groups:]
        me = _mesh_pos()
        local, sends, recvs = [], [], []
        for k, (gi, f) in enumerate(flat):
            local.append(pltpu.make_async_copy(x_refs[k].at[_index(me)], o_refs[gi].at[_index(me), f], local_sems.at[k]))
            for mask in range(1, N_DEV):
                peer = _flip(me, mask)
                sends.append(pltpu.make_async_remote_copy(
                    src_ref=x_refs[k].at[_index(peer)], dst_ref=o_refs[gi].at[_index(me), f], send_sem=send_sems.at[k, mask - 1],
                    recv_sem=recv_sems.at[k, mask - 1], device_id=peer, device_id_type=MESH))
                recvs.append(pltpu.make_async_remote_copy(
                    src_ref=x_refs[k].at[_index(me)], dst_ref=o_refs[gi].at[_index(peer), f], send_sem=send_sems.at[k, mask - 1],
                    recv_sem=recv_sems.at[k, mask - 1], device_id=peer, device_id_type=MESH))
        for cp in local + sends:
            cp.start()
        for cp in recvs:
            cp.wait_recv()
        for cp in sends:
            cp.wait_send()
        for cp in local:
            cp.wait()

    any_spec = pl.BlockSpec(memory_space=pl.ANY)
    return pl.pallas_call(
        body, name=name, in_specs=[any_spec] * n, out_specs=[any_spec] * n_groups,
        out_shape=[_sds((N_DEV, len(grp)) + grp[0].shape[1:], grp[0].dtype) for grp in groups],
        scratch_shapes=[pltpu.SemaphoreType.DMA((n, 7)), pltpu.SemaphoreType.DMA((n, 7)), pltpu.SemaphoreType.DMA((n,))],
    )(*[a for grp in groups for a in grp])


def _rope_tables(s):
    inv = 1.0 / (ROPE_THETA ** (jnp.arange(0, QK_ROPE, 2, dtype=F32) / QK_ROPE))
    ang = jnp.arange(s, dtype=F32)[:, None] * inv[None, :]
    pad = jnp.zeros((s, KV_LORA - QK_ROPE), F32)
    return (jnp.concatenate([jnp.cos(ang), jnp.cos(ang), pad], axis=1), jnp.concatenate([jnp.sin(ang), jnp.sin(ang), pad], axis=1))


def _row_of(v):
    return v.reshape(1, -1)


def kernel(x, c, ada_w, ada_b, norm_g, ffn_w_in, ffn_w_out, pool_w, pool_b, pool_scale, mla_w_in, mla_q_norm, mla_kv_norm, mla_w_uq, mla_w_uk, mla_w_uv, mla_w_o, loss_target, m_ada_w, m_ada_b, m_norm_g, m_ffn_w_in, m_ffn_w_out, m_pool_w, m_pool_b, m_pool_scale, m_mla_w_in, m_mla_q_norm, m_mla_kv_norm, m_mla_w_uq, m_mla_w_uk, m_mla_w_uv, m_mla_w_o, v_ada_w, v_ada_b, v_norm_g, v_ffn_w_in, v_ffn_w_out, v_pool_w, v_pool_b, v_pool_scale, v_mla_w_in, v_mla_q_norm, v_mla_kv_norm, v_mla_w_uq, v_mla_w_uk, v_mla_w_uv, v_mla_w_o):
    s, d = x.shape[1], x.shape[2]
    tm = min(512, s)
    tr = min(256, s)
    me = 4 * lax.axis_index("x") + 2 * lax.axis_index("y") + lax.axis_index("c")
    x0 = x.reshape(s, d)
    target = loss_target.reshape(s, d)
    n_mod = ada_w.shape[2] * N_DEV // d
    mod_blk = ada_w.shape[2]

    small = jnp.concatenate([c.reshape(-1), norm_g.reshape(-1), pool_b.reshape(-1), mla_q_norm.reshape(-1)]).reshape(1, -1)
    w_in_loc = [ffn_w_in[i, f].astype(BF16) for i in range(2) for f in range(2)]
    w_out_loc = [ffn_w_out[i, f].astype(BF16) for i in range(2) for f in range(2)]
    gathered = _all_gather(
        "gather_weights",
        [small] + w_in_loc + w_out_loc + [pool_w.reshape(-1, POOL_GROUP).astype(BF16), mla_w_in[0].astype(BF16),
                                          mla_w_uq.reshape(mla_w_uq.shape[1], -1).astype(BF16), mla_w_o[0].astype(BF16)])
    small_all = gathered[0].reshape(N_DEV, -1)
    w_in8 = gathered[1:5]
    w_out4 = [g.reshape(4, FF_BLK, d) for g in gathered[5:9]]
    pool_w_all, mla_w_in_all, mla_w_uq_all, mla_w_o_all = gathered[9:13]

    c_all = small_all[:, :d]
    off = d
    g_all = small_all[:, off:off + 12 * (d // N_DEV)].reshape(N_DEV, 2, 6, d // N_DEV).transpose(1, 2, 0, 3).reshape(2, 6, d)
    off += 12 * (d // N_DEV)
    pool_b_all = small_all[:, off:off + 4 * 32].reshape(N_DEV, 4, 32).transpose(1, 0, 2).reshape(1, d)
    off += 4 * 32
    q_norm_all = small_all[:, off:off + 32].reshape(1, Q_LORA)
    kv_norm_row = mla_kv_norm.reshape(1, KV_LORA)
    pscale_row = pool_scale.reshape(1, d)

    w4 = pool_w_all.reshape(N_DEV, 4, 32, POOL_GROUP).transpose(1, 0, 2, 3).reshape(4, POOL_GROUP, POOL_GROUP)
    uq = mla_w_uq_all.reshape(Q_LORA, N_HEADS, QK_NOPE + QK_ROPE).transpose(1, 0, 2)
    zq = jnp.zeros((N_HEADS, Q_LORA, QK_NOPE), BF16)
    wq = jnp.concatenate([uq[:, :, :QK_NOPE], zq, uq[:, :, QK_NOPE:], jnp.zeros((N_HEADS, Q_LORA, QK_PAD - KV_LORA - QK_ROPE), BF16)],
                         axis=2)
    wukp = jnp.pad(mla_w_uk[0].transpose(1, 2, 0).astype(BF16), ((0, 0), (0, QK_PAD - QK_NOPE), (0, QK_PAD - KV_LORA)))
    uv = mla_w_uv[0].transpose(1, 0, 2).astype(BF16)
    even = (jnp.arange(N_HEADS) % 2 == 0)[:, None, None]
    wuv2 = jnp.where(even, jnp.concatenate([uv, jnp.zeros_like(uv)], axis=2), jnp.concatenate([jnp.zeros_like(uv), uv], axis=2))
    mla_wts = dict(w_in=jnp.pad(mla_w_in_all.reshape(d, -1), ((0, 0), (0, LAT_PAD - mla_w_in.shape[2]))), wq=wq, wukp=wukp,
                   wuv2=wuv2, w_o=mla_w_o_all.reshape(d, d), q_norm=q_norm_all, kv_norm=kv_norm_row)
    cos_k, sin_k = _rope_tables(s)

    (sc_all,), _ = _rowmap("ada_silu", lambda cv: ((cv * jax.nn.sigmoid(cv),), ()), [(c_all, (N_DEV, d), lambda i: (0, 0))],
                           [(_sds((N_DEV, d), F32), (N_DEV, d), lambda i: (0, 0))], [], (1,))
    ada_b_loc = lax.dynamic_slice_in_dim(ada_b, me * mod_blk, mod_blk, axis=1).reshape(2, 1, mod_blk)
    m_pad = 2 * N_DEV
    modp = _matmul("ada_mod", jnp.pad(sc_all, ((0, m_pad - N_DEV), (0, 0))), ada_w, a_blk=(m_pad, d), a_map=lambda i, k: (0, 0),
                   b_blk=(None, d, mod_blk), b_map=lambda i, k: (i, 0, 0), o_shape=(2, m_pad, mod_blk), o_blk=(None, m_pad, mod_blk),
                   o_map=lambda i, k: (i, 0, 0), grid=(2, 1), contract=NN, out_dtype=F32, bias=ada_b_loc, bias_blk=(None, 1, mod_blk),
                   bias_map=lambda i, k: (i, 0, 0))[:, :N_DEV]
    (modp_all,) = _all_gather("gather_mod", [modp.reshape(2 * N_DEV, mod_blk)])
    mod = lax.dynamic_index_in_dim(modp_all.reshape(N_DEV, 2, N_DEV, mod_blk), me, axis=2, keepdims=False)
    mod = mod.transpose(1, 0, 2).reshape(2, n_mod, d)

    saved = []
    xs = x0
    for i in range(2):
        for sub in range(3):
            shift, scale, gate = (_row_of(mod[i, 3 * sub + j]) for j in range(3))
            g_pre, g_post = _row_of(g_all[i, 2 * sub]), _row_of(g_all[i, 2 * sub + 1])
            tag = f"{i}{sub}"
            if sub != 1:
                f = sub // 2
                h = _prenorm(f"prenorm_{tag}", xs, g_pre, scale, shift, BF16, tr)
                u, extra = _ffn_fwd(tag, h, w_in8[2 * i + f], w_out4[2 * i + f], tm)
                weight = 0.5
            elif i == 0:
                h = _prenorm(f"prenorm_{tag}", xs, g_pre, scale, shift, F32, tr)
                u, z, v = _pool_fwd(h, w4, pool_b_all, pscale_row, tm)
                extra = (z, v)
                weight = 1.0
            else:
                h = _prenorm(f"prenorm_{tag}", xs, g_pre, scale, shift, BF16, tr)
                u, extra = _mla_fwd(h, mla_wts, cos_k, sin_k, tm)
                weight = 1.0
            saved.append((xs, h, u, extra, (shift, scale, gate, g_pre, g_post), weight))
            xs = _postnorm(f"postnorm_{tag}", xs, u, g_post, gate, weight, tr)

    def loss_fn(yv, tv):
        e = yv - tv
        return (e * (1.0 / d),), (_colsum(e * e),)

    (dx,), (sq,) = _rowmap("loss", loss_fn, [_tile(xs, tr), _tile(target, tr)], [_otile(s, d, F32, tr)], [_ored(d)], (s // tr,))
    loss = lax.psum(0.5 * jnp.sum(sq) / d, AXES)

    d_mod = [[None] * n_mod for _ in range(2)]
    d_g = [[None] * 6 for _ in range(2)]
    d_w_in = [None] * 4
    d_w_out = [None] * 4
    pool_grads = mla_grads = None
    for i in (1, 0):
        for sub in (2, 1, 0):
            xin, h, u, extra, (shift, scale, gate, g_pre, g_post), weight = saved[3 * i + sub]
            tag = f"{i}{sub}"
            du, (dgate, dgpost) = _postnorm_bwd(f"postnorm_bwd_{tag}", dx, u, g_post, gate, weight, F32 if (sub == 1 and i == 0) else BF16, tr)
            if sub != 1:
                f = sub // 2
                dh, d_w_in[2 * i + f], d_w_out[2 * i + f] = _ffn_bwd(tag, du, h, extra, w_in8[2 * i + f], w_out4[2 * i + f], tm)
            elif i == 0:
                dh, dw4, (dpscale, dpb) = _pool_bwd(du, extra[0], extra[1], w4, pscale_row, tm)
                pool_grads = (dw4, dpscale, dpb)
            else:
                dh, mla_grads = _mla_bwd(du, h, extra, mla_wts, cos_k, sin_k, tm)
            dx, (dshift, dscale, dgpre) = _prenorm_bwd(f"prenorm_bwd_{tag}", dh, xin, dx, g_pre, scale, tr)
            d_mod[i][3 * sub:3 * sub + 3] = [dshift, dscale, dgate]
            d_g[i][2 * sub:2 * sub + 2] = [dgpre, dgpost]
    grad_x = dx.reshape(x.shape)

    dw4, dpscale, dpb = pool_grads
    dwq, dwukp, dwuv2 = mla_grads["wq"], mla_grads["wukp"], mla_grads["wuv2"]
    d_uq = jnp.concatenate([dwq[:, :, :QK_NOPE], dwq[:, :, KV_LORA:KV_LORA + QK_ROPE]], axis=2).transpose(1, 0, 2)
    d_uk = dwukp[:, :QK_NOPE, :KV_LORA].transpose(2, 0, 1).reshape(KV_LORA, -1)
    d_uv = jnp.where(even, dwuv2[:, :, :V_HEAD], dwuv2[:, :, V_HEAD:]).transpose(1, 0, 2).reshape(KV_LORA, -1)
    recv = _all_to_all(
        "scatter_grads",
        [d_w_in, [g.reshape(N_DEV, FF_BLK // 2, d) for g in d_w_out],
         [dw4.reshape(4, N_DEV, 32, POOL_GROUP).transpose(1, 0, 2, 3).reshape(N_DEV, 4 * 32, POOL_GROUP)],
         [mla_grads["w_in"][:, :mla_w_in.shape[2]].reshape(N_DEV, d // N_DEV, -1)],
         [d_uq.reshape(N_DEV, Q_LORA // N_DEV, -1)], [mla_grads["w_o"].reshape(N_DEV, d // N_DEV, d)]])
    p_w_in, p_w_out, p_pool_w, p_mla_in, p_uq, p_w_o = recv

    d_mod_row = jnp.concatenate([jnp.concatenate(r, axis=1) for r in d_mod], axis=1)
    d_g_row = jnp.concatenate([jnp.concatenate(r, axis=1) for r in d_g], axis=1)
    small_g = jnp.concatenate([d_mod_row, d_g_row, dpb, dpscale, mla_grads["q_norm"], mla_grads["kv_norm"]], axis=1)
    small_g_all, uk_all, uv_all = _all_gather("gather_small_grads", [small_g, d_uk, d_uv])
    small_g_all = small_g_all.reshape(N_DEV, -1)
    n_m = 2 * n_mod * d
    d_mod_all = small_g_all[:, :n_m].reshape(N_DEV, 2, n_mod * d)
    rest = small_g_all[:, n_m:]
    p_norm_g = lax.dynamic_slice_in_dim(rest[:, :12 * d].reshape(N_DEV, 12, d), me * (d // N_DEV), d // N_DEV, axis=2)
    p_pool_b = lax.dynamic_slice_in_dim(rest[:, 12 * d:13 * d].reshape(N_DEV, 4, POOL_GROUP), me * 32, 32, axis=2)
    p_pool_scale = rest[:, 13 * d:14 * d].reshape(N_DEV, 1, d)
    p_q_norm = lax.dynamic_slice_in_dim(rest[:, 14 * d:14 * d + Q_LORA], me * 32, 32, axis=1).reshape(N_DEV, 1, 32)
    p_kv_norm = rest[:, 14 * d + Q_LORA:].reshape(N_DEV, 1, KV_LORA)

    d_mod_loc = lax.dynamic_slice_in_dim(d_mod_all, me * mod_blk, mod_blk, axis=2).transpose(1, 0, 2)
    k_pad = 128
    sc_t = jnp.pad(sc_all.T, ((0, 0), (0, k_pad - N_DEV)))
    d_ada_w = _matmul("ada_dw", sc_t, jnp.pad(d_mod_loc, ((0, 0), (0, k_pad - N_DEV), (0, 0))), a_blk=(d, k_pad),
                      a_map=lambda i, k: (0, 0), b_blk=(None, k_pad, mod_blk), b_map=lambda i, k: (i, 0, 0), o_shape=(2, d, mod_blk),
                      o_blk=(None, d, mod_blk), o_map=lambda i, k: (i, 0, 0), grid=(2, 1), contract=NN, out_dtype=F32)

    def upd(name, parts, w, m, v):
        shape = w.shape
        r, cdim = parts.shape[1], parts.shape[2]
        return [o.reshape(shape) for o in _adamw(name, parts, w.reshape(r, cdim), m.reshape(r, cdim), v.reshape(r, cdim))]

    results = [
        upd("adam_ada_w", d_ada_w.reshape(1, 2 * d, mod_blk), ada_w, m_ada_w, v_ada_w),
        upd("adam_ada_b", d_mod_all.reshape(N_DEV, 2, n_mod * d), ada_b, m_ada_b, v_ada_b),
        upd("adam_norm_g", p_norm_g, norm_g, m_norm_g, v_norm_g),
        upd("adam_ffn_w_in", p_w_in.reshape(N_DEV, 4 * d, FF_BLK), ffn_w_in, m_ffn_w_in, v_ffn_w_in),
        upd("adam_ffn_w_out", p_w_out.reshape(N_DEV, 4 * FF_BLK // 2, d), ffn_w_out, m_ffn_w_out, v_ffn_w_out),
        upd("adam_pool_w", p_pool_w.reshape(N_DEV, 4 * 32, POOL_GROUP), pool_w, m_pool_w, v_pool_w),
        upd("adam_pool_b", p_pool_b, pool_b, m_pool_b, v_pool_b),
        upd("adam_pool_scale", p_pool_scale, pool_scale, m_pool_scale, v_pool_scale),
        upd("adam_mla_w_in", p_mla_in.reshape(N_DEV, d // N_DEV, -1), mla_w_in, m_mla_w_in, v_mla_w_in),
        upd("adam_mla_q_norm", p_q_norm, mla_q_norm, m_mla_q_norm, v_mla_q_norm),
        upd("adam_mla_kv_norm", p_kv_norm, mla_kv_norm, m_mla_kv_norm, v_mla_kv_norm),
        upd("adam_mla_w_uq", p_uq.reshape(N_DEV, Q_LORA // N_DEV, -1), mla_w_uq, m_mla_w_uq, v_mla_w_uq),
        upd("adam_mla_w_uk", uk_all, mla_w_uk, m_mla_w_uk, v_mla_w_uk),
        upd("adam_mla_w_uv", uv_all, mla_w_uv, m_mla_w_uv, v_mla_w_uv),
        upd("adam_mla_w_o", p_w_o.reshape(N_DEV, d // N_DEV, d), mla_w_o, m_mla_w_o, v_mla_w_o),
    ]
    outs = [loss, grad_x]
    for j in range(4):
        outs += [r[j] for r in results]
    return tuple(outs)
```

```python
import functools

import jax
import jax.numpy as jnp
from jax import lax
from jax.experimental import pallas as pl
from jax.experimental.pallas import tpu as pltpu

F32 = jnp.float32
BF16 = jnp.bfloat16
N_DEV = 8
AXES = ("x", "y", "c")
MESH = pl.DeviceIdType.MESH

D_MODEL = 1024
N_HEADS = 16
QK_NOPE = 64
QK_ROPE = 32
V_HEAD = 64
Q_LORA = 256
KV_LORA = 128
LAT_PAD = 512
QK_PAD = 256
D_FF = 2816
FF_BLK = 2 * D_FF // N_DEV
POOL_WINDOWS = (2, 4, 8, 16)
POOL_GROUP = 256
ROPE_THETA = 10000.0
EPS = 1e-6
ATTN_SCALE = (QK_NOPE + QK_ROPE) ** -0.5
ADAM_LR, ADAM_B1, ADAM_B2, ADAM_EPS, ADAM_WD, ADAM_STEP = 0.001, 0.9, 0.999, 1e-08, 0.01, 10
VMEM_LIMIT = 56 * 1024 * 1024

NN = ((1,), (0,))
NT = ((1,), (1,))
TN = ((0,), (0,))


def _params(**kw):
    return pltpu.CompilerParams(vmem_limit_bytes=VMEM_LIMIT, **kw)


def _dot(a, b, contract):
    return lax.dot_general(a, b, (contract, ((), ())), preferred_element_type=F32)


def _matmul(name, a, b, *, a_blk, a_map, b_blk, b_map, o_shape, o_blk, o_map, grid, contract, out_dtype,
            bias=None, bias_blk=None, bias_map=None):
    n_k = grid[-1]
    k_axis = len(grid) - 1
    acc_shape = tuple(d for d in o_blk if d is not None)

    def body(*refs):
        if bias is None:
            a_ref, b_ref, o_ref, acc = refs
        else:
            a_ref, b_ref, bias_ref, o_ref, acc = refs
        k = pl.program_id(k_axis)

        @pl.when(k == 0)
        def _():
            acc[...] = jnp.zeros_like(acc)

        acc[...] += _dot(a_ref[...].astype(BF16), b_ref[...].astype(BF16), contract)

        @pl.when(k == n_k - 1)
        def _():
            r = acc[...]
            if bias is not None:
                r = r + bias_ref[...]
            o_ref[...] = r.astype(o_ref.dtype)

    in_specs = [pl.BlockSpec(a_blk, a_map), pl.BlockSpec(b_blk, b_map)]
    args = [a, b]
    if bias is not None:
        in_specs.append(pl.BlockSpec(bias_blk, bias_map))
        args.append(bias)
    return pl.pallas_call(
        body, name=name, grid=grid, in_specs=in_specs, out_specs=pl.BlockSpec(o_blk, o_map),
        out_shape=jax.ShapeDtypeStruct(o_shape, out_dtype), scratch_shapes=[pltpu.VMEM(acc_shape, F32)],
        compiler_params=_params(),
    )(*args)


def _rowmap(name, fn, ins, outs, reds, grid):
    n_in, n_out, n_red = len(ins), len(outs), len(reds)

    def body(*refs):
        in_refs = refs[:n_in]
        out_refs = refs[n_in:n_in + n_out]
        red_refs = refs[n_in + n_out:]
        out_vals, red_vals = fn(*[r[...] for r in in_refs])
        for r, v in zip(out_refs, out_vals):
            r[...] = v.astype(r.dtype)
        if n_red:
            first = pl.program_id(0) == 0
            for ax in range(1, len(grid)):
                first = jnp.logical_and(first, pl.program_id(ax) == 0)

            @pl.when(first)
            def _():
                for r in red_refs:
                    r[...] = jnp.zeros_like(r)

            for r, v in zip(red_refs, red_vals):
                r[...] += v

    res = pl.pallas_call(
        body, name=name, grid=grid,
        in_specs=[pl.BlockSpec(blk, imap) for _, blk, imap in ins],
        out_specs=[pl.BlockSpec(blk, imap) for _, blk, imap in list(outs) + list(reds)],
        out_shape=[sds for sds, _, _ in list(outs) + list(reds)],
        compiler_params=_params(),
    )(*[a for a, _, _ in ins])
    return res[:n_out], res[n_out:]


def _sds(shape, dtype):
    return jax.ShapeDtypeStruct(shape, dtype)


def _tile(a, tm):
    return (a, (tm, a.shape[1]), lambda i: (i, 0))


def _row(a):
    return (a, (1, a.shape[1]), lambda i: (0, 0))


def _otile(n, c, dtype, tm):
    return (_sds((n, c), dtype), (tm, c), lambda i: (i, 0))


def _ored(c):
    return (_sds((1, c), F32), (1, c), lambda i: (0, 0))


def _colsum(v):
    return jnp.sum(v, axis=0, keepdims=True)


def _rstd(v):
    return lax.rsqrt(jnp.mean(v * v, axis=-1, keepdims=True) + EPS)


def _prenorm(name, x, g_pre, scale, shift, out_dtype, tm):
    n, d = x.shape

    def fn(xv, g, sc, sh):
        return (xv * _rstd(xv) * g * (1.0 + sc) + sh,), ()

    (h,), _ = _rowmap(name, fn, [_tile(x, tm), _row(g_pre), _row(scale), _row(shift)], [_otile(n, d, out_dtype, tm)], [],
                      (n // tm,))
    return h


def _postnorm(name, x, u, g_post, gate, weight, tm):
    n, d = x.shape

    def fn(xv, uv, g, gt):
        return (xv + weight * (1.0 + gt) * (uv * _rstd(uv) * g),), ()

    (y,), _ = _rowmap(name, fn, [_tile(x, tm), _tile(u, tm), _row(g_post), _row(gate)], [_otile(n, d, F32, tm)], [], (n // tm,))
    return y


def _postnorm_bwd(name, dout, u, g_post, gate, weight, out_dtype, tm):
    n, d = u.shape

    def fn(dv, uv, g, gt):
        r = _rstd(uv)
        un = uv * r
        dy = dv * (weight * (1.0 + gt))
        a = dy * g
        du = r * (a - un * jnp.mean(a * un, axis=-1, keepdims=True))
        return (du,), (_colsum(dv * (weight * (un * g))), _colsum(dy * un))

    (du,), reds = _rowmap(name, fn, [_tile(dout, tm), _tile(u, tm), _row(g_post), _row(gate)], [_otile(n, d, out_dtype, tm)],
                          [_ored(d), _ored(d)], (n // tm,))
    return du, reds


def _prenorm_bwd(name, dh, x, dout, g_pre, scale, tm):
    n, d = x.shape

    def fn(dhv, xv, dv, g, sc):
        dhv = dhv.astype(F32)
        r = _rstd(xv)
        xn = xv * r
        b = dhv * (g * (1.0 + sc))
        dx = dv + r * (b - xn * jnp.mean(b * xn, axis=-1, keepdims=True))
        return (dx,), (_colsum(dhv), _colsum(dhv * (xn * g)), _colsum(dhv * ((1.0 + sc) * xn)))

    (dx,), reds = _rowmap(name, fn, [_tile(dh, tm), _tile(x, tm), _tile(dout, tm), _row(g_pre), _row(scale)],
                          [_otile(n, d, F32, tm)], [_ored(d), _ored(d), _ored(d)], (n // tm,))
    return dx, reds


def _ffn_fwd(tag, h, w_in8, w_out4, tm):
    s, d = h.shape
    nt = s // tm
    gu = _matmul(f"ffn_up_{tag}", h, w_in8, a_blk=(tm, d), a_map=lambda g, i, k: (i, 0), b_blk=(None, d, FF_BLK),
                 b_map=lambda g, i, k: (g, 0, 0), o_shape=(8, s, FF_BLK), o_blk=(None, tm, FF_BLK), o_map=lambda g, i, k: (g, i, 0),
                 grid=(8, nt, 1), contract=NN, out_dtype=BF16)

    def act_fn(gv, uv):
        gv = gv.astype(F32)
        return (gv * jax.nn.sigmoid(gv) * uv.astype(F32),), ()

    (act,), _ = _rowmap(f"ffn_act_{tag}", act_fn,
                        [(gu, (None, tm, FF_BLK), lambda j, i: (j, i, 0)), (gu, (None, tm, FF_BLK), lambda j, i: (j + 4, i, 0))],
                        [(_sds((4, s, FF_BLK), BF16), (None, tm, FF_BLK), lambda j, i: (j, i, 0))], [], (4, nt))
    u = _matmul(f"ffn_down_{tag}", act, w_out4, a_blk=(None, tm, FF_BLK), a_map=lambda i, k: (k, i, 0), b_blk=(None, FF_BLK, d),
                b_map=lambda i, k: (k, 0, 0), o_shape=(s, d), o_blk=(tm, d), o_map=lambda i, k: (i, 0), grid=(nt, 4),
                contract=NN, out_dtype=F32)
    return u, gu


def _ffn_bwd(tag, du, h, gu, w_in8, w_out4, tm):
    s, d = h.shape
    nt = s // tm
    dact = _matmul(f"ffn_dact_{tag}", du, w_out4, a_blk=(tm, d), a_map=lambda j, i, k: (i, 0), b_blk=(None, FF_BLK, d),
                   b_map=lambda j, i, k: (j, 0, 0), o_shape=(4, s, FF_BLK), o_blk=(None, tm, FF_BLK), o_map=lambda j, i, k: (j, i, 0),
                   grid=(4, nt, 1), contract=NT, out_dtype=BF16)

    def bwd_fn(gv, uv, dav):
        gv, uv, dav = gv.astype(F32), uv.astype(F32), dav.astype(F32)
        sg = jax.nn.sigmoid(gv)
        silu = gv * sg
        return (jnp.stack([dav * uv * (sg * (1.0 + gv * (1.0 - sg))), dav * silu]), silu * uv), ()

    blk = (None, tm, FF_BLK)
    (dgu, act), _ = _rowmap(
        f"ffn_dgu_{tag}", bwd_fn,
        [(gu, blk, lambda j, i: (j, i, 0)), (gu, blk, lambda j, i: (j + 4, i, 0)), (dact, blk, lambda j, i: (j, i, 0))],
        [(_sds((2, 4, s, FF_BLK), BF16), (2, None, tm, FF_BLK), lambda j, i: (0, j, i, 0)),
         (_sds((4, s, FF_BLK), BF16), blk, lambda j, i: (j, i, 0))], [], (4, nt))
    dgu = dgu.reshape(8, s, FF_BLK)
    dw_in8 = _matmul(f"ffn_dwin_{tag}", h, dgu, a_blk=(tm, d), a_map=lambda g, k: (k, 0), b_blk=(None, tm, FF_BLK),
                     b_map=lambda g, k: (g, k, 0), o_shape=(8, d, FF_BLK), o_blk=(None, d, FF_BLK), o_map=lambda g, k: (g, 0, 0),
                     grid=(8, nt), contract=TN, out_dtype=BF16)
    dw_out4 = _matmul(f"ffn_dwout_{tag}", act, du, a_blk=(None, tm, FF_BLK), a_map=lambda j, k: (j, k, 0), b_blk=(tm, d),
                      b_map=lambda j, k: (k, 0), o_shape=(4, FF_BLK, d), o_blk=(None, FF_BLK, d), o_map=lambda j, k: (j, 0, 0),
                      grid=(4, nt), contract=TN, out_dtype=BF16)
    return dgu, dw_in8, dw_out4


def _ffn_dh(tag, dgu, w_in8, tm):
    s = dgu.shape[1]
    d = w_in8.shape[1]
    return _matmul(f"ffn_dh_{tag}", dgu, w_in8, a_blk=(None, tm, FF_BLK), a_map=lambda i, k: (k, i, 0), b_blk=(None, d, FF_BLK),
                   b_map=lambda i, k: (k, 0, 0), o_shape=(s, d), o_blk=(tm, d), o_map=lambda i, k: (i, 0), grid=(s // tm, 8),
                   contract=NT, out_dtype=F32)


def _window_sum(x, window, transpose):
    s = x.shape[0]
    t = lax.broadcasted_iota(jnp.int32, (s, 1), 0)
    half = window // 2
    cnt = jnp.minimum(t + half, s) - jnp.maximum(t - half, 0)
    inv = 1.0 / cnt.astype(F32)
    if transpose:
        x = x * inv
        offsets = range(-half + 1, half + 1)
    else:
        offsets = range(-half, half)
    acc = jnp.zeros_like(x)
    for o in offsets:
        shifted = x if o == 0 else pltpu.roll(x, (-o) % s, 0)
        valid = jnp.logical_and(t + o >= 0, t + o < s)
        acc = acc + jnp.where(valid, shifted, 0.0)
    return acc if transpose else acc * inv


def _pool_mix(name, x, transpose, out_dtype):
    s, d = x.shape

    def body(x_ref, o_ref):
        g = pl.program_id(0)
        for gi, window in enumerate(POOL_WINDOWS):
            @pl.when(g == gi)
            def _(window=window):
                xv = x_ref[...].astype(F32)
                o_ref[...] = (_window_sum(xv, window, transpose) - xv).astype(o_ref.dtype)

    return pl.pallas_call(
        body, name=name, grid=(len(POOL_WINDOWS),), in_specs=[pl.BlockSpec((s, POOL_GROUP), lambda g: (0, g))],
        out_specs=pl.BlockSpec((s, POOL_GROUP), lambda g: (0, g)), out_shape=_sds((s, d), out_dtype), compiler_params=_params(),
    )(x)


def _pool_fwd(h, w4, bias, pscale, tm):
    s, d = h.shape
    nt = s // tm
    z = _pool_mix("pool_mix", h, False, BF16)
    v = _matmul("pool_proj", z, w4, a_blk=(tm, POOL_GROUP), a_map=lambda i, g, k: (i, g), b_blk=(None, POOL_GROUP, POOL_GROUP),
                b_map=lambda i, g, k: (g, 0, 0), o_shape=(s, d), o_blk=(tm, POOL_GROUP), o_map=lambda i, g, k: (i, g),
                grid=(nt, 4, 1), contract=NN, out_dtype=F32, bias=bias, bias_blk=(1, POOL_GROUP), bias_map=lambda i, g, k: (0, g))
    (u,), _ = _rowmap("pool_scale", lambda vv, ps: ((vv * ps,), ()), [_tile(v, tm), _row(pscale)], [_otile(s, d, F32, tm)], [],
                      (nt,))
    return u, z, v


def _pool_bwd(du, z, v, w4, pscale, tm):
    s, d = du.shape
    nt = s // tm

    def fn(duv, vv, ps):
        dv = duv * ps
        return (dv,), (_colsum(duv * vv), _colsum(dv))

    (dv,), reds = _rowmap("pool_dscale", fn, [_tile(du, tm), _tile(v, tm), _row(pscale)], [_otile(s, d, BF16, tm)],
                          [_ored(d), _ored(d)], (nt,))
    dw4 = _matmul("pool_dw", z, dv, a_blk=(tm, POOL_GROUP), a_map=lambda g, k: (k, g), b_blk=(tm, POOL_GROUP),
                  b_map=lambda g, k: (k, g), o_shape=(4, POOL_GROUP, POOL_GROUP), o_blk=(None, POOL_GROUP, POOL_GROUP),
                  o_map=lambda g, k: (g, 0, 0), grid=(4, nt), contract=TN, out_dtype=F32)
    dz = _matmul("pool_dz", dv, w4, a_blk=(tm, POOL_GROUP), a_map=lambda i, g, k: (i, g), b_blk=(None, POOL_GROUP, POOL_GROUP),
                 b_map=lambda i, g, k: (g, 0, 0), o_shape=(s, d), o_blk=(tm, POOL_GROUP), o_map=lambda i, g, k: (i, g),
                 grid=(nt, 4, 1), contract=NT, out_dtype=F32)
    dh = _pool_mix("pool_mix_t", dz, True, F32)
    return dh, dw4, reds


def _lane(shape):
    return lax.broadcasted_iota(jnp.int32, shape, 1)


def _rope_swap(v, transpose):
    half = QK_ROPE // 2
    lane = _lane(v.shape)
    up = pltpu.roll(v, v.shape[1] - half, 1)
    down = pltpu.roll(v, half, 1)
    if transpose:
        return jnp.where(lane < half, up, jnp.where(lane < QK_ROPE, -down, 0.0))
    return jnp.where(lane < half, -up, jnp.where(lane < QK_ROPE, down, 0.0))


def _rope(v, cos, sin):
    return v * cos + _rope_swap(v, False) * sin


def _rope_t(g, cos, sin):
    return g * cos + _rope_swap(g * sin, True)


def _mla_mid(lat, q_norm, kv_norm, cos_k, sin_k, tm):
    s = lat.shape[0]

    def fn(lv, qn, kn, cs, sn):
        cq = lv[:, :Q_LORA]
        ckv = lv[:, Q_LORA:Q_LORA + KV_LORA]
        kr = lv[:, Q_LORA + KV_LORA:]
        cq = cq * _rstd(cq) * qn
        ckv = ckv * _rstd(ckv) * kn
        return (cq, jnp.concatenate([ckv, _rope(kr, cs, sn)], axis=1)), ()

    (cq, kcat), _ = _rowmap("mla_mid", fn, [_tile(lat, tm), _row(q_norm), _row(kv_norm), _tile(cos_k, tm), _tile(sin_k, tm)],
                            [_otile(s, Q_LORA, BF16, tm), _otile(s, QK_PAD, BF16, tm)], [], (s // tm,))
    return cq, kcat


def _mla_mid_bwd(lat, dcq, dkcat, dv, q_norm, kv_norm, cos_k, sin_k, tm):
    s = lat.shape[0]

    def fn(lv, dq, dk, dvv, qn, kn, cs, sn):
        cq = lv[:, :Q_LORA]
        ckv = lv[:, Q_LORA:Q_LORA + KV_LORA]
        rq, rk = _rstd(cq), _rstd(ckv)
        cqn, ckn = cq * rq, ckv * rk
        a = dq * qn
        d_cq = rq * (a - cqn * jnp.mean(a * cqn, axis=-1, keepdims=True))
        dckv = dk[:, :KV_LORA] + dvv
        a2 = dckv * kn
        d_ckv = rk * (a2 - ckn * jnp.mean(a2 * ckn, axis=-1, keepdims=True))
        d_kr = _rope_t(dk[:, KV_LORA:], cs, sn)
        return (jnp.concatenate([d_cq, d_ckv, d_kr], axis=1),), (_colsum(dq * cqn), _colsum(dckv * ckn))

    (dlat,), reds = _rowmap(
        "mla_mid_bwd", fn,
        [_tile(lat, tm), _tile(dcq, tm), _tile(dkcat, tm), _tile(dv, tm), _row(q_norm), _row(kv_norm), _tile(cos_k, tm),
         _tile(sin_k, tm)],
        [_otile(s, LAT_PAD, BF16, tm)], [_ored(Q_LORA), _ored(KV_LORA)], (s // tm,))
    return dlat, reds


def _mla_q(cq, wq, wukp, cos_k, sin_k, tm):
    s = cq.shape[0]

    def body(cq_ref, wq_ref, wuk_ref, cos_ref, sin_ref, o_ref):
        aq = _dot(cq_ref[...], wq_ref[...], NN)
        qlat = _dot(aq.astype(BF16), wuk_ref[...], NN)
        roped = _rope(aq[:, KV_LORA:], cos_ref[...], sin_ref[...])
        o_ref[...] = (jnp.concatenate([qlat[:, :KV_LORA], roped], axis=1) * ATTN_SCALE).astype(o_ref.dtype)

    wblk = pl.BlockSpec((None, QK_PAD, QK_PAD), lambda h, i: (h, 0, 0))
    tblk = pl.BlockSpec((tm, KV_LORA), lambda h, i: (i, 0))
    return pl.pallas_call(
        body, name="mla_q", grid=(N_HEADS, s // tm),
        in_specs=[pl.BlockSpec((tm, Q_LORA), lambda h, i: (i, 0)), wblk, wblk, tblk, tblk],
        out_specs=pl.BlockSpec((None, tm, QK_PAD), lambda h, i: (h, i, 0)), out_shape=_sds((N_HEADS, s, QK_PAD), BF16),
        compiler_params=_params(),
    )(cq, wq, wukp, cos_k, sin_k)


def _mla_q_bwd(cq, wq, wukp, cos_k, sin_k, dqcat, tm):
    s = cq.shape[0]

    def body(cq_ref, wq_ref, wuk_ref, cos_ref, sin_ref, dq_ref, dcq_ref, dwq_ref, dwuk_ref):
        h, i = pl.program_id(0), pl.program_id(1)
        cqv = cq_ref[...]
        aq = _dot(cqv, wq_ref[...], NN).astype(BF16)
        g = dq_ref[...].astype(F32) * ATTN_SCALE
        gl, gr = g[:, :KV_LORA], g[:, KV_LORA:]
        dqlat = jnp.concatenate([gl, jnp.zeros_like(gl)], axis=1).astype(BF16)
        d_rope = _rope_t(gr, cos_ref[...], sin_ref[...])
        daq = _dot(dqlat, wuk_ref[...], NT) + jnp.concatenate([jnp.zeros_like(d_rope), d_rope], axis=1)
        daq_b = daq.astype(BF16)
        d_wuk = _dot(aq, dqlat, TN)
        d_wq = _dot(cqv, daq_b, TN)
        d_cq = _dot(daq_b, wq_ref[...], NT)
        rows = pl.ds(pl.multiple_of(i * tm, tm), tm)

        @pl.when(i == 0)
        def _():
            dwq_ref[...] = d_wq
            dwuk_ref[...] = d_wuk

        @pl.when(i != 0)
        def _():
            dwq_ref[...] += d_wq
            dwuk_ref[...] += d_wuk

        @pl.when(h == 0)
        def _():
            dcq_ref[rows, :] = d_cq

        @pl.when(h != 0)
        def _():
            dcq_ref[rows, :] += d_cq

    wblk = pl.BlockSpec((None, QK_PAD, QK_PAD), lambda h, i: (h, 0, 0))
    tblk = pl.BlockSpec((tm, KV_LORA), lambda h, i: (i, 0))
    return pl.pallas_call(
        body, name="mla_q_bwd", grid=(N_HEADS, s // tm),
        in_specs=[pl.BlockSpec((tm, Q_LORA), lambda h, i: (i, 0)), wblk, wblk, tblk, tblk,
                  pl.BlockSpec((None, tm, QK_PAD), lambda h, i: (h, i, 0))],
        out_specs=[pl.BlockSpec((s, Q_LORA), lambda h, i: (0, 0)), wblk, wblk],
        out_shape=[_sds((s, Q_LORA), F32), _sds((N_HEADS, QK_PAD, QK_PAD), F32), _sds((N_HEADS, QK_PAD, QK_PAD), F32)],
        compiler_params=_params(),
    )(cq, wq, wukp, cos_k, sin_k, dqcat)


def _flash_fwd(qcat, kcat, tq, tk):
    n_h, s, _ = qcat.shape
    n_k = s // tk

    def body(q_ref, k_ref, v_ref, o_ref, lse_ref):
        q = q_ref[...]

        def step(kk, carry):
            m, l, acc = carry
            rows = pl.ds(pl.multiple_of(kk * tk, tk), tk)
            sc = _dot(q, k_ref[rows, :], NT)
            m_new = jnp.maximum(m, jnp.max(sc, axis=1, keepdims=True))
            alpha = jnp.exp(m - m_new)
            p = jnp.exp(sc - m_new)
            l = alpha * l + jnp.sum(p, axis=1, keepdims=True)
            acc = alpha * acc + _dot(p.astype(BF16), v_ref[rows, :], NN)
            return m_new, l, acc

        init = (jnp.full((tq, 1), -1e30, F32), jnp.zeros((tq, 1), F32), jnp.zeros((tq, KV_LORA), F32))
        m, l, acc = lax.fori_loop(0, n_k, step, init)
        o_ref[...] = (acc / l).astype(o_ref.dtype)
        lse_ref[...] = m + jnp.log(l)

    return pl.pallas_call(
        body, name="mla_attn", grid=(n_h, s // tq),
        in_specs=[pl.BlockSpec((None, tq, QK_PAD), lambda h, i: (h, i, 0)), pl.BlockSpec((s, QK_PAD), lambda h, i: (0, 0)),
                  pl.BlockSpec((s, KV_LORA), lambda h, i: (0, 0))],
        out_specs=[pl.BlockSpec((None, tq, KV_LORA), lambda h, i: (h, i, 0)), pl.BlockSpec((None, tq, 1), lambda h, i: (h, i, 0))],
        out_shape=[_sds((n_h, s, KV_LORA), BF16), _sds((n_h, s, 1), F32)], compiler_params=_params(),
    )(qcat, kcat, kcat)


def _flash_bwd(qcat, kcat, o, do, lse, tq, tk):
    n_h, s, _ = qcat.shape
    n_k = s // tk

    def body(q_ref, k_ref, v_ref, o_ref, do_ref, lse_ref, dq_ref, dk_ref, dv_ref, dq_acc):
        h, i = pl.program_id(0), pl.program_id(1)

        @pl.when(jnp.logical_and(h == 0, i == 0))
        def _():
            dk_ref[...] = jnp.zeros_like(dk_ref)
            dv_ref[...] = jnp.zeros_like(dv_ref)

        q = q_ref[...]
        dov = do_ref[...]
        lse_v = lse_ref[...]
        delta = jnp.sum(dov.astype(F32) * o_ref[...].astype(F32), axis=1, keepdims=True)
        dq_acc[...] = jnp.zeros_like(dq_acc)

        def step(kk, carry):
            rows = pl.ds(pl.multiple_of(kk * tk, tk), tk)
            k = k_ref[rows, :]
            p = jnp.exp(_dot(q, k, NT) - lse_v)
            dp = _dot(dov, v_ref[rows, :], NT)
            ds = (p * (dp - delta)).astype(BF16)
            dq_acc[...] += _dot(ds, k, NN)
            dv_ref[rows, :] += _dot(p.astype(BF16), dov, TN)
            dk_ref[rows, :] += _dot(ds, q, TN)
            return carry

        lax.fori_loop(0, n_k, step, 0)
        dq_ref[...] = dq_acc[...].astype(dq_ref.dtype)

    qblk = pl.BlockSpec((None, tq, QK_PAD), lambda h, i: (h, i, 0))
    oblk = pl.BlockSpec((None, tq, KV_LORA), lambda h, i: (h, i, 0))
    return pl.pallas_call(
        body, name="mla_attn_bwd", grid=(n_h, s // tq),
        in_specs=[qblk, pl.BlockSpec((s, QK_PAD), lambda h, i: (0, 0)), pl.BlockSpec((s, KV_LORA), lambda h, i: (0, 0)), oblk, oblk,
                  pl.BlockSpec((None, tq, 1), lambda h, i: (h, i, 0))],
        out_specs=[qblk, pl.BlockSpec((s, QK_PAD), lambda h, i: (0, 0)), pl.BlockSpec((s, KV_LORA), lambda h, i: (0, 0))],
        out_shape=[_sds((n_h, s, QK_PAD), BF16), _sds((s, QK_PAD), F32), _sds((s, KV_LORA), F32)],
        scratch_shapes=[pltpu.VMEM((tq, QK_PAD), F32)], compiler_params=_params(),
    )(qcat, kcat, kcat, o, do, lse)


def _mla_fwd(h, wts, cos_k, sin_k, tm):
    s, d = h.shape
    nt = s // tm
    lat = _matmul("mla_lat", h, wts["w_in"], a_blk=(tm, d), a_map=lambda i, k: (i, 0), b_blk=(d, LAT_PAD), b_map=lambda i, k: (0, 0),
                  o_shape=(s, LAT_PAD), o_blk=(tm, LAT_PAD), o_map=lambda i, k: (i, 0), grid=(nt, 1), contract=NN, out_dtype=F32)
    cq, kcat = _mla_mid(lat, wts["q_norm"], wts["kv_norm"], cos_k, sin_k, tm)
    qcat = _mla_q(cq, wts["wq"], wts["wukp"], cos_k, sin_k, tm)
    o_lat, lse = _flash_fwd(qcat, kcat, tm, tm)
    o = _matmul("mla_uv", o_lat, wts["wuv2"], a_blk=(None, tm, KV_LORA), a_map=lambda i, p, r: (2 * p + r, i, 0),
                b_blk=(None, KV_LORA, 2 * V_HEAD), b_map=lambda i, p, r: (2 * p + r, 0, 0), o_shape=(s, d), o_blk=(tm, 2 * V_HEAD),
                o_map=lambda i, p, r: (i, p), grid=(nt, N_HEADS // 2, 2), contract=NN, out_dtype=BF16)
    u = _matmul("mla_out", o, wts["w_o"], a_blk=(tm, d), a_map=lambda i, k: (i, 0), b_blk=(d, d), b_map=lambda i, k: (0, 0),
                o_shape=(s, d), o_blk=(tm, d), o_map=lambda i, k: (i, 0), grid=(nt, 1), contract=NN, out_dtype=F32)
    return u, (lat, cq, kcat, qcat, o_lat, lse, o)


def _mla_bwd(du, h, saved, wts, cos_k, sin_k, tm):
    lat, cq, kcat, qcat, o_lat, lse, o = saved
    s, d = h.shape
    nt = s // tm
    do = _matmul("mla_do", du, wts["w_o"], a_blk=(tm, d), a_map=lambda i, k: (i, 0), b_blk=(d, d), b_map=lambda i, k: (0, 0),
                 o_shape=(s, d), o_blk=(tm, d), o_map=lambda i, k: (i, 0), grid=(nt, 1), contract=NT, out_dtype=BF16)
    dw_o = _matmul("mla_dwo", o, du, a_blk=(tm, d), a_map=lambda k: (k, 0), b_blk=(tm, d), b_map=lambda k: (k, 0),
                   o_shape=(d, d), o_blk=(d, d), o_map=lambda k: (0, 0), grid=(nt,), contract=TN, out_dtype=F32)
    do_lat = _matmul("mla_dolat", do, wts["wuv2"], a_blk=(tm, 2 * V_HEAD), a_map=lambda hh, i, k: (i, hh // 2),
                     b_blk=(None, KV_LORA, 2 * V_HEAD), b_map=lambda hh, i, k: (hh, 0, 0), o_shape=(N_HEADS, s, KV_LORA),
                     o_blk=(None, tm, KV_LORA), o_map=lambda hh, i, k: (hh, i, 0), grid=(N_HEADS, nt, 1), contract=NT, out_dtype=BF16)
    dwuv2 = _matmul("mla_dwuv", o_lat, do, a_blk=(None, tm, KV_LORA), a_map=lambda hh, k: (hh, k, 0), b_blk=(tm, 2 * V_HEAD),
                    b_map=lambda hh, k: (k, hh // 2), o_shape=(N_HEADS, KV_LORA, 2 * V_HEAD), o_blk=(None, KV_LORA, 2 * V_HEAD),
                    o_map=lambda hh, k: (hh, 0, 0), grid=(N_HEADS, nt), contract=TN, out_dtype=F32)
    dqcat, dkcat, dv = _flash_bwd(qcat, kcat, o_lat, do_lat, lse, tm, tm)
    dcq, dwq, dwukp = _mla_q_bwd(cq, wts["wq"], wts["wukp"], cos_k, sin_k, dqcat, tm)
    dlat, (dqn, dkn) = _mla_mid_bwd(lat, dcq, dkcat, dv, wts["q_norm"], wts["kv_norm"], cos_k, sin_k, tm)
    dh = _matmul("mla_dh", dlat, wts["w_in"], a_blk=(tm, LAT_PAD), a_map=lambda i, k: (i, 0), b_blk=(d, LAT_PAD),
                 b_map=lambda i, k: (0, 0), o_shape=(s, d), o_blk=(tm, d), o_map=lambda i, k: (i, 0), grid=(nt, 1), contract=NT,
                 out_dtype=F32)
    dw_in = _matmul("mla_dwin", h, dlat, a_blk=(tm, d), a_map=lambda k: (k, 0), b_blk=(tm, LAT_PAD), b_map=lambda k: (k, 0),
                    o_shape=(d, LAT_PAD), o_blk=(d, LAT_PAD), o_map=lambda k: (0, 0), grid=(nt,), contract=TN, out_dtype=F32)
    return dh, dict(w_in=dw_in, wq=dwq, wukp=dwukp, wuv2=dwuv2, w_o=dw_o, q_norm=dqn, kv_norm=dkn)


def _adamw(name, parts, w, m, v):
    n_parts, r, c = parts.shape
    tr = r
    for cand in (256, 128, 64, 32, 16, 8):
        if r > cand and r % cand == 0:
            tr = cand
            break

    def body(p_ref, w_ref, m_ref, v_ref, g_ref, d_ref, nm_ref, nv_ref):
        g = p_ref[0].astype(F32)
        for k in range(1, n_parts):
            g = g + p_ref[k].astype(F32)
        nm = ADAM_B1 * m_ref[...] + (1.0 - ADAM_B1) * g
        nv = ADAM_B2 * v_ref[...] + (1.0 - ADAM_B2) * (g * g)
        m_hat = nm / (1.0 - ADAM_B1 ** ADAM_STEP)
        v_hat = nv / (1.0 - ADAM_B2 ** ADAM_STEP)
        g_ref[...] = g
        d_ref[...] = -ADAM_LR * (m_hat / (jnp.sqrt(v_hat) + ADAM_EPS) + ADAM_WD * w_ref[...])
        nm_ref[...] = nm
        nv_ref[...] = nv

    blk = pl.BlockSpec((tr, c), lambda i: (i, 0))
    return pl.pallas_call(
        body, name=name, grid=(r // tr,), in_specs=[pl.BlockSpec((n_parts, tr, c), lambda i: (0, i, 0)), blk, blk, blk],
        out_specs=[blk] * 4, out_shape=[_sds((r, c), F32)] * 4, compiler_params=_params(),
    )(parts, w, m, v)


def _adamw_slab(name, parts, w, m, v, bufs, f):
    n_parts, r, c = parts.shape
    tr = max(t for t in range(8, 257, 8) if r % t == 0)

    def body(p_ref, w_ref, m_ref, v_ref, *rest):
        g_ref, d_ref, nm_ref, nv_ref = rest[4:]
        g = p_ref[0].astype(F32)
        for k in range(1, n_parts):
            g = g + p_ref[k].astype(F32)
        nm = ADAM_B1 * m_ref[...] + (1.0 - ADAM_B1) * g
        nv = ADAM_B2 * v_ref[...] + (1.0 - ADAM_B2) * (g * g)
        m_hat = nm / (1.0 - ADAM_B1 ** ADAM_STEP)
        v_hat = nv / (1.0 - ADAM_B2 ** ADAM_STEP)
        g_ref[...] = g
        d_ref[...] = -ADAM_LR * (m_hat / (jnp.sqrt(v_hat) + ADAM_EPS) + ADAM_WD * w_ref[...])
        nm_ref[...] = nm
        nv_ref[...] = nv

    blk = pl.BlockSpec((None, tr, c), lambda i: (f, i, 0))
    return pl.pallas_call(
        body, name=name, grid=(r // tr,),
        in_specs=[pl.BlockSpec((n_parts, tr, c), lambda i: (0, i, 0)), blk, blk, blk] + [pl.BlockSpec(memory_space=pl.ANY)] * 4,
        out_specs=[blk] * 4, out_shape=[_sds(w.shape, F32)] * 4, input_output_aliases={4 + j: j for j in range(4)},
        compiler_params=_params(),
    )(parts, w, m, v, *bufs)


def _mesh_pos():
    return lax.axis_index("x"), lax.axis_index("y"), lax.axis_index("c")


def _flip(pos, mask):
    return tuple(1 - p if (mask >> (2 - b)) & 1 else p for b, p in enumerate(pos))


def _index(pos):
    return 4 * pos[0] + 2 * pos[1] + pos[2]


def _all_gather(name, xs):
    n = len(xs)

    def body(*refs):
        x_refs, o_refs = refs[:n], refs[n:2 * n]
        send_sems, recv_sems, local_sems = refs[2 * n:]
        me = _mesh_pos()
        sibling = _flip(me, 1)
        others = [_flip(me, 4), _flip(me, 2), _flip(me, 6)]

        def copy(k, j, block, to, src=None):
            dst = o_refs[k].at[_index(block)]
            return pltpu.make_async_remote_copy(
                src_ref=dst if src is None else src, dst_ref=dst, send_sem=send_sems.at[k, j], recv_sem=recv_sems.at[k, j],
                device_id=to, device_id_type=MESH)

        local = [pltpu.make_async_copy(x_refs[k], o_refs[k].at[_index(me)], local_sems.at[k]) for k in range(n)]
        for cp in local:
            cp.start()
        first = []
        for k in range(n):
            first.append(copy(k, 0, me, sibling, src=x_refs[k]))
            first += [copy(k, 1 + j, me, other, src=x_refs[k]) for j, other in enumerate(others)]
        for cp in first:
            cp.start()
        passed = []
        for j, other in enumerate(others):
            for k in range(n):
                copy(k, 1 + j, other, me).wait_recv()
                cp = copy(k, 4 + j, other, sibling)
                cp.start()
                passed.append(cp)
        for k in range(n):
            copy(k, 0, sibling, me).wait_recv()
        for j, other in enumerate(others):
            for k in range(n):
                copy(k, 4 + j, _flip(other, 1), me).wait_recv()
        for cp in first + passed:
            cp.wait_send()
        for cp in local:
            cp.wait()

    any_spec = pl.BlockSpec(memory_space=pl.ANY)
    return pl.pallas_call(
        body, name=name, in_specs=[any_spec] * n, out_specs=[any_spec] * n,
        out_shape=[_sds((N_DEV,) + x.shape, x.dtype) for x in xs],
        scratch_shapes=[pltpu.SemaphoreType.DMA((n, 7)), pltpu.SemaphoreType.DMA((n, 7)), pltpu.SemaphoreType.DMA((n,))],
    )(*xs)


def _all_to_all(name, groups):
    flat = [(gi, f) for gi, grp in enumerate(groups) for f in range(len(grp))]
    n = len(flat)
    n_groups = len(groups)

    def body(*refs):
        x_refs, o_refs = refs[:n], refs[n:n + n_groups]
        send_sems, recv_sems, local_sems = refs[n + n_groups:]
        me = _mesh_pos()
        local, sends, recvs = [], [], []
        for k, (gi, f) in enumerate(flat):
            local.append(pltpu.make_async_copy(x_refs[k].at[_index(me)], o_refs[gi].at[_index(me), f], local_sems.at[k]))
            for mask in range(1, N_DEV):
                peer = _flip(me, mask)
                sends.append(pltpu.make_async_remote_copy(
                    src_ref=x_refs[k].at[_index(peer)], dst_ref=o_refs[gi].at[_index(me), f], send_sem=send_sems.at[k, mask - 1],
                    recv_sem=recv_sems.at[k, mask - 1], device_id=peer, device_id_type=MESH))
                recvs.append(pltpu.make_async_remote_copy(
                    src_ref=x_refs[k].at[_index(me)], dst_ref=o_refs[gi].at[_index(peer), f], send_sem=send_sems.at[k, mask - 1],
                    recv_sem=recv_sems.at[k, mask - 1], device_id=peer, device_id_type=MESH))
        for cp in local + sends:
            cp.start()
        for cp in recvs:
            cp.wait_recv()
        for cp in sends:
            cp.wait_send()
        for cp in local:
            cp.wait()

    any_spec = pl.BlockSpec(memory_space=pl.ANY)
    return pl.pallas_call(
        body, name=name, in_specs=[any_spec] * n, out_specs=[any_spec] * n_groups,
        out_shape=[_sds((N_DEV, len(grp)) + grp[0].shape[1:], grp[0].dtype) for grp in groups],
        scratch_shapes=[pltpu.SemaphoreType.DMA((n, 7)), pltpu.SemaphoreType.DMA((n, 7)), pltpu.SemaphoreType.DMA((n,))],
    )(*[a for grp in groups for a in grp])


def _split_copies(kind, outgoing, x_refs, land_refs, send_sems, recv_sems):
    me = _mesh_pos()
    copies = []
    for k, (x_ref, land_ref) in enumerate(zip(x_refs, land_refs)):
        for mask in range(1, N_DEV):
            peer = _flip(me, mask)
            sem = k * (N_DEV - 1) + mask - 1
            copies.append(pltpu.make_async_remote_copy(
                src_ref=x_ref if kind == "gather" else x_ref.at[_index(peer)], dst_ref=land_ref.at[_index(me if outgoing else peer)],
                send_sem=send_sems.at[sem], recv_sem=recv_sems.at[sem], device_id=peer, device_id_type=MESH))
    return copies


_HBM_SPEC = pl.BlockSpec(memory_space=pltpu.HBM)
_SEM_SPEC = pl.BlockSpec(memory_space=pltpu.SEMAPHORE)
_EFFECT = pltpu.SideEffectType.DATAFLOW_SIDE_EFFECTING


def _split_start(name, kind, xs):
    n = len(xs)
    lands = [lax.empty(((N_DEV,) + x.shape) if kind == "gather" else x.shape, x.dtype) for x in xs]

    def body(*refs):
        x_refs, land_refs = refs[:n], refs[n:2 * n]
        send_sems, recv_sems = refs[2 * n], refs[2 * n + 1]
        token = refs[-1]
        for cp in _split_copies(kind, True, x_refs, land_refs, send_sems, recv_sems):
            cp.start()
        token[...] = jnp.zeros_like(token)

    hbm = [pltpu.HBM(a.shape, a.dtype) for a in list(xs) + lands]
    res = pl.pallas_call(
        body, name=name,
        out_shape=[pltpu.SemaphoreType.DMA((n * (N_DEV - 1),)), pltpu.SemaphoreType.DMA((n * (N_DEV - 1),))] + hbm + [_sds((8, 128), F32)],
        in_specs=[_HBM_SPEC] * (2 * n), out_specs=[_SEM_SPEC, _SEM_SPEC] + [_HBM_SPEC] * (2 * n) + [pl.BlockSpec(memory_space=pltpu.VMEM)],
        input_output_aliases={j: 2 + j for j in range(2 * n)}, compiler_params=pltpu.CompilerParams(has_side_effects=_EFFECT),
    )(*[pltpu.with_memory_space_constraint(a, pltpu.HBM) for a in list(xs) + lands])
    return (kind, n, res[0], res[1], res[2:2 + 2 * n]), res[-1]


def _split_wait(name, state, after):
    kind, n, send_sems_in, recv_sems_in, thru = state

    def body(*refs):
        x_refs, land_refs = refs[:n], refs[n:2 * n]
        send_sems, recv_sems = refs[2 * n], refs[2 * n + 1]
        for cp in _split_copies(kind, True, x_refs, land_refs, send_sems, recv_sems):
            cp.wait_send()
        for cp in _split_copies(kind, False, x_refs, land_refs, send_sems, recv_sems):
            cp.wait_recv()

    res = pl.pallas_call(
        body, name=name, out_shape=[pltpu.HBM(a.shape, a.dtype) for a in thru],
        in_specs=[_HBM_SPEC] * (2 * n) + [_SEM_SPEC, _SEM_SPEC, pl.BlockSpec(memory_space=pl.ANY)], out_specs=[_HBM_SPEC] * (2 * n),
        input_output_aliases={j: j for j in range(2 * n)}, compiler_params=pltpu.CompilerParams(has_side_effects=_EFFECT),
    )(*thru, send_sems_in, recv_sems_in, after)
    return res[n:]


def _with_own(land, own):
    me = _index(_mesh_pos())
    return lax.dynamic_update_slice(land, own[None], (me,) + (0,) * own.ndim)


def _rope_tables(s):
    inv = 1.0 / (ROPE_THETA ** (jnp.arange(0, QK_ROPE, 2, dtype=F32) / QK_ROPE))
    ang = jnp.arange(s, dtype=F32)[:, None] * inv[None, :]
    pad = jnp.zeros((s, KV_LORA - QK_ROPE), F32)
    return (jnp.concatenate([jnp.cos(ang), jnp.cos(ang), pad], axis=1), jnp.concatenate([jnp.sin(ang), jnp.sin(ang), pad], axis=1))


def _row_of(v):
    return v.reshape(1, -1)


def kernel(x, c, ada_w, ada_b, norm_g, ffn_w_in, ffn_w_out, pool_w, pool_b, pool_scale, mla_w_in, mla_q_norm, mla_kv_norm, mla_w_uq, mla_w_uk, mla_w_uv, mla_w_o, loss_target, m_ada_w, m_ada_b, m_norm_g, m_ffn_w_in, m_ffn_w_out, m_pool_w, m_pool_b, m_pool_scale, m_mla_w_in, m_mla_q_norm, m_mla_kv_norm, m_mla_w_uq, m_mla_w_uk, m_mla_w_uv, m_mla_w_o, v_ada_w, v_ada_b, v_norm_g, v_ffn_w_in, v_ffn_w_out, v_pool_w, v_pool_b, v_pool_scale, v_mla_w_in, v_mla_q_norm, v_mla_kv_norm, v_mla_w_uq, v_mla_w_uk, v_mla_w_uv, v_mla_w_o):
    s, d = x.shape[1], x.shape[2]
    tm = min(512, s)
    tr = min(256, s)
    me = 4 * lax.axis_index("x") + 2 * lax.axis_index("y") + lax.axis_index("c")
    x0 = x.reshape(s, d)
    target = loss_target.reshape(s, d)
    n_mod = ada_w.shape[2] * N_DEV // d
    mod_blk = ada_w.shape[2]

    small = jnp.concatenate([c.reshape(-1), norm_g.reshape(-1), pool_b.reshape(-1), mla_q_norm.reshape(-1)]).reshape(1, -1)
    w_in_loc = [ffn_w_in[i, f].astype(BF16) for i in range(2) for f in range(2)]
    w_out_loc = [ffn_w_out[i, f].astype(BF16) for i in range(2) for f in range(2)]
    small_all, w_in_first, w_out_first = _all_gather("gather_first", [small, w_in_loc[0], w_out_loc[0]])
    small_all = small_all.reshape(N_DEV, -1)
    later_a = [w_in_loc[1], w_out_loc[1], pool_w.reshape(-1, POOL_GROUP).astype(BF16)]
    later_b = [w_in_loc[2], w_out_loc[2], w_in_loc[3], w_out_loc[3], mla_w_in[0].astype(BF16),
               mla_w_uq.reshape(mla_w_uq.shape[1], -1).astype(BF16), mla_w_o[0].astype(BF16)]
    state_a, token_a = _split_start("gather_start_a", "gather", later_a)
    state_b, token_b = _split_start("gather_start_b", "gather", later_b)
    w_in8 = [w_in_first, None, None, None]
    w_out4 = [w_out_first.reshape(4, FF_BLK, d), None, None, None]

    c_all = small_all[:, :d] + (token_a[0, 0] + token_b[0, 0])
    off = d
    g_all = small_all[:, off:off + 12 * (d // N_DEV)].reshape(N_DEV, 2, 6, d // N_DEV).transpose(1, 2, 0, 3).reshape(2, 6, d)
    off += 12 * (d // N_DEV)
    pool_b_all = small_all[:, off:off + 4 * 32].reshape(N_DEV, 4, 32).transpose(1, 0, 2).reshape(1, d)
    off += 4 * 32
    q_norm_all = small_all[:, off:off + 32].reshape(1, Q_LORA)
    kv_norm_row = mla_kv_norm.reshape(1, KV_LORA)
    pscale_row = pool_scale.reshape(1, d)

    even = (jnp.arange(N_HEADS) % 2 == 0)[:, None, None]
    cos_k, sin_k = _rope_tables(s)

    def mla_weights(mla_w_in_all, mla_w_uq_all, mla_w_o_all):
        uq = mla_w_uq_all.reshape(Q_LORA, N_HEADS, QK_NOPE + QK_ROPE).transpose(1, 0, 2)
        zq = jnp.zeros((N_HEADS, Q_LORA, QK_NOPE), BF16)
        wq = jnp.concatenate(
            [uq[:, :, :QK_NOPE], zq, uq[:, :, QK_NOPE:], jnp.zeros((N_HEADS, Q_LORA, QK_PAD - KV_LORA - QK_ROPE), BF16)], axis=2)
        wukp = jnp.pad(mla_w_uk[0].transpose(1, 2, 0).astype(BF16), ((0, 0), (0, QK_PAD - QK_NOPE), (0, QK_PAD - KV_LORA)))
        uv = mla_w_uv[0].transpose(1, 0, 2).astype(BF16)
        wuv2 = jnp.where(even, jnp.concatenate([uv, jnp.zeros_like(uv)], axis=2), jnp.concatenate([jnp.zeros_like(uv), uv], axis=2))
        return dict(w_in=jnp.pad(mla_w_in_all.reshape(d, -1), ((0, 0), (0, LAT_PAD - mla_w_in.shape[2]))), wq=wq, wukp=wukp,
                    wuv2=wuv2, w_o=mla_w_o_all.reshape(d, d), q_norm=q_norm_all, kv_norm=kv_norm_row)

    (sc_all,), _ = _rowmap("ada_silu", lambda cv: ((cv * jax.nn.sigmoid(cv),), ()), [(c_all, (N_DEV, d), lambda i: (0, 0))],
                           [(_sds((N_DEV, d), F32), (N_DEV, d), lambda i: (0, 0))], [], (1,))
    ada_b_loc = lax.dynamic_slice_in_dim(ada_b, me * mod_blk, mod_blk, axis=1).reshape(2, 1, mod_blk)
    m_pad = 2 * N_DEV
    modp = _matmul("ada_mod", jnp.pad(sc_all, ((0, m_pad - N_DEV), (0, 0))), ada_w, a_blk=(m_pad, d), a_map=lambda i, k: (0, 0),
                   b_blk=(None, d, mod_blk), b_map=lambda i, k: (i, 0, 0), o_shape=(2, m_pad, mod_blk), o_blk=(None, m_pad, mod_blk),
                   o_map=lambda i, k: (i, 0, 0), grid=(2, 1), contract=NN, out_dtype=F32, bias=ada_b_loc, bias_blk=(None, 1, mod_blk),
                   bias_map=lambda i, k: (i, 0, 0))[:, :N_DEV]
    (modp_all,) = _all_gather("gather_mod", [modp.reshape(2 * N_DEV, mod_blk)])
    mod = lax.dynamic_index_in_dim(modp_all.reshape(N_DEV, 2, N_DEV, mod_blk), me, axis=2, keepdims=False)
    mod = mod.transpose(1, 0, 2).reshape(2, n_mod, d)

    saved = []
    xs = x0
    w4 = mla_wts = None
    for i in range(2):
        for sub in range(3):
            if (i, sub) == (0, 1):
                lands = _split_wait("gather_wait_a", state_a, xs)
                w_in8[1] = _with_own(lands[0], later_a[0])
                w_out4[1] = _with_own(lands[1], later_a[1]).reshape(4, FF_BLK, d)
                w4 = _with_own(lands[2], later_a[2]).reshape(N_DEV, 4, 32, POOL_GROUP).transpose(1, 0, 2, 3)
                w4 = w4.reshape(4, POOL_GROUP, POOL_GROUP)
            if (i, sub) == (1, 0):
                lands = [_with_own(land, own) for land, own in zip(_split_wait("gather_wait_b", state_b, xs), later_b)]
                w_in8[2], w_in8[3] = lands[0], lands[2]
                w_out4[2], w_out4[3] = lands[1].reshape(4, FF_BLK, d), lands[3].reshape(4, FF_BLK, d)
                mla_wts = mla_weights(*lands[4:])
            shift, scale, gate = (_row_of(mod[i, 3 * sub + j]) for j in range(3))
            g_pre, g_post = _row_of(g_all[i, 2 * sub]), _row_of(g_all[i, 2 * sub + 1])
            tag = f"{i}{sub}"
            if sub != 1:
                f = sub // 2
                h = _prenorm(f"prenorm_{tag}", xs, g_pre, scale, shift, BF16, tr)
                u, extra = _ffn_fwd(tag, h, w_in8[2 * i + f], w_out4[2 * i + f], tm)
                weight = 0.5
            elif i == 0:
                h = _prenorm(f"prenorm_{tag}", xs, g_pre, scale, shift, F32, tr)
                u, z, v = _pool_fwd(h, w4, pool_b_all, pscale_row, tm)
                extra = (z, v)
                weight = 1.0
            else:
                h = _prenorm(f"prenorm_{tag}", xs, g_pre, scale, shift, BF16, tr)
                u, extra = _mla_fwd(h, mla_wts, cos_k, sin_k, tm)
                weight = 1.0
            saved.append((xs, h, u, extra, (shift, scale, gate, g_pre, g_post), weight))
            xs = _postnorm(f"postnorm_{tag}", xs, u, g_post, gate, weight, tr)

    def loss_fn(yv, tv):
        e = yv - tv
        return (e * (1.0 / d),), (_colsum(e * e),)

    (dx,), (sq,) = _rowmap("loss", loss_fn, [_tile(xs, tr), _tile(target, tr)], [_otile(s, d, F32, tr)], [_ored(d)], (s // tr,))
    loss = lax.psum(0.5 * jnp.sum(sq) / d, AXES)

    d_mod = [[None] * n_mod for _ in range(2)]
    d_g = [[None] * 6 for _ in range(2)]
    sent = {}
    pool_grads = mla_grads = None

    def start_scatter(key, arrays):
        state, token = _split_start(f"scatter_start_{key}", "scatter", arrays)
        sent[key] = (state, arrays)
        return token[0, 0]

    for i in (1, 0):
        for sub in (2, 1, 0):
            xin, h, u, extra, (shift, scale, gate, g_pre, g_post), weight = saved[3 * i + sub]
            tag = f"{i}{sub}"
            du, (dgate, dgpost) = _postnorm_bwd(f"postnorm_bwd_{tag}", dx, u, g_post, gate, weight, F32 if (sub == 1 and i == 0) else BF16, tr)
            if sub != 1:
                k = 2 * i + sub // 2
                dgu, dw_in8, dw_out4 = _ffn_bwd(tag, du, h, extra, w_in8[k], w_out4[k], tm)
                arrays = [dw_in8, dw_out4.reshape(N_DEV, FF_BLK // 2, d)]
                if k == 0:
                    arrays.append(pool_grads[0].reshape(4, N_DEV, 32, POOL_GROUP).transpose(1, 0, 2, 3).reshape(N_DEV, 4 * 32, POOL_GROUP))
                g_pre = g_pre + start_scatter(tag, arrays)
                dh = _ffn_dh(tag, dgu, w_in8[k], tm)
            elif i == 0:
                dh, dw4, (dpscale, dpb) = _pool_bwd(du, extra[0], extra[1], w4, pscale_row, tm)
                pool_grads = (dw4, dpscale, dpb)
            else:
                dh, mla_grads = _mla_bwd(du, h, extra, mla_wts, cos_k, sin_k, tm)
                dwq = mla_grads["wq"]
                d_uq = jnp.concatenate([dwq[:, :, :QK_NOPE], dwq[:, :, KV_LORA:KV_LORA + QK_ROPE]], axis=2).transpose(1, 0, 2)
                g_pre = g_pre + start_scatter("mla", [mla_grads["w_in"][:, :mla_w_in.shape[2]].reshape(N_DEV, d // N_DEV, -1),
                                                      d_uq.reshape(N_DEV, Q_LORA // N_DEV, -1),
                                                      mla_grads["w_o"].reshape(N_DEV, d // N_DEV, d)])
            dx, (dshift, dscale, dgpre) = _prenorm_bwd(f"prenorm_bwd_{tag}", dh, xin, dx, g_pre, scale, tr)
            d_mod[i][3 * sub:3 * sub + 3] = [dshift, dscale, dgate]
            d_g[i][2 * sub:2 * sub + 2] = [dgpre, dgpost]
    grad_x = dx.reshape(x.shape)

    dw4, dpscale, dpb = pool_grads
    dwukp, dwuv2 = mla_grads["wukp"], mla_grads["wuv2"]
    d_uk = dwukp[:, :QK_NOPE, :KV_LORA].transpose(2, 0, 1).reshape(KV_LORA, -1)
    d_uv = jnp.where(even, dwuv2[:, :, :V_HEAD], dwuv2[:, :, V_HEAD:]).transpose(1, 0, 2).reshape(KV_LORA, -1)
    d_mod_row = jnp.concatenate([jnp.concatenate(r, axis=1) for r in d_mod], axis=1)
    d_g_row = jnp.concatenate([jnp.concatenate(r, axis=1) for r in d_g], axis=1)
    small_g = jnp.concatenate([d_mod_row, d_g_row, dpb, dpscale, mla_grads["q_norm"], mla_grads["kv_norm"]], axis=1)
    small_g_all, uk_all, uv_all = _all_gather("gather_small_grads", [small_g, d_uk, d_uv])
    small_g_all = small_g_all.reshape(N_DEV, -1)
    n_m = 2 * n_mod * d
    d_mod_all = small_g_all[:, :n_m].reshape(N_DEV, 2, n_mod * d)
    rest = small_g_all[:, n_m:]
    p_norm_g = lax.dynamic_slice_in_dim(rest[:, :12 * d].reshape(N_DEV, 12, d), me * (d // N_DEV), d // N_DEV, axis=2)
    p_pool_b = lax.dynamic_slice_in_dim(rest[:, 12 * d:13 * d].reshape(N_DEV, 4, POOL_GROUP), me * 32, 32, axis=2)
    p_pool_scale = rest[:, 13 * d:14 * d].reshape(N_DEV, 1, d)
    p_q_norm = lax.dynamic_slice_in_dim(rest[:, 14 * d:14 * d + Q_LORA], me * 32, 32, axis=1).reshape(N_DEV, 1, 32)
    p_kv_norm = rest[:, 14 * d + Q_LORA:].reshape(N_DEV, 1, KV_LORA)

    d_mod_loc = lax.dynamic_slice_in_dim(d_mod_all, me * mod_blk, mod_blk, axis=2).transpose(1, 0, 2)
    k_pad = 128
    sc_t = jnp.pad(sc_all.T, ((0, 0), (0, k_pad - N_DEV)))
    d_ada_w = _matmul("ada_dw", sc_t, jnp.pad(d_mod_loc, ((0, 0), (0, k_pad - N_DEV), (0, 0))), a_blk=(d, k_pad),
                      a_map=lambda i, k: (0, 0), b_blk=(None, k_pad, mod_blk), b_map=lambda i, k: (i, 0, 0), o_shape=(2, d, mod_blk),
                      o_blk=(None, d, mod_blk), o_map=lambda i, k: (i, 0, 0), grid=(2, 1), contract=NN, out_dtype=F32)

    def upd(name, parts, w, m, v):
        shape = w.shape
        r, cdim = parts.shape[1], parts.shape[2]
        return [o.reshape(shape) for o in _adamw(name, parts, w.reshape(r, cdim), m.reshape(r, cdim), v.reshape(r, cdim))]

    def landed(key, after):
        state, arrays = sent[key]
        lands = _split_wait(f"scatter_wait_{key}", state, after)
        return [_with_own(land, lax.dynamic_index_in_dim(a, me, 0, keepdims=False)) for land, a in zip(lands, arrays)]

    res = {}
    res["ada_w"] = upd("adam_ada_w", d_ada_w.reshape(1, 2 * d, mod_blk), ada_w, m_ada_w, v_ada_w)
    res["ada_b"] = upd("adam_ada_b", d_mod_all.reshape(N_DEV, 2, n_mod * d), ada_b, m_ada_b, v_ada_b)
    res["norm_g"] = upd("adam_norm_g", p_norm_g, norm_g, m_norm_g, v_norm_g)
    res["pool_b"] = upd("adam_pool_b", p_pool_b, pool_b, m_pool_b, v_pool_b)
    res["pool_scale"] = upd("adam_pool_scale", p_pool_scale, pool_scale, m_pool_scale, v_pool_scale)
    res["mla_q_norm"] = upd("adam_mla_q_norm", p_q_norm, mla_q_norm, m_mla_q_norm, v_mla_q_norm)
    res["mla_kv_norm"] = upd("adam_mla_kv_norm", p_kv_norm, mla_kv_norm, m_mla_kv_norm, v_mla_kv_norm)
    res["mla_w_uk"] = upd("adam_mla_w_uk", uk_all, mla_w_uk, m_mla_w_uk, v_mla_w_uk)
    res["mla_w_uv"] = upd("adam_mla_w_uv", uv_all, mla_w_uv, m_mla_w_uv, v_mla_w_uv)

    w_in_s, m_in_s, v_in_s = (a.reshape(4, d, FF_BLK) for a in (ffn_w_in, m_ffn_w_in, v_ffn_w_in))
    w_out_s, m_out_s, v_out_s = (a.reshape(4, FF_BLK // 2, d) for a in (ffn_w_out, m_ffn_w_out, v_ffn_w_out))
    bufs_in = [lax.empty(w_in_s.shape, F32) for _ in range(4)]
    bufs_out = [lax.empty(w_out_s.shape, F32) for _ in range(4)]
    for key, k in (("12", 3), ("mla", None), ("10", 2), ("02", 1), ("00", 0)):
        parts = landed(key, grad_x if key != "00" else res["ada_w"][1])
        if k is None:
            res["mla_w_in"] = upd("adam_mla_w_in", parts[0], mla_w_in, m_mla_w_in, v_mla_w_in)
            res["mla_w_uq"] = upd("adam_mla_w_uq", parts[1], mla_w_uq, m_mla_w_uq, v_mla_w_uq)
            res["mla_w_o"] = upd("adam_mla_w_o", parts[2], mla_w_o, m_mla_w_o, v_mla_w_o)
            continue
        bufs_in = _adamw_slab(f"adam_ffn_w_in_{key}", parts[0], w_in_s, m_in_s, v_in_s, bufs_in, k)
        bufs_out = _adamw_slab(f"adam_ffn_w_out_{key}", parts[1], w_out_s, m_out_s, v_out_s, bufs_out, k)
        if k == 0:
            res["pool_w"] = upd("adam_pool_w", parts[2], pool_w, m_pool_w, v_pool_w)
    res["ffn_w_in"] = [b.reshape(ffn_w_in.shape) for b in bufs_in]
    res["ffn_w_out"] = [b.reshape(ffn_w_out.shape) for b in bufs_out]

    order = ["ada_w", "ada_b", "norm_g", "ffn_w_in", "ffn_w_out", "pool_w", "pool_b", "pool_scale", "mla_w_in", "mla_q_norm",
             "mla_kv_norm", "mla_w_uq", "mla_w_uk", "mla_w_uv", "mla_w_o"]
    outs = [loss, grad_x]
    for j in range(4):
        outs += [res[name][j] for name in order]
    return tuple(outs)
```

```python
import functools

import jax
import jax.numpy as jnp
from jax import lax
from jax.experimental import pallas as pl
from jax.experimental.pallas import tpu as pltpu

F32 = jnp.float32
BF16 = jnp.bfloat16
N_DEV = 8
AXES = ("x", "y", "c")
MESH = pl.DeviceIdType.MESH

D_MODEL = 1024
N_HEADS = 16
QK_NOPE = 64
QK_ROPE = 32
V_HEAD = 64
Q_LORA = 256
KV_LORA = 128
LAT_PAD = 512
QK_PAD = 256
D_FF = 2816
FF_BLK = 2 * D_FF // N_DEV
POOL_WINDOWS = (2, 4, 8, 16)
POOL_GROUP = 256
ROPE_THETA = 10000.0
EPS = 1e-6
ATTN_SCALE = (QK_NOPE + QK_ROPE) ** -0.5
ADAM_LR, ADAM_B1, ADAM_B2, ADAM_EPS, ADAM_WD, ADAM_STEP = 0.001, 0.9, 0.999, 1e-08, 0.01, 10
VMEM_LIMIT = 56 * 1024 * 1024

NN = ((1,), (0,))
NT = ((1,), (1,))
TN = ((0,), (0,))


def _params(**kw):
    return pltpu.CompilerParams(vmem_limit_bytes=VMEM_LIMIT, **kw)


def _dot(a, b, contract):
    return lax.dot_general(a, b, (contract, ((), ())), preferred_element_type=F32)


def _matmul(name, a, b, *, a_blk, a_map, b_blk, b_map, o_shape, o_blk, o_map, grid, contract, out_dtype,
            bias=None, bias_blk=None, bias_map=None):
    n_k = grid[-1]
    k_axis = len(grid) - 1
    acc_shape = tuple(d for d in o_blk if d is not None)

    def body(*refs):
        if bias is None:
            a_ref, b_ref, o_ref, acc = refs
        else:
            a_ref, b_ref, bias_ref, o_ref, acc = refs
        k = pl.program_id(k_axis)

        @pl.when(k == 0)
        def _():
            acc[...] = jnp.zeros_like(acc)

        acc[...] += _dot(a_ref[...].astype(BF16), b_ref[...].astype(BF16), contract)

        @pl.when(k == n_k - 1)
        def _():
            r = acc[...]
            if bias is not None:
                r = r + bias_ref[...]
            o_ref[...] = r.astype(o_ref.dtype)

    in_specs = [pl.BlockSpec(a_blk, a_map), pl.BlockSpec(b_blk, b_map)]
    args = [a, b]
    if bias is not None:
        in_specs.append(pl.BlockSpec(bias_blk, bias_map))
        args.append(bias)
    return pl.pallas_call(
        body, name=name, grid=grid, in_specs=in_specs, out_specs=pl.BlockSpec(o_blk, o_map),
        out_shape=jax.ShapeDtypeStruct(o_shape, out_dtype), scratch_shapes=[pltpu.VMEM(acc_shape, F32)],
        compiler_params=_params(),
    )(*args)


def _rowmap(name, fn, ins, outs, reds, grid):
    n_in, n_out, n_red = len(ins), len(outs), len(reds)

    def body(*refs):
        in_refs = refs[:n_in]
        out_refs = refs[n_in:n_in + n_out]
        red_refs = refs[n_in + n_out:]
        out_vals, red_vals = fn(*[r[...] for r in in_refs])
        for r, v in zip(out_refs, out_vals):
            r[...] = v.astype(r.dtype)
        if n_red:
            first = pl.program_id(0) == 0
            for ax in range(1, len(grid)):
                first = jnp.logical_and(first, pl.program_id(ax) == 0)

            @pl.when(first)
            def _():
                for r in red_refs:
                    r[...] = jnp.zeros_like(r)

            for r, v in zip(red_refs, red_vals):
                r[...] += v

    res = pl.pallas_call(
        body, name=name, grid=grid,
        in_specs=[pl.BlockSpec(blk, imap) for _, blk, imap in ins],
        out_specs=[pl.BlockSpec(blk, imap) for _, blk, imap in list(outs) + list(reds)],
        out_shape=[sds for sds, _, _ in list(outs) + list(reds)],
        compiler_params=_params(),
    )(*[a for a, _, _ in ins])
    return res[:n_out], res[n_out:]


def _sds(shape, dtype):
    return jax.ShapeDtypeStruct(shape, dtype)


def _tile(a, tm):
    return (a, (tm, a.shape[1]), lambda i: (i, 0))


def _row(a):
    return (a, (1, a.shape[1]), lambda i: (0, 0))


def _otile(n, c, dtype, tm):
    return (_sds((n, c), dtype), (tm, c), lambda i: (i, 0))


def _ored(c):
    return (_sds((1, c), F32), (1, c), lambda i: (0, 0))


def _colsum(v):
    return jnp.sum(v, axis=0, keepdims=True)


def _rstd(v):
    return lax.rsqrt(jnp.mean(v * v, axis=-1, keepdims=True) + EPS)


def _prenorm(name, x, g_pre, scale, shift, out_dtype, tm):
    n, d = x.shape

    def fn(xv, g, sc, sh):
        return (xv * _rstd(xv) * g * (1.0 + sc) + sh,), ()

    (h,), _ = _rowmap(name, fn, [_tile(x, tm), _row(g_pre), _row(scale), _row(shift)], [_otile(n, d, out_dtype, tm)], [],
                      (n // tm,))
    return h


def _postnorm(name, x, u, g_post, gate, weight, tm):
    n, d = x.shape

    def fn(xv, uv, g, gt):
        return (xv + weight * (1.0 + gt) * (uv * _rstd(uv) * g),), ()

    (y,), _ = _rowmap(name, fn, [_tile(x, tm), _tile(u, tm), _row(g_post), _row(gate)], [_otile(n, d, F32, tm)], [], (n // tm,))
    return y


def _postnorm_bwd(name, dout, u, g_post, gate, weight, out_dtype, tm):
    n, d = u.shape

    def fn(dv, uv, g, gt):
        r = _rstd(uv)
        un = uv * r
        dy = dv * (weight * (1.0 + gt))
        a = dy * g
        du = r * (a - un * jnp.mean(a * un, axis=-1, keepdims=True))
        return (du,), (_colsum(dv * (weight * (un * g))), _colsum(dy * un))

    (du,), reds = _rowmap(name, fn, [_tile(dout, tm), _tile(u, tm), _row(g_post), _row(gate)], [_otile(n, d, out_dtype, tm)],
                          [_ored(d), _ored(d)], (n // tm,))
    return du, reds


def _prenorm_bwd(name, dh, x, dout, g_pre, scale, tm):
    n, d = x.shape

    def fn(dhv, xv, dv, g, sc):
        dhv = dhv.astype(F32)
        r = _rstd(xv)
        xn = xv * r
        b = dhv * (g * (1.0 + sc))
        dx = dv + r * (b - xn * jnp.mean(b * xn, axis=-1, keepdims=True))
        return (dx,), (_colsum(dhv), _colsum(dhv * (xn * g)), _colsum(dhv * ((1.0 + sc) * xn)))

    (dx,), reds = _rowmap(name, fn, [_tile(dh, tm), _tile(x, tm), _tile(dout, tm), _row(g_pre), _row(scale)],
                          [_otile(n, d, F32, tm)], [_ored(d), _ored(d), _ored(d)], (n // tm,))
    return dx, reds


def _ffn_fwd(tag, h, w_in8, w_out4, tm):
    s, d = h.shape
    nt = s // tm
    gu = _matmul(f"ffn_up_{tag}", h, w_in8, a_blk=(tm, d), a_map=lambda g, i, k: (i, 0), b_blk=(None, d, FF_BLK),
                 b_map=lambda g, i, k: (g, 0, 0), o_shape=(8, s, FF_BLK), o_blk=(None, tm, FF_BLK), o_map=lambda g, i, k: (g, i, 0),
                 grid=(8, nt, 1), contract=NN, out_dtype=BF16)

    def act_fn(gv, uv):
        gv = gv.astype(F32)
        return (gv * jax.nn.sigmoid(gv) * uv.astype(F32),), ()

    (act,), _ = _rowmap(f"ffn_act_{tag}", act_fn,
                        [(gu, (None, tm, FF_BLK), lambda j, i: (j, i, 0)), (gu, (None, tm, FF_BLK), lambda j, i: (j + 4, i, 0))],
                        [(_sds((4, s, FF_BLK), BF16), (None, tm, FF_BLK), lambda j, i: (j, i, 0))], [], (4, nt))
    u = _matmul(f"ffn_down_{tag}", act, w_out4, a_blk=(None, tm, FF_BLK), a_map=lambda i, k: (k, i, 0), b_blk=(None, FF_BLK, d),
                b_map=lambda i, k: (k, 0, 0), o_shape=(s, d), o_blk=(tm, d), o_map=lambda i, k: (i, 0), grid=(nt, 4),
                contract=NN, out_dtype=F32)
    return u, gu


def _ffn_bwd_act(tag, du, gu, w_out4, tm):
    s, d = du.shape
    nt = s // tm
    dact = _matmul(f"ffn_dact_{tag}", du, w_out4, a_blk=(tm, d), a_map=lambda j, i, k: (i, 0), b_blk=(None, FF_BLK, d),
                   b_map=lambda j, i, k: (j, 0, 0), o_shape=(4, s, FF_BLK), o_blk=(None, tm, FF_BLK), o_map=lambda j, i, k: (j, i, 0),
                   grid=(4, nt, 1), contract=NT, out_dtype=BF16)

    def bwd_fn(gv, uv, dav):
        gv, uv, dav = gv.astype(F32), uv.astype(F32), dav.astype(F32)
        sg = jax.nn.sigmoid(gv)
        silu = gv * sg
        return (jnp.stack([dav * uv * (sg * (1.0 + gv * (1.0 - sg))), dav * silu]), silu * uv), ()

    blk = (None, tm, FF_BLK)
    (dgu, act), _ = _rowmap(
        f"ffn_dgu_{tag}", bwd_fn,
        [(gu, blk, lambda j, i: (j, i, 0)), (gu, blk, lambda j, i: (j + 4, i, 0)), (dact, blk, lambda j, i: (j, i, 0))],
        [(_sds((2, 4, s, FF_BLK), BF16), (2, None, tm, FF_BLK), lambda j, i: (0, j, i, 0)),
         (_sds((4, s, FF_BLK), BF16), blk, lambda j, i: (j, i, 0))], [], (4, nt))
    return dgu.reshape(8, s, FF_BLK), act


def _ffn_dw_in(tag, h, dgu, tm):
    s, d = h.shape
    return _matmul(f"ffn_dwin_{tag}", h, dgu, a_blk=(tm, d), a_map=lambda g, k: (k, 0), b_blk=(None, tm, FF_BLK),
                   b_map=lambda g, k: (g, k, 0), o_shape=(8, d, FF_BLK), o_blk=(None, d, FF_BLK), o_map=lambda g, k: (g, 0, 0),
                   grid=(8, s // tm), contract=TN, out_dtype=BF16)


def _ffn_dw_out(tag, act, du, tm):
    s, d = du.shape
    return _matmul(f"ffn_dwout_{tag}", act, du, a_blk=(None, tm, FF_BLK), a_map=lambda j, k: (j, k, 0), b_blk=(tm, d),
                   b_map=lambda j, k: (k, 0), o_shape=(4, FF_BLK, d), o_blk=(None, FF_BLK, d), o_map=lambda j, k: (j, 0, 0),
                   grid=(4, s // tm), contract=TN, out_dtype=BF16)


def _ffn_dh(tag, dgu, w_in8, tm):
    s = dgu.shape[1]
    d = w_in8.shape[1]
    return _matmul(f"ffn_dh_{tag}", dgu, w_in8, a_blk=(None, tm, FF_BLK), a_map=lambda i, k: (k, i, 0), b_blk=(None, d, FF_BLK),
                   b_map=lambda i, k: (k, 0, 0), o_shape=(s, d), o_blk=(tm, d), o_map=lambda i, k: (i, 0), grid=(s // tm, 8),
                   contract=NT, out_dtype=F32)


def _window_sum(x, window, transpose):
    s = x.shape[0]
    t = lax.broadcasted_iota(jnp.int32, (s, 1), 0)
    half = window // 2
    cnt = jnp.minimum(t + half, s) - jnp.maximum(t - half, 0)
    inv = 1.0 / cnt.astype(F32)
    if transpose:
        x = x * inv
        offsets = range(-half + 1, half + 1)
    else:
        offsets = range(-half, half)
    acc = jnp.zeros_like(x)
    for o in offsets:
        shifted = x if o == 0 else pltpu.roll(x, (-o) % s, 0)
        valid = jnp.logical_and(t + o >= 0, t + o < s)
        acc = acc + jnp.where(valid, shifted, 0.0)
    return acc if transpose else acc * inv


def _pool_mix(name, x, transpose, out_dtype):
    s, d = x.shape

    def body(x_ref, o_ref):
        g = pl.program_id(0)
        for gi, window in enumerate(POOL_WINDOWS):
            @pl.when(g == gi)
            def _(window=window):
                xv = x_ref[...].astype(F32)
                o_ref[...] = (_window_sum(xv, window, transpose) - xv).astype(o_ref.dtype)

    return pl.pallas_call(
        body, name=name, grid=(len(POOL_WINDOWS),), in_specs=[pl.BlockSpec((s, POOL_GROUP), lambda g: (0, g))],
        out_specs=pl.BlockSpec((s, POOL_GROUP), lambda g: (0, g)), out_shape=_sds((s, d), out_dtype), compiler_params=_params(),
    )(x)


def _pool_fwd(h, w4, bias, pscale, tm):
    s, d = h.shape
    nt = s // tm
    z = _pool_mix("pool_mix", h, False, BF16)
    v = _matmul("pool_proj", z, w4, a_blk=(tm, POOL_GROUP), a_map=lambda i, g, k: (i, g), b_blk=(None, POOL_GROUP, POOL_GROUP),
                b_map=lambda i, g, k: (g, 0, 0), o_shape=(s, d), o_blk=(tm, POOL_GROUP), o_map=lambda i, g, k: (i, g),
                grid=(nt, 4, 1), contract=NN, out_dtype=F32, bias=bias, bias_blk=(1, POOL_GROUP), bias_map=lambda i, g, k: (0, g))
    (u,), _ = _rowmap("pool_scale", lambda vv, ps: ((vv * ps,), ()), [_tile(v, tm), _row(pscale)], [_otile(s, d, F32, tm)], [],
                      (nt,))
    return u, z, v


def _pool_bwd(du, z, v, w4, pscale, tm):
    s, d = du.shape
    nt = s // tm

    def fn(duv, vv, ps):
        dv = duv * ps
        return (dv,), (_colsum(duv * vv), _colsum(dv))

    (dv,), reds = _rowmap("pool_dscale", fn, [_tile(du, tm), _tile(v, tm), _row(pscale)], [_otile(s, d, BF16, tm)],
                          [_ored(d), _ored(d)], (nt,))
    dw4 = _matmul("pool_dw", z, dv, a_blk=(tm, POOL_GROUP), a_map=lambda g, k: (k, g), b_blk=(tm, POOL_GROUP),
                  b_map=lambda g, k: (k, g), o_shape=(4, POOL_GROUP, POOL_GROUP), o_blk=(None, POOL_GROUP, POOL_GROUP),
                  o_map=lambda g, k: (g, 0, 0), grid=(4, nt), contract=TN, out_dtype=F32)
    dz = _matmul("pool_dz", dv, w4, a_blk=(tm, POOL_GROUP), a_map=lambda i, g, k: (i, g), b_blk=(None, POOL_GROUP, POOL_GROUP),
                 b_map=lambda i, g, k: (g, 0, 0), o_shape=(s, d), o_blk=(tm, POOL_GROUP), o_map=lambda i, g, k: (i, g),
                 grid=(nt, 4, 1), contract=NT, out_dtype=F32)
    dh = _pool_mix("pool_mix_t", dz, True, F32)
    return dh, dw4, reds


def _lane(shape):
    return lax.broadcasted_iota(jnp.int32, shape, 1)


def _rope_swap(v, transpose):
    half = QK_ROPE // 2
    lane = _lane(v.shape)
    up = pltpu.roll(v, v.shape[1] - half, 1)
    down = pltpu.roll(v, half, 1)
    if transpose:
        return jnp.where(lane < half, up, jnp.where(lane < QK_ROPE, -down, 0.0))
    return jnp.where(lane < half, -up, jnp.where(lane < QK_ROPE, down, 0.0))


def _rope(v, cos, sin):
    return v * cos + _rope_swap(v, False) * sin


def _rope_t(g, cos, sin):
    return g * cos + _rope_swap(g * sin, True)


def _mla_mid(lat, q_norm, kv_norm, cos_k, sin_k, tm):
    s = lat.shape[0]

    def fn(lv, qn, kn, cs, sn):
        cq = lv[:, :Q_LORA]
        ckv = lv[:, Q_LORA:Q_LORA + KV_LORA]
        kr = lv[:, Q_LORA + KV_LORA:]
        cq = cq * _rstd(cq) * qn
        ckv = ckv * _rstd(ckv) * kn
        return (cq, jnp.concatenate([ckv, _rope(kr, cs, sn)], axis=1)), ()

    (cq, kcat), _ = _rowmap("mla_mid", fn, [_tile(lat, tm), _row(q_norm), _row(kv_norm), _tile(cos_k, tm), _tile(sin_k, tm)],
                            [_otile(s, Q_LORA, BF16, tm), _otile(s, QK_PAD, BF16, tm)], [], (s // tm,))
    return cq, kcat


def _mla_mid_bwd(lat, dcq, dkcat, dv, q_norm, kv_norm, cos_k, sin_k, tm):
    s = lat.shape[0]

    def fn(lv, dq, dk, dvv, qn, kn, cs, sn):
        cq = lv[:, :Q_LORA]
        ckv = lv[:, Q_LORA:Q_LORA + KV_LORA]
        rq, rk = _rstd(cq), _rstd(ckv)
        cqn, ckn = cq * rq, ckv * rk
        a = dq * qn
        d_cq = rq * (a - cqn * jnp.mean(a * cqn, axis=-1, keepdims=True))
        dckv = dk[:, :KV_LORA] + dvv
        a2 = dckv * kn
        d_ckv = rk * (a2 - ckn * jnp.mean(a2 * ckn, axis=-1, keepdims=True))
        d_kr = _rope_t(dk[:, KV_LORA:], cs, sn)
        return (jnp.concatenate([d_cq, d_ckv, d_kr], axis=1),), (_colsum(dq * cqn), _colsum(dckv * ckn))

    (dlat,), reds = _rowmap(
        "mla_mid_bwd", fn,
        [_tile(lat, tm), _tile(dcq, tm), _tile(dkcat, tm), _tile(dv, tm), _row(q_norm), _row(kv_norm), _tile(cos_k, tm),
         _tile(sin_k, tm)],
        [_otile(s, LAT_PAD, BF16, tm)], [_ored(Q_LORA), _ored(KV_LORA)], (s // tm,))
    return dlat, reds


def _mla_q(cq, wq, wukp, cos_k, sin_k, tm):
    s = cq.shape[0]

    def body(cq_ref, wq_ref, wuk_ref, cos_ref, sin_ref, o_ref):
        aq = _dot(cq_ref[...], wq_ref[...], NN)
        qlat = _dot(aq.astype(BF16), wuk_ref[...], NN)
        roped = _rope(aq[:, KV_LORA:], cos_ref[...], sin_ref[...])
        o_ref[...] = (jnp.concatenate([qlat[:, :KV_LORA], roped], axis=1) * ATTN_SCALE).astype(o_ref.dtype)

    wblk = pl.BlockSpec((None, QK_PAD, QK_PAD), lambda h, i: (h, 0, 0))
    tblk = pl.BlockSpec((tm, KV_LORA), lambda h, i: (i, 0))
    return pl.pallas_call(
        body, name="mla_q", grid=(N_HEADS, s // tm),
        in_specs=[pl.BlockSpec((tm, Q_LORA), lambda h, i: (i, 0)), wblk, wblk, tblk, tblk],
        out_specs=pl.BlockSpec((None, tm, QK_PAD), lambda h, i: (h, i, 0)), out_shape=_sds((N_HEADS, s, QK_PAD), BF16),
        compiler_params=_params(),
    )(cq, wq, wukp, cos_k, sin_k)


def _mla_q_bwd(cq, wq, wukp, cos_k, sin_k, dqcat, tm):
    s = cq.shape[0]

    def body(cq_ref, wq_ref, wuk_ref, cos_ref, sin_ref, dq_ref, dcq_ref, dwq_ref, dwuk_ref):
        h, i = pl.program_id(0), pl.program_id(1)
        cqv = cq_ref[...]
        aq = _dot(cqv, wq_ref[...], NN).astype(BF16)
        g = dq_ref[...].astype(F32) * ATTN_SCALE
        gl, gr = g[:, :KV_LORA], g[:, KV_LORA:]
        dqlat = jnp.concatenate([gl, jnp.zeros_like(gl)], axis=1).astype(BF16)
        d_rope = _rope_t(gr, cos_ref[...], sin_ref[...])
        daq = _dot(dqlat, wuk_ref[...], NT) + jnp.concatenate([jnp.zeros_like(d_rope), d_rope], axis=1)
        daq_b = daq.astype(BF16)
        d_wuk = _dot(aq, dqlat, TN)
        d_wq = _dot(cqv, daq_b, TN)
        d_cq = _dot(daq_b, wq_ref[...], NT)
        rows = pl.ds(pl.multiple_of(i * tm, tm), tm)

        @pl.when(i == 0)
        def _():
            dwq_ref[...] = d_wq
            dwuk_ref[...] = d_wuk

        @pl.when(i != 0)
        def _():
            dwq_ref[...] += d_wq
            dwuk_ref[...] += d_wuk

        @pl.when(h == 0)
        def _():
            dcq_ref[rows, :] = d_cq

        @pl.when(h != 0)
        def _():
            dcq_ref[rows, :] += d_cq

    wblk = pl.BlockSpec((None, QK_PAD, QK_PAD), lambda h, i: (h, 0, 0))
    tblk = pl.BlockSpec((tm, KV_LORA), lambda h, i: (i, 0))
    return pl.pallas_call(
        body, name="mla_q_bwd", grid=(N_HEADS, s // tm),
        in_specs=[pl.BlockSpec((tm, Q_LORA), lambda h, i: (i, 0)), wblk, wblk, tblk, tblk,
                  pl.BlockSpec((None, tm, QK_PAD), lambda h, i: (h, i, 0))],
        out_specs=[pl.BlockSpec((s, Q_LORA), lambda h, i: (0, 0)), wblk, wblk],
        out_shape=[_sds((s, Q_LORA), F32), _sds((N_HEADS, QK_PAD, QK_PAD), F32), _sds((N_HEADS, QK_PAD, QK_PAD), F32)],
        compiler_params=_params(),
    )(cq, wq, wukp, cos_k, sin_k, dqcat)


def _flash_fwd(qcat, kcat, tq, tk):
    n_h, s, _ = qcat.shape
    n_k = s // tk

    def body(q_ref, k_ref, v_ref, o_ref, lse_ref):
        q = q_ref[...]

        def step(kk, carry):
            m, l, acc = carry
            rows = pl.ds(pl.multiple_of(kk * tk, tk), tk)
            sc = _dot(q, k_ref[rows, :], NT)
            m_new = jnp.maximum(m, jnp.max(sc, axis=1, keepdims=True))
            alpha = jnp.exp(m - m_new)
            p = jnp.exp(sc - m_new)
            l = alpha * l + jnp.sum(p, axis=1, keepdims=True)
            acc = alpha * acc + _dot(p.astype(BF16), v_ref[rows, :], NN)
            return m_new, l, acc

        init = (jnp.full((tq, 1), -1e30, F32), jnp.zeros((tq, 1), F32), jnp.zeros((tq, KV_LORA), F32))
        m, l, acc = lax.fori_loop(0, n_k, step, init)
        o_ref[...] = (acc / l).astype(o_ref.dtype)
        lse_ref[...] = m + jnp.log(l)

    return pl.pallas_call(
        body, name="mla_attn", grid=(n_h, s // tq),
        in_specs=[pl.BlockSpec((None, tq, QK_PAD), lambda h, i: (h, i, 0)), pl.BlockSpec((s, QK_PAD), lambda h, i: (0, 0)),
                  pl.BlockSpec((s, KV_LORA), lambda h, i: (0, 0))],
        out_specs=[pl.BlockSpec((None, tq, KV_LORA), lambda h, i: (h, i, 0)), pl.BlockSpec((None, tq, 1), lambda h, i: (h, i, 0))],
        out_shape=[_sds((n_h, s, KV_LORA), BF16), _sds((n_h, s, 1), F32)], compiler_params=_params(),
    )(qcat, kcat, kcat)


def _flash_bwd(qcat, kcat, o, do, lse, tq, tk):
    n_h, s, _ = qcat.shape
    n_k = s // tk

    def body(q_ref, k_ref, v_ref, o_ref, do_ref, lse_ref, dq_ref, dk_ref, dv_ref, dq_acc):
        h, i = pl.program_id(0), pl.program_id(1)

        @pl.when(jnp.logical_and(h == 0, i == 0))
        def _():
            dk_ref[...] = jnp.zeros_like(dk_ref)
            dv_ref[...] = jnp.zeros_like(dv_ref)

        q = q_ref[...]
        dov = do_ref[...]
        lse_v = lse_ref[...]
        delta = jnp.sum(dov.astype(F32) * o_ref[...].astype(F32), axis=1, keepdims=True)
        dq_acc[...] = jnp.zeros_like(dq_acc)

        def step(kk, carry):
            rows = pl.ds(pl.multiple_of(kk * tk, tk), tk)
            k = k_ref[rows, :]
            p = jnp.exp(_dot(q, k, NT) - lse_v)
            dp = _dot(dov, v_ref[rows, :], NT)
            ds = (p * (dp - delta)).astype(BF16)
            dq_acc[...] += _dot(ds, k, NN)
            dv_ref[rows, :] += _dot(p.astype(BF16), dov, TN)
            dk_ref[rows, :] += _dot(ds, q, TN)
            return carry

        lax.fori_loop(0, n_k, step, 0)
        dq_ref[...] = dq_acc[...].astype(dq_ref.dtype)

    qblk = pl.BlockSpec((None, tq, QK_PAD), lambda h, i: (h, i, 0))
    oblk = pl.BlockSpec((None, tq, KV_LORA), lambda h, i: (h, i, 0))
    return pl.pallas_call(
        body, name="mla_attn_bwd", grid=(n_h, s // tq),
        in_specs=[qblk, pl.BlockSpec((s, QK_PAD), lambda h, i: (0, 0)), pl.BlockSpec((s, KV_LORA), lambda h, i: (0, 0)), oblk, oblk,
                  pl.BlockSpec((None, tq, 1), lambda h, i: (h, i, 0))],
        out_specs=[qblk, pl.BlockSpec((s, QK_PAD), lambda h, i: (0, 0)), pl.BlockSpec((s, KV_LORA), lambda h, i: (0, 0))],
        out_shape=[_sds((n_h, s, QK_PAD), BF16), _sds((s, QK_PAD), F32), _sds((s, KV_LORA), F32)],
        scratch_shapes=[pltpu.VMEM((tq, QK_PAD), F32)], compiler_params=_params(),
    )(qcat, kcat, kcat, o, do, lse)


def _mla_fwd(h, wts, cos_k, sin_k, tm):
    s, d = h.shape
    nt = s // tm
    lat = _matmul("mla_lat", h, wts["w_in"], a_blk=(tm, d), a_map=lambda i, k: (i, 0), b_blk=(d, LAT_PAD), b_map=lambda i, k: (0, 0),
                  o_shape=(s, LAT_PAD), o_blk=(tm, LAT_PAD), o_map=lambda i, k: (i, 0), grid=(nt, 1), contract=NN, out_dtype=F32)
    cq, kcat = _mla_mid(lat, wts["q_norm"], wts["kv_norm"], cos_k, sin_k, tm)
    qcat = _mla_q(cq, wts["wq"], wts["wukp"], cos_k, sin_k, tm)
    o_lat, lse = _flash_fwd(qcat, kcat, tm, tm)
    o = _matmul("mla_uv", o_lat, wts["wuv2"], a_blk=(None, tm, KV_LORA), a_map=lambda i, p, r: (2 * p + r, i, 0),
                b_blk=(None, KV_LORA, 2 * V_HEAD), b_map=lambda i, p, r: (2 * p + r, 0, 0), o_shape=(s, d), o_blk=(tm, 2 * V_HEAD),
                o_map=lambda i, p, r: (i, p), grid=(nt, N_HEADS // 2, 2), contract=NN, out_dtype=BF16)
    u = _matmul("mla_out", o, wts["w_o"], a_blk=(tm, d), a_map=lambda i, k: (i, 0), b_blk=(d, d), b_map=lambda i, k: (0, 0),
                o_shape=(s, d), o_blk=(tm, d), o_map=lambda i, k: (i, 0), grid=(nt, 1), contract=NN, out_dtype=F32)
    return u, (lat, cq, kcat, qcat, o_lat, lse, o)


def _mla_bwd(du, h, saved, wts, cos_k, sin_k, tm):
    lat, cq, kcat, qcat, o_lat, lse, o = saved
    s, d = h.shape
    nt = s // tm
    do = _matmul("mla_do", du, wts["w_o"], a_blk=(tm, d), a_map=lambda i, k: (i, 0), b_blk=(d, d), b_map=lambda i, k: (0, 0),
                 o_shape=(s, d), o_blk=(tm, d), o_map=lambda i, k: (i, 0), grid=(nt, 1), contract=NT, out_dtype=BF16)
    dw_o = _matmul("mla_dwo", o, du, a_blk=(tm, d), a_map=lambda k: (k, 0), b_blk=(tm, d), b_map=lambda k: (k, 0),
                   o_shape=(d, d), o_blk=(d, d), o_map=lambda k: (0, 0), grid=(nt,), contract=TN, out_dtype=F32)
    do_lat = _matmul("mla_dolat", do, wts["wuv2"], a_blk=(tm, 2 * V_HEAD), a_map=lambda hh, i, k: (i, hh // 2),
                     b_blk=(None, KV_LORA, 2 * V_HEAD), b_map=lambda hh, i, k: (hh, 0, 0), o_shape=(N_HEADS, s, KV_LORA),
                     o_blk=(None, tm, KV_LORA), o_map=lambda hh, i, k: (hh, i, 0), grid=(N_HEADS, nt, 1), contract=NT, out_dtype=BF16)
    dwuv2 = _matmul("mla_dwuv", o_lat, do, a_blk=(None, tm, KV_LORA), a_map=lambda hh, k: (hh, k, 0), b_blk=(tm, 2 * V_HEAD),
                    b_map=lambda hh, k: (k, hh // 2), o_shape=(N_HEADS, KV_LORA, 2 * V_HEAD), o_blk=(None, KV_LORA, 2 * V_HEAD),
                    o_map=lambda hh, k: (hh, 0, 0), grid=(N_HEADS, nt), contract=TN, out_dtype=F32)
    dqcat, dkcat, dv = _flash_bwd(qcat, kcat, o_lat, do_lat, lse, tm, tm)
    dcq, dwq, dwukp = _mla_q_bwd(cq, wts["wq"], wts["wukp"], cos_k, sin_k, dqcat, tm)
    dlat, (dqn, dkn) = _mla_mid_bwd(lat, dcq, dkcat, dv, wts["q_norm"], wts["kv_norm"], cos_k, sin_k, tm)
    dh = _matmul("mla_dh", dlat, wts["w_in"], a_blk=(tm, LAT_PAD), a_map=lambda i, k: (i, 0), b_blk=(d, LAT_PAD),
                 b_map=lambda i, k: (0, 0), o_shape=(s, d), o_blk=(tm, d), o_map=lambda i, k: (i, 0), grid=(nt, 1), contract=NT,
                 out_dtype=F32)
    dw_in = _matmul("mla_dwin", h, dlat, a_blk=(tm, d), a_map=lambda k: (k, 0), b_blk=(tm, LAT_PAD), b_map=lambda k: (k, 0),
                    o_shape=(d, LAT_PAD), o_blk=(d, LAT_PAD), o_map=lambda k: (0, 0), grid=(nt,), contract=TN, out_dtype=F32)
    return dh, dict(w_in=dw_in, wq=dwq, wukp=dwukp, wuv2=dwuv2, w_o=dw_o, q_norm=dqn, kv_norm=dkn)


def _adamw(name, parts, w, m, v):
    n_parts, r, c = parts.shape
    tr = r
    for cand in (256, 128, 64, 32, 16, 8):
        if r > cand and r % cand == 0:
            tr = cand
            break

    def body(p_ref, w_ref, m_ref, v_ref, g_ref, d_ref, nm_ref, nv_ref):
        g = p_ref[0].astype(F32)
        for k in range(1, n_parts):
            g = g + p_ref[k].astype(F32)
        nm = ADAM_B1 * m_ref[...] + (1.0 - ADAM_B1) * g
        nv = ADAM_B2 * v_ref[...] + (1.0 - ADAM_B2) * (g * g)
        m_hat = nm / (1.0 - ADAM_B1 ** ADAM_STEP)
        v_hat = nv / (1.0 - ADAM_B2 ** ADAM_STEP)
        g_ref[...] = g
        d_ref[...] = -ADAM_LR * (m_hat / (jnp.sqrt(v_hat) + ADAM_EPS) + ADAM_WD * w_ref[...])
        nm_ref[...] = nm
        nv_ref[...] = nv

    blk = pl.BlockSpec((tr, c), lambda i: (i, 0))
    return pl.pallas_call(
        body, name=name, grid=(r // tr,), in_specs=[pl.BlockSpec((n_parts, tr, c), lambda i: (0, i, 0)), blk, blk, blk],
        out_specs=[blk] * 4, out_shape=[_sds((r, c), F32)] * 4, compiler_params=_params(),
    )(parts, w, m, v)


def _adamw_slab(name, parts, w, m, v, bufs, f):
    n_parts, r, c = parts.shape
    tr = max(t for t in range(8, 257, 8) if r % t == 0)

    def body(p_ref, w_ref, m_ref, v_ref, *rest):
        g_ref, d_ref, nm_ref, nv_ref = rest[4:]
        g = p_ref[0].astype(F32)
        for k in range(1, n_parts):
            g = g + p_ref[k].astype(F32)
        nm = ADAM_B1 * m_ref[...] + (1.0 - ADAM_B1) * g
        nv = ADAM_B2 * v_ref[...] + (1.0 - ADAM_B2) * (g * g)
        m_hat = nm / (1.0 - ADAM_B1 ** ADAM_STEP)
        v_hat = nv / (1.0 - ADAM_B2 ** ADAM_STEP)
        g_ref[...] = g
        d_ref[...] = -ADAM_LR * (m_hat / (jnp.sqrt(v_hat) + ADAM_EPS) + ADAM_WD * w_ref[...])
        nm_ref[...] = nm
        nv_ref[...] = nv

    blk = pl.BlockSpec((None, tr, c), lambda i: (f, i, 0))
    return pl.pallas_call(
        body, name=name, grid=(r // tr,),
        in_specs=[pl.BlockSpec((n_parts, tr, c), lambda i: (0, i, 0)), blk, blk, blk] + [pl.BlockSpec(memory_space=pl.ANY)] * 4,
        out_specs=[blk] * 4, out_shape=[_sds(w.shape, F32)] * 4, input_output_aliases={4 + j: j for j in range(4)},
        compiler_params=_params(),
    )(parts, w, m, v, *bufs)


def _mesh_pos():
    return lax.axis_index("x"), lax.axis_index("y"), lax.axis_index("c")


def _flip(pos, mask):
    return tuple(1 - p if (mask >> (2 - b)) & 1 else p for b, p in enumerate(pos))


def _index(pos):
    return 4 * pos[0] + 2 * pos[1] + pos[2]


def _all_gather(name, xs):
    n = len(xs)

    def body(*refs):
        x_refs, o_refs = refs[:n], refs[n:2 * n]
        send_sems, recv_sems, local_sems = refs[2 * n:]
        me = _mesh_pos()
        sibling = _flip(me, 1)
        others = [_flip(me, 4), _flip(me, 2), _flip(me, 6)]

        def copy(k, j, block, to, src=None):
            dst = o_refs[k].at[_index(block)]
            return pltpu.make_async_remote_copy(
                src_ref=dst if src is None else src, dst_ref=dst, send_sem=send_sems.at[k, j], recv_sem=recv_sems.at[k, j],
                device_id=to, device_id_type=MESH)

        local = [pltpu.make_async_copy(x_refs[k], o_refs[k].at[_index(me)], local_sems.at[k]) for k in range(n)]
        for cp in local:
            cp.start()
        first = []
        for k in range(n):
            first.append(copy(k, 0, me, sibling, src=x_refs[k]))
            first += [copy(k, 1 + j, me, other, src=x_refs[k]) for j, other in enumerate(others)]
        for cp in first:
            cp.start()
        passed = []
        for j, other in enumerate(others):
            for k in range(n):
                copy(k, 1 + j, other, me).wait_recv()
                cp = copy(k, 4 + j, other, sibling)
                cp.start()
                passed.append(cp)
        for k in range(n):
            copy(k, 0, sibling, me).wait_recv()
        for j, other in enumerate(others):
            for k in range(n):
                copy(k, 4 + j, _flip(other, 1), me).wait_recv()
        for cp in first + passed:
            cp.wait_send()
        for cp in local:
            cp.wait()

    any_spec = pl.BlockSpec(memory_space=pl.ANY)
    return pl.pallas_call(
        body, name=name, in_specs=[any_spec] * n, out_specs=[any_spec] * n,
        out_shape=[_sds((N_DEV,) + x.shape, x.dtype) for x in xs],
        scratch_shapes=[pltpu.SemaphoreType.DMA((n, 7)), pltpu.SemaphoreType.DMA((n, 7)), pltpu.SemaphoreType.DMA((n,))],
    )(*xs)


def _all_to_all(name, groups):
    flat = [(gi, f) for gi, grp in enumerate(groups) for f in range(len(grp))]
    n = len(flat)
    n_groups = len(groups)

    def body(*refs):
        x_refs, o_refs = refs[:n], refs[n:n + n_groups]
        send_sems, recv_sems, local_sems = refs[n + n_groups:]
        me = _mesh_pos()
        local, sends, recvs = [], [], []
        for k, (gi, f) in enumerate(flat):
            local.append(pltpu.make_async_copy(x_refs[k].at[_index(me)], o_refs[gi].at[_index(me), f], local_sems.at[k]))
            for mask in range(1, N_DEV):
                peer = _flip(me, mask)
                sends.append(pltpu.make_async_remote_copy(
                    src_ref=x_refs[k].at[_index(peer)], dst_ref=o_refs[gi].at[_index(me), f], send_sem=send_sems.at[k, mask - 1],
                    recv_sem=recv_sems.at[k, mask - 1], device_id=peer, device_id_type=MESH))
                recvs.append(pltpu.make_async_remote_copy(
                    src_ref=x_refs[k].at[_index(me)], dst_ref=o_refs[gi].at[_index(peer), f], send_sem=send_sems.at[k, mask - 1],
                    recv_sem=recv_sems.at[k, mask - 1], device_id=peer, device_id_type=MESH))
        for cp in local + sends:
            cp.start()
        for cp in recvs:
            cp.wait_recv()
        for cp in sends:
            cp.wait_send()
        for cp in local:
            cp.wait()

    any_spec = pl.BlockSpec(memory_space=pl.ANY)
    return pl.pallas_call(
        body, name=name, in_specs=[any_spec] * n, out_specs=[any_spec] * n_groups,
        out_shape=[_sds((N_DEV, len(grp)) + grp[0].shape[1:], grp[0].dtype) for grp in groups],
        scratch_shapes=[pltpu.SemaphoreType.DMA((n, 7)), pltpu.SemaphoreType.DMA((n, 7)), pltpu.SemaphoreType.DMA((n,))],
    )(*[a for grp in groups for a in grp])


def _split_copies(kind, outgoing, x_refs, land_refs, send_sems, recv_sems):
    me = _mesh_pos()
    copies = []
    for k, (x_ref, land_ref) in enumerate(zip(x_refs, land_refs)):
        for mask in range(1, N_DEV):
            peer = _flip(me, mask)
            sem = k * (N_DEV - 1) + mask - 1
            copies.append(pltpu.make_async_remote_copy(
                src_ref=x_ref if kind == "gather" else x_ref.at[_index(peer)], dst_ref=land_ref.at[_index(me if outgoing else peer)],
                send_sem=send_sems.at[sem], recv_sem=recv_sems.at[sem], device_id=peer, device_id_type=MESH))
    return copies


_HBM_SPEC = pl.BlockSpec(memory_space=pltpu.HBM)
_SEM_SPEC = pl.BlockSpec(memory_space=pltpu.SEMAPHORE)
_EFFECT = pltpu.SideEffectType.DATAFLOW_SIDE_EFFECTING


def _split_start(name, kind, xs):
    n = len(xs)
    lands = [lax.empty(((N_DEV,) + x.shape) if kind == "gather" else x.shape, x.dtype) for x in xs]

    def body(*refs):
        x_refs, land_refs = refs[:n], refs[n:2 * n]
        send_sems, recv_sems = refs[2 * n], refs[2 * n + 1]
        token = refs[-1]
        for cp in _split_copies(kind, True, x_refs, land_refs, send_sems, recv_sems):
            cp.start()
        token[...] = jnp.zeros_like(token)

    hbm = [pltpu.HBM(a.shape, a.dtype) for a in list(xs) + lands]
    res = pl.pallas_call(
        body, name=name,
        out_shape=[pltpu.SemaphoreType.DMA((n * (N_DEV - 1),)), pltpu.SemaphoreType.DMA((n * (N_DEV - 1),))] + hbm + [_sds((8, 128), F32)],
        in_specs=[_HBM_SPEC] * (2 * n), out_specs=[_SEM_SPEC, _SEM_SPEC] + [_HBM_SPEC] * (2 * n) + [pl.BlockSpec(memory_space=pltpu.VMEM)],
        input_output_aliases={j: 2 + j for j in range(2 * n)}, compiler_params=pltpu.CompilerParams(has_side_effects=_EFFECT),
    )(*[pltpu.with_memory_space_constraint(a, pltpu.HBM) for a in list(xs) + lands])
    return (kind, n, res[0], res[1], res[2:2 + 2 * n]), res[-1]


def _split_wait(name, state, after):
    kind, n, send_sems_in, recv_sems_in, thru = state

    def body(*refs):
        x_refs, land_refs = refs[:n], refs[n:2 * n]
        send_sems, recv_sems = refs[2 * n], refs[2 * n + 1]
        for cp in _split_copies(kind, True, x_refs, land_refs, send_sems, recv_sems):
            cp.wait_send()
        for cp in _split_copies(kind, False, x_refs, land_refs, send_sems, recv_sems):
            cp.wait_recv()

    res = pl.pallas_call(
        body, name=name, out_shape=[pltpu.HBM(a.shape, a.dtype) for a in thru],
        in_specs=[_HBM_SPEC] * (2 * n) + [_SEM_SPEC, _SEM_SPEC, pl.BlockSpec(memory_space=pl.ANY)], out_specs=[_HBM_SPEC] * (2 * n),
        input_output_aliases={j: j for j in range(2 * n)}, compiler_params=pltpu.CompilerParams(has_side_effects=_EFFECT),
    )(*thru, send_sems_in, recv_sems_in, after)
    return res[n:]


def _with_own(land, own):
    me = _index(_mesh_pos())
    return lax.dynamic_update_slice(land, own[None], (me,) + (0,) * own.ndim)


def _rope_tables(s):
    inv = 1.0 / (ROPE_THETA ** (jnp.arange(0, QK_ROPE, 2, dtype=F32) / QK_ROPE))
    ang = jnp.arange(s, dtype=F32)[:, None] * inv[None, :]
    pad = jnp.zeros((s, KV_LORA - QK_ROPE), F32)
    return (jnp.concatenate([jnp.cos(ang), jnp.cos(ang), pad], axis=1), jnp.concatenate([jnp.sin(ang), jnp.sin(ang), pad], axis=1))


def _row_of(v):
    return v.reshape(1, -1)


def kernel(x, c, ada_w, ada_b, norm_g, ffn_w_in, ffn_w_out, pool_w, pool_b, pool_scale, mla_w_in, mla_q_norm, mla_kv_norm, mla_w_uq, mla_w_uk, mla_w_uv, mla_w_o, loss_target, m_ada_w, m_ada_b, m_norm_g, m_ffn_w_in, m_ffn_w_out, m_pool_w, m_pool_b, m_pool_scale, m_mla_w_in, m_mla_q_norm, m_mla_kv_norm, m_mla_w_uq, m_mla_w_uk, m_mla_w_uv, m_mla_w_o, v_ada_w, v_ada_b, v_norm_g, v_ffn_w_in, v_ffn_w_out, v_pool_w, v_pool_b, v_pool_scale, v_mla_w_in, v_mla_q_norm, v_mla_kv_norm, v_mla_w_uq, v_mla_w_uk, v_mla_w_uv, v_mla_w_o):
    s, d = x.shape[1], x.shape[2]
    tm = min(512, s)
    tr = min(256, s)
    me = 4 * lax.axis_index("x") + 2 * lax.axis_index("y") + lax.axis_index("c")
    x0 = x.reshape(s, d)
    target = loss_target.reshape(s, d)
    n_mod = ada_w.shape[2] * N_DEV // d
    mod_blk = ada_w.shape[2]

    small = jnp.concatenate([c.reshape(-1), norm_g.reshape(-1), pool_b.reshape(-1), mla_q_norm.reshape(-1)]).reshape(1, -1)
    w_in_loc = [ffn_w_in[i, f].astype(BF16) for i in range(2) for f in range(2)]
    w_out_loc = [ffn_w_out[i, f].astype(BF16) for i in range(2) for f in range(2)]
    (small_all,) = _all_gather("gather_small", [small])
    small_all = small_all.reshape(N_DEV, -1)
    c_all = small_all[:, :d]
    off = d
    g_all = small_all[:, off:off + 12 * (d // N_DEV)].reshape(N_DEV, 2, 6, d // N_DEV).transpose(1, 2, 0, 3).reshape(2, 6, d)
    off += 12 * (d // N_DEV)
    pool_b_all = small_all[:, off:off + 4 * 32].reshape(N_DEV, 4, 32).transpose(1, 0, 2).reshape(1, d)
    off += 4 * 32
    q_norm_all = small_all[:, off:off + 32].reshape(1, Q_LORA)
    kv_norm_row = mla_kv_norm.reshape(1, KV_LORA)
    pscale_row = pool_scale.reshape(1, d)

    even = (jnp.arange(N_HEADS) % 2 == 0)[:, None, None]
    cos_k, sin_k = _rope_tables(s)

    def mla_weights(mla_w_in_all, mla_w_uq_all, mla_w_o_all):
        uq = mla_w_uq_all.reshape(Q_LORA, N_HEADS, QK_NOPE + QK_ROPE).transpose(1, 0, 2)
        zq = jnp.zeros((N_HEADS, Q_LORA, QK_NOPE), BF16)
        wq = jnp.concatenate(
            [uq[:, :, :QK_NOPE], zq, uq[:, :, QK_NOPE:], jnp.zeros((N_HEADS, Q_LORA, QK_PAD - KV_LORA - QK_ROPE), BF16)], axis=2)
        wukp = jnp.pad(mla_w_uk[0].transpose(1, 2, 0).astype(BF16), ((0, 0), (0, QK_PAD - QK_NOPE), (0, QK_PAD - KV_LORA)))
        uv = mla_w_uv[0].transpose(1, 0, 2).astype(BF16)
        wuv2 = jnp.where(even, jnp.concatenate([uv, jnp.zeros_like(uv)], axis=2), jnp.concatenate([jnp.zeros_like(uv), uv], axis=2))
        return dict(w_in=jnp.pad(mla_w_in_all.reshape(d, -1), ((0, 0), (0, LAT_PAD - mla_w_in.shape[2]))), wq=wq, wukp=wukp,
                    wuv2=wuv2, w_o=mla_w_o_all.reshape(d, d), q_norm=q_norm_all, kv_norm=kv_norm_row)

    (sc_all,), _ = _rowmap("ada_silu", lambda cv: ((cv * jax.nn.sigmoid(cv),), ()), [(c_all, (N_DEV, d), lambda i: (0, 0))],
                           [(_sds((N_DEV, d), F32), (N_DEV, d), lambda i: (0, 0))], [], (1,))
    ada_b_loc = lax.dynamic_slice_in_dim(ada_b, me * mod_blk, mod_blk, axis=1).reshape(2, 1, mod_blk)
    m_pad = 2 * N_DEV
    modp = _matmul("ada_mod", jnp.pad(sc_all, ((0, m_pad - N_DEV), (0, 0))), ada_w, a_blk=(m_pad, d), a_map=lambda i, k: (0, 0),
                   b_blk=(None, d, mod_blk), b_map=lambda i, k: (i, 0, 0), o_shape=(2, m_pad, mod_blk), o_blk=(None, m_pad, mod_blk),
                   o_map=lambda i, k: (i, 0, 0), grid=(2, 1), contract=NN, out_dtype=F32, bias=ada_b_loc, bias_blk=(None, 1, mod_blk),
                   bias_map=lambda i, k: (i, 0, 0))[:, :N_DEV]
    modp_all, w_in_first, w_out_first = _all_gather("gather_first", [modp.reshape(2 * N_DEV, mod_blk), w_in_loc[0], w_out_loc[0]])
    later_a = [w_in_loc[1], w_out_loc[1], pool_w.reshape(-1, POOL_GROUP).astype(BF16)]
    later_b = [w_in_loc[2], w_out_loc[2], w_in_loc[3], w_out_loc[3], mla_w_in[0].astype(BF16),
               mla_w_uq.reshape(mla_w_uq.shape[1], -1).astype(BF16), mla_w_o[0].astype(BF16)]
    state_a, token_a = _split_start("gather_start_a", "gather", later_a)
    state_b, token_b = _split_start("gather_start_b", "gather", later_b)
    w_in8 = [w_in_first, None, None, None]
    w_out4 = [w_out_first.reshape(4, FF_BLK, d), None, None, None]
    mod = lax.dynamic_index_in_dim(modp_all.reshape(N_DEV, 2, N_DEV, mod_blk), me, axis=2, keepdims=False)
    mod = mod.transpose(1, 0, 2).reshape(2, n_mod, d) + (token_a[0, 0] + token_b[0, 0])

    saved = []
    xs = x0
    w4 = mla_wts = None
    for i in range(2):
        for sub in range(3):
            if (i, sub) == (0, 1):
                lands = _split_wait("gather_wait_a", state_a, xs)
                w_in8[1] = _with_own(lands[0], later_a[0])
                w_out4[1] = _with_own(lands[1], later_a[1]).reshape(4, FF_BLK, d)
                w4 = _with_own(lands[2], later_a[2]).reshape(N_DEV, 4, 32, POOL_GROUP).transpose(1, 0, 2, 3)
                w4 = w4.reshape(4, POOL_GROUP, POOL_GROUP)
            if (i, sub) == (1, 0):
                lands = [_with_own(land, own) for land, own in zip(_split_wait("gather_wait_b", state_b, xs), later_b)]
                w_in8[2], w_in8[3] = lands[0], lands[2]
                w_out4[2], w_out4[3] = lands[1].reshape(4, FF_BLK, d), lands[3].reshape(4, FF_BLK, d)
                mla_wts = mla_weights(*lands[4:])
            shift, scale, gate = (_row_of(mod[i, 3 * sub + j]) for j in range(3))
            g_pre, g_post = _row_of(g_all[i, 2 * sub]), _row_of(g_all[i, 2 * sub + 1])
            tag = f"{i}{sub}"
            if sub != 1:
                f = sub // 2
                h = _prenorm(f"prenorm_{tag}", xs, g_pre, scale, shift, BF16, tr)
                u, extra = _ffn_fwd(tag, h, w_in8[2 * i + f], w_out4[2 * i + f], tm)
                weight = 0.5
            elif i == 0:
                h = _prenorm(f"prenorm_{tag}", xs, g_pre, scale, shift, F32, tr)
                u, z, v = _pool_fwd(h, w4, pool_b_all, pscale_row, tm)
                extra = (z, v)
                weight = 1.0
            else:
                h = _prenorm(f"prenorm_{tag}", xs, g_pre, scale, shift, BF16, tr)
                u, extra = _mla_fwd(h, mla_wts, cos_k, sin_k, tm)
                weight = 1.0
            saved.append((xs, h, u, extra, (shift, scale, gate, g_pre, g_post), weight))
            xs = _postnorm(f"postnorm_{tag}", xs, u, g_post, gate, weight, tr)

    def loss_fn(yv, tv):
        e = yv - tv
        return (e * (1.0 / d),), (_colsum(e * e),)

    (dx,), (sq,) = _rowmap("loss", loss_fn, [_tile(xs, tr), _tile(target, tr)], [_otile(s, d, F32, tr)], [_ored(d)], (s // tr,))
    loss = lax.psum(0.5 * jnp.sum(sq) / d, AXES)

    d_mod = [[None] * n_mod for _ in range(2)]
    d_g = [[None] * 6 for _ in range(2)]
    sent = {}
    pool_grads = mla_grads = None

    def start_scatter(key, arrays):
        state, token = _split_start(f"scatter_start_{key}", "scatter", arrays)
        sent[key] = (state, arrays)
        return token[0, 0]

    for i in (1, 0):
        for sub in (2, 1, 0):
            xin, h, u, extra, (shift, scale, gate, g_pre, g_post), weight = saved[3 * i + sub]
            tag = f"{i}{sub}"
            du, (dgate, dgpost) = _postnorm_bwd(f"postnorm_bwd_{tag}", dx, u, g_post, gate, weight, F32 if (sub == 1 and i == 0) else BF16, tr)
            if sub != 1:
                k = 2 * i + sub // 2
                dgu, act = _ffn_bwd_act(tag, du, extra, w_out4[k], tm)
                dw_out = _ffn_dw_out(tag, act, du, tm).reshape(N_DEV, FF_BLK // 2, d)
                if k == 0:
                    d_pool_w = pool_grads[0].reshape(4, N_DEV, 32, POOL_GROUP).transpose(1, 0, 2, 3).reshape(N_DEV, 4 * 32, POOL_GROUP)
                    token = start_scatter(tag + "_out", [dw_out, d_pool_w])
                    token = token + start_scatter(tag + "_in", [_ffn_dw_in(tag, h, dgu, tm)])
                else:
                    token = start_scatter(tag, [_ffn_dw_in(tag, h, dgu, tm), dw_out])
                g_pre = g_pre + token
                dh = _ffn_dh(tag, dgu, w_in8[k], tm)
            elif i == 0:
                dh, dw4, (dpscale, dpb) = _pool_bwd(du, extra[0], extra[1], w4, pscale_row, tm)
                pool_grads = (dw4, dpscale, dpb)
            else:
                dh, mla_grads = _mla_bwd(du, h, extra, mla_wts, cos_k, sin_k, tm)
                dwq = mla_grads["wq"]
                d_uq = jnp.concatenate([dwq[:, :, :QK_NOPE], dwq[:, :, KV_LORA:KV_LORA + QK_ROPE]], axis=2).transpose(1, 0, 2)
                token = start_scatter("mla", [mla_grads["w_in"][:, :mla_w_in.shape[2]].reshape(N_DEV, d // N_DEV, -1),
                                              d_uq.reshape(N_DEV, Q_LORA // N_DEV, -1), mla_grads["w_o"].reshape(N_DEV, d // N_DEV, d)])
                dwukp, dwuv2 = mla_grads["wukp"], mla_grads["wuv2"]
                d_uk = dwukp[:, :QK_NOPE, :KV_LORA].transpose(2, 0, 1).reshape(KV_LORA, -1)
                d_uv = jnp.where(even, dwuv2[:, :, :V_HEAD], dwuv2[:, :, V_HEAD:]).transpose(1, 0, 2).reshape(KV_LORA, -1)
                state_ukv, token_ukv = _split_start("gather_start_ukv", "gather", [d_uk, d_uv])
                g_pre = g_pre + (token + token_ukv[0, 0])
            dx, (dshift, dscale, dgpre) = _prenorm_bwd(f"prenorm_bwd_{tag}", dh, xin, dx, g_pre, scale, tr)
            d_mod[i][3 * sub:3 * sub + 3] = [dshift, dscale, dgate]
            d_g[i][2 * sub:2 * sub + 2] = [dgpre, dgpost]
    grad_x = dx.reshape(x.shape)

    dw4, dpscale, dpb = pool_grads
    d_mod_row = jnp.concatenate([jnp.concatenate(r, axis=1) for r in d_mod], axis=1)
    d_g_row = jnp.concatenate([jnp.concatenate(r, axis=1) for r in d_g], axis=1)
    small_g = jnp.concatenate([d_mod_row, d_g_row, dpb, dpscale, mla_grads["q_norm"], mla_grads["kv_norm"]], axis=1)
    (small_g_all,) = _all_gather("gather_small_grads", [small_g])
    small_g_all = small_g_all.reshape(N_DEV, -1)
    n_m = 2 * n_mod * d
    d_mod_all = small_g_all[:, :n_m].reshape(N_DEV, 2, n_mod * d)
    rest = small_g_all[:, n_m:]
    p_norm_g = lax.dynamic_slice_in_dim(rest[:, :12 * d].reshape(N_DEV, 12, d), me * (d // N_DEV), d // N_DEV, axis=2)
    p_pool_b = lax.dynamic_slice_in_dim(rest[:, 12 * d:13 * d].reshape(N_DEV, 4, POOL_GROUP), me * 32, 32, axis=2)
    p_pool_scale = rest[:, 13 * d:14 * d].reshape(N_DEV, 1, d)
    p_q_norm = lax.dynamic_slice_in_dim(rest[:, 14 * d:14 * d + Q_LORA], me * 32, 32, axis=1).reshape(N_DEV, 1, 32)
    p_kv_norm = rest[:, 14 * d + Q_LORA:].reshape(N_DEV, 1, KV_LORA)

    d_mod_loc = lax.dynamic_slice_in_dim(d_mod_all, me * mod_blk, mod_blk, axis=2).transpose(1, 0, 2)
    k_pad = 128
    sc_t = jnp.pad(sc_all.T, ((0, 0), (0, k_pad - N_DEV)))
    d_ada_w = _matmul("ada_dw", sc_t, jnp.pad(d_mod_loc, ((0, 0), (0, k_pad - N_DEV), (0, 0))), a_blk=(d, k_pad),
                      a_map=lambda i, k: (0, 0), b_blk=(None, k_pad, mod_blk), b_map=lambda i, k: (i, 0, 0), o_shape=(2, d, mod_blk),
                      o_blk=(None, d, mod_blk), o_map=lambda i, k: (i, 0, 0), grid=(2, 1), contract=NN, out_dtype=F32)

    def upd(name, parts, w, m, v):
        shape = w.shape
        r, cdim = parts.shape[1], parts.shape[2]
        return [o.reshape(shape) for o in _adamw(name, parts, w.reshape(r, cdim), m.reshape(r, cdim), v.reshape(r, cdim))]

    def landed(key, after):
        state, arrays = sent[key]
        lands = _split_wait(f"scatter_wait_{key}", state, after)
        return [_with_own(land, lax.dynamic_index_in_dim(a, me, 0, keepdims=False)) for land, a in zip(lands, arrays)]

    res = {}
    res["ada_w"] = upd("adam_ada_w", d_ada_w.reshape(1, 2 * d, mod_blk), ada_w, m_ada_w, v_ada_w)
    res["ada_b"] = upd("adam_ada_b", d_mod_all.reshape(N_DEV, 2, n_mod * d), ada_b, m_ada_b, v_ada_b)
    res["norm_g"] = upd("adam_norm_g", p_norm_g, norm_g, m_norm_g, v_norm_g)
    res["pool_b"] = upd("adam_pool_b", p_pool_b, pool_b, m_pool_b, v_pool_b)
    res["pool_scale"] = upd("adam_pool_scale", p_pool_scale, pool_scale, m_pool_scale, v_pool_scale)
    res["mla_q_norm"] = upd("adam_mla_q_norm", p_q_norm, mla_q_norm, m_mla_q_norm, v_mla_q_norm)
    res["mla_kv_norm"] = upd("adam_mla_kv_norm", p_kv_norm, mla_kv_norm, m_mla_kv_norm, v_mla_kv_norm)

    w_in_s, m_in_s, v_in_s = (a.reshape(4, d, FF_BLK) for a in (ffn_w_in, m_ffn_w_in, v_ffn_w_in))
    w_out_s, m_out_s, v_out_s = (a.reshape(4, FF_BLK // 2, d) for a in (ffn_w_out, m_ffn_w_out, v_ffn_w_out))
    bufs_in = [lax.empty(w_in_s.shape, F32) for _ in range(4)]
    bufs_out = [lax.empty(w_out_s.shape, F32) for _ in range(4)]
    for key, k in (("12", 3), ("mla", None), ("10", 2), ("02", 1)):
        parts = landed(key, grad_x)
        if k is None:
            res["mla_w_in"] = upd("adam_mla_w_in", parts[0], mla_w_in, m_mla_w_in, v_mla_w_in)
            res["mla_w_uq"] = upd("adam_mla_w_uq", parts[1], mla_w_uq, m_mla_w_uq, v_mla_w_uq)
            res["mla_w_o"] = upd("adam_mla_w_o", parts[2], mla_w_o, m_mla_w_o, v_mla_w_o)
            uk_all, uv_all = (_with_own(land, own) for land, own in zip(_split_wait("gather_wait_ukv", state_ukv, grad_x), (d_uk, d_uv)))
            res["mla_w_uk"] = upd("adam_mla_w_uk", uk_all, mla_w_uk, m_mla_w_uk, v_mla_w_uk)
            res["mla_w_uv"] = upd("adam_mla_w_uv", uv_all, mla_w_uv, m_mla_w_uv, v_mla_w_uv)
            continue
        bufs_in = _adamw_slab(f"adam_ffn_w_in_{key}", parts[0], w_in_s, m_in_s, v_in_s, bufs_in, k)
        bufs_out = _adamw_slab(f"adam_ffn_w_out_{key}", parts[1], w_out_s, m_out_s, v_out_s, bufs_out, k)
    p_out, p_pool_w = landed("00_out", res["ada_w"][1])
    bufs_out = _adamw_slab("adam_ffn_w_out_00", p_out, w_out_s, m_out_s, v_out_s, bufs_out, 0)
    res["pool_w"] = upd("adam_pool_w", p_pool_w, pool_w, m_pool_w, v_pool_w)
    (p_in,) = landed("00_in", res["pool_w"][1])
    bufs_in = _adamw_slab("adam_ffn_w_in_00", p_in, w_in_s, m_in_s, v_in_s, bufs_in, 0)
    res["ffn_w_in"] = [b.reshape(ffn_w_in.shape) for b in bufs_in]
    res["ffn_w_out"] = [b.reshape(ffn_w_out.shape) for b in bufs_out]

    order = ["ada_w", "ada_b", "norm_g", "ffn_w_in", "ffn_w_out", "pool_w", "pool_b", "pool_scale", "mla_w_in", "mla_q_norm",
             "mla_kv_norm", "mla_w_uq", "mla_w_uk", "mla_w_uv", "mla_w_o"]
    outs = [loss, grad_x]
    for j in range(4):
        outs += [res[name][j] for name in order]
    return tuple(outs)
```

```python
import functools

import jax
import jax.numpy as jnp
from jax import lax
from jax.experimental import pallas as pl
from jax.experimental.pallas import tpu as pltpu

F32 = jnp.float32
BF16 = jnp.bfloat16
N_DEV = 8
AXES = ("x", "y", "c")
MESH = pl.DeviceIdType.MESH

D_MODEL = 1024
N_HEADS = 16
QK_NOPE = 64
QK_ROPE = 32
V_HEAD = 64
Q_LORA = 256
KV_LORA = 128
LAT_PAD = 512
QK_PAD = 256
D_FF = 2816
FF_BLK = 2 * D_FF // N_DEV
POOL_WINDOWS = (2, 4, 8, 16)
POOL_GROUP = 256
ROPE_THETA = 10000.0
EPS = 1e-6
ATTN_SCALE = (QK_NOPE + QK_ROPE) ** -0.5
ADAM_LR, ADAM_B1, ADAM_B2, ADAM_EPS, ADAM_WD, ADAM_STEP = 0.001, 0.9, 0.999, 1e-08, 0.01, 10
VMEM_LIMIT = 56 * 1024 * 1024

NN = ((1,), (0,))
NT = ((1,), (1,))
TN = ((0,), (0,))


def _params(**kw):
    return pltpu.CompilerParams(vmem_limit_bytes=VMEM_LIMIT, **kw)


def _dot(a, b, contract):
    return lax.dot_general(a, b, (contract, ((), ())), preferred_element_type=F32)


def _matmul(name, a, b, *, a_blk, a_map, b_blk, b_map, o_shape, o_blk, o_map, grid, contract, out_dtype,
            bias=None, bias_blk=None, bias_map=None, after=None):
    n_k = grid[-1]
    k_axis = len(grid) - 1
    acc_shape = tuple(d for d in o_blk if d is not None)

    def body(*refs):
        a_ref, b_ref = refs[:2]
        bias_ref = refs[2] if bias is not None else None
        o_ref, acc = refs[-2:]
        k = pl.program_id(k_axis)

        @pl.when(k == 0)
        def _():
            acc[...] = jnp.zeros_like(acc)

        acc[...] += _dot(a_ref[...].astype(BF16), b_ref[...].astype(BF16), contract)

        @pl.when(k == n_k - 1)
        def _():
            r = acc[...]
            if bias is not None:
                r = r + bias_ref[...]
            o_ref[...] = r.astype(o_ref.dtype)

    in_specs = [pl.BlockSpec(a_blk, a_map), pl.BlockSpec(b_blk, b_map)]
    args = [a, b]
    if bias is not None:
        in_specs.append(pl.BlockSpec(bias_blk, bias_map))
        args.append(bias)
    if after is not None:
        in_specs.append(pl.BlockSpec(memory_space=pl.ANY))
        args.append(after)
    return pl.pallas_call(
        body, name=name, grid=grid, in_specs=in_specs, out_specs=pl.BlockSpec(o_blk, o_map),
        out_shape=jax.ShapeDtypeStruct(o_shape, out_dtype), scratch_shapes=[pltpu.VMEM(acc_shape, F32)],
        compiler_params=_params(),
    )(*args)


def _rowmap(name, fn, ins, outs, reds, grid):
    n_in, n_out, n_red = len(ins), len(outs), len(reds)

    def body(*refs):
        in_refs = refs[:n_in]
        out_refs = refs[n_in:n_in + n_out]
        red_refs = refs[n_in + n_out:]
        out_vals, red_vals = fn(*[r[...] for r in in_refs])
        for r, v in zip(out_refs, out_vals):
            r[...] = v.astype(r.dtype)
        if n_red:
            first = pl.program_id(0) == 0
            for ax in range(1, len(grid)):
                first = jnp.logical_and(first, pl.program_id(ax) == 0)

            @pl.when(first)
            def _():
                for r in red_refs:
                    r[...] = jnp.zeros_like(r)

            for r, v in zip(red_refs, red_vals):
                r[...] += v

    res = pl.pallas_call(
        body, name=name, grid=grid,
        in_specs=[pl.BlockSpec(blk, imap) for _, blk, imap in ins],
        out_specs=[pl.BlockSpec(blk, imap) for _, blk, imap in list(outs) + list(reds)],
        out_shape=[sds for sds, _, _ in list(outs) + list(reds)],
        compiler_params=_params(),
    )(*[a for a, _, _ in ins])
    return res[:n_out], res[n_out:]


def _sds(shape, dtype):
    return jax.ShapeDtypeStruct(shape, dtype)


def _tile(a, tm):
    return (a, (tm, a.shape[1]), lambda i: (i, 0))


def _row(a):
    return (a, (1, a.shape[1]), lambda i: (0, 0))


def _otile(n, c, dtype, tm):
    return (_sds((n, c), dtype), (tm, c), lambda i: (i, 0))


def _ored(c):
    return (_sds((1, c), F32), (1, c), lambda i: (0, 0))


def _colsum(v):
    return jnp.sum(v, axis=0, keepdims=True)


def _rstd(v):
    return lax.rsqrt(jnp.mean(v * v, axis=-1, keepdims=True) + EPS)


def _prenorm(name, x, g_pre, scale, shift, out_dtype, tm):
    n, d = x.shape

    def fn(xv, g, sc, sh):
        return (xv * _rstd(xv) * g * (1.0 + sc) + sh,), ()

    (h,), _ = _rowmap(name, fn, [_tile(x, tm), _row(g_pre), _row(scale), _row(shift)], [_otile(n, d, out_dtype, tm)], [],
                      (n // tm,))
    return h


def _postnorm(name, x, u, g_post, gate, weight, tm):
    n, d = x.shape

    def fn(xv, uv, g, gt):
        return (xv + weight * (1.0 + gt) * (uv * _rstd(uv) * g),), ()

    (y,), _ = _rowmap(name, fn, [_tile(x, tm), _tile(u, tm), _row(g_post), _row(gate)], [_otile(n, d, F32, tm)], [], (n // tm,))
    return y


def _postnorm_bwd(name, dout, u, g_post, gate, weight, out_dtype, tm):
    n, d = u.shape

    def fn(dv, uv, g, gt):
        r = _rstd(uv)
        un = uv * r
        dy = dv * (weight * (1.0 + gt))
        a = dy * g
        du = r * (a - un * jnp.mean(a * un, axis=-1, keepdims=True))
        return (du,), (_colsum(dv * (weight * (un * g))), _colsum(dy * un))

    (du,), reds = _rowmap(name, fn, [_tile(dout, tm), _tile(u, tm), _row(g_post), _row(gate)], [_otile(n, d, out_dtype, tm)],
                          [_ored(d), _ored(d)], (n // tm,))
    return du, reds


def _prenorm_bwd(name, dh, x, dout, g_pre, scale, tm):
    n, d = x.shape

    def fn(dhv, xv, dv, g, sc):
        dhv = dhv.astype(F32)
        r = _rstd(xv)
        xn = xv * r
        b = dhv * (g * (1.0 + sc))
        dx = dv + r * (b - xn * jnp.mean(b * xn, axis=-1, keepdims=True))
        return (dx,), (_colsum(dhv), _colsum(dhv * (xn * g)), _colsum(dhv * ((1.0 + sc) * xn)))

    (dx,), reds = _rowmap(name, fn, [_tile(dh, tm), _tile(x, tm), _tile(dout, tm), _row(g_pre), _row(scale)],
                          [_otile(n, d, F32, tm)], [_ored(d), _ored(d), _ored(d)], (n // tm,))
    return dx, reds


def _ffn_fwd(tag, h, w_in8, w_out4, tm):
    s, d = h.shape
    nt = s // tm
    gu = _matmul(f"ffn_up_{tag}", h, w_in8, a_blk=(tm, d), a_map=lambda g, i, k: (i, 0), b_blk=(None, d, FF_BLK),
                 b_map=lambda g, i, k: (g, 0, 0), o_shape=(8, s, FF_BLK), o_blk=(None, tm, FF_BLK), o_map=lambda g, i, k: (g, i, 0),
                 grid=(8, nt, 1), contract=NN, out_dtype=BF16)

    def act_fn(gv, uv):
        gv = gv.astype(F32)
        return (gv * jax.nn.sigmoid(gv) * uv.astype(F32),), ()

    (act,), _ = _rowmap(f"ffn_act_{tag}", act_fn,
                        [(gu, (None, tm, FF_BLK), lambda j, i: (j, i, 0)), (gu, (None, tm, FF_BLK), lambda j, i: (j + 4, i, 0))],
                        [(_sds((4, s, FF_BLK), BF16), (None, tm, FF_BLK), lambda j, i: (j, i, 0))], [], (4, nt))
    u = _matmul(f"ffn_down_{tag}", act, w_out4, a_blk=(None, tm, FF_BLK), a_map=lambda i, k: (k, i, 0), b_blk=(None, FF_BLK, d),
                b_map=lambda i, k: (k, 0, 0), o_shape=(s, d), o_blk=(tm, d), o_map=lambda i, k: (i, 0), grid=(nt, 4),
                contract=NN, out_dtype=F32)
    return u, gu


def _ffn_bwd_act(tag, du, gu, w_out4, tm):
    s, d = du.shape
    nt = s // tm
    dact = _matmul(f"ffn_dact_{tag}", du, w_out4, a_blk=(tm, d), a_map=lambda j, i, k: (i, 0), b_blk=(None, FF_BLK, d),
                   b_map=lambda j, i, k: (j, 0, 0), o_shape=(4, s, FF_BLK), o_blk=(None, tm, FF_BLK), o_map=lambda j, i, k: (j, i, 0),
                   grid=(4, nt, 1), contract=NT, out_dtype=BF16)

    def bwd_fn(gv, uv, dav):
        gv, uv, dav = gv.astype(F32), uv.astype(F32), dav.astype(F32)
        sg = jax.nn.sigmoid(gv)
        silu = gv * sg
        return (jnp.stack([dav * uv * (sg * (1.0 + gv * (1.0 - sg))), dav * silu]), silu * uv), ()

    blk = (None, tm, FF_BLK)
    (dgu, act), _ = _rowmap(
        f"ffn_dgu_{tag}", bwd_fn,
        [(gu, blk, lambda j, i: (j, i, 0)), (gu, blk, lambda j, i: (j + 4, i, 0)), (dact, blk, lambda j, i: (j, i, 0))],
        [(_sds((2, 4, s, FF_BLK), BF16), (2, None, tm, FF_BLK), lambda j, i: (0, j, i, 0)),
         (_sds((4, s, FF_BLK), BF16), blk, lambda j, i: (j, i, 0))], [], (4, nt))
    return dgu.reshape(8, s, FF_BLK), act


def _ffn_dw_in(tag, h, dgu, tm, after=None):
    s, d = h.shape
    return _matmul(f"ffn_dwin_{tag}", h, dgu, a_blk=(tm, d), a_map=lambda g, k: (k, 0), b_blk=(None, tm, FF_BLK),
                   b_map=lambda g, k: (g, k, 0), o_shape=(8, d, FF_BLK), o_blk=(None, d, FF_BLK), o_map=lambda g, k: (g, 0, 0),
                   grid=(8, s // tm), contract=TN, out_dtype=BF16, after=after)


def _ffn_dw_out(tag, act, du, tm):
    s, d = du.shape
    return _matmul(f"ffn_dwout_{tag}", act, du, a_blk=(None, tm, FF_BLK), a_map=lambda j, k: (j, k, 0), b_blk=(tm, d),
                   b_map=lambda j, k: (k, 0), o_shape=(4, FF_BLK, d), o_blk=(None, FF_BLK, d), o_map=lambda j, k: (j, 0, 0),
                   grid=(4, s // tm), contract=TN, out_dtype=BF16)


def _ffn_dh(tag, dgu, w_in8, tm, after=None):
    s = dgu.shape[1]
    d = w_in8.shape[1]
    return _matmul(f"ffn_dh_{tag}", dgu, w_in8, a_blk=(None, tm, FF_BLK), a_map=lambda i, k: (k, i, 0), b_blk=(None, d, FF_BLK),
                   b_map=lambda i, k: (k, 0, 0), o_shape=(s, d), o_blk=(tm, d), o_map=lambda i, k: (i, 0), grid=(s // tm, 8),
                   contract=NT, out_dtype=F32, after=after)


def _window_sum(x, window, transpose):
    s = x.shape[0]
    t = lax.broadcasted_iota(jnp.int32, (s, 1), 0)
    half = window // 2
    cnt = jnp.minimum(t + half, s) - jnp.maximum(t - half, 0)
    inv = 1.0 / cnt.astype(F32)
    if transpose:
        x = x * inv
        offsets = range(-half + 1, half + 1)
    else:
        offsets = range(-half, half)
    acc = jnp.zeros_like(x)
    for o in offsets:
        shifted = x if o == 0 else pltpu.roll(x, (-o) % s, 0)
        valid = jnp.logical_and(t + o >= 0, t + o < s)
        acc = acc + jnp.where(valid, shifted, 0.0)
    return acc if transpose else acc * inv


def _pool_mix(name, x, transpose, out_dtype):
    s, d = x.shape

    def body(x_ref, o_ref):
        g = pl.program_id(0)
        for gi, window in enumerate(POOL_WINDOWS):
            @pl.when(g == gi)
            def _(window=window):
                xv = x_ref[...].astype(F32)
                o_ref[...] = (_window_sum(xv, window, transpose) - xv).astype(o_ref.dtype)

    return pl.pallas_call(
        body, name=name, grid=(len(POOL_WINDOWS),), in_specs=[pl.BlockSpec((s, POOL_GROUP), lambda g: (0, g))],
        out_specs=pl.BlockSpec((s, POOL_GROUP), lambda g: (0, g)), out_shape=_sds((s, d), out_dtype), compiler_params=_params(),
    )(x)


def _pool_fwd(h, w4, bias, pscale, tm):
    s, d = h.shape
    nt = s // tm
    z = _pool_mix("pool_mix", h, False, BF16)
    v = _matmul("pool_proj", z, w4, a_blk=(tm, POOL_GROUP), a_map=lambda i, g, k: (i, g), b_blk=(None, POOL_GROUP, POOL_GROUP),
                b_map=lambda i, g, k: (g, 0, 0), o_shape=(s, d), o_blk=(tm, POOL_GROUP), o_map=lambda i, g, k: (i, g),
                grid=(nt, 4, 1), contract=NN, out_dtype=F32, bias=bias, bias_blk=(1, POOL_GROUP), bias_map=lambda i, g, k: (0, g))
    (u,), _ = _rowmap("pool_scale", lambda vv, ps: ((vv * ps,), ()), [_tile(v, tm), _row(pscale)], [_otile(s, d, F32, tm)], [],
                      (nt,))
    return u, z, v


def _pool_bwd(du, z, v, w4, pscale, tm):
    s, d = du.shape
    nt = s // tm

    def fn(duv, vv, ps):
        dv = duv * ps
        return (dv,), (_colsum(duv * vv), _colsum(dv))

    (dv,), reds = _rowmap("pool_dscale", fn, [_tile(du, tm), _tile(v, tm), _row(pscale)], [_otile(s, d, BF16, tm)],
                          [_ored(d), _ored(d)], (nt,))
    dw4 = _matmul("pool_dw", z, dv, a_blk=(tm, POOL_GROUP), a_map=lambda g, k: (k, g), b_blk=(tm, POOL_GROUP),
                  b_map=lambda g, k: (k, g), o_shape=(4, POOL_GROUP, POOL_GROUP), o_blk=(None, POOL_GROUP, POOL_GROUP),
                  o_map=lambda g, k: (g, 0, 0), grid=(4, nt), contract=TN, out_dtype=F32)
    dz = _matmul("pool_dz", dv, w4, a_blk=(tm, POOL_GROUP), a_map=lambda i, g, k: (i, g), b_blk=(None, POOL_GROUP, POOL_GROUP),
                 b_map=lambda i, g, k: (g, 0, 0), o_shape=(s, d), o_blk=(tm, POOL_GROUP), o_map=lambda i, g, k: (i, g),
                 grid=(nt, 4, 1), contract=NT, out_dtype=F32)
    dh = _pool_mix("pool_mix_t", dz, True, F32)
    return dh, dw4, reds


def _lane(shape):
    return lax.broadcasted_iota(jnp.int32, shape, 1)


def _rope_swap(v, transpose):
    half = QK_ROPE // 2
    lane = _lane(v.shape)
    up = pltpu.roll(v, v.shape[1] - half, 1)
    down = pltpu.roll(v, half, 1)
    if transpose:
        return jnp.where(lane < half, up, jnp.where(lane < QK_ROPE, -down, 0.0))
    return jnp.where(lane < half, -up, jnp.where(lane < QK_ROPE, down, 0.0))


def _rope(v, cos, sin):
    return v * cos + _rope_swap(v, False) * sin


def _rope_t(g, cos, sin):
    return g * cos + _rope_swap(g * sin, True)


def _mla_mid(lat, q_norm, kv_norm, cos_k, sin_k, tm):
    s = lat.shape[0]

    def fn(lv, qn, kn, cs, sn):
        cq = lv[:, :Q_LORA]
        ckv = lv[:, Q_LORA:Q_LORA + KV_LORA]
        kr = lv[:, Q_LORA + KV_LORA:]
        cq = cq * _rstd(cq) * qn
        ckv = ckv * _rstd(ckv) * kn
        return (cq, jnp.concatenate([ckv, _rope(kr, cs, sn)], axis=1)), ()

    (cq, kcat), _ = _rowmap("mla_mid", fn, [_tile(lat, tm), _row(q_norm), _row(kv_norm), _tile(cos_k, tm), _tile(sin_k, tm)],
                            [_otile(s, Q_LORA, BF16, tm), _otile(s, QK_PAD, BF16, tm)], [], (s // tm,))
    return cq, kcat


def _mla_mid_bwd(lat, dcq, dkcat, dv, q_norm, kv_norm, cos_k, sin_k, tm):
    s = lat.shape[0]

    def fn(lv, dq, dk, dvv, qn, kn, cs, sn):
        cq = lv[:, :Q_LORA]
        ckv = lv[:, Q_LORA:Q_LORA + KV_LORA]
        rq, rk = _rstd(cq), _rstd(ckv)
        cqn, ckn = cq * rq, ckv * rk
        a = dq * qn
        d_cq = rq * (a - cqn * jnp.mean(a * cqn, axis=-1, keepdims=True))
        dckv = dk[:, :KV_LORA] + dvv
        a2 = dckv * kn
        d_ckv = rk * (a2 - ckn * jnp.mean(a2 * ckn, axis=-1, keepdims=True))
        d_kr = _rope_t(dk[:, KV_LORA:], cs, sn)
        return (jnp.concatenate([d_cq, d_ckv, d_kr], axis=1),), (_colsum(dq * cqn), _colsum(dckv * ckn))

    (dlat,), reds = _rowmap(
        "mla_mid_bwd", fn,
        [_tile(lat, tm), _tile(dcq, tm), _tile(dkcat, tm), _tile(dv, tm), _row(q_norm), _row(kv_norm), _tile(cos_k, tm),
         _tile(sin_k, tm)],
        [_otile(s, LAT_PAD, BF16, tm)], [_ored(Q_LORA), _ored(KV_LORA)], (s // tm,))
    return dlat, reds


def _mla_q(cq, wq, wukp, cos_k, sin_k, tm):
    s = cq.shape[0]

    def body(cq_ref, wq_ref, wuk_ref, cos_ref, sin_ref, o_ref):
        aq = _dot(cq_ref[...], wq_ref[...], NN)
        qlat = _dot(aq.astype(BF16), wuk_ref[...], NN)
        roped = _rope(aq[:, KV_LORA:], cos_ref[...], sin_ref[...])
        o_ref[...] = (jnp.concatenate([qlat[:, :KV_LORA], roped], axis=1) * ATTN_SCALE).astype(o_ref.dtype)

    wblk = pl.BlockSpec((None, QK_PAD, QK_PAD), lambda h, i: (h, 0, 0))
    tblk = pl.BlockSpec((tm, KV_LORA), lambda h, i: (i, 0))
    return pl.pallas_call(
        body, name="mla_q", grid=(N_HEADS, s // tm),
        in_specs=[pl.BlockSpec((tm, Q_LORA), lambda h, i: (i, 0)), wblk, wblk, tblk, tblk],
        out_specs=pl.BlockSpec((None, tm, QK_PAD), lambda h, i: (h, i, 0)), out_shape=_sds((N_HEADS, s, QK_PAD), BF16),
        compiler_params=_params(),
    )(cq, wq, wukp, cos_k, sin_k)


def _mla_q_bwd(cq, wq, wukp, cos_k, sin_k, dqcat, tm):
    s = cq.shape[0]

    def body(cq_ref, wq_ref, wuk_ref, cos_ref, sin_ref, dq_ref, dcq_ref, dwq_ref, dwuk_ref):
        h, i = pl.program_id(0), pl.program_id(1)
        cqv = cq_ref[...]
        aq = _dot(cqv, wq_ref[...], NN).astype(BF16)
        g = dq_ref[...].astype(F32) * ATTN_SCALE
        gl, gr = g[:, :KV_LORA], g[:, KV_LORA:]
        dqlat = jnp.concatenate([gl, jnp.zeros_like(gl)], axis=1).astype(BF16)
        d_rope = _rope_t(gr, cos_ref[...], sin_ref[...])
        daq = _dot(dqlat, wuk_ref[...], NT) + jnp.concatenate([jnp.zeros_like(d_rope), d_rope], axis=1)
        daq_b = daq.astype(BF16)
        d_wuk = _dot(aq, dqlat, TN)
        d_wq = _dot(cqv, daq_b, TN)
        d_cq = _dot(daq_b, wq_ref[...], NT)
        rows = pl.ds(pl.multiple_of(i * tm, tm), tm)

        @pl.when(i == 0)
        def _():
            dwq_ref[...] = d_wq
            dwuk_ref[...] = d_wuk

        @pl.when(i != 0)
        def _():
            dwq_ref[...] += d_wq
            dwuk_ref[...] += d_wuk

        @pl.when(h == 0)
        def _():
            dcq_ref[rows, :] = d_cq

        @pl.when(h != 0)
        def _():
            dcq_ref[rows, :] += d_cq

    wblk = pl.BlockSpec((None, QK_PAD, QK_PAD), lambda h, i: (h, 0, 0))
    tblk = pl.BlockSpec((tm, KV_LORA), lambda h, i: (i, 0))
    return pl.pallas_call(
        body, name="mla_q_bwd", grid=(N_HEADS, s // tm),
        in_specs=[pl.BlockSpec((tm, Q_LORA), lambda h, i: (i, 0)), wblk, wblk, tblk, tblk,
                  pl.BlockSpec((None, tm, QK_PAD), lambda h, i: (h, i, 0))],
        out_specs=[pl.BlockSpec((s, Q_LORA), lambda h, i: (0, 0)), wblk, wblk],
        out_shape=[_sds((s, Q_LORA), F32), _sds((N_HEADS, QK_PAD, QK_PAD), F32), _sds((N_HEADS, QK_PAD, QK_PAD), F32)],
        compiler_params=_params(),
    )(cq, wq, wukp, cos_k, sin_k, dqcat)


def _flash_fwd(qcat, kcat, tq, tk):
    n_h, s, _ = qcat.shape
    n_k = s // tk

    def body(q_ref, k_ref, v_ref, o_ref, lse_ref):
        q = q_ref[...]

        def step(kk, carry):
            m, l, acc = carry
            rows = pl.ds(pl.multiple_of(kk * tk, tk), tk)
            sc = _dot(q, k_ref[rows, :], NT)
            m_new = jnp.maximum(m, jnp.max(sc, axis=1, keepdims=True))
            alpha = jnp.exp(m - m_new)
            p = jnp.exp(sc - m_new)
            l = alpha * l + jnp.sum(p, axis=1, keepdims=True)
            acc = alpha * acc + _dot(p.astype(BF16), v_ref[rows, :], NN)
            return m_new, l, acc

        init = (jnp.full((tq, 1), -1e30, F32), jnp.zeros((tq, 1), F32), jnp.zeros((tq, KV_LORA), F32))
        m, l, acc = lax.fori_loop(0, n_k, step, init)
        o_ref[...] = (acc / l).astype(o_ref.dtype)
        lse_ref[...] = m + jnp.log(l)

    return pl.pallas_call(
        body, name="mla_attn", grid=(n_h, s // tq),
        in_specs=[pl.BlockSpec((None, tq, QK_PAD), lambda h, i: (h, i, 0)), pl.BlockSpec((s, QK_PAD), lambda h, i: (0, 0)),
                  pl.BlockSpec((s, KV_LORA), lambda h, i: (0, 0))],
        out_specs=[pl.BlockSpec((None, tq, KV_LORA), lambda h, i: (h, i, 0)), pl.BlockSpec((None, tq, 1), lambda h, i: (h, i, 0))],
        out_shape=[_sds((n_h, s, KV_LORA), BF16), _sds((n_h, s, 1), F32)], compiler_params=_params(),
    )(qcat, kcat, kcat)


def _flash_bwd(qcat, kcat, o, do, lse, tq, tk):
    n_h, s, _ = qcat.shape
    n_k = s // tk

    def body(q_ref, k_ref, v_ref, o_ref, do_ref, lse_ref, dq_ref, dk_ref, dv_ref, dq_acc):
        h, i = pl.program_id(0), pl.program_id(1)

        @pl.when(jnp.logical_and(h == 0, i == 0))
        def _():
            dk_ref[...] = jnp.zeros_like(dk_ref)
            dv_ref[...] = jnp.zeros_like(dv_ref)

        q = q_ref[...]
        dov = do_ref[...]
        lse_v = lse_ref[...]
        delta = jnp.sum(dov.astype(F32) * o_ref[...].astype(F32), axis=1, keepdims=True)
        dq_acc[...] = jnp.zeros_like(dq_acc)

        def step(kk, carry):
            rows = pl.ds(pl.multiple_of(kk * tk, tk), tk)
            k = k_ref[rows, :]
            p = jnp.exp(_dot(q, k, NT) - lse_v)
            dp = _dot(dov, v_ref[rows, :], NT)
            ds = (p * (dp - delta)).astype(BF16)
            dq_acc[...] += _dot(ds, k, NN)
            dv_ref[rows, :] += _dot(p.astype(BF16), dov, TN)
            dk_ref[rows, :] += _dot(ds, q, TN)
            return carry

        lax.fori_loop(0, n_k, step, 0)
        dq_ref[...] = dq_acc[...].astype(dq_ref.dtype)

    qblk = pl.BlockSpec((None, tq, QK_PAD), lambda h, i: (h, i, 0))
    oblk = pl.BlockSpec((None, tq, KV_LORA), lambda h, i: (h, i, 0))
    return pl.pallas_call(
        body, name="mla_attn_bwd", grid=(n_h, s // tq),
        in_specs=[qblk, pl.BlockSpec((s, QK_PAD), lambda h, i: (0, 0)), pl.BlockSpec((s, KV_LORA), lambda h, i: (0, 0)), oblk, oblk,
                  pl.BlockSpec((None, tq, 1), lambda h, i: (h, i, 0))],
        out_specs=[qblk, pl.BlockSpec((s, QK_PAD), lambda h, i: (0, 0)), pl.BlockSpec((s, KV_LORA), lambda h, i: (0, 0))],
        out_shape=[_sds((n_h, s, QK_PAD), BF16), _sds((s, QK_PAD), F32), _sds((s, KV_LORA), F32)],
        scratch_shapes=[pltpu.VMEM((tq, QK_PAD), F32)], compiler_params=_params(),
    )(qcat, kcat, kcat, o, do, lse)


def _mla_fwd(h, wts, cos_k, sin_k, tm):
    s, d = h.shape
    nt = s // tm
    lat = _matmul("mla_lat", h, wts["w_in"], a_blk=(tm, d), a_map=lambda i, k: (i, 0), b_blk=(d, LAT_PAD), b_map=lambda i, k: (0, 0),
                  o_shape=(s, LAT_PAD), o_blk=(tm, LAT_PAD), o_map=lambda i, k: (i, 0), grid=(nt, 1), contract=NN, out_dtype=F32)
    cq, kcat = _mla_mid(lat, wts["q_norm"], wts["kv_norm"], cos_k, sin_k, tm)
    qcat = _mla_q(cq, wts["wq"], wts["wukp"], cos_k, sin_k, tm)
    o_lat, lse = _flash_fwd(qcat, kcat, tm, tm)
    o = _matmul("mla_uv", o_lat, wts["wuv2"], a_blk=(None, tm, KV_LORA), a_map=lambda i, p, r: (2 * p + r, i, 0),
                b_blk=(None, KV_LORA, 2 * V_HEAD), b_map=lambda i, p, r: (2 * p + r, 0, 0), o_shape=(s, d), o_blk=(tm, 2 * V_HEAD),
                o_map=lambda i, p, r: (i, p), grid=(nt, N_HEADS // 2, 2), contract=NN, out_dtype=BF16)
    u = _matmul("mla_out", o, wts["w_o"], a_blk=(tm, d), a_map=lambda i, k: (i, 0), b_blk=(d, d), b_map=lambda i, k: (0, 0),
                o_shape=(s, d), o_blk=(tm, d), o_map=lambda i, k: (i, 0), grid=(nt, 1), contract=NN, out_dtype=F32)
    return u, (lat, cq, kcat, qcat, o_lat, lse, o)


def _mla_bwd(du, h, saved, wts, cos_k, sin_k, tm):
    lat, cq, kcat, qcat, o_lat, lse, o = saved
    s, d = h.shape
    nt = s // tm
    do = _matmul("mla_do", du, wts["w_o"], a_blk=(tm, d), a_map=lambda i, k: (i, 0), b_blk=(d, d), b_map=lambda i, k: (0, 0),
                 o_shape=(s, d), o_blk=(tm, d), o_map=lambda i, k: (i, 0), grid=(nt, 1), contract=NT, out_dtype=BF16)
    dw_o = _matmul("mla_dwo", o, du, a_blk=(tm, d), a_map=lambda k: (k, 0), b_blk=(tm, d), b_map=lambda k: (k, 0),
                   o_shape=(d, d), o_blk=(d, d), o_map=lambda k: (0, 0), grid=(nt,), contract=TN, out_dtype=F32)
    do_lat = _matmul("mla_dolat", do, wts["wuv2"], a_blk=(tm, 2 * V_HEAD), a_map=lambda hh, i, k: (i, hh // 2),
                     b_blk=(None, KV_LORA, 2 * V_HEAD), b_map=lambda hh, i, k: (hh, 0, 0), o_shape=(N_HEADS, s, KV_LORA),
                     o_blk=(None, tm, KV_LORA), o_map=lambda hh, i, k: (hh, i, 0), grid=(N_HEADS, nt, 1), contract=NT, out_dtype=BF16)
    dwuv2 = _matmul("mla_dwuv", o_lat, do, a_blk=(None, tm, KV_LORA), a_map=lambda hh, k: (hh, k, 0), b_blk=(tm, 2 * V_HEAD),
                    b_map=lambda hh, k: (k, hh // 2), o_shape=(N_HEADS, KV_LORA, 2 * V_HEAD), o_blk=(None, KV_LORA, 2 * V_HEAD),
                    o_map=lambda hh, k: (hh, 0, 0), grid=(N_HEADS, nt), contract=TN, out_dtype=F32)
    dqcat, dkcat, dv = _flash_bwd(qcat, kcat, o_lat, do_lat, lse, tm, tm)
    dcq, dwq, dwukp = _mla_q_bwd(cq, wts["wq"], wts["wukp"], cos_k, sin_k, dqcat, tm)
    dlat, (dqn, dkn) = _mla_mid_bwd(lat, dcq, dkcat, dv, wts["q_norm"], wts["kv_norm"], cos_k, sin_k, tm)
    dh = _matmul("mla_dh", dlat, wts["w_in"], a_blk=(tm, LAT_PAD), a_map=lambda i, k: (i, 0), b_blk=(d, LAT_PAD),
                 b_map=lambda i, k: (0, 0), o_shape=(s, d), o_blk=(tm, d), o_map=lambda i, k: (i, 0), grid=(nt, 1), contract=NT,
                 out_dtype=F32)
    dw_in = _matmul("mla_dwin", h, dlat, a_blk=(tm, d), a_map=lambda k: (k, 0), b_blk=(tm, LAT_PAD), b_map=lambda k: (k, 0),
                    o_shape=(d, LAT_PAD), o_blk=(d, LAT_PAD), o_map=lambda k: (0, 0), grid=(nt,), contract=TN, out_dtype=F32)
    return dh, dict(w_in=dw_in, wq=dwq, wukp=dwukp, wuv2=dwuv2, w_o=dw_o, q_norm=dqn, kv_norm=dkn)


def _adamw(name, parts, w, m, v):
    n_parts, r, c = parts.shape
    tr = r
    for cand in (256, 128, 64, 32, 16, 8):
        if r > cand and r % cand == 0:
            tr = cand
            break

    def body(p_ref, w_ref, m_ref, v_ref, g_ref, d_ref, nm_ref, nv_ref):
        g = p_ref[0].astype(F32)
        for k in range(1, n_parts):
            g = g + p_ref[k].astype(F32)
        nm = ADAM_B1 * m_ref[...] + (1.0 - ADAM_B1) * g
        nv = ADAM_B2 * v_ref[...] + (1.0 - ADAM_B2) * (g * g)
        m_hat = nm / (1.0 - ADAM_B1 ** ADAM_STEP)
        v_hat = nv / (1.0 - ADAM_B2 ** ADAM_STEP)
        g_ref[...] = g
        d_ref[...] = -ADAM_LR * (m_hat / (jnp.sqrt(v_hat) + ADAM_EPS) + ADAM_WD * w_ref[...])
        nm_ref[...] = nm
        nv_ref[...] = nv

    blk = pl.BlockSpec((tr, c), lambda i: (i, 0))
    return pl.pallas_call(
        body, name=name, grid=(r // tr,), in_specs=[pl.BlockSpec((n_parts, tr, c), lambda i: (0, i, 0)), blk, blk, blk],
        out_specs=[blk] * 4, out_shape=[_sds((r, c), F32)] * 4, compiler_params=_params(),
    )(parts, w, m, v)


def _adamw_slab(name, parts, w, m, v, bufs, f):
    n_parts, r, c = parts.shape
    tr = max(t for t in range(8, 257, 8) if r % t == 0)

    def body(p_ref, w_ref, m_ref, v_ref, *rest):
        g_ref, d_ref, nm_ref, nv_ref = rest[4:]
        g = p_ref[0].astype(F32)
        for k in range(1, n_parts):
            g = g + p_ref[k].astype(F32)
        nm = ADAM_B1 * m_ref[...] + (1.0 - ADAM_B1) * g
        nv = ADAM_B2 * v_ref[...] + (1.0 - ADAM_B2) * (g * g)
        m_hat = nm / (1.0 - ADAM_B1 ** ADAM_STEP)
        v_hat = nv / (1.0 - ADAM_B2 ** ADAM_STEP)
        g_ref[...] = g
        d_ref[...] = -ADAM_LR * (m_hat / (jnp.sqrt(v_hat) + ADAM_EPS) + ADAM_WD * w_ref[...])
        nm_ref[...] = nm
        nv_ref[...] = nv

    blk = pl.BlockSpec((None, tr, c), lambda i: (f, i, 0))
    return pl.pallas_call(
        body, name=name, grid=(r // tr,),
        in_specs=[pl.BlockSpec((n_parts, tr, c), lambda i: (0, i, 0)), blk, blk, blk] + [pl.BlockSpec(memory_space=pl.ANY)] * 4,
        out_specs=[blk] * 4, out_shape=[_sds(w.shape, F32)] * 4, input_output_aliases={4 + j: j for j in range(4)},
        compiler_params=_params(),
    )(parts, w, m, v, *bufs)


def _mesh_pos():
    return lax.axis_index("x"), lax.axis_index("y"), lax.axis_index("c")


def _flip(pos, mask):
    return tuple(1 - p if (mask >> (2 - b)) & 1 else p for b, p in enumerate(pos))


def _index(pos):
    return 4 * pos[0] + 2 * pos[1] + pos[2]


def _all_gather(name, xs, after=None):
    n = len(xs)
    extra = [] if after is None else [after]

    def body(*refs):
        x_refs, o_refs = refs[:n], refs[n + len(extra):2 * n + len(extra)]
        send_sems, recv_sems, local_sems = refs[2 * n + len(extra):]
        me = _mesh_pos()
        sibling = _flip(me, 1)
        others = [_flip(me, 4), _flip(me, 2), _flip(me, 6)]

        def copy(k, j, block, to, src=None):
            dst = o_refs[k].at[_index(block)]
            return pltpu.make_async_remote_copy(
                src_ref=dst if src is None else src, dst_ref=dst, send_sem=send_sems.at[k, j], recv_sem=recv_sems.at[k, j],
                device_id=to, device_id_type=MESH)

        local = [pltpu.make_async_copy(x_refs[k], o_refs[k].at[_index(me)], local_sems.at[k]) for k in range(n)]
        for cp in local:
            cp.start()
        first = []
        for k in range(n):
            first.append(copy(k, 0, me, sibling, src=x_refs[k]))
            first += [copy(k, 1 + j, me, other, src=x_refs[k]) for j, other in enumerate(others)]
        for cp in first:
            cp.start()
        passed = []
        for j, other in enumerate(others):
            for k in range(n):
                copy(k, 1 + j, other, me).wait_recv()
                cp = copy(k, 4 + j, other, sibling)
                cp.start()
                passed.append(cp)
        for k in range(n):
            copy(k, 0, sibling, me).wait_recv()
        for j, other in enumerate(others):
            for k in range(n):
                copy(k, 4 + j, _flip(other, 1), me).wait_recv()
        for cp in first + passed:
            cp.wait_send()
        for cp in local:
            cp.wait()

    any_spec = pl.BlockSpec(memory_space=pl.ANY)
    return pl.pallas_call(
        body, name=name, in_specs=[any_spec] * (n + len(extra)), out_specs=[any_spec] * n,
        out_shape=[_sds((N_DEV,) + x.shape, x.dtype) for x in xs],
        scratch_shapes=[pltpu.SemaphoreType.DMA((n, 7)), pltpu.SemaphoreType.DMA((n, 7)), pltpu.SemaphoreType.DMA((n,))],
    )(*xs, *extra)


def _all_to_all(name, groups):
    flat = [(gi, f) for gi, grp in enumerate(groups) for f in range(len(grp))]
    n = len(flat)
    n_groups = len(groups)

    def body(*refs):
        x_refs, o_refs = refs[:n], refs[n:n + n_groups]
        send_sems, recv_sems, local_sems = refs[n + n_groups:]
        me = _mesh_pos()
        local, sends, recvs = [], [], []
        for k, (gi, f) in enumerate(flat):
            local.append(pltpu.make_async_copy(x_refs[k].at[_index(me)], o_refs[gi].at[_index(me), f], local_sems.at[k]))
            for mask in range(1, N_DEV):
                peer = _flip(me, mask)
                sends.append(pltpu.make_async_remote_copy(
                    src_ref=x_refs[k].at[_index(peer)], dst_ref=o_refs[gi].at[_index(me), f], send_sem=send_sems.at[k, mask - 1],
                    recv_sem=recv_sems.at[k, mask - 1], device_id=peer, device_id_type=MESH))
                recvs.append(pltpu.make_async_remote_copy(
                    src_ref=x_refs[k].at[_index(me)], dst_ref=o_refs[gi].at[_index(peer), f], send_sem=send_sems.at[k, mask - 1],
                    recv_sem=recv_sems.at[k, mask - 1], device_id=peer, device_id_type=MESH))
        for cp in local + sends:
            cp.start()
        for cp in recvs:
            cp.wait_recv()
        for cp in sends:
            cp.wait_send()
        for cp in local:
            cp.wait()

    any_spec = pl.BlockSpec(memory_space=pl.ANY)
    return pl.pallas_call(
        body, name=name, in_specs=[any_spec] * n, out_specs=[any_spec] * n_groups,
        out_shape=[_sds((N_DEV, len(grp)) + grp[0].shape[1:], grp[0].dtype) for grp in groups],
        scratch_shapes=[pltpu.SemaphoreType.DMA((n, 7)), pltpu.SemaphoreType.DMA((n, 7)), pltpu.SemaphoreType.DMA((n,))],
    )(*[a for grp in groups for a in grp])


def _split_copies(kind, outgoing, x_refs, land_refs, send_sems, recv_sems):
    me = _mesh_pos()
    copies = []
    for k, (x_ref, land_ref) in enumerate(zip(x_refs, land_refs)):
        for mask in range(1, N_DEV):
            peer = _flip(me, mask)
            sem = k * (N_DEV - 1) + mask - 1
            copies.append(pltpu.make_async_remote_copy(
                src_ref=x_ref if kind == "gather" else x_ref.at[_index(peer)], dst_ref=land_ref.at[_index(me if outgoing else peer)],
                send_sem=send_sems.at[sem], recv_sem=recv_sems.at[sem], device_id=peer, device_id_type=MESH))
    return copies


_HBM_SPEC = pl.BlockSpec(memory_space=pltpu.HBM)
_SEM_SPEC = pl.BlockSpec(memory_space=pltpu.SEMAPHORE)
_EFFECT = pltpu.SideEffectType.DATAFLOW_SIDE_EFFECTING


def _split_start(name, kind, xs, after=None):
    n = len(xs)
    extra = [] if after is None else [after]
    lands = [lax.empty(((N_DEV,) + x.shape) if kind == "gather" else x.shape, x.dtype) for x in xs]

    def body(*refs):
        x_refs, land_refs = refs[:n], refs[n:2 * n]
        send_sems, recv_sems = refs[2 * n + len(extra)], refs[2 * n + len(extra) + 1]
        token = refs[-1]
        for cp in _split_copies(kind, True, x_refs, land_refs, send_sems, recv_sems):
            cp.start()
        token[...] = jnp.zeros_like(token)

    hbm = [pltpu.HBM(a.shape, a.dtype) for a in list(xs) + lands]
    res = pl.pallas_call(
        body, name=name,
        out_shape=[pltpu.SemaphoreType.DMA((n * (N_DEV - 1),)), pltpu.SemaphoreType.DMA((n * (N_DEV - 1),))] + hbm + [_sds((8, 128), F32)],
        in_specs=[_HBM_SPEC] * (2 * n) + [pl.BlockSpec(memory_space=pl.ANY)] * len(extra),
        out_specs=[_SEM_SPEC, _SEM_SPEC] + [_HBM_SPEC] * (2 * n) + [pl.BlockSpec(memory_space=pltpu.VMEM)],
        input_output_aliases={j: 2 + j for j in range(2 * n)}, compiler_params=pltpu.CompilerParams(has_side_effects=_EFFECT),
    )(*[pltpu.with_memory_space_constraint(a, pltpu.HBM) for a in list(xs) + lands], *extra)
    return (kind, n, res[0], res[1], res[2:2 + 2 * n]), res[-1]


def _split_wait(name, state, after):
    kind, n, send_sems_in, recv_sems_in, thru = state

    def body(*refs):
        x_refs, land_refs = refs[:n], refs[n:2 * n]
        send_sems, recv_sems = refs[2 * n], refs[2 * n + 1]
        for cp in _split_copies(kind, True, x_refs, land_refs, send_sems, recv_sems):
            cp.wait_send()
        for cp in _split_copies(kind, False, x_refs, land_refs, send_sems, recv_sems):
            cp.wait_recv()

    res = pl.pallas_call(
        body, name=name, out_shape=[pltpu.HBM(a.shape, a.dtype) for a in thru],
        in_specs=[_HBM_SPEC] * (2 * n) + [_SEM_SPEC, _SEM_SPEC, pl.BlockSpec(memory_space=pl.ANY)], out_specs=[_HBM_SPEC] * (2 * n),
        input_output_aliases={j: j for j in range(2 * n)}, compiler_params=pltpu.CompilerParams(has_side_effects=_EFFECT),
    )(*thru, send_sems_in, recv_sems_in, after)
    return res[n:]


def _with_own(land, own):
    me = _index(_mesh_pos())
    return lax.dynamic_update_slice(land, own[None], (me,) + (0,) * own.ndim)


def _rope_tables(s):
    inv = 1.0 / (ROPE_THETA ** (jnp.arange(0, QK_ROPE, 2, dtype=F32) / QK_ROPE))
    ang = jnp.arange(s, dtype=F32)[:, None] * inv[None, :]
    pad = jnp.zeros((s, KV_LORA - QK_ROPE), F32)
    return (jnp.concatenate([jnp.cos(ang), jnp.cos(ang), pad], axis=1), jnp.concatenate([jnp.sin(ang), jnp.sin(ang), pad], axis=1))


def _row_of(v):
    return v.reshape(1, -1)


def kernel(x, c, ada_w, ada_b, norm_g, ffn_w_in, ffn_w_out, pool_w, pool_b, pool_scale, mla_w_in, mla_q_norm, mla_kv_norm, mla_w_uq, mla_w_uk, mla_w_uv, mla_w_o, loss_target, m_ada_w, m_ada_b, m_norm_g, m_ffn_w_in, m_ffn_w_out, m_pool_w, m_pool_b, m_pool_scale, m_mla_w_in, m_mla_q_norm, m_mla_kv_norm, m_mla_w_uq, m_mla_w_uk, m_mla_w_uv, m_mla_w_o, v_ada_w, v_ada_b, v_norm_g, v_ffn_w_in, v_ffn_w_out, v_pool_w, v_pool_b, v_pool_scale, v_mla_w_in, v_mla_q_norm, v_mla_kv_norm, v_mla_w_uq, v_mla_w_uk, v_mla_w_uv, v_mla_w_o):
    s, d = x.shape[1], x.shape[2]
    tm = min(512, s)
    tr = min(256, s)
    me = 4 * lax.axis_index("x") + 2 * lax.axis_index("y") + lax.axis_index("c")
    x0 = x.reshape(s, d)
    target = loss_target.reshape(s, d)
    n_mod = ada_w.shape[2] * N_DEV // d
    mod_blk = ada_w.shape[2]

    small = jnp.concatenate([c.reshape(-1), norm_g.reshape(-1), pool_b.reshape(-1), mla_q_norm.reshape(-1)]).reshape(1, -1)
    w_in_loc = [ffn_w_in[i, f].astype(BF16) for i in range(2) for f in range(2)]
    w_out_loc = [ffn_w_out[i, f].astype(BF16) for i in range(2) for f in range(2)]
    (small_all,) = _all_gather("gather_small", [small])
    small_all = small_all.reshape(N_DEV, -1)
    c_all = small_all[:, :d]
    off = d
    g_all = small_all[:, off:off + 12 * (d // N_DEV)].reshape(N_DEV, 2, 6, d // N_DEV).transpose(1, 2, 0, 3).reshape(2, 6, d)
    off += 12 * (d // N_DEV)
    pool_b_all = small_all[:, off:off + 4 * 32].reshape(N_DEV, 4, 32).transpose(1, 0, 2).reshape(1, d)
    off += 4 * 32
    q_norm_all = small_all[:, off:off + 32].reshape(1, Q_LORA)
    kv_norm_row = mla_kv_norm.reshape(1, KV_LORA)
    pscale_row = pool_scale.reshape(1, d)

    even = (jnp.arange(N_HEADS) % 2 == 0)[:, None, None]
    cos_k, sin_k = _rope_tables(s)

    def mla_weights(mla_w_in_all, mla_w_uq_all, mla_w_o_all):
        uq = mla_w_uq_all.reshape(Q_LORA, N_HEADS, QK_NOPE + QK_ROPE).transpose(1, 0, 2)
        zq = jnp.zeros((N_HEADS, Q_LORA, QK_NOPE), BF16)
        wq = jnp.concatenate(
            [uq[:, :, :QK_NOPE], zq, uq[:, :, QK_NOPE:], jnp.zeros((N_HEADS, Q_LORA, QK_PAD - KV_LORA - QK_ROPE), BF16)], axis=2)
        wukp = jnp.pad(mla_w_uk[0].transpose(1, 2, 0).astype(BF16), ((0, 0), (0, QK_PAD - QK_NOPE), (0, QK_PAD - KV_LORA)))
        uv = mla_w_uv[0].transpose(1, 0, 2).astype(BF16)
        wuv2 = jnp.where(even, jnp.concatenate([uv, jnp.zeros_like(uv)], axis=2), jnp.concatenate([jnp.zeros_like(uv), uv], axis=2))
        return dict(w_in=jnp.pad(mla_w_in_all.reshape(d, -1), ((0, 0), (0, LAT_PAD - mla_w_in.shape[2]))), wq=wq, wukp=wukp,
                    wuv2=wuv2, w_o=mla_w_o_all.reshape(d, d), q_norm=q_norm_all, kv_norm=kv_norm_row)

    (sc_all,), _ = _rowmap("ada_silu", lambda cv: ((cv * jax.nn.sigmoid(cv),), ()), [(c_all, (N_DEV, d), lambda i: (0, 0))],
                           [(_sds((N_DEV, d), F32), (N_DEV, d), lambda i: (0, 0))], [], (1,))
    ada_b_loc = lax.dynamic_slice_in_dim(ada_b, me * mod_blk, mod_blk, axis=1).reshape(2, 1, mod_blk)
    m_pad = 2 * N_DEV
    modp = _matmul("ada_mod", jnp.pad(sc_all, ((0, m_pad - N_DEV), (0, 0))), ada_w, a_blk=(m_pad, d), a_map=lambda i, k: (0, 0),
                   b_blk=(None, d, mod_blk), b_map=lambda i, k: (i, 0, 0), o_shape=(2, m_pad, mod_blk), o_blk=(None, m_pad, mod_blk),
                   o_map=lambda i, k: (i, 0, 0), grid=(2, 1), contract=NN, out_dtype=F32, bias=ada_b_loc, bias_blk=(None, 1, mod_blk),
                   bias_map=lambda i, k: (i, 0, 0))[:, :N_DEV]
    modp_all, w_in_first, w_out_first = _all_gather("gather_first", [modp.reshape(2 * N_DEV, mod_blk), w_in_loc[0], w_out_loc[0]])
    later_a = [w_in_loc[1], w_out_loc[1], pool_w.reshape(-1, POOL_GROUP).astype(BF16)]
    later_b = [w_in_loc[2], w_out_loc[2], w_in_loc[3], w_out_loc[3], mla_w_in[0].astype(BF16),
               mla_w_uq.reshape(mla_w_uq.shape[1], -1).astype(BF16), mla_w_o[0].astype(BF16)]
    state_a, token_a = _split_start("gather_start_a", "gather", later_a, after=modp_all)
    state_b, token_b = _split_start("gather_start_b", "gather", later_b, after=token_a)
    w_in8 = [w_in_first, None, None, None]
    w_out4 = [w_out_first.reshape(4, FF_BLK, d), None, None, None]
    mod = lax.dynamic_index_in_dim(modp_all.reshape(N_DEV, 2, N_DEV, mod_blk), me, axis=2, keepdims=False)
    mod = mod.transpose(1, 0, 2).reshape(2, n_mod, d) + (token_a[0, 0] + token_b[0, 0])

    saved = []
    xs = x0
    w4 = mla_wts = None
    for i in range(2):
        for sub in range(3):
            if (i, sub) == (0, 1):
                lands = _split_wait("gather_wait_a", state_a, xs)
                w_in8[1] = _with_own(lands[0], later_a[0])
                w_out4[1] = _with_own(lands[1], later_a[1]).reshape(4, FF_BLK, d)
                w4 = _with_own(lands[2], later_a[2]).reshape(N_DEV, 4, 32, POOL_GROUP).transpose(1, 0, 2, 3)
                w4 = w4.reshape(4, POOL_GROUP, POOL_GROUP)
            if (i, sub) == (1, 0):
                lands = [_with_own(land, own) for land, own in zip(_split_wait("gather_wait_b", state_b, xs), later_b)]
                w_in8[2], w_in8[3] = lands[0], lands[2]
                w_out4[2], w_out4[3] = lands[1].reshape(4, FF_BLK, d), lands[3].reshape(4, FF_BLK, d)
                mla_wts = mla_weights(*lands[4:])
            shift, scale, gate = (_row_of(mod[i, 3 * sub + j]) for j in range(3))
            g_pre, g_post = _row_of(g_all[i, 2 * sub]), _row_of(g_all[i, 2 * sub + 1])
            tag = f"{i}{sub}"
            if sub != 1:
                f = sub // 2
                h = _prenorm(f"prenorm_{tag}", xs, g_pre, scale, shift, BF16, tr)
                u, extra = _ffn_fwd(tag, h, w_in8[2 * i + f], w_out4[2 * i + f], tm)
                weight = 0.5
            elif i == 0:
                h = _prenorm(f"prenorm_{tag}", xs, g_pre, scale, shift, F32, tr)
                u, z, v = _pool_fwd(h, w4, pool_b_all, pscale_row, tm)
                extra = (z, v)
                weight = 1.0
            else:
                h = _prenorm(f"prenorm_{tag}", xs, g_pre, scale, shift, BF16, tr)
                u, extra = _mla_fwd(h, mla_wts, cos_k, sin_k, tm)
                weight = 1.0
            saved.append((xs, h, u, extra, (shift, scale, gate, g_pre, g_post), weight))
            xs = _postnorm(f"postnorm_{tag}", xs, u, g_post, gate, weight, tr)

    def loss_fn(yv, tv):
        e = yv - tv
        return (e * (1.0 / d),), (_colsum(e * e),)

    (dx,), (sq,) = _rowmap("loss", loss_fn, [_tile(xs, tr), _tile(target, tr)], [_otile(s, d, F32, tr)], [_ored(d)], (s // tr,))
    loss = lax.psum(0.5 * jnp.sum(sq) / d, AXES)

    d_mod = [[None] * n_mod for _ in range(2)]
    d_g = [[None] * 6 for _ in range(2)]
    sent = {}
    pool_grads = mla_grads = None

    def start_scatter(key, arrays):
        state, token = _split_start(f"scatter_start_{key}", "scatter", arrays)
        sent[key] = (state, arrays)
        return token

    for i in (1, 0):
        for sub in (2, 1, 0):
            xin, h, u, extra, (shift, scale, gate, g_pre, g_post), weight = saved[3 * i + sub]
            tag = f"{i}{sub}"
            du, (dgate, dgpost) = _postnorm_bwd(f"postnorm_bwd_{tag}", dx, u, g_post, gate, weight, F32 if (sub == 1 and i == 0) else BF16, tr)
            if sub != 1:
                k = 2 * i + sub // 2
                dgu, act = _ffn_bwd_act(tag, du, extra, w_out4[k], tm)
                dw_out = _ffn_dw_out(tag, act, du, tm).reshape(N_DEV, FF_BLK // 2, d)
                if k == 0:
                    d_pool_w = pool_grads[0].reshape(4, N_DEV, 32, POOL_GROUP).transpose(1, 0, 2, 3).reshape(N_DEV, 4 * 32, POOL_GROUP)
                    token = start_scatter(tag + "_out", [dw_out, d_pool_w])
                    token = start_scatter(tag + "_in", [_ffn_dw_in(tag, h, dgu, tm, after=token)])
                else:
                    token = start_scatter(tag, [_ffn_dw_in(tag, h, dgu, tm), dw_out])
                dh = _ffn_dh(tag, dgu, w_in8[k], tm, after=token)
            elif i == 0:
                dh, dw4, (dpscale, dpb) = _pool_bwd(du, extra[0], extra[1], w4, pscale_row, tm)
                pool_grads = (dw4, dpscale, dpb)
            else:
                dh, mla_grads = _mla_bwd(du, h, extra, mla_wts, cos_k, sin_k, tm)
                dwq = mla_grads["wq"]
                d_uq = jnp.concatenate([dwq[:, :, :QK_NOPE], dwq[:, :, KV_LORA:KV_LORA + QK_ROPE]], axis=2).transpose(1, 0, 2)
                token = start_scatter("mla", [mla_grads["w_in"][:, :mla_w_in.shape[2]].reshape(N_DEV, d // N_DEV, -1),
                                              d_uq.reshape(N_DEV, Q_LORA // N_DEV, -1), mla_grads["w_o"].reshape(N_DEV, d // N_DEV, d)])
                dwukp, dwuv2 = mla_grads["wukp"], mla_grads["wuv2"]
                d_uk = dwukp[:, :QK_NOPE, :KV_LORA].transpose(2, 0, 1).reshape(KV_LORA, -1)
                d_uv = jnp.where(even, dwuv2[:, :, :V_HEAD], dwuv2[:, :, V_HEAD:]).transpose(1, 0, 2).reshape(KV_LORA, -1)
                state_ukv, token_ukv = _split_start("gather_start_ukv", "gather", [d_uk, d_uv], after=token)
                g_pre = g_pre + token_ukv[0, 0]
            dx, (dshift, dscale, dgpre) = _prenorm_bwd(f"prenorm_bwd_{tag}", dh, xin, dx, g_pre, scale, tr)
            d_mod[i][3 * sub:3 * sub + 3] = [dshift, dscale, dgate]
            d_g[i][2 * sub:2 * sub + 2] = [dgpre, dgpost]
    grad_x = dx.reshape(x.shape)

    def upd(name, parts, w, m, v):
        shape = w.shape
        r, cdim = parts.shape[1], parts.shape[2]
        return [o.reshape(shape) for o in _adamw(name, parts, w.reshape(r, cdim), m.reshape(r, cdim), v.reshape(r, cdim))]

    def landed(key, after):
        state, arrays = sent[key]
        lands = _split_wait(f"scatter_wait_{key}", state, after)
        return [_with_own(land, lax.dynamic_index_in_dim(a, me, 0, keepdims=False)) for land, a in zip(lands, arrays)]

    res = {}
    w_in_s, m_in_s, v_in_s = (a.reshape(4, d, FF_BLK) for a in (ffn_w_in, m_ffn_w_in, v_ffn_w_in))
    w_out_s, m_out_s, v_out_s = (a.reshape(4, FF_BLK // 2, d) for a in (ffn_w_out, m_ffn_w_out, v_ffn_w_out))
    bufs_in = [lax.empty(w_in_s.shape, F32) for _ in range(4)]
    bufs_out = [lax.empty(w_out_s.shape, F32) for _ in range(4)]
    for key, k in (("12", 3), ("mla", None), ("10", 2), ("02", 1)):
        parts = landed(key, grad_x)
        if k is None:
            res["mla_w_in"] = upd("adam_mla_w_in", parts[0], mla_w_in, m_mla_w_in, v_mla_w_in)
            res["mla_w_uq"] = upd("adam_mla_w_uq", parts[1], mla_w_uq, m_mla_w_uq, v_mla_w_uq)
            res["mla_w_o"] = upd("adam_mla_w_o", parts[2], mla_w_o, m_mla_w_o, v_mla_w_o)
            uk_all, uv_all = (_with_own(land, own) for land, own in zip(_split_wait("gather_wait_ukv", state_ukv, grad_x), (d_uk, d_uv)))
            res["mla_w_uk"] = upd("adam_mla_w_uk", uk_all, mla_w_uk, m_mla_w_uk, v_mla_w_uk)
            res["mla_w_uv"] = upd("adam_mla_w_uv", uv_all, mla_w_uv, m_mla_w_uv, v_mla_w_uv)
            continue
        bufs_in = _adamw_slab(f"adam_ffn_w_in_{key}", parts[0], w_in_s, m_in_s, v_in_s, bufs_in, k)
        bufs_out = _adamw_slab(f"adam_ffn_w_out_{key}", parts[1], w_out_s, m_out_s, v_out_s, bufs_out, k)

    dw4, dpscale, dpb = pool_grads
    d_mod_row = jnp.concatenate([jnp.concatenate(r, axis=1) for r in d_mod], axis=1)
    d_g_row = jnp.concatenate([jnp.concatenate(r, axis=1) for r in d_g], axis=1)
    small_g = jnp.concatenate([d_mod_row, d_g_row, dpb, dpscale, mla_grads["q_norm"], mla_grads["kv_norm"]], axis=1)
    (small_g_all,) = _all_gather("gather_small_grads", [small_g], after=bufs_out[0])
    small_g_all = small_g_all.reshape(N_DEV, -1)
    n_m = 2 * n_mod * d
    d_mod_all = small_g_all[:, :n_m].reshape(N_DEV, 2, n_mod * d)
    rest = small_g_all[:, n_m:]
    p_norm_g = lax.dynamic_slice_in_dim(rest[:, :12 * d].reshape(N_DEV, 12, d), me * (d // N_DEV), d // N_DEV, axis=2)
    p_pool_b = lax.dynamic_slice_in_dim(rest[:, 12 * d:13 * d].reshape(N_DEV, 4, POOL_GROUP), me * 32, 32, axis=2)
    p_pool_scale = rest[:, 13 * d:14 * d].reshape(N_DEV, 1, d)
    p_q_norm = lax.dynamic_slice_in_dim(rest[:, 14 * d:14 * d + Q_LORA], me * 32, 32, axis=1).reshape(N_DEV, 1, 32)
    p_kv_norm = rest[:, 14 * d + Q_LORA:].reshape(N_DEV, 1, KV_LORA)

    d_mod_loc = lax.dynamic_slice_in_dim(d_mod_all, me * mod_blk, mod_blk, axis=2).transpose(1, 0, 2)
    k_pad = 128
    sc_t = jnp.pad(sc_all.T, ((0, 0), (0, k_pad - N_DEV)))
    d_ada_w = _matmul("ada_dw", sc_t, jnp.pad(d_mod_loc, ((0, 0), (0, k_pad - N_DEV), (0, 0))), a_blk=(d, k_pad),
                      a_map=lambda i, k: (0, 0), b_blk=(None, k_pad, mod_blk), b_map=lambda i, k: (i, 0, 0), o_shape=(2, d, mod_blk),
                      o_blk=(None, d, mod_blk), o_map=lambda i, k: (i, 0, 0), grid=(2, 1), contract=NN, out_dtype=F32)
    res["ada_w"] = upd("adam_ada_w", d_ada_w.reshape(1, 2 * d, mod_blk), ada_w, m_ada_w, v_ada_w)
    res["ada_b"] = upd("adam_ada_b", d_mod_all.reshape(N_DEV, 2, n_mod * d), ada_b, m_ada_b, v_ada_b)
    res["norm_g"] = upd("adam_norm_g", p_norm_g, norm_g, m_norm_g, v_norm_g)
    res["pool_b"] = upd("adam_pool_b", p_pool_b, pool_b, m_pool_b, v_pool_b)
    res["pool_scale"] = upd("adam_pool_scale", p_pool_scale, pool_scale, m_pool_scale, v_pool_scale)
    res["mla_q_norm"] = upd("adam_mla_q_norm", p_q_norm, mla_q_norm, m_mla_q_norm, v_mla_q_norm)
    res["mla_kv_norm"] = upd("adam_mla_kv_norm", p_kv_norm, mla_kv_norm, m_mla_kv_norm, v_mla_kv_norm)

    p_out, p_pool_w = landed("00_out", res["ada_w"][1])
    bufs_out = _adamw_slab("adam_ffn_w_out_00", p_out, w_out_s, m_out_s, v_out_s, bufs_out, 0)
    res["pool_w"] = upd("adam_pool_w", p_pool_w, pool_w, m_pool_w, v_pool_w)
    (p_in,) = landed("00_in", res["pool_w"][1])
    bufs_in = _adamw_slab("adam_ffn_w_in_00", p_in, w_in_s, m_in_s, v_in_s, bufs_in, 0)
    res["ffn_w_in"] = [b.reshape(ffn_w_in.shape) for b in bufs_in]
    res["ffn_w_out"] = [b.reshape(ffn_w_out.shape) for b in bufs_out]

    order = ["ada_w", "ada_b", "norm_g", "ffn_w_in", "ffn_w_out", "pool_w", "pool_b", "pool_scale", "mla_w_in", "mla_q_norm",
             "mla_kv_norm", "mla_w_uq", "mla_w_uk", "mla_w_uv", "mla_w_o"]
    outs = [loss, grad_x]
    for j in range(4):
        outs += [res[name][j] for name in order]
    return tuple(outs)
```

```python
import functools

import jax
import jax.numpy as jnp
from jax import lax
from jax.experimental import pallas as pl
from jax.experimental.pallas import tpu as pltpu

F32 = jnp.float32
BF16 = jnp.bfloat16
N_DEV = 8
AXES = ("x", "y", "c")
MESH = pl.DeviceIdType.MESH

D_MODEL = 1024
N_HEADS = 16
QK_NOPE = 64
QK_ROPE = 32
V_HEAD = 64
Q_LORA = 256
KV_LORA = 128
LAT_PAD = 512
QK_PAD = 256
D_FF = 2816
FF_BLK = 2 * D_FF // N_DEV
POOL_WINDOWS = (2, 4, 8, 16)
POOL_GROUP = 256
ROPE_THETA = 10000.0
EPS = 1e-6
ATTN_SCALE = (QK_NOPE + QK_ROPE) ** -0.5
LOG2_E = 1.4426950408889634
ADAM_LR, ADAM_B1, ADAM_B2, ADAM_EPS, ADAM_WD, ADAM_STEP = 0.001, 0.9, 0.999, 1e-08, 0.01, 10
VMEM_LIMIT = 56 * 1024 * 1024

NN = ((1,), (0,))
NT = ((1,), (1,))
TN = ((0,), (0,))


def _params(**kw):
    return pltpu.CompilerParams(vmem_limit_bytes=VMEM_LIMIT, **kw)


def _dot(a, b, contract):
    return lax.dot_general(a, b, (contract, ((), ())), preferred_element_type=F32)


def _matmul(name, a, b, *, a_blk, a_map, b_blk, b_map, o_shape, o_blk, o_map, grid, contract, out_dtype,
            bias=None, bias_blk=None, bias_map=None, after=None):
    n_k = grid[-1]
    k_axis = len(grid) - 1
    acc_shape = tuple(d for d in o_blk if d is not None)

    def body(*refs):
        a_ref, b_ref = refs[:2]
        bias_ref = refs[2] if bias is not None else None
        o_ref, acc = refs[-2:]
        k = pl.program_id(k_axis)

        @pl.when(k == 0)
        def _():
            acc[...] = jnp.zeros_like(acc)

        acc[...] += _dot(a_ref[...].astype(BF16), b_ref[...].astype(BF16), contract)

        @pl.when(k == n_k - 1)
        def _():
            r = acc[...]
            if bias is not None:
                r = r + bias_ref[...]
            o_ref[...] = r.astype(o_ref.dtype)

    in_specs = [pl.BlockSpec(a_blk, a_map), pl.BlockSpec(b_blk, b_map)]
    args = [a, b]
    if bias is not None:
        in_specs.append(pl.BlockSpec(bias_blk, bias_map))
        args.append(bias)
    if after is not None:
        in_specs.append(pl.BlockSpec(memory_space=pl.ANY))
        args.append(after)
    return pl.pallas_call(
        body, name=name, grid=grid, in_specs=in_specs, out_specs=pl.BlockSpec(o_blk, o_map),
        out_shape=jax.ShapeDtypeStruct(o_shape, out_dtype), scratch_shapes=[pltpu.VMEM(acc_shape, F32)],
        compiler_params=_params(),
    )(*args)


def _rowmap(name, fn, ins, outs, reds, grid):
    n_in, n_out, n_red = len(ins), len(outs), len(reds)

    def body(*refs):
        in_refs = refs[:n_in]
        out_refs = refs[n_in:n_in + n_out]
        red_refs = refs[n_in + n_out:]
        out_vals, red_vals = fn(*[r[...] for r in in_refs])
        for r, v in zip(out_refs, out_vals):
            r[...] = v.astype(r.dtype)
        if n_red:
            first = pl.program_id(0) == 0
            for ax in range(1, len(grid)):
                first = jnp.logical_and(first, pl.program_id(ax) == 0)

            @pl.when(first)
            def _():
                for r in red_refs:
                    r[...] = jnp.zeros_like(r)

            for r, v in zip(red_refs, red_vals):
                r[...] += v

    res = pl.pallas_call(
        body, name=name, grid=grid,
        in_specs=[pl.BlockSpec(blk, imap) for _, blk, imap in ins],
        out_specs=[pl.BlockSpec(blk, imap) for _, blk, imap in list(outs) + list(reds)],
        out_shape=[sds for sds, _, _ in list(outs) + list(reds)],
        compiler_params=_params(),
    )(*[a for a, _, _ in ins])
    return res[:n_out], res[n_out:]


def _sds(shape, dtype):
    return jax.ShapeDtypeStruct(shape, dtype)


def _tile(a, tm):
    return (a, (tm, a.shape[1]), lambda i: (i, 0))


def _row(a):
    return (a, (1, a.shape[1]), lambda i: (0, 0))


def _otile(n, c, dtype, tm):
    return (_sds((n, c), dtype), (tm, c), lambda i: (i, 0))


def _ored(c):
    return (_sds((1, c), F32), (1, c), lambda i: (0, 0))


def _colsum(v):
    return jnp.sum(v, axis=0, keepdims=True)


def _rstd(v):
    return lax.rsqrt(jnp.mean(v * v, axis=-1, keepdims=True) + EPS)


def _prenorm(name, x, g_pre, scale, shift, out_dtype, tm):
    n, d = x.shape

    def fn(xv, g, sc, sh):
        return (xv * _rstd(xv) * g * (1.0 + sc) + sh,), ()

    (h,), _ = _rowmap(name, fn, [_tile(x, tm), _row(g_pre), _row(scale), _row(shift)], [_otile(n, d, out_dtype, tm)], [],
                      (n // tm,))
    return h


def _postnorm(name, x, u, g_post, gate, weight, tm):
    n, d = x.shape

    def fn(xv, uv, g, gt):
        return (xv + weight * (1.0 + gt) * (uv * _rstd(uv) * g),), ()

    (y,), _ = _rowmap(name, fn, [_tile(x, tm), _tile(u, tm), _row(g_post), _row(gate)], [_otile(n, d, F32, tm)], [], (n // tm,))
    return y


def _postnorm_bwd(name, dout, u, g_post, gate, weight, out_dtype, tm):
    n, d = u.shape

    def fn(dv, uv, g, gt):
        r = _rstd(uv)
        un = uv * r
        dy = dv * (weight * (1.0 + gt))
        a = dy * g
        du = r * (a - un * jnp.mean(a * un, axis=-1, keepdims=True))
        return (du,), (_colsum(dv * (weight * (un * g))), _colsum(dy * un))

    (du,), reds = _rowmap(name, fn, [_tile(dout, tm), _tile(u, tm), _row(g_post), _row(gate)], [_otile(n, d, out_dtype, tm)],
                          [_ored(d), _ored(d)], (n // tm,))
    return du, reds


def _prenorm_bwd(name, dh, x, dout, g_pre, scale, tm):
    n, d = x.shape

    def fn(dhv, xv, dv, g, sc):
        dhv = dhv.astype(F32)
        r = _rstd(xv)
        xn = xv * r
        b = dhv * (g * (1.0 + sc))
        dx = dv + r * (b - xn * jnp.mean(b * xn, axis=-1, keepdims=True))
        return (dx,), (_colsum(dhv), _colsum(dhv * (xn * g)), _colsum(dhv * ((1.0 + sc) * xn)))

    (dx,), reds = _rowmap(name, fn, [_tile(dh, tm), _tile(x, tm), _tile(dout, tm), _row(g_pre), _row(scale)],
                          [_otile(n, d, F32, tm)], [_ored(d), _ored(d), _ored(d)], (n // tm,))
    return dx, reds


def _ffn_fwd(tag, h, w_in8, w_out4, tm):
    s, d = h.shape
    nt = s // tm
    gu = _matmul(f"ffn_up_{tag}", h, w_in8, a_blk=(tm, d), a_map=lambda g, i, k: (i, 0), b_blk=(None, d, FF_BLK),
                 b_map=lambda g, i, k: (g, 0, 0), o_shape=(8, s, FF_BLK), o_blk=(None, tm, FF_BLK), o_map=lambda g, i, k: (g, i, 0),
                 grid=(8, nt, 1), contract=NN, out_dtype=BF16)

    def act_fn(gv, uv):
        gv = gv.astype(F32)
        return (gv * jax.nn.sigmoid(gv) * uv.astype(F32),), ()

    (act,), _ = _rowmap(f"ffn_act_{tag}", act_fn,
                        [(gu, (None, tm, FF_BLK), lambda j, i: (j, i, 0)), (gu, (None, tm, FF_BLK), lambda j, i: (j + 4, i, 0))],
                        [(_sds((4, s, FF_BLK), BF16), (None, tm, FF_BLK), lambda j, i: (j, i, 0))], [], (4, nt))
    u = _matmul(f"ffn_down_{tag}", act, w_out4, a_blk=(None, tm, FF_BLK), a_map=lambda i, k: (k, i, 0), b_blk=(None, FF_BLK, d),
                b_map=lambda i, k: (k, 0, 0), o_shape=(s, d), o_blk=(tm, d), o_map=lambda i, k: (i, 0), grid=(nt, 4),
                contract=NN, out_dtype=F32)
    return u, gu


def _ffn_bwd_act(tag, du, gu, w_out4, tm):
    s, d = du.shape
    nt = s // tm
    dact = _matmul(f"ffn_dact_{tag}", du, w_out4, a_blk=(tm, d), a_map=lambda j, i, k: (i, 0), b_blk=(None, FF_BLK, d),
                   b_map=lambda j, i, k: (j, 0, 0), o_shape=(4, s, FF_BLK), o_blk=(None, tm, FF_BLK), o_map=lambda j, i, k: (j, i, 0),
                   grid=(4, nt, 1), contract=NT, out_dtype=BF16)

    def bwd_fn(gv, uv, dav):
        gv, uv, dav = gv.astype(F32), uv.astype(F32), dav.astype(F32)
        sg = jax.nn.sigmoid(gv)
        silu = gv * sg
        return (jnp.stack([dav * uv * (sg * (1.0 + gv * (1.0 - sg))), dav * silu]), silu * uv), ()

    blk = (None, tm, FF_BLK)
    (dgu, act), _ = _rowmap(
        f"ffn_dgu_{tag}", bwd_fn,
        [(gu, blk, lambda j, i: (j, i, 0)), (gu, blk, lambda j, i: (j + 4, i, 0)), (dact, blk, lambda j, i: (j, i, 0))],
        [(_sds((2, 4, s, FF_BLK), BF16), (2, None, tm, FF_BLK), lambda j, i: (0, j, i, 0)),
         (_sds((4, s, FF_BLK), BF16), blk, lambda j, i: (j, i, 0))], [], (4, nt))
    return dgu.reshape(8, s, FF_BLK), act


def _ffn_dw_in(tag, h, dgu, tm, after=None):
    s, d = h.shape
    return _matmul(f"ffn_dwin_{tag}", h, dgu, a_blk=(tm, d), a_map=lambda g, k: (k, 0), b_blk=(None, tm, FF_BLK),
                   b_map=lambda g, k: (g, k, 0), o_shape=(8, d, FF_BLK), o_blk=(None, d, FF_BLK), o_map=lambda g, k: (g, 0, 0),
                   grid=(8, s // tm), contract=TN, out_dtype=BF16, after=after)


def _ffn_dw_out(tag, act, du, tm):
    s, d = du.shape
    return _matmul(f"ffn_dwout_{tag}", act, du, a_blk=(None, tm, FF_BLK), a_map=lambda j, k: (j, k, 0), b_blk=(tm, d),
                   b_map=lambda j, k: (k, 0), o_shape=(4, FF_BLK, d), o_blk=(None, FF_BLK, d), o_map=lambda j, k: (j, 0, 0),
                   grid=(4, s // tm), contract=TN, out_dtype=BF16)


def _ffn_dh(tag, dgu, w_in8, tm, after=None):
    s = dgu.shape[1]
    d = w_in8.shape[1]
    return _matmul(f"ffn_dh_{tag}", dgu, w_in8, a_blk=(None, tm, FF_BLK), a_map=lambda i, k: (k, i, 0), b_blk=(None, d, FF_BLK),
                   b_map=lambda i, k: (k, 0, 0), o_shape=(s, d), o_blk=(tm, d), o_map=lambda i, k: (i, 0), grid=(s // tm, 8),
                   contract=NT, out_dtype=F32, after=after)


def _window_sum(x, window, transpose):
    s = x.shape[0]
    t = lax.broadcasted_iota(jnp.int32, (s, 1), 0)
    half = window // 2
    cnt = jnp.minimum(t + half, s) - jnp.maximum(t - half, 0)
    inv = 1.0 / cnt.astype(F32)
    if transpose:
        x = x * inv
        offsets = range(-half + 1, half + 1)
    else:
        offsets = range(-half, half)
    acc = jnp.zeros_like(x)
    for o in offsets:
        shifted = x if o == 0 else pltpu.roll(x, (-o) % s, 0)
        valid = jnp.logical_and(t + o >= 0, t + o < s)
        acc = acc + jnp.where(valid, shifted, 0.0)
    return acc if transpose else acc * inv


def _pool_mix(name, x, transpose, out_dtype):
    s, d = x.shape

    def body(x_ref, o_ref):
        g = pl.program_id(0)
        for gi, window in enumerate(POOL_WINDOWS):
            @pl.when(g == gi)
            def _(window=window):
                xv = x_ref[...].astype(F32)
                o_ref[...] = (_window_sum(xv, window, transpose) - xv).astype(o_ref.dtype)

    return pl.pallas_call(
        body, name=name, grid=(len(POOL_WINDOWS),), in_specs=[pl.BlockSpec((s, POOL_GROUP), lambda g: (0, g))],
        out_specs=pl.BlockSpec((s, POOL_GROUP), lambda g: (0, g)), out_shape=_sds((s, d), out_dtype), compiler_params=_params(),
    )(x)


def _pool_fwd(h, w4, bias, pscale, tm):
    s, d = h.shape
    nt = s // tm
    z = _pool_mix("pool_mix", h, False, BF16)
    v = _matmul("pool_proj", z, w4, a_blk=(tm, POOL_GROUP), a_map=lambda i, g, k: (i, g), b_blk=(None, POOL_GROUP, POOL_GROUP),
                b_map=lambda i, g, k: (g, 0, 0), o_shape=(s, d), o_blk=(tm, POOL_GROUP), o_map=lambda i, g, k: (i, g),
                grid=(nt, 4, 1), contract=NN, out_dtype=F32, bias=bias, bias_blk=(1, POOL_GROUP), bias_map=lambda i, g, k: (0, g))
    (u,), _ = _rowmap("pool_scale", lambda vv, ps: ((vv * ps,), ()), [_tile(v, tm), _row(pscale)], [_otile(s, d, F32, tm)], [],
                      (nt,))
    return u, z, v


def _pool_bwd(du, z, v, w4, pscale, tm):
    s, d = du.shape
    nt = s // tm

    def fn(duv, vv, ps):
        dv = duv * ps
        return (dv,), (_colsum(duv * vv), _colsum(dv))

    (dv,), reds = _rowmap("pool_dscale", fn, [_tile(du, tm), _tile(v, tm), _row(pscale)], [_otile(s, d, BF16, tm)],
                          [_ored(d), _ored(d)], (nt,))
    dw4 = _matmul("pool_dw", z, dv, a_blk=(tm, POOL_GROUP), a_map=lambda g, k: (k, g), b_blk=(tm, POOL_GROUP),
                  b_map=lambda g, k: (k, g), o_shape=(4, POOL_GROUP, POOL_GROUP), o_blk=(None, POOL_GROUP, POOL_GROUP),
                  o_map=lambda g, k: (g, 0, 0), grid=(4, nt), contract=TN, out_dtype=F32)
    dz = _matmul("pool_dz", dv, w4, a_blk=(tm, POOL_GROUP), a_map=lambda i, g, k: (i, g), b_blk=(None, POOL_GROUP, POOL_GROUP),
                 b_map=lambda i, g, k: (g, 0, 0), o_shape=(s, d), o_blk=(tm, POOL_GROUP), o_map=lambda i, g, k: (i, g),
                 grid=(nt, 4, 1), contract=NT, out_dtype=F32)
    dh = _pool_mix("pool_mix_t", dz, True, F32)
    return dh, dw4, reds


def _lane(shape):
    return lax.broadcasted_iota(jnp.int32, shape, 1)


def _rope_swap(v, transpose):
    half = QK_ROPE // 2
    lane = _lane(v.shape)
    up = pltpu.roll(v, v.shape[1] - half, 1)
    down = pltpu.roll(v, half, 1)
    if transpose:
        return jnp.where(lane < half, up, jnp.where(lane < QK_ROPE, -down, 0.0))
    return jnp.where(lane < half, -up, jnp.where(lane < QK_ROPE, down, 0.0))


def _rope(v, cos, sin):
    return v * cos + _rope_swap(v, False) * sin


def _rope_t(g, cos, sin):
    return g * cos + _rope_swap(g * sin, True)


def _mla_mid(lat, q_norm, kv_norm, cos_k, sin_k, tm):
    s = lat.shape[0]

    def fn(lv, qn, kn, cs, sn):
        cq = lv[:, :Q_LORA]
        ckv = lv[:, Q_LORA:Q_LORA + KV_LORA]
        kr = lv[:, Q_LORA + KV_LORA:]
        cq = cq * _rstd(cq) * qn
        ckv = ckv * _rstd(ckv) * kn
        return (cq, jnp.concatenate([ckv, _rope(kr, cs, sn)], axis=1)), ()

    (cq, kcat), _ = _rowmap("mla_mid", fn, [_tile(lat, tm), _row(q_norm), _row(kv_norm), _tile(cos_k, tm), _tile(sin_k, tm)],
                            [_otile(s, Q_LORA, BF16, tm), _otile(s, QK_PAD, BF16, tm)], [], (s // tm,))
    return cq, kcat


def _mla_mid_bwd(lat, dcq, dkcat, dv, q_norm, kv_norm, cos_k, sin_k, tm):
    s = lat.shape[0]

    def fn(lv, dq, dk, dvv, qn, kn, cs, sn):
        dk = dk * (1.0 / LOG2_E)
        cq = lv[:, :Q_LORA]
        ckv = lv[:, Q_LORA:Q_LORA + KV_LORA]
        rq, rk = _rstd(cq), _rstd(ckv)
        cqn, ckn = cq * rq, ckv * rk
        a = dq * qn
        d_cq = rq * (a - cqn * jnp.mean(a * cqn, axis=-1, keepdims=True))
        dckv = dk[:, :KV_LORA] + dvv
        a2 = dckv * kn
        d_ckv = rk * (a2 - ckn * jnp.mean(a2 * ckn, axis=-1, keepdims=True))
        d_kr = _rope_t(dk[:, KV_LORA:], cs, sn)
        return (jnp.concatenate([d_cq, d_ckv, d_kr], axis=1),), (_colsum(dq * cqn), _colsum(dckv * ckn))

    (dlat,), reds = _rowmap(
        "mla_mid_bwd", fn,
        [_tile(lat, tm), _tile(dcq, tm), _tile(dkcat, tm), _tile(dv, tm), _row(q_norm), _row(kv_norm), _tile(cos_k, tm),
         _tile(sin_k, tm)],
        [_otile(s, LAT_PAD, BF16, tm)], [_ored(Q_LORA), _ored(KV_LORA)], (s // tm,))
    return dlat, reds


def _mla_q(cq, wq, wukp, cos_k, sin_k, tm):
    s = cq.shape[0]

    def body(cq_ref, wq_ref, wuk_ref, cos_ref, sin_ref, o_ref):
        aq = _dot(cq_ref[...], wq_ref[...], NN)
        qlat = _dot(aq.astype(BF16), wuk_ref[...], NN)
        roped = _rope(aq[:, KV_LORA:], cos_ref[...], sin_ref[...])
        o_ref[...] = (jnp.concatenate([qlat[:, :KV_LORA], roped], axis=1) * (ATTN_SCALE * LOG2_E)).astype(o_ref.dtype)

    wblk = pl.BlockSpec((None, QK_PAD, QK_PAD), lambda h, i: (h, 0, 0))
    tblk = pl.BlockSpec((tm, KV_LORA), lambda h, i: (i, 0))
    return pl.pallas_call(
        body, name="mla_q", grid=(N_HEADS, s // tm),
        in_specs=[pl.BlockSpec((tm, Q_LORA), lambda h, i: (i, 0)), wblk, wblk, tblk, tblk],
        out_specs=pl.BlockSpec((None, tm, QK_PAD), lambda h, i: (h, i, 0)), out_shape=_sds((N_HEADS, s, QK_PAD), BF16),
        compiler_params=_params(),
    )(cq, wq, wukp, cos_k, sin_k)


def _mla_q_bwd(cq, wq, wukp, cos_k, sin_k, dqcat, tm):
    s = cq.shape[0]

    def body(cq_ref, wq_ref, wuk_ref, cos_ref, sin_ref, dq_ref, dcq_ref, dwq_ref, dwuk_ref):
        h, i = pl.program_id(0), pl.program_id(1)
        cqv = cq_ref[...]
        aq = _dot(cqv, wq_ref[...], NN).astype(BF16)
        g = dq_ref[...].astype(F32) * ATTN_SCALE
        gl, gr = g[:, :KV_LORA], g[:, KV_LORA:]
        dqlat = jnp.concatenate([gl, jnp.zeros_like(gl)], axis=1).astype(BF16)
        d_rope = _rope_t(gr, cos_ref[...], sin_ref[...])
        daq = _dot(dqlat, wuk_ref[...], NT) + jnp.concatenate([jnp.zeros_like(d_rope), d_rope], axis=1)
        daq_b = daq.astype(BF16)
        d_wuk = _dot(aq, dqlat, TN)
        d_wq = _dot(cqv, daq_b, TN)
        d_cq = _dot(daq_b, wq_ref[...], NT)
        rows = pl.ds(pl.multiple_of(i * tm, tm), tm)

        @pl.when(i == 0)
        def _():
            dwq_ref[...] = d_wq
            dwuk_ref[...] = d_wuk

        @pl.when(i != 0)
        def _():
            dwq_ref[...] += d_wq
            dwuk_ref[...] += d_wuk

        @pl.when(h == 0)
        def _():
            dcq_ref[rows, :] = d_cq

        @pl.when(h != 0)
        def _():
            dcq_ref[rows, :] += d_cq

    wblk = pl.BlockSpec((None, QK_PAD, QK_PAD), lambda h, i: (h, 0, 0))
    tblk = pl.BlockSpec((tm, KV_LORA), lambda h, i: (i, 0))
    return pl.pallas_call(
        body, name="mla_q_bwd", grid=(N_HEADS, s // tm),
        in_specs=[pl.BlockSpec((tm, Q_LORA), lambda h, i: (i, 0)), wblk, wblk, tblk, tblk,
                  pl.BlockSpec((None, tm, QK_PAD), lambda h, i: (h, i, 0))],
        out_specs=[pl.BlockSpec((s, Q_LORA), lambda h, i: (0, 0)), wblk, wblk],
        out_shape=[_sds((s, Q_LORA), F32), _sds((N_HEADS, QK_PAD, QK_PAD), F32), _sds((N_HEADS, QK_PAD, QK_PAD), F32)],
        compiler_params=_params(),
    )(cq, wq, wukp, cos_k, sin_k, dqcat)


def _flash_fwd(qcat, kcat, tq, tk):
    n_h, s, _ = qcat.shape
    n_k = s // tk

    def body(q_ref, k_ref, v_ref, o_ref, lse_ref):
        q = q_ref[...]
        m = jnp.full((tq, 1), -1e30, F32)
        l = jnp.zeros((tq, 1), F32)
        acc = jnp.zeros((tq, KV_LORA), F32)
        for kk in range(n_k):
            rows = pl.ds(kk * tk, tk)
            sc = _dot(q, k_ref[rows, :], NT)
            m_new = jnp.maximum(m, jnp.max(sc, axis=1, keepdims=True))
            alpha = jnp.exp2(m - m_new)
            p = jnp.exp2(sc - m_new)
            l = alpha * l + jnp.sum(p, axis=1, keepdims=True)
            acc = alpha * acc + _dot(p.astype(BF16), v_ref[rows, :], NN)
            m = m_new
        o_ref[...] = (acc / l).astype(o_ref.dtype)
        lse_ref[...] = m + jnp.log2(l)

    return pl.pallas_call(
        body, name="mla_attn", grid=(n_h, s // tq),
        in_specs=[pl.BlockSpec((None, tq, QK_PAD), lambda h, i: (h, i, 0)), pl.BlockSpec((s, QK_PAD), lambda h, i: (0, 0)),
                  pl.BlockSpec((s, KV_LORA), lambda h, i: (0, 0))],
        out_specs=[pl.BlockSpec((None, tq, KV_LORA), lambda h, i: (h, i, 0)), pl.BlockSpec((None, tq, 1), lambda h, i: (h, i, 0))],
        out_shape=[_sds((n_h, s, KV_LORA), BF16), _sds((n_h, s, 1), F32)], compiler_params=_params(),
    )(qcat, kcat, kcat)


def _flash_bwd(qcat, kcat, o, do, lse, tq, tk):
    n_h, s, _ = qcat.shape
    n_k = s // tk

    def body(q_ref, k_ref, v_ref, o_ref, do_ref, lse_ref, dq_ref, dk_ref, dv_ref, dq_acc):
        h, i = pl.program_id(0), pl.program_id(1)

        @pl.when(jnp.logical_and(h == 0, i == 0))
        def _():
            dk_ref[...] = jnp.zeros_like(dk_ref)
            dv_ref[...] = jnp.zeros_like(dv_ref)

        q = q_ref[...]
        dov = do_ref[...]
        lse_v = lse_ref[...]
        delta = jnp.sum(dov.astype(F32) * o_ref[...].astype(F32), axis=1, keepdims=True)
        dq_acc[...] = jnp.zeros_like(dq_acc)

        for kk in range(n_k):
            rows = pl.ds(kk * tk, tk)
            k = k_ref[rows, :]
            p = jnp.exp2(_dot(q, k, NT) - lse_v)
            dp = _dot(dov, v_ref[rows, :], NT)
            ds = (p * (dp - delta)).astype(BF16)
            dq_acc[...] += _dot(ds, k, NN)
            dv_ref[rows, :] += _dot(p.astype(BF16), dov, TN)
            dk_ref[rows, :] += _dot(ds, q, TN)
        dq_ref[...] = dq_acc[...].astype(dq_ref.dtype)

    qblk = pl.BlockSpec((None, tq, QK_PAD), lambda h, i: (h, i, 0))
    oblk = pl.BlockSpec((None, tq, KV_LORA), lambda h, i: (h, i, 0))
    return pl.pallas_call(
        body, name="mla_attn_bwd", grid=(n_h, s // tq),
        in_specs=[qblk, pl.BlockSpec((s, QK_PAD), lambda h, i: (0, 0)), pl.BlockSpec((s, KV_LORA), lambda h, i: (0, 0)), oblk, oblk,
                  pl.BlockSpec((None, tq, 1), lambda h, i: (h, i, 0))],
        out_specs=[qblk, pl.BlockSpec((s, QK_PAD), lambda h, i: (0, 0)), pl.BlockSpec((s, KV_LORA), lambda h, i: (0, 0))],
        out_shape=[_sds((n_h, s, QK_PAD), BF16), _sds((s, QK_PAD), F32), _sds((s, KV_LORA), F32)],
        scratch_shapes=[pltpu.VMEM((tq, QK_PAD), F32)], compiler_params=_params(),
    )(qcat, kcat, kcat, o, do, lse)


def _mla_fwd(h, wts, cos_k, sin_k, tm):
    s, d = h.shape
    nt = s // tm
    lat = _matmul("mla_lat", h, wts["w_in"], a_blk=(tm, d), a_map=lambda i, k: (i, 0), b_blk=(d, LAT_PAD), b_map=lambda i, k: (0, 0),
                  o_shape=(s, LAT_PAD), o_blk=(tm, LAT_PAD), o_map=lambda i, k: (i, 0), grid=(nt, 1), contract=NN, out_dtype=F32)
    cq, kcat = _mla_mid(lat, wts["q_norm"], wts["kv_norm"], cos_k, sin_k, tm)
    qcat = _mla_q(cq, wts["wq"], wts["wukp"], cos_k, sin_k, tm)
    o_lat, lse = _flash_fwd(qcat, kcat, tm, tm)
    o = _matmul("mla_uv", o_lat, wts["wuv2"], a_blk=(None, tm, KV_LORA), a_map=lambda i, p, r: (2 * p + r, i, 0),
                b_blk=(None, KV_LORA, 2 * V_HEAD), b_map=lambda i, p, r: (2 * p + r, 0, 0), o_shape=(s, d), o_blk=(tm, 2 * V_HEAD),
                o_map=lambda i, p, r: (i, p), grid=(nt, N_HEADS // 2, 2), contract=NN, out_dtype=BF16)
    u = _matmul("mla_out", o, wts["w_o"], a_blk=(tm, d), a_map=lambda i, k: (i, 0), b_blk=(d, d), b_map=lambda i, k: (0, 0),
                o_shape=(s, d), o_blk=(tm, d), o_map=lambda i, k: (i, 0), grid=(nt, 1), contract=NN, out_dtype=F32)
    return u, (lat, cq, kcat, qcat, o_lat, lse, o)


def _mla_bwd(du, h, saved, wts, cos_k, sin_k, tm):
    lat, cq, kcat, qcat, o_lat, lse, o = saved
    s, d = h.shape
    nt = s // tm
    do = _matmul("mla_do", du, wts["w_o"], a_blk=(tm, d), a_map=lambda i, k: (i, 0), b_blk=(d, d), b_map=lambda i, k: (0, 0),
                 o_shape=(s, d), o_blk=(tm, d), o_map=lambda i, k: (i, 0), grid=(nt, 1), contract=NT, out_dtype=BF16)
    dw_o = _matmul("mla_dwo", o, du, a_blk=(tm, d), a_map=lambda k: (k, 0), b_blk=(tm, d), b_map=lambda k: (k, 0),
                   o_shape=(d, d), o_blk=(d, d), o_map=lambda k: (0, 0), grid=(nt,), contract=TN, out_dtype=F32)
    do_lat = _matmul("mla_dolat", do, wts["wuv2"], a_blk=(tm, 2 * V_HEAD), a_map=lambda hh, i, k: (i, hh // 2),
                     b_blk=(None, KV_LORA, 2 * V_HEAD), b_map=lambda hh, i, k: (hh, 0, 0), o_shape=(N_HEADS, s, KV_LORA),
                     o_blk=(None, tm, KV_LORA), o_map=lambda hh, i, k: (hh, i, 0), grid=(N_HEADS, nt, 1), contract=NT, out_dtype=BF16)
    dwuv2 = _matmul("mla_dwuv", o_lat, do, a_blk=(None, tm, KV_LORA), a_map=lambda hh, k: (hh, k, 0), b_blk=(tm, 2 * V_HEAD),
                    b_map=lambda hh, k: (k, hh // 2), o_shape=(N_HEADS, KV_LORA, 2 * V_HEAD), o_blk=(None, KV_LORA, 2 * V_HEAD),
                    o_map=lambda hh, k: (hh, 0, 0), grid=(N_HEADS, nt), contract=TN, out_dtype=F32)
    dqcat, dkcat, dv = _flash_bwd(qcat, kcat, o_lat, do_lat, lse, tm, tm)
    dcq, dwq, dwukp = _mla_q_bwd(cq, wts["wq"], wts["wukp"], cos_k, sin_k, dqcat, tm)
    dlat, (dqn, dkn) = _mla_mid_bwd(lat, dcq, dkcat, dv, wts["q_norm"], wts["kv_norm"], cos_k, sin_k, tm)
    dh = _matmul("mla_dh", dlat, wts["w_in"], a_blk=(tm, LAT_PAD), a_map=lambda i, k: (i, 0), b_blk=(d, LAT_PAD),
                 b_map=lambda i, k: (0, 0), o_shape=(s, d), o_blk=(tm, d), o_map=lambda i, k: (i, 0), grid=(nt, 1), contract=NT,
                 out_dtype=F32)
    dw_in = _matmul("mla_dwin", h, dlat, a_blk=(tm, d), a_map=lambda k: (k, 0), b_blk=(tm, LAT_PAD), b_map=lambda k: (k, 0),
                    o_shape=(d, LAT_PAD), o_blk=(d, LAT_PAD), o_map=lambda k: (0, 0), grid=(nt,), contract=TN, out_dtype=F32)
    return dh, dict(w_in=dw_in, wq=dwq, wukp=dwukp, wuv2=dwuv2, w_o=dw_o, q_norm=dqn, kv_norm=dkn)


def _adamw(name, parts, w, m, v):
    n_parts, r, c = parts.shape
    tr = r
    for cand in (256, 128, 64, 32, 16, 8):
        if r > cand and r % cand == 0:
            tr = cand
            break

    def body(p_ref, w_ref, m_ref, v_ref, g_ref, d_ref, nm_ref, nv_ref):
        g = p_ref[0].astype(F32)
        for k in range(1, n_parts):
            g = g + p_ref[k].astype(F32)
        nm = ADAM_B1 * m_ref[...] + (1.0 - ADAM_B1) * g
        nv = ADAM_B2 * v_ref[...] + (1.0 - ADAM_B2) * (g * g)
        m_hat = nm / (1.0 - ADAM_B1 ** ADAM_STEP)
        v_hat = nv / (1.0 - ADAM_B2 ** ADAM_STEP)
        g_ref[...] = g
        d_ref[...] = -ADAM_LR * (m_hat / (jnp.sqrt(v_hat) + ADAM_EPS) + ADAM_WD * w_ref[...])
        nm_ref[...] = nm
        nv_ref[...] = nv

    blk = pl.BlockSpec((tr, c), lambda i: (i, 0))
    return pl.pallas_call(
        body, name=name, grid=(r // tr,), in_specs=[pl.BlockSpec((n_parts, tr, c), lambda i: (0, i, 0)), blk, blk, blk],
        out_specs=[blk] * 4, out_shape=[_sds((r, c), F32)] * 4, compiler_params=_params(),
    )(parts, w, m, v)


def _adamw_slab(name, parts, w, m, v, bufs, f):
    n_parts, r, c = parts.shape
    tr = max(t for t in range(8, 257, 8) if r % t == 0)

    def body(p_ref, w_ref, m_ref, v_ref, *rest):
        g_ref, d_ref, nm_ref, nv_ref = rest[4:]
        g = p_ref[0].astype(F32)
        for k in range(1, n_parts):
            g = g + p_ref[k].astype(F32)
        nm = ADAM_B1 * m_ref[...] + (1.0 - ADAM_B1) * g
        nv = ADAM_B2 * v_ref[...] + (1.0 - ADAM_B2) * (g * g)
        m_hat = nm / (1.0 - ADAM_B1 ** ADAM_STEP)
        v_hat = nv / (1.0 - ADAM_B2 ** ADAM_STEP)
        g_ref[...] = g
        d_ref[...] = -ADAM_LR * (m_hat / (jnp.sqrt(v_hat) + ADAM_EPS) + ADAM_WD * w_ref[...])
        nm_ref[...] = nm
        nv_ref[...] = nv

    blk = pl.BlockSpec((None, tr, c), lambda i: (f, i, 0))
    return pl.pallas_call(
        body, name=name, grid=(r // tr,),
        in_specs=[pl.BlockSpec((n_parts, tr, c), lambda i: (0, i, 0)), blk, blk, blk] + [pl.BlockSpec(memory_space=pl.ANY)] * 4,
        out_specs=[blk] * 4, out_shape=[_sds(w.shape, F32)] * 4, input_output_aliases={4 + j: j for j in range(4)},
        compiler_params=_params(),
    )(parts, w, m, v, *bufs)


def _mesh_pos():
    return lax.axis_index("x"), lax.axis_index("y"), lax.axis_index("c")


def _flip(pos, mask):
    return tuple(1 - p if (mask >> (2 - b)) & 1 else p for b, p in enumerate(pos))


def _index(pos):
    return 4 * pos[0] + 2 * pos[1] + pos[2]


def _all_gather(name, xs, after=None):
    n = len(xs)
    extra = [] if after is None else [after]

    def body(*refs):
        x_refs, o_refs = refs[:n], refs[n + len(extra):2 * n + len(extra)]
        send_sems, recv_sems, local_sems = refs[2 * n + len(extra):]
        me = _mesh_pos()
        sibling = _flip(me, 1)
        others = [_flip(me, 4), _flip(me, 2), _flip(me, 6)]

        def copy(k, j, block, to, src=None):
            dst = o_refs[k].at[_index(block)]
            return pltpu.make_async_remote_copy(
                src_ref=dst if src is None else src, dst_ref=dst, send_sem=send_sems.at[k, j], recv_sem=recv_sems.at[k, j],
                device_id=to, device_id_type=MESH)

        local = [pltpu.make_async_copy(x_refs[k], o_refs[k].at[_index(me)], local_sems.at[k]) for k in range(n)]
        for cp in local:
            cp.start()
        first = []
        for k in range(n):
            first.append(copy(k, 0, me, sibling, src=x_refs[k]))
            first += [copy(k, 1 + j, me, other, src=x_refs[k]) for j, other in enumerate(others)]
        for cp in first:
            cp.start()
        passed = []
        for j, other in enumerate(others):
            for k in range(n):
                copy(k, 1 + j, other, me).wait_recv()
                cp = copy(k, 4 + j, other, sibling)
                cp.start()
                passed.append(cp)
        for k in range(n):
            copy(k, 0, sibling, me).wait_recv()
        for j, other in enumerate(others):
            for k in range(n):
                copy(k, 4 + j, _flip(other, 1), me).wait_recv()
        for cp in first + passed:
            cp.wait_send()
        for cp in local:
            cp.wait()

    any_spec = pl.BlockSpec(memory_space=pl.ANY)
    return pl.pallas_call(
        body, name=name, in_specs=[any_spec] * (n + len(extra)), out_specs=[any_spec] * n,
        out_shape=[_sds((N_DEV,) + x.shape, x.dtype) for x in xs],
        scratch_shapes=[pltpu.SemaphoreType.DMA((n, 7)), pltpu.SemaphoreType.DMA((n, 7)), pltpu.SemaphoreType.DMA((n,))],
    )(*xs, *extra)


def _all_to_all(name, groups):
    flat = [(gi, f) for gi, grp in enumerate(groups) for f in range(len(grp))]
    n = len(flat)
    n_groups = len(groups)

    def body(*refs):
        x_refs, o_refs = refs[:n], refs[n:n + n_groups]
        send_sems, recv_sems, local_sems = refs[n + n_groups:]
        me = _mesh_pos()
        local, sends, recvs = [], [], []
        for k, (gi, f) in enumerate(flat):
            local.append(pltpu.make_async_copy(x_refs[k].at[_index(me)], o_refs[gi].at[_index(me), f], local_sems.at[k]))
            for mask in range(1, N_DEV):
                peer = _flip(me, mask)
                sends.append(pltpu.make_async_remote_copy(
                    src_ref=x_refs[k].at[_index(peer)], dst_ref=o_refs[gi].at[_index(me), f], send_sem=send_sems.at[k, mask - 1],
                    recv_sem=recv_sems.at[k, mask - 1], device_id=peer, device_id_type=MESH))
                recvs.append(pltpu.make_async_remote_copy(
                    src_ref=x_refs[k].at[_index(me)], dst_ref=o_refs[gi].at[_index(peer), f], send_sem=send_sems.at[k, mask - 1],
                    recv_sem=recv_sems.at[k, mask - 1], device_id=peer, device_id_type=MESH))
        for cp in local + sends:
            cp.start()
        for cp in recvs:
            cp.wait_recv()
        for cp in sends:
            cp.wait_send()
        for cp in local:
            cp.wait()

    any_spec = pl.BlockSpec(memory_space=pl.ANY)
    return pl.pallas_call(
        body, name=name, in_specs=[any_spec] * n, out_specs=[any_spec] * n_groups,
        out_shape=[_sds((N_DEV, len(grp)) + grp[0].shape[1:], grp[0].dtype) for grp in groups],
        scratch_shapes=[pltpu.SemaphoreType.DMA((n, 7)), pltpu.SemaphoreType.DMA((n, 7)), pltpu.SemaphoreType.DMA((n,))],
    )(*[a for grp in groups for a in grp])


def _split_copies(kind, outgoing, x_refs, land_refs, send_sems, recv_sems):
    me = _mesh_pos()
    copies = []
    for k, (x_ref, land_ref) in enumerate(zip(x_refs, land_refs)):
        for mask in range(1, N_DEV):
            peer = _flip(me, mask)
            sem = k * (N_DEV - 1) + mask - 1
            copies.append(pltpu.make_async_remote_copy(
                src_ref=x_ref if kind == "gather" else x_ref.at[_index(peer)], dst_ref=land_ref.at[_index(me if outgoing else peer)],
                send_sem=send_sems.at[sem], recv_sem=recv_sems.at[sem], device_id=peer, device_id_type=MESH))
    return copies


_HBM_SPEC = pl.BlockSpec(memory_space=pltpu.HBM)
_SEM_SPEC = pl.BlockSpec(memory_space=pltpu.SEMAPHORE)
_EFFECT = pltpu.SideEffectType.DATAFLOW_SIDE_EFFECTING


def _split_start(name, kind, xs, after=None):
    n = len(xs)
    extra = [] if after is None else [after]
    lands = [lax.empty(((N_DEV,) + x.shape) if kind == "gather" else x.shape, x.dtype) for x in xs]

    def body(*refs):
        x_refs, land_refs = refs[:n], refs[n:2 * n]
        send_sems, recv_sems = refs[2 * n + len(extra)], refs[2 * n + len(extra) + 1]
        token = refs[-1]
        for cp in _split_copies(kind, True, x_refs, land_refs, send_sems, recv_sems):
            cp.start()
        token[...] = jnp.zeros_like(token)

    hbm = [pltpu.HBM(a.shape, a.dtype) for a in list(xs) + lands]
    res = pl.pallas_call(
        body, name=name,
        out_shape=[pltpu.SemaphoreType.DMA((n * (N_DEV - 1),)), pltpu.SemaphoreType.DMA((n * (N_DEV - 1),))] + hbm + [_sds((8, 128), F32)],
        in_specs=[_HBM_SPEC] * (2 * n) + [pl.BlockSpec(memory_space=pl.ANY)] * len(extra),
        out_specs=[_SEM_SPEC, _SEM_SPEC] + [_HBM_SPEC] * (2 * n) + [pl.BlockSpec(memory_space=pltpu.VMEM)],
        input_output_aliases={j: 2 + j for j in range(2 * n)}, compiler_params=pltpu.CompilerParams(has_side_effects=_EFFECT),
    )(*[pltpu.with_memory_space_constraint(a, pltpu.HBM) for a in list(xs) + lands], *extra)
    return (kind, n, res[0], res[1], res[2:2 + 2 * n]), res[-1]


def _split_wait(name, state, after):
    kind, n, send_sems_in, recv_sems_in, thru = state

    def body(*refs):
        x_refs, land_refs = refs[:n], refs[n:2 * n]
        send_sems, recv_sems = refs[2 * n], refs[2 * n + 1]
        for cp in _split_copies(kind, True, x_refs, land_refs, send_sems, recv_sems):
            cp.wait_send()
        for cp in _split_copies(kind, False, x_refs, land_refs, send_sems, recv_sems):
            cp.wait_recv()

    res = pl.pallas_call(
        body, name=name, out_shape=[pltpu.HBM(a.shape, a.dtype) for a in thru],
        in_specs=[_HBM_SPEC] * (2 * n) + [_SEM_SPEC, _SEM_SPEC, pl.BlockSpec(memory_space=pl.ANY)], out_specs=[_HBM_SPEC] * (2 * n),
        input_output_aliases={j: j for j in range(2 * n)}, compiler_params=pltpu.CompilerParams(has_side_effects=_EFFECT),
    )(*thru, send_sems_in, recv_sems_in, after)
    return res[n:]


def _with_own(land, own):
    me = _index(_mesh_pos())
    return lax.dynamic_update_slice(land, own[None], (me,) + (0,) * own.ndim)


def _rope_tables(s):
    inv = 1.0 / (ROPE_THETA ** (jnp.arange(0, QK_ROPE, 2, dtype=F32) / QK_ROPE))
    ang = jnp.arange(s, dtype=F32)[:, None] * inv[None, :]
    pad = jnp.zeros((s, KV_LORA - QK_ROPE), F32)
    return (jnp.concatenate([jnp.cos(ang), jnp.cos(ang), pad], axis=1), jnp.concatenate([jnp.sin(ang), jnp.sin(ang), pad], axis=1))


def _row_of(v):
    return v.reshape(1, -1)


def kernel(x, c, ada_w, ada_b, norm_g, ffn_w_in, ffn_w_out, pool_w, pool_b, pool_scale, mla_w_in, mla_q_norm, mla_kv_norm, mla_w_uq, mla_w_uk, mla_w_uv, mla_w_o, loss_target, m_ada_w, m_ada_b, m_norm_g, m_ffn_w_in, m_ffn_w_out, m_pool_w, m_pool_b, m_pool_scale, m_mla_w_in, m_mla_q_norm, m_mla_kv_norm, m_mla_w_uq, m_mla_w_uk, m_mla_w_uv, m_mla_w_o, v_ada_w, v_ada_b, v_norm_g, v_ffn_w_in, v_ffn_w_out, v_pool_w, v_pool_b, v_pool_scale, v_mla_w_in, v_mla_q_norm, v_mla_kv_norm, v_mla_w_uq, v_mla_w_uk, v_mla_w_uv, v_mla_w_o):
    s, d = x.shape[1], x.shape[2]
    tm = min(512, s)
    tr = min(256, s)
    me = 4 * lax.axis_index("x") + 2 * lax.axis_index("y") + lax.axis_index("c")
    x0 = x.reshape(s, d)
    target = loss_target.reshape(s, d)
    n_mod = ada_w.shape[2] * N_DEV // d
    mod_blk = ada_w.shape[2]

    small = jnp.concatenate([c.reshape(-1), norm_g.reshape(-1), pool_b.reshape(-1), mla_q_norm.reshape(-1)]).reshape(1, -1)
    w_in_loc = [ffn_w_in[i, f].astype(BF16) for i in range(2) for f in range(2)]
    w_out_loc = [ffn_w_out[i, f].astype(BF16) for i in range(2) for f in range(2)]
    (small_all,) = _all_gather("gather_small", [small])
    small_all = small_all.reshape(N_DEV, -1)
    c_all = small_all[:, :d]
    off = d
    g_all = small_all[:, off:off + 12 * (d // N_DEV)].reshape(N_DEV, 2, 6, d // N_DEV).transpose(1, 2, 0, 3).reshape(2, 6, d)
    off += 12 * (d // N_DEV)
    pool_b_all = small_all[:, off:off + 4 * 32].reshape(N_DEV, 4, 32).transpose(1, 0, 2).reshape(1, d)
    off += 4 * 32
    q_norm_all = small_all[:, off:off + 32].reshape(1, Q_LORA)
    kv_norm_row = mla_kv_norm.reshape(1, KV_LORA)
    pscale_row = pool_scale.reshape(1, d)

    even = (jnp.arange(N_HEADS) % 2 == 0)[:, None, None]
    cos_k, sin_k = _rope_tables(s)

    def mla_weights(mla_w_in_all, mla_w_uq_all, mla_w_o_all):
        uq = mla_w_uq_all.reshape(Q_LORA, N_HEADS, QK_NOPE + QK_ROPE).transpose(1, 0, 2)
        zq = jnp.zeros((N_HEADS, Q_LORA, QK_NOPE), BF16)
        wq = jnp.concatenate(
            [uq[:, :, :QK_NOPE], zq, uq[:, :, QK_NOPE:], jnp.zeros((N_HEADS, Q_LORA, QK_PAD - KV_LORA - QK_ROPE), BF16)], axis=2)
        wukp = jnp.pad(mla_w_uk[0].transpose(1, 2, 0).astype(BF16), ((0, 0), (0, QK_PAD - QK_NOPE), (0, QK_PAD - KV_LORA)))
        uv = mla_w_uv[0].transpose(1, 0, 2).astype(BF16)
        wuv2 = jnp.where(even, jnp.concatenate([uv, jnp.zeros_like(uv)], axis=2), jnp.concatenate([jnp.zeros_like(uv), uv], axis=2))
        return dict(w_in=jnp.pad(mla_w_in_all.reshape(d, -1), ((0, 0), (0, LAT_PAD - mla_w_in.shape[2]))), wq=wq, wukp=wukp,
                    wuv2=wuv2, w_o=mla_w_o_all.reshape(d, d), q_norm=q_norm_all, kv_norm=kv_norm_row)

    (sc_all,), _ = _rowmap("ada_silu", lambda cv: ((cv * jax.nn.sigmoid(cv),), ()), [(c_all, (N_DEV, d), lambda i: (0, 0))],
                           [(_sds((N_DEV, d), F32), (N_DEV, d), lambda i: (0, 0))], [], (1,))
    ada_b_loc = lax.dynamic_slice_in_dim(ada_b, me * mod_blk, mod_blk, axis=1).reshape(2, 1, mod_blk)
    m_pad = 2 * N_DEV
    modp = _matmul("ada_mod", jnp.pad(sc_all, ((0, m_pad - N_DEV), (0, 0))), ada_w, a_blk=(m_pad, d), a_map=lambda i, k: (0, 0),
                   b_blk=(None, d, mod_blk), b_map=lambda i, k: (i, 0, 0), o_shape=(2, m_pad, mod_blk), o_blk=(None, m_pad, mod_blk),
                   o_map=lambda i, k: (i, 0, 0), grid=(2, 1), contract=NN, out_dtype=F32, bias=ada_b_loc, bias_blk=(None, 1, mod_blk),
                   bias_map=lambda i, k: (i, 0, 0))[:, :N_DEV]
    modp_all, w_in_first, w_out_first = _all_gather("gather_first", [modp.reshape(2 * N_DEV, mod_blk), w_in_loc[0], w_out_loc[0]])
    later_a = [w_in_loc[1], w_out_loc[1], pool_w.reshape(-1, POOL_GROUP).astype(BF16)]
    later_b = [w_in_loc[2], w_out_loc[2], w_in_loc[3], w_out_loc[3], mla_w_in[0].astype(BF16),
               mla_w_uq.reshape(mla_w_uq.shape[1], -1).astype(BF16), mla_w_o[0].astype(BF16)]
    state_a, token_a = _split_start("gather_start_a", "gather", later_a, after=modp_all)
    state_b, token_b = _split_start("gather_start_b", "gather", later_b, after=token_a)
    w_in8 = [w_in_first, None, None, None]
    w_out4 = [w_out_first.reshape(4, FF_BLK, d), None, None, None]
    mod = lax.dynamic_index_in_dim(modp_all.reshape(N_DEV, 2, N_DEV, mod_blk), me, axis=2, keepdims=False)
    mod = mod.transpose(1, 0, 2).reshape(2, n_mod, d) + (token_a[0, 0] + token_b[0, 0])

    saved = []
    xs = x0
    w4 = mla_wts = None
    for i in range(2):
        for sub in range(3):
            if (i, sub) == (0, 1):
                lands = _split_wait("gather_wait_a", state_a, xs)
                w_in8[1] = _with_own(lands[0], later_a[0])
                w_out4[1] = _with_own(lands[1], later_a[1]).reshape(4, FF_BLK, d)
                w4 = _with_own(lands[2], later_a[2]).reshape(N_DEV, 4, 32, POOL_GROUP).transpose(1, 0, 2, 3)
                w4 = w4.reshape(4, POOL_GROUP, POOL_GROUP)
            if (i, sub) == (1, 0):
                lands = [_with_own(land, own) for land, own in zip(_split_wait("gather_wait_b", state_b, xs), later_b)]
                w_in8[2], w_in8[3] = lands[0], lands[2]
                w_out4[2], w_out4[3] = lands[1].reshape(4, FF_BLK, d), lands[3].reshape(4, FF_BLK, d)
                mla_wts = mla_weights(*lands[4:])
            shift, scale, gate = (_row_of(mod[i, 3 * sub + j]) for j in range(3))
            g_pre, g_post = _row_of(g_all[i, 2 * sub]), _row_of(g_all[i, 2 * sub + 1])
            tag = f"{i}{sub}"
            if sub != 1:
                f = sub // 2
                h = _prenorm(f"prenorm_{tag}", xs, g_pre, scale, shift, BF16, tr)
                u, extra = _ffn_fwd(tag, h, w_in8[2 * i + f], w_out4[2 * i + f], tm)
                weight = 0.5
            elif i == 0:
                h = _prenorm(f"prenorm_{tag}", xs, g_pre, scale, shift, F32, tr)
                u, z, v = _pool_fwd(h, w4, pool_b_all, pscale_row, tm)
                extra = (z, v)
                weight = 1.0
            else:
                h = _prenorm(f"prenorm_{tag}", xs, g_pre, scale, shift, BF16, tr)
                u, extra = _mla_fwd(h, mla_wts, cos_k, sin_k, tm)
                weight = 1.0
            saved.append((xs, h, u, extra, (shift, scale, gate, g_pre, g_post), weight))
            xs = _postnorm(f"postnorm_{tag}", xs, u, g_post, gate, weight, tr)

    def loss_fn(yv, tv):
        e = yv - tv
        return (e * (1.0 / d),), (_colsum(e * e),)

    (dx,), (sq,) = _rowmap("loss", loss_fn, [_tile(xs, tr), _tile(target, tr)], [_otile(s, d, F32, tr)], [_ored(d)], (s // tr,))
    loss = lax.psum(0.5 * jnp.sum(sq) / d, AXES)

    d_mod = [[None] * n_mod for _ in range(2)]
    d_g = [[None] * 6 for _ in range(2)]
    sent = {}
    pool_grads = mla_grads = None

    def start_scatter(key, arrays):
        state, token = _split_start(f"scatter_start_{key}", "scatter", arrays)
        sent[key] = (state, arrays)
        return token

    for i in (1, 0):
        for sub in (2, 1, 0):
            xin, h, u, extra, (shift, scale, gate, g_pre, g_post), weight = saved[3 * i + sub]
            tag = f"{i}{sub}"
            du, (dgate, dgpost) = _postnorm_bwd(f"postnorm_bwd_{tag}", dx, u, g_post, gate, weight, F32 if (sub == 1 and i == 0) else BF16, tr)
            if sub != 1:
                k = 2 * i + sub // 2
                dgu, act = _ffn_bwd_act(tag, du, extra, w_out4[k], tm)
                dw_out = _ffn_dw_out(tag, act, du, tm).reshape(N_DEV, FF_BLK // 2, d)
                if k == 0:
                    d_pool_w = pool_grads[0].reshape(4, N_DEV, 32, POOL_GROUP).transpose(1, 0, 2, 3).reshape(N_DEV, 4 * 32, POOL_GROUP)
                    token = start_scatter(tag + "_out", [dw_out, d_pool_w])
                    token = start_scatter(tag + "_in", [_ffn_dw_in(tag, h, dgu, tm, after=token)])
                else:
                    token = start_scatter(tag, [_ffn_dw_in(tag, h, dgu, tm), dw_out])
                dh = _ffn_dh(tag, dgu, w_in8[k], tm, after=token)
            elif i == 0:
                dh, dw4, (dpscale, dpb) = _pool_bwd(du, extra[0], extra[1], w4, pscale_row, tm)
                pool_grads = (dw4, dpscale, dpb)
            else:
                dh, mla_grads = _mla_bwd(du, h, extra, mla_wts, cos_k, sin_k, tm)
                dwq = mla_grads["wq"]
                d_uq = jnp.concatenate([dwq[:, :, :QK_NOPE], dwq[:, :, KV_LORA:KV_LORA + QK_ROPE]], axis=2).transpose(1, 0, 2)
                token = start_scatter("mla", [mla_grads["w_in"][:, :mla_w_in.shape[2]].reshape(N_DEV, d // N_DEV, -1),
                                              d_uq.reshape(N_DEV, Q_LORA // N_DEV, -1), mla_grads["w_o"].reshape(N_DEV, d // N_DEV, d)])
                dwukp, dwuv2 = mla_grads["wukp"], mla_grads["wuv2"]
                d_uk = dwukp[:, :QK_NOPE, :KV_LORA].transpose(2, 0, 1).reshape(KV_LORA, -1)
                d_uv = jnp.where(even, dwuv2[:, :, :V_HEAD], dwuv2[:, :, V_HEAD:]).transpose(1, 0, 2).reshape(KV_LORA, -1)
                state_ukv, token_ukv = _split_start("gather_start_ukv", "gather", [d_uk, d_uv], after=token)
                g_pre = g_pre + token_ukv[0, 0]
            dx, (dshift, dscale, dgpre) = _prenorm_bwd(f"prenorm_bwd_{tag}", dh, xin, dx, g_pre, scale, tr)
            d_mod[i][3 * sub:3 * sub + 3] = [dshift, dscale, dgate]
            d_g[i][2 * sub:2 * sub + 2] = [dgpre, dgpost]
    grad_x = dx.reshape(x.shape)

    def upd(name, parts, w, m, v):
        shape = w.shape
        r, cdim = parts.shape[1], parts.shape[2]
        return [o.reshape(shape) for o in _adamw(name, parts, w.reshape(r, cdim), m.reshape(r, cdim), v.reshape(r, cdim))]

    def landed(key, after):
        state, arrays = sent[key]
        lands = _split_wait(f"scatter_wait_{key}", state, after)
        return [_with_own(land, lax.dynamic_index_in_dim(a, me, 0, keepdims=False)) for land, a in zip(lands, arrays)]

    res = {}
    w_in_s, m_in_s, v_in_s = (a.reshape(4, d, FF_BLK) for a in (ffn_w_in, m_ffn_w_in, v_ffn_w_in))
    w_out_s, m_out_s, v_out_s = (a.reshape(4, FF_BLK // 2, d) for a in (ffn_w_out, m_ffn_w_out, v_ffn_w_out))
    bufs_in = [lax.empty(w_in_s.shape, F32) for _ in range(4)]
    bufs_out = [lax.empty(w_out_s.shape, F32) for _ in range(4)]
    for key, k in (("12", 3), ("mla", None), ("10", 2), ("02", 1)):
        parts = landed(key, grad_x)
        if k is None:
            res["mla_w_in"] = upd("adam_mla_w_in", parts[0], mla_w_in, m_mla_w_in, v_mla_w_in)
            res["mla_w_uq"] = upd("adam_mla_w_uq", parts[1], mla_w_uq, m_mla_w_uq, v_mla_w_uq)
            res["mla_w_o"] = upd("adam_mla_w_o", parts[2], mla_w_o, m_mla_w_o, v_mla_w_o)
            uk_all, uv_all = (_with_own(land, own) for land, own in zip(_split_wait("gather_wait_ukv", state_ukv, grad_x), (d_uk, d_uv)))
            res["mla_w_uk"] = upd("adam_mla_w_uk", uk_all, mla_w_uk, m_mla_w_uk, v_mla_w_uk)
            res["mla_w_uv"] = upd("adam_mla_w_uv", uv_all, mla_w_uv, m_mla_w_uv, v_mla_w_uv)
            continue
        bufs_in = _adamw_slab(f"adam_ffn_w_in_{key}", parts[0], w_in_s, m_in_s, v_in_s, bufs_in, k)
        bufs_out = _adamw_slab(f"adam_ffn_w_out_{key}", parts[1], w_out_s, m_out_s, v_out_s, bufs_out, k)

    dw4, dpscale, dpb = pool_grads
    d_mod_row = jnp.concatenate([jnp.concatenate(r, axis=1) for r in d_mod], axis=1)
    d_g_row = jnp.concatenate([jnp.concatenate(r, axis=1) for r in d_g], axis=1)
    small_g = jnp.concatenate([d_mod_row, d_g_row, dpb, dpscale, mla_grads["q_norm"], mla_grads["kv_norm"]], axis=1)
    (small_g_all,) = _all_gather("gather_small_grads", [small_g], after=bufs_out[0])
    small_g_all = small_g_all.reshape(N_DEV, -1)
    n_m = 2 * n_mod * d
    d_mod_all = small_g_all[:, :n_m].reshape(N_DEV, 2, n_mod * d)
    rest = small_g_all[:, n_m:]
    p_norm_g = lax.dynamic_slice_in_dim(rest[:, :12 * d].reshape(N_DEV, 12, d), me * (d // N_DEV), d // N_DEV, axis=2)
    p_pool_b = lax.dynamic_slice_in_dim(rest[:, 12 * d:13 * d].reshape(N_DEV, 4, POOL_GROUP), me * 32, 32, axis=2)
    p_pool_scale = rest[:, 13 * d:14 * d].reshape(N_DEV, 1, d)
    p_q_norm = lax.dynamic_slice_in_dim(rest[:, 14 * d:14 * d + Q_LORA], me * 32, 32, axis=1).reshape(N_DEV, 1, 32)
    p_kv_norm = rest[:, 14 * d + Q_LORA:].reshape(N_DEV, 1, KV_LORA)

    d_mod_loc = lax.dynamic_slice_in_dim(d_mod_all, me * mod_blk, mod_blk, axis=2).transpose(1, 0, 2)
    k_pad = 128
    sc_t = jnp.pad(sc_all.T, ((0, 0), (0, k_pad - N_DEV)))
    d_ada_w = _matmul("ada_dw", sc_t, jnp.pad(d_mod_loc, ((0, 0), (0, k_pad - N_DEV), (0, 0))), a_blk=(d, k_pad),
                      a_map=lambda i, k: (0, 0), b_blk=(None, k_pad, mod_blk), b_map=lambda i, k: (i, 0, 0), o_shape=(2, d, mod_blk),
                      o_blk=(None, d, mod_blk), o_map=lambda i, k: (i, 0, 0), grid=(2, 1), contract=NN, out_dtype=F32)
    res["ada_w"] = upd("adam_ada_w", d_ada_w.reshape(1, 2 * d, mod_blk), ada_w, m_ada_w, v_ada_w)
    res["ada_b"] = upd("adam_ada_b", d_mod_all.reshape(N_DEV, 2, n_mod * d), ada_b, m_ada_b, v_ada_b)
    res["norm_g"] = upd("adam_norm_g", p_norm_g, norm_g, m_norm_g, v_norm_g)
    res["pool_b"] = upd("adam_pool_b", p_pool_b, pool_b, m_pool_b, v_pool_b)
    res["pool_scale"] = upd("adam_pool_scale", p_pool_scale, pool_scale, m_pool_scale, v_pool_scale)
    res["mla_q_norm"] = upd("adam_mla_q_norm", p_q_norm, mla_q_norm, m_mla_q_norm, v_mla_q_norm)
    res["mla_kv_norm"] = upd("adam_mla_kv_norm", p_kv_norm, mla_kv_norm, m_mla_kv_norm, v_mla_kv_norm)

    p_out, p_pool_w = landed("00_out", res["ada_w"][1])
    bufs_out = _adamw_slab("adam_ffn_w_out_00", p_out, w_out_s, m_out_s, v_out_s, bufs_out, 0)
    res["pool_w"] = upd("adam_pool_w", p_pool_w, pool_w, m_pool_w, v_pool_w)
    (p_in,) = landed("00_in", res["pool_w"][1])
    bufs_in = _adamw_slab("adam_ffn_w_in_00", p_in, w_in_s, m_in_s, v_in_s, bufs_in, 0)
    res["ffn_w_in"] = [b.reshape(ffn_w_in.shape) for b in bufs_in]
    res["ffn_w_out"] = [b.reshape(ffn_w_out.shape) for b in bufs_out]

    order = ["ada_w", "ada_b", "norm_g", "ffn_w_in", "ffn_w_out", "pool_w", "pool_b", "pool_scale", "mla_w_in", "mla_q_norm",
             "mla_kv_norm", "mla_w_uq", "mla_w_uk", "mla_w_uv", "mla_w_o"]
    outs = [loss, grad_x]
    for j in range(4):
        outs += [res[name][j] for name in order]
    return tuple(outs)
```

```python
import functools

import jax
import jax.numpy as jnp
from jax import lax
from jax.experimental import pallas as pl
from jax.experimental.pallas import tpu as pltpu

F32 = jnp.float32
BF16 = jnp.bfloat16
N_DEV = 8
AXES = ("x", "y", "c")
MESH = pl.DeviceIdType.MESH

D_MODEL = 1024
N_HEADS = 16
QK_NOPE = 64
QK_ROPE = 32
V_HEAD = 64
Q_LORA = 256
KV_LORA = 128
LAT_PAD = 512
QK_PAD = 256
D_FF = 2816
FF_BLK = 2 * D_FF // N_DEV
POOL_WINDOWS = (2, 4, 8, 16)
POOL_GROUP = 256
ROPE_THETA = 10000.0
EPS = 1e-6
ATTN_SCALE = (QK_NOPE + QK_ROPE) ** -0.5
LOG2_E = 1.4426950408889634
ADAM_LR, ADAM_B1, ADAM_B2, ADAM_EPS, ADAM_WD, ADAM_STEP = 0.001, 0.9, 0.999, 1e-08, 0.01, 10
VMEM_LIMIT = 56 * 1024 * 1024

NN = ((1,), (0,))
NT = ((1,), (1,))
TN = ((0,), (0,))


def _params(**kw):
    return pltpu.CompilerParams(vmem_limit_bytes=VMEM_LIMIT, **kw)


def _dot(a, b, contract):
    return lax.dot_general(a, b, (contract, ((), ())), preferred_element_type=F32)


def _matmul(name, a, b, *, a_blk, a_map, b_blk, b_map, o_shape, o_blk, o_map, grid, contract, out_dtype,
            bias=None, bias_blk=None, bias_map=None, after=None):
    n_k = grid[-1]
    k_axis = len(grid) - 1
    acc_shape = tuple(d for d in o_blk if d is not None)

    def body(*refs):
        a_ref, b_ref = refs[:2]
        bias_ref = refs[2] if bias is not None else None
        if n_k == 1:
            r = _dot(a_ref[...].astype(BF16), b_ref[...].astype(BF16), contract)
            if bias is not None:
                r = r + bias_ref[...]
            refs[-1][...] = r.astype(refs[-1].dtype)
            return
        o_ref, acc = refs[-2:]
        k = pl.program_id(k_axis)

        @pl.when(k == 0)
        def _():
            acc[...] = jnp.zeros_like(acc)

        acc[...] += _dot(a_ref[...].astype(BF16), b_ref[...].astype(BF16), contract)

        @pl.when(k == n_k - 1)
        def _():
            r = acc[...]
            if bias is not None:
                r = r + bias_ref[...]
            o_ref[...] = r.astype(o_ref.dtype)

    in_specs = [pl.BlockSpec(a_blk, a_map), pl.BlockSpec(b_blk, b_map)]
    args = [a, b]
    if bias is not None:
        in_specs.append(pl.BlockSpec(bias_blk, bias_map))
        args.append(bias)
    if after is not None:
        in_specs.append(pl.BlockSpec(memory_space=pl.ANY))
        args.append(after)
    return pl.pallas_call(
        body, name=name, grid=grid, in_specs=in_specs, out_specs=pl.BlockSpec(o_blk, o_map),
        out_shape=jax.ShapeDtypeStruct(o_shape, out_dtype), scratch_shapes=[pltpu.VMEM(acc_shape, F32)] if n_k > 1 else [],
        compiler_params=_params(),
    )(*args)


def _rowmap(name, fn, ins, outs, reds, grid):
    n_in, n_out, n_red = len(ins), len(outs), len(reds)

    def body(*refs):
        in_refs = refs[:n_in]
        out_refs = refs[n_in:n_in + n_out]
        red_refs = refs[n_in + n_out:]
        out_vals, red_vals = fn(*[r[...] for r in in_refs])
        for r, v in zip(out_refs, out_vals):
            r[...] = v.astype(r.dtype)
        if n_red:
            first = pl.program_id(0) == 0
            for ax in range(1, len(grid)):
                first = jnp.logical_and(first, pl.program_id(ax) == 0)

            @pl.when(first)
            def _():
                for r in red_refs:
                    r[...] = jnp.zeros_like(r)

            for r, v in zip(red_refs, red_vals):
                r[...] += v

    res = pl.pallas_call(
        body, name=name, grid=grid,
        in_specs=[pl.BlockSpec(blk, imap) for _, blk, imap in ins],
        out_specs=[pl.BlockSpec(blk, imap) for _, blk, imap in list(outs) + list(reds)],
        out_shape=[sds for sds, _, _ in list(outs) + list(reds)],
        compiler_params=_params(),
    )(*[a for a, _, _ in ins])
    return res[:n_out], res[n_out:]


def _sds(shape, dtype):
    return jax.ShapeDtypeStruct(shape, dtype)


def _tile(a, tm):
    return (a, (tm, a.shape[1]), lambda i: (i, 0))


def _row(a):
    return (a, (1, a.shape[1]), lambda i: (0, 0))


def _otile(n, c, dtype, tm):
    return (_sds((n, c), dtype), (tm, c), lambda i: (i, 0))


def _ored(c):
    return (_sds((1, c), F32), (1, c), lambda i: (0, 0))


def _colsum(v):
    return jnp.sum(v, axis=0, keepdims=True)


def _rstd(v):
    return lax.rsqrt(jnp.mean(v * v, axis=-1, keepdims=True) + EPS)


def _prenorm(name, x, g_pre, scale, shift, out_dtype, tm):
    n, d = x.shape

    def fn(xv, g, sc, sh):
        return (xv * _rstd(xv) * g * (1.0 + sc) + sh,), ()

    (h,), _ = _rowmap(name, fn, [_tile(x, tm), _row(g_pre), _row(scale), _row(shift)], [_otile(n, d, out_dtype, tm)], [],
                      (n // tm,))
    return h


def _postnorm(name, x, u, g_post, gate, weight, tm):
    n, d = x.shape

    def fn(xv, uv, g, gt):
        return (xv + weight * (1.0 + gt) * (uv * _rstd(uv) * g),), ()

    (y,), _ = _rowmap(name, fn, [_tile(x, tm), _tile(u, tm), _row(g_post), _row(gate)], [_otile(n, d, F32, tm)], [], (n // tm,))
    return y


def _postnorm_bwd(name, dout, u, g_post, gate, weight, out_dtype, tm):
    n, d = u.shape

    def fn(dv, uv, g, gt):
        r = _rstd(uv)
        un = uv * r
        dy = dv * (weight * (1.0 + gt))
        a = dy * g
        du = r * (a - un * jnp.mean(a * un, axis=-1, keepdims=True))
        return (du,), (_colsum(dv * (weight * (un * g))), _colsum(dy * un))

    (du,), reds = _rowmap(name, fn, [_tile(dout, tm), _tile(u, tm), _row(g_post), _row(gate)], [_otile(n, d, out_dtype, tm)],
                          [_ored(d), _ored(d)], (n // tm,))
    return du, reds


def _prenorm_bwd(name, dh, x, dout, g_pre, scale, tm):
    n, d = x.shape

    def fn(dhv, xv, dv, g, sc):
        dhv = dhv.astype(F32)
        r = _rstd(xv)
        xn = xv * r
        b = dhv * (g * (1.0 + sc))
        dx = dv + r * (b - xn * jnp.mean(b * xn, axis=-1, keepdims=True))
        return (dx,), (_colsum(dhv), _colsum(dhv * (xn * g)), _colsum(dhv * ((1.0 + sc) * xn)))

    (dx,), reds = _rowmap(name, fn, [_tile(dh, tm), _tile(x, tm), _tile(dout, tm), _row(g_pre), _row(scale)],
                          [_otile(n, d, F32, tm)], [_ored(d), _ored(d), _ored(d)], (n // tm,))
    return dx, reds


def _ffn_fwd(tag, h, w_in8, w_out4, tm, sub):
    s, d = h.shape

    def body(h_ref, wg_ref, wu_ref, wo_ref, u_ref, gu_ref):
        @pl.when(pl.program_id(1) == 0)
        def _():
            u_ref[...] = jnp.zeros_like(u_ref)

        for r in range(tm // sub):
            rows = pl.ds(r * sub, sub)
            hv = h_ref[rows, :]
            gate = _dot(hv, wg_ref[...], NN)
            up = _dot(hv, wu_ref[...], NN)
            gu_ref[0, rows, :] = gate.astype(BF16)
            gu_ref[1, rows, :] = up.astype(BF16)
            u_ref[rows, :] += _dot((gate * jax.nn.sigmoid(gate) * up).astype(BF16), wo_ref[...], NN)

    w_blk = (None, d, FF_BLK)
    return pl.pallas_call(
        body, name=f"ffn_fwd_{tag}", grid=(s // tm, 4),
        in_specs=[pl.BlockSpec((tm, d), lambda i, j: (i, 0)), pl.BlockSpec(w_blk, lambda i, j: (j, 0, 0)),
                  pl.BlockSpec(w_blk, lambda i, j: (j + 4, 0, 0)), pl.BlockSpec((None, FF_BLK, d), lambda i, j: (j, 0, 0))],
        out_specs=[pl.BlockSpec((tm, d), lambda i, j: (i, 0)), pl.BlockSpec((2, None, tm, FF_BLK), lambda i, j: (0, j, i, 0))],
        out_shape=[_sds((s, d), F32), _sds((2, 4, s, FF_BLK), BF16)], compiler_params=_params(),
    )(h, w_in8, w_in8, w_out4)


def _ffn_bwd_act(tag, du, gu, w_in8, w_out4, tm, sub):
    s, d = du.shape

    def body(du_ref, gu_ref, wg_ref, wu_ref, wo_ref, dh_ref, dgu_ref, act_ref):
        @pl.when(pl.program_id(1) == 0)
        def _():
            dh_ref[...] = jnp.zeros_like(dh_ref)

        for r in range(tm // sub):
            rows = pl.ds(r * sub, sub)
            dact = _dot(du_ref[rows, :], wo_ref[...], NT)
            gate, up = gu_ref[0, rows, :].astype(F32), gu_ref[1, rows, :].astype(F32)
            sg = jax.nn.sigmoid(gate)
            silu = gate * sg
            dg = (dact * up * (sg * (1.0 + gate * (1.0 - sg)))).astype(BF16)
            dup = (dact * silu).astype(BF16)
            dgu_ref[0, rows, :] = dg
            dgu_ref[1, rows, :] = dup
            act_ref[rows, :] = (silu * up).astype(BF16)
            dh_ref[rows, :] += _dot(dg, wg_ref[...], NT) + _dot(dup, wu_ref[...], NT)

    w_blk = (None, d, FF_BLK)
    gu_blk = pl.BlockSpec((2, None, tm, FF_BLK), lambda i, j: (0, j, i, 0))
    dh, dgu, act = pl.pallas_call(
        body, name=f"ffn_bwd_{tag}", grid=(s // tm, 4),
        in_specs=[pl.BlockSpec((tm, d), lambda i, j: (i, 0)), gu_blk, pl.BlockSpec(w_blk, lambda i, j: (j, 0, 0)),
                  pl.BlockSpec(w_blk, lambda i, j: (j + 4, 0, 0)), pl.BlockSpec((None, FF_BLK, d), lambda i, j: (j, 0, 0))],
        out_specs=[pl.BlockSpec((tm, d), lambda i, j: (i, 0)), gu_blk, pl.BlockSpec((None, tm, FF_BLK), lambda i, j: (j, i, 0))],
        out_shape=[_sds((s, d), F32), _sds((2, 4, s, FF_BLK), BF16), _sds((4, s, FF_BLK), BF16)], compiler_params=_params(),
    )(du, gu, w_in8, w_in8, w_out4)
    return dh, dgu.reshape(8, s, FF_BLK), act


def _ffn_dw_in(tag, h, dgu, tm, after=None):
    s, d = h.shape
    return _matmul(f"ffn_dwin_{tag}", h, dgu, a_blk=(tm, d), a_map=lambda g, k: (k, 0), b_blk=(None, tm, FF_BLK),
                   b_map=lambda g, k: (g, k, 0), o_shape=(8, d, FF_BLK), o_blk=(None, d, FF_BLK), o_map=lambda g, k: (g, 0, 0),
                   grid=(8, s // tm), contract=TN, out_dtype=BF16, after=after)


def _ffn_dw_out(tag, act, du, tm):
    s, d = du.shape
    return _matmul(f"ffn_dwout_{tag}", act, du, a_blk=(None, tm, FF_BLK), a_map=lambda j, k: (j, k, 0), b_blk=(tm, d),
                   b_map=lambda j, k: (k, 0), o_shape=(4, FF_BLK, d), o_blk=(None, FF_BLK, d), o_map=lambda j, k: (j, 0, 0),
                   grid=(4, s // tm), contract=TN, out_dtype=BF16)


def _window_sum(x, window, transpose):
    s = x.shape[0]
    t = lax.broadcasted_iota(jnp.int32, (s, 1), 0)
    half = window // 2
    cnt = jnp.minimum(t + half, s) - jnp.maximum(t - half, 0)
    inv = 1.0 / cnt.astype(F32)
    if transpose:
        x = x * inv
        offsets = range(-half + 1, half + 1)
    else:
        offsets = range(-half, half)
    acc = jnp.zeros_like(x)
    for o in offsets:
        shifted = x if o == 0 else pltpu.roll(x, (-o) % s, 0)
        valid = jnp.logical_and(t + o >= 0, t + o < s)
        acc = acc + jnp.where(valid, shifted, 0.0)
    return acc if transpose else acc * inv


def _pool_mix(name, x, transpose, out_dtype):
    s, d = x.shape

    def body(x_ref, o_ref):
        g = pl.program_id(0)
        for gi, window in enumerate(POOL_WINDOWS):
            @pl.when(g == gi)
            def _(window=window):
                xv = x_ref[...].astype(F32)
                o_ref[...] = (_window_sum(xv, window, transpose) - xv).astype(o_ref.dtype)

    return pl.pallas_call(
        body, name=name, grid=(len(POOL_WINDOWS),), in_specs=[pl.BlockSpec((s, POOL_GROUP), lambda g: (0, g))],
        out_specs=pl.BlockSpec((s, POOL_GROUP), lambda g: (0, g)), out_shape=_sds((s, d), out_dtype), compiler_params=_params(),
    )(x)


def _pool_fwd(h, w4, bias, pscale, tm):
    s, d = h.shape
    nt = s // tm
    z = _pool_mix("pool_mix", h, False, BF16)
    v = _matmul("pool_proj", z, w4, a_blk=(tm, POOL_GROUP), a_map=lambda i, g, k: (i, g), b_blk=(None, POOL_GROUP, POOL_GROUP),
                b_map=lambda i, g, k: (g, 0, 0), o_shape=(s, d), o_blk=(tm, POOL_GROUP), o_map=lambda i, g, k: (i, g),
                grid=(nt, 4, 1), contract=NN, out_dtype=F32, bias=bias, bias_blk=(1, POOL_GROUP), bias_map=lambda i, g, k: (0, g))
    (u,), _ = _rowmap("pool_scale", lambda vv, ps: ((vv * ps,), ()), [_tile(v, tm), _row(pscale)], [_otile(s, d, F32, tm)], [],
                      (nt,))
    return u, z, v


def _pool_bwd(du, z, v, w4, pscale, tm):
    s, d = du.shape
    nt = s // tm

    def fn(duv, vv, ps):
        dv = duv * ps
        return (dv,), (_colsum(duv * vv), _colsum(dv))

    (dv,), reds = _rowmap("pool_dscale", fn, [_tile(du, tm), _tile(v, tm), _row(pscale)], [_otile(s, d, BF16, tm)],
                          [_ored(d), _ored(d)], (nt,))
    dw4 = _matmul("pool_dw", z, dv, a_blk=(tm, POOL_GROUP), a_map=lambda g, k: (k, g), b_blk=(tm, POOL_GROUP),
                  b_map=lambda g, k: (k, g), o_shape=(4, POOL_GROUP, POOL_GROUP), o_blk=(None, POOL_GROUP, POOL_GROUP),
                  o_map=lambda g, k: (g, 0, 0), grid=(4, nt), contract=TN, out_dtype=F32)
    dz = _matmul("pool_dz", dv, w4, a_blk=(tm, POOL_GROUP), a_map=lambda i, g, k: (i, g), b_blk=(None, POOL_GROUP, POOL_GROUP),
                 b_map=lambda i, g, k: (g, 0, 0), o_shape=(s, d), o_blk=(tm, POOL_GROUP), o_map=lambda i, g, k: (i, g),
                 grid=(nt, 4, 1), contract=NT, out_dtype=F32)
    dh = _pool_mix("pool_mix_t", dz, True, F32)
    return dh, dw4, reds


def _lane(shape):
    return lax.broadcasted_iota(jnp.int32, shape, 1)


def _rope_swap(v, transpose):
    half = QK_ROPE // 2
    lane = _lane(v.shape)
    up = pltpu.roll(v, v.shape[1] - half, 1)
    down = pltpu.roll(v, half, 1)
    if transpose:
        return jnp.where(lane < half, up, jnp.where(lane < QK_ROPE, -down, 0.0))
    return jnp.where(lane < half, -up, jnp.where(lane < QK_ROPE, down, 0.0))


def _rope(v, cos, sin):
    return v * cos + _rope_swap(v, False) * sin


def _rope_t(g, cos, sin):
    return g * cos + _rope_swap(g * sin, True)


def _mla_mid(lat, q_norm, kv_norm, cos_k, sin_k, tm):
    s = lat.shape[0]

    def fn(lv, qn, kn, cs, sn):
        cq = lv[:, :Q_LORA]
        ckv = lv[:, Q_LORA:Q_LORA + KV_LORA]
        kr = lv[:, Q_LORA + KV_LORA:]
        cq = cq * _rstd(cq) * qn
        ckv = ckv * _rstd(ckv) * kn
        return (cq, jnp.concatenate([ckv, _rope(kr, cs, sn)], axis=1)), ()

    (cq, kcat), _ = _rowmap("mla_mid", fn, [_tile(lat, tm), _row(q_norm), _row(kv_norm), _tile(cos_k, tm), _tile(sin_k, tm)],
                            [_otile(s, Q_LORA, BF16, tm), _otile(s, QK_PAD, BF16, tm)], [], (s // tm,))
    return cq, kcat


def _mla_mid_bwd(lat, dcq, dkcat, dv, q_norm, kv_norm, cos_k, sin_k, tm):
    s = lat.shape[0]

    def fn(lv, dq, dk, dvv, qn, kn, cs, sn):
        dk = dk * (1.0 / LOG2_E)
        cq = lv[:, :Q_LORA]
        ckv = lv[:, Q_LORA:Q_LORA + KV_LORA]
        rq, rk = _rstd(cq), _rstd(ckv)
        cqn, ckn = cq * rq, ckv * rk
        a = dq * qn
        d_cq = rq * (a - cqn * jnp.mean(a * cqn, axis=-1, keepdims=True))
        dckv = dk[:, :KV_LORA] + dvv
        a2 = dckv * kn
        d_ckv = rk * (a2 - ckn * jnp.mean(a2 * ckn, axis=-1, keepdims=True))
        d_kr = _rope_t(dk[:, KV_LORA:], cs, sn)
        return (jnp.concatenate([d_cq, d_ckv, d_kr], axis=1),), (_colsum(dq * cqn), _colsum(dckv * ckn))

    (dlat,), reds = _rowmap(
        "mla_mid_bwd", fn,
        [_tile(lat, tm), _tile(dcq, tm), _tile(dkcat, tm), _tile(dv, tm), _row(q_norm), _row(kv_norm), _tile(cos_k, tm),
         _tile(sin_k, tm)],
        [_otile(s, LAT_PAD, BF16, tm)], [_ored(Q_LORA), _ored(KV_LORA)], (s // tm,))
    return dlat, reds


def _mla_q(cq, wq, wukp, cos_k, sin_k, tm):
    s = cq.shape[0]

    def body(cq_ref, wq_ref, wuk_ref, cos_ref, sin_ref, o_ref):
        aq = _dot(cq_ref[...], wq_ref[...], NN)
        qlat = _dot(aq.astype(BF16), wuk_ref[...], NN)
        roped = _rope(aq[:, KV_LORA:], cos_ref[...], sin_ref[...])
        o_ref[...] = (jnp.concatenate([qlat[:, :KV_LORA], roped], axis=1) * (ATTN_SCALE * LOG2_E)).astype(o_ref.dtype)

    wblk = pl.BlockSpec((None, QK_PAD, QK_PAD), lambda h, i: (h, 0, 0))
    tblk = pl.BlockSpec((tm, KV_LORA), lambda h, i: (i, 0))
    return pl.pallas_call(
        body, name="mla_q", grid=(N_HEADS, s // tm),
        in_specs=[pl.BlockSpec((tm, Q_LORA), lambda h, i: (i, 0)), wblk, wblk, tblk, tblk],
        out_specs=pl.BlockSpec((None, tm, QK_PAD), lambda h, i: (h, i, 0)), out_shape=_sds((N_HEADS, s, QK_PAD), BF16),
        compiler_params=_params(),
    )(cq, wq, wukp, cos_k, sin_k)


def _mla_q_bwd(cq, wq, wukp, cos_k, sin_k, dqcat, tm):
    s = cq.shape[0]

    def body(cq_ref, wq_ref, wuk_ref, cos_ref, sin_ref, dq_ref, dcq_ref, dwq_ref, dwuk_ref):
        h, i = pl.program_id(0), pl.program_id(1)
        cqv = cq_ref[...]
        aq = _dot(cqv, wq_ref[...], NN).astype(BF16)
        g = dq_ref[...].astype(F32) * ATTN_SCALE
        gl, gr = g[:, :KV_LORA], g[:, KV_LORA:]
        dqlat = jnp.concatenate([gl, jnp.zeros_like(gl)], axis=1).astype(BF16)
        d_rope = _rope_t(gr, cos_ref[...], sin_ref[...])
        daq = _dot(dqlat, wuk_ref[...], NT) + jnp.concatenate([jnp.zeros_like(d_rope), d_rope], axis=1)
        daq_b = daq.astype(BF16)
        d_wuk = _dot(aq, dqlat, TN)
        d_wq = _dot(cqv, daq_b, TN)
        d_cq = _dot(daq_b, wq_ref[...], NT)
        rows = pl.ds(pl.multiple_of(i * tm, tm), tm)

        @pl.when(i == 0)
        def _():
            dwq_ref[...] = d_wq
            dwuk_ref[...] = d_wuk

        @pl.when(i != 0)
        def _():
            dwq_ref[...] += d_wq
            dwuk_ref[...] += d_wuk

        @pl.when(h == 0)
        def _():
            dcq_ref[rows, :] = d_cq

        @pl.when(h != 0)
        def _():
            dcq_ref[rows, :] += d_cq

    wblk = pl.BlockSpec((None, QK_PAD, QK_PAD), lambda h, i: (h, 0, 0))
    tblk = pl.BlockSpec((tm, KV_LORA), lambda h, i: (i, 0))
    return pl.pallas_call(
        body, name="mla_q_bwd", grid=(N_HEADS, s // tm),
        in_specs=[pl.BlockSpec((tm, Q_LORA), lambda h, i: (i, 0)), wblk, wblk, tblk, tblk,
                  pl.BlockSpec((None, tm, QK_PAD), lambda h, i: (h, i, 0))],
        out_specs=[pl.BlockSpec((s, Q_LORA), lambda h, i: (0, 0)), wblk, wblk],
        out_shape=[_sds((s, Q_LORA), F32), _sds((N_HEADS, QK_PAD, QK_PAD), F32), _sds((N_HEADS, QK_PAD, QK_PAD), F32)],
        compiler_params=_params(),
    )(cq, wq, wukp, cos_k, sin_k, dqcat)


def _flash_fwd(qcat, kcat, tq, tk):
    n_h, s, _ = qcat.shape
    n_k = s // tk

    def body(q_ref, k_ref, v_ref, o_ref, lse_ref):
        q = q_ref[...]
        m = jnp.full((tq, 1), -1e30, F32)
        l = jnp.zeros((tq, 1), F32)
        acc = jnp.zeros((tq, KV_LORA), F32)
        for kk in range(n_k):
            rows = pl.ds(kk * tk, tk)
            sc = _dot(q, k_ref[rows, :], NT)
            m_new = jnp.maximum(m, jnp.max(sc, axis=1, keepdims=True))
            alpha = jnp.exp2(m - m_new)
            p = jnp.exp2(sc - m_new)
            l = alpha * l + jnp.sum(p, axis=1, keepdims=True)
            acc = alpha * acc + _dot(p.astype(BF16), v_ref[rows, :], NN)
            m = m_new
        o_ref[...] = (acc / l).astype(o_ref.dtype)
        lse_ref[...] = m + jnp.log2(l)

    return pl.pallas_call(
        body, name="mla_attn", grid=(n_h, s // tq),
        in_specs=[pl.BlockSpec((None, tq, QK_PAD), lambda h, i: (h, i, 0)), pl.BlockSpec((s, QK_PAD), lambda h, i: (0, 0)),
                  pl.BlockSpec((s, KV_LORA), lambda h, i: (0, 0))],
        out_specs=[pl.BlockSpec((None, tq, KV_LORA), lambda h, i: (h, i, 0)), pl.BlockSpec((None, tq, 1), lambda h, i: (h, i, 0))],
        out_shape=[_sds((n_h, s, KV_LORA), BF16), _sds((n_h, s, 1), F32)], compiler_params=_params(),
    )(qcat, kcat, kcat)


def _flash_bwd(qcat, kcat, o, do, lse, tq, tk):
    n_h, s, _ = qcat.shape
    n_k = s // tk

    def body(q_ref, k_ref, v_ref, o_ref, do_ref, lse_ref, dq_ref, dk_ref, dv_ref, dq_acc):
        h, i = pl.program_id(0), pl.program_id(1)

        @pl.when(jnp.logical_and(h == 0, i == 0))
        def _():
            dk_ref[...] = jnp.zeros_like(dk_ref)
            dv_ref[...] = jnp.zeros_like(dv_ref)

        q = q_ref[...]
        dov = do_ref[...]
        lse_v = lse_ref[...]
        delta = jnp.sum(dov.astype(F32) * o_ref[...].astype(F32), axis=1, keepdims=True)
        dq_acc[...] = jnp.zeros_like(dq_acc)

        for kk in range(n_k):
            rows = pl.ds(kk * tk, tk)
            k = k_ref[rows, :]
            p = jnp.exp2(_dot(q, k, NT) - lse_v)
            dp = _dot(dov, v_ref[rows, :], NT)
            ds = (p * (dp - delta)).astype(BF16)
            dq_acc[...] += _dot(ds, k, NN)
            dv_ref[rows, :] += _dot(p.astype(BF16), dov, TN)
            dk_ref[rows, :] += _dot(ds, q, TN)
        dq_ref[...] = dq_acc[...].astype(dq_ref.dtype)

    qblk = pl.BlockSpec((None, tq, QK_PAD), lambda h, i: (h, i, 0))
    oblk = pl.BlockSpec((None, tq, KV_LORA), lambda h, i: (h, i, 0))
    return pl.pallas_call(
        body, name="mla_attn_bwd", grid=(n_h, s // tq),
        in_specs=[qblk, pl.BlockSpec((s, QK_PAD), lambda h, i: (0, 0)), pl.BlockSpec((s, KV_LORA), lambda h, i: (0, 0)), oblk, oblk,
                  pl.BlockSpec((None, tq, 1), lambda h, i: (h, i, 0))],
        out_specs=[qblk, pl.BlockSpec((s, QK_PAD), lambda h, i: (0, 0)), pl.BlockSpec((s, KV_LORA), lambda h, i: (0, 0))],
        out_shape=[_sds((n_h, s, QK_PAD), BF16), _sds((s, QK_PAD), F32), _sds((s, KV_LORA), F32)],
        scratch_shapes=[pltpu.VMEM((tq, QK_PAD), F32)], compiler_params=_params(),
    )(qcat, kcat, kcat, o, do, lse)


def _mla_fwd(h, wts, cos_k, sin_k, tm):
    s, d = h.shape
    nt = s // tm
    lat = _matmul("mla_lat", h, wts["w_in"], a_blk=(tm, d), a_map=lambda i, k: (i, 0), b_blk=(d, LAT_PAD), b_map=lambda i, k: (0, 0),
                  o_shape=(s, LAT_PAD), o_blk=(tm, LAT_PAD), o_map=lambda i, k: (i, 0), grid=(nt, 1), contract=NN, out_dtype=F32)
    cq, kcat = _mla_mid(lat, wts["q_norm"], wts["kv_norm"], cos_k, sin_k, tm)
    qcat = _mla_q(cq, wts["wq"], wts["wukp"], cos_k, sin_k, tm)
    o_lat, lse = _flash_fwd(qcat, kcat, tm, tm)
    o = _matmul("mla_uv", o_lat, wts["wuv2"], a_blk=(None, tm, KV_LORA), a_map=lambda i, p, r: (2 * p + r, i, 0),
                b_blk=(None, KV_LORA, 2 * V_HEAD), b_map=lambda i, p, r: (2 * p + r, 0, 0), o_shape=(s, d), o_blk=(tm, 2 * V_HEAD),
                o_map=lambda i, p, r: (i, p), grid=(nt, N_HEADS // 2, 2), contract=NN, out_dtype=BF16)
    u = _matmul("mla_out", o, wts["w_o"], a_blk=(tm, d), a_map=lambda i, k: (i, 0), b_blk=(d, d), b_map=lambda i, k: (0, 0),
                o_shape=(s, d), o_blk=(tm, d), o_map=lambda i, k: (i, 0), grid=(nt, 1), contract=NN, out_dtype=F32)
    return u, (lat, cq, kcat, qcat, o_lat, lse, o)


def _mla_bwd(du, h, saved, wts, cos_k, sin_k, tm):
    lat, cq, kcat, qcat, o_lat, lse, o = saved
    s, d = h.shape
    nt = s // tm
    do = _matmul("mla_do", du, wts["w_o"], a_blk=(tm, d), a_map=lambda i, k: (i, 0), b_blk=(d, d), b_map=lambda i, k: (0, 0),
                 o_shape=(s, d), o_blk=(tm, d), o_map=lambda i, k: (i, 0), grid=(nt, 1), contract=NT, out_dtype=BF16)
    dw_o = _matmul("mla_dwo", o, du, a_blk=(tm, d), a_map=lambda k: (k, 0), b_blk=(tm, d), b_map=lambda k: (k, 0),
                   o_shape=(d, d), o_blk=(d, d), o_map=lambda k: (0, 0), grid=(nt,), contract=TN, out_dtype=F32)
    do_lat = _matmul("mla_dolat", do, wts["wuv2"], a_blk=(tm, 2 * V_HEAD), a_map=lambda hh, i, k: (i, hh // 2),
                     b_blk=(None, KV_LORA, 2 * V_HEAD), b_map=lambda hh, i, k: (hh, 0, 0), o_shape=(N_HEADS, s, KV_LORA),
                     o_blk=(None, tm, KV_LORA), o_map=lambda hh, i, k: (hh, i, 0), grid=(N_HEADS, nt, 1), contract=NT, out_dtype=BF16)
    dwuv2 = _matmul("mla_dwuv", o_lat, do, a_blk=(None, tm, KV_LORA), a_map=lambda hh, k: (hh, k, 0), b_blk=(tm, 2 * V_HEAD),
                    b_map=lambda hh, k: (k, hh // 2), o_shape=(N_HEADS, KV_LORA, 2 * V_HEAD), o_blk=(None, KV_LORA, 2 * V_HEAD),
                    o_map=lambda hh, k: (hh, 0, 0), grid=(N_HEADS, nt), contract=TN, out_dtype=F32)
    dqcat, dkcat, dv = _flash_bwd(qcat, kcat, o_lat, do_lat, lse, tm, tm)
    dcq, dwq, dwukp = _mla_q_bwd(cq, wts["wq"], wts["wukp"], cos_k, sin_k, dqcat, tm)
    dlat, (dqn, dkn) = _mla_mid_bwd(lat, dcq, dkcat, dv, wts["q_norm"], wts["kv_norm"], cos_k, sin_k, tm)
    dh = _matmul("mla_dh", dlat, wts["w_in"], a_blk=(tm, LAT_PAD), a_map=lambda i, k: (i, 0), b_blk=(d, LAT_PAD),
                 b_map=lambda i, k: (0, 0), o_shape=(s, d), o_blk=(tm, d), o_map=lambda i, k: (i, 0), grid=(nt, 1), contract=NT,
                 out_dtype=F32)
    dw_in = _matmul("mla_dwin", h, dlat, a_blk=(tm, d), a_map=lambda k: (k, 0), b_blk=(tm, LAT_PAD), b_map=lambda k: (k, 0),
                    o_shape=(d, LAT_PAD), o_blk=(d, LAT_PAD), o_map=lambda k: (0, 0), grid=(nt,), contract=TN, out_dtype=F32)
    return dh, dict(w_in=dw_in, wq=dwq, wukp=dwukp, wuv2=dwuv2, w_o=dw_o, q_norm=dqn, kv_norm=dkn)


def _adamw(name, parts, w, m, v):
    n_parts, r, c = parts.shape
    tr = r
    for cand in (256, 128, 64, 32, 16, 8):
        if r > cand and r % cand == 0:
            tr = cand
            break

    def body(p_ref, w_ref, m_ref, v_ref, g_ref, d_ref, nm_ref, nv_ref):
        g = p_ref[0].astype(F32)
        for k in range(1, n_parts):
            g = g + p_ref[k].astype(F32)
        nm = ADAM_B1 * m_ref[...] + (1.0 - ADAM_B1) * g
        nv = ADAM_B2 * v_ref[...] + (1.0 - ADAM_B2) * (g * g)
        m_hat = nm / (1.0 - ADAM_B1 ** ADAM_STEP)
        v_hat = nv / (1.0 - ADAM_B2 ** ADAM_STEP)
        g_ref[...] = g
        d_ref[...] = -ADAM_LR * (m_hat / (jnp.sqrt(v_hat) + ADAM_EPS) + ADAM_WD * w_ref[...])
        nm_ref[...] = nm
        nv_ref[...] = nv

    blk = pl.BlockSpec((tr, c), lambda i: (i, 0))
    return pl.pallas_call(
        body, name=name, grid=(r // tr,), in_specs=[pl.BlockSpec((n_parts, tr, c), lambda i: (0, i, 0)), blk, blk, blk],
        out_specs=[blk] * 4, out_shape=[_sds((r, c), F32)] * 4, compiler_params=_params(),
    )(parts, w, m, v)


def _adamw_slab(name, parts, w, m, v, bufs, f):
    n_parts, r, c = parts.shape
    tr = max(t for t in range(8, 257, 8) if r % t == 0)

    def body(p_ref, w_ref, m_ref, v_ref, *rest):
        g_ref, d_ref, nm_ref, nv_ref = rest[4:]
        g = p_ref[0].astype(F32)
        for k in range(1, n_parts):
            g = g + p_ref[k].astype(F32)
        nm = ADAM_B1 * m_ref[...] + (1.0 - ADAM_B1) * g
        nv = ADAM_B2 * v_ref[...] + (1.0 - ADAM_B2) * (g * g)
        m_hat = nm / (1.0 - ADAM_B1 ** ADAM_STEP)
        v_hat = nv / (1.0 - ADAM_B2 ** ADAM_STEP)
        g_ref[...] = g
        d_ref[...] = -ADAM_LR * (m_hat / (jnp.sqrt(v_hat) + ADAM_EPS) + ADAM_WD * w_ref[...])
        nm_ref[...] = nm
        nv_ref[...] = nv

    blk = pl.BlockSpec((None, tr, c), lambda i: (f, i, 0))
    return pl.pallas_call(
        body, name=name, grid=(r // tr,),
        in_specs=[pl.BlockSpec((n_parts, tr, c), lambda i: (0, i, 0)), blk, blk, blk] + [pl.BlockSpec(memory_space=pl.ANY)] * 4,
        out_specs=[blk] * 4, out_shape=[_sds(w.shape, F32)] * 4, input_output_aliases={4 + j: j for j in range(4)},
        compiler_params=_params(),
    )(parts, w, m, v, *bufs)


def _mesh_pos():
    return lax.axis_index("x"), lax.axis_index("y"), lax.axis_index("c")


def _flip(pos, mask):
    return tuple(1 - p if (mask >> (2 - b)) & 1 else p for b, p in enumerate(pos))


def _index(pos):
    return 4 * pos[0] + 2 * pos[1] + pos[2]


def _all_gather(name, xs, after=None):
    n = len(xs)
    extra = [] if after is None else [after]

    def body(*refs):
        x_refs, o_refs = refs[:n], refs[n + len(extra):2 * n + len(extra)]
        send_sems, recv_sems, local_sems = refs[2 * n + len(extra):]
        me = _mesh_pos()
        sibling = _flip(me, 1)
        others = [_flip(me, 4), _flip(me, 2), _flip(me, 6)]

        def copy(k, j, block, to, src=None):
            dst = o_refs[k].at[_index(block)]
            return pltpu.make_async_remote_copy(
                src_ref=dst if src is None else src, dst_ref=dst, send_sem=send_sems.at[k, j], recv_sem=recv_sems.at[k, j],
                device_id=to, device_id_type=MESH)

        local = [pltpu.make_async_copy(x_refs[k], o_refs[k].at[_index(me)], local_sems.at[k]) for k in range(n)]
        for cp in local:
            cp.start()
        first = []
        for k in range(n):
            first.append(copy(k, 0, me, sibling, src=x_refs[k]))
            first += [copy(k, 1 + j, me, other, src=x_refs[k]) for j, other in enumerate(others)]
        for cp in first:
            cp.start()
        passed = []
        for j, other in enumerate(others):
            for k in range(n):
                copy(k, 1 + j, other, me).wait_recv()
                cp = copy(k, 4 + j, other, sibling)
                cp.start()
                passed.append(cp)
        for k in range(n):
            copy(k, 0, sibling, me).wait_recv()
        for j, other in enumerate(others):
            for k in range(n):
                copy(k, 4 + j, _flip(other, 1), me).wait_recv()
        for cp in first + passed:
            cp.wait_send()
        for cp in local:
            cp.wait()

    any_spec = pl.BlockSpec(memory_space=pl.ANY)
    return pl.pallas_call(
        body, name=name, in_specs=[any_spec] * (n + len(extra)), out_specs=[any_spec] * n,
        out_shape=[_sds((N_DEV,) + x.shape, x.dtype) for x in xs],
        scratch_shapes=[pltpu.SemaphoreType.DMA((n, 7)), pltpu.SemaphoreType.DMA((n, 7)), pltpu.SemaphoreType.DMA((n,))],
    )(*xs, *extra)


def _all_to_all(name, groups):
    flat = [(gi, f) for gi, grp in enumerate(groups) for f in range(len(grp))]
    n = len(flat)
    n_groups = len(groups)

    def body(*refs):
        x_refs, o_refs = refs[:n], refs[n:n + n_groups]
        send_sems, recv_sems, local_sems = refs[n + n_groups:]
        me = _mesh_pos()
        local, sends, recvs = [], [], []
        for k, (gi, f) in enumerate(flat):
            local.append(pltpu.make_async_copy(x_refs[k].at[_index(me)], o_refs[gi].at[_index(me), f], local_sems.at[k]))
            for mask in range(1, N_DEV):
                peer = _flip(me, mask)
                sends.append(pltpu.make_async_remote_copy(
                    src_ref=x_refs[k].at[_index(peer)], dst_ref=o_refs[gi].at[_index(me), f], send_sem=send_sems.at[k, mask - 1],
                    recv_sem=recv_sems.at[k, mask - 1], device_id=peer, device_id_type=MESH))
                recvs.append(pltpu.make_async_remote_copy(
                    src_ref=x_refs[k].at[_index(me)], dst_ref=o_refs[gi].at[_index(peer), f], send_sem=send_sems.at[k, mask - 1],
                    recv_sem=recv_sems.at[k, mask - 1], device_id=peer, device_id_type=MESH))
        for cp in local + sends:
            cp.start()
        for cp in recvs:
            cp.wait_recv()
        for cp in sends:
            cp.wait_send()
        for cp in local:
            cp.wait()

    any_spec = pl.BlockSpec(memory_space=pl.ANY)
    return pl.pallas_call(
        body, name=name, in_specs=[any_spec] * n, out_specs=[any_spec] * n_groups,
        out_shape=[_sds((N_DEV, len(grp)) + grp[0].shape[1:], grp[0].dtype) for grp in groups],
        scratch_shapes=[pltpu.SemaphoreType.DMA((n, 7)), pltpu.SemaphoreType.DMA((n, 7)), pltpu.SemaphoreType.DMA((n,))],
    )(*[a for grp in groups for a in grp])


def _split_copies(kind, outgoing, x_refs, land_refs, send_sems, recv_sems):
    me = _mesh_pos()
    copies = []
    for k, (x_ref, land_ref) in enumerate(zip(x_refs, land_refs)):
        for mask in range(1, N_DEV):
            peer = _flip(me, mask)
            sem = k * (N_DEV - 1) + mask - 1
            copies.append(pltpu.make_async_remote_copy(
                src_ref=x_ref if kind == "gather" else x_ref.at[_index(peer)], dst_ref=land_ref.at[_index(me if outgoing else peer)],
                send_sem=send_sems.at[sem], recv_sem=recv_sems.at[sem], device_id=peer, device_id_type=MESH))
    return copies


_HBM_SPEC = pl.BlockSpec(memory_space=pltpu.HBM)
_SEM_SPEC = pl.BlockSpec(memory_space=pltpu.SEMAPHORE)
_EFFECT = pltpu.SideEffectType.DATAFLOW_SIDE_EFFECTING


def _split_start(name, kind, xs, after=None):
    n = len(xs)
    extra = [] if after is None else [after]
    lands = [lax.empty(((N_DEV,) + x.shape) if kind == "gather" else x.shape, x.dtype) for x in xs]

    def body(*refs):
        x_refs, land_refs = refs[:n], refs[n:2 * n]
        send_sems, recv_sems = refs[2 * n + len(extra)], refs[2 * n + len(extra) + 1]
        token = refs[-1]
        for cp in _split_copies(kind, True, x_refs, land_refs, send_sems, recv_sems):
            cp.start()
        token[...] = jnp.zeros_like(token)

    hbm = [pltpu.HBM(a.shape, a.dtype) for a in list(xs) + lands]
    res = pl.pallas_call(
        body, name=name,
        out_shape=[pltpu.SemaphoreType.DMA((n * (N_DEV - 1),)), pltpu.SemaphoreType.DMA((n * (N_DEV - 1),))] + hbm + [_sds((8, 128), F32)],
        in_specs=[_HBM_SPEC] * (2 * n) + [pl.BlockSpec(memory_space=pl.ANY)] * len(extra),
        out_specs=[_SEM_SPEC, _SEM_SPEC] + [_HBM_SPEC] * (2 * n) + [pl.BlockSpec(memory_space=pltpu.VMEM)],
        input_output_aliases={j: 2 + j for j in range(2 * n)}, compiler_params=pltpu.CompilerParams(has_side_effects=_EFFECT),
    )(*[pltpu.with_memory_space_constraint(a, pltpu.HBM) for a in list(xs) + lands], *extra)
    return (kind, n, res[0], res[1], res[2:2 + 2 * n]), res[-1]


def _split_wait(name, state, after):
    kind, n, send_sems_in, recv_sems_in, thru = state

    def body(*refs):
        x_refs, land_refs = refs[:n], refs[n:2 * n]
        send_sems, recv_sems = refs[2 * n], refs[2 * n + 1]
        for cp in _split_copies(kind, True, x_refs, land_refs, send_sems, recv_sems):
            cp.wait_send()
        for cp in _split_copies(kind, False, x_refs, land_refs, send_sems, recv_sems):
            cp.wait_recv()

    res = pl.pallas_call(
        body, name=name, out_shape=[pltpu.HBM(a.shape, a.dtype) for a in thru],
        in_specs=[_HBM_SPEC] * (2 * n) + [_SEM_SPEC, _SEM_SPEC, pl.BlockSpec(memory_space=pl.ANY)], out_specs=[_HBM_SPEC] * (2 * n),
        input_output_aliases={j: j for j in range(2 * n)}, compiler_params=pltpu.CompilerParams(has_side_effects=_EFFECT),
    )(*thru, send_sems_in, recv_sems_in, after)
    return res[n:]


def _with_own(land, own):
    me = _index(_mesh_pos())
    return lax.dynamic_update_slice(land, own[None], (me,) + (0,) * own.ndim)


def _rope_tables(s):
    inv = 1.0 / (ROPE_THETA ** (jnp.arange(0, QK_ROPE, 2, dtype=F32) / QK_ROPE))
    ang = jnp.arange(s, dtype=F32)[:, None] * inv[None, :]
    pad = jnp.zeros((s, KV_LORA - QK_ROPE), F32)
    return (jnp.concatenate([jnp.cos(ang), jnp.cos(ang), pad], axis=1), jnp.concatenate([jnp.sin(ang), jnp.sin(ang), pad], axis=1))


def _row_of(v):
    return v.reshape(1, -1)


def kernel(x, c, ada_w, ada_b, norm_g, ffn_w_in, ffn_w_out, pool_w, pool_b, pool_scale, mla_w_in, mla_q_norm, mla_kv_norm, mla_w_uq, mla_w_uk, mla_w_uv, mla_w_o, loss_target, m_ada_w, m_ada_b, m_norm_g, m_ffn_w_in, m_ffn_w_out, m_pool_w, m_pool_b, m_pool_scale, m_mla_w_in, m_mla_q_norm, m_mla_kv_norm, m_mla_w_uq, m_mla_w_uk, m_mla_w_uv, m_mla_w_o, v_ada_w, v_ada_b, v_norm_g, v_ffn_w_in, v_ffn_w_out, v_pool_w, v_pool_b, v_pool_scale, v_mla_w_in, v_mla_q_norm, v_mla_kv_norm, v_mla_w_uq, v_mla_w_uk, v_mla_w_uv, v_mla_w_o):
    s, d = x.shape[1], x.shape[2]
    tm = min(512, s)
    tr = min(256, s)
    tf = min(1024, s)
    me = 4 * lax.axis_index("x") + 2 * lax.axis_index("y") + lax.axis_index("c")
    x0 = x.reshape(s, d)
    target = loss_target.reshape(s, d)
    n_mod = ada_w.shape[2] * N_DEV // d
    mod_blk = ada_w.shape[2]

    small = jnp.concatenate([c.reshape(-1), norm_g.reshape(-1), pool_b.reshape(-1), mla_q_norm.reshape(-1)]).reshape(1, -1)
    w_in_loc = [ffn_w_in[i, f].astype(BF16) for i in range(2) for f in range(2)]
    w_out_loc = [ffn_w_out[i, f].astype(BF16) for i in range(2) for f in range(2)]
    (small_all,) = _all_gather("gather_small", [small])
    small_all = small_all.reshape(N_DEV, -1)
    c_all = small_all[:, :d]
    off = d
    g_all = small_all[:, off:off + 12 * (d // N_DEV)].reshape(N_DEV, 2, 6, d // N_DEV).transpose(1, 2, 0, 3).reshape(2, 6, d)
    off += 12 * (d // N_DEV)
    pool_b_all = small_all[:, off:off + 4 * 32].reshape(N_DEV, 4, 32).transpose(1, 0, 2).reshape(1, d)
    off += 4 * 32
    q_norm_all = small_all[:, off:off + 32].reshape(1, Q_LORA)
    kv_norm_row = mla_kv_norm.reshape(1, KV_LORA)
    pscale_row = pool_scale.reshape(1, d)

    even = (jnp.arange(N_HEADS) % 2 == 0)[:, None, None]
    cos_k, sin_k = _rope_tables(s)

    def mla_weights(mla_w_in_all, mla_w_uq_all, mla_w_o_all):
        uq = mla_w_uq_all.reshape(Q_LORA, N_HEADS, QK_NOPE + QK_ROPE).transpose(1, 0, 2)
        zq = jnp.zeros((N_HEADS, Q_LORA, QK_NOPE), BF16)
        wq = jnp.concatenate(
            [uq[:, :, :QK_NOPE], zq, uq[:, :, QK_NOPE:], jnp.zeros((N_HEADS, Q_LORA, QK_PAD - KV_LORA - QK_ROPE), BF16)], axis=2)
        wukp = jnp.pad(mla_w_uk[0].transpose(1, 2, 0).astype(BF16), ((0, 0), (0, QK_PAD - QK_NOPE), (0, QK_PAD - KV_LORA)))
        uv = mla_w_uv[0].transpose(1, 0, 2).astype(BF16)
        wuv2 = jnp.where(even, jnp.concatenate([uv, jnp.zeros_like(uv)], axis=2), jnp.concatenate([jnp.zeros_like(uv), uv], axis=2))
        return dict(w_in=jnp.pad(mla_w_in_all.reshape(d, -1), ((0, 0), (0, LAT_PAD - mla_w_in.shape[2]))), wq=wq, wukp=wukp,
                    wuv2=wuv2, w_o=mla_w_o_all.reshape(d, d), q_norm=q_norm_all, kv_norm=kv_norm_row)

    (sc_all,), _ = _rowmap("ada_silu", lambda cv: ((cv * jax.nn.sigmoid(cv),), ()), [(c_all, (N_DEV, d), lambda i: (0, 0))],
                           [(_sds((N_DEV, d), F32), (N_DEV, d), lambda i: (0, 0))], [], (1,))
    ada_b_loc = lax.dynamic_slice_in_dim(ada_b, me * mod_blk, mod_blk, axis=1).reshape(2, 1, mod_blk)
    m_pad = 2 * N_DEV
    modp = _matmul("ada_mod", jnp.pad(sc_all, ((0, m_pad - N_DEV), (0, 0))), ada_w, a_blk=(m_pad, d), a_map=lambda i, k: (0, 0),
                   b_blk=(None, d, mod_blk), b_map=lambda i, k: (i, 0, 0), o_shape=(2, m_pad, mod_blk), o_blk=(None, m_pad, mod_blk),
                   o_map=lambda i, k: (i, 0, 0), grid=(2, 1), contract=NN, out_dtype=F32, bias=ada_b_loc, bias_blk=(None, 1, mod_blk),
                   bias_map=lambda i, k: (i, 0, 0))[:, :N_DEV]
    modp_all, w_in_first, w_out_first = _all_gather("gather_first", [modp.reshape(2 * N_DEV, mod_blk), w_in_loc[0], w_out_loc[0]])
    later_a = [w_in_loc[1], w_out_loc[1], pool_w.reshape(-1, POOL_GROUP).astype(BF16)]
    later_b = [w_in_loc[2], w_out_loc[2], w_in_loc[3], w_out_loc[3], mla_w_in[0].astype(BF16),
               mla_w_uq.reshape(mla_w_uq.shape[1], -1).astype(BF16), mla_w_o[0].astype(BF16)]
    state_a, token_a = _split_start("gather_start_a", "gather", later_a, after=modp_all)
    state_b, token_b = _split_start("gather_start_b", "gather", later_b, after=token_a)
    w_in8 = [w_in_first, None, None, None]
    w_out4 = [w_out_first.reshape(4, FF_BLK, d), None, None, None]
    mod = lax.dynamic_index_in_dim(modp_all.reshape(N_DEV, 2, N_DEV, mod_blk), me, axis=2, keepdims=False)
    mod = mod.transpose(1, 0, 2).reshape(2, n_mod, d) + (token_a[0, 0] + token_b[0, 0])

    saved = []
    xs = x0
    w4 = mla_wts = None
    for i in range(2):
        for sub in range(3):
            if (i, sub) == (0, 1):
                lands = _split_wait("gather_wait_a", state_a, xs)
                w_in8[1] = _with_own(lands[0], later_a[0])
                w_out4[1] = _with_own(lands[1], later_a[1]).reshape(4, FF_BLK, d)
                w4 = _with_own(lands[2], later_a[2]).reshape(N_DEV, 4, 32, POOL_GROUP).transpose(1, 0, 2, 3)
                w4 = w4.reshape(4, POOL_GROUP, POOL_GROUP)
            if (i, sub) == (1, 0):
                lands = [_with_own(land, own) for land, own in zip(_split_wait("gather_wait_b", state_b, xs), later_b)]
                w_in8[2], w_in8[3] = lands[0], lands[2]
                w_out4[2], w_out4[3] = lands[1].reshape(4, FF_BLK, d), lands[3].reshape(4, FF_BLK, d)
                mla_wts = mla_weights(*lands[4:])
            shift, scale, gate = (_row_of(mod[i, 3 * sub + j]) for j in range(3))
            g_pre, g_post = _row_of(g_all[i, 2 * sub]), _row_of(g_all[i, 2 * sub + 1])
            tag = f"{i}{sub}"
            if sub != 1:
                f = sub // 2
                h = _prenorm(f"prenorm_{tag}", xs, g_pre, scale, shift, BF16, tr)
                u, extra = _ffn_fwd(tag, h, w_in8[2 * i + f], w_out4[2 * i + f], tf, tf // 2)
                weight = 0.5
            elif i == 0:
                h = _prenorm(f"prenorm_{tag}", xs, g_pre, scale, shift, F32, tr)
                u, z, v = _pool_fwd(h, w4, pool_b_all, pscale_row, tm)
                extra = (z, v)
                weight = 1.0
            else:
                h = _prenorm(f"prenorm_{tag}", xs, g_pre, scale, shift, BF16, tr)
                u, extra = _mla_fwd(h, mla_wts, cos_k, sin_k, tm)
                weight = 1.0
            saved.append((xs, h, u, extra, (shift, scale, gate, g_pre, g_post), weight))
            xs = _postnorm(f"postnorm_{tag}", xs, u, g_post, gate, weight, tr)

    def loss_fn(yv, tv):
        e = yv - tv
        return (e * (1.0 / d),), (_colsum(e * e),)

    (dx,), (sq,) = _rowmap("loss", loss_fn, [_tile(xs, tr), _tile(target, tr)], [_otile(s, d, F32, tr)], [_ored(d)], (s // tr,))
    loss = lax.psum(0.5 * jnp.sum(sq) / d, AXES)

    d_mod = [[None] * n_mod for _ in range(2)]
    d_g = [[None] * 6 for _ in range(2)]
    sent = {}
    pool_grads = mla_grads = None

    def start_scatter(key, arrays):
        state, token = _split_start(f"scatter_start_{key}", "scatter", arrays)
        sent[key] = (state, arrays)
        return token

    for i in (1, 0):
        for sub in (2, 1, 0):
            xin, h, u, extra, (shift, scale, gate, g_pre, g_post), weight = saved[3 * i + sub]
            tag = f"{i}{sub}"
            du, (dgate, dgpost) = _postnorm_bwd(f"postnorm_bwd_{tag}", dx, u, g_post, gate, weight, F32 if (sub == 1 and i == 0) else BF16, tr)
            if sub != 1:
                k = 2 * i + sub // 2
                dh, dgu, act = _ffn_bwd_act(tag, du, extra, w_in8[k], w_out4[k], tf, tf // 2)
                dw_out = _ffn_dw_out(tag, act, du, tm).reshape(N_DEV, FF_BLK // 2, d)
                if k == 0:
                    d_pool_w = pool_grads[0].reshape(4, N_DEV, 32, POOL_GROUP).transpose(1, 0, 2, 3).reshape(N_DEV, 4 * 32, POOL_GROUP)
                    token = start_scatter(tag + "_out", [dw_out, d_pool_w])
                    token = start_scatter(tag + "_in", [_ffn_dw_in(tag, h, dgu, tm, after=token)])
                else:
                    token = start_scatter(tag, [_ffn_dw_in(tag, h, dgu, tm), dw_out])
                g_pre = g_pre + token[0, 0]
            elif i == 0:
                dh, dw4, (dpscale, dpb) = _pool_bwd(du, extra[0], extra[1], w4, pscale_row, tm)
                pool_grads = (dw4, dpscale, dpb)
            else:
                dh, mla_grads = _mla_bwd(du, h, extra, mla_wts, cos_k, sin_k, tm)
                dwq = mla_grads["wq"]
                d_uq = jnp.concatenate([dwq[:, :, :QK_NOPE], dwq[:, :, KV_LORA:KV_LORA + QK_ROPE]], axis=2).transpose(1, 0, 2)
                token = start_scatter("mla", [mla_grads["w_in"][:, :mla_w_in.shape[2]].reshape(N_DEV, d // N_DEV, -1),
                                              d_uq.reshape(N_DEV, Q_LORA // N_DEV, -1), mla_grads["w_o"].reshape(N_DEV, d // N_DEV, d)])
                dwukp, dwuv2 = mla_grads["wukp"], mla_grads["wuv2"]
                d_uk = dwukp[:, :QK_NOPE, :KV_LORA].transpose(2, 0, 1).reshape(KV_LORA, -1)
                d_uv = jnp.where(even, dwuv2[:, :, :V_HEAD], dwuv2[:, :, V_HEAD:]).transpose(1, 0, 2).reshape(KV_LORA, -1)
                state_ukv, token_ukv = _split_start("gather_start_ukv", "gather", [d_uk, d_uv], after=token)
                g_pre = g_pre + token_ukv[0, 0]
            dx, (dshift, dscale, dgpre) = _prenorm_bwd(f"prenorm_bwd_{tag}", dh, xin, dx, g_pre, scale, tr)
            d_mod[i][3 * sub:3 * sub + 3] = [dshift, dscale, dgate]
            d_g[i][2 * sub:2 * sub + 2] = [dgpre, dgpost]
    grad_x = dx.reshape(x.shape)

    def upd(name, parts, w, m, v):
        shape = w.shape
        r, cdim = parts.shape[1], parts.shape[2]
        return [o.reshape(shape) for o in _adamw(name, parts, w.reshape(r, cdim), m.reshape(r, cdim), v.reshape(r, cdim))]

    def landed(key, after):
        state, arrays = sent[key]
        lands = _split_wait(f"scatter_wait_{key}", state, after)
        return [_with_own(land, lax.dynamic_index_in_dim(a, me, 0, keepdims=False)) for land, a in zip(lands, arrays)]

    res = {}
    w_in_s, m_in_s, v_in_s = (a.reshape(4, d, FF_BLK) for a in (ffn_w_in, m_ffn_w_in, v_ffn_w_in))
    w_out_s, m_out_s, v_out_s = (a.reshape(4, FF_BLK // 2, d) for a in (ffn_w_out, m_ffn_w_out, v_ffn_w_out))
    bufs_in = [lax.empty(w_in_s.shape, F32) for _ in range(4)]
    bufs_out = [lax.empty(w_out_s.shape, F32) for _ in range(4)]
    for key, k in (("12", 3), ("mla", None), ("10", 2), ("02", 1)):
        parts = landed(key, grad_x)
        if k is None:
            res["mla_w_in"] = upd("adam_mla_w_in", parts[0], mla_w_in, m_mla_w_in, v_mla_w_in)
            res["mla_w_uq"] = upd("adam_mla_w_uq", parts[1], mla_w_uq, m_mla_w_uq, v_mla_w_uq)
            res["mla_w_o"] = upd("adam_mla_w_o", parts[2], mla_w_o, m_mla_w_o, v_mla_w_o)
            uk_all, uv_all = (_with_own(land, own) for land, own in zip(_split_wait("gather_wait_ukv", state_ukv, grad_x), (d_uk, d_uv)))
            res["mla_w_uk"] = upd("adam_mla_w_uk", uk_all, mla_w_uk, m_mla_w_uk, v_mla_w_uk)
            res["mla_w_uv"] = upd("adam_mla_w_uv", uv_all, mla_w_uv, m_mla_w_uv, v_mla_w_uv)
            continue
        bufs_in = _adamw_slab(f"adam_ffn_w_in_{key}", parts[0], w_in_s, m_in_s, v_in_s, bufs_in, k)
        bufs_out = _adamw_slab(f"adam_ffn_w_out_{key}", parts[1], w_out_s, m_out_s, v_out_s, bufs_out, k)

    dw4, dpscale, dpb = pool_grads
    d_mod_row = jnp.concatenate([jnp.concatenate(r, axis=1) for r in d_mod], axis=1)
    d_g_row = jnp.concatenate([jnp.concatenate(r, axis=1) for r in d_g], axis=1)
    small_g = jnp.concatenate([d_mod_row, d_g_row, dpb, dpscale, mla_grads["q_norm"], mla_grads["kv_norm"]], axis=1)
    (small_g_all,) = _all_gather("gather_small_grads", [small_g], after=bufs_out[0])
    small_g_all = small_g_all.reshape(N_DEV, -1)
    n_m = 2 * n_mod * d
    d_mod_all = small_g_all[:, :n_m].reshape(N_DEV, 2, n_mod * d)
    rest = small_g_all[:, n_m:]
    p_norm_g = lax.dynamic_slice_in_dim(rest[:, :12 * d].reshape(N_DEV, 12, d), me * (d // N_DEV), d // N_DEV, axis=2)
    p_pool_b = lax.dynamic_slice_in_dim(rest[:, 12 * d:13 * d].reshape(N_DEV, 4, POOL_GROUP), me * 32, 32, axis=2)
    p_pool_scale = rest[:, 13 * d:14 * d].reshape(N_DEV, 1, d)
    p_q_norm = lax.dynamic_slice_in_dim(rest[:, 14 * d:14 * d + Q_LORA], me * 32, 32, axis=1).reshape(N_DEV, 1, 32)
    p_kv_norm = rest[:, 14 * d + Q_LORA:].reshape(N_DEV, 1, KV_LORA)

    d_mod_loc = lax.dynamic_slice_in_dim(d_mod_all, me * mod_blk, mod_blk, axis=2).transpose(1, 0, 2)
    k_pad = 128
    sc_t = jnp.pad(sc_all.T, ((0, 0), (0, k_pad - N_DEV)))
    d_ada_w = _matmul("ada_dw", sc_t, jnp.pad(d_mod_loc, ((0, 0), (0, k_pad - N_DEV), (0, 0))), a_blk=(d, k_pad),
                      a_map=lambda i, k: (0, 0), b_blk=(None, k_pad, mod_blk), b_map=lambda i, k: (i, 0, 0), o_shape=(2, d, mod_blk),
                      o_blk=(None, d, mod_blk), o_map=lambda i, k: (i, 0, 0), grid=(2, 1), contract=NN, out_dtype=F32)
    res["ada_w"] = upd("adam_ada_w", d_ada_w.reshape(1, 2 * d, mod_blk), ada_w, m_ada_w, v_ada_w)
    res["ada_b"] = upd("adam_ada_b", d_mod_all.reshape(N_DEV, 2, n_mod * d), ada_b, m_ada_b, v_ada_b)
    res["norm_g"] = upd("adam_norm_g", p_norm_g, norm_g, m_norm_g, v_norm_g)
    res["pool_b"] = upd("adam_pool_b", p_pool_b, pool_b, m_pool_b, v_pool_b)
    res["pool_scale"] = upd("adam_pool_scale", p_pool_scale, pool_scale, m_pool_scale, v_pool_scale)
    res["mla_q_norm"] = upd("adam_mla_q_norm", p_q_norm, mla_q_norm, m_mla_q_norm, v_mla_q_norm)
    res["mla_kv_norm"] = upd("adam_mla_kv_norm", p_kv_norm, mla_kv_norm, m_mla_kv_norm, v_mla_kv_norm)

    p_out, p_pool_w = landed("00_out", res["ada_w"][1])
    bufs_out = _adamw_slab("adam_ffn_w_out_00", p_out, w_out_s, m_out_s, v_out_s, bufs_out, 0)
    res["pool_w"] = upd("adam_pool_w", p_pool_w, pool_w, m_pool_w, v_pool_w)
    (p_in,) = landed("00_in", res["pool_w"][1])
    bufs_in = _adamw_slab("adam_ffn_w_in_00", p_in, w_in_s, m_in_s, v_in_s, bufs_in, 0)
    res["ffn_w_in"] = [b.reshape(ffn_w_in.shape) for b in bufs_in]
    res["ffn_w_out"] = [b.reshape(ffn_w_out.shape) for b in bufs_out]

    order = ["ada_w", "ada_b", "norm_g", "ffn_w_in", "ffn_w_out", "pool_w", "pool_b", "pool_scale", "mla_w_in", "mla_q_norm",
             "mla_kv_norm", "mla_w_uq", "mla_w_uk", "mla_w_uv", "mla_w_o"]
    outs = [loss, grad_x]
    for j in range(4):
        outs += [res[name][j] for name in order]
    return tuple(outs)
```

```python
import functools

import jax
import jax.numpy as jnp
from jax import lax
from jax.experimental import pallas as pl
from jax.experimental.pallas import tpu as pltpu

F32 = jnp.float32
BF16 = jnp.bfloat16
N_DEV = 8
AXES = ("x", "y", "c")
MESH = pl.DeviceIdType.MESH

D_MODEL = 1024
N_HEADS = 16
QK_NOPE = 64
QK_ROPE = 32
V_HEAD = 64
Q_LORA = 256
KV_LORA = 128
LAT_PAD = 512
QK_PAD = 256
D_FF = 2816
FF_BLK = 2 * D_FF // N_DEV
POOL_WINDOWS = (2, 4, 8, 16)
POOL_GROUP = 256
ROPE_THETA = 10000.0
EPS = 1e-6
ATTN_SCALE = (QK_NOPE + QK_ROPE) ** -0.5
LOG2_E = 1.4426950408889634
ADAM_LR, ADAM_B1, ADAM_B2, ADAM_EPS, ADAM_WD, ADAM_STEP = 0.001, 0.9, 0.999, 1e-08, 0.01, 10
VMEM_LIMIT = 56 * 1024 * 1024

NN = ((1,), (0,))
NT = ((1,), (1,))
TN = ((0,), (0,))


def _params(**kw):
    return pltpu.CompilerParams(vmem_limit_bytes=VMEM_LIMIT, **kw)


def _dot(a, b, contract):
    return lax.dot_general(a, b, (contract, ((), ())), preferred_element_type=F32)


def _matmul(name, a, b, *, a_blk, a_map, b_blk, b_map, o_shape, o_blk, o_map, grid, contract, out_dtype,
            bias=None, bias_blk=None, bias_map=None, after=None):
    n_k = grid[-1]
    k_axis = len(grid) - 1
    acc_shape = tuple(d for d in o_blk if d is not None)

    def body(*refs):
        a_ref, b_ref = refs[:2]
        bias_ref = refs[2] if bias is not None else None
        if n_k == 1:
            r = _dot(a_ref[...].astype(BF16), b_ref[...].astype(BF16), contract)
            if bias is not None:
                r = r + bias_ref[...]
            refs[-1][...] = r.astype(refs[-1].dtype)
            return
        o_ref, acc = refs[-2:]
        k = pl.program_id(k_axis)

        @pl.when(k == 0)
        def _():
            acc[...] = jnp.zeros_like(acc)

        acc[...] += _dot(a_ref[...].astype(BF16), b_ref[...].astype(BF16), contract)

        @pl.when(k == n_k - 1)
        def _():
            r = acc[...]
            if bias is not None:
                r = r + bias_ref[...]
            o_ref[...] = r.astype(o_ref.dtype)

    in_specs = [pl.BlockSpec(a_blk, a_map), pl.BlockSpec(b_blk, b_map)]
    args = [a, b]
    if bias is not None:
        in_specs.append(pl.BlockSpec(bias_blk, bias_map))
        args.append(bias)
    if after is not None:
        in_specs.append(pl.BlockSpec(memory_space=pl.ANY))
        args.append(after)
    return pl.pallas_call(
        body, name=name, grid=grid, in_specs=in_specs, out_specs=pl.BlockSpec(o_blk, o_map),
        out_shape=jax.ShapeDtypeStruct(o_shape, out_dtype), scratch_shapes=[pltpu.VMEM(acc_shape, F32)] if n_k > 1 else [],
        compiler_params=_params(),
    )(*args)


def _rowmap(name, fn, ins, outs, reds, grid):
    n_in, n_out, n_red = len(ins), len(outs), len(reds)

    def body(*refs):
        in_refs = refs[:n_in]
        out_refs = refs[n_in:n_in + n_out]
        red_refs = refs[n_in + n_out:]
        out_vals, red_vals = fn(*[r[...] for r in in_refs])
        for r, v in zip(out_refs, out_vals):
            r[...] = v.astype(r.dtype)
        if n_red:
            first = pl.program_id(0) == 0
            for ax in range(1, len(grid)):
                first = jnp.logical_and(first, pl.program_id(ax) == 0)

            @pl.when(first)
            def _():
                for r in red_refs:
                    r[...] = jnp.zeros_like(r)

            for r, v in zip(red_refs, red_vals):
                r[...] += v

    res = pl.pallas_call(
        body, name=name, grid=grid,
        in_specs=[pl.BlockSpec(blk, imap) for _, blk, imap in ins],
        out_specs=[pl.BlockSpec(blk, imap) for _, blk, imap in list(outs) + list(reds)],
        out_shape=[sds for sds, _, _ in list(outs) + list(reds)],
        compiler_params=_params(),
    )(*[a for a, _, _ in ins])
    return res[:n_out], res[n_out:]


def _sds(shape, dtype):
    return jax.ShapeDtypeStruct(shape, dtype)


def _tile(a, tm):
    return (a, (tm, a.shape[1]), lambda i: (i, 0))


def _row(a):
    return (a, (1, a.shape[1]), lambda i: (0, 0))


def _otile(n, c, dtype, tm):
    return (_sds((n, c), dtype), (tm, c), lambda i: (i, 0))


def _ored(c):
    return (_sds((1, c), F32), (1, c), lambda i: (0, 0))


def _colsum(v):
    return jnp.sum(v, axis=0, keepdims=True)


def _rstd(v):
    return lax.rsqrt(jnp.mean(v * v, axis=-1, keepdims=True) + EPS)


def _prenorm(name, x, g_pre, scale, shift, out_dtype, tm):
    n, d = x.shape

    def fn(xv, g, sc, sh):
        return (xv * _rstd(xv) * g * (1.0 + sc) + sh,), ()

    (h,), _ = _rowmap(name, fn, [_tile(x, tm), _row(g_pre), _row(scale), _row(shift)], [_otile(n, d, out_dtype, tm)], [],
                      (n // tm,))
    return h


def _postnorm(name, x, u, g_post, gate, weight, tm):
    n, d = x.shape

    def fn(xv, uv, g, gt):
        return (xv + weight * (1.0 + gt) * (uv * _rstd(uv) * g),), ()

    (y,), _ = _rowmap(name, fn, [_tile(x, tm), _tile(u, tm), _row(g_post), _row(gate)], [_otile(n, d, F32, tm)], [], (n // tm,))
    return y


def _postnorm_bwd(name, dout, u, g_post, gate, weight, out_dtype, tm):
    n, d = u.shape

    def fn(dv, uv, g, gt):
        r = _rstd(uv)
        un = uv * r
        dy = dv * (weight * (1.0 + gt))
        a = dy * g
        du = r * (a - un * jnp.mean(a * un, axis=-1, keepdims=True))
        return (du,), (_colsum(dv * (weight * (un * g))), _colsum(dy * un))

    (du,), reds = _rowmap(name, fn, [_tile(dout, tm), _tile(u, tm), _row(g_post), _row(gate)], [_otile(n, d, out_dtype, tm)],
                          [_ored(d), _ored(d)], (n // tm,))
    return du, reds


def _prenorm_bwd(name, dh, x, dout, g_pre, scale, tm):
    n, d = x.shape

    def fn(dhv, xv, dv, g, sc):
        dhv = dhv.astype(F32)
        r = _rstd(xv)
        xn = xv * r
        b = dhv * (g * (1.0 + sc))
        dx = dv + r * (b - xn * jnp.mean(b * xn, axis=-1, keepdims=True))
        return (dx,), (_colsum(dhv), _colsum(dhv * (xn * g)), _colsum(dhv * ((1.0 + sc) * xn)))

    (dx,), reds = _rowmap(name, fn, [_tile(dh, tm), _tile(x, tm), _tile(dout, tm), _row(g_pre), _row(scale)],
                          [_otile(n, d, F32, tm)], [_ored(d), _ored(d), _ored(d)], (n // tm,))
    return dx, reds


def _ffn_fwd(tag, h, w_in8, w_out4, tm, sub):
    s, d = h.shape

    def body(h_ref, wg_ref, wu_ref, wo_ref, u_ref, gu_ref):
        @pl.when(pl.program_id(1) == 0)
        def _():
            u_ref[...] = jnp.zeros_like(u_ref)

        for r in range(tm // sub):
            rows = pl.ds(r * sub, sub)
            hv = h_ref[rows, :]
            gate = _dot(hv, wg_ref[...], NT)
            up = _dot(hv, wu_ref[...], NT)
            gu_ref[0, rows, :] = gate.astype(BF16)
            gu_ref[1, rows, :] = up.astype(BF16)
            u_ref[rows, :] += _dot((gate * jax.nn.sigmoid(gate) * up).astype(BF16), wo_ref[...], NN)

    w_blk = (None, FF_BLK, d)
    return pl.pallas_call(
        body, name=f"ffn_fwd_{tag}", grid=(s // tm, 4),
        in_specs=[pl.BlockSpec((tm, d), lambda i, j: (i, 0)), pl.BlockSpec(w_blk, lambda i, j: (j, 0, 0)),
                  pl.BlockSpec(w_blk, lambda i, j: (j + 4, 0, 0)), pl.BlockSpec((None, FF_BLK, d), lambda i, j: (j, 0, 0))],
        out_specs=[pl.BlockSpec((tm, d), lambda i, j: (i, 0)), pl.BlockSpec((2, None, tm, FF_BLK), lambda i, j: (0, j, i, 0))],
        out_shape=[_sds((s, d), F32), _sds((2, 4, s, FF_BLK), BF16)], compiler_params=_params(),
    )(h, w_in8, w_in8, w_out4)


def _ffn_bwd_act(tag, du, gu, w_in8, w_out4, tm, sub):
    s, d = du.shape

    def body(du_ref, gu_ref, wg_ref, wu_ref, wo_ref, dh_ref, dgu_ref, act_ref):
        @pl.when(pl.program_id(1) == 0)
        def _():
            dh_ref[...] = jnp.zeros_like(dh_ref)

        for r in range(tm // sub):
            rows = pl.ds(r * sub, sub)
            dact = _dot(du_ref[rows, :], wo_ref[...], NT)
            gate, up = gu_ref[0, rows, :].astype(F32), gu_ref[1, rows, :].astype(F32)
            sg = jax.nn.sigmoid(gate)
            silu = gate * sg
            dg = (dact * up * (sg * (1.0 + gate * (1.0 - sg)))).astype(BF16)
            dup = (dact * silu).astype(BF16)
            dgu_ref[0, rows, :] = dg
            dgu_ref[1, rows, :] = dup
            act_ref[rows, :] = (silu * up).astype(BF16)
            dh_ref[rows, :] += _dot(dg, wg_ref[...], NN) + _dot(dup, wu_ref[...], NN)

    w_blk = (None, FF_BLK, d)
    gu_blk = pl.BlockSpec((2, None, tm, FF_BLK), lambda i, j: (0, j, i, 0))
    dh, dgu, act = pl.pallas_call(
        body, name=f"ffn_bwd_{tag}", grid=(s // tm, 4),
        in_specs=[pl.BlockSpec((tm, d), lambda i, j: (i, 0)), gu_blk, pl.BlockSpec(w_blk, lambda i, j: (j, 0, 0)),
                  pl.BlockSpec(w_blk, lambda i, j: (j + 4, 0, 0)), pl.BlockSpec((None, FF_BLK, d), lambda i, j: (j, 0, 0))],
        out_specs=[pl.BlockSpec((tm, d), lambda i, j: (i, 0)), gu_blk, pl.BlockSpec((None, tm, FF_BLK), lambda i, j: (j, i, 0))],
        out_shape=[_sds((s, d), F32), _sds((2, 4, s, FF_BLK), BF16), _sds((4, s, FF_BLK), BF16)], compiler_params=_params(),
    )(du, gu, w_in8, w_in8, w_out4)
    return dh, dgu.reshape(8, s, FF_BLK), act


def _ffn_dw_in(tag, h, dgu, tm, after=None):
    s, d = h.shape
    return _matmul(f"ffn_dwin_{tag}", dgu, h, a_blk=(None, tm, FF_BLK), a_map=lambda g, k: (g, k, 0), b_blk=(tm, d),
                   b_map=lambda g, k: (k, 0), o_shape=(8, FF_BLK, d), o_blk=(None, FF_BLK, d), o_map=lambda g, k: (g, 0, 0),
                   grid=(8, s // tm), contract=TN, out_dtype=BF16, after=after)


def _ffn_dw_out(tag, act, du, tm):
    s, d = du.shape
    return _matmul(f"ffn_dwout_{tag}", act, du, a_blk=(None, tm, FF_BLK), a_map=lambda j, k: (j, k, 0), b_blk=(tm, d),
                   b_map=lambda j, k: (k, 0), o_shape=(4, FF_BLK, d), o_blk=(None, FF_BLK, d), o_map=lambda j, k: (j, 0, 0),
                   grid=(4, s // tm), contract=TN, out_dtype=BF16)


def _window_sum(x, window, transpose):
    s = x.shape[0]
    t = lax.broadcasted_iota(jnp.int32, (s, 1), 0)
    half = window // 2
    cnt = jnp.minimum(t + half, s) - jnp.maximum(t - half, 0)
    inv = 1.0 / cnt.astype(F32)
    if transpose:
        x = x * inv
        offsets = range(-half + 1, half + 1)
    else:
        offsets = range(-half, half)
    acc = jnp.zeros_like(x)
    for o in offsets:
        shifted = x if o == 0 else pltpu.roll(x, (-o) % s, 0)
        valid = jnp.logical_and(t + o >= 0, t + o < s)
        acc = acc + jnp.where(valid, shifted, 0.0)
    return acc if transpose else acc * inv


def _pool_mix(name, x, transpose, out_dtype):
    s, d = x.shape

    def body(x_ref, o_ref):
        g = pl.program_id(0)
        for gi, window in enumerate(POOL_WINDOWS):
            @pl.when(g == gi)
            def _(window=window):
                xv = x_ref[...].astype(F32)
                o_ref[...] = (_window_sum(xv, window, transpose) - xv).astype(o_ref.dtype)

    return pl.pallas_call(
        body, name=name, grid=(len(POOL_WINDOWS),), in_specs=[pl.BlockSpec((s, POOL_GROUP), lambda g: (0, g))],
        out_specs=pl.BlockSpec((s, POOL_GROUP), lambda g: (0, g)), out_shape=_sds((s, d), out_dtype), compiler_params=_params(),
    )(x)


def _pool_fwd(h, w4, bias, pscale, tm):
    s, d = h.shape
    nt = s // tm
    z = _pool_mix("pool_mix", h, False, BF16)
    v = _matmul("pool_proj", z, w4, a_blk=(tm, POOL_GROUP), a_map=lambda i, g, k: (i, g), b_blk=(None, POOL_GROUP, POOL_GROUP),
                b_map=lambda i, g, k: (g, 0, 0), o_shape=(s, d), o_blk=(tm, POOL_GROUP), o_map=lambda i, g, k: (i, g),
                grid=(nt, 4, 1), contract=NN, out_dtype=F32, bias=bias, bias_blk=(1, POOL_GROUP), bias_map=lambda i, g, k: (0, g))
    (u,), _ = _rowmap("pool_scale", lambda vv, ps: ((vv * ps,), ()), [_tile(v, tm), _row(pscale)], [_otile(s, d, F32, tm)], [],
                      (nt,))
    return u, z, v


def _pool_bwd(du, z, v, w4, pscale, tm):
    s, d = du.shape
    nt = s // tm

    def fn(duv, vv, ps):
        dv = duv * ps
        return (dv,), (_colsum(duv * vv), _colsum(dv))

    (dv,), reds = _rowmap("pool_dscale", fn, [_tile(du, tm), _tile(v, tm), _row(pscale)], [_otile(s, d, BF16, tm)],
                          [_ored(d), _ored(d)], (nt,))
    dw4 = _matmul("pool_dw", z, dv, a_blk=(tm, POOL_GROUP), a_map=lambda g, k: (k, g), b_blk=(tm, POOL_GROUP),
                  b_map=lambda g, k: (k, g), o_shape=(4, POOL_GROUP, POOL_GROUP), o_blk=(None, POOL_GROUP, POOL_GROUP),
                  o_map=lambda g, k: (g, 0, 0), grid=(4, nt), contract=TN, out_dtype=F32)
    dz = _matmul("pool_dz", dv, w4, a_blk=(tm, POOL_GROUP), a_map=lambda i, g, k: (i, g), b_blk=(None, POOL_GROUP, POOL_GROUP),
                 b_map=lambda i, g, k: (g, 0, 0), o_shape=(s, d), o_blk=(tm, POOL_GROUP), o_map=lambda i, g, k: (i, g),
                 grid=(nt, 4, 1), contract=NT, out_dtype=F32)
    dh = _pool_mix("pool_mix_t", dz, True, F32)
    return dh, dw4, reds


def _lane(shape):
    return lax.broadcasted_iota(jnp.int32, shape, 1)


def _rope_swap(v, transpose):
    half = QK_ROPE // 2
    lane = _lane(v.shape)
    up = pltpu.roll(v, v.shape[1] - half, 1)
    down = pltpu.roll(v, half, 1)
    if transpose:
        return jnp.where(lane < half, up, jnp.where(lane < QK_ROPE, -down, 0.0))
    return jnp.where(lane < half, -up, jnp.where(lane < QK_ROPE, down, 0.0))


def _rope(v, cos, sin):
    return v * cos + _rope_swap(v, False) * sin


def _rope_t(g, cos, sin):
    return g * cos + _rope_swap(g * sin, True)


def _mla_mid(lat, q_norm, kv_norm, cos_k, sin_k, tm):
    s = lat.shape[0]

    def fn(lv, qn, kn, cs, sn):
        cq = lv[:, :Q_LORA]
        ckv = lv[:, Q_LORA:Q_LORA + KV_LORA]
        kr = lv[:, Q_LORA + KV_LORA:]
        cq = cq * _rstd(cq) * qn
        ckv = ckv * _rstd(ckv) * kn
        return (cq, jnp.concatenate([ckv, _rope(kr, cs, sn)], axis=1)), ()

    (cq, kcat), _ = _rowmap("mla_mid", fn, [_tile(lat, tm), _row(q_norm), _row(kv_norm), _tile(cos_k, tm), _tile(sin_k, tm)],
                            [_otile(s, Q_LORA, BF16, tm), _otile(s, QK_PAD, BF16, tm)], [], (s // tm,))
    return cq, kcat


def _mla_mid_bwd(lat, dcq, dkcat, dv, q_norm, kv_norm, cos_k, sin_k, tm):
    s = lat.shape[0]

    def fn(lv, dq, dk, dvv, qn, kn, cs, sn):
        dk = dk * (1.0 / LOG2_E)
        cq = lv[:, :Q_LORA]
        ckv = lv[:, Q_LORA:Q_LORA + KV_LORA]
        rq, rk = _rstd(cq), _rstd(ckv)
        cqn, ckn = cq * rq, ckv * rk
        a = dq * qn
        d_cq = rq * (a - cqn * jnp.mean(a * cqn, axis=-1, keepdims=True))
        dckv = dk[:, :KV_LORA] + dvv
        a2 = dckv * kn
        d_ckv = rk * (a2 - ckn * jnp.mean(a2 * ckn, axis=-1, keepdims=True))
        d_kr = _rope_t(dk[:, KV_LORA:], cs, sn)
        return (jnp.concatenate([d_cq, d_ckv, d_kr], axis=1),), (_colsum(dq * cqn), _colsum(dckv * ckn))

    (dlat,), reds = _rowmap(
        "mla_mid_bwd", fn,
        [_tile(lat, tm), _tile(dcq, tm), _tile(dkcat, tm), _tile(dv, tm), _row(q_norm), _row(kv_norm), _tile(cos_k, tm),
         _tile(sin_k, tm)],
        [_otile(s, LAT_PAD, BF16, tm)], [_ored(Q_LORA), _ored(KV_LORA)], (s // tm,))
    return dlat, reds


def _mla_q(cq, wq, wukp, cos_k, sin_k, tm):
    s = cq.shape[0]

    def body(cq_ref, wq_ref, wuk_ref, cos_ref, sin_ref, o_ref):
        aq = _dot(cq_ref[...], wq_ref[...], NN)
        qlat = _dot(aq.astype(BF16), wuk_ref[...], NN)
        roped = _rope(aq[:, KV_LORA:], cos_ref[...], sin_ref[...])
        o_ref[...] = (jnp.concatenate([qlat[:, :KV_LORA], roped], axis=1) * (ATTN_SCALE * LOG2_E)).astype(o_ref.dtype)

    wblk = pl.BlockSpec((None, QK_PAD, QK_PAD), lambda h, i: (h, 0, 0))
    tblk = pl.BlockSpec((tm, KV_LORA), lambda h, i: (i, 0))
    return pl.pallas_call(
        body, name="mla_q", grid=(N_HEADS, s // tm),
        in_specs=[pl.BlockSpec((tm, Q_LORA), lambda h, i: (i, 0)), wblk, wblk, tblk, tblk],
        out_specs=pl.BlockSpec((None, tm, QK_PAD), lambda h, i: (h, i, 0)), out_shape=_sds((N_HEADS, s, QK_PAD), BF16),
        compiler_params=_params(),
    )(cq, wq, wukp, cos_k, sin_k)


def _mla_q_bwd(cq, wq, wukp, cos_k, sin_k, dqcat, tm):
    s = cq.shape[0]

    def body(cq_ref, wq_ref, wuk_ref, cos_ref, sin_ref, dq_ref, dcq_ref, dwq_ref, dwuk_ref):
        h, i = pl.program_id(0), pl.program_id(1)
        cqv = cq_ref[...]
        aq = _dot(cqv, wq_ref[...], NN).astype(BF16)
        g = dq_ref[...].astype(F32) * ATTN_SCALE
        gl, gr = g[:, :KV_LORA], g[:, KV_LORA:]
        dqlat = jnp.concatenate([gl, jnp.zeros_like(gl)], axis=1).astype(BF16)
        d_rope = _rope_t(gr, cos_ref[...], sin_ref[...])
        daq = _dot(dqlat, wuk_ref[...], NT) + jnp.concatenate([jnp.zeros_like(d_rope), d_rope], axis=1)
        daq_b = daq.astype(BF16)
        d_wuk = _dot(aq, dqlat, TN)
        d_wq = _dot(cqv, daq_b, TN)
        d_cq = _dot(daq_b, wq_ref[...], NT)
        rows = pl.ds(pl.multiple_of(i * tm, tm), tm)

        @pl.when(i == 0)
        def _():
            dwq_ref[...] = d_wq
            dwuk_ref[...] = d_wuk

        @pl.when(i != 0)
        def _():
            dwq_ref[...] += d_wq
            dwuk_ref[...] += d_wuk

        @pl.when(h == 0)
        def _():
            dcq_ref[rows, :] = d_cq

        @pl.when(h != 0)
        def _():
            dcq_ref[rows, :] += d_cq

    wblk = pl.BlockSpec((None, QK_PAD, QK_PAD), lambda h, i: (h, 0, 0))
    tblk = pl.BlockSpec((tm, KV_LORA), lambda h, i: (i, 0))
    return pl.pallas_call(
        body, name="mla_q_bwd", grid=(N_HEADS, s // tm),
        in_specs=[pl.BlockSpec((tm, Q_LORA), lambda h, i: (i, 0)), wblk, wblk, tblk, tblk,
                  pl.BlockSpec((None, tm, QK_PAD), lambda h, i: (h, i, 0))],
        out_specs=[pl.BlockSpec((s, Q_LORA), lambda h, i: (0, 0)), wblk, wblk],
        out_shape=[_sds((s, Q_LORA), F32), _sds((N_HEADS, QK_PAD, QK_PAD), F32), _sds((N_HEADS, QK_PAD, QK_PAD), F32)],
        compiler_params=_params(),
    )(cq, wq, wukp, cos_k, sin_k, dqcat)


def _flash_fwd(qcat, kcat, tq, tk):
    n_h, s, _ = qcat.shape
    n_k = s // tk

    def body(q_ref, k_ref, v_ref, o_ref, lse_ref):
        q = q_ref[...]
        m = jnp.full((tq, 1), -1e30, F32)
        l = jnp.zeros((tq, 1), F32)
        acc = jnp.zeros((tq, KV_LORA), F32)
        for kk in range(n_k):
            rows = pl.ds(kk * tk, tk)
            sc = _dot(q, k_ref[rows, :], NT)
            m_new = jnp.maximum(m, jnp.max(sc, axis=1, keepdims=True))
            alpha = jnp.exp2(m - m_new)
            p = jnp.exp2(sc - m_new)
            l = alpha * l + jnp.sum(p, axis=1, keepdims=True)
            acc = alpha * acc + _dot(p.astype(BF16), v_ref[rows, :], NN)
            m = m_new
        o_ref[...] = (acc / l).astype(o_ref.dtype)
        lse_ref[...] = m + jnp.log2(l)

    return pl.pallas_call(
        body, name="mla_attn", grid=(n_h, s // tq),
        in_specs=[pl.BlockSpec((None, tq, QK_PAD), lambda h, i: (h, i, 0)), pl.BlockSpec((s, QK_PAD), lambda h, i: (0, 0)),
                  pl.BlockSpec((s, KV_LORA), lambda h, i: (0, 0))],
        out_specs=[pl.BlockSpec((None, tq, KV_LORA), lambda h, i: (h, i, 0)), pl.BlockSpec((None, tq, 1), lambda h, i: (h, i, 0))],
        out_shape=[_sds((n_h, s, KV_LORA), BF16), _sds((n_h, s, 1), F32)], compiler_params=_params(),
    )(qcat, kcat, kcat)


def _flash_bwd(qcat, kcat, o, do, lse, tq, tk):
    n_h, s, _ = qcat.shape
    n_k = s // tk

    def body(q_ref, k_ref, v_ref, o_ref, do_ref, lse_ref, dq_ref, dk_ref, dv_ref, dq_acc):
        h, i = pl.program_id(0), pl.program_id(1)

        @pl.when(jnp.logical_and(h == 0, i == 0))
        def _():
            dk_ref[...] = jnp.zeros_like(dk_ref)
            dv_ref[...] = jnp.zeros_like(dv_ref)

        q = q_ref[...]
        dov = do_ref[...]
        lse_v = lse_ref[...]
        delta = jnp.sum(dov.astype(F32) * o_ref[...].astype(F32), axis=1, keepdims=True)
        dq_acc[...] = jnp.zeros_like(dq_acc)

        for kk in range(n_k):
            rows = pl.ds(kk * tk, tk)
            k = k_ref[rows, :]
            p = jnp.exp2(_dot(q, k, NT) - lse_v)
            dp = _dot(dov, v_ref[rows, :], NT)
            ds = (p * (dp - delta)).astype(BF16)
            dq_acc[...] += _dot(ds, k, NN)
            dv_ref[rows, :] += _dot(p.astype(BF16), dov, TN)
            dk_ref[rows, :] += _dot(ds, q, TN)
        dq_ref[...] = dq_acc[...].astype(dq_ref.dtype)

    qblk = pl.BlockSpec((None, tq, QK_PAD), lambda h, i: (h, i, 0))
    oblk = pl.BlockSpec((None, tq, KV_LORA), lambda h, i: (h, i, 0))
    return pl.pallas_call(
        body, name="mla_attn_bwd", grid=(n_h, s // tq),
        in_specs=[qblk, pl.BlockSpec((s, QK_PAD), lambda h, i: (0, 0)), pl.BlockSpec((s, KV_LORA), lambda h, i: (0, 0)), oblk, oblk,
                  pl.BlockSpec((None, tq, 1), lambda h, i: (h, i, 0))],
        out_specs=[qblk, pl.BlockSpec((s, QK_PAD), lambda h, i: (0, 0)), pl.BlockSpec((s, KV_LORA), lambda h, i: (0, 0))],
        out_shape=[_sds((n_h, s, QK_PAD), BF16), _sds((s, QK_PAD), F32), _sds((s, KV_LORA), F32)],
        scratch_shapes=[pltpu.VMEM((tq, QK_PAD), F32)], compiler_params=_params(),
    )(qcat, kcat, kcat, o, do, lse)


def _mla_fwd(h, wts, cos_k, sin_k, tm):
    s, d = h.shape
    nt = s // tm
    lat = _matmul("mla_lat", h, wts["w_in"], a_blk=(tm, d), a_map=lambda i, k: (i, 0), b_blk=(d, LAT_PAD), b_map=lambda i, k: (0, 0),
                  o_shape=(s, LAT_PAD), o_blk=(tm, LAT_PAD), o_map=lambda i, k: (i, 0), grid=(nt, 1), contract=NN, out_dtype=F32)
    cq, kcat = _mla_mid(lat, wts["q_norm"], wts["kv_norm"], cos_k, sin_k, tm)
    qcat = _mla_q(cq, wts["wq"], wts["wukp"], cos_k, sin_k, tm)
    o_lat, lse = _flash_fwd(qcat, kcat, tm, tm)
    o = _matmul("mla_uv", o_lat, wts["wuv2"], a_blk=(None, tm, KV_LORA), a_map=lambda i, p, r: (2 * p + r, i, 0),
                b_blk=(None, KV_LORA, 2 * V_HEAD), b_map=lambda i, p, r: (2 * p + r, 0, 0), o_shape=(s, d), o_blk=(tm, 2 * V_HEAD),
                o_map=lambda i, p, r: (i, p), grid=(nt, N_HEADS // 2, 2), contract=NN, out_dtype=BF16)
    u = _matmul("mla_out", o, wts["w_o"], a_blk=(tm, d), a_map=lambda i, k: (i, 0), b_blk=(d, d), b_map=lambda i, k: (0, 0),
                o_shape=(s, d), o_blk=(tm, d), o_map=lambda i, k: (i, 0), grid=(nt, 1), contract=NN, out_dtype=F32)
    return u, (lat, cq, kcat, qcat, o_lat, lse, o)


def _mla_bwd(du, h, saved, wts, cos_k, sin_k, tm):
    lat, cq, kcat, qcat, o_lat, lse, o = saved
    s, d = h.shape
    nt = s // tm
    do = _matmul("mla_do", du, wts["w_o"], a_blk=(tm, d), a_map=lambda i, k: (i, 0), b_blk=(d, d), b_map=lambda i, k: (0, 0),
                 o_shape=(s, d), o_blk=(tm, d), o_map=lambda i, k: (i, 0), grid=(nt, 1), contract=NT, out_dtype=BF16)
    dw_o = _matmul("mla_dwo", o, du, a_blk=(tm, d), a_map=lambda k: (k, 0), b_blk=(tm, d), b_map=lambda k: (k, 0),
                   o_shape=(d, d), o_blk=(d, d), o_map=lambda k: (0, 0), grid=(nt,), contract=TN, out_dtype=F32)
    do_lat = _matmul("mla_dolat", do, wts["wuv2"], a_blk=(tm, 2 * V_HEAD), a_map=lambda hh, i, k: (i, hh // 2),
                     b_blk=(None, KV_LORA, 2 * V_HEAD), b_map=lambda hh, i, k: (hh, 0, 0), o_shape=(N_HEADS, s, KV_LORA),
                     o_blk=(None, tm, KV_LORA), o_map=lambda hh, i, k: (hh, i, 0), grid=(N_HEADS, nt, 1), contract=NT, out_dtype=BF16)
    dwuv2 = _matmul("mla_dwuv", o_lat, do, a_blk=(None, tm, KV_LORA), a_map=lambda hh, k: (hh, k, 0), b_blk=(tm, 2 * V_HEAD),
                    b_map=lambda hh, k: (k, hh // 2), o_shape=(N_HEADS, KV_LORA, 2 * V_HEAD), o_blk=(None, KV_LORA, 2 * V_HEAD),
                    o_map=lambda hh, k: (hh, 0, 0), grid=(N_HEADS, nt), contract=TN, out_dtype=F32)
    dqcat, dkcat, dv = _flash_bwd(qcat, kcat, o_lat, do_lat, lse, tm, tm)
    dcq, dwq, dwukp = _mla_q_bwd(cq, wts["wq"], wts["wukp"], cos_k, sin_k, dqcat, tm)
    dlat, (dqn, dkn) = _mla_mid_bwd(lat, dcq, dkcat, dv, wts["q_norm"], wts["kv_norm"], cos_k, sin_k, tm)
    dh = _matmul("mla_dh", dlat, wts["w_in"], a_blk=(tm, LAT_PAD), a_map=lambda i, k: (i, 0), b_blk=(d, LAT_PAD),
                 b_map=lambda i, k: (0, 0), o_shape=(s, d), o_blk=(tm, d), o_map=lambda i, k: (i, 0), grid=(nt, 1), contract=NT,
                 out_dtype=F32)
    dw_in = _matmul("mla_dwin", h, dlat, a_blk=(tm, d), a_map=lambda k: (k, 0), b_blk=(tm, LAT_PAD), b_map=lambda k: (k, 0),
                    o_shape=(d, LAT_PAD), o_blk=(d, LAT_PAD), o_map=lambda k: (0, 0), grid=(nt,), contract=TN, out_dtype=F32)
    return dh, dict(w_in=dw_in, wq=dwq, wukp=dwukp, wuv2=dwuv2, w_o=dw_o, q_norm=dqn, kv_norm=dkn)


def _adamw(name, parts, w, m, v):
    n_parts, r, c = parts.shape
    tr = r
    for cand in (256, 128, 64, 32, 16, 8):
        if r > cand and r % cand == 0:
            tr = cand
            break

    def body(p_ref, w_ref, m_ref, v_ref, g_ref, d_ref, nm_ref, nv_ref):
        g = p_ref[0].astype(F32)
        for k in range(1, n_parts):
            g = g + p_ref[k].astype(F32)
        nm = ADAM_B1 * m_ref[...] + (1.0 - ADAM_B1) * g
        nv = ADAM_B2 * v_ref[...] + (1.0 - ADAM_B2) * (g * g)
        m_hat = nm / (1.0 - ADAM_B1 ** ADAM_STEP)
        v_hat = nv / (1.0 - ADAM_B2 ** ADAM_STEP)
        g_ref[...] = g
        d_ref[...] = -ADAM_LR * (m_hat / (jnp.sqrt(v_hat) + ADAM_EPS) + ADAM_WD * w_ref[...])
        nm_ref[...] = nm
        nv_ref[...] = nv

    blk = pl.BlockSpec((tr, c), lambda i: (i, 0))
    return pl.pallas_call(
        body, name=name, grid=(r // tr,), in_specs=[pl.BlockSpec((n_parts, tr, c), lambda i: (0, i, 0)), blk, blk, blk],
        out_specs=[blk] * 4, out_shape=[_sds((r, c), F32)] * 4, compiler_params=_params(),
    )(parts, w, m, v)


def _adamw_slab(name, parts, w, m, v, bufs, f):
    n_parts, r, c = parts.shape
    tr = max(t for t in range(8, 257, 8) if r % t == 0)

    def body(p_ref, w_ref, m_ref, v_ref, *rest):
        g_ref, d_ref, nm_ref, nv_ref = rest[4:]
        g = p_ref[0].astype(F32)
        for k in range(1, n_parts):
            g = g + p_ref[k].astype(F32)
        nm = ADAM_B1 * m_ref[...] + (1.0 - ADAM_B1) * g
        nv = ADAM_B2 * v_ref[...] + (1.0 - ADAM_B2) * (g * g)
        m_hat = nm / (1.0 - ADAM_B1 ** ADAM_STEP)
        v_hat = nv / (1.0 - ADAM_B2 ** ADAM_STEP)
        g_ref[...] = g
        d_ref[...] = -ADAM_LR * (m_hat / (jnp.sqrt(v_hat) + ADAM_EPS) + ADAM_WD * w_ref[...])
        nm_ref[...] = nm
        nv_ref[...] = nv

    blk = pl.BlockSpec((None, tr, c), lambda i: (f, i, 0))
    return pl.pallas_call(
        body, name=name, grid=(r // tr,),
        in_specs=[pl.BlockSpec((n_parts, tr, c), lambda i: (0, i, 0)), blk, blk, blk] + [pl.BlockSpec(memory_space=pl.ANY)] * 4,
        out_specs=[blk] * 4, out_shape=[_sds(w.shape, F32)] * 4, input_output_aliases={4 + j: j for j in range(4)},
        compiler_params=_params(),
    )(parts, w, m, v, *bufs)


def _mesh_pos():
    return lax.axis_index("x"), lax.axis_index("y"), lax.axis_index("c")


def _flip(pos, mask):
    return tuple(1 - p if (mask >> (2 - b)) & 1 else p for b, p in enumerate(pos))


def _index(pos):
    return 4 * pos[0] + 2 * pos[1] + pos[2]


def _all_gather(name, xs, after=None):
    n = len(xs)
    extra = [] if after is None else [after]

    def body(*refs):
        x_refs, o_refs = refs[:n], refs[n + len(extra):2 * n + len(extra)]
        send_sems, recv_sems, local_sems = refs[2 * n + len(extra):]
        me = _mesh_pos()
        sibling = _flip(me, 1)
        others = [_flip(me, 4), _flip(me, 2), _flip(me, 6)]

        def copy(k, j, block, to, src=None):
            dst = o_refs[k].at[_index(block)]
            return pltpu.make_async_remote_copy(
                src_ref=dst if src is None else src, dst_ref=dst, send_sem=send_sems.at[k, j], recv_sem=recv_sems.at[k, j],
                device_id=to, device_id_type=MESH)

        local = [pltpu.make_async_copy(x_refs[k], o_refs[k].at[_index(me)], local_sems.at[k]) for k in range(n)]
        for cp in local:
            cp.start()
        first = []
        for k in range(n):
            first.append(copy(k, 0, me, sibling, src=x_refs[k]))
            first += [copy(k, 1 + j, me, other, src=x_refs[k]) for j, other in enumerate(others)]
        for cp in first:
            cp.start()
        passed = []
        for j, other in enumerate(others):
            for k in range(n):
                copy(k, 1 + j, other, me).wait_recv()
                cp = copy(k, 4 + j, other, sibling)
                cp.start()
                passed.append(cp)
        for k in range(n):
            copy(k, 0, sibling, me).wait_recv()
        for j, other in enumerate(others):
            for k in range(n):
                copy(k, 4 + j, _flip(other, 1), me).wait_recv()
        for cp in first + passed:
            cp.wait_send()
        for cp in local:
            cp.wait()

    any_spec = pl.BlockSpec(memory_space=pl.ANY)
    return pl.pallas_call(
        body, name=name, in_specs=[any_spec] * (n + len(extra)), out_specs=[any_spec] * n,
        out_shape=[_sds((N_DEV,) + x.shape, x.dtype) for x in xs],
        scratch_shapes=[pltpu.SemaphoreType.DMA((n, 7)), pltpu.SemaphoreType.DMA((n, 7)), pltpu.SemaphoreType.DMA((n,))],
    )(*xs, *extra)


def _all_to_all(name, groups):
    flat = [(gi, f) for gi, grp in enumerate(groups) for f in range(len(grp))]
    n = len(flat)
    n_groups = len(groups)

    def body(*refs):
        x_refs, o_refs = refs[:n], refs[n:n + n_groups]
        send_sems, recv_sems, local_sems = refs[n + n_groups:]
        me = _mesh_pos()
        local, sends, recvs = [], [], []
        for k, (gi, f) in enumerate(flat):
            local.append(pltpu.make_async_copy(x_refs[k].at[_index(me)], o_refs[gi].at[_index(me), f], local_sems.at[k]))
            for mask in range(1, N_DEV):
                peer = _flip(me, mask)
                sends.append(pltpu.make_async_remote_copy(
                    src_ref=x_refs[k].at[_index(peer)], dst_ref=o_refs[gi].at[_index(me), f], send_sem=send_sems.at[k, mask - 1],
                    recv_sem=recv_sems.at[k, mask - 1], device_id=peer, device_id_type=MESH))
                recvs.append(pltpu.make_async_remote_copy(
                    src_ref=x_refs[k].at[_index(me)], dst_ref=o_refs[gi].at[_index(peer), f], send_sem=send_sems.at[k, mask - 1],
                    recv_sem=recv_sems.at[k, mask - 1], device_id=peer, device_id_type=MESH))
        for cp in local + sends:
            cp.start()
        for cp in recvs:
            cp.wait_recv()
        for cp in sends:
            cp.wait_send()
        for cp in local:
            cp.wait()

    any_spec = pl.BlockSpec(memory_space=pl.ANY)
    return pl.pallas_call(
        body, name=name, in_specs=[any_spec] * n, out_specs=[any_spec] * n_groups,
        out_shape=[_sds((N_DEV, len(grp)) + grp[0].shape[1:], grp[0].dtype) for grp in groups],
        scratch_shapes=[pltpu.SemaphoreType.DMA((n, 7)), pltpu.SemaphoreType.DMA((n, 7)), pltpu.SemaphoreType.DMA((n,))],
    )(*[a for grp in groups for a in grp])


def _split_copies(kind, outgoing, x_refs, land_refs, send_sems, recv_sems):
    me = _mesh_pos()
    copies = []
    for k, (x_ref, land_ref) in enumerate(zip(x_refs, land_refs)):
        for mask in range(1, N_DEV):
            peer = _flip(me, mask)
            sem = k * (N_DEV - 1) + mask - 1
            copies.append(pltpu.make_async_remote_copy(
                src_ref=x_ref if kind == "gather" else x_ref.at[_index(peer)], dst_ref=land_ref.at[_index(me if outgoing else peer)],
                send_sem=send_sems.at[sem], recv_sem=recv_sems.at[sem], device_id=peer, device_id_type=MESH))
    return copies


_HBM_SPEC = pl.BlockSpec(memory_space=pltpu.HBM)
_SEM_SPEC = pl.BlockSpec(memory_space=pltpu.SEMAPHORE)
_EFFECT = pltpu.SideEffectType.DATAFLOW_SIDE_EFFECTING


def _split_start(name, kind, xs, after=None):
    n = len(xs)
    extra = [] if after is None else [after]
    lands = [lax.empty(((N_DEV,) + x.shape) if kind == "gather" else x.shape, x.dtype) for x in xs]

    def body(*refs):
        x_refs, land_refs = refs[:n], refs[n:2 * n]
        send_sems, recv_sems = refs[2 * n + len(extra)], refs[2 * n + len(extra) + 1]
        token = refs[-1]
        for cp in _split_copies(kind, True, x_refs, land_refs, send_sems, recv_sems):
            cp.start()
        token[...] = jnp.zeros_like(token)

    hbm = [pltpu.HBM(a.shape, a.dtype) for a in list(xs) + lands]
    res = pl.pallas_call(
        body, name=name,
        out_shape=[pltpu.SemaphoreType.DMA((n * (N_DEV - 1),)), pltpu.SemaphoreType.DMA((n * (N_DEV - 1),))] + hbm + [_sds((8, 128), F32)],
        in_specs=[_HBM_SPEC] * (2 * n) + [pl.BlockSpec(memory_space=pl.ANY)] * len(extra),
        out_specs=[_SEM_SPEC, _SEM_SPEC] + [_HBM_SPEC] * (2 * n) + [pl.BlockSpec(memory_space=pltpu.VMEM)],
        input_output_aliases={j: 2 + j for j in range(2 * n)}, compiler_params=pltpu.CompilerParams(has_side_effects=_EFFECT),
    )(*[pltpu.with_memory_space_constraint(a, pltpu.HBM) for a in list(xs) + lands], *extra)
    return (kind, n, res[0], res[1], res[2:2 + 2 * n]), res[-1]


def _split_wait(name, state, after):
    kind, n, send_sems_in, recv_sems_in, thru = state

    def body(*refs):
        x_refs, land_refs = refs[:n], refs[n:2 * n]
        send_sems, recv_sems = refs[2 * n], refs[2 * n + 1]
        for cp in _split_copies(kind, True, x_refs, land_refs, send_sems, recv_sems):
            cp.wait_send()
        for cp in _split_copies(kind, False, x_refs, land_refs, send_sems, recv_sems):
            cp.wait_recv()

    res = pl.pallas_call(
        body, name=name, out_shape=[pltpu.HBM(a.shape, a.dtype) for a in thru],
        in_specs=[_HBM_SPEC] * (2 * n) + [_SEM_SPEC, _SEM_SPEC, pl.BlockSpec(memory_space=pl.ANY)], out_specs=[_HBM_SPEC] * (2 * n),
        input_output_aliases={j: j for j in range(2 * n)}, compiler_params=pltpu.CompilerParams(has_side_effects=_EFFECT),
    )(*thru, send_sems_in, recv_sems_in, after)
    me = _index(_mesh_pos())
    out = []
    for x, land in zip(res[:n], res[n:]):
        own = x if kind == "gather" else lax.dynamic_index_in_dim(x, me, 0, keepdims=False)
        out.append(lax.dynamic_update_slice(land, own[None], (me,) + (0,) * own.ndim))
    return out


def _rope_tables(s):
    inv = 1.0 / (ROPE_THETA ** (jnp.arange(0, QK_ROPE, 2, dtype=F32) / QK_ROPE))
    ang = jnp.arange(s, dtype=F32)[:, None] * inv[None, :]
    pad = jnp.zeros((s, KV_LORA - QK_ROPE), F32)
    return (jnp.concatenate([jnp.cos(ang), jnp.cos(ang), pad], axis=1), jnp.concatenate([jnp.sin(ang), jnp.sin(ang), pad], axis=1))


def _row_of(v):
    return v.reshape(1, -1)


def kernel(x, c, ada_w, ada_b, norm_g, ffn_w_in, ffn_w_out, pool_w, pool_b, pool_scale, mla_w_in, mla_q_norm, mla_kv_norm, mla_w_uq, mla_w_uk, mla_w_uv, mla_w_o, loss_target, m_ada_w, m_ada_b, m_norm_g, m_ffn_w_in, m_ffn_w_out, m_pool_w, m_pool_b, m_pool_scale, m_mla_w_in, m_mla_q_norm, m_mla_kv_norm, m_mla_w_uq, m_mla_w_uk, m_mla_w_uv, m_mla_w_o, v_ada_w, v_ada_b, v_norm_g, v_ffn_w_in, v_ffn_w_out, v_pool_w, v_pool_b, v_pool_scale, v_mla_w_in, v_mla_q_norm, v_mla_kv_norm, v_mla_w_uq, v_mla_w_uk, v_mla_w_uv, v_mla_w_o):
    s, d = x.shape[1], x.shape[2]
    tm = min(512, s)
    tr = min(512, s)
    tf = min(1024, s)
    me = 4 * lax.axis_index("x") + 2 * lax.axis_index("y") + lax.axis_index("c")
    x0 = x.reshape(s, d)
    target = loss_target.reshape(s, d)
    n_mod = ada_w.shape[2] * N_DEV // d
    mod_blk = ada_w.shape[2]

    small = jnp.concatenate([c.reshape(-1), norm_g.reshape(-1), pool_b.reshape(-1), mla_q_norm.reshape(-1)]).reshape(1, -1)
    w_in_t, m_in_t, v_in_t = (jnp.swapaxes(a, 2, 3) for a in (ffn_w_in, m_ffn_w_in, v_ffn_w_in))
    w_in_loc = [w_in_t[i, f].astype(BF16) for i in range(2) for f in range(2)]
    w_out_loc = [ffn_w_out[i, f].astype(BF16) for i in range(2) for f in range(2)]
    (small_all,) = _all_gather("gather_small", [small])
    small_all = small_all.reshape(N_DEV, -1)
    c_all = small_all[:, :d]
    off = d
    g_all = small_all[:, off:off + 12 * (d // N_DEV)].reshape(N_DEV, 2, 6, d // N_DEV).transpose(1, 2, 0, 3).reshape(2, 6, d)
    off += 12 * (d // N_DEV)
    pool_b_all = small_all[:, off:off + 4 * 32].reshape(N_DEV, 4, 32).transpose(1, 0, 2).reshape(1, d)
    off += 4 * 32
    q_norm_all = small_all[:, off:off + 32].reshape(1, Q_LORA)
    kv_norm_row = mla_kv_norm.reshape(1, KV_LORA)
    pscale_row = pool_scale.reshape(1, d)

    even = (jnp.arange(N_HEADS) % 2 == 0)[:, None, None]
    cos_k, sin_k = _rope_tables(s)

    def mla_weights(mla_w_in_all, mla_w_uq_all, mla_w_o_all):
        uq = mla_w_uq_all.reshape(Q_LORA, N_HEADS, QK_NOPE + QK_ROPE).transpose(1, 0, 2)
        zq = jnp.zeros((N_HEADS, Q_LORA, QK_NOPE), BF16)
        wq = jnp.concatenate(
            [uq[:, :, :QK_NOPE], zq, uq[:, :, QK_NOPE:], jnp.zeros((N_HEADS, Q_LORA, QK_PAD - KV_LORA - QK_ROPE), BF16)], axis=2)
        wukp = jnp.pad(mla_w_uk[0].transpose(1, 2, 0).astype(BF16), ((0, 0), (0, QK_PAD - QK_NOPE), (0, QK_PAD - KV_LORA)))
        uv = mla_w_uv[0].transpose(1, 0, 2).astype(BF16)
        wuv2 = jnp.where(even, jnp.concatenate([uv, jnp.zeros_like(uv)], axis=2), jnp.concatenate([jnp.zeros_like(uv), uv], axis=2))
        return dict(w_in=jnp.pad(mla_w_in_all.reshape(d, -1), ((0, 0), (0, LAT_PAD - mla_w_in.shape[2]))), wq=wq, wukp=wukp,
                    wuv2=wuv2, w_o=mla_w_o_all.reshape(d, d), q_norm=q_norm_all, kv_norm=kv_norm_row)

    (sc_all,), _ = _rowmap("ada_silu", lambda cv: ((cv * jax.nn.sigmoid(cv),), ()), [(c_all, (N_DEV, d), lambda i: (0, 0))],
                           [(_sds((N_DEV, d), F32), (N_DEV, d), lambda i: (0, 0))], [], (1,))
    ada_b_loc = lax.dynamic_slice_in_dim(ada_b, me * mod_blk, mod_blk, axis=1).reshape(2, 1, mod_blk)
    m_pad = 2 * N_DEV
    modp = _matmul("ada_mod", jnp.pad(sc_all, ((0, m_pad - N_DEV), (0, 0))), ada_w, a_blk=(m_pad, d), a_map=lambda i, k: (0, 0),
                   b_blk=(None, d, mod_blk), b_map=lambda i, k: (i, 0, 0), o_shape=(2, m_pad, mod_blk), o_blk=(None, m_pad, mod_blk),
                   o_map=lambda i, k: (i, 0, 0), grid=(2, 1), contract=NN, out_dtype=F32, bias=ada_b_loc, bias_blk=(None, 1, mod_blk),
                   bias_map=lambda i, k: (i, 0, 0))[:, :N_DEV]
    modp_all, w_in_first, w_out_first = _all_gather("gather_first", [modp.reshape(2 * N_DEV, mod_blk), w_in_loc[0], w_out_loc[0]])
    later_a = [w_in_loc[1], w_out_loc[1], pool_w.reshape(-1, POOL_GROUP).astype(BF16)]
    later_b = [w_in_loc[2], w_out_loc[2], mla_w_in[0].astype(BF16), mla_w_uq.reshape(mla_w_uq.shape[1], -1).astype(BF16),
               mla_w_o[0].astype(BF16)]
    later_c = [w_in_loc[3], w_out_loc[3]]
    state_a, token_a = _split_start("gather_start_a", "gather", later_a, after=modp_all)
    state_b, token_b = _split_start("gather_start_b", "gather", later_b, after=token_a)
    state_c, token_c = _split_start("gather_start_c", "gather", later_c, after=token_b)
    w_in8 = [w_in_first, None, None, None]
    w_out4 = [w_out_first.reshape(4, FF_BLK, d), None, None, None]
    mod = lax.dynamic_index_in_dim(modp_all.reshape(N_DEV, 2, N_DEV, mod_blk), me, axis=2, keepdims=False)
    mod = mod.transpose(1, 0, 2).reshape(2, n_mod, d) + token_c[0, 0]

    saved = []
    xs = x0
    w4 = mla_wts = None
    for i in range(2):
        for sub in range(3):
            if (i, sub) == (0, 1):
                lands = _split_wait("gather_wait_a", state_a, xs)
                w_in8[1], w_out4[1] = lands[0], lands[1].reshape(4, FF_BLK, d)
                w4 = lands[2].reshape(N_DEV, 4, 32, POOL_GROUP).transpose(1, 0, 2, 3).reshape(4, POOL_GROUP, POOL_GROUP)
            if (i, sub) == (1, 0):
                lands = _split_wait("gather_wait_b", state_b, xs)
                w_in8[2], w_out4[2] = lands[0], lands[1].reshape(4, FF_BLK, d)
                mla_wts = mla_weights(*lands[2:])
            if (i, sub) == (1, 2):
                lands = _split_wait("gather_wait_c", state_c, xs)
                w_in8[3], w_out4[3] = lands[0], lands[1].reshape(4, FF_BLK, d)
            shift, scale, gate = (_row_of(mod[i, 3 * sub + j]) for j in range(3))
            g_pre, g_post = _row_of(g_all[i, 2 * sub]), _row_of(g_all[i, 2 * sub + 1])
            tag = f"{i}{sub}"
            if sub != 1:
                f = sub // 2
                h = _prenorm(f"prenorm_{tag}", xs, g_pre, scale, shift, BF16, tr)
                u, extra = _ffn_fwd(tag, h, w_in8[2 * i + f], w_out4[2 * i + f], tf, tf // 2)
                weight = 0.5
            elif i == 0:
                h = _prenorm(f"prenorm_{tag}", xs, g_pre, scale, shift, F32, tr)
                u, z, v = _pool_fwd(h, w4, pool_b_all, pscale_row, tm)
                extra = (z, v)
                weight = 1.0
            else:
                h = _prenorm(f"prenorm_{tag}", xs, g_pre, scale, shift, BF16, tr)
                u, extra = _mla_fwd(h, mla_wts, cos_k, sin_k, tm)
                weight = 1.0
            saved.append((xs, h, u, extra, (shift, scale, gate, g_pre, g_post), weight))
            xs = _postnorm(f"postnorm_{tag}", xs, u, g_post, gate, weight, tr)

    def loss_fn(yv, tv):
        e = yv - tv
        return (e * (1.0 / d),), (_colsum(e * e),)

    (dx,), (sq,) = _rowmap("loss", loss_fn, [_tile(xs, tr), _tile(target, tr)], [_otile(s, d, F32, tr)], [_ored(d)], (s // tr,))
    loss = lax.psum(0.5 * jnp.sum(sq) / d, AXES)

    d_mod = [[None] * n_mod for _ in range(2)]
    d_g = [[None] * 6 for _ in range(2)]
    sent = {}
    pool_grads = mla_grads = None

    def start_scatter(key, arrays):
        state, token = _split_start(f"scatter_start_{key}", "scatter", arrays)
        sent[key] = state
        return token

    for i in (1, 0):
        for sub in (2, 1, 0):
            xin, h, u, extra, (shift, scale, gate, g_pre, g_post), weight = saved[3 * i + sub]
            tag = f"{i}{sub}"
            du, (dgate, dgpost) = _postnorm_bwd(f"postnorm_bwd_{tag}", dx, u, g_post, gate, weight, F32 if (sub == 1 and i == 0) else BF16, tr)
            if sub != 1:
                k = 2 * i + sub // 2
                dh, dgu, act = _ffn_bwd_act(tag, du, extra, w_in8[k], w_out4[k], tf, tf // 2)
                dw_out = _ffn_dw_out(tag, act, du, tm).reshape(N_DEV, FF_BLK // 2, d)
                if k == 0:
                    d_pool_w = pool_grads[0].reshape(4, N_DEV, 32, POOL_GROUP).transpose(1, 0, 2, 3).reshape(N_DEV, 4 * 32, POOL_GROUP)
                    token = start_scatter(tag + "_out", [dw_out, d_pool_w])
                    token = start_scatter(tag + "_in", [_ffn_dw_in(tag, h, dgu, tm, after=token)])
                else:
                    token = start_scatter(tag, [_ffn_dw_in(tag, h, dgu, tm), dw_out])
                g_pre = g_pre + token[0, 0]
            elif i == 0:
                dh, dw4, (dpscale, dpb) = _pool_bwd(du, extra[0], extra[1], w4, pscale_row, tm)
                pool_grads = (dw4, dpscale, dpb)
            else:
                dh, mla_grads = _mla_bwd(du, h, extra, mla_wts, cos_k, sin_k, tm)
                dwq = mla_grads["wq"]
                d_uq = jnp.concatenate([dwq[:, :, :QK_NOPE], dwq[:, :, KV_LORA:KV_LORA + QK_ROPE]], axis=2).transpose(1, 0, 2)
                token = start_scatter("mla", [mla_grads["w_in"][:, :mla_w_in.shape[2]].reshape(N_DEV, d // N_DEV, -1),
                                              d_uq.reshape(N_DEV, Q_LORA // N_DEV, -1), mla_grads["w_o"].reshape(N_DEV, d // N_DEV, d)])
                dwukp, dwuv2 = mla_grads["wukp"], mla_grads["wuv2"]
                d_uk = dwukp[:, :QK_NOPE, :KV_LORA].transpose(2, 0, 1).reshape(KV_LORA, -1)
                d_uv = jnp.where(even, dwuv2[:, :, :V_HEAD], dwuv2[:, :, V_HEAD:]).transpose(1, 0, 2).reshape(KV_LORA, -1)
                state_ukv, token_ukv = _split_start("gather_start_ukv", "gather", [d_uk, d_uv], after=token)
                g_pre = g_pre + token_ukv[0, 0]
            dx, (dshift, dscale, dgpre) = _prenorm_bwd(f"prenorm_bwd_{tag}", dh, xin, dx, g_pre, scale, tr)
            d_mod[i][3 * sub:3 * sub + 3] = [dshift, dscale, dgate]
            d_g[i][2 * sub:2 * sub + 2] = [dgpre, dgpost]
    grad_x = dx.reshape(x.shape)

    def upd(name, parts, w, m, v):
        shape = w.shape
        r, cdim = parts.shape[1], parts.shape[2]
        return [o.reshape(shape) for o in _adamw(name, parts, w.reshape(r, cdim), m.reshape(r, cdim), v.reshape(r, cdim))]

    def landed(key, after):
        return _split_wait(f"scatter_wait_{key}", sent[key], after)

    res = {}
    w_in_s, m_in_s, v_in_s = (a.reshape(4, FF_BLK, d) for a in (w_in_t, m_in_t, v_in_t))
    w_out_s, m_out_s, v_out_s = (a.reshape(4, FF_BLK // 2, d) for a in (ffn_w_out, m_ffn_w_out, v_ffn_w_out))
    bufs_in = [lax.empty(w_in_s.shape, F32) for _ in range(4)]
    bufs_out = [lax.empty(w_out_s.shape, F32) for _ in range(4)]
    for key, k in (("12", 3), ("mla", None), ("10", 2), ("02", 1)):
        parts = landed(key, grad_x)
        if k is None:
            res["mla_w_in"] = upd("adam_mla_w_in", parts[0], mla_w_in, m_mla_w_in, v_mla_w_in)
            res["mla_w_uq"] = upd("adam_mla_w_uq", parts[1], mla_w_uq, m_mla_w_uq, v_mla_w_uq)
            res["mla_w_o"] = upd("adam_mla_w_o", parts[2], mla_w_o, m_mla_w_o, v_mla_w_o)
            uk_all, uv_all = _split_wait("gather_wait_ukv", state_ukv, grad_x)
            res["mla_w_uk"] = upd("adam_mla_w_uk", uk_all, mla_w_uk, m_mla_w_uk, v_mla_w_uk)
            res["mla_w_uv"] = upd("adam_mla_w_uv", uv_all, mla_w_uv, m_mla_w_uv, v_mla_w_uv)
            continue
        bufs_in = _adamw_slab(f"adam_ffn_w_in_{key}", parts[0], w_in_s, m_in_s, v_in_s, bufs_in, k)
        bufs_out = _adamw_slab(f"adam_ffn_w_out_{key}", parts[1], w_out_s, m_out_s, v_out_s, bufs_out, k)

    dw4, dpscale, dpb = pool_grads
    d_mod_row = jnp.concatenate([jnp.concatenate(r, axis=1) for r in d_mod], axis=1)
    d_g_row = jnp.concatenate([jnp.concatenate(r, axis=1) for r in d_g], axis=1)
    small_g = jnp.concatenate([d_mod_row, d_g_row, dpb, dpscale, mla_grads["q_norm"], mla_grads["kv_norm"]], axis=1)
    (small_g_all,) = _all_gather("gather_small_grads", [small_g], after=bufs_out[0])
    small_g_all = small_g_all.reshape(N_DEV, -1)
    n_m = 2 * n_mod * d
    d_mod_all = small_g_all[:, :n_m].reshape(N_DEV, 2, n_mod * d)
    rest = small_g_all[:, n_m:]
    p_norm_g = lax.dynamic_slice_in_dim(rest[:, :12 * d].reshape(N_DEV, 12, d), me * (d // N_DEV), d // N_DEV, axis=2)
    p_pool_b = lax.dynamic_slice_in_dim(rest[:, 12 * d:13 * d].reshape(N_DEV, 4, POOL_GROUP), me * 32, 32, axis=2)
    p_pool_scale = rest[:, 13 * d:14 * d].reshape(N_DEV, 1, d)
    p_q_norm = lax.dynamic_slice_in_dim(rest[:, 14 * d:14 * d + Q_LORA], me * 32, 32, axis=1).reshape(N_DEV, 1, 32)
    p_kv_norm = rest[:, 14 * d + Q_LORA:].reshape(N_DEV, 1, KV_LORA)

    d_mod_loc = lax.dynamic_slice_in_dim(d_mod_all, me * mod_blk, mod_blk, axis=2).transpose(1, 0, 2)
    k_pad = 128
    sc_t = jnp.pad(sc_all.T, ((0, 0), (0, k_pad - N_DEV)))
    d_ada_w = _matmul("ada_dw", sc_t, jnp.pad(d_mod_loc, ((0, 0), (0, k_pad - N_DEV), (0, 0))), a_blk=(d, k_pad),
                      a_map=lambda i, k: (0, 0), b_blk=(None, k_pad, mod_blk), b_map=lambda i, k: (i, 0, 0), o_shape=(2, d, mod_blk),
                      o_blk=(None, d, mod_blk), o_map=lambda i, k: (i, 0, 0), grid=(2, 1), contract=NN, out_dtype=F32)
    res["ada_w"] = upd("adam_ada_w", d_ada_w.reshape(1, 2 * d, mod_blk), ada_w, m_ada_w, v_ada_w)
    res["ada_b"] = upd("adam_ada_b", d_mod_all.reshape(N_DEV, 2, n_mod * d), ada_b, m_ada_b, v_ada_b)
    res["norm_g"] = upd("adam_norm_g", p_norm_g, norm_g, m_norm_g, v_norm_g)
    res["pool_b"] = upd("adam_pool_b", p_pool_b, pool_b, m_pool_b, v_pool_b)
    res["pool_scale"] = upd("adam_pool_scale", p_pool_scale, pool_scale, m_pool_scale, v_pool_scale)
    res["mla_q_norm"] = upd("adam_mla_q_norm", p_q_norm, mla_q_norm, m_mla_q_norm, v_mla_q_norm)
    res["mla_kv_norm"] = upd("adam_mla_kv_norm", p_kv_norm, mla_kv_norm, m_mla_kv_norm, v_mla_kv_norm)

    p_out, p_pool_w = landed("00_out", res["ada_w"][1])
    bufs_out = _adamw_slab("adam_ffn_w_out_00", p_out, w_out_s, m_out_s, v_out_s, bufs_out, 0)
    res["pool_w"] = upd("adam_pool_w", p_pool_w, pool_w, m_pool_w, v_pool_w)
    (p_in,) = landed("00_in", res["pool_w"][1])
    bufs_in = _adamw_slab("adam_ffn_w_in_00", p_in, w_in_s, m_in_s, v_in_s, bufs_in, 0)
    res["ffn_w_in"] = [jnp.swapaxes(b.reshape(w_in_t.shape), 2, 3) for b in bufs_in]
    res["ffn_w_out"] = [b.reshape(ffn_w_out.shape) for b in bufs_out]

    order = ["ada_w", "ada_b", "norm_g", "ffn_w_in", "ffn_w_out", "pool_w", "pool_b", "pool_scale", "mla_w_in", "mla_q_norm",
             "mla_kv_norm", "mla_w_uq", "mla_w_uk", "mla_w_uv", "mla_w_o"]
    outs = [loss, grad_x]
    for j in range(4):
        outs += [res[name][j] for name in order]
    return tuple(outs)
```

```python
import functools

import jax
import jax.numpy as jnp
from jax import lax
from jax.experimental import pallas as pl
from jax.experimental.pallas import tpu as pltpu

F32 = jnp.float32
BF16 = jnp.bfloat16
N_DEV = 8
AXES = ("x", "y", "c")
MESH = pl.DeviceIdType.MESH

D_MODEL = 1024
N_HEADS = 16
QK_NOPE = 64
QK_ROPE = 32
V_HEAD = 64
Q_LORA = 256
KV_LORA = 128
LAT_PAD = 512
QK_PAD = 256
D_FF = 2816
FF_BLK = 2 * D_FF // N_DEV
POOL_WINDOWS = (2, 4, 8, 16)
POOL_GROUP = 256
ROPE_THETA = 10000.0
EPS = 1e-6
ATTN_SCALE = (QK_NOPE + QK_ROPE) ** -0.5
LOG2_E = 1.4426950408889634
ADAM_LR, ADAM_B1, ADAM_B2, ADAM_EPS, ADAM_WD, ADAM_STEP = 0.001, 0.9, 0.999, 1e-08, 0.01, 10
VMEM_LIMIT = 56 * 1024 * 1024

NN = ((1,), (0,))
NT = ((1,), (1,))
TN = ((0,), (0,))


def _params(**kw):
    return pltpu.CompilerParams(vmem_limit_bytes=VMEM_LIMIT, **kw)


def _dot(a, b, contract):
    return lax.dot_general(a, b, (contract, ((), ())), preferred_element_type=F32)


def _matmul(name, a, b, *, a_blk, a_map, b_blk, b_map, o_shape, o_blk, o_map, grid, contract, out_dtype,
            bias=None, bias_blk=None, bias_map=None, after=None):
    n_k = grid[-1]
    k_axis = len(grid) - 1
    acc_shape = tuple(d for d in o_blk if d is not None)

    def body(*refs):
        a_ref, b_ref = refs[:2]
        bias_ref = refs[2] if bias is not None else None
        if n_k == 1:
            r = _dot(a_ref[...].astype(BF16), b_ref[...].astype(BF16), contract)
            if bias is not None:
                r = r + bias_ref[...]
            refs[-1][...] = r.astype(refs[-1].dtype)
            return
        o_ref, acc = refs[-2:]
        k = pl.program_id(k_axis)

        @pl.when(k == 0)
        def _():
            acc[...] = jnp.zeros_like(acc)

        acc[...] += _dot(a_ref[...].astype(BF16), b_ref[...].astype(BF16), contract)

        @pl.when(k == n_k - 1)
        def _():
            r = acc[...]
            if bias is not None:
                r = r + bias_ref[...]
            o_ref[...] = r.astype(o_ref.dtype)

    in_specs = [pl.BlockSpec(a_blk, a_map), pl.BlockSpec(b_blk, b_map)]
    args = [a, b]
    if bias is not None:
        in_specs.append(pl.BlockSpec(bias_blk, bias_map))
        args.append(bias)
    if after is not None:
        in_specs.append(pl.BlockSpec(memory_space=pl.ANY))
        args.append(after)
    return pl.pallas_call(
        body, name=name, grid=grid, in_specs=in_specs, out_specs=pl.BlockSpec(o_blk, o_map),
        out_shape=jax.ShapeDtypeStruct(o_shape, out_dtype), scratch_shapes=[pltpu.VMEM(acc_shape, F32)] if n_k > 1 else [],
        compiler_params=_params(),
    )(*args)


def _rowmap(name, fn, ins, outs, reds, grid):
    n_in, n_out, n_red = len(ins), len(outs), len(reds)

    def body(*refs):
        in_refs = refs[:n_in]
        out_refs = refs[n_in:n_in + n_out]
        red_refs = refs[n_in + n_out:]
        out_vals, red_vals = fn(*[r[...] for r in in_refs])
        for r, v in zip(out_refs, out_vals):
            r[...] = v.astype(r.dtype)
        if n_red:
            first = pl.program_id(0) == 0
            for ax in range(1, len(grid)):
                first = jnp.logical_and(first, pl.program_id(ax) == 0)

            @pl.when(first)
            def _():
                for r in red_refs:
                    r[...] = jnp.zeros_like(r)

            for r, v in zip(red_refs, red_vals):
                r[...] += v

    res = pl.pallas_call(
        body, name=name, grid=grid,
        in_specs=[pl.BlockSpec(blk, imap) for _, blk, imap in ins],
        out_specs=[pl.BlockSpec(blk, imap) for _, blk, imap in list(outs) + list(reds)],
        out_shape=[sds for sds, _, _ in list(outs) + list(reds)],
        compiler_params=_params(),
    )(*[a for a, _, _ in ins])
    return res[:n_out], res[n_out:]


def _sds(shape, dtype):
    return jax.ShapeDtypeStruct(shape, dtype)


def _tile(a, tm):
    return (a, (tm, a.shape[1]), lambda i: (i, 0))


def _row(a):
    return (a, (1, a.shape[1]), lambda i: (0, 0))


def _otile(n, c, dtype, tm):
    return (_sds((n, c), dtype), (tm, c), lambda i: (i, 0))


def _ored(c):
    return (_sds((1, c), F32), (1, c), lambda i: (0, 0))


def _colsum(v):
    return jnp.sum(v, axis=0, keepdims=True)


def _rstd(v):
    return lax.rsqrt(jnp.mean(v * v, axis=-1, keepdims=True) + EPS)


def _prenorm(name, x, g_pre, scale, shift, out_dtype, tm):
    n, d = x.shape

    def fn(xv, g, sc, sh):
        return (xv * _rstd(xv) * g * (1.0 + sc) + sh,), ()

    (h,), _ = _rowmap(name, fn, [_tile(x, tm), _row(g_pre), _row(scale), _row(shift)], [_otile(n, d, out_dtype, tm)], [],
                      (n // tm,))
    return h


def _postnorm(name, x, u, g_post, gate, weight, tm):
    n, d = x.shape

    def fn(xv, uv, g, gt):
        return (xv + weight * (1.0 + gt) * (uv * _rstd(uv) * g),), ()

    (y,), _ = _rowmap(name, fn, [_tile(x, tm), _tile(u, tm), _row(g_post), _row(gate)], [_otile(n, d, F32, tm)], [], (n // tm,))
    return y


def _postnorm_bwd(name, dout, u, g_post, gate, weight, out_dtype, tm):
    n, d = u.shape

    def fn(dv, uv, g, gt):
        r = _rstd(uv)
        un = uv * r
        dy = dv * (weight * (1.0 + gt))
        a = dy * g
        du = r * (a - un * jnp.mean(a * un, axis=-1, keepdims=True))
        return (du,), (_colsum(dv * (weight * (un * g))), _colsum(dy * un))

    (du,), reds = _rowmap(name, fn, [_tile(dout, tm), _tile(u, tm), _row(g_post), _row(gate)], [_otile(n, d, out_dtype, tm)],
                          [_ored(d), _ored(d)], (n // tm,))
    return du, reds


def _prenorm_bwd(name, dh, x, dout, g_pre, scale, tm):
    n, d = x.shape

    def fn(dhv, xv, dv, g, sc):
        dhv = dhv.astype(F32)
        r = _rstd(xv)
        xn = xv * r
        b = dhv * (g * (1.0 + sc))
        dx = dv + r * (b - xn * jnp.mean(b * xn, axis=-1, keepdims=True))
        return (dx,), (_colsum(dhv), _colsum(dhv * (xn * g)), _colsum(dhv * ((1.0 + sc) * xn)))

    (dx,), reds = _rowmap(name, fn, [_tile(dh, tm), _tile(x, tm), _tile(dout, tm), _row(g_pre), _row(scale)],
                          [_otile(n, d, F32, tm)], [_ored(d), _ored(d), _ored(d)], (n // tm,))
    return dx, reds


def _ffn_fwd(tag, h, w_in8, w_out4, tm, sub):
    s, d = h.shape

    def body(h_ref, wg_ref, wu_ref, wo_ref, u_ref, gu_ref):
        @pl.when(pl.program_id(1) == 0)
        def _():
            u_ref[...] = jnp.zeros_like(u_ref)

        for r in range(tm // sub):
            rows = pl.ds(r * sub, sub)
            hv = h_ref[rows, :]
            gate = _dot(hv, wg_ref[...], NT)
            up = _dot(hv, wu_ref[...], NT)
            gu_ref[0, rows, :] = gate.astype(BF16)
            gu_ref[1, rows, :] = up.astype(BF16)
            u_ref[rows, :] += _dot((gate * jax.nn.sigmoid(gate) * up).astype(BF16), wo_ref[...], NN)

    w_blk = (None, FF_BLK, d)
    return pl.pallas_call(
        body, name=f"ffn_fwd_{tag}", grid=(s // tm, 4),
        in_specs=[pl.BlockSpec((tm, d), lambda i, j: (i, 0)), pl.BlockSpec(w_blk, lambda i, j: (j, 0, 0)),
                  pl.BlockSpec(w_blk, lambda i, j: (j + 4, 0, 0)), pl.BlockSpec((None, FF_BLK, d), lambda i, j: (j, 0, 0))],
        out_specs=[pl.BlockSpec((tm, d), lambda i, j: (i, 0)), pl.BlockSpec((2, None, tm, FF_BLK), lambda i, j: (0, j, i, 0))],
        out_shape=[_sds((s, d), F32), _sds((2, 4, s, FF_BLK), BF16)], compiler_params=_params(),
    )(h, w_in8, w_in8, w_out4)


def _ffn_bwd_act(tag, du, gu, w_in8, w_out4, tm, sub):
    s, d = du.shape

    def body(du_ref, gu_ref, wg_ref, wu_ref, wo_ref, dh_ref, dgu_ref, act_ref):
        @pl.when(pl.program_id(1) == 0)
        def _():
            dh_ref[...] = jnp.zeros_like(dh_ref)

        for r in range(tm // sub):
            rows = pl.ds(r * sub, sub)
            dact = _dot(du_ref[rows, :], wo_ref[...], NT)
            gate, up = gu_ref[0, rows, :].astype(F32), gu_ref[1, rows, :].astype(F32)
            sg = jax.nn.sigmoid(gate)
            silu = gate * sg
            dg = (dact * up * (sg * (1.0 + gate * (1.0 - sg)))).astype(BF16)
            dup = (dact * silu).astype(BF16)
            dgu_ref[0, :, rows] = dg.T
            dgu_ref[1, :, rows] = dup.T
            act_ref[:, rows] = (silu * up).astype(BF16).T
            dh_ref[rows, :] += _dot(dg, wg_ref[...], NN) + _dot(dup, wu_ref[...], NN)

    w_blk = (None, FF_BLK, d)
    dh, dgu_t, act_t = pl.pallas_call(
        body, name=f"ffn_bwd_{tag}", grid=(s // tm, 4),
        in_specs=[pl.BlockSpec((tm, d), lambda i, j: (i, 0)), pl.BlockSpec((2, None, tm, FF_BLK), lambda i, j: (0, j, i, 0)),
                  pl.BlockSpec(w_blk, lambda i, j: (j, 0, 0)), pl.BlockSpec(w_blk, lambda i, j: (j + 4, 0, 0)),
                  pl.BlockSpec((None, FF_BLK, d), lambda i, j: (j, 0, 0))],
        out_specs=[pl.BlockSpec((tm, d), lambda i, j: (i, 0)), pl.BlockSpec((2, None, FF_BLK, tm), lambda i, j: (0, j, 0, i)),
                   pl.BlockSpec((None, FF_BLK, tm), lambda i, j: (j, 0, i))],
        out_shape=[_sds((s, d), F32), _sds((2, 4, FF_BLK, s), BF16), _sds((4, FF_BLK, s), BF16)], compiler_params=_params(),
    )(du, gu, w_in8, w_in8, w_out4)
    return dh, dgu_t.reshape(8, FF_BLK, s), act_t


def _ffn_dw(name, lhs_t, rhs, tk, after=None):
    n_g, _, s = lhs_t.shape
    d = rhs.shape[1]
    return _matmul(name, lhs_t, rhs, a_blk=(None, FF_BLK, tk), a_map=lambda g, k: (g, 0, k), b_blk=(tk, d), b_map=lambda g, k: (k, 0),
                   o_shape=(n_g, FF_BLK, d), o_blk=(None, FF_BLK, d), o_map=lambda g, k: (g, 0, 0), grid=(n_g, s // tk),
                   contract=NN, out_dtype=BF16, after=after)


def _window_sum(x, window, transpose):
    s = x.shape[0]
    t = lax.broadcasted_iota(jnp.int32, (s, 1), 0)
    half = window // 2
    cnt = jnp.minimum(t + half, s) - jnp.maximum(t - half, 0)
    inv = 1.0 / cnt.astype(F32)
    if transpose:
        x = x * inv
        offsets = range(-half + 1, half + 1)
    else:
        offsets = range(-half, half)
    acc = jnp.zeros_like(x)
    for o in offsets:
        shifted = x if o == 0 else pltpu.roll(x, (-o) % s, 0)
        valid = jnp.logical_and(t + o >= 0, t + o < s)
        acc = acc + jnp.where(valid, shifted, 0.0)
    return acc if transpose else acc * inv


def _pool_mix(name, x, transpose, out_dtype):
    s, d = x.shape

    def body(x_ref, o_ref):
        g = pl.program_id(0)
        for gi, window in enumerate(POOL_WINDOWS):
            @pl.when(g == gi)
            def _(window=window):
                xv = x_ref[...].astype(F32)
                o_ref[...] = (_window_sum(xv, window, transpose) - xv).astype(o_ref.dtype)

    return pl.pallas_call(
        body, name=name, grid=(len(POOL_WINDOWS),), in_specs=[pl.BlockSpec((s, POOL_GROUP), lambda g: (0, g))],
        out_specs=pl.BlockSpec((s, POOL_GROUP), lambda g: (0, g)), out_shape=_sds((s, d), out_dtype), compiler_params=_params(),
    )(x)


def _pool_fwd(h, w4, bias, pscale, tm):
    s, d = h.shape
    nt = s // tm
    z = _pool_mix("pool_mix", h, False, BF16)
    v = _matmul("pool_proj", z, w4, a_blk=(tm, POOL_GROUP), a_map=lambda i, g, k: (i, g), b_blk=(None, POOL_GROUP, POOL_GROUP),
                b_map=lambda i, g, k: (g, 0, 0), o_shape=(s, d), o_blk=(tm, POOL_GROUP), o_map=lambda i, g, k: (i, g),
                grid=(nt, 4, 1), contract=NN, out_dtype=F32, bias=bias, bias_blk=(1, POOL_GROUP), bias_map=lambda i, g, k: (0, g))
    (u,), _ = _rowmap("pool_scale", lambda vv, ps: ((vv * ps,), ()), [_tile(v, tm), _row(pscale)], [_otile(s, d, F32, tm)], [],
                      (nt,))
    return u, z, v


def _pool_bwd(du, z, v, w4, pscale, tm):
    s, d = du.shape
    nt = s // tm

    def fn(duv, vv, ps):
        dv = duv * ps
        return (dv,), (_colsum(duv * vv), _colsum(dv))

    (dv,), reds = _rowmap("pool_dscale", fn, [_tile(du, tm), _tile(v, tm), _row(pscale)], [_otile(s, d, BF16, tm)],
                          [_ored(d), _ored(d)], (nt,))
    dw4 = _matmul("pool_dw", z, dv, a_blk=(tm, POOL_GROUP), a_map=lambda g, k: (k, g), b_blk=(tm, POOL_GROUP),
                  b_map=lambda g, k: (k, g), o_shape=(4, POOL_GROUP, POOL_GROUP), o_blk=(None, POOL_GROUP, POOL_GROUP),
                  o_map=lambda g, k: (g, 0, 0), grid=(4, nt), contract=TN, out_dtype=F32)
    dz = _matmul("pool_dz", dv, w4, a_blk=(tm, POOL_GROUP), a_map=lambda i, g, k: (i, g), b_blk=(None, POOL_GROUP, POOL_GROUP),
                 b_map=lambda i, g, k: (g, 0, 0), o_shape=(s, d), o_blk=(tm, POOL_GROUP), o_map=lambda i, g, k: (i, g),
                 grid=(nt, 4, 1), contract=NT, out_dtype=F32)
    dh = _pool_mix("pool_mix_t", dz, True, F32)
    return dh, dw4, reds


def _lane(shape):
    return lax.broadcasted_iota(jnp.int32, shape, 1)


def _rope_swap(v, transpose):
    half = QK_ROPE // 2
    lane = _lane(v.shape)
    up = pltpu.roll(v, v.shape[1] - half, 1)
    down = pltpu.roll(v, half, 1)
    if transpose:
        return jnp.where(lane < half, up, jnp.where(lane < QK_ROPE, -down, 0.0))
    return jnp.where(lane < half, -up, jnp.where(lane < QK_ROPE, down, 0.0))


def _rope(v, cos, sin):
    return v * cos + _rope_swap(v, False) * sin


def _rope_t(g, cos, sin):
    return g * cos + _rope_swap(g * sin, True)


def _mla_mid(lat, q_norm, kv_norm, cos_k, sin_k, tm):
    s = lat.shape[0]

    def fn(lv, qn, kn, cs, sn):
        cq = lv[:, :Q_LORA]
        ckv = lv[:, Q_LORA:Q_LORA + KV_LORA]
        kr = lv[:, Q_LORA + KV_LORA:]
        cq = cq * _rstd(cq) * qn
        ckv = ckv * _rstd(ckv) * kn
        return (cq, jnp.concatenate([ckv, _rope(kr, cs, sn)], axis=1)), ()

    (cq, kcat), _ = _rowmap("mla_mid", fn, [_tile(lat, tm), _row(q_norm), _row(kv_norm), _tile(cos_k, tm), _tile(sin_k, tm)],
                            [_otile(s, Q_LORA, BF16, tm), _otile(s, QK_PAD, BF16, tm)], [], (s // tm,))
    return cq, kcat


def _mla_mid_bwd(lat, dcq, dkcat, dv, q_norm, kv_norm, cos_k, sin_k, tm):
    s = lat.shape[0]

    def fn(lv, dq, dk, dvv, qn, kn, cs, sn):
        dk = dk * (1.0 / LOG2_E)
        cq = lv[:, :Q_LORA]
        ckv = lv[:, Q_LORA:Q_LORA + KV_LORA]
        rq, rk = _rstd(cq), _rstd(ckv)
        cqn, ckn = cq * rq, ckv * rk
        a = dq * qn
        d_cq = rq * (a - cqn * jnp.mean(a * cqn, axis=-1, keepdims=True))
        dckv = dk[:, :KV_LORA] + dvv
        a2 = dckv * kn
        d_ckv = rk * (a2 - ckn * jnp.mean(a2 * ckn, axis=-1, keepdims=True))
        d_kr = _rope_t(dk[:, KV_LORA:], cs, sn)
        return (jnp.concatenate([d_cq, d_ckv, d_kr], axis=1),), (_colsum(dq * cqn), _colsum(dckv * ckn))

    (dlat,), reds = _rowmap(
        "mla_mid_bwd", fn,
        [_tile(lat, tm), _tile(dcq, tm), _tile(dkcat, tm), _tile(dv, tm), _row(q_norm), _row(kv_norm), _tile(cos_k, tm),
         _tile(sin_k, tm)],
        [_otile(s, LAT_PAD, BF16, tm)], [_ored(Q_LORA), _ored(KV_LORA)], (s // tm,))
    return dlat, reds


def _mla_q(cq, wq, wukp, cos_k, sin_k, tm):
    s = cq.shape[0]

    def body(cq_ref, wq_ref, wuk_ref, cos_ref, sin_ref, o_ref):
        cqv, cs, sn = cq_ref[...], cos_ref[...], sin_ref[...]
        for h in range(N_HEADS):
            aq = _dot(cqv, wq_ref[h], NN)
            qlat = _dot(aq.astype(BF16), wuk_ref[h], NN)
            roped = _rope(aq[:, KV_LORA:], cs, sn)
            o_ref[h] = (jnp.concatenate([qlat[:, :KV_LORA], roped], axis=1) * (ATTN_SCALE * LOG2_E)).astype(o_ref.dtype)

    wblk = pl.BlockSpec((N_HEADS, QK_PAD, QK_PAD), lambda i: (0, 0, 0))
    tblk = pl.BlockSpec((tm, KV_LORA), lambda i: (i, 0))
    return pl.pallas_call(
        body, name="mla_q", grid=(s // tm,),
        in_specs=[pl.BlockSpec((tm, Q_LORA), lambda i: (i, 0)), wblk, wblk, tblk, tblk],
        out_specs=pl.BlockSpec((N_HEADS, tm, QK_PAD), lambda i: (0, i, 0)), out_shape=_sds((N_HEADS, s, QK_PAD), BF16),
        compiler_params=_params(),
    )(cq, wq, wukp, cos_k, sin_k)


def _mla_q_bwd(cq, wq, wukp, cos_k, sin_k, dqcat, tm):
    s = cq.shape[0]

    def body(cq_ref, wq_ref, wuk_ref, cos_ref, sin_ref, dq_ref, dcq_ref, dwq_ref, dwuk_ref):
        @pl.when(pl.program_id(0) == 0)
        def _():
            dwq_ref[...] = jnp.zeros_like(dwq_ref)
            dwuk_ref[...] = jnp.zeros_like(dwuk_ref)

        cqv, cs, sn = cq_ref[...], cos_ref[...], sin_ref[...]
        d_cq = jnp.zeros((tm, Q_LORA), F32)
        for h in range(N_HEADS):
            aq = _dot(cqv, wq_ref[h], NN).astype(BF16)
            g = dq_ref[h].astype(F32) * ATTN_SCALE
            gl, gr = g[:, :KV_LORA], g[:, KV_LORA:]
            dqlat = jnp.concatenate([gl, jnp.zeros_like(gl)], axis=1).astype(BF16)
            d_rope = _rope_t(gr, cs, sn)
            daq = _dot(dqlat, wuk_ref[h], NT) + jnp.concatenate([jnp.zeros_like(d_rope), d_rope], axis=1)
            daq_b = daq.astype(BF16)
            dwuk_ref[h] += _dot(aq, dqlat, TN)
            dwq_ref[h] += _dot(cqv, daq_b, TN)
            d_cq = d_cq + _dot(daq_b, wq_ref[h], NT)
        dcq_ref[...] = d_cq

    wblk = pl.BlockSpec((N_HEADS, QK_PAD, QK_PAD), lambda i: (0, 0, 0))
    tblk = pl.BlockSpec((tm, KV_LORA), lambda i: (i, 0))
    return pl.pallas_call(
        body, name="mla_q_bwd", grid=(s // tm,),
        in_specs=[pl.BlockSpec((tm, Q_LORA), lambda i: (i, 0)), wblk, wblk, tblk, tblk,
                  pl.BlockSpec((N_HEADS, tm, QK_PAD), lambda i: (0, i, 0))],
        out_specs=[pl.BlockSpec((tm, Q_LORA), lambda i: (i, 0)), wblk, wblk],
        out_shape=[_sds((s, Q_LORA), F32), _sds((N_HEADS, QK_PAD, QK_PAD), F32), _sds((N_HEADS, QK_PAD, QK_PAD), F32)],
        compiler_params=_params(),
    )(cq, wq, wukp, cos_k, sin_k, dqcat)


def _flash_fwd(qcat, kcat, tq, tk):
    n_h, s, _ = qcat.shape
    n_k = s // tk

    def body(q_ref, k_ref, v_ref, o_ref, lse_ref):
        q = q_ref[...]
        m = jnp.full((tq, 1), -1e30, F32)
        l = jnp.zeros((tq, 1), F32)
        acc = jnp.zeros((tq, KV_LORA), F32)
        for kk in range(n_k):
            rows = pl.ds(kk * tk, tk)
            sc = _dot(q, k_ref[rows, :], NT)
            m_new = jnp.maximum(m, jnp.max(sc, axis=1, keepdims=True))
            alpha = jnp.exp2(m - m_new)
            p = jnp.exp2(sc - m_new)
            l = alpha * l + jnp.sum(p, axis=1, keepdims=True)
            acc = alpha * acc + _dot(p.astype(BF16), v_ref[rows, :], NN)
            m = m_new
        o_ref[...] = (acc / l).astype(o_ref.dtype)
        lse_ref[...] = m + jnp.log2(l)

    return pl.pallas_call(
        body, name="mla_attn", grid=(n_h, s // tq),
        in_specs=[pl.BlockSpec((None, tq, QK_PAD), lambda h, i: (h, i, 0)), pl.BlockSpec((s, QK_PAD), lambda h, i: (0, 0)),
                  pl.BlockSpec((s, KV_LORA), lambda h, i: (0, 0))],
        out_specs=[pl.BlockSpec((None, tq, KV_LORA), lambda h, i: (h, i, 0)), pl.BlockSpec((None, tq, 1), lambda h, i: (h, i, 0))],
        out_shape=[_sds((n_h, s, KV_LORA), BF16), _sds((n_h, s, 1), F32)], compiler_params=_params(),
    )(qcat, kcat, kcat)


def _flash_bwd(qcat, kcat, o, do, lse, tq, tk):
    n_h, s, _ = qcat.shape
    n_k = s // tk

    def body(q_ref, k_ref, v_ref, o_ref, do_ref, lse_ref, dq_ref, dk_ref, dv_ref, dq_acc):
        h, i = pl.program_id(0), pl.program_id(1)

        @pl.when(jnp.logical_and(h == 0, i == 0))
        def _():
            dk_ref[...] = jnp.zeros_like(dk_ref)
            dv_ref[...] = jnp.zeros_like(dv_ref)

        q = q_ref[...]
        dov = do_ref[...]
        lse_v = lse_ref[...]
        delta = jnp.sum(dov.astype(F32) * o_ref[...].astype(F32), axis=1, keepdims=True)
        dq_acc[...] = jnp.zeros_like(dq_acc)

        for kk in range(n_k):
            rows = pl.ds(kk * tk, tk)
            k = k_ref[rows, :]
            p = jnp.exp2(_dot(q, k, NT) - lse_v)
            dp = _dot(dov, v_ref[rows, :], NT)
            ds = (p * (dp - delta)).astype(BF16)
            dq_acc[...] += _dot(ds, k, NN)
            dv_ref[rows, :] += _dot(p.astype(BF16), dov, TN)
            dk_ref[rows, :] += _dot(ds, q, TN)
        dq_ref[...] = dq_acc[...].astype(dq_ref.dtype)

    qblk = pl.BlockSpec((None, tq, QK_PAD), lambda h, i: (h, i, 0))
    oblk = pl.BlockSpec((None, tq, KV_LORA), lambda h, i: (h, i, 0))
    return pl.pallas_call(
        body, name="mla_attn_bwd", grid=(n_h, s // tq),
        in_specs=[qblk, pl.BlockSpec((s, QK_PAD), lambda h, i: (0, 0)), pl.BlockSpec((s, KV_LORA), lambda h, i: (0, 0)), oblk, oblk,
                  pl.BlockSpec((None, tq, 1), lambda h, i: (h, i, 0))],
        out_specs=[qblk, pl.BlockSpec((s, QK_PAD), lambda h, i: (0, 0)), pl.BlockSpec((s, KV_LORA), lambda h, i: (0, 0))],
        out_shape=[_sds((n_h, s, QK_PAD), BF16), _sds((s, QK_PAD), F32), _sds((s, KV_LORA), F32)],
        scratch_shapes=[pltpu.VMEM((tq, QK_PAD), F32)], compiler_params=_params(),
    )(qcat, kcat, kcat, o, do, lse)


def _mla_uv(o_lat, wuv2, do, tm):
    n_h, s, _ = o_lat.shape
    d = n_h * V_HEAD
    pair = 2 * V_HEAD
    lat_blk = pl.BlockSpec((n_h, tm, KV_LORA), lambda i: (0, i, 0))
    w_blk = pl.BlockSpec((n_h, KV_LORA, pair), lambda i: (0, 0, 0))
    row_blk = pl.BlockSpec((tm, d), lambda i: (i, 0))

    if do is None:
        def body(a_ref, w_ref, o_ref):
            for p in range(n_h // 2):
                o_ref[:, p * pair:(p + 1) * pair] = (
                    _dot(a_ref[2 * p], w_ref[2 * p], NN) + _dot(a_ref[2 * p + 1], w_ref[2 * p + 1], NN)).astype(o_ref.dtype)

        return pl.pallas_call(body, name="mla_uv", grid=(s // tm,), in_specs=[lat_blk, w_blk], out_specs=row_blk,
                              out_shape=_sds((s, d), BF16), compiler_params=_params())(o_lat, wuv2)

    def body(a_ref, w_ref, do_ref, dlat_ref, dw_ref):
        @pl.when(pl.program_id(0) == 0)
        def _():
            dw_ref[...] = jnp.zeros_like(dw_ref)

        for h in range(n_h):
            dov = do_ref[:, (h // 2) * pair:(h // 2 + 1) * pair]
            dlat_ref[h] = _dot(dov, w_ref[h], NT).astype(dlat_ref.dtype)
            dw_ref[h] += _dot(a_ref[h], dov, TN)

    return pl.pallas_call(body, name="mla_uv_bwd", grid=(s // tm,), in_specs=[lat_blk, w_blk, row_blk], out_specs=[lat_blk, w_blk],
                          out_shape=[_sds((n_h, s, KV_LORA), BF16), _sds((n_h, KV_LORA, pair), F32)], compiler_params=_params(),
                          )(o_lat, wuv2, do)


def _mla_fwd(h, wts, cos_k, sin_k, tm):
    s, d = h.shape
    nt = s // tm
    lat = _matmul("mla_lat", h, wts["w_in"], a_blk=(tm, d), a_map=lambda i, k: (i, 0), b_blk=(d, LAT_PAD), b_map=lambda i, k: (0, 0),
                  o_shape=(s, LAT_PAD), o_blk=(tm, LAT_PAD), o_map=lambda i, k: (i, 0), grid=(nt, 1), contract=NN, out_dtype=F32)
    cq, kcat = _mla_mid(lat, wts["q_norm"], wts["kv_norm"], cos_k, sin_k, tm)
    qcat = _mla_q(cq, wts["wq"], wts["wukp"], cos_k, sin_k, tm)
    o_lat, lse = _flash_fwd(qcat, kcat, tm, tm)
    o = _mla_uv(o_lat, wts["wuv2"], None, tm)
    u = _matmul("mla_out", o, wts["w_o"], a_blk=(tm, d), a_map=lambda i, k: (i, 0), b_blk=(d, d), b_map=lambda i, k: (0, 0),
                o_shape=(s, d), o_blk=(tm, d), o_map=lambda i, k: (i, 0), grid=(nt, 1), contract=NN, out_dtype=F32)
    return u, (lat, cq, kcat, qcat, o_lat, lse, o)


def _mla_bwd(du, h, saved, wts, cos_k, sin_k, tm):
    lat, cq, kcat, qcat, o_lat, lse, o = saved
    s, d = h.shape
    nt = s // tm
    do = _matmul("mla_do", du, wts["w_o"], a_blk=(tm, d), a_map=lambda i, k: (i, 0), b_blk=(d, d), b_map=lambda i, k: (0, 0),
                 o_shape=(s, d), o_blk=(tm, d), o_map=lambda i, k: (i, 0), grid=(nt, 1), contract=NT, out_dtype=BF16)
    dw_o = _matmul("mla_dwo", o, du, a_blk=(tm, d), a_map=lambda k: (k, 0), b_blk=(tm, d), b_map=lambda k: (k, 0),
                   o_shape=(d, d), o_blk=(d, d), o_map=lambda k: (0, 0), grid=(nt,), contract=TN, out_dtype=F32)
    do_lat, dwuv2 = _mla_uv(o_lat, wts["wuv2"], do, tm)
    dqcat, dkcat, dv = _flash_bwd(qcat, kcat, o_lat, do_lat, lse, tm, tm)
    dcq, dwq, dwukp = _mla_q_bwd(cq, wts["wq"], wts["wukp"], cos_k, sin_k, dqcat, tm)
    dlat, (dqn, dkn) = _mla_mid_bwd(lat, dcq, dkcat, dv, wts["q_norm"], wts["kv_norm"], cos_k, sin_k, tm)
    dh = _matmul("mla_dh", dlat, wts["w_in"], a_blk=(tm, LAT_PAD), a_map=lambda i, k: (i, 0), b_blk=(d, LAT_PAD),
                 b_map=lambda i, k: (0, 0), o_shape=(s, d), o_blk=(tm, d), o_map=lambda i, k: (i, 0), grid=(nt, 1), contract=NT,
                 out_dtype=F32)
    dw_in = _matmul("mla_dwin", h, dlat, a_blk=(tm, d), a_map=lambda k: (k, 0), b_blk=(tm, LAT_PAD), b_map=lambda k: (k, 0),
                    o_shape=(d, LAT_PAD), o_blk=(d, LAT_PAD), o_map=lambda k: (0, 0), grid=(nt,), contract=TN, out_dtype=F32)
    return dh, dict(w_in=dw_in, wq=dwq, wukp=dwukp, wuv2=dwuv2, w_o=dw_o, q_norm=dqn, kv_norm=dkn)


def _adamw(name, parts, w, m, v):
    n_parts, r, c = parts.shape
    tr = r
    for cand in (256, 128, 64, 32, 16, 8):
        if r > cand and r % cand == 0:
            tr = cand
            break

    def body(p_ref, w_ref, m_ref, v_ref, g_ref, d_ref, nm_ref, nv_ref):
        g = p_ref[0].astype(F32)
        for k in range(1, n_parts):
            g = g + p_ref[k].astype(F32)
        nm = ADAM_B1 * m_ref[...] + (1.0 - ADAM_B1) * g
        nv = ADAM_B2 * v_ref[...] + (1.0 - ADAM_B2) * (g * g)
        m_hat = nm / (1.0 - ADAM_B1 ** ADAM_STEP)
        v_hat = nv / (1.0 - ADAM_B2 ** ADAM_STEP)
        g_ref[...] = g
        d_ref[...] = -ADAM_LR * (m_hat / (jnp.sqrt(v_hat) + ADAM_EPS) + ADAM_WD * w_ref[...])
        nm_ref[...] = nm
        nv_ref[...] = nv

    blk = pl.BlockSpec((tr, c), lambda i: (i, 0))
    return pl.pallas_call(
        body, name=name, grid=(r // tr,), in_specs=[pl.BlockSpec((n_parts, tr, c), lambda i: (0, i, 0)), blk, blk, blk],
        out_specs=[blk] * 4, out_shape=[_sds((r, c), F32)] * 4, compiler_params=_params(),
    )(parts, w, m, v)


def _adamw_slab(name, parts, w, m, v, bufs, f):
    n_parts, r, c = parts.shape
    tr = max(t for t in range(8, 257, 8) if r % t == 0)

    def body(p_ref, w_ref, m_ref, v_ref, *rest):
        g_ref, d_ref, nm_ref, nv_ref = rest[4:]
        g = p_ref[0].astype(F32)
        for k in range(1, n_parts):
            g = g + p_ref[k].astype(F32)
        nm = ADAM_B1 * m_ref[...] + (1.0 - ADAM_B1) * g
        nv = ADAM_B2 * v_ref[...] + (1.0 - ADAM_B2) * (g * g)
        m_hat = nm / (1.0 - ADAM_B1 ** ADAM_STEP)
        v_hat = nv / (1.0 - ADAM_B2 ** ADAM_STEP)
        g_ref[...] = g
        d_ref[...] = -ADAM_LR * (m_hat / (jnp.sqrt(v_hat) + ADAM_EPS) + ADAM_WD * w_ref[...])
        nm_ref[...] = nm
        nv_ref[...] = nv

    blk = pl.BlockSpec((None, tr, c), lambda i: (f, i, 0))
    return pl.pallas_call(
        body, name=name, grid=(r // tr,),
        in_specs=[pl.BlockSpec((n_parts, tr, c), lambda i: (0, i, 0)), blk, blk, blk] + [pl.BlockSpec(memory_space=pl.ANY)] * 4,
        out_specs=[blk] * 4, out_shape=[_sds(w.shape, F32)] * 4, input_output_aliases={4 + j: j for j in range(4)},
        compiler_params=_params(),
    )(parts, w, m, v, *bufs)


def _mesh_pos():
    return lax.axis_index("x"), lax.axis_index("y"), lax.axis_index("c")


def _flip(pos, mask):
    return tuple(1 - p if (mask >> (2 - b)) & 1 else p for b, p in enumerate(pos))


def _index(pos):
    return 4 * pos[0] + 2 * pos[1] + pos[2]


def _all_gather(name, xs, after=None):
    n = len(xs)
    extra = [] if after is None else [after]

    def body(*refs):
        x_refs, o_refs = refs[:n], refs[n + len(extra):2 * n + len(extra)]
        send_sems, recv_sems, local_sems = refs[2 * n + len(extra):]
        me = _mesh_pos()
        sibling = _flip(me, 1)
        others = [_flip(me, 4), _flip(me, 2), _flip(me, 6)]

        def copy(k, j, block, to, src=None):
            dst = o_refs[k].at[_index(block)]
            return pltpu.make_async_remote_copy(
                src_ref=dst if src is None else src, dst_ref=dst, send_sem=send_sems.at[k, j], recv_sem=recv_sems.at[k, j],
                device_id=to, device_id_type=MESH)

        local = [pltpu.make_async_copy(x_refs[k], o_refs[k].at[_index(me)], local_sems.at[k]) for k in range(n)]
        for cp in local:
            cp.start()
        first = []
        for k in range(n):
            first.append(copy(k, 0, me, sibling, src=x_refs[k]))
            first += [copy(k, 1 + j, me, other, src=x_refs[k]) for j, other in enumerate(others)]
        for cp in first:
            cp.start()
        passed = []
        for j, other in enumerate(others):
            for k in range(n):
                copy(k, 1 + j, other, me).wait_recv()
                cp = copy(k, 4 + j, other, sibling)
                cp.start()
                passed.append(cp)
        for k in range(n):
            copy(k, 0, sibling, me).wait_recv()
        for j, other in enumerate(others):
            for k in range(n):
                copy(k, 4 + j, _flip(other, 1), me).wait_recv()
        for cp in first + passed:
            cp.wait_send()
        for cp in local:
            cp.wait()

    any_spec = pl.BlockSpec(memory_space=pl.ANY)
    return pl.pallas_call(
        body, name=name, in_specs=[any_spec] * (n + len(extra)), out_specs=[any_spec] * n,
        out_shape=[_sds((N_DEV,) + x.shape, x.dtype) for x in xs],
        scratch_shapes=[pltpu.SemaphoreType.DMA((n, 7)), pltpu.SemaphoreType.DMA((n, 7)), pltpu.SemaphoreType.DMA((n,))],
    )(*xs, *extra)


def _all_to_all(name, groups):
    flat = [(gi, f) for gi, grp in enumerate(groups) for f in range(len(grp))]
    n = len(flat)
    n_groups = len(groups)

    def body(*refs):
        x_refs, o_refs = refs[:n], refs[n:n + n_groups]
        send_sems, recv_sems, local_sems = refs[n + n_groups:]
        me = _mesh_pos()
        local, sends, recvs = [], [], []
        for k, (gi, f) in enumerate(flat):
            local.append(pltpu.make_async_copy(x_refs[k].at[_index(me)], o_refs[gi].at[_index(me), f], local_sems.at[k]))
            for mask in range(1, N_DEV):
                peer = _flip(me, mask)
                sends.append(pltpu.make_async_remote_copy(
                    src_ref=x_refs[k].at[_index(peer)], dst_ref=o_refs[gi].at[_index(me), f], send_sem=send_sems.at[k, mask - 1],
                    recv_sem=recv_sems.at[k, mask - 1], device_id=peer, device_id_type=MESH))
                recvs.append(pltpu.make_async_remote_copy(
                    src_ref=x_refs[k].at[_index(me)], dst_ref=o_refs[gi].at[_index(peer), f], send_sem=send_sems.at[k, mask - 1],
                    recv_sem=recv_sems.at[k, mask - 1], device_id=peer, device_id_type=MESH))
        for cp in local + sends:
            cp.start()
        for cp in recvs:
            cp.wait_recv()
        for cp in sends:
            cp.wait_send()
        for cp in local:
            cp.wait()

    any_spec = pl.BlockSpec(memory_space=pl.ANY)
    return pl.pallas_call(
        body, name=name, in_specs=[any_spec] * n, out_specs=[any_spec] * n_groups,
        out_shape=[_sds((N_DEV, len(grp)) + grp[0].shape[1:], grp[0].dtype) for grp in groups],
        scratch_shapes=[pltpu.SemaphoreType.DMA((n, 7)), pltpu.SemaphoreType.DMA((n, 7)), pltpu.SemaphoreType.DMA((n,))],
    )(*[a for grp in groups for a in grp])


def _split_copies(kind, outgoing, x_refs, land_refs, send_sems, recv_sems):
    me = _mesh_pos()
    copies = []
    for k, (x_ref, land_ref) in enumerate(zip(x_refs, land_refs)):
        for mask in range(1, N_DEV):
            peer = _flip(me, mask)
            sem = k * (N_DEV - 1) + mask - 1
            copies.append(pltpu.make_async_remote_copy(
                src_ref=x_ref if kind == "gather" else x_ref.at[_index(peer)], dst_ref=land_ref.at[_index(me if outgoing else peer)],
                send_sem=send_sems.at[sem], recv_sem=recv_sems.at[sem], device_id=peer, device_id_type=MESH))
    return copies


_HBM_SPEC = pl.BlockSpec(memory_space=pltpu.HBM)
_SEM_SPEC = pl.BlockSpec(memory_space=pltpu.SEMAPHORE)
_EFFECT = pltpu.SideEffectType.DATAFLOW_SIDE_EFFECTING


def _split_start(name, kind, xs, after=None):
    n = len(xs)
    extra = [] if after is None else [after]
    lands = [lax.empty(((N_DEV,) + x.shape) if kind == "gather" else x.shape, x.dtype) for x in xs]

    def body(*refs):
        x_refs, land_refs = refs[:n], refs[n:2 * n]
        send_sems, recv_sems = refs[2 * n + len(extra)], refs[2 * n + len(extra) + 1]
        token = refs[-1]
        for cp in _split_copies(kind, True, x_refs, land_refs, send_sems, recv_sems):
            cp.start()
        token[...] = jnp.zeros_like(token)

    hbm = [pltpu.HBM(a.shape, a.dtype) for a in list(xs) + lands]
    res = pl.pallas_call(
        body, name=name,
        out_shape=[pltpu.SemaphoreType.DMA((n * (N_DEV - 1),)), pltpu.SemaphoreType.DMA((n * (N_DEV - 1),))] + hbm + [_sds((8, 128), F32)],
        in_specs=[_HBM_SPEC] * (2 * n) + [pl.BlockSpec(memory_space=pl.ANY)] * len(extra),
        out_specs=[_SEM_SPEC, _SEM_SPEC] + [_HBM_SPEC] * (2 * n) + [pl.BlockSpec(memory_space=pltpu.VMEM)],
        input_output_aliases={j: 2 + j for j in range(2 * n)}, compiler_params=pltpu.CompilerParams(has_side_effects=_EFFECT),
    )(*[pltpu.with_memory_space_constraint(a, pltpu.HBM) for a in list(xs) + lands], *extra)
    return (kind, n, res[0], res[1], res[2:2 + 2 * n]), res[-1]


def _split_wait(name, state, after):
    kind, n, send_sems_in, recv_sems_in, thru = state

    def body(*refs):
        x_refs, land_refs = refs[:n], refs[n:2 * n]
        send_sems, recv_sems = refs[2 * n], refs[2 * n + 1]
        for cp in _split_copies(kind, True, x_refs, land_refs, send_sems, recv_sems):
            cp.wait_send()
        for cp in _split_copies(kind, False, x_refs, land_refs, send_sems, recv_sems):
            cp.wait_recv()

    res = pl.pallas_call(
        body, name=name, out_shape=[pltpu.HBM(a.shape, a.dtype) for a in thru],
        in_specs=[_HBM_SPEC] * (2 * n) + [_SEM_SPEC, _SEM_SPEC, pl.BlockSpec(memory_space=pl.ANY)], out_specs=[_HBM_SPEC] * (2 * n),
        input_output_aliases={j: j for j in range(2 * n)}, compiler_params=pltpu.CompilerParams(has_side_effects=_EFFECT),
    )(*thru, send_sems_in, recv_sems_in, after)
    me = _index(_mesh_pos())
    out = []
    for x, land in zip(res[:n], res[n:]):
        own = x if kind == "gather" else lax.dynamic_index_in_dim(x, me, 0, keepdims=False)
        out.append(lax.dynamic_update_slice(land, own[None], (me,) + (0,) * own.ndim))
    return out


def _rope_tables(s):
    inv = 1.0 / (ROPE_THETA ** (jnp.arange(0, QK_ROPE, 2, dtype=F32) / QK_ROPE))
    ang = jnp.arange(s, dtype=F32)[:, None] * inv[None, :]
    pad = jnp.zeros((s, KV_LORA - QK_ROPE), F32)
    return (jnp.concatenate([jnp.cos(ang), jnp.cos(ang), pad], axis=1), jnp.concatenate([jnp.sin(ang), jnp.sin(ang), pad], axis=1))


def _row_of(v):
    return v.reshape(1, -1)


def kernel(x, c, ada_w, ada_b, norm_g, ffn_w_in, ffn_w_out, pool_w, pool_b, pool_scale, mla_w_in, mla_q_norm, mla_kv_norm, mla_w_uq, mla_w_uk, mla_w_uv, mla_w_o, loss_target, m_ada_w, m_ada_b, m_norm_g, m_ffn_w_in, m_ffn_w_out, m_pool_w, m_pool_b, m_pool_scale, m_mla_w_in, m_mla_q_norm, m_mla_kv_norm, m_mla_w_uq, m_mla_w_uk, m_mla_w_uv, m_mla_w_o, v_ada_w, v_ada_b, v_norm_g, v_ffn_w_in, v_ffn_w_out, v_pool_w, v_pool_b, v_pool_scale, v_mla_w_in, v_mla_q_norm, v_mla_kv_norm, v_mla_w_uq, v_mla_w_uk, v_mla_w_uv, v_mla_w_o):
    s, d = x.shape[1], x.shape[2]
    tm = min(512, s)
    tr = min(512, s)
    tf = min(1024, s)
    me = 4 * lax.axis_index("x") + 2 * lax.axis_index("y") + lax.axis_index("c")
    x0 = x.reshape(s, d)
    target = loss_target.reshape(s, d)
    n_mod = ada_w.shape[2] * N_DEV // d
    mod_blk = ada_w.shape[2]

    small = jnp.concatenate([c.reshape(-1), norm_g.reshape(-1), pool_b.reshape(-1), mla_q_norm.reshape(-1)]).reshape(1, -1)
    w_in_t, m_in_t, v_in_t = (jnp.swapaxes(a, 2, 3) for a in (ffn_w_in, m_ffn_w_in, v_ffn_w_in))
    w_in_loc = [w_in_t[i, f].astype(BF16) for i in range(2) for f in range(2)]
    w_out_loc = [ffn_w_out[i, f].astype(BF16) for i in range(2) for f in range(2)]
    (small_all,) = _all_gather("gather_small", [small])
    small_all = small_all.reshape(N_DEV, -1)
    c_all = small_all[:, :d]
    off = d
    g_all = small_all[:, off:off + 12 * (d // N_DEV)].reshape(N_DEV, 2, 6, d // N_DEV).transpose(1, 2, 0, 3).reshape(2, 6, d)
    off += 12 * (d // N_DEV)
    pool_b_all = small_all[:, off:off + 4 * 32].reshape(N_DEV, 4, 32).transpose(1, 0, 2).reshape(1, d)
    off += 4 * 32
    q_norm_all = small_all[:, off:off + 32].reshape(1, Q_LORA)
    kv_norm_row = mla_kv_norm.reshape(1, KV_LORA)
    pscale_row = pool_scale.reshape(1, d)

    even = (jnp.arange(N_HEADS) % 2 == 0)[:, None, None]
    cos_k, sin_k = _rope_tables(s)

    def mla_weights(mla_w_in_all, mla_w_uq_all, mla_w_o_all):
        uq = mla_w_uq_all.reshape(Q_LORA, N_HEADS, QK_NOPE + QK_ROPE).transpose(1, 0, 2)
        zq = jnp.zeros((N_HEADS, Q_LORA, QK_NOPE), BF16)
        wq = jnp.concatenate(
            [uq[:, :, :QK_NOPE], zq, uq[:, :, QK_NOPE:], jnp.zeros((N_HEADS, Q_LORA, QK_PAD - KV_LORA - QK_ROPE), BF16)], axis=2)
        wukp = jnp.pad(mla_w_uk[0].transpose(1, 2, 0).astype(BF16), ((0, 0), (0, QK_PAD - QK_NOPE), (0, QK_PAD - KV_LORA)))
        uv = mla_w_uv[0].transpose(1, 0, 2).astype(BF16)
        wuv2 = jnp.where(even, jnp.concatenate([uv, jnp.zeros_like(uv)], axis=2), jnp.concatenate([jnp.zeros_like(uv), uv], axis=2))
        return dict(w_in=jnp.pad(mla_w_in_all.reshape(d, -1), ((0, 0), (0, LAT_PAD - mla_w_in.shape[2]))), wq=wq, wukp=wukp,
                    wuv2=wuv2, w_o=mla_w_o_all.reshape(d, d), q_norm=q_norm_all, kv_norm=kv_norm_row)

    (sc_all,), _ = _rowmap("ada_silu", lambda cv: ((cv * jax.nn.sigmoid(cv),), ()), [(c_all, (N_DEV, d), lambda i: (0, 0))],
                           [(_sds((N_DEV, d), F32), (N_DEV, d), lambda i: (0, 0))], [], (1,))
    ada_b_loc = lax.dynamic_slice_in_dim(ada_b, me * mod_blk, mod_blk, axis=1).reshape(2, 1, mod_blk)
    m_pad = 2 * N_DEV
    modp = _matmul("ada_mod", jnp.pad(sc_all, ((0, m_pad - N_DEV), (0, 0))), ada_w, a_blk=(m_pad, d), a_map=lambda i, k: (0, 0),
                   b_blk=(None, d, mod_blk), b_map=lambda i, k: (i, 0, 0), o_shape=(2, m_pad, mod_blk), o_blk=(None, m_pad, mod_blk),
                   o_map=lambda i, k: (i, 0, 0), grid=(2, 1), contract=NN, out_dtype=F32, bias=ada_b_loc, bias_blk=(None, 1, mod_blk),
                   bias_map=lambda i, k: (i, 0, 0))[:, :N_DEV]
    modp_all, w_in_first, w_out_first = _all_gather("gather_first", [modp.reshape(2 * N_DEV, mod_blk), w_in_loc[0], w_out_loc[0]])
    later_a = [w_in_loc[1], w_out_loc[1], pool_w.reshape(-1, POOL_GROUP).astype(BF16)]
    later_b = [w_in_loc[2], w_out_loc[2], mla_w_in[0].astype(BF16), mla_w_uq.reshape(mla_w_uq.shape[1], -1).astype(BF16),
               mla_w_o[0].astype(BF16)]
    later_c = [w_in_loc[3], w_out_loc[3]]
    state_a, token_a = _split_start("gather_start_a", "gather", later_a, after=modp_all)
    state_b, token_b = _split_start("gather_start_b", "gather", later_b, after=token_a)
    state_c, token_c = _split_start("gather_start_c", "gather", later_c, after=token_b)
    w_in8 = [w_in_first, None, None, None]
    w_out4 = [w_out_first.reshape(4, FF_BLK, d), None, None, None]
    mod = lax.dynamic_index_in_dim(modp_all.reshape(N_DEV, 2, N_DEV, mod_blk), me, axis=2, keepdims=False)
    mod = mod.transpose(1, 0, 2).reshape(2, n_mod, d) + token_c[0, 0]

    saved = []
    xs = x0
    w4 = mla_wts = None
    for i in range(2):
        for sub in range(3):
            if (i, sub) == (0, 1):
                lands = _split_wait("gather_wait_a", state_a, xs)
                w_in8[1], w_out4[1] = lands[0], lands[1].reshape(4, FF_BLK, d)
                w4 = lands[2].reshape(N_DEV, 4, 32, POOL_GROUP).transpose(1, 0, 2, 3).reshape(4, POOL_GROUP, POOL_GROUP)
            if (i, sub) == (1, 0):
                lands = _split_wait("gather_wait_b", state_b, xs)
                w_in8[2], w_out4[2] = lands[0], lands[1].reshape(4, FF_BLK, d)
                mla_wts = mla_weights(*lands[2:])
            if (i, sub) == (1, 2):
                lands = _split_wait("gather_wait_c", state_c, xs)
                w_in8[3], w_out4[3] = lands[0], lands[1].reshape(4, FF_BLK, d)
            shift, scale, gate = (_row_of(mod[i, 3 * sub + j]) for j in range(3))
            g_pre, g_post = _row_of(g_all[i, 2 * sub]), _row_of(g_all[i, 2 * sub + 1])
            tag = f"{i}{sub}"
            if sub != 1:
                f = sub // 2
                h = _prenorm(f"prenorm_{tag}", xs, g_pre, scale, shift, BF16, tr)
                u, extra = _ffn_fwd(tag, h, w_in8[2 * i + f], w_out4[2 * i + f], tf, tf // 2)
                weight = 0.5
            elif i == 0:
                h = _prenorm(f"prenorm_{tag}", xs, g_pre, scale, shift, F32, tr)
                u, z, v = _pool_fwd(h, w4, pool_b_all, pscale_row, tm)
                extra = (z, v)
                weight = 1.0
            else:
                h = _prenorm(f"prenorm_{tag}", xs, g_pre, scale, shift, BF16, tr)
                u, extra = _mla_fwd(h, mla_wts, cos_k, sin_k, tm)
                weight = 1.0
            saved.append((xs, h, u, extra, (shift, scale, gate, g_pre, g_post), weight))
            xs = _postnorm(f"postnorm_{tag}", xs, u, g_post, gate, weight, tr)

    def loss_fn(yv, tv):
        e = yv - tv
        return (e * (1.0 / d),), (_colsum(e * e),)

    (dx,), (sq,) = _rowmap("loss", loss_fn, [_tile(xs, tr), _tile(target, tr)], [_otile(s, d, F32, tr)], [_ored(d)], (s // tr,))
    loss = lax.psum(0.5 * jnp.sum(sq) / d, AXES)

    d_mod = [[None] * n_mod for _ in range(2)]
    d_g = [[None] * 6 for _ in range(2)]
    sent = {}
    pool_grads = mla_grads = None

    def start_scatter(key, arrays):
        state, token = _split_start(f"scatter_start_{key}", "scatter", arrays)
        sent[key] = state
        return token

    for i in (1, 0):
        for sub in (2, 1, 0):
            xin, h, u, extra, (shift, scale, gate, g_pre, g_post), weight = saved[3 * i + sub]
            tag = f"{i}{sub}"
            du, (dgate, dgpost) = _postnorm_bwd(f"postnorm_bwd_{tag}", dx, u, g_post, gate, weight, F32 if (sub == 1 and i == 0) else BF16, tr)
            if sub != 1:
                k = 2 * i + sub // 2
                dh, dgu, act = _ffn_bwd_act(tag, du, extra, w_in8[k], w_out4[k], tf, tf // 2)
                dw_out = _ffn_dw(f"ffn_dwout_{tag}", act, du, tf).reshape(N_DEV, FF_BLK // 2, d)
                if k == 0:
                    d_pool_w = pool_grads[0].reshape(4, N_DEV, 32, POOL_GROUP).transpose(1, 0, 2, 3).reshape(N_DEV, 4 * 32, POOL_GROUP)
                    token = start_scatter(tag + "_out", [dw_out, d_pool_w])
                    token = start_scatter(tag + "_in", [_ffn_dw(f"ffn_dwin_{tag}", dgu, h, tf, after=token)])
                else:
                    token = start_scatter(tag, [_ffn_dw(f"ffn_dwin_{tag}", dgu, h, tf), dw_out])
                g_pre = g_pre + token[0, 0]
            elif i == 0:
                dh, dw4, (dpscale, dpb) = _pool_bwd(du, extra[0], extra[1], w4, pscale_row, tm)
                pool_grads = (dw4, dpscale, dpb)
            else:
                dh, mla_grads = _mla_bwd(du, h, extra, mla_wts, cos_k, sin_k, tm)
                dwq = mla_grads["wq"]
                d_uq = jnp.concatenate([dwq[:, :, :QK_NOPE], dwq[:, :, KV_LORA:KV_LORA + QK_ROPE]], axis=2).transpose(1, 0, 2)
                token = start_scatter("mla", [mla_grads["w_in"][:, :mla_w_in.shape[2]].reshape(N_DEV, d // N_DEV, -1),
                                              d_uq.reshape(N_DEV, Q_LORA // N_DEV, -1), mla_grads["w_o"].reshape(N_DEV, d // N_DEV, d)])
                dwukp, dwuv2 = mla_grads["wukp"], mla_grads["wuv2"]
                d_uk = dwukp[:, :QK_NOPE, :KV_LORA].transpose(2, 0, 1).reshape(KV_LORA, -1)
                d_uv = jnp.where(even, dwuv2[:, :, :V_HEAD], dwuv2[:, :, V_HEAD:]).transpose(1, 0, 2).reshape(KV_LORA, -1)
                state_ukv, token_ukv = _split_start("gather_start_ukv", "gather", [d_uk, d_uv], after=token)
                g_pre = g_pre + token_ukv[0, 0]
            dx, (dshift, dscale, dgpre) = _prenorm_bwd(f"prenorm_bwd_{tag}", dh, xin, dx, g_pre, scale, tr)
            d_mod[i][3 * sub:3 * sub + 3] = [dshift, dscale, dgate]
            d_g[i][2 * sub:2 * sub + 2] = [dgpre, dgpost]
    grad_x = dx.reshape(x.shape)

    def upd(name, parts, w, m, v):
        shape = w.shape
        r, cdim = parts.shape[1], parts.shape[2]
        return [o.reshape(shape) for o in _adamw(name, parts, w.reshape(r, cdim), m.reshape(r, cdim), v.reshape(r, cdim))]

    def landed(key, after):
        return _split_wait(f"scatter_wait_{key}", sent[key], after)

    res = {}
    w_in_s, m_in_s, v_in_s = (a.reshape(4, FF_BLK, d) for a in (w_in_t, m_in_t, v_in_t))
    w_out_s, m_out_s, v_out_s = (a.reshape(4, FF_BLK // 2, d) for a in (ffn_w_out, m_ffn_w_out, v_ffn_w_out))
    bufs_in = [lax.empty(w_in_s.shape, F32) for _ in range(4)]
    bufs_out = [lax.empty(w_out_s.shape, F32) for _ in range(4)]
    for key, k in (("12", 3), ("mla", None), ("10", 2), ("02", 1)):
        parts = landed(key, grad_x)
        if k is None:
            res["mla_w_in"] = upd("adam_mla_w_in", parts[0], mla_w_in, m_mla_w_in, v_mla_w_in)
            res["mla_w_uq"] = upd("adam_mla_w_uq", parts[1], mla_w_uq, m_mla_w_uq, v_mla_w_uq)
            res["mla_w_o"] = upd("adam_mla_w_o", parts[2], mla_w_o, m_mla_w_o, v_mla_w_o)
            uk_all, uv_all = _split_wait("gather_wait_ukv", state_ukv, grad_x)
            res["mla_w_uk"] = upd("adam_mla_w_uk", uk_all, mla_w_uk, m_mla_w_uk, v_mla_w_uk)
            res["mla_w_uv"] = upd("adam_mla_w_uv", uv_all, mla_w_uv, m_mla_w_uv, v_mla_w_uv)
            continue
        bufs_in = _adamw_slab(f"adam_ffn_w_in_{key}", parts[0], w_in_s, m_in_s, v_in_s, bufs_in, k)
        bufs_out = _adamw_slab(f"adam_ffn_w_out_{key}", parts[1], w_out_s, m_out_s, v_out_s, bufs_out, k)

    dw4, dpscale, dpb = pool_grads
    d_mod_row = jnp.concatenate([jnp.concatenate(r, axis=1) for r in d_mod], axis=1)
    d_g_row = jnp.concatenate([jnp.concatenate(r, axis=1) for r in d_g], axis=1)
    small_g = jnp.concatenate([d_mod_row, d_g_row, dpb, dpscale, mla_grads["q_norm"], mla_grads["kv_norm"]], axis=1)
    done = [bufs_in[0], bufs_out[0], res["mla_w_o"][0], res["mla_w_uv"][0]]
    (small_g_all,) = _all_gather("gather_small_grads", [small_g], after=sum(a.reshape(-1)[:1] for a in done))
    small_g_all = small_g_all.reshape(N_DEV, -1)
    n_m = 2 * n_mod * d
    d_mod_all = small_g_all[:, :n_m].reshape(N_DEV, 2, n_mod * d)
    rest = small_g_all[:, n_m:]
    p_norm_g = lax.dynamic_slice_in_dim(rest[:, :12 * d].reshape(N_DEV, 12, d), me * (d // N_DEV), d // N_DEV, axis=2)
    p_pool_b = lax.dynamic_slice_in_dim(rest[:, 12 * d:13 * d].reshape(N_DEV, 4, POOL_GROUP), me * 32, 32, axis=2)
    p_pool_scale = rest[:, 13 * d:14 * d].reshape(N_DEV, 1, d)
    p_q_norm = lax.dynamic_slice_in_dim(rest[:, 14 * d:14 * d + Q_LORA], me * 32, 32, axis=1).reshape(N_DEV, 1, 32)
    p_kv_norm = rest[:, 14 * d + Q_LORA:].reshape(N_DEV, 1, KV_LORA)

    d_mod_loc = lax.dynamic_slice_in_dim(d_mod_all, me * mod_blk, mod_blk, axis=2).transpose(1, 0, 2)
    k_pad = 128
    sc_t = jnp.pad(sc_all.T, ((0, 0), (0, k_pad - N_DEV)))
    d_ada_w = _matmul("ada_dw", sc_t, jnp.pad(d_mod_loc, ((0, 0), (0, k_pad - N_DEV), (0, 0))), a_blk=(d, k_pad),
                      a_map=lambda i, k: (0, 0), b_blk=(None, k_pad, mod_blk), b_map=lambda i, k: (i, 0, 0), o_shape=(2, d, mod_blk),
                      o_blk=(None, d, mod_blk), o_map=lambda i, k: (i, 0, 0), grid=(2, 1), contract=NN, out_dtype=F32)
    res["ada_w"] = upd("adam_ada_w", d_ada_w.reshape(1, 2 * d, mod_blk), ada_w, m_ada_w, v_ada_w)
    res["ada_b"] = upd("adam_ada_b", d_mod_all.reshape(N_DEV, 2, n_mod * d), ada_b, m_ada_b, v_ada_b)
    res["norm_g"] = upd("adam_norm_g", p_norm_g, norm_g, m_norm_g, v_norm_g)
    res["pool_b"] = upd("adam_pool_b", p_pool_b, pool_b, m_pool_b, v_pool_b)
    res["pool_scale"] = upd("adam_pool_scale", p_pool_scale, pool_scale, m_pool_scale, v_pool_scale)
    res["mla_q_norm"] = upd("adam_mla_q_norm", p_q_norm, mla_q_norm, m_mla_q_norm, v_mla_q_norm)
    res["mla_kv_norm"] = upd("adam_mla_kv_norm", p_kv_norm, mla_kv_norm, m_mla_kv_norm, v_mla_kv_norm)

    p_out, p_pool_w = landed("00_out", res["ada_w"][1])
    bufs_out = _adamw_slab("adam_ffn_w_out_00", p_out, w_out_s, m_out_s, v_out_s, bufs_out, 0)
    res["pool_w"] = upd("adam_pool_w", p_pool_w, pool_w, m_pool_w, v_pool_w)
    (p_in,) = landed("00_in", res["pool_w"][1])
    bufs_in = _adamw_slab("adam_ffn_w_in_00", p_in, w_in_s, m_in_s, v_in_s, bufs_in, 0)
    res["ffn_w_in"] = [jnp.swapaxes(b.reshape(w_in_t.shape), 2, 3) for b in bufs_in]
    res["ffn_w_out"] = [b.reshape(ffn_w_out.shape) for b in bufs_out]

    order = ["ada_w", "ada_b", "norm_g", "ffn_w_in", "ffn_w_out", "pool_w", "pool_b", "pool_scale", "mla_w_in", "mla_q_norm",
             "mla_kv_norm", "mla_w_uq", "mla_w_uk", "mla_w_uv", "mla_w_o"]
    outs = [loss, grad_x]
    for j in range(4):
        outs += [res[name][j] for name in order]
    return tuple(outs)
```

```python
import functools

import jax
import jax.numpy as jnp
from jax import lax
from jax.experimental import pallas as pl
from jax.experimental.pallas import tpu as pltpu

F32 = jnp.float32
BF16 = jnp.bfloat16
N_DEV = 8
AXES = ("x", "y", "c")
MESH = pl.DeviceIdType.MESH

D_MODEL = 1024
N_HEADS = 16
QK_NOPE = 64
QK_ROPE = 32
V_HEAD = 64
Q_LORA = 256
KV_LORA = 128
LAT_PAD = 512
QK_PAD = 256
D_FF = 2816
FF_BLK = 2 * D_FF // N_DEV
POOL_WINDOWS = (2, 4, 8, 16)
POOL_GROUP = 256
ROPE_THETA = 10000.0
EPS = 1e-6
ATTN_SCALE = (QK_NOPE + QK_ROPE) ** -0.5
LOG2_E = 1.4426950408889634
ADAM_LR, ADAM_B1, ADAM_B2, ADAM_EPS, ADAM_WD, ADAM_STEP = 0.001, 0.9, 0.999, 1e-08, 0.01, 10
VMEM_LIMIT = 56 * 1024 * 1024

NN = ((1,), (0,))
NT = ((1,), (1,))
TN = ((0,), (0,))


def _params(**kw):
    return pltpu.CompilerParams(vmem_limit_bytes=VMEM_LIMIT, **kw)


def _dot(a, b, contract):
    return lax.dot_general(a, b, (contract, ((), ())), preferred_element_type=F32)


def _matmul(name, a, b, *, a_blk, a_map, b_blk, b_map, o_shape, o_blk, o_map, grid, contract, out_dtype,
            bias=None, bias_blk=None, bias_map=None, after=None):
    n_k = grid[-1]
    k_axis = len(grid) - 1
    acc_shape = tuple(d for d in o_blk if d is not None)

    def body(*refs):
        a_ref, b_ref = refs[:2]
        bias_ref = refs[2] if bias is not None else None
        if n_k == 1:
            r = _dot(a_ref[...].astype(BF16), b_ref[...].astype(BF16), contract)
            if bias is not None:
                r = r + bias_ref[...]
            refs[-1][...] = r.astype(refs[-1].dtype)
            return
        o_ref, acc = refs[-2:]
        k = pl.program_id(k_axis)

        @pl.when(k == 0)
        def _():
            acc[...] = jnp.zeros_like(acc)

        acc[...] += _dot(a_ref[...].astype(BF16), b_ref[...].astype(BF16), contract)

        @pl.when(k == n_k - 1)
        def _():
            r = acc[...]
            if bias is not None:
                r = r + bias_ref[...]
            o_ref[...] = r.astype(o_ref.dtype)

    in_specs = [pl.BlockSpec(a_blk, a_map), pl.BlockSpec(b_blk, b_map)]
    args = [a, b]
    if bias is not None:
        in_specs.append(pl.BlockSpec(bias_blk, bias_map))
        args.append(bias)
    if after is not None:
        in_specs.append(pl.BlockSpec(memory_space=pl.ANY))
        args.append(after)
    return pl.pallas_call(
        body, name=name, grid=grid, in_specs=in_specs, out_specs=pl.BlockSpec(o_blk, o_map),
        out_shape=jax.ShapeDtypeStruct(o_shape, out_dtype), scratch_shapes=[pltpu.VMEM(acc_shape, F32)] if n_k > 1 else [],
        compiler_params=_params(),
    )(*args)


def _rowmap(name, fn, ins, outs, reds, grid):
    n_in, n_out, n_red = len(ins), len(outs), len(reds)

    def body(*refs):
        in_refs = refs[:n_in]
        out_refs = refs[n_in:n_in + n_out]
        red_refs = refs[n_in + n_out:]
        out_vals, red_vals = fn(*[r[...] for r in in_refs])
        for r, v in zip(out_refs, out_vals):
            r[...] = v.astype(r.dtype)
        if n_red:
            first = pl.program_id(0) == 0
            for ax in range(1, len(grid)):
                first = jnp.logical_and(first, pl.program_id(ax) == 0)

            @pl.when(first)
            def _():
                for r in red_refs:
                    r[...] = jnp.zeros_like(r)

            for r, v in zip(red_refs, red_vals):
                r[...] += v

    res = pl.pallas_call(
        body, name=name, grid=grid,
        in_specs=[pl.BlockSpec(blk, imap) for _, blk, imap in ins],
        out_specs=[pl.BlockSpec(blk, imap) for _, blk, imap in list(outs) + list(reds)],
        out_shape=[sds for sds, _, _ in list(outs) + list(reds)],
        compiler_params=_params(),
    )(*[a for a, _, _ in ins])
    return res[:n_out], res[n_out:]


def _sds(shape, dtype):
    return jax.ShapeDtypeStruct(shape, dtype)


def _tile(a, tm):
    return (a, (tm, a.shape[1]), lambda i: (i, 0))


def _row(a):
    return (a, (1, a.shape[1]), lambda i: (0, 0))


def _otile(n, c, dtype, tm):
    return (_sds((n, c), dtype), (tm, c), lambda i: (i, 0))


def _ored(c):
    return (_sds((1, c), F32), (1, c), lambda i: (0, 0))


def _colsum(v):
    return jnp.sum(v, axis=0, keepdims=True)


def _rstd(v):
    return lax.rsqrt(jnp.mean(v * v, axis=-1, keepdims=True) + EPS)


def _prenorm(name, x, g_pre, scale, shift, out_dtype, tm):
    n, d = x.shape

    def fn(xv, g, sc, sh):
        return (xv * _rstd(xv) * g * (1.0 + sc) + sh,), ()

    (h,), _ = _rowmap(name, fn, [_tile(x, tm), _row(g_pre), _row(scale), _row(shift)], [_otile(n, d, out_dtype, tm)], [],
                      (n // tm,))
    return h


def _postnorm(name, x, u, g_post, gate, weight, tm):
    n, d = x.shape

    def fn(xv, uv, g, gt):
        return (xv + weight * (1.0 + gt) * (uv * _rstd(uv) * g),), ()

    (y,), _ = _rowmap(name, fn, [_tile(x, tm), _tile(u, tm), _row(g_post), _row(gate)], [_otile(n, d, F32, tm)], [], (n // tm,))
    return y


def _postnorm_bwd(name, dout, u, g_post, gate, weight, out_dtype, tm):
    n, d = u.shape

    def fn(dv, uv, g, gt):
        r = _rstd(uv)
        un = uv * r
        dy = dv * (weight * (1.0 + gt))
        a = dy * g
        du = r * (a - un * jnp.mean(a * un, axis=-1, keepdims=True))
        return (du,), (_colsum(dv * (weight * (un * g))), _colsum(dy * un))

    (du,), reds = _rowmap(name, fn, [_tile(dout, tm), _tile(u, tm), _row(g_post), _row(gate)], [_otile(n, d, out_dtype, tm)],
                          [_ored(d), _ored(d)], (n // tm,))
    return du, reds


def _prenorm_bwd(name, dh, x, dout, g_pre, scale, tm):
    n, d = x.shape

    def fn(dhv, xv, dv, g, sc):
        dhv = dhv.astype(F32)
        r = _rstd(xv)
        xn = xv * r
        b = dhv * (g * (1.0 + sc))
        dx = dv + r * (b - xn * jnp.mean(b * xn, axis=-1, keepdims=True))
        return (dx,), (_colsum(dhv), _colsum(dhv * (xn * g)), _colsum(dhv * ((1.0 + sc) * xn)))

    (dx,), reds = _rowmap(name, fn, [_tile(dh, tm), _tile(x, tm), _tile(dout, tm), _row(g_pre), _row(scale)],
                          [_otile(n, d, F32, tm)], [_ored(d), _ored(d), _ored(d)], (n // tm,))
    return dx, reds


def _ffn_fwd(tag, h, w_in8, w_out4, tm, sub):
    s, d = h.shape

    def body(h_ref, wg_ref, wu_ref, wo_ref, u_ref, gu_ref):
        @pl.when(pl.program_id(1) == 0)
        def _():
            u_ref[...] = jnp.zeros_like(u_ref)

        for r in range(tm // sub):
            rows = pl.ds(r * sub, sub)
            hv = h_ref[rows, :]
            gate = _dot(hv, wg_ref[...], NT)
            up = _dot(hv, wu_ref[...], NT)
            gu_ref[0, rows, :] = gate.astype(BF16)
            gu_ref[1, rows, :] = up.astype(BF16)
            u_ref[rows, :] += _dot((gate * jax.nn.sigmoid(gate) * up).astype(BF16), wo_ref[...], NN)

    w_blk = (None, FF_BLK, d)
    return pl.pallas_call(
        body, name=f"ffn_fwd_{tag}", grid=(s // tm, 4),
        in_specs=[pl.BlockSpec((tm, d), lambda i, j: (i, 0)), pl.BlockSpec(w_blk, lambda i, j: (j, 0, 0)),
                  pl.BlockSpec(w_blk, lambda i, j: (j + 4, 0, 0)), pl.BlockSpec((None, FF_BLK, d), lambda i, j: (j, 0, 0))],
        out_specs=[pl.BlockSpec((tm, d), lambda i, j: (i, 0)), pl.BlockSpec((2, None, tm, FF_BLK), lambda i, j: (0, j, i, 0))],
        out_shape=[_sds((s, d), F32), _sds((2, 4, s, FF_BLK), BF16)], compiler_params=_params(),
    )(h, w_in8, w_in8, w_out4)


def _ffn_bwd_act(tag, du, gu, w_in8, w_out4, tm, sub):
    s, d = du.shape

    def body(du_ref, gu_ref, wg_ref, wu_ref, wo_ref, dh_ref, dgu_ref, act_ref):
        @pl.when(pl.program_id(1) == 0)
        def _():
            dh_ref[...] = jnp.zeros_like(dh_ref)

        for r in range(tm // sub):
            rows = pl.ds(r * sub, sub)
            dact = _dot(du_ref[rows, :], wo_ref[...], NT)
            gate, up = gu_ref[0, rows, :].astype(F32), gu_ref[1, rows, :].astype(F32)
            sg = jax.nn.sigmoid(gate)
            silu = gate * sg
            dg = (dact * up * (sg * (1.0 + gate * (1.0 - sg)))).astype(BF16)
            dup = (dact * silu).astype(BF16)
            dgu_ref[0, :, rows] = dg.T
            dgu_ref[1, :, rows] = dup.T
            act_ref[:, rows] = (silu * up).astype(BF16).T
            dh_ref[rows, :] += _dot(dg, wg_ref[...], NN) + _dot(dup, wu_ref[...], NN)

    w_blk = (None, FF_BLK, d)
    dh, dgu_t, act_t = pl.pallas_call(
        body, name=f"ffn_bwd_{tag}", grid=(s // tm, 4),
        in_specs=[pl.BlockSpec((tm, d), lambda i, j: (i, 0)), pl.BlockSpec((2, None, tm, FF_BLK), lambda i, j: (0, j, i, 0)),
                  pl.BlockSpec(w_blk, lambda i, j: (j, 0, 0)), pl.BlockSpec(w_blk, lambda i, j: (j + 4, 0, 0)),
                  pl.BlockSpec((None, FF_BLK, d), lambda i, j: (j, 0, 0))],
        out_specs=[pl.BlockSpec((tm, d), lambda i, j: (i, 0)), pl.BlockSpec((2, None, FF_BLK, tm), lambda i, j: (0, j, 0, i)),
                   pl.BlockSpec((None, FF_BLK, tm), lambda i, j: (j, 0, i))],
        out_shape=[_sds((s, d), F32), _sds((2, 4, FF_BLK, s), BF16), _sds((4, FF_BLK, s), BF16)], compiler_params=_params(),
    )(du, gu, w_in8, w_in8, w_out4)
    return dh, dgu_t.reshape(8, FF_BLK, s), act_t


def _ffn_dw(name, lhs_t, rhs, tk, after=None):
    n_g, _, s = lhs_t.shape
    d = rhs.shape[1]
    return _matmul(name, lhs_t, rhs, a_blk=(None, FF_BLK, tk), a_map=lambda g, k: (g, 0, k), b_blk=(tk, d), b_map=lambda g, k: (k, 0),
                   o_shape=(n_g, FF_BLK, d), o_blk=(None, FF_BLK, d), o_map=lambda g, k: (g, 0, 0), grid=(n_g, s // tk),
                   contract=NN, out_dtype=BF16, after=after)


def _window_sum(x, window, transpose):
    s = x.shape[0]
    t = lax.broadcasted_iota(jnp.int32, (s, 1), 0)
    half = window // 2
    cnt = jnp.minimum(t + half, s) - jnp.maximum(t - half, 0)
    inv = 1.0 / cnt.astype(F32)
    if transpose:
        x = x * inv
        offsets = range(-half + 1, half + 1)
    else:
        offsets = range(-half, half)
    acc = jnp.zeros_like(x)
    for o in offsets:
        shifted = x if o == 0 else pltpu.roll(x, (-o) % s, 0)
        valid = jnp.logical_and(t + o >= 0, t + o < s)
        acc = acc + jnp.where(valid, shifted, 0.0)
    return acc if transpose else acc * inv


def _pool_mix(name, x, transpose, out_dtype):
    s, d = x.shape

    def body(x_ref, o_ref):
        g = pl.program_id(0)
        for gi, window in enumerate(POOL_WINDOWS):
            @pl.when(g == gi)
            def _(window=window):
                xv = x_ref[...].astype(F32)
                o_ref[...] = (_window_sum(xv, window, transpose) - xv).astype(o_ref.dtype)

    return pl.pallas_call(
        body, name=name, grid=(len(POOL_WINDOWS),), in_specs=[pl.BlockSpec((s, POOL_GROUP), lambda g: (0, g))],
        out_specs=pl.BlockSpec((s, POOL_GROUP), lambda g: (0, g)), out_shape=_sds((s, d), out_dtype), compiler_params=_params(),
    )(x)


def _pool_fwd(h, w4, bias, pscale, tm):
    s, d = h.shape
    nt = s // tm
    z = _pool_mix("pool_mix", h, False, BF16)
    v = _matmul("pool_proj", z, w4, a_blk=(tm, POOL_GROUP), a_map=lambda i, g, k: (i, g), b_blk=(None, POOL_GROUP, POOL_GROUP),
                b_map=lambda i, g, k: (g, 0, 0), o_shape=(s, d), o_blk=(tm, POOL_GROUP), o_map=lambda i, g, k: (i, g),
                grid=(nt, 4, 1), contract=NN, out_dtype=F32, bias=bias, bias_blk=(1, POOL_GROUP), bias_map=lambda i, g, k: (0, g))
    tr = min(512, s)
    (u,), _ = _rowmap("pool_scale", lambda vv, ps: ((vv * ps,), ()), [_tile(v, tr), _row(pscale)], [_otile(s, d, F32, tr)], [],
                      (s // tr,))
    return u, z, v


def _pool_bwd(du, z, v, w4, pscale, tm):
    s, d = du.shape
    nt = s // tm

    def fn(duv, vv, ps):
        dv = duv * ps
        return (dv,), (_colsum(duv * vv), _colsum(dv))

    tr = min(512, s)
    (dv,), reds = _rowmap("pool_dscale", fn, [_tile(du, tr), _tile(v, tr), _row(pscale)], [_otile(s, d, BF16, tr)],
                          [_ored(d), _ored(d)], (s // tr,))
    dw4 = _matmul("pool_dw", z, dv, a_blk=(tm, POOL_GROUP), a_map=lambda g, k: (k, g), b_blk=(tm, POOL_GROUP),
                  b_map=lambda g, k: (k, g), o_shape=(4, POOL_GROUP, POOL_GROUP), o_blk=(None, POOL_GROUP, POOL_GROUP),
                  o_map=lambda g, k: (g, 0, 0), grid=(4, nt), contract=TN, out_dtype=F32)
    dz = _matmul("pool_dz", dv, w4, a_blk=(tm, POOL_GROUP), a_map=lambda i, g, k: (i, g), b_blk=(None, POOL_GROUP, POOL_GROUP),
                 b_map=lambda i, g, k: (g, 0, 0), o_shape=(s, d), o_blk=(tm, POOL_GROUP), o_map=lambda i, g, k: (i, g),
                 grid=(nt, 4, 1), contract=NT, out_dtype=F32)
    dh = _pool_mix("pool_mix_t", dz, True, F32)
    return dh, dw4, reds


def _lane(shape):
    return lax.broadcasted_iota(jnp.int32, shape, 1)


def _rope_swap(v, transpose):
    half = QK_ROPE // 2
    lane = _lane(v.shape)
    up = pltpu.roll(v, v.shape[1] - half, 1)
    down = pltpu.roll(v, half, 1)
    if transpose:
        return jnp.where(lane < half, up, jnp.where(lane < QK_ROPE, -down, 0.0))
    return jnp.where(lane < half, -up, jnp.where(lane < QK_ROPE, down, 0.0))


def _rope(v, cos, sin):
    return v * cos + _rope_swap(v, False) * sin


def _rope_t(g, cos, sin):
    return g * cos + _rope_swap(g * sin, True)


def _mla_mid(lat, q_norm, kv_norm, cos_k, sin_k, tm):
    s = lat.shape[0]

    def fn(lv, qn, kn, cs, sn):
        cq = lv[:, :Q_LORA]
        ckv = lv[:, Q_LORA:Q_LORA + KV_LORA]
        kr = lv[:, Q_LORA + KV_LORA:]
        cq = cq * _rstd(cq) * qn
        ckv = ckv * _rstd(ckv) * kn
        return (cq, jnp.concatenate([ckv, _rope(kr, cs, sn)], axis=1)), ()

    (cq, kcat), _ = _rowmap("mla_mid", fn, [_tile(lat, tm), _row(q_norm), _row(kv_norm), _tile(cos_k, tm), _tile(sin_k, tm)],
                            [_otile(s, Q_LORA, BF16, tm), _otile(s, QK_PAD, BF16, tm)], [], (s // tm,))
    return cq, kcat


def _mla_mid_bwd(lat, dcq, dkcat, dv, q_norm, kv_norm, cos_k, sin_k, tm):
    s = lat.shape[0]

    def fn(lv, dq, dk, dvv, qn, kn, cs, sn):
        dk = dk * (1.0 / LOG2_E)
        cq = lv[:, :Q_LORA]
        ckv = lv[:, Q_LORA:Q_LORA + KV_LORA]
        rq, rk = _rstd(cq), _rstd(ckv)
        cqn, ckn = cq * rq, ckv * rk
        a = dq * qn
        d_cq = rq * (a - cqn * jnp.mean(a * cqn, axis=-1, keepdims=True))
        dckv = dk[:, :KV_LORA] + dvv
        a2 = dckv * kn
        d_ckv = rk * (a2 - ckn * jnp.mean(a2 * ckn, axis=-1, keepdims=True))
        d_kr = _rope_t(dk[:, KV_LORA:], cs, sn)
        return (jnp.concatenate([d_cq, d_ckv, d_kr], axis=1),), (_colsum(dq * cqn), _colsum(dckv * ckn))

    (dlat,), reds = _rowmap(
        "mla_mid_bwd", fn,
        [_tile(lat, tm), _tile(dcq, tm), _tile(dkcat, tm), _tile(dv, tm), _row(q_norm), _row(kv_norm), _tile(cos_k, tm),
         _tile(sin_k, tm)],
        [_otile(s, LAT_PAD, BF16, tm)], [_ored(Q_LORA), _ored(KV_LORA)], (s // tm,))
    return dlat, reds


def _mla_q(cq, wq, wukp, cos_k, sin_k, tm):
    s = cq.shape[0]

    def body(cq_ref, wq_ref, wuk_ref, cos_ref, sin_ref, o_ref):
        cqv, cs, sn = cq_ref[...], cos_ref[...], sin_ref[...]
        for h in range(N_HEADS):
            aq = _dot(cqv, wq_ref[h], NN)
            qlat = _dot(aq.astype(BF16), wuk_ref[h], NN)
            roped = _rope(aq[:, KV_LORA:], cs, sn)
            o_ref[h] = (jnp.concatenate([qlat[:, :KV_LORA], roped], axis=1) * (ATTN_SCALE * LOG2_E)).astype(o_ref.dtype)

    wblk = pl.BlockSpec((N_HEADS, QK_PAD, QK_PAD), lambda i: (0, 0, 0))
    tblk = pl.BlockSpec((tm, KV_LORA), lambda i: (i, 0))
    return pl.pallas_call(
        body, name="mla_q", grid=(s // tm,),
        in_specs=[pl.BlockSpec((tm, Q_LORA), lambda i: (i, 0)), wblk, wblk, tblk, tblk],
        out_specs=pl.BlockSpec((N_HEADS, tm, QK_PAD), lambda i: (0, i, 0)), out_shape=_sds((N_HEADS, s, QK_PAD), BF16),
        compiler_params=_params(),
    )(cq, wq, wukp, cos_k, sin_k)


def _mla_q_bwd(cq, wq, wukp, cos_k, sin_k, dqcat, tm):
    s = cq.shape[0]

    def body(cq_ref, wq_ref, wuk_ref, cos_ref, sin_ref, dq_ref, dcq_ref, dwq_ref, dwuk_ref):
        @pl.when(pl.program_id(0) == 0)
        def _():
            dwq_ref[...] = jnp.zeros_like(dwq_ref)
            dwuk_ref[...] = jnp.zeros_like(dwuk_ref)

        cqv, cs, sn = cq_ref[...], cos_ref[...], sin_ref[...]
        d_cq = jnp.zeros((tm, Q_LORA), F32)
        for h in range(N_HEADS):
            aq = _dot(cqv, wq_ref[h], NN).astype(BF16)
            g = dq_ref[h].astype(F32) * ATTN_SCALE
            gl, gr = g[:, :KV_LORA], g[:, KV_LORA:]
            dqlat = jnp.concatenate([gl, jnp.zeros_like(gl)], axis=1).astype(BF16)
            d_rope = _rope_t(gr, cs, sn)
            daq = _dot(dqlat, wuk_ref[h], NT) + jnp.concatenate([jnp.zeros_like(d_rope), d_rope], axis=1)
            daq_b = daq.astype(BF16)
            dwuk_ref[h] += _dot(aq, dqlat, TN)
            dwq_ref[h] += _dot(cqv, daq_b, TN)
            d_cq = d_cq + _dot(daq_b, wq_ref[h], NT)
        dcq_ref[...] = d_cq

    wblk = pl.BlockSpec((N_HEADS, QK_PAD, QK_PAD), lambda i: (0, 0, 0))
    tblk = pl.BlockSpec((tm, KV_LORA), lambda i: (i, 0))
    return pl.pallas_call(
        body, name="mla_q_bwd", grid=(s // tm,),
        in_specs=[pl.BlockSpec((tm, Q_LORA), lambda i: (i, 0)), wblk, wblk, tblk, tblk,
                  pl.BlockSpec((N_HEADS, tm, QK_PAD), lambda i: (0, i, 0))],
        out_specs=[pl.BlockSpec((tm, Q_LORA), lambda i: (i, 0)), wblk, wblk],
        out_shape=[_sds((s, Q_LORA), F32), _sds((N_HEADS, QK_PAD, QK_PAD), F32), _sds((N_HEADS, QK_PAD, QK_PAD), F32)],
        compiler_params=_params(),
    )(cq, wq, wukp, cos_k, sin_k, dqcat)


def _flash_fwd(qcat, kcat, tq, tk):
    n_h, s, _ = qcat.shape
    n_k = s // tk

    def body(q_ref, k_ref, v_ref, o_ref, lse_ref):
        q = q_ref[...]
        m = jnp.full((tq, 1), -1e30, F32)
        l = jnp.zeros((tq, 1), F32)
        acc = jnp.zeros((tq, KV_LORA), F32)
        for kk in range(n_k):
            rows = pl.ds(kk * tk, tk)
            sc = _dot(q, k_ref[rows, :], NT)
            m_new = jnp.maximum(m, jnp.max(sc, axis=1, keepdims=True))
            alpha = jnp.exp2(m - m_new)
            p = jnp.exp2(sc - m_new)
            l = alpha * l + jnp.sum(p, axis=1, keepdims=True)
            acc = alpha * acc + _dot(p.astype(BF16), v_ref[rows, :], NN)
            m = m_new
        o_ref[...] = (acc / l).astype(o_ref.dtype)
        lse_ref[...] = m + jnp.log2(l)

    return pl.pallas_call(
        body, name="mla_attn", grid=(n_h, s // tq),
        in_specs=[pl.BlockSpec((None, tq, QK_PAD), lambda h, i: (h, i, 0)), pl.BlockSpec((s, QK_PAD), lambda h, i: (0, 0)),
                  pl.BlockSpec((s, KV_LORA), lambda h, i: (0, 0))],
        out_specs=[pl.BlockSpec((None, tq, KV_LORA), lambda h, i: (h, i, 0)), pl.BlockSpec((None, tq, 1), lambda h, i: (h, i, 0))],
        out_shape=[_sds((n_h, s, KV_LORA), BF16), _sds((n_h, s, 1), F32)], compiler_params=_params(),
    )(qcat, kcat, kcat)


def _flash_bwd(qcat, kcat, o, do, lse, tq, tk):
    n_h, s, _ = qcat.shape
    n_k = s // tk

    def body(q_ref, k_ref, v_ref, o_ref, do_ref, lse_ref, dq_ref, dk_ref, dv_ref, dq_acc):
        h, i = pl.program_id(0), pl.program_id(1)

        @pl.when(jnp.logical_and(h == 0, i == 0))
        def _():
            dk_ref[...] = jnp.zeros_like(dk_ref)
            dv_ref[...] = jnp.zeros_like(dv_ref)

        q = q_ref[...]
        dov = do_ref[...]
        lse_v = lse_ref[...]
        delta = jnp.sum(dov.astype(F32) * o_ref[...].astype(F32), axis=1, keepdims=True)
        dq_acc[...] = jnp.zeros_like(dq_acc)

        for kk in range(n_k):
            rows = pl.ds(kk * tk, tk)
            k = k_ref[rows, :]
            p = jnp.exp2(_dot(q, k, NT) - lse_v)
            dp = _dot(dov, v_ref[rows, :], NT)
            ds = (p * (dp - delta)).astype(BF16)
            dq_acc[...] += _dot(ds, k, NN)
            dv_ref[rows, :] += _dot(p.astype(BF16), dov, TN)
            dk_ref[rows, :] += _dot(ds, q, TN)
        dq_ref[...] = dq_acc[...].astype(dq_ref.dtype)

    qblk = pl.BlockSpec((None, tq, QK_PAD), lambda h, i: (h, i, 0))
    oblk = pl.BlockSpec((None, tq, KV_LORA), lambda h, i: (h, i, 0))
    return pl.pallas_call(
        body, name="mla_attn_bwd", grid=(n_h, s // tq),
        in_specs=[qblk, pl.BlockSpec((s, QK_PAD), lambda h, i: (0, 0)), pl.BlockSpec((s, KV_LORA), lambda h, i: (0, 0)), oblk, oblk,
                  pl.BlockSpec((None, tq, 1), lambda h, i: (h, i, 0))],
        out_specs=[qblk, pl.BlockSpec((s, QK_PAD), lambda h, i: (0, 0)), pl.BlockSpec((s, KV_LORA), lambda h, i: (0, 0))],
        out_shape=[_sds((n_h, s, QK_PAD), BF16), _sds((s, QK_PAD), F32), _sds((s, KV_LORA), F32)],
        scratch_shapes=[pltpu.VMEM((tq, QK_PAD), F32)], compiler_params=_params(),
    )(qcat, kcat, kcat, o, do, lse)


def _mla_uv(o_lat, wuv2, do, tm):
    n_h, s, _ = o_lat.shape
    d = n_h * V_HEAD
    pair = 2 * V_HEAD
    lat_blk = pl.BlockSpec((n_h, tm, KV_LORA), lambda i: (0, i, 0))
    w_blk = pl.BlockSpec((n_h, KV_LORA, pair), lambda i: (0, 0, 0))
    row_blk = pl.BlockSpec((tm, d), lambda i: (i, 0))

    if do is None:
        def body(a_ref, w_ref, o_ref):
            for p in range(n_h // 2):
                o_ref[:, p * pair:(p + 1) * pair] = (
                    _dot(a_ref[2 * p], w_ref[2 * p], NN) + _dot(a_ref[2 * p + 1], w_ref[2 * p + 1], NN)).astype(o_ref.dtype)

        return pl.pallas_call(body, name="mla_uv", grid=(s // tm,), in_specs=[lat_blk, w_blk], out_specs=row_blk,
                              out_shape=_sds((s, d), BF16), compiler_params=_params())(o_lat, wuv2)

    def body(a_ref, w_ref, do_ref, dlat_ref, dw_ref):
        @pl.when(pl.program_id(0) == 0)
        def _():
            dw_ref[...] = jnp.zeros_like(dw_ref)

        for h in range(n_h):
            dov = do_ref[:, (h // 2) * pair:(h // 2 + 1) * pair]
            dlat_ref[h] = _dot(dov, w_ref[h], NT).astype(dlat_ref.dtype)
            dw_ref[h] += _dot(a_ref[h], dov, TN)

    return pl.pallas_call(body, name="mla_uv_bwd", grid=(s // tm,), in_specs=[lat_blk, w_blk, row_blk], out_specs=[lat_blk, w_blk],
                          out_shape=[_sds((n_h, s, KV_LORA), BF16), _sds((n_h, KV_LORA, pair), F32)], compiler_params=_params(),
                          )(o_lat, wuv2, do)


def _mla_fwd(h, wts, cos_k, sin_k, tm):
    s, d = h.shape
    nt = s // tm
    lat = _matmul("mla_lat", h, wts["w_in"], a_blk=(tm, d), a_map=lambda i, k: (i, 0), b_blk=(d, LAT_PAD), b_map=lambda i, k: (0, 0),
                  o_shape=(s, LAT_PAD), o_blk=(tm, LAT_PAD), o_map=lambda i, k: (i, 0), grid=(nt, 1), contract=NN, out_dtype=F32)
    cq, kcat = _mla_mid(lat, wts["q_norm"], wts["kv_norm"], cos_k, sin_k, tm)
    qcat = _mla_q(cq, wts["wq"], wts["wukp"], cos_k, sin_k, tm)
    o_lat, lse = _flash_fwd(qcat, kcat, min(2 * tm, s), tm)
    o = _mla_uv(o_lat, wts["wuv2"], None, tm)
    u = _matmul("mla_out", o, wts["w_o"], a_blk=(tm, d), a_map=lambda i, k: (i, 0), b_blk=(d, d), b_map=lambda i, k: (0, 0),
                o_shape=(s, d), o_blk=(tm, d), o_map=lambda i, k: (i, 0), grid=(nt, 1), contract=NN, out_dtype=F32)
    return u, (lat, cq, kcat, qcat, o_lat, lse, o)


def _mla_bwd(du, h, saved, wts, cos_k, sin_k, tm):
    lat, cq, kcat, qcat, o_lat, lse, o = saved
    s, d = h.shape
    nt = s // tm
    do = _matmul("mla_do", du, wts["w_o"], a_blk=(tm, d), a_map=lambda i, k: (i, 0), b_blk=(d, d), b_map=lambda i, k: (0, 0),
                 o_shape=(s, d), o_blk=(tm, d), o_map=lambda i, k: (i, 0), grid=(nt, 1), contract=NT, out_dtype=BF16)
    dw_o = _matmul("mla_dwo", o, du, a_blk=(tm, d), a_map=lambda k: (k, 0), b_blk=(tm, d), b_map=lambda k: (k, 0),
                   o_shape=(d, d), o_blk=(d, d), o_map=lambda k: (0, 0), grid=(nt,), contract=TN, out_dtype=F32)
    do_lat, dwuv2 = _mla_uv(o_lat, wts["wuv2"], do, tm)
    dqcat, dkcat, dv = _flash_bwd(qcat, kcat, o_lat, do_lat, lse, min(2 * tm, s), tm)
    dcq, dwq, dwukp = _mla_q_bwd(cq, wts["wq"], wts["wukp"], cos_k, sin_k, dqcat, tm)
    dlat, (dqn, dkn) = _mla_mid_bwd(lat, dcq, dkcat, dv, wts["q_norm"], wts["kv_norm"], cos_k, sin_k, tm)
    dh = _matmul("mla_dh", dlat, wts["w_in"], a_blk=(tm, LAT_PAD), a_map=lambda i, k: (i, 0), b_blk=(d, LAT_PAD),
                 b_map=lambda i, k: (0, 0), o_shape=(s, d), o_blk=(tm, d), o_map=lambda i, k: (i, 0), grid=(nt, 1), contract=NT,
                 out_dtype=F32)
    dw_in = _matmul("mla_dwin", h, dlat, a_blk=(tm, d), a_map=lambda k: (k, 0), b_blk=(tm, LAT_PAD), b_map=lambda k: (k, 0),
                    o_shape=(d, LAT_PAD), o_blk=(d, LAT_PAD), o_map=lambda k: (0, 0), grid=(nt,), contract=TN, out_dtype=F32)
    return dh, dict(w_in=dw_in, wq=dwq, wukp=dwukp, wuv2=dwuv2, w_o=dw_o, q_norm=dqn, kv_norm=dkn)


def _adamw(name, parts, w, m, v):
    n_parts, r, c = parts.shape
    tr = r
    for cand in (256, 128, 64, 32, 16, 8):
        if r > cand and r % cand == 0:
            tr = cand
            break

    def body(p_ref, w_ref, m_ref, v_ref, g_ref, d_ref, nm_ref, nv_ref):
        g = p_ref[0].astype(F32)
        for k in range(1, n_parts):
            g = g + p_ref[k].astype(F32)
        nm = ADAM_B1 * m_ref[...] + (1.0 - ADAM_B1) * g
        nv = ADAM_B2 * v_ref[...] + (1.0 - ADAM_B2) * (g * g)
        m_hat = nm / (1.0 - ADAM_B1 ** ADAM_STEP)
        v_hat = nv / (1.0 - ADAM_B2 ** ADAM_STEP)
        g_ref[...] = g
        d_ref[...] = -ADAM_LR * (m_hat / (jnp.sqrt(v_hat) + ADAM_EPS) + ADAM_WD * w_ref[...])
        nm_ref[...] = nm
        nv_ref[...] = nv

    blk = pl.BlockSpec((tr, c), lambda i: (i, 0))
    return pl.pallas_call(
        body, name=name, grid=(r // tr,), in_specs=[pl.BlockSpec((n_parts, tr, c), lambda i: (0, i, 0)), blk, blk, blk],
        out_specs=[blk] * 4, out_shape=[_sds((r, c), F32)] * 4, compiler_params=_params(),
    )(parts, w, m, v)


def _adamw_slab(name, parts, w, m, v, bufs, f):
    n_parts, r, c = parts.shape
    tr = max(t for t in range(8, 257, 8) if r % t == 0)

    def body(p_ref, w_ref, m_ref, v_ref, *rest):
        g_ref, d_ref, nm_ref, nv_ref = rest[4:]
        g = p_ref[0].astype(F32)
        for k in range(1, n_parts):
            g = g + p_ref[k].astype(F32)
        nm = ADAM_B1 * m_ref[...] + (1.0 - ADAM_B1) * g
        nv = ADAM_B2 * v_ref[...] + (1.0 - ADAM_B2) * (g * g)
        m_hat = nm / (1.0 - ADAM_B1 ** ADAM_STEP)
        v_hat = nv / (1.0 - ADAM_B2 ** ADAM_STEP)
        g_ref[...] = g
        d_ref[...] = -ADAM_LR * (m_hat / (jnp.sqrt(v_hat) + ADAM_EPS) + ADAM_WD * w_ref[...])
        nm_ref[...] = nm
        nv_ref[...] = nv

    blk = pl.BlockSpec((None, tr, c), lambda i: (f, i, 0))
    return pl.pallas_call(
        body, name=name, grid=(r // tr,),
        in_specs=[pl.BlockSpec((n_parts, tr, c), lambda i: (0, i, 0)), blk, blk, blk] + [pl.BlockSpec(memory_space=pl.ANY)] * 4,
        out_specs=[blk] * 4, out_shape=[_sds(w.shape, F32)] * 4, input_output_aliases={4 + j: j for j in range(4)},
        compiler_params=_params(),
    )(parts, w, m, v, *bufs)


def _mesh_pos():
    return lax.axis_index("x"), lax.axis_index("y"), lax.axis_index("c")


def _flip(pos, mask):
    return tuple(1 - p if (mask >> (2 - b)) & 1 else p for b, p in enumerate(pos))


def _index(pos):
    return 4 * pos[0] + 2 * pos[1] + pos[2]


def _all_gather(name, xs, after=None):
    n = len(xs)
    extra = [] if after is None else [after]

    def body(*refs):
        x_refs, o_refs = refs[:n], refs[n + len(extra):2 * n + len(extra)]
        send_sems, recv_sems, local_sems = refs[2 * n + len(extra):]
        me = _mesh_pos()
        sibling = _flip(me, 1)
        others = [_flip(me, 4), _flip(me, 2), _flip(me, 6)]

        def copy(k, j, block, to, src=None):
            dst = o_refs[k].at[_index(block)]
            return pltpu.make_async_remote_copy(
                src_ref=dst if src is None else src, dst_ref=dst, send_sem=send_sems.at[k, j], recv_sem=recv_sems.at[k, j],
                device_id=to, device_id_type=MESH)

        local = [pltpu.make_async_copy(x_refs[k], o_refs[k].at[_index(me)], local_sems.at[k]) for k in range(n)]
        for cp in local:
            cp.start()
        first = []
        for k in range(n):
            first.append(copy(k, 0, me, sibling, src=x_refs[k]))
            first += [copy(k, 1 + j, me, other, src=x_refs[k]) for j, other in enumerate(others)]
        for cp in first:
            cp.start()
        passed = []
        for j, other in enumerate(others):
            for k in range(n):
                copy(k, 1 + j, other, me).wait_recv()
                cp = copy(k, 4 + j, other, sibling)
                cp.start()
                passed.append(cp)
        for k in range(n):
            copy(k, 0, sibling, me).wait_recv()
        for j, other in enumerate(others):
            for k in range(n):
                copy(k, 4 + j, _flip(other, 1), me).wait_recv()
        for cp in first + passed:
            cp.wait_send()
        for cp in local:
            cp.wait()

    any_spec = pl.BlockSpec(memory_space=pl.ANY)
    return pl.pallas_call(
        body, name=name, in_specs=[any_spec] * (n + len(extra)), out_specs=[any_spec] * n,
        out_shape=[_sds((N_DEV,) + x.shape, x.dtype) for x in xs],
        scratch_shapes=[pltpu.SemaphoreType.DMA((n, 7)), pltpu.SemaphoreType.DMA((n, 7)), pltpu.SemaphoreType.DMA((n,))],
    )(*xs, *extra)


def _all_to_all(name, groups):
    flat = [(gi, f) for gi, grp in enumerate(groups) for f in range(len(grp))]
    n = len(flat)
    n_groups = len(groups)

    def body(*refs):
        x_refs, o_refs = refs[:n], refs[n:n + n_groups]
        send_sems, recv_sems, local_sems = refs[n + n_groups:]
        me = _mesh_pos()
        local, sends, recvs = [], [], []
        for k, (gi, f) in enumerate(flat):
            local.append(pltpu.make_async_copy(x_refs[k].at[_index(me)], o_refs[gi].at[_index(me), f], local_sems.at[k]))
            for mask in range(1, N_DEV):
                peer = _flip(me, mask)
                sends.append(pltpu.make_async_remote_copy(
                    src_ref=x_refs[k].at[_index(peer)], dst_ref=o_refs[gi].at[_index(me), f], send_sem=send_sems.at[k, mask - 1],
                    recv_sem=recv_sems.at[k, mask - 1], device_id=peer, device_id_type=MESH))
                recvs.append(pltpu.make_async_remote_copy(
                    src_ref=x_refs[k].at[_index(me)], dst_ref=o_refs[gi].at[_index(peer), f], send_sem=send_sems.at[k, mask - 1],
                    recv_sem=recv_sems.at[k, mask - 1], device_id=peer, device_id_type=MESH))
        for cp in local + sends:
            cp.start()
        for cp in recvs:
            cp.wait_recv()
        for cp in sends:
            cp.wait_send()
        for cp in local:
            cp.wait()

    any_spec = pl.BlockSpec(memory_space=pl.ANY)
    return pl.pallas_call(
        body, name=name, in_specs=[any_spec] * n, out_specs=[any_spec] * n_groups,
        out_shape=[_sds((N_DEV, len(grp)) + grp[0].shape[1:], grp[0].dtype) for grp in groups],
        scratch_shapes=[pltpu.SemaphoreType.DMA((n, 7)), pltpu.SemaphoreType.DMA((n, 7)), pltpu.SemaphoreType.DMA((n,))],
    )(*[a for grp in groups for a in grp])


def _split_copies(kind, outgoing, x_refs, land_refs, send_sems, recv_sems):
    me = _mesh_pos()
    copies = []
    for k, (x_ref, land_ref) in enumerate(zip(x_refs, land_refs)):
        for mask in range(1, N_DEV):
            peer = _flip(me, mask)
            sem = k * (N_DEV - 1) + mask - 1
            copies.append(pltpu.make_async_remote_copy(
                src_ref=x_ref if kind == "gather" else x_ref.at[_index(peer)], dst_ref=land_ref.at[_index(me if outgoing else peer)],
                send_sem=send_sems.at[sem], recv_sem=recv_sems.at[sem], device_id=peer, device_id_type=MESH))
    return copies


_HBM_SPEC = pl.BlockSpec(memory_space=pltpu.HBM)
_SEM_SPEC = pl.BlockSpec(memory_space=pltpu.SEMAPHORE)
_EFFECT = pltpu.SideEffectType.DATAFLOW_SIDE_EFFECTING


def _split_start(name, kind, xs, after=None):
    n = len(xs)
    extra = [] if after is None else [after]
    lands = [lax.empty(((N_DEV,) + x.shape) if kind == "gather" else x.shape, x.dtype) for x in xs]

    def body(*refs):
        x_refs, land_refs = refs[:n], refs[n:2 * n]
        send_sems, recv_sems = refs[2 * n + len(extra)], refs[2 * n + len(extra) + 1]
        token = refs[-1]
        for cp in _split_copies(kind, True, x_refs, land_refs, send_sems, recv_sems):
            cp.start()
        token[...] = jnp.zeros_like(token)

    hbm = [pltpu.HBM(a.shape, a.dtype) for a in list(xs) + lands]
    res = pl.pallas_call(
        body, name=name,
        out_shape=[pltpu.SemaphoreType.DMA((n * (N_DEV - 1),)), pltpu.SemaphoreType.DMA((n * (N_DEV - 1),))] + hbm + [_sds((8, 128), F32)],
        in_specs=[_HBM_SPEC] * (2 * n) + [pl.BlockSpec(memory_space=pl.ANY)] * len(extra),
        out_specs=[_SEM_SPEC, _SEM_SPEC] + [_HBM_SPEC] * (2 * n) + [pl.BlockSpec(memory_space=pltpu.VMEM)],
        input_output_aliases={j: 2 + j for j in range(2 * n)}, compiler_params=pltpu.CompilerParams(has_side_effects=_EFFECT),
    )(*[pltpu.with_memory_space_constraint(a, pltpu.HBM) for a in list(xs) + lands], *extra)
    return (kind, n, res[0], res[1], res[2:2 + 2 * n]), res[-1]


def _split_wait(name, state, after):
    kind, n, send_sems_in, recv_sems_in, thru = state

    def body(*refs):
        x_refs, land_refs = refs[:n], refs[n:2 * n]
        send_sems, recv_sems = refs[2 * n], refs[2 * n + 1]
        for cp in _split_copies(kind, True, x_refs, land_refs, send_sems, recv_sems):
            cp.wait_send()
        for cp in _split_copies(kind, False, x_refs, land_refs, send_sems, recv_sems):
            cp.wait_recv()

    res = pl.pallas_call(
        body, name=name, out_shape=[pltpu.HBM(a.shape, a.dtype) for a in thru],
        in_specs=[_HBM_SPEC] * (2 * n) + [_SEM_SPEC, _SEM_SPEC, pl.BlockSpec(memory_space=pl.ANY)], out_specs=[_HBM_SPEC] * (2 * n),
        input_output_aliases={j: j for j in range(2 * n)}, compiler_params=pltpu.CompilerParams(has_side_effects=_EFFECT),
    )(*thru, send_sems_in, recv_sems_in, after)
    me = _index(_mesh_pos())
    out = []
    for x, land in zip(res[:n], res[n:]):
        own = x if kind == "gather" else lax.dynamic_index_in_dim(x, me, 0, keepdims=False)
        out.append(lax.dynamic_update_slice(land, own[None], (me,) + (0,) * own.ndim))
    return out


def _rope_tables(s):
    inv = 1.0 / (ROPE_THETA ** (jnp.arange(0, QK_ROPE, 2, dtype=F32) / QK_ROPE))
    ang = jnp.arange(s, dtype=F32)[:, None] * inv[None, :]
    pad = jnp.zeros((s, KV_LORA - QK_ROPE), F32)
    return (jnp.concatenate([jnp.cos(ang), jnp.cos(ang), pad], axis=1), jnp.concatenate([jnp.sin(ang), jnp.sin(ang), pad], axis=1))


def _row_of(v):
    return v.reshape(1, -1)


def kernel(x, c, ada_w, ada_b, norm_g, ffn_w_in, ffn_w_out, pool_w, pool_b, pool_scale, mla_w_in, mla_q_norm, mla_kv_norm, mla_w_uq, mla_w_uk, mla_w_uv, mla_w_o, loss_target, m_ada_w, m_ada_b, m_norm_g, m_ffn_w_in, m_ffn_w_out, m_pool_w, m_pool_b, m_pool_scale, m_mla_w_in, m_mla_q_norm, m_mla_kv_norm, m_mla_w_uq, m_mla_w_uk, m_mla_w_uv, m_mla_w_o, v_ada_w, v_ada_b, v_norm_g, v_ffn_w_in, v_ffn_w_out, v_pool_w, v_pool_b, v_pool_scale, v_mla_w_in, v_mla_q_norm, v_mla_kv_norm, v_mla_w_uq, v_mla_w_uk, v_mla_w_uv, v_mla_w_o):
    s, d = x.shape[1], x.shape[2]
    tm = min(512, s)
    tr = min(512, s)
    tf = min(1024, s)
    me = 4 * lax.axis_index("x") + 2 * lax.axis_index("y") + lax.axis_index("c")
    x0 = x.reshape(s, d)
    target = loss_target.reshape(s, d)
    n_mod = ada_w.shape[2] * N_DEV // d
    mod_blk = ada_w.shape[2]

    small = jnp.concatenate([c.reshape(-1), norm_g.reshape(-1), pool_b.reshape(-1), mla_q_norm.reshape(-1)]).reshape(1, -1)
    w_in_t, m_in_t, v_in_t = (jnp.swapaxes(a, 2, 3) for a in (ffn_w_in, m_ffn_w_in, v_ffn_w_in))
    w_in_loc = [w_in_t[i, f].astype(BF16) for i in range(2) for f in range(2)]
    w_out_loc = [ffn_w_out[i, f].astype(BF16) for i in range(2) for f in range(2)]
    (small_all,) = _all_gather("gather_small", [small])
    small_all = small_all.reshape(N_DEV, -1)
    c_all = small_all[:, :d]
    off = d
    g_all = small_all[:, off:off + 12 * (d // N_DEV)].reshape(N_DEV, 2, 6, d // N_DEV).transpose(1, 2, 0, 3).reshape(2, 6, d)
    off += 12 * (d // N_DEV)
    pool_b_all = small_all[:, off:off + 4 * 32].reshape(N_DEV, 4, 32).transpose(1, 0, 2).reshape(1, d)
    off += 4 * 32
    q_norm_all = small_all[:, off:off + 32].reshape(1, Q_LORA)
    kv_norm_row = mla_kv_norm.reshape(1, KV_LORA)
    pscale_row = pool_scale.reshape(1, d)

    even = (jnp.arange(N_HEADS) % 2 == 0)[:, None, None]
    cos_k, sin_k = _rope_tables(s)

    def mla_weights(mla_w_in_all, mla_w_uq_all, mla_w_o_all):
        uq = mla_w_uq_all.reshape(Q_LORA, N_HEADS, QK_NOPE + QK_ROPE).transpose(1, 0, 2)
        zq = jnp.zeros((N_HEADS, Q_LORA, QK_NOPE), BF16)
        wq = jnp.concatenate(
            [uq[:, :, :QK_NOPE], zq, uq[:, :, QK_NOPE:], jnp.zeros((N_HEADS, Q_LORA, QK_PAD - KV_LORA - QK_ROPE), BF16)], axis=2)
        wukp = jnp.pad(mla_w_uk[0].transpose(1, 2, 0).astype(BF16), ((0, 0), (0, QK_PAD - QK_NOPE), (0, QK_PAD - KV_LORA)))
        uv = mla_w_uv[0].transpose(1, 0, 2).astype(BF16)
        wuv2 = jnp.where(even, jnp.concatenate([uv, jnp.zeros_like(uv)], axis=2), jnp.concatenate([jnp.zeros_like(uv), uv], axis=2))
        return dict(w_in=jnp.pad(mla_w_in_all.reshape(d, -1), ((0, 0), (0, LAT_PAD - mla_w_in.shape[2]))), wq=wq, wukp=wukp,
                    wuv2=wuv2, w_o=mla_w_o_all.reshape(d, d), q_norm=q_norm_all, kv_norm=kv_norm_row)

    (sc_all,), _ = _rowmap("ada_silu", lambda cv: ((cv * jax.nn.sigmoid(cv),), ()), [(c_all, (N_DEV, d), lambda i: (0, 0))],
                           [(_sds((N_DEV, d), F32), (N_DEV, d), lambda i: (0, 0))], [], (1,))
    ada_b_loc = lax.dynamic_slice_in_dim(ada_b, me * mod_blk, mod_blk, axis=1).reshape(2, 1, mod_blk)
    m_pad = 2 * N_DEV
    modp = _matmul("ada_mod", jnp.pad(sc_all, ((0, m_pad - N_DEV), (0, 0))), ada_w, a_blk=(m_pad, d), a_map=lambda i, k: (0, 0),
                   b_blk=(None, d, mod_blk), b_map=lambda i, k: (i, 0, 0), o_shape=(2, m_pad, mod_blk), o_blk=(None, m_pad, mod_blk),
                   o_map=lambda i, k: (i, 0, 0), grid=(2, 1), contract=NN, out_dtype=F32, bias=ada_b_loc, bias_blk=(None, 1, mod_blk),
                   bias_map=lambda i, k: (i, 0, 0))[:, :N_DEV]
    modp_all, w_in_first, w_out_first = _all_gather("gather_first", [modp.reshape(2 * N_DEV, mod_blk), w_in_loc[0], w_out_loc[0]])
    later_a = [w_in_loc[1], w_out_loc[1], pool_w.reshape(-1, POOL_GROUP).astype(BF16)]
    later_b = [w_in_loc[2], w_out_loc[2], mla_w_in[0].astype(BF16), mla_w_uq.reshape(mla_w_uq.shape[1], -1).astype(BF16),
               mla_w_o[0].astype(BF16)]
    later_c = [w_in_loc[3], w_out_loc[3]]
    state_a, token_a = _split_start("gather_start_a", "gather", later_a, after=modp_all)
    state_b, token_b = _split_start("gather_start_b", "gather", later_b, after=token_a)
    state_c, token_c = _split_start("gather_start_c", "gather", later_c, after=token_b)
    w_in8 = [w_in_first, None, None, None]
    w_out4 = [w_out_first.reshape(4, FF_BLK, d), None, None, None]
    mod = lax.dynamic_index_in_dim(modp_all.reshape(N_DEV, 2, N_DEV, mod_blk), me, axis=2, keepdims=False)
    mod = mod.transpose(1, 0, 2).reshape(2, n_mod, d) + token_c[0, 0]

    saved = []
    xs = x0
    w4 = mla_wts = None
    for i in range(2):
        for sub in range(3):
            if (i, sub) == (0, 1):
                lands = _split_wait("gather_wait_a", state_a, xs)
                w_in8[1], w_out4[1] = lands[0], lands[1].reshape(4, FF_BLK, d)
                w4 = lands[2].reshape(N_DEV, 4, 32, POOL_GROUP).transpose(1, 0, 2, 3).reshape(4, POOL_GROUP, POOL_GROUP)
            if (i, sub) == (1, 0):
                lands = _split_wait("gather_wait_b", state_b, xs)
                w_in8[2], w_out4[2] = lands[0], lands[1].reshape(4, FF_BLK, d)
                mla_wts = mla_weights(*lands[2:])
            if (i, sub) == (1, 2):
                lands = _split_wait("gather_wait_c", state_c, xs)
                w_in8[3], w_out4[3] = lands[0], lands[1].reshape(4, FF_BLK, d)
            shift, scale, gate = (_row_of(mod[i, 3 * sub + j]) for j in range(3))
            g_pre, g_post = _row_of(g_all[i, 2 * sub]), _row_of(g_all[i, 2 * sub + 1])
            tag = f"{i}{sub}"
            if sub != 1:
                f = sub // 2
                h = _prenorm(f"prenorm_{tag}", xs, g_pre, scale, shift, BF16, tr)
                u, extra = _ffn_fwd(tag, h, w_in8[2 * i + f], w_out4[2 * i + f], tf, tf // 2)
                weight = 0.5
            elif i == 0:
                h = _prenorm(f"prenorm_{tag}", xs, g_pre, scale, shift, F32, tr)
                u, z, v = _pool_fwd(h, w4, pool_b_all, pscale_row, min(4 * tm, s))
                extra = (z, v)
                weight = 1.0
            else:
                h = _prenorm(f"prenorm_{tag}", xs, g_pre, scale, shift, BF16, tr)
                u, extra = _mla_fwd(h, mla_wts, cos_k, sin_k, tm)
                weight = 1.0
            saved.append((xs, h, u, extra, (shift, scale, gate, g_pre, g_post), weight))
            xs = _postnorm(f"postnorm_{tag}", xs, u, g_post, gate, weight, tr)

    def loss_fn(yv, tv):
        e = yv - tv
        return (e * (1.0 / d),), (_colsum(e * e),)

    (dx,), (sq,) = _rowmap("loss", loss_fn, [_tile(xs, tr), _tile(target, tr)], [_otile(s, d, F32, tr)], [_ored(d)], (s // tr,))
    loss_part = (0.5 * jnp.sum(sq) / d).reshape(1, 1)

    d_mod = [[None] * n_mod for _ in range(2)]
    d_g = [[None] * 6 for _ in range(2)]
    sent = {}
    pool_grads = mla_grads = None

    def start_scatter(key, arrays):
        state, token = _split_start(f"scatter_start_{key}", "scatter", arrays)
        sent[key] = state
        return token

    for i in (1, 0):
        for sub in (2, 1, 0):
            xin, h, u, extra, (shift, scale, gate, g_pre, g_post), weight = saved[3 * i + sub]
            tag = f"{i}{sub}"
            du, (dgate, dgpost) = _postnorm_bwd(f"postnorm_bwd_{tag}", dx, u, g_post, gate, weight, F32 if (sub == 1 and i == 0) else BF16, tr)
            if sub != 1:
                k = 2 * i + sub // 2
                dh, dgu, act = _ffn_bwd_act(tag, du, extra, w_in8[k], w_out4[k], tf, tf // 4)
                dw_out = _ffn_dw(f"ffn_dwout_{tag}", act, du, tf).reshape(N_DEV, FF_BLK // 2, d)
                if k == 0:
                    d_pool_w = pool_grads[0].reshape(4, N_DEV, 32, POOL_GROUP).transpose(1, 0, 2, 3).reshape(N_DEV, 4 * 32, POOL_GROUP)
                    token = start_scatter(tag + "_out", [dw_out, d_pool_w])
                    token = start_scatter(tag + "_in", [_ffn_dw(f"ffn_dwin_{tag}", dgu, h, tf, after=token)])
                else:
                    token = start_scatter(tag, [_ffn_dw(f"ffn_dwin_{tag}", dgu, h, tf), dw_out])
                g_pre = g_pre + token[0, 0]
            elif i == 0:
                dh, dw4, (dpscale, dpb) = _pool_bwd(du, extra[0], extra[1], w4, pscale_row, min(4 * tm, s))
                pool_grads = (dw4, dpscale, dpb)
            else:
                dh, mla_grads = _mla_bwd(du, h, extra, mla_wts, cos_k, sin_k, tm)
                dwq = mla_grads["wq"]
                d_uq = jnp.concatenate([dwq[:, :, :QK_NOPE], dwq[:, :, KV_LORA:KV_LORA + QK_ROPE]], axis=2).transpose(1, 0, 2)
                token = start_scatter("mla", [mla_grads["w_in"][:, :mla_w_in.shape[2]].reshape(N_DEV, d // N_DEV, -1),
                                              d_uq.reshape(N_DEV, Q_LORA // N_DEV, -1), mla_grads["w_o"].reshape(N_DEV, d // N_DEV, d)])
                dwukp, dwuv2 = mla_grads["wukp"], mla_grads["wuv2"]
                d_uk = dwukp[:, :QK_NOPE, :KV_LORA].transpose(2, 0, 1).reshape(KV_LORA, -1)
                d_uv = jnp.where(even, dwuv2[:, :, :V_HEAD], dwuv2[:, :, V_HEAD:]).transpose(1, 0, 2).reshape(KV_LORA, -1)
                state_ukv, token_ukv = _split_start("gather_start_ukv", "gather", [d_uk, d_uv], after=token)
                g_pre = g_pre + token_ukv[0, 0]
            dx, (dshift, dscale, dgpre) = _prenorm_bwd(f"prenorm_bwd_{tag}", dh, xin, dx, g_pre, scale, tr)
            d_mod[i][3 * sub:3 * sub + 3] = [dshift, dscale, dgate]
            d_g[i][2 * sub:2 * sub + 2] = [dgpre, dgpost]
    grad_x = dx.reshape(x.shape)

    def upd(name, parts, w, m, v):
        shape = w.shape
        r, cdim = parts.shape[1], parts.shape[2]
        return [o.reshape(shape) for o in _adamw(name, parts, w.reshape(r, cdim), m.reshape(r, cdim), v.reshape(r, cdim))]

    def landed(key, after):
        return _split_wait(f"scatter_wait_{key}", sent[key], after)

    res = {}
    w_in_s, m_in_s, v_in_s = (a.reshape(4, FF_BLK, d) for a in (w_in_t, m_in_t, v_in_t))
    w_out_s, m_out_s, v_out_s = (a.reshape(4, FF_BLK // 2, d) for a in (ffn_w_out, m_ffn_w_out, v_ffn_w_out))
    bufs_in = [lax.empty(w_in_s.shape, F32) for _ in range(4)]
    bufs_out = [lax.empty(w_out_s.shape, F32) for _ in range(4)]
    for key, k in (("12", 3), ("mla", None), ("10", 2), ("02", 1)):
        parts = landed(key, grad_x)
        if k is None:
            res["mla_w_in"] = upd("adam_mla_w_in", parts[0], mla_w_in, m_mla_w_in, v_mla_w_in)
            res["mla_w_uq"] = upd("adam_mla_w_uq", parts[1], mla_w_uq, m_mla_w_uq, v_mla_w_uq)
            res["mla_w_o"] = upd("adam_mla_w_o", parts[2], mla_w_o, m_mla_w_o, v_mla_w_o)
            uk_all, uv_all = _split_wait("gather_wait_ukv", state_ukv, grad_x)
            res["mla_w_uk"] = upd("adam_mla_w_uk", uk_all, mla_w_uk, m_mla_w_uk, v_mla_w_uk)
            res["mla_w_uv"] = upd("adam_mla_w_uv", uv_all, mla_w_uv, m_mla_w_uv, v_mla_w_uv)
            continue
        bufs_in = _adamw_slab(f"adam_ffn_w_in_{key}", parts[0], w_in_s, m_in_s, v_in_s, bufs_in, k)
        bufs_out = _adamw_slab(f"adam_ffn_w_out_{key}", parts[1], w_out_s, m_out_s, v_out_s, bufs_out, k)

    dw4, dpscale, dpb = pool_grads
    d_mod_row = jnp.concatenate([jnp.concatenate(r, axis=1) for r in d_mod], axis=1)
    d_g_row = jnp.concatenate([jnp.concatenate(r, axis=1) for r in d_g], axis=1)
    small_g = jnp.concatenate([d_mod_row, d_g_row, dpb, dpscale, mla_grads["q_norm"], mla_grads["kv_norm"], loss_part], axis=1)
    done = [bufs_in[0], bufs_out[0], res["mla_w_o"][0], res["mla_w_uv"][0]]
    (small_g_all,) = _all_gather("gather_small_grads", [small_g], after=sum(a.reshape(-1)[:1] for a in done))
    small_g_all = small_g_all.reshape(N_DEV, -1)
    n_m = 2 * n_mod * d
    d_mod_all = small_g_all[:, :n_m].reshape(N_DEV, 2, n_mod * d)
    rest = small_g_all[:, n_m:]
    p_norm_g = lax.dynamic_slice_in_dim(rest[:, :12 * d].reshape(N_DEV, 12, d), me * (d // N_DEV), d // N_DEV, axis=2)
    p_pool_b = lax.dynamic_slice_in_dim(rest[:, 12 * d:13 * d].reshape(N_DEV, 4, POOL_GROUP), me * 32, 32, axis=2)
    p_pool_scale = rest[:, 13 * d:14 * d].reshape(N_DEV, 1, d)
    p_q_norm = lax.dynamic_slice_in_dim(rest[:, 14 * d:14 * d + Q_LORA], me * 32, 32, axis=1).reshape(N_DEV, 1, 32)
    p_kv_norm = rest[:, 14 * d + Q_LORA:14 * d + Q_LORA + KV_LORA].reshape(N_DEV, 1, KV_LORA)
    loss = jnp.sum(rest[:, -1])

    d_mod_loc = lax.dynamic_slice_in_dim(d_mod_all, me * mod_blk, mod_blk, axis=2).transpose(1, 0, 2)
    k_pad = 128
    sc_t = jnp.pad(sc_all.T, ((0, 0), (0, k_pad - N_DEV)))
    d_ada_w = _matmul("ada_dw", sc_t, jnp.pad(d_mod_loc, ((0, 0), (0, k_pad - N_DEV), (0, 0))), a_blk=(d, k_pad),
                      a_map=lambda i, k: (0, 0), b_blk=(None, k_pad, mod_blk), b_map=lambda i, k: (i, 0, 0), o_shape=(2, d, mod_blk),
                      o_blk=(None, d, mod_blk), o_map=lambda i, k: (i, 0, 0), grid=(2, 1), contract=NN, out_dtype=F32)
    res["ada_w"] = upd("adam_ada_w", d_ada_w.reshape(1, 2 * d, mod_blk), ada_w, m_ada_w, v_ada_w)
    res["ada_b"] = upd("adam_ada_b", d_mod_all.reshape(N_DEV, 2, n_mod * d), ada_b, m_ada_b, v_ada_b)
    res["norm_g"] = upd("adam_norm_g", p_norm_g, norm_g, m_norm_g, v_norm_g)
    res["pool_b"] = upd("adam_pool_b", p_pool_b, pool_b, m_pool_b, v_pool_b)
    res["pool_scale"] = upd("adam_pool_scale", p_pool_scale, pool_scale, m_pool_scale, v_pool_scale)
    res["mla_q_norm"] = upd("adam_mla_q_norm", p_q_norm, mla_q_norm, m_mla_q_norm, v_mla_q_norm)
    res["mla_kv_norm"] = upd("adam_mla_kv_norm", p_kv_norm, mla_kv_norm, m_mla_kv_norm, v_mla_kv_norm)

    p_out, p_pool_w = landed("00_out", res["ada_w"][1])
    bufs_out = _adamw_slab("adam_ffn_w_out_00", p_out, w_out_s, m_out_s, v_out_s, bufs_out, 0)
    res["pool_w"] = upd("adam_pool_w", p_pool_w, pool_w, m_pool_w, v_pool_w)
    (p_in,) = landed("00_in", res["pool_w"][1])
    bufs_in = _adamw_slab("adam_ffn_w_in_00", p_in, w_in_s, m_in_s, v_in_s, bufs_in, 0)
    res["ffn_w_in"] = [jnp.swapaxes(b.reshape(w_in_t.shape), 2, 3) for b in bufs_in]
    res["ffn_w_out"] = [b.reshape(ffn_w_out.shape) for b in bufs_out]

    order = ["ada_w", "ada_b", "norm_g", "ffn_w_in", "ffn_w_out", "pool_w", "pool_b", "pool_scale", "mla_w_in", "mla_q_norm",
             "mla_kv_norm", "mla_w_uq", "mla_w_uk", "mla_w_uv", "mla_w_o"]
    outs = [loss, grad_x]
    for j in range(4):
        outs += [res[name][j] for name in order]
    return tuple(outs)
```

```python
import functools

import jax
import jax.numpy as jnp
from jax import lax
from jax.experimental import pallas as pl
from jax.experimental.pallas import tpu as pltpu

F32 = jnp.float32
BF16 = jnp.bfloat16
N_DEV = 8
AXES = ("x", "y", "c")
MESH = pl.DeviceIdType.MESH

D_MODEL = 1024
N_HEADS = 16
QK_NOPE = 64
QK_ROPE = 32
V_HEAD = 64
Q_LORA = 256
KV_LORA = 128
LAT_PAD = 512
QK_PAD = 256
D_FF = 2816
FF_BLK = 2 * D_FF // N_DEV
POOL_WINDOWS = (2, 4, 8, 16)
POOL_GROUP = 256
ROPE_THETA = 10000.0
EPS = 1e-6
ATTN_SCALE = (QK_NOPE + QK_ROPE) ** -0.5
LOG2_E = 1.4426950408889634
ADAM_LR, ADAM_B1, ADAM_B2, ADAM_EPS, ADAM_WD, ADAM_STEP = 0.001, 0.9, 0.999, 1e-08, 0.01, 10
VMEM_LIMIT = 56 * 1024 * 1024

NN = ((1,), (0,))
NT = ((1,), (1,))
TN = ((0,), (0,))


def _params(**kw):
    return pltpu.CompilerParams(vmem_limit_bytes=VMEM_LIMIT, **kw)


def _dot(a, b, contract):
    return lax.dot_general(a, b, (contract, ((), ())), preferred_element_type=F32)


def _matmul(name, a, b, *, a_blk, a_map, b_blk, b_map, o_shape, o_blk, o_map, grid, contract, out_dtype,
            bias=None, bias_blk=None, bias_map=None, after=None):
    n_k = grid[-1]
    k_axis = len(grid) - 1
    acc_shape = tuple(d for d in o_blk if d is not None)

    def body(*refs):
        a_ref, b_ref = refs[:2]
        bias_ref = refs[2] if bias is not None else None
        if n_k == 1:
            r = _dot(a_ref[...].astype(BF16), b_ref[...].astype(BF16), contract)
            if bias is not None:
                r = r + bias_ref[...]
            refs[-1][...] = r.astype(refs[-1].dtype)
            return
        o_ref, acc = refs[-2:]
        k = pl.program_id(k_axis)

        @pl.when(k == 0)
        def _():
            acc[...] = jnp.zeros_like(acc)

        acc[...] += _dot(a_ref[...].astype(BF16), b_ref[...].astype(BF16), contract)

        @pl.when(k == n_k - 1)
        def _():
            r = acc[...]
            if bias is not None:
                r = r + bias_ref[...]
            o_ref[...] = r.astype(o_ref.dtype)

    in_specs = [pl.BlockSpec(a_blk, a_map), pl.BlockSpec(b_blk, b_map)]
    args = [a, b]
    if bias is not None:
        in_specs.append(pl.BlockSpec(bias_blk, bias_map))
        args.append(bias)
    if after is not None:
        in_specs.append(pl.BlockSpec(memory_space=pl.ANY))
        args.append(after)
    return pl.pallas_call(
        body, name=name, grid=grid, in_specs=in_specs, out_specs=pl.BlockSpec(o_blk, o_map),
        out_shape=jax.ShapeDtypeStruct(o_shape, out_dtype), scratch_shapes=[pltpu.VMEM(acc_shape, F32)] if n_k > 1 else [],
        compiler_params=_params(),
    )(*args)


def _rowmap(name, fn, ins, outs, reds, grid, after=None):
    n_in, n_out, n_red = len(ins), len(outs), len(reds)
    extra = [] if after is None else [after]

    def body(*refs):
        in_refs = refs[:n_in]
        out_refs = refs[n_in + len(extra):n_in + len(extra) + n_out]
        red_refs = refs[n_in + len(extra) + n_out:]
        out_vals, red_vals = fn(*[r[...] for r in in_refs])
        for r, v in zip(out_refs, out_vals):
            r[...] = v.astype(r.dtype)
        if n_red:
            first = pl.program_id(0) == 0
            for ax in range(1, len(grid)):
                first = jnp.logical_and(first, pl.program_id(ax) == 0)

            @pl.when(first)
            def _():
                for r in red_refs:
                    r[...] = jnp.zeros_like(r)

            for r, v in zip(red_refs, red_vals):
                r[...] += v

    res = pl.pallas_call(
        body, name=name, grid=grid,
        in_specs=[pl.BlockSpec(blk, imap) for _, blk, imap in ins] + [pl.BlockSpec(memory_space=pl.ANY)] * len(extra),
        out_specs=[pl.BlockSpec(blk, imap) for _, blk, imap in list(outs) + list(reds)],
        out_shape=[sds for sds, _, _ in list(outs) + list(reds)],
        compiler_params=_params(),
    )(*[a for a, _, _ in ins], *extra)
    return res[:n_out], res[n_out:]


def _sds(shape, dtype):
    return jax.ShapeDtypeStruct(shape, dtype)


def _tile(a, tm):
    return (a, (tm, a.shape[1]), lambda i: (i, 0))


def _row(a):
    return (a, (1, a.shape[1]), lambda i: (0, 0))


def _otile(n, c, dtype, tm):
    return (_sds((n, c), dtype), (tm, c), lambda i: (i, 0))


def _ored(c):
    return (_sds((1, c), F32), (1, c), lambda i: (0, 0))


def _colsum(v):
    return jnp.sum(v, axis=0, keepdims=True)


def _rstd(v):
    return lax.rsqrt(jnp.mean(v * v, axis=-1, keepdims=True) + EPS)


def _pre(xv, g, sc, sh):
    return xv * _rstd(xv) * g * (1.0 + sc) + sh


def _post(xv, uv, g, gt, weight):
    return xv + weight * (1.0 + gt) * (uv * _rstd(uv) * g)


def _post_bwd(dv, uv, g, gt, weight):
    r = _rstd(uv)
    un = uv * r
    dy = dv * (weight * (1.0 + gt))
    a = dy * g
    du = r * (a - un * jnp.mean(a * un, axis=-1, keepdims=True))
    return du, (_colsum(dv * (weight * (un * g))), _colsum(dy * un))


def _pre_bwd(dhv, xv, dv, g, sc):
    dhv = dhv.astype(F32)
    r = _rstd(xv)
    xn = xv * r
    b = dhv * (g * (1.0 + sc))
    dx = dv + r * (b - xn * jnp.mean(b * xn, axis=-1, keepdims=True))
    return dx, (_colsum(dhv), _colsum(dhv * (xn * g)), _colsum(dhv * ((1.0 + sc) * xn)))


def _rowk(rows, k):
    return (rows, (None, 1, rows.shape[2]), lambda i: (k, 0, 0))


def _prenorm(name, x, pre, out_dtype, tm):
    n, d = x.shape
    (h,), _ = _rowmap(name, lambda xv, g, sc, sh: ((_pre(xv, g, sc, sh),), ()), [_tile(x, tm), *pre], [_otile(n, d, out_dtype, tm)],
                      [], (n // tm,))
    return h


def _norm_link(name, x, u, post, weight, pre, out_dtype, tm):
    n, d = x.shape

    def fn(xv, uv, g, gt, g2, sc, sh):
        xn = _post(xv, uv, g, gt, weight)
        return (xn, _pre(xn, g2, sc, sh)), ()

    (xn, h), _ = _rowmap(name, fn, [_tile(x, tm), _tile(u, tm), *post, *pre], [_otile(n, d, F32, tm), _otile(n, d, out_dtype, tm)],
                         [], (n // tm,))
    return xn, h


def _norm_loss(name, x, u, target, post, weight, tm):
    n, d = x.shape

    def fn(xv, uv, tv, g, gt):
        e = _post(xv, uv, g, gt, weight) - tv
        dv = e * (1.0 / d)
        du, reds = _post_bwd(dv, uv, g, gt, weight)
        return (dv, du), (_colsum(e * e), *reds)

    (dx, du), reds = _rowmap(name, fn, [_tile(x, tm), _tile(u, tm), _tile(target, tm), *post],
                             [_otile(n, d, F32, tm), _otile(n, d, BF16, tm)], [_ored(d)] * 3, (n // tm,))
    return dx, du, reds


def _norm_link_bwd(name, dh, x, dout, u_prev, pre, post_prev, weight_prev, out_dtype, tm, after=None):
    n, d = x.shape

    def fn(dhv, xv, dv, uv, g, sc, g2, gt):
        dx, reds = _pre_bwd(dhv, xv, dv, g, sc)
        du, reds_prev = _post_bwd(dx, uv, g2, gt, weight_prev)
        return (dx, du), (*reds, *reds_prev)

    (dx, du), reds = _rowmap(name, fn, [_tile(dh, tm), _tile(x, tm), _tile(dout, tm), _tile(u_prev, tm), *pre, *post_prev],
                             [_otile(n, d, F32, tm), _otile(n, d, out_dtype, tm)], [_ored(d)] * 5, (n // tm,), after=after)
    return dx, du, reds


def _prenorm_bwd(name, dh, x, dout, pre, tm, after=None):
    n, d = x.shape

    def fn(dhv, xv, dv, g, sc):
        dx, reds = _pre_bwd(dhv, xv, dv, g, sc)
        return (dx,), reds

    (dx,), reds = _rowmap(name, fn, [_tile(dh, tm), _tile(x, tm), _tile(dout, tm), *pre], [_otile(n, d, F32, tm)], [_ored(d)] * 3,
                          (n // tm,), after=after)
    return dx, reds


def _ffn_fwd(tag, h, w_in8, w_out4, tm, sub):
    s, d = h.shape

    def body(h_ref, wg_ref, wu_ref, wo_ref, u_ref, gu_ref):
        @pl.when(pl.program_id(1) == 0)
        def _():
            u_ref[...] = jnp.zeros_like(u_ref)

        for r in range(tm // sub):
            rows = pl.ds(r * sub, sub)
            hv = h_ref[rows, :]
            gate = _dot(hv, wg_ref[...], NT)
            up = _dot(hv, wu_ref[...], NT)
            gu_ref[0, rows, :] = gate.astype(BF16)
            gu_ref[1, rows, :] = up.astype(BF16)
            u_ref[rows, :] += _dot((gate * jax.nn.sigmoid(gate) * up).astype(BF16), wo_ref[...], NN)

    w_blk = (None, FF_BLK, d)
    return pl.pallas_call(
        body, name=f"ffn_fwd_{tag}", grid=(s // tm, 4),
        in_specs=[pl.BlockSpec((tm, d), lambda i, j: (i, 0)), pl.BlockSpec(w_blk, lambda i, j: (j, 0, 0)),
                  pl.BlockSpec(w_blk, lambda i, j: (j + 4, 0, 0)), pl.BlockSpec((None, FF_BLK, d), lambda i, j: (j, 0, 0))],
        out_specs=[pl.BlockSpec((tm, d), lambda i, j: (i, 0)), pl.BlockSpec((2, None, tm, FF_BLK), lambda i, j: (0, j, i, 0))],
        out_shape=[_sds((s, d), F32), _sds((2, 4, s, FF_BLK), BF16)], compiler_params=_params(),
    )(h, w_in8, w_in8, w_out4)


def _ffn_bwd_act(tag, du, gu, w_in8, w_out4, tm, sub):
    s, d = du.shape

    def body(du_ref, gu_ref, wg_ref, wu_ref, wo_ref, dh_ref, dgu_ref, act_ref):
        @pl.when(pl.program_id(1) == 0)
        def _():
            dh_ref[...] = jnp.zeros_like(dh_ref)

        for r in range(tm // sub):
            rows = pl.ds(r * sub, sub)
            dact = _dot(du_ref[rows, :], wo_ref[...], NT)
            gate, up = gu_ref[0, rows, :].astype(F32), gu_ref[1, rows, :].astype(F32)
            sg = jax.nn.sigmoid(gate)
            silu = gate * sg
            dg = (dact * up * (sg * (1.0 + gate * (1.0 - sg)))).astype(BF16)
            dup = (dact * silu).astype(BF16)
            dgu_ref[0, :, rows] = dg.T
            dgu_ref[1, :, rows] = dup.T
            act_ref[:, rows] = (silu * up).astype(BF16).T
            dh_ref[rows, :] += _dot(dg, wg_ref[...], NN) + _dot(dup, wu_ref[...], NN)

    w_blk = (None, FF_BLK, d)
    dh, dgu_t, act_t = pl.pallas_call(
        body, name=f"ffn_bwd_{tag}", grid=(s // tm, 4),
        in_specs=[pl.BlockSpec((tm, d), lambda i, j: (i, 0)), pl.BlockSpec((2, None, tm, FF_BLK), lambda i, j: (0, j, i, 0)),
                  pl.BlockSpec(w_blk, lambda i, j: (j, 0, 0)), pl.BlockSpec(w_blk, lambda i, j: (j + 4, 0, 0)),
                  pl.BlockSpec((None, FF_BLK, d), lambda i, j: (j, 0, 0))],
        out_specs=[pl.BlockSpec((tm, d), lambda i, j: (i, 0)), pl.BlockSpec((2, None, FF_BLK, tm), lambda i, j: (0, j, 0, i)),
                   pl.BlockSpec((None, FF_BLK, tm), lambda i, j: (j, 0, i))],
        out_shape=[_sds((s, d), F32), _sds((2, 4, FF_BLK, s), BF16), _sds((4, FF_BLK, s), BF16)], compiler_params=_params(),
    )(du, gu, w_in8, w_in8, w_out4)
    return dh, dgu_t.reshape(8, FF_BLK, s), act_t


def _ffn_dw(name, lhs_t, rhs, tk, after=None):
    n_g, _, s = lhs_t.shape
    d = rhs.shape[1]
    return _matmul(name, lhs_t, rhs, a_blk=(None, FF_BLK, tk), a_map=lambda g, k: (g, 0, k), b_blk=(tk, d), b_map=lambda g, k: (k, 0),
                   o_shape=(n_g, FF_BLK, d), o_blk=(None, FF_BLK, d), o_map=lambda g, k: (g, 0, 0), grid=(n_g, s // tk),
                   contract=NN, out_dtype=BF16, after=after)


def _window_sum(x, window, transpose):
    s = x.shape[0]
    t = lax.broadcasted_iota(jnp.int32, (s, 1), 0)
    half = window // 2
    cnt = jnp.minimum(t + half, s) - jnp.maximum(t - half, 0)
    inv = 1.0 / cnt.astype(F32)
    if transpose:
        x = x * inv
        offsets = range(-half + 1, half + 1)
    else:
        offsets = range(-half, half)
    acc = jnp.zeros_like(x)
    for o in offsets:
        shifted = x if o == 0 else pltpu.roll(x, (-o) % s, 0)
        valid = jnp.logical_and(t + o >= 0, t + o < s)
        acc = acc + jnp.where(valid, shifted, 0.0)
    return acc if transpose else acc * inv


def _pool_mix(name, x, transpose, out_dtype):
    s, d = x.shape

    def body(x_ref, o_ref):
        g = pl.program_id(0)
        for gi, window in enumerate(POOL_WINDOWS):
            @pl.when(g == gi)
            def _(window=window):
                xv = x_ref[...].astype(F32)
                o_ref[...] = (_window_sum(xv, window, transpose) - xv).astype(o_ref.dtype)

    return pl.pallas_call(
        body, name=name, grid=(len(POOL_WINDOWS),), in_specs=[pl.BlockSpec((s, POOL_GROUP), lambda g: (0, g))],
        out_specs=pl.BlockSpec((s, POOL_GROUP), lambda g: (0, g)), out_shape=_sds((s, d), out_dtype), compiler_params=_params(),
    )(x)


def _pool_fwd(h, w4, bias, pscale, tm):
    s, d = h.shape
    nt = s // tm
    z = _pool_mix("pool_mix", h, False, BF16)
    v = _matmul("pool_proj", z, w4, a_blk=(tm, POOL_GROUP), a_map=lambda i, g, k: (i, g), b_blk=(None, POOL_GROUP, POOL_GROUP),
                b_map=lambda i, g, k: (g, 0, 0), o_shape=(s, d), o_blk=(tm, POOL_GROUP), o_map=lambda i, g, k: (i, g),
                grid=(nt, 4, 1), contract=NN, out_dtype=F32, bias=bias, bias_blk=(1, POOL_GROUP), bias_map=lambda i, g, k: (0, g))
    tr = min(512, s)
    (u,), _ = _rowmap("pool_scale", lambda vv, ps: ((vv * ps,), ()), [_tile(v, tr), _row(pscale)], [_otile(s, d, F32, tr)], [],
                      (s // tr,))
    return u, z, v


def _pool_bwd(du, z, v, w4, pscale, tm):
    s, d = du.shape
    nt = s // tm

    def fn(duv, vv, ps):
        dv = duv * ps
        return (dv,), (_colsum(duv * vv), _colsum(dv))

    tr = min(512, s)
    (dv,), reds = _rowmap("pool_dscale", fn, [_tile(du, tr), _tile(v, tr), _row(pscale)], [_otile(s, d, BF16, tr)],
                          [_ored(d), _ored(d)], (s // tr,))
    dw4 = _matmul("pool_dw", z, dv, a_blk=(tm, POOL_GROUP), a_map=lambda g, k: (k, g), b_blk=(tm, POOL_GROUP),
                  b_map=lambda g, k: (k, g), o_shape=(4, POOL_GROUP, POOL_GROUP), o_blk=(None, POOL_GROUP, POOL_GROUP),
                  o_map=lambda g, k: (g, 0, 0), grid=(4, nt), contract=TN, out_dtype=F32)
    dz = _matmul("pool_dz", dv, w4, a_blk=(tm, POOL_GROUP), a_map=lambda i, g, k: (i, g), b_blk=(None, POOL_GROUP, POOL_GROUP),
                 b_map=lambda i, g, k: (g, 0, 0), o_shape=(s, d), o_blk=(tm, POOL_GROUP), o_map=lambda i, g, k: (i, g),
                 grid=(nt, 4, 1), contract=NT, out_dtype=F32)
    dh = _pool_mix("pool_mix_t", dz, True, F32)
    return dh, dw4, reds


def _lane(shape):
    return lax.broadcasted_iota(jnp.int32, shape, 1)


def _rope_swap(v, transpose):
    half = QK_ROPE // 2
    lane = _lane(v.shape)
    up = pltpu.roll(v, v.shape[1] - half, 1)
    down = pltpu.roll(v, half, 1)
    if transpose:
        return jnp.where(lane < half, up, jnp.where(lane < QK_ROPE, -down, 0.0))
    return jnp.where(lane < half, -up, jnp.where(lane < QK_ROPE, down, 0.0))


def _rope(v, cos, sin):
    return v * cos + _rope_swap(v, False) * sin


def _rope_t(g, cos, sin):
    return g * cos + _rope_swap(g * sin, True)


def _mla_mid(lat, q_norm, kv_norm, cos_k, sin_k, tm):
    s = lat.shape[0]

    def fn(lv, qn, kn, cs, sn):
        cq = lv[:, :Q_LORA]
        ckv = lv[:, Q_LORA:Q_LORA + KV_LORA]
        kr = lv[:, Q_LORA + KV_LORA:]
        cq = cq * _rstd(cq) * qn
        ckv = ckv * _rstd(ckv) * kn
        return (cq, jnp.concatenate([ckv, _rope(kr, cs, sn)], axis=1)), ()

    (cq, kcat), _ = _rowmap("mla_mid", fn, [_tile(lat, tm), _row(q_norm), _row(kv_norm), _tile(cos_k, tm), _tile(sin_k, tm)],
                            [_otile(s, Q_LORA, BF16, tm), _otile(s, QK_PAD, BF16, tm)], [], (s // tm,))
    return cq, kcat


def _mla_mid_bwd(lat, dcq, dkcat, dv, q_norm, kv_norm, cos_k, sin_k, tm):
    s = lat.shape[0]

    def fn(lv, dq, dk, dvv, qn, kn, cs, sn):
        dk = dk * (1.0 / LOG2_E)
        cq = lv[:, :Q_LORA]
        ckv = lv[:, Q_LORA:Q_LORA + KV_LORA]
        rq, rk = _rstd(cq), _rstd(ckv)
        cqn, ckn = cq * rq, ckv * rk
        a = dq * qn
        d_cq = rq * (a - cqn * jnp.mean(a * cqn, axis=-1, keepdims=True))
        dckv = dk[:, :KV_LORA] + dvv
        a2 = dckv * kn
        d_ckv = rk * (a2 - ckn * jnp.mean(a2 * ckn, axis=-1, keepdims=True))
        d_kr = _rope_t(dk[:, KV_LORA:], cs, sn)
        return (jnp.concatenate([d_cq, d_ckv, d_kr], axis=1),), (_colsum(dq * cqn), _colsum(dckv * ckn))

    (dlat,), reds = _rowmap(
        "mla_mid_bwd", fn,
        [_tile(lat, tm), _tile(dcq, tm), _tile(dkcat, tm), _tile(dv, tm), _row(q_norm), _row(kv_norm), _tile(cos_k, tm),
         _tile(sin_k, tm)],
        [_otile(s, LAT_PAD, BF16, tm)], [_ored(Q_LORA), _ored(KV_LORA)], (s // tm,))
    return dlat, reds


def _mla_q(cq, wq, wukp, cos_k, sin_k, tm):
    s = cq.shape[0]

    def body(cq_ref, wq_ref, wuk_ref, cos_ref, sin_ref, o_ref):
        cqv, cs, sn = cq_ref[...], cos_ref[...], sin_ref[...]
        for h in range(N_HEADS):
            aq = _dot(cqv, wq_ref[h], NN)
            qlat = _dot(aq.astype(BF16), wuk_ref[h], NN)
            roped = _rope(aq[:, KV_LORA:], cs, sn)
            o_ref[h] = (jnp.concatenate([qlat[:, :KV_LORA], roped], axis=1) * (ATTN_SCALE * LOG2_E)).astype(o_ref.dtype)

    wblk = pl.BlockSpec((N_HEADS, QK_PAD, QK_PAD), lambda i: (0, 0, 0))
    tblk = pl.BlockSpec((tm, KV_LORA), lambda i: (i, 0))
    return pl.pallas_call(
        body, name="mla_q", grid=(s // tm,),
        in_specs=[pl.BlockSpec((tm, Q_LORA), lambda i: (i, 0)), wblk, wblk, tblk, tblk],
        out_specs=pl.BlockSpec((N_HEADS, tm, QK_PAD), lambda i: (0, i, 0)), out_shape=_sds((N_HEADS, s, QK_PAD), BF16),
        compiler_params=_params(),
    )(cq, wq, wukp, cos_k, sin_k)


def _mla_q_bwd(cq, wq, wukp, cos_k, sin_k, dqcat, tm):
    s = cq.shape[0]

    def body(cq_ref, wq_ref, wuk_ref, cos_ref, sin_ref, dq_ref, dcq_ref, dwq_ref, dwuk_ref):
        @pl.when(pl.program_id(0) == 0)
        def _():
            dwq_ref[...] = jnp.zeros_like(dwq_ref)
            dwuk_ref[...] = jnp.zeros_like(dwuk_ref)

        cqv, cs, sn = cq_ref[...], cos_ref[...], sin_ref[...]
        d_cq = jnp.zeros((tm, Q_LORA), F32)
        for h in range(N_HEADS):
            aq = _dot(cqv, wq_ref[h], NN).astype(BF16)
            g = dq_ref[h].astype(F32) * ATTN_SCALE
            gl, gr = g[:, :KV_LORA], g[:, KV_LORA:]
            dqlat = jnp.concatenate([gl, jnp.zeros_like(gl)], axis=1).astype(BF16)
            d_rope = _rope_t(gr, cs, sn)
            daq = _dot(dqlat, wuk_ref[h], NT) + jnp.concatenate([jnp.zeros_like(d_rope), d_rope], axis=1)
            daq_b = daq.astype(BF16)
            dwuk_ref[h] += _dot(aq, dqlat, TN)
            dwq_ref[h] += _dot(cqv, daq_b, TN)
            d_cq = d_cq + _dot(daq_b, wq_ref[h], NT)
        dcq_ref[...] = d_cq

    wblk = pl.BlockSpec((N_HEADS, QK_PAD, QK_PAD), lambda i: (0, 0, 0))
    tblk = pl.BlockSpec((tm, KV_LORA), lambda i: (i, 0))
    return pl.pallas_call(
        body, name="mla_q_bwd", grid=(s // tm,),
        in_specs=[pl.BlockSpec((tm, Q_LORA), lambda i: (i, 0)), wblk, wblk, tblk, tblk,
                  pl.BlockSpec((N_HEADS, tm, QK_PAD), lambda i: (0, i, 0))],
        out_specs=[pl.BlockSpec((tm, Q_LORA), lambda i: (i, 0)), wblk, wblk],
        out_shape=[_sds((s, Q_LORA), F32), _sds((N_HEADS, QK_PAD, QK_PAD), F32), _sds((N_HEADS, QK_PAD, QK_PAD), F32)],
        compiler_params=_params(),
    )(cq, wq, wukp, cos_k, sin_k, dqcat)


def _flash_fwd(qcat, kcat, tq, tk):
    n_h, s, _ = qcat.shape
    n_k = s // tk

    def body(q_ref, k_ref, v_ref, o_ref, lse_ref):
        q = q_ref[...]
        m = jnp.full((tq, 1), -1e30, F32)
        l = jnp.zeros((tq, 1), F32)
        acc = jnp.zeros((tq, KV_LORA), F32)
        for kk in range(n_k):
            rows = pl.ds(kk * tk, tk)
            sc = _dot(q, k_ref[rows, :], NT)
            m_new = jnp.maximum(m, jnp.max(sc, axis=1, keepdims=True))
            alpha = jnp.exp2(m - m_new)
            p = jnp.exp2(sc - m_new)
            l = alpha * l + jnp.sum(p, axis=1, keepdims=True)
            acc = alpha * acc + _dot(p.astype(BF16), v_ref[rows, :], NN)
            m = m_new
        o_ref[...] = (acc / l).astype(o_ref.dtype)
        lse_ref[...] = m + jnp.log2(l)

    return pl.pallas_call(
        body, name="mla_attn", grid=(n_h, s // tq),
        in_specs=[pl.BlockSpec((None, tq, QK_PAD), lambda h, i: (h, i, 0)), pl.BlockSpec((s, QK_PAD), lambda h, i: (0, 0)),
                  pl.BlockSpec((s, KV_LORA), lambda h, i: (0, 0))],
        out_specs=[pl.BlockSpec((None, tq, KV_LORA), lambda h, i: (h, i, 0)), pl.BlockSpec((None, tq, 1), lambda h, i: (h, i, 0))],
        out_shape=[_sds((n_h, s, KV_LORA), BF16), _sds((n_h, s, 1), F32)], compiler_params=_params(),
    )(qcat, kcat, kcat)


def _flash_bwd(qcat, kcat, o, do, lse, tq, tk):
    n_h, s, _ = qcat.shape
    n_k = s // tk

    def body(q_ref, k_ref, v_ref, o_ref, do_ref, lse_ref, dq_ref, dk_ref, dv_ref, dq_acc):
        h, i = pl.program_id(0), pl.program_id(1)

        @pl.when(jnp.logical_and(h == 0, i == 0))
        def _():
            dk_ref[...] = jnp.zeros_like(dk_ref)
            dv_ref[...] = jnp.zeros_like(dv_ref)

        q = q_ref[...]
        dov = do_ref[...]
        lse_v = lse_ref[...]
        delta = jnp.sum(dov.astype(F32) * o_ref[...].astype(F32), axis=1, keepdims=True)
        dq_acc[...] = jnp.zeros_like(dq_acc)

        for kk in range(n_k):
            rows = pl.ds(kk * tk, tk)
            k = k_ref[rows, :]
            p = jnp.exp2(_dot(q, k, NT) - lse_v)
            dp = _dot(dov, v_ref[rows, :], NT)
            ds = (p * (dp - delta)).astype(BF16)
            dq_acc[...] += _dot(ds, k, NN)
            dv_ref[rows, :] += _dot(p.astype(BF16), dov, TN)
            dk_ref[rows, :] += _dot(ds, q, TN)
        dq_ref[...] = dq_acc[...].astype(dq_ref.dtype)

    qblk = pl.BlockSpec((None, tq, QK_PAD), lambda h, i: (h, i, 0))
    oblk = pl.BlockSpec((None, tq, KV_LORA), lambda h, i: (h, i, 0))
    return pl.pallas_call(
        body, name="mla_attn_bwd", grid=(n_h, s // tq),
        in_specs=[qblk, pl.BlockSpec((s, QK_PAD), lambda h, i: (0, 0)), pl.BlockSpec((s, KV_LORA), lambda h, i: (0, 0)), oblk, oblk,
                  pl.BlockSpec((None, tq, 1), lambda h, i: (h, i, 0))],
        out_specs=[qblk, pl.BlockSpec((s, QK_PAD), lambda h, i: (0, 0)), pl.BlockSpec((s, KV_LORA), lambda h, i: (0, 0))],
        out_shape=[_sds((n_h, s, QK_PAD), BF16), _sds((s, QK_PAD), F32), _sds((s, KV_LORA), F32)],
        scratch_shapes=[pltpu.VMEM((tq, QK_PAD), F32)], compiler_params=_params(),
    )(qcat, kcat, kcat, o, do, lse)


def _mla_uv(o_lat, wuv2, do, tm):
    n_h, s, _ = o_lat.shape
    d = n_h * V_HEAD
    pair = 2 * V_HEAD
    lat_blk = pl.BlockSpec((n_h, tm, KV_LORA), lambda i: (0, i, 0))
    w_blk = pl.BlockSpec((n_h, KV_LORA, pair), lambda i: (0, 0, 0))
    row_blk = pl.BlockSpec((tm, d), lambda i: (i, 0))

    if do is None:
        def body(a_ref, w_ref, o_ref):
            for p in range(n_h // 2):
                o_ref[:, p * pair:(p + 1) * pair] = (
                    _dot(a_ref[2 * p], w_ref[2 * p], NN) + _dot(a_ref[2 * p + 1], w_ref[2 * p + 1], NN)).astype(o_ref.dtype)

        return pl.pallas_call(body, name="mla_uv", grid=(s // tm,), in_specs=[lat_blk, w_blk], out_specs=row_blk,
                              out_shape=_sds((s, d), BF16), compiler_params=_params())(o_lat, wuv2)

    def body(a_ref, w_ref, do_ref, dlat_ref, dw_ref):
        @pl.when(pl.program_id(0) == 0)
        def _():
            dw_ref[...] = jnp.zeros_like(dw_ref)

        for h in range(n_h):
            dov = do_ref[:, (h // 2) * pair:(h // 2 + 1) * pair]
            dlat_ref[h] = _dot(dov, w_ref[h], NT).astype(dlat_ref.dtype)
            dw_ref[h] += _dot(a_ref[h], dov, TN)

    return pl.pallas_call(body, name="mla_uv_bwd", grid=(s // tm,), in_specs=[lat_blk, w_blk, row_blk], out_specs=[lat_blk, w_blk],
                          out_shape=[_sds((n_h, s, KV_LORA), BF16), _sds((n_h, KV_LORA, pair), F32)], compiler_params=_params(),
                          )(o_lat, wuv2, do)


def _mla_fwd(h, wts, cos_k, sin_k, tm):
    s, d = h.shape
    nt = s // tm
    lat = _matmul("mla_lat", h, wts["w_in"], a_blk=(tm, d), a_map=lambda i, k: (i, 0), b_blk=(d, LAT_PAD), b_map=lambda i, k: (0, 0),
                  o_shape=(s, LAT_PAD), o_blk=(tm, LAT_PAD), o_map=lambda i, k: (i, 0), grid=(nt, 1), contract=NN, out_dtype=F32)
    cq, kcat = _mla_mid(lat, wts["q_norm"], wts["kv_norm"], cos_k, sin_k, tm)
    qcat = _mla_q(cq, wts["wq"], wts["wukp"], cos_k, sin_k, tm)
    o_lat, lse = _flash_fwd(qcat, kcat, min(2 * tm, s), tm)
    o = _mla_uv(o_lat, wts["wuv2"], None, tm)
    u = _matmul("mla_out", o, wts["w_o"], a_blk=(tm, d), a_map=lambda i, k: (i, 0), b_blk=(d, d), b_map=lambda i, k: (0, 0),
                o_shape=(s, d), o_blk=(tm, d), o_map=lambda i, k: (i, 0), grid=(nt, 1), contract=NN, out_dtype=F32)
    return u, (lat, cq, kcat, qcat, o_lat, lse, o)


def _mla_bwd(du, h, saved, wts, cos_k, sin_k, tm):
    lat, cq, kcat, qcat, o_lat, lse, o = saved
    s, d = h.shape
    nt = s // tm
    do = _matmul("mla_do", du, wts["w_o"], a_blk=(tm, d), a_map=lambda i, k: (i, 0), b_blk=(d, d), b_map=lambda i, k: (0, 0),
                 o_shape=(s, d), o_blk=(tm, d), o_map=lambda i, k: (i, 0), grid=(nt, 1), contract=NT, out_dtype=BF16)
    dw_o = _matmul("mla_dwo", o, du, a_blk=(tm, d), a_map=lambda k: (k, 0), b_blk=(tm, d), b_map=lambda k: (k, 0),
                   o_shape=(d, d), o_blk=(d, d), o_map=lambda k: (0, 0), grid=(nt,), contract=TN, out_dtype=F32)
    do_lat, dwuv2 = _mla_uv(o_lat, wts["wuv2"], do, tm)
    dqcat, dkcat, dv = _flash_bwd(qcat, kcat, o_lat, do_lat, lse, min(2 * tm, s), tm)
    dcq, dwq, dwukp = _mla_q_bwd(cq, wts["wq"], wts["wukp"], cos_k, sin_k, dqcat, tm)
    dlat, (dqn, dkn) = _mla_mid_bwd(lat, dcq, dkcat, dv, wts["q_norm"], wts["kv_norm"], cos_k, sin_k, tm)
    dh = _matmul("mla_dh", dlat, wts["w_in"], a_blk=(tm, LAT_PAD), a_map=lambda i, k: (i, 0), b_blk=(d, LAT_PAD),
                 b_map=lambda i, k: (0, 0), o_shape=(s, d), o_blk=(tm, d), o_map=lambda i, k: (i, 0), grid=(nt, 1), contract=NT,
                 out_dtype=F32)
    dw_in = _matmul("mla_dwin", h, dlat, a_blk=(tm, d), a_map=lambda k: (k, 0), b_blk=(tm, LAT_PAD), b_map=lambda k: (k, 0),
                    o_shape=(d, LAT_PAD), o_blk=(d, LAT_PAD), o_map=lambda k: (0, 0), grid=(nt,), contract=TN, out_dtype=F32)
    return dh, dict(w_in=dw_in, wq=dwq, wukp=dwukp, wuv2=dwuv2, w_o=dw_o, q_norm=dqn, kv_norm=dkn)


def _adamw(name, parts, w, m, v):
    n_parts, r, c = parts.shape
    tr = r
    for cand in (256, 128, 64, 32, 16, 8):
        if r > cand and r % cand == 0:
            tr = cand
            break

    def body(p_ref, w_ref, m_ref, v_ref, g_ref, d_ref, nm_ref, nv_ref):
        g = p_ref[0].astype(F32)
        for k in range(1, n_parts):
            g = g + p_ref[k].astype(F32)
        nm = ADAM_B1 * m_ref[...] + (1.0 - ADAM_B1) * g
        nv = ADAM_B2 * v_ref[...] + (1.0 - ADAM_B2) * (g * g)
        m_hat = nm / (1.0 - ADAM_B1 ** ADAM_STEP)
        v_hat = nv / (1.0 - ADAM_B2 ** ADAM_STEP)
        g_ref[...] = g
        d_ref[...] = -ADAM_LR * (m_hat / (jnp.sqrt(v_hat) + ADAM_EPS) + ADAM_WD * w_ref[...])
        nm_ref[...] = nm
        nv_ref[...] = nv

    blk = pl.BlockSpec((tr, c), lambda i: (i, 0))
    return pl.pallas_call(
        body, name=name, grid=(r // tr,), in_specs=[pl.BlockSpec((n_parts, tr, c), lambda i: (0, i, 0)), blk, blk, blk],
        out_specs=[blk] * 4, out_shape=[_sds((r, c), F32)] * 4, compiler_params=_params(),
    )(parts, w, m, v)


def _adamw_slab(name, parts, w, m, v, bufs, f):
    n_parts, r, c = parts.shape
    tr = max(t for t in range(8, 257, 8) if r % t == 0)

    def body(p_ref, w_ref, m_ref, v_ref, *rest):
        g_ref, d_ref, nm_ref, nv_ref = rest[4:]
        g = p_ref[0].astype(F32)
        for k in range(1, n_parts):
            g = g + p_ref[k].astype(F32)
        nm = ADAM_B1 * m_ref[...] + (1.0 - ADAM_B1) * g
        nv = ADAM_B2 * v_ref[...] + (1.0 - ADAM_B2) * (g * g)
        m_hat = nm / (1.0 - ADAM_B1 ** ADAM_STEP)
        v_hat = nv / (1.0 - ADAM_B2 ** ADAM_STEP)
        g_ref[...] = g
        d_ref[...] = -ADAM_LR * (m_hat / (jnp.sqrt(v_hat) + ADAM_EPS) + ADAM_WD * w_ref[...])
        nm_ref[...] = nm
        nv_ref[...] = nv

    blk = pl.BlockSpec((None, tr, c), lambda i: (f, i, 0))
    return pl.pallas_call(
        body, name=name, grid=(r // tr,),
        in_specs=[pl.BlockSpec((n_parts, tr, c), lambda i: (0, i, 0)), blk, blk, blk] + [pl.BlockSpec(memory_space=pl.ANY)] * 4,
        out_specs=[blk] * 4, out_shape=[_sds(w.shape, F32)] * 4, input_output_aliases={4 + j: j for j in range(4)},
        compiler_params=_params(),
    )(parts, w, m, v, *bufs)


def _mesh_pos():
    return lax.axis_index("x"), lax.axis_index("y"), lax.axis_index("c")


def _flip(pos, mask):
    return tuple(1 - p if (mask >> (2 - b)) & 1 else p for b, p in enumerate(pos))


def _index(pos):
    return 4 * pos[0] + 2 * pos[1] + pos[2]


def _all_gather(name, xs, after=None):
    n = len(xs)
    extra = [] if after is None else [after]

    def body(*refs):
        x_refs, o_refs = refs[:n], refs[n + len(extra):2 * n + len(extra)]
        send_sems, recv_sems, local_sems = refs[2 * n + len(extra):]
        me = _mesh_pos()
        sibling = _flip(me, 1)
        others = [_flip(me, 4), _flip(me, 2), _flip(me, 6)]

        def copy(k, j, block, to, src=None):
            dst = o_refs[k].at[_index(block)]
            return pltpu.make_async_remote_copy(
                src_ref=dst if src is None else src, dst_ref=dst, send_sem=send_sems.at[k, j], recv_sem=recv_sems.at[k, j],
                device_id=to, device_id_type=MESH)

        local = [pltpu.make_async_copy(x_refs[k], o_refs[k].at[_index(me)], local_sems.at[k]) for k in range(n)]
        for cp in local:
            cp.start()
        first = []
        for k in range(n):
            first.append(copy(k, 0, me, sibling, src=x_refs[k]))
            first += [copy(k, 1 + j, me, other, src=x_refs[k]) for j, other in enumerate(others)]
        for cp in first:
            cp.start()
        passed = []
        for j, other in enumerate(others):
            for k in range(n):
                copy(k, 1 + j, other, me).wait_recv()
                cp = copy(k, 4 + j, other, sibling)
                cp.start()
                passed.append(cp)
        for k in range(n):
            copy(k, 0, sibling, me).wait_recv()
        for j, other in enumerate(others):
            for k in range(n):
                copy(k, 4 + j, _flip(other, 1), me).wait_recv()
        for cp in first + passed:
            cp.wait_send()
        for cp in local:
            cp.wait()

    any_spec = pl.BlockSpec(memory_space=pl.ANY)
    return pl.pallas_call(
        body, name=name, in_specs=[any_spec] * (n + len(extra)), out_specs=[any_spec] * n,
        out_shape=[_sds((N_DEV,) + x.shape, x.dtype) for x in xs],
        scratch_shapes=[pltpu.SemaphoreType.DMA((n, 7)), pltpu.SemaphoreType.DMA((n, 7)), pltpu.SemaphoreType.DMA((n,))],
    )(*xs, *extra)


def _all_to_all(name, groups):
    flat = [(gi, f) for gi, grp in enumerate(groups) for f in range(len(grp))]
    n = len(flat)
    n_groups = len(groups)

    def body(*refs):
        x_refs, o_refs = refs[:n], refs[n:n + n_groups]
        send_sems, recv_sems, local_sems = refs[n + n_groups:]
        me = _mesh_pos()
        local, sends, recvs = [], [], []
        for k, (gi, f) in enumerate(flat):
            local.append(pltpu.make_async_copy(x_refs[k].at[_index(me)], o_refs[gi].at[_index(me), f], local_sems.at[k]))
            for mask in range(1, N_DEV):
                peer = _flip(me, mask)
                sends.append(pltpu.make_async_remote_copy(
                    src_ref=x_refs[k].at[_index(peer)], dst_ref=o_refs[gi].at[_index(me), f], send_sem=send_sems.at[k, mask - 1],
                    recv_sem=recv_sems.at[k, mask - 1], device_id=peer, device_id_type=MESH))
                recvs.append(pltpu.make_async_remote_copy(
                    src_ref=x_refs[k].at[_index(me)], dst_ref=o_refs[gi].at[_index(peer), f], send_sem=send_sems.at[k, mask - 1],
                    recv_sem=recv_sems.at[k, mask - 1], device_id=peer, device_id_type=MESH))
        for cp in local + sends:
            cp.start()
        for cp in recvs:
            cp.wait_recv()
        for cp in sends:
            cp.wait_send()
        for cp in local:
            cp.wait()

    any_spec = pl.BlockSpec(memory_space=pl.ANY)
    return pl.pallas_call(
        body, name=name, in_specs=[any_spec] * n, out_specs=[any_spec] * n_groups,
        out_shape=[_sds((N_DEV, len(grp)) + grp[0].shape[1:], grp[0].dtype) for grp in groups],
        scratch_shapes=[pltpu.SemaphoreType.DMA((n, 7)), pltpu.SemaphoreType.DMA((n, 7)), pltpu.SemaphoreType.DMA((n,))],
    )(*[a for grp in groups for a in grp])


def _split_copies(kind, outgoing, x_refs, land_refs, send_sems, recv_sems):
    me = _mesh_pos()
    copies = []
    for k, (x_ref, land_ref) in enumerate(zip(x_refs, land_refs)):
        for mask in range(1, N_DEV):
            peer = _flip(me, mask)
            sem = k * (N_DEV - 1) + mask - 1
            copies.append(pltpu.make_async_remote_copy(
                src_ref=x_ref if kind == "gather" else x_ref.at[_index(peer)], dst_ref=land_ref.at[_index(me if outgoing else peer)],
                send_sem=send_sems.at[sem], recv_sem=recv_sems.at[sem], device_id=peer, device_id_type=MESH))
    return copies


_HBM_SPEC = pl.BlockSpec(memory_space=pltpu.HBM)
_SEM_SPEC = pl.BlockSpec(memory_space=pltpu.SEMAPHORE)
_EFFECT = pltpu.SideEffectType.DATAFLOW_SIDE_EFFECTING


def _split_start(name, kind, xs, after=None):
    n = len(xs)
    extra = [] if after is None else [after]
    lands = [lax.empty(((N_DEV,) + x.shape) if kind == "gather" else x.shape, x.dtype) for x in xs]

    def body(*refs):
        x_refs, land_refs = refs[:n], refs[n:2 * n]
        send_sems, recv_sems = refs[2 * n + len(extra)], refs[2 * n + len(extra) + 1]
        token = refs[-1]
        for cp in _split_copies(kind, True, x_refs, land_refs, send_sems, recv_sems):
            cp.start()
        token[...] = jnp.zeros_like(token)

    hbm = [pltpu.HBM(a.shape, a.dtype) for a in list(xs) + lands]
    res = pl.pallas_call(
        body, name=name,
        out_shape=[pltpu.SemaphoreType.DMA((n * (N_DEV - 1),)), pltpu.SemaphoreType.DMA((n * (N_DEV - 1),))] + hbm + [_sds((8, 128), F32)],
        in_specs=[_HBM_SPEC] * (2 * n) + [pl.BlockSpec(memory_space=pl.ANY)] * len(extra),
        out_specs=[_SEM_SPEC, _SEM_SPEC] + [_HBM_SPEC] * (2 * n) + [pl.BlockSpec(memory_space=pltpu.VMEM)],
        input_output_aliases={j: 2 + j for j in range(2 * n)}, compiler_params=pltpu.CompilerParams(has_side_effects=_EFFECT),
    )(*[pltpu.with_memory_space_constraint(a, pltpu.HBM) for a in list(xs) + lands], *extra)
    return (kind, n, res[0], res[1], res[2:2 + 2 * n]), res[-1]


def _split_wait(name, state, after):
    kind, n, send_sems_in, recv_sems_in, thru = state

    def body(*refs):
        x_refs, land_refs = refs[:n], refs[n:2 * n]
        send_sems, recv_sems = refs[2 * n], refs[2 * n + 1]
        for cp in _split_copies(kind, True, x_refs, land_refs, send_sems, recv_sems):
            cp.wait_send()
        for cp in _split_copies(kind, False, x_refs, land_refs, send_sems, recv_sems):
            cp.wait_recv()

    res = pl.pallas_call(
        body, name=name, out_shape=[pltpu.HBM(a.shape, a.dtype) for a in thru],
        in_specs=[_HBM_SPEC] * (2 * n) + [_SEM_SPEC, _SEM_SPEC, pl.BlockSpec(memory_space=pl.ANY)], out_specs=[_HBM_SPEC] * (2 * n),
        input_output_aliases={j: j for j in range(2 * n)}, compiler_params=pltpu.CompilerParams(has_side_effects=_EFFECT),
    )(*thru, send_sems_in, recv_sems_in, after)
    me = _index(_mesh_pos())
    out = []
    for x, land in zip(res[:n], res[n:]):
        own = x if kind == "gather" else lax.dynamic_index_in_dim(x, me, 0, keepdims=False)
        out.append(lax.dynamic_update_slice(land, own[None], (me,) + (0,) * own.ndim))
    return out


def _rope_tables(s):
    inv = 1.0 / (ROPE_THETA ** (jnp.arange(0, QK_ROPE, 2, dtype=F32) / QK_ROPE))
    ang = jnp.arange(s, dtype=F32)[:, None] * inv[None, :]
    pad = jnp.zeros((s, KV_LORA - QK_ROPE), F32)
    return (jnp.concatenate([jnp.cos(ang), jnp.cos(ang), pad], axis=1), jnp.concatenate([jnp.sin(ang), jnp.sin(ang), pad], axis=1))


def _row_of(v):
    return v.reshape(1, -1)


def kernel(x, c, ada_w, ada_b, norm_g, ffn_w_in, ffn_w_out, pool_w, pool_b, pool_scale, mla_w_in, mla_q_norm, mla_kv_norm, mla_w_uq, mla_w_uk, mla_w_uv, mla_w_o, loss_target, m_ada_w, m_ada_b, m_norm_g, m_ffn_w_in, m_ffn_w_out, m_pool_w, m_pool_b, m_pool_scale, m_mla_w_in, m_mla_q_norm, m_mla_kv_norm, m_mla_w_uq, m_mla_w_uk, m_mla_w_uv, m_mla_w_o, v_ada_w, v_ada_b, v_norm_g, v_ffn_w_in, v_ffn_w_out, v_pool_w, v_pool_b, v_pool_scale, v_mla_w_in, v_mla_q_norm, v_mla_kv_norm, v_mla_w_uq, v_mla_w_uk, v_mla_w_uv, v_mla_w_o):
    s, d = x.shape[1], x.shape[2]
    tm = min(512, s)
    tr = min(512, s)
    tf = min(1024, s)
    me = 4 * lax.axis_index("x") + 2 * lax.axis_index("y") + lax.axis_index("c")
    x0 = x.reshape(s, d)
    target = loss_target.reshape(s, d)
    n_mod = ada_w.shape[2] * N_DEV // d
    mod_blk = ada_w.shape[2]

    small = jnp.concatenate([c.reshape(-1), norm_g.reshape(-1), pool_b.reshape(-1), mla_q_norm.reshape(-1)]).reshape(1, -1)
    w_in_t, m_in_t, v_in_t = (jnp.swapaxes(a, 2, 3) for a in (ffn_w_in, m_ffn_w_in, v_ffn_w_in))
    w_in_loc = [w_in_t[i, f].astype(BF16) for i in range(2) for f in range(2)]
    w_out_loc = [ffn_w_out[i, f].astype(BF16) for i in range(2) for f in range(2)]
    (small_all,) = _all_gather("gather_small", [small])
    small_all = small_all.reshape(N_DEV, -1)
    c_all = small_all[:, :d]
    off = d
    g_all = small_all[:, off:off + 12 * (d // N_DEV)].reshape(N_DEV, 2, 6, d // N_DEV).transpose(1, 2, 0, 3).reshape(2, 6, d)
    off += 12 * (d // N_DEV)
    pool_b_all = small_all[:, off:off + 4 * 32].reshape(N_DEV, 4, 32).transpose(1, 0, 2).reshape(1, d)
    off += 4 * 32
    q_norm_all = small_all[:, off:off + 32].reshape(1, Q_LORA)
    kv_norm_row = mla_kv_norm.reshape(1, KV_LORA)
    pscale_row = pool_scale.reshape(1, d)

    even = (jnp.arange(N_HEADS) % 2 == 0)[:, None, None]
    cos_k, sin_k = _rope_tables(s)

    def mla_weights(mla_w_in_all, mla_w_uq_all, mla_w_o_all):
        uq = mla_w_uq_all.reshape(Q_LORA, N_HEADS, QK_NOPE + QK_ROPE).transpose(1, 0, 2)
        zq = jnp.zeros((N_HEADS, Q_LORA, QK_NOPE), BF16)
        wq = jnp.concatenate(
            [uq[:, :, :QK_NOPE], zq, uq[:, :, QK_NOPE:], jnp.zeros((N_HEADS, Q_LORA, QK_PAD - KV_LORA - QK_ROPE), BF16)], axis=2)
        wukp = jnp.pad(mla_w_uk[0].transpose(1, 2, 0).astype(BF16), ((0, 0), (0, QK_PAD - QK_NOPE), (0, QK_PAD - KV_LORA)))
        uv = mla_w_uv[0].transpose(1, 0, 2).astype(BF16)
        wuv2 = jnp.where(even, jnp.concatenate([uv, jnp.zeros_like(uv)], axis=2), jnp.concatenate([jnp.zeros_like(uv), uv], axis=2))
        return dict(w_in=jnp.pad(mla_w_in_all.reshape(d, -1), ((0, 0), (0, LAT_PAD - mla_w_in.shape[2]))), wq=wq, wukp=wukp,
                    wuv2=wuv2, w_o=mla_w_o_all.reshape(d, d), q_norm=q_norm_all, kv_norm=kv_norm_row)

    (sc_all,), _ = _rowmap("ada_silu", lambda cv: ((cv * jax.nn.sigmoid(cv),), ()), [(c_all, (N_DEV, d), lambda i: (0, 0))],
                           [(_sds((N_DEV, d), F32), (N_DEV, d), lambda i: (0, 0))], [], (1,))
    ada_b_loc = lax.dynamic_slice_in_dim(ada_b, me * mod_blk, mod_blk, axis=1).reshape(2, 1, mod_blk)
    m_pad = 2 * N_DEV
    modp = _matmul("ada_mod", jnp.pad(sc_all, ((0, m_pad - N_DEV), (0, 0))), ada_w, a_blk=(m_pad, d), a_map=lambda i, k: (0, 0),
                   b_blk=(None, d, mod_blk), b_map=lambda i, k: (i, 0, 0), o_shape=(2, m_pad, mod_blk), o_blk=(None, m_pad, mod_blk),
                   o_map=lambda i, k: (i, 0, 0), grid=(2, 1), contract=NN, out_dtype=F32, bias=ada_b_loc, bias_blk=(None, 1, mod_blk),
                   bias_map=lambda i, k: (i, 0, 0))[:, :N_DEV]
    modp_all, w_in_first, w_out_first, pool_w_all = _all_gather(
        "gather_first", [modp.reshape(2 * N_DEV, mod_blk), w_in_loc[0], w_out_loc[0], pool_w.reshape(-1, POOL_GROUP).astype(BF16)])
    w4 = pool_w_all.reshape(N_DEV, 4, 32, POOL_GROUP).transpose(1, 0, 2, 3).reshape(4, POOL_GROUP, POOL_GROUP)
    later_a = [w_in_loc[1], w_out_loc[1]]
    later_b = [w_in_loc[2], w_out_loc[2], mla_w_in[0].astype(BF16), mla_w_uq.reshape(mla_w_uq.shape[1], -1).astype(BF16),
               mla_w_o[0].astype(BF16)]
    later_c = [w_in_loc[3], w_out_loc[3]]
    state_a, token_a = _split_start("gather_start_a", "gather", later_a, after=modp_all)
    state_b, token_b = _split_start("gather_start_b", "gather", later_b, after=token_a)
    state_c, token_c = _split_start("gather_start_c", "gather", later_c, after=token_b)
    w_in8 = [w_in_first, None, None, None]
    w_out4 = [w_out_first.reshape(4, FF_BLK, d), None, None, None]
    mod = lax.dynamic_index_in_dim(modp_all.reshape(N_DEV, 2, N_DEV, mod_blk), me, axis=2, keepdims=False)
    mod = mod.transpose(1, 0, 2).reshape(2, n_mod, d) + token_c[0, 0]
    mod_rows = mod.reshape(2 * n_mod, 1, d)
    g_rows = g_all.reshape(12, 1, d)
    weights_of = (0.5, 1.0, 0.5, 0.5, 1.0, 0.5)

    def pre_rows(k, with_shift):
        rows = [_rowk(g_rows, 2 * k), _rowk(mod_rows, 3 * k + 1)]
        return rows + [_rowk(mod_rows, 3 * k)] if with_shift else rows

    def post_rows(k):
        return [_rowk(g_rows, 2 * k + 1), _rowk(mod_rows, 3 * k + 2)]

    def act_dtype(k):
        return F32 if k == 1 else BF16

    saved = []
    xs = x0
    mla_wts = None
    h = _prenorm("prenorm_0", xs, pre_rows(0, True), act_dtype(0), tr)
    for k in range(6):
        i, sub = divmod(k, 3)
        tag = f"{i}{sub}"
        if k == 2:
            lands = _split_wait("gather_wait_a", state_a, xs)
            w_in8[1], w_out4[1] = lands[0], lands[1].reshape(4, FF_BLK, d)
        if k == 3:
            lands = _split_wait("gather_wait_b", state_b, xs)
            w_in8[2], w_out4[2] = lands[0], lands[1].reshape(4, FF_BLK, d)
            mla_wts = mla_weights(*lands[2:])
        if k == 5:
            lands = _split_wait("gather_wait_c", state_c, xs)
            w_in8[3], w_out4[3] = lands[0], lands[1].reshape(4, FF_BLK, d)
        if sub != 1:
            u, extra = _ffn_fwd(tag, h, w_in8[2 * i + sub // 2], w_out4[2 * i + sub // 2], tf, tf // 2)
        elif i == 0:
            u, z, v = _pool_fwd(h, w4, pool_b_all, pscale_row, min(4 * tm, s))
            extra = (z, v)
        else:
            u, extra = _mla_fwd(h, mla_wts, cos_k, sin_k, tm)
        saved.append((xs, h, u, extra))
        if k < 5:
            xs, h = _norm_link(f"norm_link_{k + 1}", xs, u, post_rows(k), weights_of[k], pre_rows(k + 1, True), act_dtype(k + 1), tr)

    dx, du, (sq, dgate, dgpost) = _norm_loss("norm_loss", xs, u, target, post_rows(5), weights_of[5], tr)
    loss_part = (0.5 * jnp.sum(sq) / d).reshape(1, 1)

    d_mod = [None] * (2 * n_mod)
    d_g = [None] * 12
    d_mod[3 * 5 + 2], d_g[2 * 5 + 1] = dgate, dgpost
    sent = {}
    pool_grads = mla_grads = None

    def start_scatter(key, arrays):
        state, token = _split_start(f"scatter_start_{key}", "scatter", arrays)
        sent[key] = state
        return token

    for k in (5, 4, 3, 2, 1, 0):
        i, sub = divmod(k, 3)
        tag = f"{i}{sub}"
        xin, h, u, extra = saved[k]
        token = None
        if sub != 1:
            f = 2 * i + sub // 2
            dh, dgu, act = _ffn_bwd_act(tag, du, extra, w_in8[f], w_out4[f], tf, tf // 4)
            dw_out = _ffn_dw(f"ffn_dwout_{tag}", act, du, tf).reshape(N_DEV, FF_BLK // 2, d)
            if k == 0:
                d_pool_w = pool_grads[0].reshape(4, N_DEV, 32, POOL_GROUP).transpose(1, 0, 2, 3).reshape(N_DEV, 4 * 32, POOL_GROUP)
                token = start_scatter(tag + "_out", [dw_out, d_pool_w])
                token = start_scatter(tag + "_in", [_ffn_dw(f"ffn_dwin_{tag}", dgu, h, tf, after=token)])
            else:
                token = start_scatter(tag, [_ffn_dw(f"ffn_dwin_{tag}", dgu, h, tf), dw_out])
        elif i == 0:
            dh, dw4, (dpscale, dpb) = _pool_bwd(du, extra[0], extra[1], w4, pscale_row, min(4 * tm, s))
            pool_grads = (dw4, dpscale, dpb)
        else:
            dh, mla_grads = _mla_bwd(du, h, extra, mla_wts, cos_k, sin_k, tm)
            dwq = mla_grads["wq"]
            d_uq = jnp.concatenate([dwq[:, :, :QK_NOPE], dwq[:, :, KV_LORA:KV_LORA + QK_ROPE]], axis=2).transpose(1, 0, 2)
            token = start_scatter("mla", [mla_grads["w_in"][:, :mla_w_in.shape[2]].reshape(N_DEV, d // N_DEV, -1),
                                          d_uq.reshape(N_DEV, Q_LORA // N_DEV, -1), mla_grads["w_o"].reshape(N_DEV, d // N_DEV, d)])
            dwukp, dwuv2 = mla_grads["wukp"], mla_grads["wuv2"]
            d_uk = dwukp[:, :QK_NOPE, :KV_LORA].transpose(2, 0, 1).reshape(KV_LORA, -1)
            d_uv = jnp.where(even, dwuv2[:, :, :V_HEAD], dwuv2[:, :, V_HEAD:]).transpose(1, 0, 2).reshape(KV_LORA, -1)
            state_ukv, token = _split_start("gather_start_ukv", "gather", [d_uk, d_uv], after=token)
        if k > 0:
            dx, du, reds = _norm_link_bwd(f"norm_link_bwd_{k}", dh, xin, dx, saved[k - 1][2], pre_rows(k, False), post_rows(k - 1),
                                          weights_of[k - 1], act_dtype(k - 1), tr, after=token)
            d_mod[3 * (k - 1) + 2], d_g[2 * (k - 1) + 1] = reds[3], reds[4]
        else:
            dx, reds = _prenorm_bwd("prenorm_bwd_0", dh, xin, dx, pre_rows(0, False), tr, after=token)
        d_mod[3 * k], d_mod[3 * k + 1], d_g[2 * k] = reds[0], reds[1], reds[2]
    grad_x = dx.reshape(x.shape)

    def upd(name, parts, w, m, v):
        shape = w.shape
        r, cdim = parts.shape[1], parts.shape[2]
        return [o.reshape(shape) for o in _adamw(name, parts, w.reshape(r, cdim), m.reshape(r, cdim), v.reshape(r, cdim))]

    def landed(key, after):
        return _split_wait(f"scatter_wait_{key}", sent[key], after)

    res = {}
    w_in_s, m_in_s, v_in_s = (a.reshape(4, FF_BLK, d) for a in (w_in_t, m_in_t, v_in_t))
    w_out_s, m_out_s, v_out_s = (a.reshape(4, FF_BLK // 2, d) for a in (ffn_w_out, m_ffn_w_out, v_ffn_w_out))
    bufs_in = [lax.empty(w_in_s.shape, F32) for _ in range(4)]
    bufs_out = [lax.empty(w_out_s.shape, F32) for _ in range(4)]
    for key, k in (("12", 3), ("mla", None), ("10", 2), ("02", 1)):
        parts = landed(key, grad_x)
        if k is None:
            res["mla_w_in"] = upd("adam_mla_w_in", parts[0], mla_w_in, m_mla_w_in, v_mla_w_in)
            res["mla_w_uq"] = upd("adam_mla_w_uq", parts[1], mla_w_uq, m_mla_w_uq, v_mla_w_uq)
            res["mla_w_o"] = upd("adam_mla_w_o", parts[2], mla_w_o, m_mla_w_o, v_mla_w_o)
            uk_all, uv_all = _split_wait("gather_wait_ukv", state_ukv, grad_x)
            res["mla_w_uk"] = upd("adam_mla_w_uk", uk_all, mla_w_uk, m_mla_w_uk, v_mla_w_uk)
            res["mla_w_uv"] = upd("adam_mla_w_uv", uv_all, mla_w_uv, m_mla_w_uv, v_mla_w_uv)
            continue
        bufs_in = _adamw_slab(f"adam_ffn_w_in_{key}", parts[0], w_in_s, m_in_s, v_in_s, bufs_in, k)
        bufs_out = _adamw_slab(f"adam_ffn_w_out_{key}", parts[1], w_out_s, m_out_s, v_out_s, bufs_out, k)

    dw4, dpscale, dpb = pool_grads
    small_g = jnp.concatenate(d_mod + d_g + [dpb, dpscale, mla_grads["q_norm"], mla_grads["kv_norm"], loss_part], axis=1)
    done = [bufs_in[0], bufs_out[0], res["mla_w_o"][0], res["mla_w_uv"][0]]
    (small_g_all,) = _all_gather("gather_small_grads", [small_g], after=sum(a.reshape(-1)[:1] for a in done))
    small_g_all = small_g_all.reshape(N_DEV, -1)
    n_m = 2 * n_mod * d
    d_mod_all = small_g_all[:, :n_m].reshape(N_DEV, 2, n_mod * d)
    rest = small_g_all[:, n_m:]
    p_norm_g = lax.dynamic_slice_in_dim(rest[:, :12 * d].reshape(N_DEV, 12, d), me * (d // N_DEV), d // N_DEV, axis=2)
    p_pool_b = lax.dynamic_slice_in_dim(rest[:, 12 * d:13 * d].reshape(N_DEV, 4, POOL_GROUP), me * 32, 32, axis=2)
    p_pool_scale = rest[:, 13 * d:14 * d].reshape(N_DEV, 1, d)
    p_q_norm = lax.dynamic_slice_in_dim(rest[:, 14 * d:14 * d + Q_LORA], me * 32, 32, axis=1).reshape(N_DEV, 1, 32)
    p_kv_norm = rest[:, 14 * d + Q_LORA:14 * d + Q_LORA + KV_LORA].reshape(N_DEV, 1, KV_LORA)
    loss = jnp.sum(rest[:, -1])

    d_mod_loc = lax.dynamic_slice_in_dim(d_mod_all, me * mod_blk, mod_blk, axis=2).transpose(1, 0, 2)
    k_pad = 128
    sc_t = jnp.pad(sc_all.T, ((0, 0), (0, k_pad - N_DEV)))
    d_ada_w = _matmul("ada_dw", sc_t, jnp.pad(d_mod_loc, ((0, 0), (0, k_pad - N_DEV), (0, 0))), a_blk=(d, k_pad),
                      a_map=lambda i, k: (0, 0), b_blk=(None, k_pad, mod_blk), b_map=lambda i, k: (i, 0, 0), o_shape=(2, d, mod_blk),
                      o_blk=(None, d, mod_blk), o_map=lambda i, k: (i, 0, 0), grid=(2, 1), contract=NN, out_dtype=F32)
    res["ada_w"] = upd("adam_ada_w", d_ada_w.reshape(1, 2 * d, mod_blk), ada_w, m_ada_w, v_ada_w)
    res["ada_b"] = upd("adam_ada_b", d_mod_all.reshape(N_DEV, 2, n_mod * d), ada_b, m_ada_b, v_ada_b)
    res["norm_g"] = upd("adam_norm_g", p_norm_g, norm_g, m_norm_g, v_norm_g)
    res["pool_b"] = upd("adam_pool_b", p_pool_b, pool_b, m_pool_b, v_pool_b)
    res["pool_scale"] = upd("adam_pool_scale", p_pool_scale, pool_scale, m_pool_scale, v_pool_scale)
    res["mla_q_norm"] = upd("adam_mla_q_norm", p_q_norm, mla_q_norm, m_mla_q_norm, v_mla_q_norm)
    res["mla_kv_norm"] = upd("adam_mla_kv_norm", p_kv_norm, mla_kv_norm, m_mla_kv_norm, v_mla_kv_norm)

    p_out, p_pool_w = landed("00_out", res["ada_w"][1])
    bufs_out = _adamw_slab("adam_ffn_w_out_00", p_out, w_out_s, m_out_s, v_out_s, bufs_out, 0)
    res["pool_w"] = upd("adam_pool_w", p_pool_w, pool_w, m_pool_w, v_pool_w)
    (p_in,) = landed("00_in", res["pool_w"][1])
    bufs_in = _adamw_slab("adam_ffn_w_in_00", p_in, w_in_s, m_in_s, v_in_s, bufs_in, 0)
    res["ffn_w_in"] = [jnp.swapaxes(b.reshape(w_in_t.shape), 2, 3) for b in bufs_in]
    res["ffn_w_out"] = [b.reshape(ffn_w_out.shape) for b in bufs_out]

    order = ["ada_w", "ada_b", "norm_g", "ffn_w_in", "ffn_w_out", "pool_w", "pool_b", "pool_scale", "mla_w_in", "mla_q_norm",
             "mla_kv_norm", "mla_w_uq", "mla_w_uk", "mla_w_uv", "mla_w_o"]
    outs = [loss, grad_x]
    for j in range(4):
        outs += [res[name][j] for name in order]
    return tuple(outs)
```

```python
import functools

import jax
import jax.numpy as jnp
from jax import lax
from jax.experimental import pallas as pl
from jax.experimental.pallas import tpu as pltpu

F32 = jnp.float32
BF16 = jnp.bfloat16
N_DEV = 8
AXES = ("x", "y", "c")
MESH = pl.DeviceIdType.MESH

D_MODEL = 1024
N_HEADS = 16
QK_NOPE = 64
QK_ROPE = 32
V_HEAD = 64
Q_LORA = 256
KV_LORA = 128
LAT_PAD = 512
QK_PAD = 256
D_FF = 2816
FF_BLK = 2 * D_FF // N_DEV
POOL_WINDOWS = (2, 4, 8, 16)
POOL_GROUP = 256
ROPE_THETA = 10000.0
EPS = 1e-6
ATTN_SCALE = (QK_NOPE + QK_ROPE) ** -0.5
LOG2_E = 1.4426950408889634
ADAM_LR, ADAM_B1, ADAM_B2, ADAM_EPS, ADAM_WD, ADAM_STEP = 0.001, 0.9, 0.999, 1e-08, 0.01, 10
VMEM_LIMIT = 56 * 1024 * 1024

NN = ((1,), (0,))
NT = ((1,), (1,))
TN = ((0,), (0,))


def _params(**kw):
    return pltpu.CompilerParams(vmem_limit_bytes=VMEM_LIMIT, **kw)


def _dot(a, b, contract):
    return lax.dot_general(a, b, (contract, ((), ())), preferred_element_type=F32)


def _matmul(name, a, b, *, a_blk, a_map, b_blk, b_map, o_shape, o_blk, o_map, grid, contract, out_dtype,
            bias=None, bias_blk=None, bias_map=None, after=None):
    n_k = grid[-1]
    k_axis = len(grid) - 1
    acc_shape = tuple(d for d in o_blk if d is not None)

    def body(*refs):
        a_ref, b_ref = refs[:2]
        bias_ref = refs[2] if bias is not None else None
        if n_k == 1:
            r = _dot(a_ref[...].astype(BF16), b_ref[...].astype(BF16), contract)
            if bias is not None:
                r = r + bias_ref[...]
            refs[-1][...] = r.astype(refs[-1].dtype)
            return
        o_ref, acc = refs[-2:]
        k = pl.program_id(k_axis)

        @pl.when(k == 0)
        def _():
            acc[...] = jnp.zeros_like(acc)

        acc[...] += _dot(a_ref[...].astype(BF16), b_ref[...].astype(BF16), contract)

        @pl.when(k == n_k - 1)
        def _():
            r = acc[...]
            if bias is not None:
                r = r + bias_ref[...]
            o_ref[...] = r.astype(o_ref.dtype)

    in_specs = [pl.BlockSpec(a_blk, a_map), pl.BlockSpec(b_blk, b_map)]
    args = [a, b]
    if bias is not None:
        in_specs.append(pl.BlockSpec(bias_blk, bias_map))
        args.append(bias)
    if after is not None:
        in_specs.append(pl.BlockSpec(memory_space=pl.ANY))
        args.append(after)
    return pl.pallas_call(
        body, name=name, grid=grid, in_specs=in_specs, out_specs=pl.BlockSpec(o_blk, o_map),
        out_shape=jax.ShapeDtypeStruct(o_shape, out_dtype), scratch_shapes=[pltpu.VMEM(acc_shape, F32)] if n_k > 1 else [],
        compiler_params=_params(),
    )(*args)


def _rowmap(name, fn, ins, outs, reds, grid, after=None):
    n_in, n_out, n_red = len(ins), len(outs), len(reds)
    extra = [] if after is None else [after]

    def body(*refs):
        in_refs = refs[:n_in]
        out_refs = refs[n_in + len(extra):n_in + len(extra) + n_out]
        red_refs = refs[n_in + len(extra) + n_out:]
        out_vals, red_vals = fn(*[r[...] for r in in_refs])
        for r, v in zip(out_refs, out_vals):
            r[...] = v.astype(r.dtype)
        if n_red:
            first = pl.program_id(0) == 0
            for ax in range(1, len(grid)):
                first = jnp.logical_and(first, pl.program_id(ax) == 0)

            @pl.when(first)
            def _():
                for r in red_refs:
                    r[...] = jnp.zeros_like(r)

            for r, v in zip(red_refs, red_vals):
                r[...] += v

    res = pl.pallas_call(
        body, name=name, grid=grid,
        in_specs=[pl.BlockSpec(blk, imap) for _, blk, imap in ins] + [pl.BlockSpec(memory_space=pl.ANY)] * len(extra),
        out_specs=[pl.BlockSpec(blk, imap) for _, blk, imap in list(outs) + list(reds)],
        out_shape=[sds for sds, _, _ in list(outs) + list(reds)],
        compiler_params=_params(),
    )(*[a for a, _, _ in ins], *extra)
    return res[:n_out], res[n_out:]


def _sds(shape, dtype):
    return jax.ShapeDtypeStruct(shape, dtype)


def _tile(a, tm):
    return (a, (tm, a.shape[1]), lambda i: (i, 0))


def _row(a):
    return (a, (1, a.shape[1]), lambda i: (0, 0))


def _otile(n, c, dtype, tm):
    return (_sds((n, c), dtype), (tm, c), lambda i: (i, 0))


def _ored(c):
    return (_sds((1, c), F32), (1, c), lambda i: (0, 0))


def _colsum(v):
    return jnp.sum(v, axis=0, keepdims=True)


def _rstd(v):
    return lax.rsqrt(jnp.mean(v * v, axis=-1, keepdims=True) + EPS)


def _pre(xv, g, sc, sh):
    return xv * _rstd(xv) * g * (1.0 + sc) + sh


def _post(xv, uv, g, gt, weight):
    return xv + weight * (1.0 + gt) * (uv * _rstd(uv) * g)


def _post_bwd(dv, uv, g, gt, weight):
    r = _rstd(uv)
    un = uv * r
    dy = dv * (weight * (1.0 + gt))
    a = dy * g
    du = r * (a - un * jnp.mean(a * un, axis=-1, keepdims=True))
    return du, (_colsum(dv * (weight * (un * g))), _colsum(dy * un))


def _pre_bwd(dhv, xv, dv, g, sc):
    dhv = dhv.astype(F32)
    r = _rstd(xv)
    xn = xv * r
    b = dhv * (g * (1.0 + sc))
    dx = dv + r * (b - xn * jnp.mean(b * xn, axis=-1, keepdims=True))
    return dx, (_colsum(dhv), _colsum(dhv * (xn * g)), _colsum(dhv * ((1.0 + sc) * xn)))


def _rowk(rows, k):
    return (rows, (None, 1, rows.shape[2]), lambda i: (k, 0, 0))


def _prenorm(name, x, pre, out_dtype, tm):
    n, d = x.shape
    (h,), _ = _rowmap(name, lambda xv, g, sc, sh: ((_pre(xv, g, sc, sh),), ()), [_tile(x, tm), *pre], [_otile(n, d, out_dtype, tm)],
                      [], (n // tm,))
    return h


def _norm_link(name, x, u, post, weight, pre, out_dtype, tm):
    n, d = x.shape

    def fn(xv, uv, g, gt, g2, sc, sh):
        xn = _post(xv, uv, g, gt, weight)
        return (xn, _pre(xn, g2, sc, sh)), ()

    (xn, h), _ = _rowmap(name, fn, [_tile(x, tm), _tile(u, tm), *post, *pre], [_otile(n, d, F32, tm), _otile(n, d, out_dtype, tm)],
                         [], (n // tm,))
    return xn, h


def _norm_loss(name, x, u, target, post, weight, tm):
    n, d = x.shape

    def fn(xv, uv, tv, g, gt):
        e = _post(xv, uv, g, gt, weight) - tv
        dv = e * (1.0 / d)
        du, reds = _post_bwd(dv, uv, g, gt, weight)
        return (dv, du), (_colsum(e * e), *reds)

    (dx, du), reds = _rowmap(name, fn, [_tile(x, tm), _tile(u, tm), _tile(target, tm), *post],
                             [_otile(n, d, F32, tm), _otile(n, d, BF16, tm)], [_ored(d)] * 3, (n // tm,))
    return dx, du, reds


def _norm_link_bwd(name, dh, x, dout, u_prev, pre, post_prev, weight_prev, out_dtype, tm, after=None):
    n, d = x.shape

    def fn(dhv, xv, dv, uv, g, sc, g2, gt):
        dx, reds = _pre_bwd(dhv, xv, dv, g, sc)
        du, reds_prev = _post_bwd(dx, uv, g2, gt, weight_prev)
        return (dx, du), (*reds, *reds_prev)

    (dx, du), reds = _rowmap(name, fn, [_tile(dh, tm), _tile(x, tm), _tile(dout, tm), _tile(u_prev, tm), *pre, *post_prev],
                             [_otile(n, d, F32, tm), _otile(n, d, out_dtype, tm)], [_ored(d)] * 5, (n // tm,), after=after)
    return dx, du, reds


def _prenorm_bwd(name, dh, x, dout, pre, tm, after=None):
    n, d = x.shape

    def fn(dhv, xv, dv, g, sc):
        dx, reds = _pre_bwd(dhv, xv, dv, g, sc)
        return (dx,), reds

    (dx,), reds = _rowmap(name, fn, [_tile(dh, tm), _tile(x, tm), _tile(dout, tm), *pre], [_otile(n, d, F32, tm)], [_ored(d)] * 3,
                          (n // tm,), after=after)
    return dx, reds


def _ffn_fwd(tag, h, w_in8, w_out4, tm, sub):
    s, d = h.shape

    def body(h_ref, wg_ref, wu_ref, wo_ref, u_ref, gu_ref):
        @pl.when(pl.program_id(1) == 0)
        def _():
            u_ref[...] = jnp.zeros_like(u_ref)

        for r in range(tm // sub):
            rows = pl.ds(r * sub, sub)
            hv = h_ref[rows, :]
            gate = _dot(hv, wg_ref[...], NT)
            up = _dot(hv, wu_ref[...], NT)
            gu_ref[0, rows, :] = gate.astype(BF16)
            gu_ref[1, rows, :] = up.astype(BF16)
            u_ref[rows, :] += _dot((gate * jax.nn.sigmoid(gate) * up).astype(BF16), wo_ref[...], NN)

    w_blk = (None, FF_BLK, d)
    return pl.pallas_call(
        body, name=f"ffn_fwd_{tag}", grid=(s // tm, 4),
        in_specs=[pl.BlockSpec((tm, d), lambda i, j: (i, 0)), pl.BlockSpec(w_blk, lambda i, j: (j, 0, 0)),
                  pl.BlockSpec(w_blk, lambda i, j: (j + 4, 0, 0)), pl.BlockSpec((None, FF_BLK, d), lambda i, j: (j, 0, 0))],
        out_specs=[pl.BlockSpec((tm, d), lambda i, j: (i, 0)), pl.BlockSpec((2, None, tm, FF_BLK), lambda i, j: (0, j, i, 0))],
        out_shape=[_sds((s, d), F32), _sds((2, 4, s, FF_BLK), BF16)], compiler_params=_params(),
    )(h, w_in8, w_in8, w_out4)


def _ffn_bwd_act(tag, du, gu, w_in8, w_out4, tm, sub):
    s, d = du.shape

    def body(du_ref, gu_ref, wg_ref, wu_ref, wo_ref, dh_ref, dgu_ref, act_ref):
        @pl.when(pl.program_id(1) == 0)
        def _():
            dh_ref[...] = jnp.zeros_like(dh_ref)

        for r in range(tm // sub):
            rows = pl.ds(r * sub, sub)
            dact = _dot(du_ref[rows, :], wo_ref[...], NT)
            gate, up = gu_ref[0, rows, :].astype(F32), gu_ref[1, rows, :].astype(F32)
            sg = jax.nn.sigmoid(gate)
            silu = gate * sg
            dg = dact * up * (sg * (1.0 + gate * (1.0 - sg)))
            dup = dact * silu
            dgu_ref[0, :, rows] = dg.T.astype(BF16)
            dgu_ref[1, :, rows] = dup.T.astype(BF16)
            act_ref[:, rows] = (silu * up).T.astype(BF16)
            dh_ref[rows, :] += _dot(dg.astype(BF16), wg_ref[...], NN) + _dot(dup.astype(BF16), wu_ref[...], NN)

    w_blk = (None, FF_BLK, d)
    dh, dgu_t, act_t = pl.pallas_call(
        body, name=f"ffn_bwd_{tag}", grid=(s // tm, 4),
        in_specs=[pl.BlockSpec((tm, d), lambda i, j: (i, 0)), pl.BlockSpec((2, None, tm, FF_BLK), lambda i, j: (0, j, i, 0)),
                  pl.BlockSpec(w_blk, lambda i, j: (j, 0, 0)), pl.BlockSpec(w_blk, lambda i, j: (j + 4, 0, 0)),
                  pl.BlockSpec((None, FF_BLK, d), lambda i, j: (j, 0, 0))],
        out_specs=[pl.BlockSpec((tm, d), lambda i, j: (i, 0)), pl.BlockSpec((2, None, FF_BLK, tm), lambda i, j: (0, j, 0, i)),
                   pl.BlockSpec((None, FF_BLK, tm), lambda i, j: (j, 0, i))],
        out_shape=[_sds((s, d), F32), _sds((2, 4, FF_BLK, s), BF16), _sds((4, FF_BLK, s), BF16)], compiler_params=_params(),
    )(du, gu, w_in8, w_in8, w_out4)
    return dh, dgu_t.reshape(8, FF_BLK, s), act_t


def _ffn_dw(name, lhs_t, rhs, tk, after=None):
    n_g, _, s = lhs_t.shape
    d = rhs.shape[1]
    return _matmul(name, lhs_t, rhs, a_blk=(None, FF_BLK, tk), a_map=lambda g, k: (g, 0, k), b_blk=(tk, d), b_map=lambda g, k: (k, 0),
                   o_shape=(n_g, FF_BLK, d), o_blk=(None, FF_BLK, d), o_map=lambda g, k: (g, 0, 0), grid=(n_g, s // tk),
                   contract=NN, out_dtype=BF16, after=after)


def _window_sum(x, window, transpose):
    s = x.shape[0]
    t = lax.broadcasted_iota(jnp.int32, (s, 1), 0)
    half = window // 2
    cnt = jnp.minimum(t + half, s) - jnp.maximum(t - half, 0)
    inv = 1.0 / cnt.astype(F32)
    if transpose:
        x = x * inv
        offsets = range(-half + 1, half + 1)
    else:
        offsets = range(-half, half)
    acc = jnp.zeros_like(x)
    for o in offsets:
        shifted = x if o == 0 else pltpu.roll(x, (-o) % s, 0)
        valid = jnp.logical_and(t + o >= 0, t + o < s)
        acc = acc + jnp.where(valid, shifted, 0.0)
    return acc if transpose else acc * inv


def _pool_mix(name, x, transpose, out_dtype):
    s, d = x.shape

    def body(x_ref, o_ref):
        g = pl.program_id(0)
        for gi, window in enumerate(POOL_WINDOWS):
            @pl.when(g == gi)
            def _(window=window):
                xv = x_ref[...].astype(F32)
                o_ref[...] = (_window_sum(xv, window, transpose) - xv).astype(o_ref.dtype)

    return pl.pallas_call(
        body, name=name, grid=(len(POOL_WINDOWS),), in_specs=[pl.BlockSpec((s, POOL_GROUP), lambda g: (0, g))],
        out_specs=pl.BlockSpec((s, POOL_GROUP), lambda g: (0, g)), out_shape=_sds((s, d), out_dtype), compiler_params=_params(),
    )(x)


def _pool_fwd(h, w4, bias, pscale, tm):
    s, d = h.shape
    nt = s // tm
    z = _pool_mix("pool_mix", h, False, BF16)
    v = _matmul("pool_proj", z, w4, a_blk=(tm, POOL_GROUP), a_map=lambda i, g, k: (i, g), b_blk=(None, POOL_GROUP, POOL_GROUP),
                b_map=lambda i, g, k: (g, 0, 0), o_shape=(s, d), o_blk=(tm, POOL_GROUP), o_map=lambda i, g, k: (i, g),
                grid=(nt, 4, 1), contract=NN, out_dtype=F32, bias=bias, bias_blk=(1, POOL_GROUP), bias_map=lambda i, g, k: (0, g))
    tr = min(512, s)
    (u,), _ = _rowmap("pool_scale", lambda vv, ps: ((vv * ps,), ()), [_tile(v, tr), _row(pscale)], [_otile(s, d, F32, tr)], [],
                      (s // tr,))
    return u, z, v


def _pool_bwd(du, z, v, w4, pscale, tm):
    s, d = du.shape
    nt = s // tm

    def fn(duv, vv, ps):
        dv = duv * ps
        return (dv,), (_colsum(duv * vv), _colsum(dv))

    tr = min(512, s)
    (dv,), reds = _rowmap("pool_dscale", fn, [_tile(du, tr), _tile(v, tr), _row(pscale)], [_otile(s, d, BF16, tr)],
                          [_ored(d), _ored(d)], (s // tr,))
    dw4 = _matmul("pool_dw", z, dv, a_blk=(tm, POOL_GROUP), a_map=lambda g, k: (k, g), b_blk=(tm, POOL_GROUP),
                  b_map=lambda g, k: (k, g), o_shape=(4, POOL_GROUP, POOL_GROUP), o_blk=(None, POOL_GROUP, POOL_GROUP),
                  o_map=lambda g, k: (g, 0, 0), grid=(4, nt), contract=TN, out_dtype=F32)
    dz = _matmul("pool_dz", dv, w4, a_blk=(tm, POOL_GROUP), a_map=lambda i, g, k: (i, g), b_blk=(None, POOL_GROUP, POOL_GROUP),
                 b_map=lambda i, g, k: (g, 0, 0), o_shape=(s, d), o_blk=(tm, POOL_GROUP), o_map=lambda i, g, k: (i, g),
                 grid=(nt, 4, 1), contract=NT, out_dtype=F32)
    dh = _pool_mix("pool_mix_t", dz, True, F32)
    return dh, dw4, reds


def _lane(shape):
    return lax.broadcasted_iota(jnp.int32, shape, 1)


def _rope_swap(v, transpose):
    half = QK_ROPE // 2
    lane = _lane(v.shape)
    up = pltpu.roll(v, v.shape[1] - half, 1)
    down = pltpu.roll(v, half, 1)
    if transpose:
        return jnp.where(lane < half, up, jnp.where(lane < QK_ROPE, -down, 0.0))
    return jnp.where(lane < half, -up, jnp.where(lane < QK_ROPE, down, 0.0))


def _rope(v, cos, sin):
    return v * cos + _rope_swap(v, False) * sin


def _rope_t(g, cos, sin):
    return g * cos + _rope_swap(g * sin, True)


def _mla_mid(lat, q_norm, kv_norm, cos_k, sin_k, tm):
    s = lat.shape[0]

    def fn(lv, qn, kn, cs, sn):
        cq = lv[:, :Q_LORA]
        ckv = lv[:, Q_LORA:Q_LORA + KV_LORA]
        kr = lv[:, Q_LORA + KV_LORA:]
        cq = cq * _rstd(cq) * qn
        ckv = ckv * _rstd(ckv) * kn
        return (cq, jnp.concatenate([ckv, _rope(kr, cs, sn)], axis=1)), ()

    (cq, kcat), _ = _rowmap("mla_mid", fn, [_tile(lat, tm), _row(q_norm), _row(kv_norm), _tile(cos_k, tm), _tile(sin_k, tm)],
                            [_otile(s, Q_LORA, BF16, tm), _otile(s, QK_PAD, BF16, tm)], [], (s // tm,))
    return cq, kcat


def _mla_mid_bwd(lat, dcq, dkcat, dv, q_norm, kv_norm, cos_k, sin_k, tm):
    s = lat.shape[0]

    def fn(lv, dq, dk, dvv, qn, kn, cs, sn):
        dk = dk * (1.0 / LOG2_E)
        cq = lv[:, :Q_LORA]
        ckv = lv[:, Q_LORA:Q_LORA + KV_LORA]
        rq, rk = _rstd(cq), _rstd(ckv)
        cqn, ckn = cq * rq, ckv * rk
        a = dq * qn
        d_cq = rq * (a - cqn * jnp.mean(a * cqn, axis=-1, keepdims=True))
        dckv = dk[:, :KV_LORA] + dvv
        a2 = dckv * kn
        d_ckv = rk * (a2 - ckn * jnp.mean(a2 * ckn, axis=-1, keepdims=True))
        d_kr = _rope_t(dk[:, KV_LORA:], cs, sn)
        return (jnp.concatenate([d_cq, d_ckv, d_kr], axis=1),), (_colsum(dq * cqn), _colsum(dckv * ckn))

    (dlat,), reds = _rowmap(
        "mla_mid_bwd", fn,
        [_tile(lat, tm), _tile(dcq, tm), _tile(dkcat, tm), _tile(dv, tm), _row(q_norm), _row(kv_norm), _tile(cos_k, tm),
         _tile(sin_k, tm)],
        [_otile(s, LAT_PAD, BF16, tm)], [_ored(Q_LORA), _ored(KV_LORA)], (s // tm,))
    return dlat, reds


def _mla_q(cq, wq, wukp, cos_k, sin_k, tm):
    s = cq.shape[0]

    def body(cq_ref, wq_ref, wuk_ref, cos_ref, sin_ref, o_ref):
        cqv, cs, sn = cq_ref[...], cos_ref[...], sin_ref[...]
        for h in range(N_HEADS):
            aq = _dot(cqv, wq_ref[h], NN)
            qlat = _dot(aq.astype(BF16), wuk_ref[h], NN)
            roped = _rope(aq[:, KV_LORA:], cs, sn)
            o_ref[h] = (jnp.concatenate([qlat[:, :KV_LORA], roped], axis=1) * (ATTN_SCALE * LOG2_E)).astype(o_ref.dtype)

    wblk = pl.BlockSpec((N_HEADS, QK_PAD, QK_PAD), lambda i: (0, 0, 0))
    tblk = pl.BlockSpec((tm, KV_LORA), lambda i: (i, 0))
    return pl.pallas_call(
        body, name="mla_q", grid=(s // tm,),
        in_specs=[pl.BlockSpec((tm, Q_LORA), lambda i: (i, 0)), wblk, wblk, tblk, tblk],
        out_specs=pl.BlockSpec((N_HEADS, tm, QK_PAD), lambda i: (0, i, 0)), out_shape=_sds((N_HEADS, s, QK_PAD), BF16),
        compiler_params=_params(),
    )(cq, wq, wukp, cos_k, sin_k)


def _mla_q_bwd(cq, wq, wukp, cos_k, sin_k, dqcat, tm):
    s = cq.shape[0]

    def body(cq_ref, wq_ref, wuk_ref, cos_ref, sin_ref, dq_ref, dcq_ref, dwq_ref, dwuk_ref):
        @pl.when(pl.program_id(0) == 0)
        def _():
            dwq_ref[...] = jnp.zeros_like(dwq_ref)
            dwuk_ref[...] = jnp.zeros_like(dwuk_ref)

        cqv, cs, sn = cq_ref[...], cos_ref[...], sin_ref[...]
        d_cq = jnp.zeros((tm, Q_LORA), F32)
        for h in range(N_HEADS):
            aq = _dot(cqv, wq_ref[h], NN).astype(BF16)
            g = dq_ref[h].astype(F32) * ATTN_SCALE
            gl, gr = g[:, :KV_LORA], g[:, KV_LORA:]
            dqlat = jnp.concatenate([gl, jnp.zeros_like(gl)], axis=1).astype(BF16)
            d_rope = _rope_t(gr, cs, sn)
            daq = _dot(dqlat, wuk_ref[h], NT) + jnp.concatenate([jnp.zeros_like(d_rope), d_rope], axis=1)
            daq_b = daq.astype(BF16)
            dwuk_ref[h] += _dot(aq, dqlat, TN)
            dwq_ref[h] += _dot(cqv, daq_b, TN)
            d_cq = d_cq + _dot(daq_b, wq_ref[h], NT)
        dcq_ref[...] = d_cq

    wblk = pl.BlockSpec((N_HEADS, QK_PAD, QK_PAD), lambda i: (0, 0, 0))
    tblk = pl.BlockSpec((tm, KV_LORA), lambda i: (i, 0))
    return pl.pallas_call(
        body, name="mla_q_bwd", grid=(s // tm,),
        in_specs=[pl.BlockSpec((tm, Q_LORA), lambda i: (i, 0)), wblk, wblk, tblk, tblk,
                  pl.BlockSpec((N_HEADS, tm, QK_PAD), lambda i: (0, i, 0))],
        out_specs=[pl.BlockSpec((tm, Q_LORA), lambda i: (i, 0)), wblk, wblk],
        out_shape=[_sds((s, Q_LORA), F32), _sds((N_HEADS, QK_PAD, QK_PAD), F32), _sds((N_HEADS, QK_PAD, QK_PAD), F32)],
        compiler_params=_params(),
    )(cq, wq, wukp, cos_k, sin_k, dqcat)


def _flash_fwd(qcat, kcat, tq, tk):
    n_h, s, _ = qcat.shape
    n_k = s // tk

    def body(q_ref, k_ref, v_ref, o_ref, lse_ref):
        q = q_ref[...]
        m = jnp.full((tq, 1), -1e30, F32)
        l = jnp.zeros((tq, 1), F32)
        acc = jnp.zeros((tq, KV_LORA), F32)
        for kk in range(n_k):
            rows = pl.ds(kk * tk, tk)
            sc = _dot(q, k_ref[rows, :], NT)
            m_new = jnp.maximum(m, jnp.max(sc, axis=1, keepdims=True))
            alpha = jnp.exp2(m - m_new)
            p = jnp.exp2(sc - m_new)
            l = alpha * l + jnp.sum(p, axis=1, keepdims=True)
            acc = alpha * acc + _dot(p.astype(BF16), v_ref[rows, :], NN)
            m = m_new
        o_ref[...] = (acc / l).astype(o_ref.dtype)
        lse_ref[...] = m + jnp.log2(l)

    return pl.pallas_call(
        body, name="mla_attn", grid=(n_h, s // tq),
        in_specs=[pl.BlockSpec((None, tq, QK_PAD), lambda h, i: (h, i, 0)), pl.BlockSpec((s, QK_PAD), lambda h, i: (0, 0)),
                  pl.BlockSpec((s, KV_LORA), lambda h, i: (0, 0))],
        out_specs=[pl.BlockSpec((None, tq, KV_LORA), lambda h, i: (h, i, 0)), pl.BlockSpec((None, tq, 1), lambda h, i: (h, i, 0))],
        out_shape=[_sds((n_h, s, KV_LORA), BF16), _sds((n_h, s, 1), F32)], compiler_params=_params(),
    )(qcat, kcat, kcat)


def _flash_bwd(qcat, kcat, o, do, lse, tq, tk):
    n_h, s, _ = qcat.shape
    n_k = s // tk

    def body(q_ref, k_ref, v_ref, o_ref, do_ref, lse_ref, dq_ref, dk_ref, dv_ref, dq_acc):
        h, i = pl.program_id(0), pl.program_id(1)

        @pl.when(jnp.logical_and(h == 0, i == 0))
        def _():
            dk_ref[...] = jnp.zeros_like(dk_ref)
            dv_ref[...] = jnp.zeros_like(dv_ref)

        q = q_ref[...]
        dov = do_ref[...]
        lse_v = lse_ref[...]
        delta = jnp.sum(dov.astype(F32) * o_ref[...].astype(F32), axis=1, keepdims=True)
        dq_acc[...] = jnp.zeros_like(dq_acc)

        for kk in range(n_k):
            rows = pl.ds(kk * tk, tk)
            k = k_ref[rows, :]
            p = jnp.exp2(_dot(q, k, NT) - lse_v)
            dp = _dot(dov, v_ref[rows, :], NT)
            ds = (p * (dp - delta)).astype(BF16)
            dq_acc[...] += _dot(ds, k, NN)
            dv_ref[rows, :] += _dot(p.astype(BF16), dov, TN)
            dk_ref[rows, :] += _dot(ds, q, TN)
        dq_ref[...] = dq_acc[...].astype(dq_ref.dtype)

    qblk = pl.BlockSpec((None, tq, QK_PAD), lambda h, i: (h, i, 0))
    oblk = pl.BlockSpec((None, tq, KV_LORA), lambda h, i: (h, i, 0))
    return pl.pallas_call(
        body, name="mla_attn_bwd", grid=(n_h, s // tq),
        in_specs=[qblk, pl.BlockSpec((s, QK_PAD), lambda h, i: (0, 0)), pl.BlockSpec((s, KV_LORA), lambda h, i: (0, 0)), oblk, oblk,
                  pl.BlockSpec((None, tq, 1), lambda h, i: (h, i, 0))],
        out_specs=[qblk, pl.BlockSpec((s, QK_PAD), lambda h, i: (0, 0)), pl.BlockSpec((s, KV_LORA), lambda h, i: (0, 0))],
        out_shape=[_sds((n_h, s, QK_PAD), BF16), _sds((s, QK_PAD), F32), _sds((s, KV_LORA), F32)],
        scratch_shapes=[pltpu.VMEM((tq, QK_PAD), F32)], compiler_params=_params(),
    )(qcat, kcat, kcat, o, do, lse)


def _mla_uv(o_lat, wuv2, do, tm):
    n_h, s, _ = o_lat.shape
    d = n_h * V_HEAD
    pair = 2 * V_HEAD
    lat_blk = pl.BlockSpec((n_h, tm, KV_LORA), lambda i: (0, i, 0))
    w_blk = pl.BlockSpec((n_h, KV_LORA, pair), lambda i: (0, 0, 0))
    row_blk = pl.BlockSpec((tm, d), lambda i: (i, 0))

    if do is None:
        def body(a_ref, w_ref, o_ref):
            for p in range(n_h // 2):
                o_ref[:, p * pair:(p + 1) * pair] = (
                    _dot(a_ref[2 * p], w_ref[2 * p], NN) + _dot(a_ref[2 * p + 1], w_ref[2 * p + 1], NN)).astype(o_ref.dtype)

        return pl.pallas_call(body, name="mla_uv", grid=(s // tm,), in_specs=[lat_blk, w_blk], out_specs=row_blk,
                              out_shape=_sds((s, d), BF16), compiler_params=_params())(o_lat, wuv2)

    def body(a_ref, w_ref, do_ref, dlat_ref, dw_ref):
        @pl.when(pl.program_id(0) == 0)
        def _():
            dw_ref[...] = jnp.zeros_like(dw_ref)

        for h in range(n_h):
            dov = do_ref[:, (h // 2) * pair:(h // 2 + 1) * pair]
            dlat_ref[h] = _dot(dov, w_ref[h], NT).astype(dlat_ref.dtype)
            dw_ref[h] += _dot(a_ref[h], dov, TN)

    return pl.pallas_call(body, name="mla_uv_bwd", grid=(s // tm,), in_specs=[lat_blk, w_blk, row_blk], out_specs=[lat_blk, w_blk],
                          out_shape=[_sds((n_h, s, KV_LORA), BF16), _sds((n_h, KV_LORA, pair), F32)], compiler_params=_params(),
                          )(o_lat, wuv2, do)


def _mla_fwd(h, wts, cos_k, sin_k, tm):
    s, d = h.shape
    nt = s // tm
    lat = _matmul("mla_lat", h, wts["w_in"], a_blk=(tm, d), a_map=lambda i, k: (i, 0), b_blk=(d, LAT_PAD), b_map=lambda i, k: (0, 0),
                  o_shape=(s, LAT_PAD), o_blk=(tm, LAT_PAD), o_map=lambda i, k: (i, 0), grid=(nt, 1), contract=NN, out_dtype=F32)
    cq, kcat = _mla_mid(lat, wts["q_norm"], wts["kv_norm"], cos_k, sin_k, tm)
    qcat = _mla_q(cq, wts["wq"], wts["wukp"], cos_k, sin_k, tm)
    o_lat, lse = _flash_fwd(qcat, kcat, min(2 * tm, s), tm)
    o = _mla_uv(o_lat, wts["wuv2"], None, tm)
    u = _matmul("mla_out", o, wts["w_o"], a_blk=(tm, d), a_map=lambda i, k: (i, 0), b_blk=(d, d), b_map=lambda i, k: (0, 0),
                o_shape=(s, d), o_blk=(tm, d), o_map=lambda i, k: (i, 0), grid=(nt, 1), contract=NN, out_dtype=F32)
    return u, (lat, cq, kcat, qcat, o_lat, lse, o)


def _mla_bwd(du, h, saved, wts, cos_k, sin_k, tm):
    lat, cq, kcat, qcat, o_lat, lse, o = saved
    s, d = h.shape
    nt = s // tm
    do = _matmul("mla_do", du, wts["w_o"], a_blk=(tm, d), a_map=lambda i, k: (i, 0), b_blk=(d, d), b_map=lambda i, k: (0, 0),
                 o_shape=(s, d), o_blk=(tm, d), o_map=lambda i, k: (i, 0), grid=(nt, 1), contract=NT, out_dtype=BF16)
    dw_o = _matmul("mla_dwo", o, du, a_blk=(tm, d), a_map=lambda k: (k, 0), b_blk=(tm, d), b_map=lambda k: (k, 0),
                   o_shape=(d, d), o_blk=(d, d), o_map=lambda k: (0, 0), grid=(nt,), contract=TN, out_dtype=F32)
    do_lat, dwuv2 = _mla_uv(o_lat, wts["wuv2"], do, tm)
    dqcat, dkcat, dv = _flash_bwd(qcat, kcat, o_lat, do_lat, lse, min(2 * tm, s), tm)
    dcq, dwq, dwukp = _mla_q_bwd(cq, wts["wq"], wts["wukp"], cos_k, sin_k, dqcat, tm)
    dlat, (dqn, dkn) = _mla_mid_bwd(lat, dcq, dkcat, dv, wts["q_norm"], wts["kv_norm"], cos_k, sin_k, tm)
    dh = _matmul("mla_dh", dlat, wts["w_in"], a_blk=(tm, LAT_PAD), a_map=lambda i, k: (i, 0), b_blk=(d, LAT_PAD),
                 b_map=lambda i, k: (0, 0), o_shape=(s, d), o_blk=(tm, d), o_map=lambda i, k: (i, 0), grid=(nt, 1), contract=NT,
                 out_dtype=F32)
    dw_in = _matmul("mla_dwin", h, dlat, a_blk=(tm, d), a_map=lambda k: (k, 0), b_blk=(tm, LAT_PAD), b_map=lambda k: (k, 0),
                    o_shape=(d, LAT_PAD), o_blk=(d, LAT_PAD), o_map=lambda k: (0, 0), grid=(nt,), contract=TN, out_dtype=F32)
    return dh, dict(w_in=dw_in, wq=dwq, wukp=dwukp, wuv2=dwuv2, w_o=dw_o, q_norm=dqn, kv_norm=dkn)


def _adamw(name, parts, w, m, v):
    n_parts, r, c = parts.shape
    tr = r
    for cand in (256, 128, 64, 32, 16, 8):
        if r > cand and r % cand == 0:
            tr = cand
            break

    def body(p_ref, w_ref, m_ref, v_ref, g_ref, d_ref, nm_ref, nv_ref):
        g = p_ref[0].astype(F32)
        for k in range(1, n_parts):
            g = g + p_ref[k].astype(F32)
        nm = ADAM_B1 * m_ref[...] + (1.0 - ADAM_B1) * g
        nv = ADAM_B2 * v_ref[...] + (1.0 - ADAM_B2) * (g * g)
        m_hat = nm / (1.0 - ADAM_B1 ** ADAM_STEP)
        v_hat = nv / (1.0 - ADAM_B2 ** ADAM_STEP)
        g_ref[...] = g
        d_ref[...] = -ADAM_LR * (m_hat / (jnp.sqrt(v_hat) + ADAM_EPS) + ADAM_WD * w_ref[...])
        nm_ref[...] = nm
        nv_ref[...] = nv

    blk = pl.BlockSpec((tr, c), lambda i: (i, 0))
    return pl.pallas_call(
        body, name=name, grid=(r // tr,), in_specs=[pl.BlockSpec((n_parts, tr, c), lambda i: (0, i, 0)), blk, blk, blk],
        out_specs=[blk] * 4, out_shape=[_sds((r, c), F32)] * 4, compiler_params=_params(),
    )(parts, w, m, v)


def _adamw_slab(name, parts, w, m, v, bufs, f):
    n_parts, r, c = parts.shape
    tr = max(t for t in range(8, 257, 8) if r % t == 0)

    def body(p_ref, w_ref, m_ref, v_ref, *rest):
        g_ref, d_ref, nm_ref, nv_ref = rest[4:]
        g = p_ref[0].astype(F32)
        for k in range(1, n_parts):
            g = g + p_ref[k].astype(F32)
        nm = ADAM_B1 * m_ref[...] + (1.0 - ADAM_B1) * g
        nv = ADAM_B2 * v_ref[...] + (1.0 - ADAM_B2) * (g * g)
        m_hat = nm / (1.0 - ADAM_B1 ** ADAM_STEP)
        v_hat = nv / (1.0 - ADAM_B2 ** ADAM_STEP)
        g_ref[...] = g
        d_ref[...] = -ADAM_LR * (m_hat / (jnp.sqrt(v_hat) + ADAM_EPS) + ADAM_WD * w_ref[...])
        nm_ref[...] = nm
        nv_ref[...] = nv

    blk = pl.BlockSpec((None, tr, c), lambda i: (f, i, 0))
    return pl.pallas_call(
        body, name=name, grid=(r // tr,),
        in_specs=[pl.BlockSpec((n_parts, tr, c), lambda i: (0, i, 0)), blk, blk, blk] + [pl.BlockSpec(memory_space=pl.ANY)] * 4,
        out_specs=[blk] * 4, out_shape=[_sds(w.shape, F32)] * 4, input_output_aliases={4 + j: j for j in range(4)},
        compiler_params=_params(),
    )(parts, w, m, v, *bufs)


def _mesh_pos():
    return lax.axis_index("x"), lax.axis_index("y"), lax.axis_index("c")


def _flip(pos, mask):
    return tuple(1 - p if (mask >> (2 - b)) & 1 else p for b, p in enumerate(pos))


def _index(pos):
    return 4 * pos[0] + 2 * pos[1] + pos[2]


def _all_gather(name, xs, after=None):
    n = len(xs)
    extra = [] if after is None else [after]

    def body(*refs):
        x_refs, o_refs = refs[:n], refs[n + len(extra):2 * n + len(extra)]
        send_sems, recv_sems, local_sems = refs[2 * n + len(extra):]
        me = _mesh_pos()
        sibling = _flip(me, 1)
        others = [_flip(me, 4), _flip(me, 2), _flip(me, 6)]

        def copy(k, j, block, to, src=None):
            dst = o_refs[k].at[_index(block)]
            return pltpu.make_async_remote_copy(
                src_ref=dst if src is None else src, dst_ref=dst, send_sem=send_sems.at[k, j], recv_sem=recv_sems.at[k, j],
                device_id=to, device_id_type=MESH)

        local = [pltpu.make_async_copy(x_refs[k], o_refs[k].at[_index(me)], local_sems.at[k]) for k in range(n)]
        for cp in local:
            cp.start()
        first = []
        for k in range(n):
            first.append(copy(k, 0, me, sibling, src=x_refs[k]))
            first += [copy(k, 1 + j, me, other, src=x_refs[k]) for j, other in enumerate(others)]
        for cp in first:
            cp.start()
        passed = []
        for j, other in enumerate(others):
            for k in range(n):
                copy(k, 1 + j, other, me).wait_recv()
                cp = copy(k, 4 + j, other, sibling)
                cp.start()
                passed.append(cp)
        for k in range(n):
            copy(k, 0, sibling, me).wait_recv()
        for j, other in enumerate(others):
            for k in range(n):
                copy(k, 4 + j, _flip(other, 1), me).wait_recv()
        for cp in first + passed:
            cp.wait_send()
        for cp in local:
            cp.wait()

    any_spec = pl.BlockSpec(memory_space=pl.ANY)
    return pl.pallas_call(
        body, name=name, in_specs=[any_spec] * (n + len(extra)), out_specs=[any_spec] * n,
        out_shape=[_sds((N_DEV,) + x.shape, x.dtype) for x in xs],
        scratch_shapes=[pltpu.SemaphoreType.DMA((n, 7)), pltpu.SemaphoreType.DMA((n, 7)), pltpu.SemaphoreType.DMA((n,))],
    )(*xs, *extra)


def _all_to_all(name, groups):
    flat = [(gi, f) for gi, grp in enumerate(groups) for f in range(len(grp))]
    n = len(flat)
    n_groups = len(groups)

    def body(*refs):
        x_refs, o_refs = refs[:n], refs[n:n + n_groups]
        send_sems, recv_sems, local_sems = refs[n + n_groups:]
        me = _mesh_pos()
        local, sends, recvs = [], [], []
        for k, (gi, f) in enumerate(flat):
            local.append(pltpu.make_async_copy(x_refs[k].at[_index(me)], o_refs[gi].at[_index(me), f], local_sems.at[k]))
            for mask in range(1, N_DEV):
                peer = _flip(me, mask)
                sends.append(pltpu.make_async_remote_copy(
                    src_ref=x_refs[k].at[_index(peer)], dst_ref=o_refs[gi].at[_index(me), f], send_sem=send_sems.at[k, mask - 1],
                    recv_sem=recv_sems.at[k, mask - 1], device_id=peer, device_id_type=MESH))
                recvs.append(pltpu.make_async_remote_copy(
                    src_ref=x_refs[k].at[_index(me)], dst_ref=o_refs[gi].at[_index(peer), f], send_sem=send_sems.at[k, mask - 1],
                    recv_sem=recv_sems.at[k, mask - 1], device_id=peer, device_id_type=MESH))
        for cp in local + sends:
            cp.start()
        for cp in recvs:
            cp.wait_recv()
        for cp in sends:
            cp.wait_send()
        for cp in local:
            cp.wait()

    any_spec = pl.BlockSpec(memory_space=pl.ANY)
    return pl.pallas_call(
        body, name=name, in_specs=[any_spec] * n, out_specs=[any_spec] * n_groups,
        out_shape=[_sds((N_DEV, len(grp)) + grp[0].shape[1:], grp[0].dtype) for grp in groups],
        scratch_shapes=[pltpu.SemaphoreType.DMA((n, 7)), pltpu.SemaphoreType.DMA((n, 7)), pltpu.SemaphoreType.DMA((n,))],
    )(*[a for grp in groups for a in grp])


def _split_copies(kind, outgoing, x_refs, land_refs, send_sems, recv_sems):
    me = _mesh_pos()
    copies = []
    for k, (x_ref, land_ref) in enumerate(zip(x_refs, land_refs)):
        for mask in range(1, N_DEV):
            peer = _flip(me, mask)
            sem = k * (N_DEV - 1) + mask - 1
            copies.append(pltpu.make_async_remote_copy(
                src_ref=x_ref if kind == "gather" else x_ref.at[_index(peer)], dst_ref=land_ref.at[_index(me if outgoing else peer)],
                send_sem=send_sems.at[sem], recv_sem=recv_sems.at[sem], device_id=peer, device_id_type=MESH))
    return copies


_HBM_SPEC = pl.BlockSpec(memory_space=pltpu.HBM)
_SEM_SPEC = pl.BlockSpec(memory_space=pltpu.SEMAPHORE)
_EFFECT = pltpu.SideEffectType.DATAFLOW_SIDE_EFFECTING


def _split_start(name, kind, xs, after=None):
    n = len(xs)
    extra = [] if after is None else [after]
    lands = [lax.empty(((N_DEV,) + x.shape) if kind == "gather" else x.shape, x.dtype) for x in xs]

    def body(*refs):
        x_refs, land_refs = refs[:n], refs[n:2 * n]
        send_sems, recv_sems = refs[2 * n + len(extra)], refs[2 * n + len(extra) + 1]
        token = refs[-1]
        for cp in _split_copies(kind, True, x_refs, land_refs, send_sems, recv_sems):
            cp.start()
        token[...] = jnp.zeros_like(token)

    hbm = [pltpu.HBM(a.shape, a.dtype) for a in list(xs) + lands]
    res = pl.pallas_call(
        body, name=name,
        out_shape=[pltpu.SemaphoreType.DMA((n * (N_DEV - 1),)), pltpu.SemaphoreType.DMA((n * (N_DEV - 1),))] + hbm + [_sds((8, 128), F32)],
        in_specs=[_HBM_SPEC] * (2 * n) + [pl.BlockSpec(memory_space=pl.ANY)] * len(extra),
        out_specs=[_SEM_SPEC, _SEM_SPEC] + [_HBM_SPEC] * (2 * n) + [pl.BlockSpec(memory_space=pltpu.VMEM)],
        input_output_aliases={j: 2 + j for j in range(2 * n)}, compiler_params=pltpu.CompilerParams(has_side_effects=_EFFECT),
    )(*[pltpu.with_memory_space_constraint(a, pltpu.HBM) for a in list(xs) + lands], *extra)
    return (kind, n, res[0], res[1], res[2:2 + 2 * n]), res[-1]


def _split_wait(name, state, after):
    kind, n, send_sems_in, recv_sems_in, thru = state

    def body(*refs):
        x_refs, land_refs = refs[:n], refs[n:2 * n]
        send_sems, recv_sems = refs[2 * n], refs[2 * n + 1]
        for cp in _split_copies(kind, True, x_refs, land_refs, send_sems, recv_sems):
            cp.wait_send()
        for cp in _split_copies(kind, False, x_refs, land_refs, send_sems, recv_sems):
            cp.wait_recv()

    res = pl.pallas_call(
        body, name=name, out_shape=[pltpu.HBM(a.shape, a.dtype) for a in thru],
        in_specs=[_HBM_SPEC] * (2 * n) + [_SEM_SPEC, _SEM_SPEC, pl.BlockSpec(memory_space=pl.ANY)], out_specs=[_HBM_SPEC] * (2 * n),
        input_output_aliases={j: j for j in range(2 * n)}, compiler_params=pltpu.CompilerParams(has_side_effects=_EFFECT),
    )(*thru, send_sems_in, recv_sems_in, after)
    me = _index(_mesh_pos())
    out = []
    for x, land in zip(res[:n], res[n:]):
        own = x if kind == "gather" else lax.dynamic_index_in_dim(x, me, 0, keepdims=False)
        out.append(lax.dynamic_update_slice(land, own[None], (me,) + (0,) * own.ndim))
    return out


def _rope_tables(s):
    inv = 1.0 / (ROPE_THETA ** (jnp.arange(0, QK_ROPE, 2, dtype=F32) / QK_ROPE))
    ang = jnp.arange(s, dtype=F32)[:, None] * inv[None, :]
    pad = jnp.zeros((s, KV_LORA - QK_ROPE), F32)
    return (jnp.concatenate([jnp.cos(ang), jnp.cos(ang), pad], axis=1), jnp.concatenate([jnp.sin(ang), jnp.sin(ang), pad], axis=1))


def _row_of(v):
    return v.reshape(1, -1)


def kernel(x, c, ada_w, ada_b, norm_g, ffn_w_in, ffn_w_out, pool_w, pool_b, pool_scale, mla_w_in, mla_q_norm, mla_kv_norm, mla_w_uq, mla_w_uk, mla_w_uv, mla_w_o, loss_target, m_ada_w, m_ada_b, m_norm_g, m_ffn_w_in, m_ffn_w_out, m_pool_w, m_pool_b, m_pool_scale, m_mla_w_in, m_mla_q_norm, m_mla_kv_norm, m_mla_w_uq, m_mla_w_uk, m_mla_w_uv, m_mla_w_o, v_ada_w, v_ada_b, v_norm_g, v_ffn_w_in, v_ffn_w_out, v_pool_w, v_pool_b, v_pool_scale, v_mla_w_in, v_mla_q_norm, v_mla_kv_norm, v_mla_w_uq, v_mla_w_uk, v_mla_w_uv, v_mla_w_o):
    s, d = x.shape[1], x.shape[2]
    tm = min(512, s)
    tr = min(512, s)
    tf = min(1024, s)
    tw = min(2048, s)
    me = 4 * lax.axis_index("x") + 2 * lax.axis_index("y") + lax.axis_index("c")
    x0 = x.reshape(s, d)
    target = loss_target.reshape(s, d)
    n_mod = ada_w.shape[2] * N_DEV // d
    mod_blk = ada_w.shape[2]

    small = jnp.concatenate([c.reshape(-1), norm_g.reshape(-1), pool_b.reshape(-1), mla_q_norm.reshape(-1)]).reshape(1, -1)
    w_in_t, m_in_t, v_in_t = (jnp.swapaxes(a, 2, 3) for a in (ffn_w_in, m_ffn_w_in, v_ffn_w_in))
    w_in_loc = [w_in_t[i, f].astype(BF16) for i in range(2) for f in range(2)]
    w_out_loc = [ffn_w_out[i, f].astype(BF16) for i in range(2) for f in range(2)]
    (small_all,) = _all_gather("gather_small", [small])
    small_all = small_all.reshape(N_DEV, -1)
    c_all = small_all[:, :d]
    off = d
    g_all = small_all[:, off:off + 12 * (d // N_DEV)].reshape(N_DEV, 2, 6, d // N_DEV).transpose(1, 2, 0, 3).reshape(2, 6, d)
    off += 12 * (d // N_DEV)
    pool_b_all = small_all[:, off:off + 4 * 32].reshape(N_DEV, 4, 32).transpose(1, 0, 2).reshape(1, d)
    off += 4 * 32
    q_norm_all = small_all[:, off:off + 32].reshape(1, Q_LORA)
    kv_norm_row = mla_kv_norm.reshape(1, KV_LORA)
    pscale_row = pool_scale.reshape(1, d)

    even = (jnp.arange(N_HEADS) % 2 == 0)[:, None, None]
    cos_k, sin_k = _rope_tables(s)

    def mla_weights(mla_w_in_all, mla_w_uq_all, mla_w_o_all):
        uq = mla_w_uq_all.reshape(Q_LORA, N_HEADS, QK_NOPE + QK_ROPE).transpose(1, 0, 2)
        zq = jnp.zeros((N_HEADS, Q_LORA, QK_NOPE), BF16)
        wq = jnp.concatenate(
            [uq[:, :, :QK_NOPE], zq, uq[:, :, QK_NOPE:], jnp.zeros((N_HEADS, Q_LORA, QK_PAD - KV_LORA - QK_ROPE), BF16)], axis=2)
        wukp = jnp.pad(mla_w_uk[0].transpose(1, 2, 0).astype(BF16), ((0, 0), (0, QK_PAD - QK_NOPE), (0, QK_PAD - KV_LORA)))
        uv = mla_w_uv[0].transpose(1, 0, 2).astype(BF16)
        wuv2 = jnp.where(even, jnp.concatenate([uv, jnp.zeros_like(uv)], axis=2), jnp.concatenate([jnp.zeros_like(uv), uv], axis=2))
        return dict(w_in=jnp.pad(mla_w_in_all.reshape(d, -1), ((0, 0), (0, LAT_PAD - mla_w_in.shape[2]))), wq=wq, wukp=wukp,
                    wuv2=wuv2, w_o=mla_w_o_all.reshape(d, d), q_norm=q_norm_all, kv_norm=kv_norm_row)

    (sc_all,), _ = _rowmap("ada_silu", lambda cv: ((cv * jax.nn.sigmoid(cv),), ()), [(c_all, (N_DEV, d), lambda i: (0, 0))],
                           [(_sds((N_DEV, d), F32), (N_DEV, d), lambda i: (0, 0))], [], (1,))
    ada_b_loc = lax.dynamic_slice_in_dim(ada_b, me * mod_blk, mod_blk, axis=1).reshape(2, 1, mod_blk)
    m_pad = 2 * N_DEV
    modp = _matmul("ada_mod", jnp.pad(sc_all, ((0, m_pad - N_DEV), (0, 0))), ada_w, a_blk=(m_pad, d), a_map=lambda i, k: (0, 0),
                   b_blk=(None, d, mod_blk), b_map=lambda i, k: (i, 0, 0), o_shape=(2, m_pad, mod_blk), o_blk=(None, m_pad, mod_blk),
                   o_map=lambda i, k: (i, 0, 0), grid=(2, 1), contract=NN, out_dtype=F32, bias=ada_b_loc, bias_blk=(None, 1, mod_blk),
                   bias_map=lambda i, k: (i, 0, 0))[:, :N_DEV]
    modp_all, w_in_first, w_out_first, pool_w_all = _all_gather(
        "gather_first", [modp.reshape(2 * N_DEV, mod_blk), w_in_loc[0], w_out_loc[0], pool_w.reshape(-1, POOL_GROUP).astype(BF16)])
    w4 = pool_w_all.reshape(N_DEV, 4, 32, POOL_GROUP).transpose(1, 0, 2, 3).reshape(4, POOL_GROUP, POOL_GROUP)
    later_a = [w_in_loc[1], w_out_loc[1]]
    later_b = [w_in_loc[2], w_out_loc[2], mla_w_in[0].astype(BF16), mla_w_uq.reshape(mla_w_uq.shape[1], -1).astype(BF16),
               mla_w_o[0].astype(BF16)]
    later_c = [w_in_loc[3], w_out_loc[3]]
    state_a, token_a = _split_start("gather_start_a", "gather", later_a, after=modp_all)
    state_b, token_b = _split_start("gather_start_b", "gather", later_b, after=token_a)
    state_c, token_c = _split_start("gather_start_c", "gather", later_c, after=token_b)
    w_in8 = [w_in_first, None, None, None]
    w_out4 = [w_out_first.reshape(4, FF_BLK, d), None, None, None]
    mod = lax.dynamic_index_in_dim(modp_all.reshape(N_DEV, 2, N_DEV, mod_blk), me, axis=2, keepdims=False)
    mod = mod.transpose(1, 0, 2).reshape(2, n_mod, d) + token_c[0, 0]
    mod_rows = mod.reshape(2 * n_mod, 1, d)
    g_rows = g_all.reshape(12, 1, d)
    weights_of = (0.5, 1.0, 0.5, 0.5, 1.0, 0.5)

    def pre_rows(k, with_shift):
        rows = [_rowk(g_rows, 2 * k), _rowk(mod_rows, 3 * k + 1)]
        return rows + [_rowk(mod_rows, 3 * k)] if with_shift else rows

    def post_rows(k):
        return [_rowk(g_rows, 2 * k + 1), _rowk(mod_rows, 3 * k + 2)]

    def act_dtype(k):
        return F32 if k == 1 else BF16

    saved = []
    xs = x0
    mla_wts = None
    h = _prenorm("prenorm_0", xs, pre_rows(0, True), act_dtype(0), tr)
    for k in range(6):
        i, sub = divmod(k, 3)
        tag = f"{i}{sub}"
        if k == 2:
            lands = _split_wait("gather_wait_a", state_a, xs)
            w_in8[1], w_out4[1] = lands[0], lands[1].reshape(4, FF_BLK, d)
        if k == 3:
            lands = _split_wait("gather_wait_b", state_b, xs)
            w_in8[2], w_out4[2] = lands[0], lands[1].reshape(4, FF_BLK, d)
            mla_wts = mla_weights(*lands[2:])
        if k == 5:
            lands = _split_wait("gather_wait_c", state_c, xs)
            w_in8[3], w_out4[3] = lands[0], lands[1].reshape(4, FF_BLK, d)
        if sub != 1:
            u, extra = _ffn_fwd(tag, h, w_in8[2 * i + sub // 2], w_out4[2 * i + sub // 2], tf, tf // 2)
        elif i == 0:
            u, z, v = _pool_fwd(h, w4, pool_b_all, pscale_row, min(4 * tm, s))
            extra = (z, v)
        else:
            u, extra = _mla_fwd(h, mla_wts, cos_k, sin_k, tm)
        saved.append((xs, h, u, extra))
        if k < 5:
            xs, h = _norm_link(f"norm_link_{k + 1}", xs, u, post_rows(k), weights_of[k], pre_rows(k + 1, True), act_dtype(k + 1), tr)

    dx, du, (sq, dgate, dgpost) = _norm_loss("norm_loss", xs, u, target, post_rows(5), weights_of[5], tr)
    loss_part = (0.5 * jnp.sum(sq) / d).reshape(1, 1)

    d_mod = [None] * (2 * n_mod)
    d_g = [None] * 12
    d_mod[3 * 5 + 2], d_g[2 * 5 + 1] = dgate, dgpost
    sent = {}
    pool_grads = mla_grads = None

    def start_scatter(key, arrays):
        state, token = _split_start(f"scatter_start_{key}", "scatter", arrays)
        sent[key] = state
        return token

    for k in (5, 4, 3, 2, 1, 0):
        i, sub = divmod(k, 3)
        tag = f"{i}{sub}"
        xin, h, u, extra = saved[k]
        token = None
        if sub != 1:
            f = 2 * i + sub // 2
            dh, dgu, act = _ffn_bwd_act(tag, du, extra, w_in8[f], w_out4[f], tf, tf // 4)
            dw_out = _ffn_dw(f"ffn_dwout_{tag}", act, du, tw).reshape(N_DEV, FF_BLK // 2, d)
            if k == 0:
                d_pool_w = pool_grads[0].reshape(4, N_DEV, 32, POOL_GROUP).transpose(1, 0, 2, 3).reshape(N_DEV, 4 * 32, POOL_GROUP)
                token = start_scatter(tag + "_out", [dw_out, d_pool_w])
                token = start_scatter(tag + "_in", [_ffn_dw(f"ffn_dwin_{tag}", dgu, h, tw, after=token)])
            else:
                token = start_scatter(tag, [_ffn_dw(f"ffn_dwin_{tag}", dgu, h, tw), dw_out])
        elif i == 0:
            dh, dw4, (dpscale, dpb) = _pool_bwd(du, extra[0], extra[1], w4, pscale_row, min(4 * tm, s))
            pool_grads = (dw4, dpscale, dpb)
        else:
            dh, mla_grads = _mla_bwd(du, h, extra, mla_wts, cos_k, sin_k, tm)
            dwq = mla_grads["wq"]
            d_uq = jnp.concatenate([dwq[:, :, :QK_NOPE], dwq[:, :, KV_LORA:KV_LORA + QK_ROPE]], axis=2).transpose(1, 0, 2)
            token = start_scatter("mla", [mla_grads["w_in"][:, :mla_w_in.shape[2]].reshape(N_DEV, d // N_DEV, -1),
                                          d_uq.reshape(N_DEV, Q_LORA // N_DEV, -1), mla_grads["w_o"].reshape(N_DEV, d // N_DEV, d)])
            dwukp, dwuv2 = mla_grads["wukp"], mla_grads["wuv2"]
            d_uk = dwukp[:, :QK_NOPE, :KV_LORA].transpose(2, 0, 1).reshape(KV_LORA, -1)
            d_uv = jnp.where(even, dwuv2[:, :, :V_HEAD], dwuv2[:, :, V_HEAD:]).transpose(1, 0, 2).reshape(KV_LORA, -1)
            state_ukv, token = _split_start("gather_start_ukv", "gather", [d_uk, d_uv], after=token)
        if k > 0:
            dx, du, reds = _norm_link_bwd(f"norm_link_bwd_{k}", dh, xin, dx, saved[k - 1][2], pre_rows(k, False), post_rows(k - 1),
                                          weights_of[k - 1], act_dtype(k - 1), tr, after=token)
            d_mod[3 * (k - 1) + 2], d_g[2 * (k - 1) + 1] = reds[3], reds[4]
        else:
            dx, reds = _prenorm_bwd("prenorm_bwd_0", dh, xin, dx, pre_rows(0, False), tr, after=token)
        d_mod[3 * k], d_mod[3 * k + 1], d_g[2 * k] = reds[0], reds[1], reds[2]
    grad_x = dx.reshape(x.shape)

    def upd(name, parts, w, m, v):
        shape = w.shape
        r, cdim = parts.shape[1], parts.shape[2]
        return [o.reshape(shape) for o in _adamw(name, parts, w.reshape(r, cdim), m.reshape(r, cdim), v.reshape(r, cdim))]

    def landed(key, after):
        return _split_wait(f"scatter_wait_{key}", sent[key], after)

    res = {}
    w_in_s, m_in_s, v_in_s = (a.reshape(4, FF_BLK, d) for a in (w_in_t, m_in_t, v_in_t))
    w_out_s, m_out_s, v_out_s = (a.reshape(4, FF_BLK // 2, d) for a in (ffn_w_out, m_ffn_w_out, v_ffn_w_out))
    bufs_in = [lax.empty(w_in_s.shape, F32) for _ in range(4)]
    bufs_out = [lax.empty(w_out_s.shape, F32) for _ in range(4)]
    for key, k in (("12", 3), ("mla", None), ("10", 2), ("02", 1)):
        parts = landed(key, grad_x)
        if k is None:
            res["mla_w_in"] = upd("adam_mla_w_in", parts[0], mla_w_in, m_mla_w_in, v_mla_w_in)
            res["mla_w_uq"] = upd("adam_mla_w_uq", parts[1], mla_w_uq, m_mla_w_uq, v_mla_w_uq)
            res["mla_w_o"] = upd("adam_mla_w_o", parts[2], mla_w_o, m_mla_w_o, v_mla_w_o)
            uk_all, uv_all = _split_wait("gather_wait_ukv", state_ukv, grad_x)
            res["mla_w_uk"] = upd("adam_mla_w_uk", uk_all, mla_w_uk, m_mla_w_uk, v_mla_w_uk)
            res["mla_w_uv"] = upd("adam_mla_w_uv", uv_all, mla_w_uv, m_mla_w_uv, v_mla_w_uv)
            continue
        bufs_in = _adamw_slab(f"adam_ffn_w_in_{key}", parts[0], w_in_s, m_in_s, v_in_s, bufs_in, k)
        bufs_out = _adamw_slab(f"adam_ffn_w_out_{key}", parts[1], w_out_s, m_out_s, v_out_s, bufs_out, k)

    dw4, dpscale, dpb = pool_grads
    small_g = jnp.concatenate(d_mod + d_g + [dpb, dpscale, mla_grads["q_norm"], mla_grads["kv_norm"], loss_part], axis=1)
    done = [bufs_in[0], bufs_out[0], res["mla_w_o"][0], res["mla_w_uv"][0]]
    (small_g_all,) = _all_gather("gather_small_grads", [small_g], after=sum(a.reshape(-1)[:1] for a in done))
    small_g_all = small_g_all.reshape(N_DEV, -1)
    n_m = 2 * n_mod * d
    d_mod_all = small_g_all[:, :n_m].reshape(N_DEV, 2, n_mod * d)
    rest = small_g_all[:, n_m:]
    p_norm_g = lax.dynamic_slice_in_dim(rest[:, :12 * d].reshape(N_DEV, 12, d), me * (d // N_DEV), d // N_DEV, axis=2)
    p_pool_b = lax.dynamic_slice_in_dim(rest[:, 12 * d:13 * d].reshape(N_DEV, 4, POOL_GROUP), me * 32, 32, axis=2)
    p_pool_scale = rest[:, 13 * d:14 * d].reshape(N_DEV, 1, d)
    p_q_norm = lax.dynamic_slice_in_dim(rest[:, 14 * d:14 * d + Q_LORA], me * 32, 32, axis=1).reshape(N_DEV, 1, 32)
    p_kv_norm = rest[:, 14 * d + Q_LORA:14 * d + Q_LORA + KV_LORA].reshape(N_DEV, 1, KV_LORA)
    loss = jnp.sum(rest[:, -1])

    d_mod_loc = lax.dynamic_slice_in_dim(d_mod_all, me * mod_blk, mod_blk, axis=2).transpose(1, 0, 2)
    k_pad = 128
    sc_t = jnp.pad(sc_all.T, ((0, 0), (0, k_pad - N_DEV)))
    d_ada_w = _matmul("ada_dw", sc_t, jnp.pad(d_mod_loc, ((0, 0), (0, k_pad - N_DEV), (0, 0))), a_blk=(d, k_pad),
                      a_map=lambda i, k: (0, 0), b_blk=(None, k_pad, mod_blk), b_map=lambda i, k: (i, 0, 0), o_shape=(2, d, mod_blk),
                      o_blk=(None, d, mod_blk), o_map=lambda i, k: (i, 0, 0), grid=(2, 1), contract=NN, out_dtype=F32)
    res["ada_w"] = upd("adam_ada_w", d_ada_w.reshape(1, 2 * d, mod_blk), ada_w, m_ada_w, v_ada_w)
    res["ada_b"] = upd("adam_ada_b", d_mod_all.reshape(N_DEV, 2, n_mod * d), ada_b, m_ada_b, v_ada_b)
    res["norm_g"] = upd("adam_norm_g", p_norm_g, norm_g, m_norm_g, v_norm_g)
    res["pool_b"] = upd("adam_pool_b", p_pool_b, pool_b, m_pool_b, v_pool_b)
    res["pool_scale"] = upd("adam_pool_scale", p_pool_scale, pool_scale, m_pool_scale, v_pool_scale)
    res["mla_q_norm"] = upd("adam_mla_q_norm", p_q_norm, mla_q_norm, m_mla_q_norm, v_mla_q_norm)
    res["mla_kv_norm"] = upd("adam_mla_kv_norm", p_kv_norm, mla_kv_norm, m_mla_kv_norm, v_mla_kv_norm)

    p_out, p_pool_w = landed("00_out", res["ada_w"][1])
    bufs_out = _adamw_slab("adam_ffn_w_out_00", p_out, w_out_s, m_out_s, v_out_s, bufs_out, 0)
    res["pool_w"] = upd("adam_pool_w", p_pool_w, pool_w, m_pool_w, v_pool_w)
    (p_in,) = landed("00_in", res["pool_w"][1])
    bufs_in = _adamw_slab("adam_ffn_w_in_00", p_in, w_in_s, m_in_s, v_in_s, bufs_in, 0)
    res["ffn_w_in"] = [jnp.swapaxes(b.reshape(w_in_t.shape), 2, 3) for b in bufs_in]
    res["ffn_w_out"] = [b.reshape(ffn_w_out.shape) for b in bufs_out]

    order = ["ada_w", "ada_b", "norm_g", "ffn_w_in", "ffn_w_out", "pool_w", "pool_b", "pool_scale", "mla_w_in", "mla_q_norm",
             "mla_kv_norm", "mla_w_uq", "mla_w_uk", "mla_w_uv", "mla_w_o"]
    outs = [loss, grad_x]
    for j in range(4):
        outs += [res[name][j] for name in order]
    return tuple(outs)
```

```python
import functools

import jax
import jax.numpy as jnp
from jax import lax
from jax.experimental import pallas as pl
from jax.experimental.pallas import tpu as pltpu

F32 = jnp.float32
BF16 = jnp.bfloat16
N_DEV = 8
AXES = ("x", "y", "c")
MESH = pl.DeviceIdType.MESH

D_MODEL = 1024
N_HEADS = 16
QK_NOPE = 64
QK_ROPE = 32
V_HEAD = 64
Q_LORA = 256
KV_LORA = 128
LAT_PAD = 512
QK_PAD = 256
D_FF = 2816
FF_BLK = 2 * D_FF // N_DEV
POOL_WINDOWS = (2, 4, 8, 16)
POOL_GROUP = 256
ROPE_THETA = 10000.0
EPS = 1e-6
ATTN_SCALE = (QK_NOPE + QK_ROPE) ** -0.5
LOG2_E = 1.4426950408889634
ADAM_LR, ADAM_B1, ADAM_B2, ADAM_EPS, ADAM_WD, ADAM_STEP = 0.001, 0.9, 0.999, 1e-08, 0.01, 10
VMEM_LIMIT = 56 * 1024 * 1024

NN = ((1,), (0,))
NT = ((1,), (1,))
TN = ((0,), (0,))


def _params(**kw):
    return pltpu.CompilerParams(vmem_limit_bytes=VMEM_LIMIT, **kw)


def _dot(a, b, contract):
    return lax.dot_general(a, b, (contract, ((), ())), preferred_element_type=F32)


def _matmul(name, a, b, *, a_blk, a_map, b_blk, b_map, o_shape, o_blk, o_map, grid, contract, out_dtype,
            bias=None, bias_blk=None, bias_map=None, after=None):
    n_k = grid[-1]
    k_axis = len(grid) - 1
    acc_shape = tuple(d for d in o_blk if d is not None)

    def body(*refs):
        a_ref, b_ref = refs[:2]
        bias_ref = refs[2] if bias is not None else None
        if n_k == 1:
            r = _dot(a_ref[...].astype(BF16), b_ref[...].astype(BF16), contract)
            if bias is not None:
                r = r + bias_ref[...]
            refs[-1][...] = r.astype(refs[-1].dtype)
            return
        o_ref, acc = refs[-2:]
        k = pl.program_id(k_axis)

        @pl.when(k == 0)
        def _():
            acc[...] = jnp.zeros_like(acc)

        acc[...] += _dot(a_ref[...].astype(BF16), b_ref[...].astype(BF16), contract)

        @pl.when(k == n_k - 1)
        def _():
            r = acc[...]
            if bias is not None:
                r = r + bias_ref[...]
            o_ref[...] = r.astype(o_ref.dtype)

    in_specs = [pl.BlockSpec(a_blk, a_map), pl.BlockSpec(b_blk, b_map)]
    args = [a, b]
    if bias is not None:
        in_specs.append(pl.BlockSpec(bias_blk, bias_map))
        args.append(bias)
    if after is not None:
        in_specs.append(pl.BlockSpec(memory_space=pl.ANY))
        args.append(after)
    return pl.pallas_call(
        body, name=name, grid=grid, in_specs=in_specs, out_specs=pl.BlockSpec(o_blk, o_map),
        out_shape=jax.ShapeDtypeStruct(o_shape, out_dtype), scratch_shapes=[pltpu.VMEM(acc_shape, F32)] if n_k > 1 else [],
        compiler_params=_params(),
    )(*args)


def _rowmap(name, fn, ins, outs, reds, grid, after=None):
    n_in, n_out, n_red = len(ins), len(outs), len(reds)
    extra = [] if after is None else [after]

    def body(*refs):
        in_refs = refs[:n_in]
        out_refs = refs[n_in + len(extra):n_in + len(extra) + n_out]
        red_refs = refs[n_in + len(extra) + n_out:]
        out_vals, red_vals = fn(*[r[...] for r in in_refs])
        for r, v in zip(out_refs, out_vals):
            r[...] = v.astype(r.dtype)
        if n_red:
            first = pl.program_id(0) == 0
            for ax in range(1, len(grid)):
                first = jnp.logical_and(first, pl.program_id(ax) == 0)

            @pl.when(first)
            def _():
                for r in red_refs:
                    r[...] = jnp.zeros_like(r)

            for r, v in zip(red_refs, red_vals):
                r[...] += v

    res = pl.pallas_call(
        body, name=name, grid=grid,
        in_specs=[pl.BlockSpec(blk, imap) for _, blk, imap in ins] + [pl.BlockSpec(memory_space=pl.ANY)] * len(extra),
        out_specs=[pl.BlockSpec(blk, imap) for _, blk, imap in list(outs) + list(reds)],
        out_shape=[sds for sds, _, _ in list(outs) + list(reds)],
        compiler_params=_params(),
    )(*[a for a, _, _ in ins], *extra)
    return res[:n_out], res[n_out:]


def _sds(shape, dtype):
    return jax.ShapeDtypeStruct(shape, dtype)


def _tile(a, tm):
    return (a, (tm, a.shape[1]), lambda i: (i, 0))


def _row(a):
    return (a, (1, a.shape[1]), lambda i: (0, 0))


def _otile(n, c, dtype, tm):
    return (_sds((n, c), dtype), (tm, c), lambda i: (i, 0))


def _ored(c):
    return (_sds((1, c), F32), (1, c), lambda i: (0, 0))


def _colsum(v):
    return jnp.sum(v, axis=0, keepdims=True)


def _rstd(v):
    return lax.rsqrt(jnp.mean(v * v, axis=-1, keepdims=True) + EPS)


def _pre(xv, g, sc, sh):
    return xv * _rstd(xv) * g * (1.0 + sc) + sh


def _post(xv, uv, g, gt, weight):
    return xv + weight * (1.0 + gt) * (uv * _rstd(uv) * g)


def _post_bwd(dv, uv, g, gt, weight):
    r = _rstd(uv)
    un = uv * r
    dy = dv * (weight * (1.0 + gt))
    a = dy * g
    du = r * (a - un * jnp.mean(a * un, axis=-1, keepdims=True))
    return du, (_colsum(dv * (weight * (un * g))), _colsum(dy * un))


def _pre_bwd(dhv, xv, dv, g, sc):
    dhv = dhv.astype(F32)
    r = _rstd(xv)
    xn = xv * r
    b = dhv * (g * (1.0 + sc))
    dx = dv + r * (b - xn * jnp.mean(b * xn, axis=-1, keepdims=True))
    return dx, (_colsum(dhv), _colsum(dhv * (xn * g)), _colsum(dhv * ((1.0 + sc) * xn)))


def _rowk(rows, k):
    return (rows, (None, 1, rows.shape[2]), lambda i: (k, 0, 0))


def _prenorm(name, x, pre, out_dtype, tm):
    n, d = x.shape
    (h,), _ = _rowmap(name, lambda xv, g, sc, sh: ((_pre(xv, g, sc, sh),), ()), [_tile(x, tm), *pre], [_otile(n, d, out_dtype, tm)],
                      [], (n // tm,))
    return h


def _norm_link(name, x, u, post, weight, pre, out_dtype, tm):
    n, d = x.shape

    def fn(xv, uv, g, gt, g2, sc, sh):
        xn = _post(xv, uv, g, gt, weight)
        return (xn, _pre(xn, g2, sc, sh)), ()

    (xn, h), _ = _rowmap(name, fn, [_tile(x, tm), _tile(u, tm), *post, *pre], [_otile(n, d, F32, tm), _otile(n, d, out_dtype, tm)],
                         [], (n // tm,))
    return xn, h


def _norm_loss(name, x, u, target, post, weight, tm):
    n, d = x.shape

    def fn(xv, uv, tv, g, gt):
        e = _post(xv, uv, g, gt, weight) - tv
        dv = e * (1.0 / d)
        du, reds = _post_bwd(dv, uv, g, gt, weight)
        return (dv, du), (_colsum(e * e), *reds)

    (dx, du), reds = _rowmap(name, fn, [_tile(x, tm), _tile(u, tm), _tile(target, tm), *post],
                             [_otile(n, d, F32, tm), _otile(n, d, BF16, tm)], [_ored(d)] * 3, (n // tm,))
    return dx, du, reds


def _norm_link_bwd(name, dh, x, dout, u_prev, pre, post_prev, weight_prev, out_dtype, tm, after=None):
    n, d = x.shape

    def fn(dhv, xv, dv, uv, g, sc, g2, gt):
        dx, reds = _pre_bwd(dhv, xv, dv, g, sc)
        du, reds_prev = _post_bwd(dx, uv, g2, gt, weight_prev)
        return (dx, du), (*reds, *reds_prev)

    (dx, du), reds = _rowmap(name, fn, [_tile(dh, tm), _tile(x, tm), _tile(dout, tm), _tile(u_prev, tm), *pre, *post_prev],
                             [_otile(n, d, F32, tm), _otile(n, d, out_dtype, tm)], [_ored(d)] * 5, (n // tm,), after=after)
    return dx, du, reds


def _prenorm_bwd(name, dh, x, dout, pre, tm, after=None):
    n, d = x.shape

    def fn(dhv, xv, dv, g, sc):
        dx, reds = _pre_bwd(dhv, xv, dv, g, sc)
        return (dx,), reds

    (dx,), reds = _rowmap(name, fn, [_tile(dh, tm), _tile(x, tm), _tile(dout, tm), *pre], [_otile(n, d, F32, tm)], [_ored(d)] * 3,
                          (n // tm,), after=after)
    return dx, reds


def _ffn_fwd(tag, h, w_in8, w_out4, tm, sub, after=None):
    s, d = h.shape

    extra = [] if after is None else [after]

    def body(h_ref, wg_ref, wu_ref, wo_ref, *rest):
        u_ref, gu_ref = rest[-2:]

        @pl.when(pl.program_id(1) == 0)
        def _():
            u_ref[...] = jnp.zeros_like(u_ref)

        for r in range(tm // sub):
            rows = pl.ds(r * sub, sub)
            hv = h_ref[rows, :]
            gate = _dot(hv, wg_ref[...], NT)
            up = _dot(hv, wu_ref[...], NT)
            gu_ref[0, rows, :] = gate.astype(BF16)
            gu_ref[1, rows, :] = up.astype(BF16)
            u_ref[rows, :] += _dot((gate * jax.nn.sigmoid(gate) * up).astype(BF16), wo_ref[...], NN)

    w_blk = (None, FF_BLK, d)
    return pl.pallas_call(
        body, name=f"ffn_fwd_{tag}", grid=(s // tm, 4),
        in_specs=[pl.BlockSpec((tm, d), lambda i, j: (i, 0)), pl.BlockSpec(w_blk, lambda i, j: (j, 0, 0)),
                  pl.BlockSpec(w_blk, lambda i, j: (j + 4, 0, 0)), pl.BlockSpec((None, FF_BLK, d), lambda i, j: (j, 0, 0))]
        + [pl.BlockSpec(memory_space=pl.ANY)] * len(extra),
        out_specs=[pl.BlockSpec((tm, d), lambda i, j: (i, 0)), pl.BlockSpec((2, None, tm, FF_BLK), lambda i, j: (0, j, i, 0))],
        out_shape=[_sds((s, d), F32), _sds((2, 4, s, FF_BLK), BF16)], compiler_params=_params(),
    )(h, w_in8, w_in8, w_out4, *extra)


def _ffn_bwd_act(tag, du, gu, w_in8, w_out4, tm, sub):
    s, d = du.shape

    def body(du_ref, gu_ref, wg_ref, wu_ref, wo_ref, dh_ref, dgu_ref, act_ref):
        @pl.when(pl.program_id(1) == 0)
        def _():
            dh_ref[...] = jnp.zeros_like(dh_ref)

        for r in range(tm // sub):
            rows = pl.ds(r * sub, sub)
            dact = _dot(du_ref[rows, :], wo_ref[...], NT)
            gate, up = gu_ref[0, rows, :].astype(F32), gu_ref[1, rows, :].astype(F32)
            sg = jax.nn.sigmoid(gate)
            silu = gate * sg
            dg = dact * up * (sg * (1.0 + gate * (1.0 - sg)))
            dup = dact * silu
            dgu_ref[0, :, rows] = dg.T.astype(BF16)
            dgu_ref[1, :, rows] = dup.T.astype(BF16)
            act_ref[:, rows] = (silu * up).T.astype(BF16)
            dh_ref[rows, :] += _dot(dg.astype(BF16), wg_ref[...], NN) + _dot(dup.astype(BF16), wu_ref[...], NN)

    w_blk = (None, FF_BLK, d)
    dh, dgu_t, act_t = pl.pallas_call(
        body, name=f"ffn_bwd_{tag}", grid=(s // tm, 4),
        in_specs=[pl.BlockSpec((tm, d), lambda i, j: (i, 0)), pl.BlockSpec((2, None, tm, FF_BLK), lambda i, j: (0, j, i, 0)),
                  pl.BlockSpec(w_blk, lambda i, j: (j, 0, 0)), pl.BlockSpec(w_blk, lambda i, j: (j + 4, 0, 0)),
                  pl.BlockSpec((None, FF_BLK, d), lambda i, j: (j, 0, 0))],
        out_specs=[pl.BlockSpec((tm, d), lambda i, j: (i, 0)), pl.BlockSpec((2, None, FF_BLK, tm), lambda i, j: (0, j, 0, i)),
                   pl.BlockSpec((None, FF_BLK, tm), lambda i, j: (j, 0, i))],
        out_shape=[_sds((s, d), F32), _sds((2, 4, FF_BLK, s), BF16), _sds((4, FF_BLK, s), BF16)], compiler_params=_params(),
    )(du, gu, w_in8, w_in8, w_out4)
    return dh, dgu_t.reshape(8, FF_BLK, s), act_t


def _ffn_dw(name, lhs_t, rhs, tk, after=None):
    n_g, _, s = lhs_t.shape
    d = rhs.shape[1]
    return _matmul(name, lhs_t, rhs, a_blk=(None, FF_BLK, tk), a_map=lambda g, k: (g, 0, k), b_blk=(tk, d), b_map=lambda g, k: (k, 0),
                   o_shape=(n_g, FF_BLK, d), o_blk=(None, FF_BLK, d), o_map=lambda g, k: (g, 0, 0), grid=(n_g, s // tk),
                   contract=NN, out_dtype=BF16, after=after)


def _window_sum(x, window, transpose):
    s = x.shape[0]
    t = lax.broadcasted_iota(jnp.int32, (s, 1), 0)
    half = window // 2
    cnt = jnp.minimum(t + half, s) - jnp.maximum(t - half, 0)
    inv = 1.0 / cnt.astype(F32)
    if transpose:
        x = x * inv
        offsets = range(-half + 1, half + 1)
    else:
        offsets = range(-half, half)
    acc = jnp.zeros_like(x)
    for o in offsets:
        shifted = x if o == 0 else pltpu.roll(x, (-o) % s, 0)
        valid = jnp.logical_and(t + o >= 0, t + o < s)
        acc = acc + jnp.where(valid, shifted, 0.0)
    return acc if transpose else acc * inv


def _pool_mix(name, x, transpose, out_dtype):
    s, d = x.shape

    def body(x_ref, o_ref):
        g = pl.program_id(0)
        for gi, window in enumerate(POOL_WINDOWS):
            @pl.when(g == gi)
            def _(window=window):
                xv = x_ref[...].astype(F32)
                o_ref[...] = (_window_sum(xv, window, transpose) - xv).astype(o_ref.dtype)

    return pl.pallas_call(
        body, name=name, grid=(len(POOL_WINDOWS),), in_specs=[pl.BlockSpec((s, POOL_GROUP), lambda g: (0, g))],
        out_specs=pl.BlockSpec((s, POOL_GROUP), lambda g: (0, g)), out_shape=_sds((s, d), out_dtype), compiler_params=_params(),
    )(x)


def _pool_fwd(h, w4, bias, pscale, tm):
    s, d = h.shape
    nt = s // tm
    z = _pool_mix("pool_mix", h, False, BF16)
    v = _matmul("pool_proj", z, w4, a_blk=(tm, POOL_GROUP), a_map=lambda i, g, k: (i, g), b_blk=(None, POOL_GROUP, POOL_GROUP),
                b_map=lambda i, g, k: (g, 0, 0), o_shape=(s, d), o_blk=(tm, POOL_GROUP), o_map=lambda i, g, k: (i, g),
                grid=(nt, 4, 1), contract=NN, out_dtype=F32, bias=bias, bias_blk=(1, POOL_GROUP), bias_map=lambda i, g, k: (0, g))
    tr = min(512, s)
    (u,), _ = _rowmap("pool_scale", lambda vv, ps: ((vv * ps,), ()), [_tile(v, tr), _row(pscale)], [_otile(s, d, F32, tr)], [],
                      (s // tr,))
    return u, z, v


def _pool_bwd(du, z, v, w4, pscale, tm):
    s, d = du.shape
    nt = s // tm

    def fn(duv, vv, ps):
        dv = duv * ps
        return (dv,), (_colsum(duv * vv), _colsum(dv))

    tr = min(512, s)
    (dv,), reds = _rowmap("pool_dscale", fn, [_tile(du, tr), _tile(v, tr), _row(pscale)], [_otile(s, d, BF16, tr)],
                          [_ored(d), _ored(d)], (s // tr,))
    dw4 = _matmul("pool_dw", z, dv, a_blk=(tm, POOL_GROUP), a_map=lambda g, k: (k, g), b_blk=(tm, POOL_GROUP),
                  b_map=lambda g, k: (k, g), o_shape=(4, POOL_GROUP, POOL_GROUP), o_blk=(None, POOL_GROUP, POOL_GROUP),
                  o_map=lambda g, k: (g, 0, 0), grid=(4, nt), contract=TN, out_dtype=F32)
    dz = _matmul("pool_dz", dv, w4, a_blk=(tm, POOL_GROUP), a_map=lambda i, g, k: (i, g), b_blk=(None, POOL_GROUP, POOL_GROUP),
                 b_map=lambda i, g, k: (g, 0, 0), o_shape=(s, d), o_blk=(tm, POOL_GROUP), o_map=lambda i, g, k: (i, g),
                 grid=(nt, 4, 1), contract=NT, out_dtype=F32)
    dh = _pool_mix("pool_mix_t", dz, True, F32)
    return dh, dw4, reds


def _lane(shape):
    return lax.broadcasted_iota(jnp.int32, shape, 1)


def _rope_swap(v, transpose):
    half = QK_ROPE // 2
    lane = _lane(v.shape)
    up = pltpu.roll(v, v.shape[1] - half, 1)
    down = pltpu.roll(v, half, 1)
    if transpose:
        return jnp.where(lane < half, up, jnp.where(lane < QK_ROPE, -down, 0.0))
    return jnp.where(lane < half, -up, jnp.where(lane < QK_ROPE, down, 0.0))


def _rope(v, cos, sin):
    return v * cos + _rope_swap(v, False) * sin


def _rope_t(g, cos, sin):
    return g * cos + _rope_swap(g * sin, True)


def _mla_mid(lat, q_norm, kv_norm, cos_k, sin_k, tm):
    s = lat.shape[0]

    def fn(lv, qn, kn, cs, sn):
        cq = lv[:, :Q_LORA]
        ckv = lv[:, Q_LORA:Q_LORA + KV_LORA]
        kr = lv[:, Q_LORA + KV_LORA:]
        cq = cq * _rstd(cq) * qn
        ckv = ckv * _rstd(ckv) * kn
        return (cq, jnp.concatenate([ckv, _rope(kr, cs, sn)], axis=1)), ()

    (cq, kcat), _ = _rowmap("mla_mid", fn, [_tile(lat, tm), _row(q_norm), _row(kv_norm), _tile(cos_k, tm), _tile(sin_k, tm)],
                            [_otile(s, Q_LORA, BF16, tm), _otile(s, QK_PAD, BF16, tm)], [], (s // tm,))
    return cq, kcat


def _mla_mid_bwd(lat, dcq, dkcat, dv, q_norm, kv_norm, cos_k, sin_k, tm):
    s = lat.shape[0]

    def fn(lv, dq, dk, dvv, qn, kn, cs, sn):
        dk = dk * (1.0 / LOG2_E)
        cq = lv[:, :Q_LORA]
        ckv = lv[:, Q_LORA:Q_LORA + KV_LORA]
        rq, rk = _rstd(cq), _rstd(ckv)
        cqn, ckn = cq * rq, ckv * rk
        a = dq * qn
        d_cq = rq * (a - cqn * jnp.mean(a * cqn, axis=-1, keepdims=True))
        dckv = dk[:, :KV_LORA] + dvv
        a2 = dckv * kn
        d_ckv = rk * (a2 - ckn * jnp.mean(a2 * ckn, axis=-1, keepdims=True))
        d_kr = _rope_t(dk[:, KV_LORA:], cs, sn)
        return (jnp.concatenate([d_cq, d_ckv, d_kr], axis=1),), (_colsum(dq * cqn), _colsum(dckv * ckn))

    (dlat,), reds = _rowmap(
        "mla_mid_bwd", fn,
        [_tile(lat, tm), _tile(dcq, tm), _tile(dkcat, tm), _tile(dv, tm), _row(q_norm), _row(kv_norm), _tile(cos_k, tm),
         _tile(sin_k, tm)],
        [_otile(s, LAT_PAD, BF16, tm)], [_ored(Q_LORA), _ored(KV_LORA)], (s // tm,))
    return dlat, reds


def _mla_q(cq, wq, wukp, cos_k, sin_k, tm):
    s = cq.shape[0]

    def body(cq_ref, wq_ref, wuk_ref, cos_ref, sin_ref, o_ref):
        cqv, cs, sn = cq_ref[...], cos_ref[...], sin_ref[...]
        for h in range(N_HEADS):
            aq = _dot(cqv, wq_ref[h], NN)
            qlat = _dot(aq.astype(BF16), wuk_ref[h], NN)
            roped = _rope(aq[:, KV_LORA:], cs, sn)
            o_ref[h] = (jnp.concatenate([qlat[:, :KV_LORA], roped], axis=1) * (ATTN_SCALE * LOG2_E)).astype(o_ref.dtype)

    wblk = pl.BlockSpec((N_HEADS, QK_PAD, QK_PAD), lambda i: (0, 0, 0))
    tblk = pl.BlockSpec((tm, KV_LORA), lambda i: (i, 0))
    return pl.pallas_call(
        body, name="mla_q", grid=(s // tm,),
        in_specs=[pl.BlockSpec((tm, Q_LORA), lambda i: (i, 0)), wblk, wblk, tblk, tblk],
        out_specs=pl.BlockSpec((N_HEADS, tm, QK_PAD), lambda i: (0, i, 0)), out_shape=_sds((N_HEADS, s, QK_PAD), BF16),
        compiler_params=_params(),
    )(cq, wq, wukp, cos_k, sin_k)


def _mla_q_bwd(cq, wq, wukp, cos_k, sin_k, dqcat, tm):
    s = cq.shape[0]

    def body(cq_ref, wq_ref, wuk_ref, cos_ref, sin_ref, dq_ref, dcq_ref, dwq_ref, dwuk_ref):
        @pl.when(pl.program_id(0) == 0)
        def _():
            dwq_ref[...] = jnp.zeros_like(dwq_ref)
            dwuk_ref[...] = jnp.zeros_like(dwuk_ref)

        cqv, cs, sn = cq_ref[...], cos_ref[...], sin_ref[...]
        d_cq = jnp.zeros((tm, Q_LORA), F32)
        for h in range(N_HEADS):
            aq = _dot(cqv, wq_ref[h], NN).astype(BF16)
            g = dq_ref[h].astype(F32) * ATTN_SCALE
            gl, gr = g[:, :KV_LORA], g[:, KV_LORA:]
            dqlat = jnp.concatenate([gl, jnp.zeros_like(gl)], axis=1).astype(BF16)
            d_rope = _rope_t(gr, cs, sn)
            daq = _dot(dqlat, wuk_ref[h], NT) + jnp.concatenate([jnp.zeros_like(d_rope), d_rope], axis=1)
            daq_b = daq.astype(BF16)
            dwuk_ref[h] += _dot(aq, dqlat, TN)
            dwq_ref[h] += _dot(cqv, daq_b, TN)
            d_cq = d_cq + _dot(daq_b, wq_ref[h], NT)
        dcq_ref[...] = d_cq

    wblk = pl.BlockSpec((N_HEADS, QK_PAD, QK_PAD), lambda i: (0, 0, 0))
    tblk = pl.BlockSpec((tm, KV_LORA), lambda i: (i, 0))
    return pl.pallas_call(
        body, name="mla_q_bwd", grid=(s // tm,),
        in_specs=[pl.BlockSpec((tm, Q_LORA), lambda i: (i, 0)), wblk, wblk, tblk, tblk,
                  pl.BlockSpec((N_HEADS, tm, QK_PAD), lambda i: (0, i, 0))],
        out_specs=[pl.BlockSpec((tm, Q_LORA), lambda i: (i, 0)), wblk, wblk],
        out_shape=[_sds((s, Q_LORA), F32), _sds((N_HEADS, QK_PAD, QK_PAD), F32), _sds((N_HEADS, QK_PAD, QK_PAD), F32)],
        compiler_params=_params(),
    )(cq, wq, wukp, cos_k, sin_k, dqcat)


def _flash_fwd(qcat, kcat, tq, tk):
    n_h, s, _ = qcat.shape
    n_k = s // tk

    def body(q_ref, k_ref, v_ref, o_ref, lse_ref):
        q = q_ref[...]
        m = jnp.full((tq, 1), -1e30, F32)
        l = jnp.zeros((tq, 1), F32)
        acc = jnp.zeros((tq, KV_LORA), F32)
        for kk in range(n_k):
            rows = pl.ds(kk * tk, tk)
            sc = _dot(q, k_ref[rows, :], NT)
            m_new = jnp.maximum(m, jnp.max(sc, axis=1, keepdims=True))
            alpha = jnp.exp2(m - m_new)
            p = jnp.exp2(sc - m_new)
            l = alpha * l + jnp.sum(p, axis=1, keepdims=True)
            acc = alpha * acc + _dot(p.astype(BF16), v_ref[rows, :], NN)
            m = m_new
        o_ref[...] = (acc / l).astype(o_ref.dtype)
        lse_ref[...] = m + jnp.log2(l)

    return pl.pallas_call(
        body, name="mla_attn", grid=(n_h, s // tq),
        in_specs=[pl.BlockSpec((None, tq, QK_PAD), lambda h, i: (h, i, 0)), pl.BlockSpec((s, QK_PAD), lambda h, i: (0, 0)),
                  pl.BlockSpec((s, KV_LORA), lambda h, i: (0, 0))],
        out_specs=[pl.BlockSpec((None, tq, KV_LORA), lambda h, i: (h, i, 0)), pl.BlockSpec((None, tq, 1), lambda h, i: (h, i, 0))],
        out_shape=[_sds((n_h, s, KV_LORA), BF16), _sds((n_h, s, 1), F32)], compiler_params=_params(),
    )(qcat, kcat, kcat)


def _flash_bwd(qcat, kcat, o, do, lse, tq, tk):
    n_h, s, _ = qcat.shape
    n_k = s // tk

    def body(q_ref, k_ref, v_ref, o_ref, do_ref, lse_ref, dq_ref, dk_ref, dv_ref, dq_acc):
        h, i = pl.program_id(0), pl.program_id(1)

        @pl.when(jnp.logical_and(h == 0, i == 0))
        def _():
            dk_ref[...] = jnp.zeros_like(dk_ref)
            dv_ref[...] = jnp.zeros_like(dv_ref)

        q = q_ref[...]
        dov = do_ref[...]
        lse_v = lse_ref[...]
        delta = jnp.sum(dov.astype(F32) * o_ref[...].astype(F32), axis=1, keepdims=True)
        dq_acc[...] = jnp.zeros_like(dq_acc)

        for kk in range(n_k):
            rows = pl.ds(kk * tk, tk)
            k = k_ref[rows, :]
            p = jnp.exp2(_dot(q, k, NT) - lse_v)
            dp = _dot(dov, v_ref[rows, :], NT)
            ds = (p * (dp - delta)).astype(BF16)
            dq_acc[...] += _dot(ds, k, NN)
            dv_ref[rows, :] += _dot(p.astype(BF16), dov, TN)
            dk_ref[rows, :] += _dot(ds, q, TN)
        dq_ref[...] = dq_acc[...].astype(dq_ref.dtype)

    qblk = pl.BlockSpec((None, tq, QK_PAD), lambda h, i: (h, i, 0))
    oblk = pl.BlockSpec((None, tq, KV_LORA), lambda h, i: (h, i, 0))
    return pl.pallas_call(
        body, name="mla_attn_bwd", grid=(n_h, s // tq),
        in_specs=[qblk, pl.BlockSpec((s, QK_PAD), lambda h, i: (0, 0)), pl.BlockSpec((s, KV_LORA), lambda h, i: (0, 0)), oblk, oblk,
                  pl.BlockSpec((None, tq, 1), lambda h, i: (h, i, 0))],
        out_specs=[qblk, pl.BlockSpec((s, QK_PAD), lambda h, i: (0, 0)), pl.BlockSpec((s, KV_LORA), lambda h, i: (0, 0))],
        out_shape=[_sds((n_h, s, QK_PAD), BF16), _sds((s, QK_PAD), F32), _sds((s, KV_LORA), F32)],
        scratch_shapes=[pltpu.VMEM((tq, QK_PAD), F32)], compiler_params=_params(),
    )(qcat, kcat, kcat, o, do, lse)


def _mla_uv(o_lat, wuv2, do, tm):
    n_h, s, _ = o_lat.shape
    d = n_h * V_HEAD
    pair = 2 * V_HEAD
    lat_blk = pl.BlockSpec((n_h, tm, KV_LORA), lambda i: (0, i, 0))
    w_blk = pl.BlockSpec((n_h, KV_LORA, pair), lambda i: (0, 0, 0))
    row_blk = pl.BlockSpec((tm, d), lambda i: (i, 0))

    if do is None:
        def body(a_ref, w_ref, o_ref):
            for p in range(n_h // 2):
                o_ref[:, p * pair:(p + 1) * pair] = (
                    _dot(a_ref[2 * p], w_ref[2 * p], NN) + _dot(a_ref[2 * p + 1], w_ref[2 * p + 1], NN)).astype(o_ref.dtype)

        return pl.pallas_call(body, name="mla_uv", grid=(s // tm,), in_specs=[lat_blk, w_blk], out_specs=row_blk,
                              out_shape=_sds((s, d), BF16), compiler_params=_params())(o_lat, wuv2)

    def body(a_ref, w_ref, do_ref, dlat_ref, dw_ref):
        @pl.when(pl.program_id(0) == 0)
        def _():
            dw_ref[...] = jnp.zeros_like(dw_ref)

        for h in range(n_h):
            dov = do_ref[:, (h // 2) * pair:(h // 2 + 1) * pair]
            dlat_ref[h] = _dot(dov, w_ref[h], NT).astype(dlat_ref.dtype)
            dw_ref[h] += _dot(a_ref[h], dov, TN)

    return pl.pallas_call(body, name="mla_uv_bwd", grid=(s // tm,), in_specs=[lat_blk, w_blk, row_blk], out_specs=[lat_blk, w_blk],
                          out_shape=[_sds((n_h, s, KV_LORA), BF16), _sds((n_h, KV_LORA, pair), F32)], compiler_params=_params(),
                          )(o_lat, wuv2, do)


def _mla_fwd(h, wts, cos_k, sin_k, tm):
    s, d = h.shape
    nt = s // tm
    lat = _matmul("mla_lat", h, wts["w_in"], a_blk=(tm, d), a_map=lambda i, k: (i, 0), b_blk=(d, LAT_PAD), b_map=lambda i, k: (0, 0),
                  o_shape=(s, LAT_PAD), o_blk=(tm, LAT_PAD), o_map=lambda i, k: (i, 0), grid=(nt, 1), contract=NN, out_dtype=F32)
    cq, kcat = _mla_mid(lat, wts["q_norm"], wts["kv_norm"], cos_k, sin_k, tm)
    qcat = _mla_q(cq, wts["wq"], wts["wukp"], cos_k, sin_k, tm)
    o_lat, lse = _flash_fwd(qcat, kcat, min(2 * tm, s), tm)
    o = _mla_uv(o_lat, wts["wuv2"], None, tm)
    u = _matmul("mla_out", o, wts["w_o"], a_blk=(tm, d), a_map=lambda i, k: (i, 0), b_blk=(d, d), b_map=lambda i, k: (0, 0),
                o_shape=(s, d), o_blk=(tm, d), o_map=lambda i, k: (i, 0), grid=(nt, 1), contract=NN, out_dtype=F32)
    return u, (lat, cq, kcat, qcat, o_lat, lse, o)


def _mla_bwd(du, h, saved, wts, cos_k, sin_k, tm):
    lat, cq, kcat, qcat, o_lat, lse, o = saved
    s, d = h.shape
    nt = s // tm
    do = _matmul("mla_do", du, wts["w_o"], a_blk=(tm, d), a_map=lambda i, k: (i, 0), b_blk=(d, d), b_map=lambda i, k: (0, 0),
                 o_shape=(s, d), o_blk=(tm, d), o_map=lambda i, k: (i, 0), grid=(nt, 1), contract=NT, out_dtype=BF16)
    dw_o = _matmul("mla_dwo", o, du, a_blk=(tm, d), a_map=lambda k: (k, 0), b_blk=(tm, d), b_map=lambda k: (k, 0),
                   o_shape=(d, d), o_blk=(d, d), o_map=lambda k: (0, 0), grid=(nt,), contract=TN, out_dtype=F32)
    do_lat, dwuv2 = _mla_uv(o_lat, wts["wuv2"], do, tm)
    dqcat, dkcat, dv = _flash_bwd(qcat, kcat, o_lat, do_lat, lse, min(2 * tm, s), tm)
    dcq, dwq, dwukp = _mla_q_bwd(cq, wts["wq"], wts["wukp"], cos_k, sin_k, dqcat, tm)
    dlat, (dqn, dkn) = _mla_mid_bwd(lat, dcq, dkcat, dv, wts["q_norm"], wts["kv_norm"], cos_k, sin_k, tm)
    dh = _matmul("mla_dh", dlat, wts["w_in"], a_blk=(tm, LAT_PAD), a_map=lambda i, k: (i, 0), b_blk=(d, LAT_PAD),
                 b_map=lambda i, k: (0, 0), o_shape=(s, d), o_blk=(tm, d), o_map=lambda i, k: (i, 0), grid=(nt, 1), contract=NT,
                 out_dtype=F32)
    dw_in = _matmul("mla_dwin", h, dlat, a_blk=(tm, d), a_map=lambda k: (k, 0), b_blk=(tm, LAT_PAD), b_map=lambda k: (k, 0),
                    o_shape=(d, LAT_PAD), o_blk=(d, LAT_PAD), o_map=lambda k: (0, 0), grid=(nt,), contract=TN, out_dtype=F32)
    return dh, dict(w_in=dw_in, wq=dwq, wukp=dwukp, wuv2=dwuv2, w_o=dw_o, q_norm=dqn, kv_norm=dkn)


def _adamw(name, parts, w, m, v):
    n_parts, r, c = parts.shape
    tr = r
    for cand in (256, 128, 64, 32, 16, 8):
        if r > cand and r % cand == 0:
            tr = cand
            break

    def body(p_ref, w_ref, m_ref, v_ref, g_ref, d_ref, nm_ref, nv_ref):
        g = p_ref[0].astype(F32)
        for k in range(1, n_parts):
            g = g + p_ref[k].astype(F32)
        nm = ADAM_B1 * m_ref[...] + (1.0 - ADAM_B1) * g
        nv = ADAM_B2 * v_ref[...] + (1.0 - ADAM_B2) * (g * g)
        m_hat = nm / (1.0 - ADAM_B1 ** ADAM_STEP)
        v_hat = nv / (1.0 - ADAM_B2 ** ADAM_STEP)
        g_ref[...] = g
        d_ref[...] = -ADAM_LR * (m_hat / (jnp.sqrt(v_hat) + ADAM_EPS) + ADAM_WD * w_ref[...])
        nm_ref[...] = nm
        nv_ref[...] = nv

    blk = pl.BlockSpec((tr, c), lambda i: (i, 0))
    return pl.pallas_call(
        body, name=name, grid=(r // tr,), in_specs=[pl.BlockSpec((n_parts, tr, c), lambda i: (0, i, 0)), blk, blk, blk],
        out_specs=[blk] * 4, out_shape=[_sds((r, c), F32)] * 4, compiler_params=_params(),
    )(parts, w, m, v)


def _adamw_slab(name, parts, w, m, v, bufs, f):
    n_parts, r, c = parts.shape
    tr = max(t for t in range(8, 257, 8) if r % t == 0)

    def body(p_ref, w_ref, m_ref, v_ref, *rest):
        g_ref, d_ref, nm_ref, nv_ref = rest[4:]
        g = p_ref[0].astype(F32)
        for k in range(1, n_parts):
            g = g + p_ref[k].astype(F32)
        nm = ADAM_B1 * m_ref[...] + (1.0 - ADAM_B1) * g
        nv = ADAM_B2 * v_ref[...] + (1.0 - ADAM_B2) * (g * g)
        m_hat = nm / (1.0 - ADAM_B1 ** ADAM_STEP)
        v_hat = nv / (1.0 - ADAM_B2 ** ADAM_STEP)
        g_ref[...] = g
        d_ref[...] = -ADAM_LR * (m_hat / (jnp.sqrt(v_hat) + ADAM_EPS) + ADAM_WD * w_ref[...])
        nm_ref[...] = nm
        nv_ref[...] = nv

    blk = pl.BlockSpec((None, tr, c), lambda i: (f, i, 0))
    return pl.pallas_call(
        body, name=name, grid=(r // tr,),
        in_specs=[pl.BlockSpec((n_parts, tr, c), lambda i: (0, i, 0)), blk, blk, blk] + [pl.BlockSpec(memory_space=pl.ANY)] * 4,
        out_specs=[blk] * 4, out_shape=[_sds(w.shape, F32)] * 4, input_output_aliases={4 + j: j for j in range(4)},
        compiler_params=_params(),
    )(parts, w, m, v, *bufs)


def _mesh_pos():
    return lax.axis_index("x"), lax.axis_index("y"), lax.axis_index("c")


def _flip(pos, mask):
    return tuple(1 - p if (mask >> (2 - b)) & 1 else p for b, p in enumerate(pos))


def _index(pos):
    return 4 * pos[0] + 2 * pos[1] + pos[2]


def _all_gather(name, xs, after=None):
    n = len(xs)
    extra = [] if after is None else [after]

    def body(*refs):
        x_refs, o_refs = refs[:n], refs[n + len(extra):2 * n + len(extra)]
        send_sems, recv_sems, local_sems = refs[2 * n + len(extra):]
        me = _mesh_pos()
        sibling = _flip(me, 1)
        others = [_flip(me, 4), _flip(me, 2), _flip(me, 6)]

        def copy(k, j, block, to, src=None):
            dst = o_refs[k].at[_index(block)]
            return pltpu.make_async_remote_copy(
                src_ref=dst if src is None else src, dst_ref=dst, send_sem=send_sems.at[k, j], recv_sem=recv_sems.at[k, j],
                device_id=to, device_id_type=MESH)

        local = [pltpu.make_async_copy(x_refs[k], o_refs[k].at[_index(me)], local_sems.at[k]) for k in range(n)]
        for cp in local:
            cp.start()
        first = []
        for k in range(n):
            first.append(copy(k, 0, me, sibling, src=x_refs[k]))
            first += [copy(k, 1 + j, me, other, src=x_refs[k]) for j, other in enumerate(others)]
        for cp in first:
            cp.start()
        passed = []
        for j, other in enumerate(others):
            for k in range(n):
                copy(k, 1 + j, other, me).wait_recv()
                cp = copy(k, 4 + j, other, sibling)
                cp.start()
                passed.append(cp)
        for k in range(n):
            copy(k, 0, sibling, me).wait_recv()
        for j, other in enumerate(others):
            for k in range(n):
                copy(k, 4 + j, _flip(other, 1), me).wait_recv()
        for cp in first + passed:
            cp.wait_send()
        for cp in local:
            cp.wait()

    any_spec = pl.BlockSpec(memory_space=pl.ANY)
    return pl.pallas_call(
        body, name=name, in_specs=[any_spec] * (n + len(extra)), out_specs=[any_spec] * n,
        out_shape=[_sds((N_DEV,) + x.shape, x.dtype) for x in xs],
        scratch_shapes=[pltpu.SemaphoreType.DMA((n, 7)), pltpu.SemaphoreType.DMA((n, 7)), pltpu.SemaphoreType.DMA((n,))],
    )(*xs, *extra)


def _all_to_all(name, groups):
    flat = [(gi, f) for gi, grp in enumerate(groups) for f in range(len(grp))]
    n = len(flat)
    n_groups = len(groups)

    def body(*refs):
        x_refs, o_refs = refs[:n], refs[n:n + n_groups]
        send_sems, recv_sems, local_sems = refs[n + n_groups:]
        me = _mesh_pos()
        local, sends, recvs = [], [], []
        for k, (gi, f) in enumerate(flat):
            local.append(pltpu.make_async_copy(x_refs[k].at[_index(me)], o_refs[gi].at[_index(me), f], local_sems.at[k]))
            for mask in range(1, N_DEV):
                peer = _flip(me, mask)
                sends.append(pltpu.make_async_remote_copy(
                    src_ref=x_refs[k].at[_index(peer)], dst_ref=o_refs[gi].at[_index(me), f], send_sem=send_sems.at[k, mask - 1],
                    recv_sem=recv_sems.at[k, mask - 1], device_id=peer, device_id_type=MESH))
                recvs.append(pltpu.make_async_remote_copy(
                    src_ref=x_refs[k].at[_index(me)], dst_ref=o_refs[gi].at[_index(peer), f], send_sem=send_sems.at[k, mask - 1],
                    recv_sem=recv_sems.at[k, mask - 1], device_id=peer, device_id_type=MESH))
        for cp in local + sends:
            cp.start()
        for cp in recvs:
            cp.wait_recv()
        for cp in sends:
            cp.wait_send()
        for cp in local:
            cp.wait()

    any_spec = pl.BlockSpec(memory_space=pl.ANY)
    return pl.pallas_call(
        body, name=name, in_specs=[any_spec] * n, out_specs=[any_spec] * n_groups,
        out_shape=[_sds((N_DEV, len(grp)) + grp[0].shape[1:], grp[0].dtype) for grp in groups],
        scratch_shapes=[pltpu.SemaphoreType.DMA((n, 7)), pltpu.SemaphoreType.DMA((n, 7)), pltpu.SemaphoreType.DMA((n,))],
    )(*[a for grp in groups for a in grp])


_MASKS = {
    "gather": tuple(range(1, N_DEV)),
    "scatter": tuple(range(1, N_DEV)),
    "own": (1, 4, 2, 6),
    "pass": (4, 2, 6),
}


def _split_copies(kind, outgoing, x_refs, land_refs, send_sems, recv_sems):
    me = _mesh_pos()
    masks = _MASKS[kind]
    copies = []
    for k, (x_ref, land_ref) in enumerate(zip(x_refs, land_refs)):
        for j, mask in enumerate(masks):
            if kind == "pass":
                peer = _flip(me, 1)
                src = land_ref.at[_index(_flip(me, mask))]
                dst = land_ref.at[_index(_flip(me, mask if outgoing else mask | 1))]
            else:
                peer = _flip(me, mask)
                src = x_ref.at[_index(peer)] if kind == "scatter" else x_ref
                dst = land_ref.at[_index(me if outgoing else peer)]
            sem = k * len(masks) + j
            copies.append(pltpu.make_async_remote_copy(src_ref=src, dst_ref=dst, send_sem=send_sems.at[sem], recv_sem=recv_sems.at[sem],
                                                       device_id=peer, device_id_type=MESH))
    return copies


_HBM_SPEC = pl.BlockSpec(memory_space=pltpu.HBM)
_SEM_SPEC = pl.BlockSpec(memory_space=pltpu.SEMAPHORE)
_EFFECT = pltpu.SideEffectType.DATAFLOW_SIDE_EFFECTING


def _split_start(name, kind, xs, after=None):
    n = len(xs)
    n_sem = n * len(_MASKS[kind])
    extra = [] if after is None else [after]
    lands = [lax.empty(x.shape if kind == "scatter" else (N_DEV,) + x.shape, x.dtype) for x in xs]

    def body(*refs):
        x_refs, land_refs = refs[:n], refs[n:2 * n]
        send_sems, recv_sems = refs[2 * n + len(extra)], refs[2 * n + len(extra) + 1]
        token = refs[-1]
        for cp in _split_copies(kind, True, x_refs, land_refs, send_sems, recv_sems):
            cp.start()
        token[...] = jnp.zeros_like(token)

    hbm = [pltpu.HBM(a.shape, a.dtype) for a in list(xs) + lands]
    res = pl.pallas_call(
        body, name=name,
        out_shape=[pltpu.SemaphoreType.DMA((n_sem,)), pltpu.SemaphoreType.DMA((n_sem,))] + hbm + [_sds((8, 128), F32)],
        in_specs=[_HBM_SPEC] * (2 * n) + [pl.BlockSpec(memory_space=pl.ANY)] * len(extra),
        out_specs=[_SEM_SPEC, _SEM_SPEC] + [_HBM_SPEC] * (2 * n) + [pl.BlockSpec(memory_space=pltpu.VMEM)],
        input_output_aliases={j: 2 + j for j in range(2 * n)}, compiler_params=pltpu.CompilerParams(has_side_effects=_EFFECT),
    )(*[pltpu.with_memory_space_constraint(a, pltpu.HBM) for a in list(xs) + lands], *extra)
    return (kind, n, res[0], res[1], res[2:2 + 2 * n]), res[-1]


def _split_pass(name, state, after):
    kind, n, send_sems_in, recv_sems_in, thru = state
    n_sem = n * len(_MASKS["pass"])

    def body(*refs):
        x_refs, land_refs = refs[:n], refs[n:2 * n]
        send_sems, recv_sems = refs[2 * n], refs[2 * n + 1]
        next_send, next_recv, token = refs[-3:]
        for cp in _split_copies(kind, True, x_refs, land_refs, send_sems, recv_sems):
            cp.wait_send()
        for cp in _split_copies(kind, False, x_refs, land_refs, send_sems, recv_sems):
            cp.wait_recv()
        for cp in _split_copies("pass", True, land_refs, land_refs, next_send, next_recv):
            cp.start()
        token[...] = jnp.zeros_like(token)

    res = pl.pallas_call(
        body, name=name,
        out_shape=[pltpu.HBM(a.shape, a.dtype) for a in thru] + [pltpu.SemaphoreType.DMA((n_sem,)), pltpu.SemaphoreType.DMA((n_sem,)),
                                                                 _sds((8, 128), F32)],
        in_specs=[_HBM_SPEC] * (2 * n) + [_SEM_SPEC, _SEM_SPEC, pl.BlockSpec(memory_space=pl.ANY)],
        out_specs=[_HBM_SPEC] * (2 * n) + [_SEM_SPEC, _SEM_SPEC, pl.BlockSpec(memory_space=pltpu.VMEM)],
        input_output_aliases={j: j for j in range(2 * n)}, compiler_params=pltpu.CompilerParams(has_side_effects=_EFFECT),
    )(*thru, send_sems_in, recv_sems_in, after)
    return ("pass", n, res[2 * n], res[2 * n + 1], res[:2 * n]), res[-1]


def _split_wait(name, state, after):
    kind, n, send_sems_in, recv_sems_in, thru = state

    def body(*refs):
        x_refs, land_refs = refs[:n], refs[n:2 * n]
        send_sems, recv_sems = refs[2 * n], refs[2 * n + 1]
        for cp in _split_copies(kind, True, x_refs, land_refs, send_sems, recv_sems):
            cp.wait_send()
        for cp in _split_copies(kind, False, x_refs, land_refs, send_sems, recv_sems):
            cp.wait_recv()

    res = pl.pallas_call(
        body, name=name, out_shape=[pltpu.HBM(a.shape, a.dtype) for a in thru],
        in_specs=[_HBM_SPEC] * (2 * n) + [_SEM_SPEC, _SEM_SPEC, pl.BlockSpec(memory_space=pl.ANY)], out_specs=[_HBM_SPEC] * (2 * n),
        input_output_aliases={j: j for j in range(2 * n)}, compiler_params=pltpu.CompilerParams(has_side_effects=_EFFECT),
    )(*thru, send_sems_in, recv_sems_in, after)
    me = _index(_mesh_pos())
    out = []
    for x, land in zip(res[:n], res[n:]):
        own = lax.dynamic_index_in_dim(x, me, 0, keepdims=False) if kind == "scatter" else x
        out.append(lax.dynamic_update_slice(land, own[None], (me,) + (0,) * own.ndim))
    return out


def _rope_tables(s):
    inv = 1.0 / (ROPE_THETA ** (jnp.arange(0, QK_ROPE, 2, dtype=F32) / QK_ROPE))
    ang = jnp.arange(s, dtype=F32)[:, None] * inv[None, :]
    pad = jnp.zeros((s, KV_LORA - QK_ROPE), F32)
    return (jnp.concatenate([jnp.cos(ang), jnp.cos(ang), pad], axis=1), jnp.concatenate([jnp.sin(ang), jnp.sin(ang), pad], axis=1))


def _row_of(v):
    return v.reshape(1, -1)


def kernel(x, c, ada_w, ada_b, norm_g, ffn_w_in, ffn_w_out, pool_w, pool_b, pool_scale, mla_w_in, mla_q_norm, mla_kv_norm, mla_w_uq, mla_w_uk, mla_w_uv, mla_w_o, loss_target, m_ada_w, m_ada_b, m_norm_g, m_ffn_w_in, m_ffn_w_out, m_pool_w, m_pool_b, m_pool_scale, m_mla_w_in, m_mla_q_norm, m_mla_kv_norm, m_mla_w_uq, m_mla_w_uk, m_mla_w_uv, m_mla_w_o, v_ada_w, v_ada_b, v_norm_g, v_ffn_w_in, v_ffn_w_out, v_pool_w, v_pool_b, v_pool_scale, v_mla_w_in, v_mla_q_norm, v_mla_kv_norm, v_mla_w_uq, v_mla_w_uk, v_mla_w_uv, v_mla_w_o):
    s, d = x.shape[1], x.shape[2]
    tm = min(512, s)
    tr = min(512, s)
    tf = min(1024, s)
    tw = min(2048, s)
    me = 4 * lax.axis_index("x") + 2 * lax.axis_index("y") + lax.axis_index("c")
    x0 = x.reshape(s, d)
    target = loss_target.reshape(s, d)
    n_mod = ada_w.shape[2] * N_DEV // d
    mod_blk = ada_w.shape[2]

    small = jnp.concatenate([c.reshape(-1), norm_g.reshape(-1), pool_b.reshape(-1), mla_q_norm.reshape(-1)]).reshape(1, -1)
    w_in_t, m_in_t, v_in_t = (jnp.swapaxes(a, 2, 3) for a in (ffn_w_in, m_ffn_w_in, v_ffn_w_in))
    w_in_loc = [w_in_t[i, f].astype(BF16) for i in range(2) for f in range(2)]
    w_out_loc = [ffn_w_out[i, f].astype(BF16) for i in range(2) for f in range(2)]
    (small_all,) = _all_gather("gather_small", [small])
    small_all = small_all.reshape(N_DEV, -1)
    c_all = small_all[:, :d]
    off = d
    g_all = small_all[:, off:off + 12 * (d // N_DEV)].reshape(N_DEV, 2, 6, d // N_DEV).transpose(1, 2, 0, 3).reshape(2, 6, d)
    off += 12 * (d // N_DEV)
    pool_b_all = small_all[:, off:off + 4 * 32].reshape(N_DEV, 4, 32).transpose(1, 0, 2).reshape(1, d)
    off += 4 * 32
    q_norm_all = small_all[:, off:off + 32].reshape(1, Q_LORA)
    kv_norm_row = mla_kv_norm.reshape(1, KV_LORA)
    pscale_row = pool_scale.reshape(1, d)

    even = (jnp.arange(N_HEADS) % 2 == 0)[:, None, None]
    cos_k, sin_k = _rope_tables(s)

    def mla_weights(mla_w_in_all, mla_w_uq_all, mla_w_o_all):
        uq = mla_w_uq_all.reshape(Q_LORA, N_HEADS, QK_NOPE + QK_ROPE).transpose(1, 0, 2)
        zq = jnp.zeros((N_HEADS, Q_LORA, QK_NOPE), BF16)
        wq = jnp.concatenate(
            [uq[:, :, :QK_NOPE], zq, uq[:, :, QK_NOPE:], jnp.zeros((N_HEADS, Q_LORA, QK_PAD - KV_LORA - QK_ROPE), BF16)], axis=2)
        wukp = jnp.pad(mla_w_uk[0].transpose(1, 2, 0).astype(BF16), ((0, 0), (0, QK_PAD - QK_NOPE), (0, QK_PAD - KV_LORA)))
        uv = mla_w_uv[0].transpose(1, 0, 2).astype(BF16)
        wuv2 = jnp.where(even, jnp.concatenate([uv, jnp.zeros_like(uv)], axis=2), jnp.concatenate([jnp.zeros_like(uv), uv], axis=2))
        return dict(w_in=jnp.pad(mla_w_in_all.reshape(d, -1), ((0, 0), (0, LAT_PAD - mla_w_in.shape[2]))), wq=wq, wukp=wukp,
                    wuv2=wuv2, w_o=mla_w_o_all.reshape(d, d), q_norm=q_norm_all, kv_norm=kv_norm_row)

    (sc_all,), _ = _rowmap("ada_silu", lambda cv: ((cv * jax.nn.sigmoid(cv),), ()), [(c_all, (N_DEV, d), lambda i: (0, 0))],
                           [(_sds((N_DEV, d), F32), (N_DEV, d), lambda i: (0, 0))], [], (1,))
    ada_b_loc = lax.dynamic_slice_in_dim(ada_b, me * mod_blk, mod_blk, axis=1).reshape(2, 1, mod_blk)
    m_pad = 2 * N_DEV
    modp = _matmul("ada_mod", jnp.pad(sc_all, ((0, m_pad - N_DEV), (0, 0))), ada_w, a_blk=(m_pad, d), a_map=lambda i, k: (0, 0),
                   b_blk=(None, d, mod_blk), b_map=lambda i, k: (i, 0, 0), o_shape=(2, m_pad, mod_blk), o_blk=(None, m_pad, mod_blk),
                   o_map=lambda i, k: (i, 0, 0), grid=(2, 1), contract=NN, out_dtype=F32, bias=ada_b_loc, bias_blk=(None, 1, mod_blk),
                   bias_map=lambda i, k: (i, 0, 0))[:, :N_DEV]
    (modp_all,) = _all_gather("gather_mod", [modp.reshape(2 * N_DEV, mod_blk)])
    groups = [[w_in_loc[0], w_out_loc[0], pool_w.reshape(-1, POOL_GROUP).astype(BF16)], [w_in_loc[1], w_out_loc[1]],
              [w_in_loc[2], w_out_loc[2], mla_w_in[0].astype(BF16), mla_w_uq.reshape(mla_w_uq.shape[1], -1).astype(BF16),
               mla_w_o[0].astype(BF16)], [w_in_loc[3], w_out_loc[3]]]
    states, token = [], modp_all
    for name, group in zip("0abc", groups):
        state, token = _split_start(f"gather_start_{name}", "own", group, after=token)
        states.append(state)
    w_in8 = [None] * 4
    w_out4 = [None] * 4
    mod = lax.dynamic_index_in_dim(modp_all.reshape(N_DEV, 2, N_DEV, mod_blk), me, axis=2, keepdims=False)
    mod = mod.transpose(1, 0, 2).reshape(2, n_mod, d) + token[0, 0]
    mod_rows = mod.reshape(2 * n_mod, 1, d)
    g_rows = g_all.reshape(12, 1, d)
    weights_of = (0.5, 1.0, 0.5, 0.5, 1.0, 0.5)

    def pre_rows(k, with_shift):
        rows = [_rowk(g_rows, 2 * k), _rowk(mod_rows, 3 * k + 1)]
        return rows + [_rowk(mod_rows, 3 * k)] if with_shift else rows

    def post_rows(k):
        return [_rowk(g_rows, 2 * k + 1), _rowk(mod_rows, 3 * k + 2)]

    def act_dtype(k):
        return F32 if k == 1 else BF16

    saved = []
    xs = x0
    mla_wts = None
    h = _prenorm("prenorm_0", xs, pre_rows(0, True), act_dtype(0), tr)
    state, token = _split_pass("gather_pass_0", states[0], h)
    lands = _split_wait("gather_wait_0", state, token)
    w_in8[0], w_out4[0] = lands[0], lands[1].reshape(4, FF_BLK, d)
    w4 = lands[2].reshape(N_DEV, 4, 32, POOL_GROUP).transpose(1, 0, 2, 3).reshape(4, POOL_GROUP, POOL_GROUP)
    token = None
    for k in range(6):
        i, sub = divmod(k, 3)
        tag = f"{i}{sub}"
        if k == 1:
            states[1], token = _split_pass("gather_pass_a", states[1], xs)
        if k == 2:
            lands = _split_wait("gather_wait_a", states[1], xs)
            w_in8[1], w_out4[1] = lands[0], lands[1].reshape(4, FF_BLK, d)
            states[2], token = _split_pass("gather_pass_b", states[2], lands[0])
        if k == 3:
            lands = _split_wait("gather_wait_b", states[2], xs)
            w_in8[2], w_out4[2] = lands[0], lands[1].reshape(4, FF_BLK, d)
            mla_wts = mla_weights(*lands[2:])
            states[3], token = _split_pass("gather_pass_c", states[3], lands[0])
        if k == 5:
            lands = _split_wait("gather_wait_c", states[3], xs)
            w_in8[3], w_out4[3] = lands[0], lands[1].reshape(4, FF_BLK, d)
        if sub != 1:
            u, extra = _ffn_fwd(tag, h, w_in8[2 * i + sub // 2], w_out4[2 * i + sub // 2], tf, tf // 2, after=token)
            token = None
        elif i == 0:
            u, z, v = _pool_fwd(h, w4, pool_b_all + token[0, 0], pscale_row, min(4 * tm, s))
            extra = (z, v)
            token = None
        else:
            u, extra = _mla_fwd(h, mla_wts, cos_k, sin_k, tm)
        saved.append((xs, h, u, extra))
        if k < 5:
            xs, h = _norm_link(f"norm_link_{k + 1}", xs, u, post_rows(k), weights_of[k], pre_rows(k + 1, True), act_dtype(k + 1), tr)

    dx, du, (sq, dgate, dgpost) = _norm_loss("norm_loss", xs, u, target, post_rows(5), weights_of[5], tr)
    loss_part = (0.5 * jnp.sum(sq) / d).reshape(1, 1)

    d_mod = [None] * (2 * n_mod)
    d_g = [None] * 12
    d_mod[3 * 5 + 2], d_g[2 * 5 + 1] = dgate, dgpost
    sent = {}
    pool_grads = mla_grads = None

    def start_scatter(key, arrays):
        state, token = _split_start(f"scatter_start_{key}", "scatter", arrays)
        sent[key] = state
        return token

    for k in (5, 4, 3, 2, 1, 0):
        i, sub = divmod(k, 3)
        tag = f"{i}{sub}"
        xin, h, u, extra = saved[k]
        token = None
        if sub != 1:
            f = 2 * i + sub // 2
            dh, dgu, act = _ffn_bwd_act(tag, du, extra, w_in8[f], w_out4[f], tf, tf // 4)
            dw_out = _ffn_dw(f"ffn_dwout_{tag}", act, du, tw).reshape(N_DEV, FF_BLK // 2, d)
            if k == 0:
                d_pool_w = pool_grads[0].reshape(4, N_DEV, 32, POOL_GROUP).transpose(1, 0, 2, 3).reshape(N_DEV, 4 * 32, POOL_GROUP)
                token = start_scatter(tag + "_out", [dw_out, d_pool_w])
                token = start_scatter(tag + "_in", [_ffn_dw(f"ffn_dwin_{tag}", dgu, h, tw, after=token)])
            else:
                token = start_scatter(tag, [_ffn_dw(f"ffn_dwin_{tag}", dgu, h, tw), dw_out])
        elif i == 0:
            dh, dw4, (dpscale, dpb) = _pool_bwd(du, extra[0], extra[1], w4, pscale_row, min(4 * tm, s))
            pool_grads = (dw4, dpscale, dpb)
        else:
            dh, mla_grads = _mla_bwd(du, h, extra, mla_wts, cos_k, sin_k, tm)
            dwq = mla_grads["wq"]
            d_uq = jnp.concatenate([dwq[:, :, :QK_NOPE], dwq[:, :, KV_LORA:KV_LORA + QK_ROPE]], axis=2).transpose(1, 0, 2)
            token = start_scatter("mla", [mla_grads["w_in"][:, :mla_w_in.shape[2]].reshape(N_DEV, d // N_DEV, -1),
                                          d_uq.reshape(N_DEV, Q_LORA // N_DEV, -1), mla_grads["w_o"].reshape(N_DEV, d // N_DEV, d)])
            dwukp, dwuv2 = mla_grads["wukp"], mla_grads["wuv2"]
            d_uk = dwukp[:, :QK_NOPE, :KV_LORA].transpose(2, 0, 1).reshape(KV_LORA, -1)
            d_uv = jnp.where(even, dwuv2[:, :, :V_HEAD], dwuv2[:, :, V_HEAD:]).transpose(1, 0, 2).reshape(KV_LORA, -1)
            state_ukv, token = _split_start("gather_start_ukv", "gather", [d_uk, d_uv], after=token)
        if k > 0:
            dx, du, reds = _norm_link_bwd(f"norm_link_bwd_{k}", dh, xin, dx, saved[k - 1][2], pre_rows(k, False), post_rows(k - 1),
                                          weights_of[k - 1], act_dtype(k - 1), tr, after=token)
            d_mod[3 * (k - 1) + 2], d_g[2 * (k - 1) + 1] = reds[3], reds[4]
        else:
            dx, reds = _prenorm_bwd("prenorm_bwd_0", dh, xin, dx, pre_rows(0, False), tr, after=token)
        d_mod[3 * k], d_mod[3 * k + 1], d_g[2 * k] = reds[0], reds[1], reds[2]
    grad_x = dx.reshape(x.shape)

    def upd(name, parts, w, m, v):
        shape = w.shape
        r, cdim = parts.shape[1], parts.shape[2]
        return [o.reshape(shape) for o in _adamw(name, parts, w.reshape(r, cdim), m.reshape(r, cdim), v.reshape(r, cdim))]

    def landed(key, after):
        return _split_wait(f"scatter_wait_{key}", sent[key], after)

    res = {}
    w_in_s, m_in_s, v_in_s = (a.reshape(4, FF_BLK, d) for a in (w_in_t, m_in_t, v_in_t))
    w_out_s, m_out_s, v_out_s = (a.reshape(4, FF_BLK // 2, d) for a in (ffn_w_out, m_ffn_w_out, v_ffn_w_out))
    bufs_in = [lax.empty(w_in_s.shape, F32) for _ in range(4)]
    bufs_out = [lax.empty(w_out_s.shape, F32) for _ in range(4)]
    for key, k in (("12", 3), ("mla", None), ("10", 2), ("02", 1)):
        parts = landed(key, grad_x)
        if k is None:
            res["mla_w_in"] = upd("adam_mla_w_in", parts[0], mla_w_in, m_mla_w_in, v_mla_w_in)
            res["mla_w_uq"] = upd("adam_mla_w_uq", parts[1], mla_w_uq, m_mla_w_uq, v_mla_w_uq)
            res["mla_w_o"] = upd("adam_mla_w_o", parts[2], mla_w_o, m_mla_w_o, v_mla_w_o)
            uk_all, uv_all = _split_wait("gather_wait_ukv", state_ukv, grad_x)
            res["mla_w_uk"] = upd("adam_mla_w_uk", uk_all, mla_w_uk, m_mla_w_uk, v_mla_w_uk)
            res["mla_w_uv"] = upd("adam_mla_w_uv", uv_all, mla_w_uv, m_mla_w_uv, v_mla_w_uv)
            continue
        bufs_in = _adamw_slab(f"adam_ffn_w_in_{key}", parts[0], w_in_s, m_in_s, v_in_s, bufs_in, k)
        bufs_out = _adamw_slab(f"adam_ffn_w_out_{key}", parts[1], w_out_s, m_out_s, v_out_s, bufs_out, k)

    dw4, dpscale, dpb = pool_grads
    small_g = jnp.concatenate(d_mod + d_g + [dpb, dpscale, mla_grads["q_norm"], mla_grads["kv_norm"], loss_part], axis=1)
    done = [bufs_in[0], bufs_out[0], res["mla_w_o"][0], res["mla_w_uv"][0]]
    (small_g_all,) = _all_gather("gather_small_grads", [small_g], after=sum(a.reshape(-1)[:1] for a in done))
    small_g_all = small_g_all.reshape(N_DEV, -1)
    n_m = 2 * n_mod * d
    d_mod_all = small_g_all[:, :n_m].reshape(N_DEV, 2, n_mod * d)
    rest = small_g_all[:, n_m:]
    p_norm_g = lax.dynamic_slice_in_dim(rest[:, :12 * d].reshape(N_DEV, 12, d), me * (d // N_DEV), d // N_DEV, axis=2)
    p_pool_b = lax.dynamic_slice_in_dim(rest[:, 12 * d:13 * d].reshape(N_DEV, 4, POOL_GROUP), me * 32, 32, axis=2)
    p_pool_scale = rest[:, 13 * d:14 * d].reshape(N_DEV, 1, d)
    p_q_norm = lax.dynamic_slice_in_dim(rest[:, 14 * d:14 * d + Q_LORA], me * 32, 32, axis=1).reshape(N_DEV, 1, 32)
    p_kv_norm = rest[:, 14 * d + Q_LORA:14 * d + Q_LORA + KV_LORA].reshape(N_DEV, 1, KV_LORA)
    loss = jnp.sum(rest[:, -1])

    d_mod_loc = lax.dynamic_slice_in_dim(d_mod_all, me * mod_blk, mod_blk, axis=2).transpose(1, 0, 2)
    k_pad = 128
    sc_t = jnp.pad(sc_all.T, ((0, 0), (0, k_pad - N_DEV)))
    d_ada_w = _matmul("ada_dw", sc_t, jnp.pad(d_mod_loc, ((0, 0), (0, k_pad - N_DEV), (0, 0))), a_blk=(d, k_pad),
                      a_map=lambda i, k: (0, 0), b_blk=(None, k_pad, mod_blk), b_map=lambda i, k: (i, 0, 0), o_shape=(2, d, mod_blk),
                      o_blk=(None, d, mod_blk), o_map=lambda i, k: (i, 0, 0), grid=(2, 1), contract=NN, out_dtype=F32)
    res["ada_w"] = upd("adam_ada_w", d_ada_w.reshape(1, 2 * d, mod_blk), ada_w, m_ada_w, v_ada_w)
    res["ada_b"] = upd("adam_ada_b", d_mod_all.reshape(N_DEV, 2, n_mod * d), ada_b, m_ada_b, v_ada_b)
    res["norm_g"] = upd("adam_norm_g", p_norm_g, norm_g, m_norm_g, v_norm_g)
    res["pool_b"] = upd("adam_pool_b", p_pool_b, pool_b, m_pool_b, v_pool_b)
    res["pool_scale"] = upd("adam_pool_scale", p_pool_scale, pool_scale, m_pool_scale, v_pool_scale)
    res["mla_q_norm"] = upd("adam_mla_q_norm", p_q_norm, mla_q_norm, m_mla_q_norm, v_mla_q_norm)
    res["mla_kv_norm"] = upd("adam_mla_kv_norm", p_kv_norm, mla_kv_norm, m_mla_kv_norm, v_mla_kv_norm)

    p_out, p_pool_w = landed("00_out", res["ada_w"][1])
    bufs_out = _adamw_slab("adam_ffn_w_out_00", p_out, w_out_s, m_out_s, v_out_s, bufs_out, 0)
    res["pool_w"] = upd("adam_pool_w", p_pool_w, pool_w, m_pool_w, v_pool_w)
    (p_in,) = landed("00_in", res["pool_w"][1])
    bufs_in = _adamw_slab("adam_ffn_w_in_00", p_in, w_in_s, m_in_s, v_in_s, bufs_in, 0)
    res["ffn_w_in"] = [jnp.swapaxes(b.reshape(w_in_t.shape), 2, 3) for b in bufs_in]
    res["ffn_w_out"] = [b.reshape(ffn_w_out.shape) for b in bufs_out]

    order = ["ada_w", "ada_b", "norm_g", "ffn_w_in", "ffn_w_out", "pool_w", "pool_b", "pool_scale", "mla_w_in", "mla_q_norm",
             "mla_kv_norm", "mla_w_uq", "mla_w_uk", "mla_w_uv", "mla_w_o"]
    outs = [loss, grad_x]
    for j in range(4):
        outs += [res[name][j] for name in order]
    return tuple(outs)
```

```python
import jax
import jax.numpy as jnp
from jax import lax
from jax.experimental import pallas as pl
from jax.experimental.pallas import tpu as pltpu

F32 = jnp.float32
BF16 = jnp.bfloat16
N_DEV = 8
MESH = pl.DeviceIdType.MESH

D_MODEL = 1024
N_HEADS = 16
QK_NOPE = 64
QK_ROPE = 32
V_HEAD = 64
Q_LORA = 256
KV_LORA = 128
LAT_PAD = 512
QK_PAD = 256
ONE_COL = 160
D_FF = 2816
FF_BLK = 2 * D_FF // N_DEV
POOL_WINDOWS = (2, 4, 8, 16)
POOL_GROUP = 256
ROPE_THETA = 10000.0
EPS = 1e-6
ATTN_SCALE = (QK_NOPE + QK_ROPE) ** -0.5
LOG2_E = 1.4426950408889634
ADAM_LR, ADAM_B1, ADAM_B2, ADAM_EPS, ADAM_WD, ADAM_STEP = 0.001, 0.9, 0.999, 1e-08, 0.01, 10
VMEM_LIMIT = 56 * 1024 * 1024

NN = ((1,), (0,))
NT = ((1,), (1,))
TN = ((0,), (0,))


def _params(**kw):
    return pltpu.CompilerParams(vmem_limit_bytes=VMEM_LIMIT, **kw)


def _dot(a, b, contract):
    return lax.dot_general(a, b, (contract, ((), ())), preferred_element_type=F32)


def _matmul(name, a, b, *, a_blk, a_map, b_blk, b_map, o_shape, o_blk, o_map, grid, contract, out_dtype,
            bias=None, bias_blk=None, bias_map=None, after=None):
    n_k = grid[-1]
    k_axis = len(grid) - 1
    acc_shape = tuple(d for d in o_blk if d is not None)

    def body(*refs):
        a_ref, b_ref = refs[:2]
        bias_ref = refs[2] if bias is not None else None
        if n_k == 1:
            r = _dot(a_ref[...].astype(BF16), b_ref[...].astype(BF16), contract)
            if bias is not None:
                r = r + bias_ref[...]
            refs[-1][...] = r.astype(refs[-1].dtype)
            return
        o_ref, acc = refs[-2:]
        k = pl.program_id(k_axis)

        @pl.when(k == 0)
        def _():
            acc[...] = jnp.zeros_like(acc)

        acc[...] += _dot(a_ref[...].astype(BF16), b_ref[...].astype(BF16), contract)

        @pl.when(k == n_k - 1)
        def _():
            r = acc[...]
            if bias is not None:
                r = r + bias_ref[...]
            o_ref[...] = r.astype(o_ref.dtype)

    in_specs = [pl.BlockSpec(a_blk, a_map), pl.BlockSpec(b_blk, b_map)]
    args = [a, b]
    if bias is not None:
        in_specs.append(pl.BlockSpec(bias_blk, bias_map))
        args.append(bias)
    if after is not None:
        in_specs.append(pl.BlockSpec(memory_space=pl.ANY))
        args.append(after)
    return pl.pallas_call(
        body, name=name, grid=grid, in_specs=in_specs, out_specs=pl.BlockSpec(o_blk, o_map),
        out_shape=jax.ShapeDtypeStruct(o_shape, out_dtype), scratch_shapes=[pltpu.VMEM(acc_shape, F32)] if n_k > 1 else [],
        compiler_params=_params(),
    )(*args)


def _rowmap(name, fn, ins, outs, reds, grid, after=None):
    n_in, n_out, n_red = len(ins), len(outs), len(reds)
    extra = [] if after is None else [after]

    def body(*refs):
        in_refs = refs[:n_in]
        out_refs = refs[n_in + len(extra):n_in + len(extra) + n_out]
        red_refs = refs[n_in + len(extra) + n_out:]
        out_vals, red_vals = fn(*[r[...] for r in in_refs])
        for r, v in zip(out_refs, out_vals):
            r[...] = v.astype(r.dtype)
        if n_red:
            first = pl.program_id(0) == 0
            for ax in range(1, len(grid)):
                first = jnp.logical_and(first, pl.program_id(ax) == 0)

            @pl.when(first)
            def _():
                for r in red_refs:
                    r[...] = jnp.zeros_like(r)

            for r, v in zip(red_refs, red_vals):
                r[...] += v

    res = pl.pallas_call(
        body, name=name, grid=grid,
        in_specs=[pl.BlockSpec(blk, imap) for _, blk, imap in ins] + [pl.BlockSpec(memory_space=pl.ANY)] * len(extra),
        out_specs=[pl.BlockSpec(blk, imap) for _, blk, imap in list(outs) + list(reds)],
        out_shape=[sds for sds, _, _ in list(outs) + list(reds)],
        compiler_params=_params(),
    )(*[a for a, _, _ in ins], *extra)
    return res[:n_out], res[n_out:]


def _sds(shape, dtype):
    return jax.ShapeDtypeStruct(shape, dtype)


def _tile(a, tm):
    return (a, (tm, a.shape[1]), lambda i: (i, 0))


def _row(a):
    return (a, (1, a.shape[1]), lambda i: (0, 0))


def _otile(n, c, dtype, tm):
    return (_sds((n, c), dtype), (tm, c), lambda i: (i, 0))


def _ored(c):
    return (_sds((1, c), F32), (1, c), lambda i: (0, 0))


def _colsum(v):
    return jnp.sum(v, axis=0, keepdims=True)


def _rstd(v):
    return lax.rsqrt(jnp.mean(v * v, axis=-1, keepdims=True) + EPS)


def _pre(xv, g, sc, sh):
    return xv * _rstd(xv) * g * (1.0 + sc) + sh


def _post(xv, uv, g, gt, weight):
    return xv + weight * (1.0 + gt) * (uv * _rstd(uv) * g)


def _post_bwd(dv, uv, g, gt, weight):
    r = _rstd(uv)
    un = uv * r
    dy = dv * (weight * (1.0 + gt))
    a = dy * g
    du = r * (a - un * jnp.mean(a * un, axis=-1, keepdims=True))
    return du, (_colsum(dv * (weight * (un * g))), _colsum(dy * un))


def _pre_bwd(dhv, xv, dv, g, sc):
    dhv = dhv.astype(F32)
    r = _rstd(xv)
    xn = xv * r
    b = dhv * (g * (1.0 + sc))
    dx = dv + r * (b - xn * jnp.mean(b * xn, axis=-1, keepdims=True))
    return dx, (_colsum(dhv), _colsum(dhv * (xn * g)), _colsum(dhv * ((1.0 + sc) * xn)))


def _rowk(rows, k):
    return (rows, (None, 1, rows.shape[2]), lambda i: (k, 0, 0))


def _prenorm(name, x, pre, out_dtype, tm):
    n, d = x.shape
    (h,), _ = _rowmap(name, lambda xv, g, sc, sh: ((_pre(xv, g, sc, sh),), ()), [_tile(x, tm), *pre], [_otile(n, d, out_dtype, tm)],
                      [], (n // tm,))
    return h


def _norm_link(name, x, u, post, weight, pre, out_dtype, tm):
    n, d = x.shape

    def fn(xv, uv, g, gt, g2, sc, sh):
        xn = _post(xv, uv, g, gt, weight)
        return (xn, _pre(xn, g2, sc, sh)), ()

    (xn, h), _ = _rowmap(name, fn, [_tile(x, tm), _tile(u, tm), *post, *pre], [_otile(n, d, F32, tm), _otile(n, d, out_dtype, tm)],
                         [], (n // tm,))
    return xn, h


def _norm_loss(name, x, u, target, post, weight, tm):
    n, d = x.shape

    def fn(xv, uv, tv, g, gt):
        e = _post(xv, uv, g, gt, weight) - tv
        dv = e * (1.0 / d)
        du, reds = _post_bwd(dv, uv, g, gt, weight)
        return (dv, du), (_colsum(e * e), *reds)

    (dx, du), reds = _rowmap(name, fn, [_tile(x, tm), _tile(u, tm), _tile(target, tm), *post],
                             [_otile(n, d, F32, tm), _otile(n, d, BF16, tm)], [_ored(d)] * 3, (n // tm,))
    return dx, du, reds


def _norm_link_bwd(name, dh, x, dout, u_prev, pre, post_prev, weight_prev, out_dtype, tm, after=None):
    n, d = x.shape

    def fn(dhv, xv, dv, uv, g, sc, g2, gt):
        dx, reds = _pre_bwd(dhv, xv, dv, g, sc)
        du, reds_prev = _post_bwd(dx, uv, g2, gt, weight_prev)
        return (dx, du), (*reds, *reds_prev)

    (dx, du), reds = _rowmap(name, fn, [_tile(dh, tm), _tile(x, tm), _tile(dout, tm), _tile(u_prev, tm), *pre, *post_prev],
                             [_otile(n, d, F32, tm), _otile(n, d, out_dtype, tm)], [_ored(d)] * 5, (n // tm,), after=after)
    return dx, du, reds


def _prenorm_bwd(name, dh, x, dout, pre, tm, after=None):
    n, d = x.shape

    def fn(dhv, xv, dv, g, sc):
        dx, reds = _pre_bwd(dhv, xv, dv, g, sc)
        return (dx,), reds

    (dx,), reds = _rowmap(name, fn, [_tile(dh, tm), _tile(x, tm), _tile(dout, tm), *pre], [_otile(n, d, F32, tm)], [_ored(d)] * 3,
                          (n // tm,), after=after)
    return dx, reds


def _ffn_fwd(tag, h, w_in8, w_out4, tm, sub, after=None):
    s, d = h.shape

    extra = [] if after is None else [after]

    def body(h_ref, wg_ref, wu_ref, wo_ref, *rest):
        u_ref, gu_ref = rest[-2:]

        @pl.when(pl.program_id(1) == 0)
        def _():
            u_ref[...] = jnp.zeros_like(u_ref)

        for r in range(tm // sub):
            rows = pl.ds(r * sub, sub)
            hv = h_ref[rows, :]
            gate = _dot(hv, wg_ref[...], NT)
            up = _dot(hv, wu_ref[...], NT)
            gu_ref[0, rows, :] = gate.astype(BF16)
            gu_ref[1, rows, :] = up.astype(BF16)
            u_ref[rows, :] += _dot((gate * jax.nn.sigmoid(gate) * up).astype(BF16), wo_ref[...], NN)

    w_blk = (None, FF_BLK, d)
    return pl.pallas_call(
        body, name=f"ffn_fwd_{tag}", grid=(s // tm, 4),
        in_specs=[pl.BlockSpec((tm, d), lambda i, j: (i, 0)), pl.BlockSpec(w_blk, lambda i, j: (j, 0, 0)),
                  pl.BlockSpec(w_blk, lambda i, j: (j + 4, 0, 0)), pl.BlockSpec((None, FF_BLK, d), lambda i, j: (j, 0, 0))]
        + [pl.BlockSpec(memory_space=pl.ANY)] * len(extra),
        out_specs=[pl.BlockSpec((tm, d), lambda i, j: (i, 0)), pl.BlockSpec((2, None, tm, FF_BLK), lambda i, j: (0, j, i, 0))],
        out_shape=[_sds((s, d), F32), _sds((2, 4, s, FF_BLK), BF16)], compiler_params=_params(),
    )(h, w_in8, w_in8, w_out4, *extra)


def _ffn_bwd_act(tag, du, gu, w_in8, w_out4, tm, sub):
    s, d = du.shape

    def body(du_ref, gu_ref, wg_ref, wu_ref, wo_ref, dh_ref, dgu_ref, act_ref):
        @pl.when(pl.program_id(1) == 0)
        def _():
            dh_ref[...] = jnp.zeros_like(dh_ref)

        for r in range(tm // sub):
            rows = pl.ds(r * sub, sub)
            dact = _dot(du_ref[rows, :], wo_ref[...], NT)
            gate, up = gu_ref[0, rows, :].astype(F32), gu_ref[1, rows, :].astype(F32)
            sg = jax.nn.sigmoid(gate)
            silu = gate * sg
            dg = dact * up * (sg * (1.0 + gate * (1.0 - sg)))
            dup = dact * silu
            dgu_ref[0, :, rows] = dg.T.astype(BF16)
            dgu_ref[1, :, rows] = dup.T.astype(BF16)
            act_ref[:, rows] = (silu * up).T.astype(BF16)
            dh_ref[rows, :] += _dot(dg.astype(BF16), wg_ref[...], NN) + _dot(dup.astype(BF16), wu_ref[...], NN)

    w_blk = (None, FF_BLK, d)
    dh, dgu_t, act_t = pl.pallas_call(
        body, name=f"ffn_bwd_{tag}", grid=(s // tm, 4),
        in_specs=[pl.BlockSpec((tm, d), lambda i, j: (i, 0)), pl.BlockSpec((2, None, tm, FF_BLK), lambda i, j: (0, j, i, 0)),
                  pl.BlockSpec(w_blk, lambda i, j: (j, 0, 0)), pl.BlockSpec(w_blk, lambda i, j: (j + 4, 0, 0)),
                  pl.BlockSpec((None, FF_BLK, d), lambda i, j: (j, 0, 0))],
        out_specs=[pl.BlockSpec((tm, d), lambda i, j: (i, 0)), pl.BlockSpec((2, None, FF_BLK, tm), lambda i, j: (0, j, 0, i)),
                   pl.BlockSpec((None, FF_BLK, tm), lambda i, j: (j, 0, i))],
        out_shape=[_sds((s, d), F32), _sds((2, 4, FF_BLK, s), BF16), _sds((4, FF_BLK, s), BF16)], compiler_params=_params(),
    )(du, gu, w_in8, w_in8, w_out4)
    return dh, dgu_t.reshape(8, FF_BLK, s), act_t


def _ffn_dw(name, lhs_t, rhs, tk, after=None):
    n_g, _, s = lhs_t.shape
    d = rhs.shape[1]
    return _matmul(name, lhs_t, rhs, a_blk=(None, FF_BLK, tk), a_map=lambda g, k: (g, 0, k), b_blk=(tk, d), b_map=lambda g, k: (k, 0),
                   o_shape=(n_g, FF_BLK, d), o_blk=(None, FF_BLK, d), o_map=lambda g, k: (g, 0, 0), grid=(n_g, s // tk),
                   contract=NN, out_dtype=BF16, after=after)


def _window_sum(x, window, transpose):
    s = x.shape[0]
    t = lax.broadcasted_iota(jnp.int32, (s, 1), 0)
    half = window // 2
    cnt = jnp.minimum(t + half, s) - jnp.maximum(t - half, 0)
    inv = 1.0 / cnt.astype(F32)
    if transpose:
        x = x * inv
        offsets = range(-half + 1, half + 1)
    else:
        offsets = range(-half, half)
    acc = jnp.zeros_like(x)
    for o in offsets:
        shifted = x if o == 0 else pltpu.roll(x, (-o) % s, 0)
        valid = jnp.logical_and(t + o >= 0, t + o < s)
        acc = acc + jnp.where(valid, shifted, 0.0)
    return acc if transpose else acc * inv


def _pool_mix(name, x, transpose, out_dtype):
    s, d = x.shape

    def body(x_ref, o_ref):
        g = pl.program_id(0)
        for gi, window in enumerate(POOL_WINDOWS):
            @pl.when(g == gi)
            def _(window=window):
                xv = x_ref[...].astype(F32)
                o_ref[...] = (_window_sum(xv, window, transpose) - xv).astype(o_ref.dtype)

    return pl.pallas_call(
        body, name=name, grid=(len(POOL_WINDOWS),), in_specs=[pl.BlockSpec((s, POOL_GROUP), lambda g: (0, g))],
        out_specs=pl.BlockSpec((s, POOL_GROUP), lambda g: (0, g)), out_shape=_sds((s, d), out_dtype), compiler_params=_params(),
    )(x)


def _pool_fwd(h, w4, bias, pscale, tm):
    s, d = h.shape
    nt = s // tm
    z = _pool_mix("pool_mix", h, False, BF16)
    v = _matmul("pool_proj", z, w4, a_blk=(tm, POOL_GROUP), a_map=lambda i, g, k: (i, g), b_blk=(None, POOL_GROUP, POOL_GROUP),
                b_map=lambda i, g, k: (g, 0, 0), o_shape=(s, d), o_blk=(tm, POOL_GROUP), o_map=lambda i, g, k: (i, g),
                grid=(nt, 4, 1), contract=NN, out_dtype=F32, bias=bias, bias_blk=(1, POOL_GROUP), bias_map=lambda i, g, k: (0, g))
    tr = min(512, s)
    (u,), _ = _rowmap("pool_scale", lambda vv, ps: ((vv * ps,), ()), [_tile(v, tr), _row(pscale)], [_otile(s, d, F32, tr)], [],
                      (s // tr,))
    return u, z, v


def _pool_bwd(du, z, v, w4, pscale, tm):
    s, d = du.shape
    nt = s // tm

    def fn(duv, vv, ps):
        dv = duv * ps
        return (dv,), (_colsum(duv * vv), _colsum(dv))

    tr = min(512, s)
    (dv,), reds = _rowmap("pool_dscale", fn, [_tile(du, tr), _tile(v, tr), _row(pscale)], [_otile(s, d, BF16, tr)],
                          [_ored(d), _ored(d)], (s // tr,))
    dw4 = _matmul("pool_dw", z, dv, a_blk=(tm, POOL_GROUP), a_map=lambda g, k: (k, g), b_blk=(tm, POOL_GROUP),
                  b_map=lambda g, k: (k, g), o_shape=(4, POOL_GROUP, POOL_GROUP), o_blk=(None, POOL_GROUP, POOL_GROUP),
                  o_map=lambda g, k: (g, 0, 0), grid=(4, nt), contract=TN, out_dtype=F32)
    dz = _matmul("pool_dz", dv, w4, a_blk=(tm, POOL_GROUP), a_map=lambda i, g, k: (i, g), b_blk=(None, POOL_GROUP, POOL_GROUP),
                 b_map=lambda i, g, k: (g, 0, 0), o_shape=(s, d), o_blk=(tm, POOL_GROUP), o_map=lambda i, g, k: (i, g),
                 grid=(nt, 4, 1), contract=NT, out_dtype=F32)
    dh = _pool_mix("pool_mix_t", dz, True, F32)
    return dh, dw4, reds


def _lane(shape):
    return lax.broadcasted_iota(jnp.int32, shape, 1)


def _rope_swap(v, transpose):
    half = QK_ROPE // 2
    lane = _lane(v.shape)
    up = pltpu.roll(v, v.shape[1] - half, 1)
    down = pltpu.roll(v, half, 1)
    if transpose:
        return jnp.where(lane < half, up, jnp.where(lane < QK_ROPE, -down, 0.0))
    return jnp.where(lane < half, -up, jnp.where(lane < QK_ROPE, down, 0.0))


def _rope(v, cos, sin):
    return v * cos + _rope_swap(v, False) * sin


def _rope_t(g, cos, sin):
    return g * cos + _rope_swap(g * sin, True)


def _mla_mid(lat, q_norm, kv_norm, cos_k, sin_k, tm):
    s = lat.shape[0]

    def fn(lv, qn, kn, cs, sn):
        cq = lv[:, :Q_LORA]
        ckv = lv[:, Q_LORA:Q_LORA + KV_LORA]
        kr = lv[:, Q_LORA + KV_LORA:]
        cq = cq * _rstd(cq) * qn
        ckv = ckv * _rstd(ckv) * kn
        k_rope = jnp.where(_lane(kr.shape) == ONE_COL - KV_LORA, 1.0, _rope(kr, cs, sn))
        return (cq, jnp.concatenate([ckv, k_rope], axis=1)), ()

    (cq, kcat), _ = _rowmap("mla_mid", fn, [_tile(lat, tm), _row(q_norm), _row(kv_norm), _tile(cos_k, tm), _tile(sin_k, tm)],
                            [_otile(s, Q_LORA, BF16, tm), _otile(s, QK_PAD, BF16, tm)], [], (s // tm,))
    return cq, kcat


def _mla_mid_bwd(lat, dcq, dkcat, dv, q_norm, kv_norm, cos_k, sin_k, tm):
    s = lat.shape[0]

    def fn(lv, dq, dk, dvv, qn, kn, cs, sn):
        dk = dk * (1.0 / LOG2_E)
        cq = lv[:, :Q_LORA]
        ckv = lv[:, Q_LORA:Q_LORA + KV_LORA]
        rq, rk = _rstd(cq), _rstd(ckv)
        cqn, ckn = cq * rq, ckv * rk
        a = dq * qn
        d_cq = rq * (a - cqn * jnp.mean(a * cqn, axis=-1, keepdims=True))
        dckv = dk[:, :KV_LORA] + dvv
        a2 = dckv * kn
        d_ckv = rk * (a2 - ckn * jnp.mean(a2 * ckn, axis=-1, keepdims=True))
        d_kr = _rope_t(dk[:, KV_LORA:], cs, sn)
        return (jnp.concatenate([d_cq, d_ckv, d_kr], axis=1),), (_colsum(dq * cqn), _colsum(dckv * ckn))

    (dlat,), reds = _rowmap(
        "mla_mid_bwd", fn,
        [_tile(lat, tm), _tile(dcq, tm), _tile(dkcat, tm), _tile(dv, tm), _row(q_norm), _row(kv_norm), _tile(cos_k, tm),
         _tile(sin_k, tm)],
        [_otile(s, LAT_PAD, BF16, tm)], [_ored(Q_LORA), _ored(KV_LORA)], (s // tm,))
    return dlat, reds


def _mla_q(cq, wq, wukp, cos_k, sin_k, tm):
    s = cq.shape[0]

    def body(cq_ref, wq_ref, wuk_ref, cos_ref, sin_ref, o_ref):
        cqv, cs, sn = cq_ref[...], cos_ref[...], sin_ref[...]
        for h in range(N_HEADS):
            aq = _dot(cqv, wq_ref[h], NN)
            qlat = _dot(aq.astype(BF16), wuk_ref[h], NN)
            roped = _rope(aq[:, KV_LORA:], cs, sn)
            o_ref[h] = (jnp.concatenate([qlat[:, :KV_LORA], roped], axis=1) * (ATTN_SCALE * LOG2_E)).astype(o_ref.dtype)

    wblk = pl.BlockSpec((N_HEADS, QK_PAD, QK_PAD), lambda i: (0, 0, 0))
    tblk = pl.BlockSpec((tm, KV_LORA), lambda i: (i, 0))
    return pl.pallas_call(
        body, name="mla_q", grid=(s // tm,),
        in_specs=[pl.BlockSpec((tm, Q_LORA), lambda i: (i, 0)), wblk, wblk, tblk, tblk],
        out_specs=pl.BlockSpec((N_HEADS, tm, QK_PAD), lambda i: (0, i, 0)), out_shape=_sds((N_HEADS, s, QK_PAD), BF16),
        compiler_params=_params(),
    )(cq, wq, wukp, cos_k, sin_k)


def _mla_q_bwd(cq, wq, wukp, cos_k, sin_k, dqcat, tm):
    s = cq.shape[0]

    def body(cq_ref, wq_ref, wuk_ref, cos_ref, sin_ref, dq_ref, dcq_ref, dwq_ref, dwuk_ref):
        @pl.when(pl.program_id(0) == 0)
        def _():
            dwq_ref[...] = jnp.zeros_like(dwq_ref)
            dwuk_ref[...] = jnp.zeros_like(dwuk_ref)

        cqv, cs, sn = cq_ref[...], cos_ref[...], sin_ref[...]
        d_cq = jnp.zeros((tm, Q_LORA), F32)
        for h in range(N_HEADS):
            aq = _dot(cqv, wq_ref[h], NN).astype(BF16)
            g = dq_ref[h].astype(F32) * ATTN_SCALE
            gl, gr = g[:, :KV_LORA], g[:, KV_LORA:]
            dqlat = jnp.concatenate([gl, jnp.zeros_like(gl)], axis=1).astype(BF16)
            d_rope = _rope_t(gr, cs, sn)
            daq = _dot(dqlat, wuk_ref[h], NT) + jnp.concatenate([jnp.zeros_like(d_rope), d_rope], axis=1)
            daq_b = daq.astype(BF16)
            dwuk_ref[h] += _dot(aq, dqlat, TN)
            dwq_ref[h] += _dot(cqv, daq_b, TN)
            d_cq = d_cq + _dot(daq_b, wq_ref[h], NT)
        dcq_ref[...] = d_cq

    wblk = pl.BlockSpec((N_HEADS, QK_PAD, QK_PAD), lambda i: (0, 0, 0))
    tblk = pl.BlockSpec((tm, KV_LORA), lambda i: (i, 0))
    return pl.pallas_call(
        body, name="mla_q_bwd", grid=(s // tm,),
        in_specs=[pl.BlockSpec((tm, Q_LORA), lambda i: (i, 0)), wblk, wblk, tblk, tblk,
                  pl.BlockSpec((N_HEADS, tm, QK_PAD), lambda i: (0, i, 0))],
        out_specs=[pl.BlockSpec((tm, Q_LORA), lambda i: (i, 0)), wblk, wblk],
        out_shape=[_sds((s, Q_LORA), F32), _sds((N_HEADS, QK_PAD, QK_PAD), F32), _sds((N_HEADS, QK_PAD, QK_PAD), F32)],
        compiler_params=_params(),
    )(cq, wq, wukp, cos_k, sin_k, dqcat)


def _flash_fwd(qcat, kcat, tq, tk):
    n_h, s, _ = qcat.shape
    n_k = s // tk

    def body(q_ref, k_ref, o_ref, lse_ref):
        q = q_ref[...]
        m = jnp.full((tq, 1), -1e30, F32)
        acc = jnp.zeros((tq, QK_PAD), F32)
        for kk in range(n_k):
            k = k_ref[pl.ds(kk * tk, tk), :]
            sc = _dot(q, k, NT)
            m_new = jnp.maximum(m, jnp.max(sc, axis=1, keepdims=True))
            p = jnp.exp2(sc - m_new).astype(BF16)
            acc = jnp.exp2(m - m_new) * acc + _dot(p, k, NN)
            m = m_new
        l = jnp.sum(jnp.where(_lane(acc.shape) == ONE_COL, acc, 0.0), axis=1, keepdims=True)
        o_ref[...] = (acc[:, :KV_LORA] / l).astype(o_ref.dtype)
        lse_ref[...] = m + jnp.log2(l)

    return pl.pallas_call(
        body, name="mla_attn", grid=(n_h, s // tq),
        in_specs=[pl.BlockSpec((None, tq, QK_PAD), lambda h, i: (h, i, 0)), pl.BlockSpec((s, QK_PAD), lambda h, i: (0, 0))],
        out_specs=[pl.BlockSpec((None, tq, KV_LORA), lambda h, i: (h, i, 0)), pl.BlockSpec((None, tq, 1), lambda h, i: (h, i, 0))],
        out_shape=[_sds((n_h, s, KV_LORA), BF16), _sds((n_h, s, 1), F32)], compiler_params=_params(),
    )(qcat, kcat)


def _flash_bwd(qcat, kcat, o, do, lse, tq, tk):
    n_h, s, _ = qcat.shape
    n_k = s // tk

    def body(q_ref, k_ref, v_ref, o_ref, do_ref, lse_ref, dq_ref, dk_ref, dv_ref, dq_acc):
        h, i = pl.program_id(0), pl.program_id(1)

        @pl.when(jnp.logical_and(h == 0, i == 0))
        def _():
            dk_ref[...] = jnp.zeros_like(dk_ref)
            dv_ref[...] = jnp.zeros_like(dv_ref)

        q = q_ref[...]
        dov = do_ref[...]
        lse_v = lse_ref[...]
        delta = jnp.sum(dov.astype(F32) * o_ref[...].astype(F32), axis=1, keepdims=True)
        dq_acc[...] = jnp.zeros_like(dq_acc)

        for kk in range(n_k):
            rows = pl.ds(kk * tk, tk)
            k = k_ref[rows, :]
            p = jnp.exp2(_dot(q, k, NT) - lse_v)
            dp = _dot(dov, v_ref[rows, :], NT)
            ds = (p * (dp - delta)).astype(BF16)
            dq_acc[...] += _dot(ds, k, NN)
            dv_ref[rows, :] += _dot(p.astype(BF16), dov, TN)
            dk_ref[rows, :] += _dot(ds, q, TN)
        dq_ref[...] = dq_acc[...].astype(dq_ref.dtype)

    qblk = pl.BlockSpec((None, tq, QK_PAD), lambda h, i: (h, i, 0))
    oblk = pl.BlockSpec((None, tq, KV_LORA), lambda h, i: (h, i, 0))
    return pl.pallas_call(
        body, name="mla_attn_bwd", grid=(n_h, s // tq),
        in_specs=[qblk, pl.BlockSpec((s, QK_PAD), lambda h, i: (0, 0)), pl.BlockSpec((s, KV_LORA), lambda h, i: (0, 0)), oblk, oblk,
                  pl.BlockSpec((None, tq, 1), lambda h, i: (h, i, 0))],
        out_specs=[qblk, pl.BlockSpec((s, QK_PAD), lambda h, i: (0, 0)), pl.BlockSpec((s, KV_LORA), lambda h, i: (0, 0))],
        out_shape=[_sds((n_h, s, QK_PAD), BF16), _sds((s, QK_PAD), F32), _sds((s, KV_LORA), F32)],
        scratch_shapes=[pltpu.VMEM((tq, QK_PAD), F32)], compiler_params=_params(),
    )(qcat, kcat, kcat, o, do, lse)


def _mla_uv(o_lat, wuv2, do, tm):
    n_h, s, _ = o_lat.shape
    d = n_h * V_HEAD
    pair = 2 * V_HEAD
    lat_blk = pl.BlockSpec((n_h, tm, KV_LORA), lambda i: (0, i, 0))
    w_blk = pl.BlockSpec((n_h, KV_LORA, pair), lambda i: (0, 0, 0))
    row_blk = pl.BlockSpec((tm, d), lambda i: (i, 0))

    if do is None:
        def body(a_ref, w_ref, o_ref):
            for p in range(n_h // 2):
                o_ref[:, p * pair:(p + 1) * pair] = (
                    _dot(a_ref[2 * p], w_ref[2 * p], NN) + _dot(a_ref[2 * p + 1], w_ref[2 * p + 1], NN)).astype(o_ref.dtype)

        return pl.pallas_call(body, name="mla_uv", grid=(s // tm,), in_specs=[lat_blk, w_blk], out_specs=row_blk,
                              out_shape=_sds((s, d), BF16), compiler_params=_params())(o_lat, wuv2)

    def body(a_ref, w_ref, do_ref, dlat_ref, dw_ref):
        @pl.when(pl.program_id(0) == 0)
        def _():
            dw_ref[...] = jnp.zeros_like(dw_ref)

        for h in range(n_h):
            dov = do_ref[:, (h // 2) * pair:(h // 2 + 1) * pair]
            dlat_ref[h] = _dot(dov, w_ref[h], NT).astype(dlat_ref.dtype)
            dw_ref[h] += _dot(a_ref[h], dov, TN)

    return pl.pallas_call(body, name="mla_uv_bwd", grid=(s // tm,), in_specs=[lat_blk, w_blk, row_blk], out_specs=[lat_blk, w_blk],
                          out_shape=[_sds((n_h, s, KV_LORA), BF16), _sds((n_h, KV_LORA, pair), F32)], compiler_params=_params(),
                          )(o_lat, wuv2, do)


def _mla_fwd(h, wts, cos_k, sin_k, tm):
    s, d = h.shape
    nt = s // tm
    lat = _matmul("mla_lat", h, wts["w_in"], a_blk=(tm, d), a_map=lambda i, k: (i, 0), b_blk=(d, LAT_PAD), b_map=lambda i, k: (0, 0),
                  o_shape=(s, LAT_PAD), o_blk=(tm, LAT_PAD), o_map=lambda i, k: (i, 0), grid=(nt, 1), contract=NN, out_dtype=F32)
    cq, kcat = _mla_mid(lat, wts["q_norm"], wts["kv_norm"], cos_k, sin_k, tm)
    qcat = _mla_q(cq, wts["wq"], wts["wukp"], cos_k, sin_k, tm)
    o_lat, lse = _flash_fwd(qcat, kcat, min(2 * tm, s), tm)
    o = _mla_uv(o_lat, wts["wuv2"], None, tm)
    u = _matmul("mla_out", o, wts["w_o"], a_blk=(tm, d), a_map=lambda i, k: (i, 0), b_blk=(d, d), b_map=lambda i, k: (0, 0),
                o_shape=(s, d), o_blk=(tm, d), o_map=lambda i, k: (i, 0), grid=(nt, 1), contract=NN, out_dtype=F32)
    return u, (lat, cq, kcat, qcat, o_lat, lse, o)


def _mla_bwd(du, h, saved, wts, cos_k, sin_k, tm):
    lat, cq, kcat, qcat, o_lat, lse, o = saved
    s, d = h.shape
    nt = s // tm
    do = _matmul("mla_do", du, wts["w_o"], a_blk=(tm, d), a_map=lambda i, k: (i, 0), b_blk=(d, d), b_map=lambda i, k: (0, 0),
                 o_shape=(s, d), o_blk=(tm, d), o_map=lambda i, k: (i, 0), grid=(nt, 1), contract=NT, out_dtype=BF16)
    dw_o = _matmul("mla_dwo", o, du, a_blk=(tm, d), a_map=lambda k: (k, 0), b_blk=(tm, d), b_map=lambda k: (k, 0),
                   o_shape=(d, d), o_blk=(d, d), o_map=lambda k: (0, 0), grid=(nt,), contract=TN, out_dtype=F32)
    do_lat, dwuv2 = _mla_uv(o_lat, wts["wuv2"], do, tm)
    dqcat, dkcat, dv = _flash_bwd(qcat, kcat, o_lat, do_lat, lse, min(2 * tm, s), tm)
    dcq, dwq, dwukp = _mla_q_bwd(cq, wts["wq"], wts["wukp"], cos_k, sin_k, dqcat, tm)
    dlat, (dqn, dkn) = _mla_mid_bwd(lat, dcq, dkcat, dv, wts["q_norm"], wts["kv_norm"], cos_k, sin_k, tm)
    dh = _matmul("mla_dh", dlat, wts["w_in"], a_blk=(tm, LAT_PAD), a_map=lambda i, k: (i, 0), b_blk=(d, LAT_PAD),
                 b_map=lambda i, k: (0, 0), o_shape=(s, d), o_blk=(tm, d), o_map=lambda i, k: (i, 0), grid=(nt, 1), contract=NT,
                 out_dtype=F32)
    dw_in = _matmul("mla_dwin", h, dlat, a_blk=(tm, d), a_map=lambda k: (k, 0), b_blk=(tm, LAT_PAD), b_map=lambda k: (k, 0),
                    o_shape=(d, LAT_PAD), o_blk=(d, LAT_PAD), o_map=lambda k: (0, 0), grid=(nt,), contract=TN, out_dtype=F32)
    return dh, dict(w_in=dw_in, wq=dwq, wukp=dwukp, wuv2=dwuv2, w_o=dw_o, q_norm=dqn, kv_norm=dkn)


def _adamw(name, parts, w, m, v):
    n_parts, r, c = parts.shape
    tr = r
    for cand in (256, 128, 64, 32, 16, 8):
        if r > cand and r % cand == 0:
            tr = cand
            break

    def body(p_ref, w_ref, m_ref, v_ref, g_ref, d_ref, nm_ref, nv_ref):
        g = p_ref[0].astype(F32)
        for k in range(1, n_parts):
            g = g + p_ref[k].astype(F32)
        nm = ADAM_B1 * m_ref[...] + (1.0 - ADAM_B1) * g
        nv = ADAM_B2 * v_ref[...] + (1.0 - ADAM_B2) * (g * g)
        m_hat = nm / (1.0 - ADAM_B1 ** ADAM_STEP)
        v_hat = nv / (1.0 - ADAM_B2 ** ADAM_STEP)
        g_ref[...] = g
        d_ref[...] = -ADAM_LR * (m_hat / (jnp.sqrt(v_hat) + ADAM_EPS) + ADAM_WD * w_ref[...])
        nm_ref[...] = nm
        nv_ref[...] = nv

    blk = pl.BlockSpec((tr, c), lambda i: (i, 0))
    return pl.pallas_call(
        body, name=name, grid=(r // tr,), in_specs=[pl.BlockSpec((n_parts, tr, c), lambda i: (0, i, 0)), blk, blk, blk],
        out_specs=[blk] * 4, out_shape=[_sds((r, c), F32)] * 4, compiler_params=_params(),
    )(parts, w, m, v)


def _adamw_slab(name, parts, w, m, v, bufs, f):
    n_parts, r, c = parts.shape
    tr = max(t for t in range(8, 257, 8) if r % t == 0)

    def body(p_ref, w_ref, m_ref, v_ref, *rest):
        g_ref, d_ref, nm_ref, nv_ref = rest[4:]
        g = p_ref[0].astype(F32)
        for k in range(1, n_parts):
            g = g + p_ref[k].astype(F32)
        nm = ADAM_B1 * m_ref[...] + (1.0 - ADAM_B1) * g
        nv = ADAM_B2 * v_ref[...] + (1.0 - ADAM_B2) * (g * g)
        m_hat = nm / (1.0 - ADAM_B1 ** ADAM_STEP)
        v_hat = nv / (1.0 - ADAM_B2 ** ADAM_STEP)
        g_ref[...] = g
        d_ref[...] = -ADAM_LR * (m_hat / (jnp.sqrt(v_hat) + ADAM_EPS) + ADAM_WD * w_ref[...])
        nm_ref[...] = nm
        nv_ref[...] = nv

    blk = pl.BlockSpec((None, tr, c), lambda i: (f, i, 0))
    return pl.pallas_call(
        body, name=name, grid=(r // tr,),
        in_specs=[pl.BlockSpec((n_parts, tr, c), lambda i: (0, i, 0)), blk, blk, blk] + [pl.BlockSpec(memory_space=pl.ANY)] * 4,
        out_specs=[blk] * 4, out_shape=[_sds(w.shape, F32)] * 4, input_output_aliases={4 + j: j for j in range(4)},
        compiler_params=_params(),
    )(parts, w, m, v, *bufs)


def _mesh_pos():
    return lax.axis_index("x"), lax.axis_index("y"), lax.axis_index("c")


def _flip(pos, mask):
    return tuple(1 - p if (mask >> (2 - b)) & 1 else p for b, p in enumerate(pos))


def _index(pos):
    return 4 * pos[0] + 2 * pos[1] + pos[2]


def _all_gather(name, xs, after=None):
    n = len(xs)
    extra = [] if after is None else [after]

    def body(*refs):
        x_refs, o_refs = refs[:n], refs[n + len(extra):2 * n + len(extra)]
        send_sems, recv_sems, local_sems = refs[2 * n + len(extra):]
        me = _mesh_pos()
        sibling = _flip(me, 1)
        others = [_flip(me, 4), _flip(me, 2), _flip(me, 6)]

        def copy(k, j, block, to, src=None):
            dst = o_refs[k].at[_index(block)]
            return pltpu.make_async_remote_copy(
                src_ref=dst if src is None else src, dst_ref=dst, send_sem=send_sems.at[k, j], recv_sem=recv_sems.at[k, j],
                device_id=to, device_id_type=MESH)

        local = [pltpu.make_async_copy(x_refs[k], o_refs[k].at[_index(me)], local_sems.at[k]) for k in range(n)]
        for cp in local:
            cp.start()
        first = []
        for k in range(n):
            first.append(copy(k, 0, me, sibling, src=x_refs[k]))
            first += [copy(k, 1 + j, me, other, src=x_refs[k]) for j, other in enumerate(others)]
        for cp in first:
            cp.start()
        passed = []
        for j, other in enumerate(others):
            for k in range(n):
                copy(k, 1 + j, other, me).wait_recv()
                cp = copy(k, 4 + j, other, sibling)
                cp.start()
                passed.append(cp)
        for k in range(n):
            copy(k, 0, sibling, me).wait_recv()
        for j, other in enumerate(others):
            for k in range(n):
                copy(k, 4 + j, _flip(other, 1), me).wait_recv()
        for cp in first + passed:
            cp.wait_send()
        for cp in local:
            cp.wait()

    any_spec = pl.BlockSpec(memory_space=pl.ANY)
    return pl.pallas_call(
        body, name=name, in_specs=[any_spec] * (n + len(extra)), out_specs=[any_spec] * n,
        out_shape=[_sds((N_DEV,) + x.shape, x.dtype) for x in xs],
        scratch_shapes=[pltpu.SemaphoreType.DMA((n, 7)), pltpu.SemaphoreType.DMA((n, 7)), pltpu.SemaphoreType.DMA((n,))],
    )(*xs, *extra)


_MASKS = {
    "gather": tuple(range(1, N_DEV)),
    "scatter": tuple(range(1, N_DEV)),
    "own": (1, 4, 2, 6),
    "pass": (4, 2, 6),
}


def _split_copies(kind, outgoing, x_refs, land_refs, send_sems, recv_sems):
    me = _mesh_pos()
    masks = _MASKS[kind]
    copies = []
    for k, (x_ref, land_ref) in enumerate(zip(x_refs, land_refs)):
        for j, mask in enumerate(masks):
            if kind == "pass":
                peer = _flip(me, 1)
                src = land_ref.at[_index(_flip(me, mask))]
                dst = land_ref.at[_index(_flip(me, mask if outgoing else mask | 1))]
            else:
                peer = _flip(me, mask)
                src = x_ref.at[_index(peer)] if kind == "scatter" else x_ref
                dst = land_ref.at[_index(me if outgoing else peer)]
            sem = k * len(masks) + j
            copies.append(pltpu.make_async_remote_copy(src_ref=src, dst_ref=dst, send_sem=send_sems.at[sem], recv_sem=recv_sems.at[sem],
                                                       device_id=peer, device_id_type=MESH))
    return copies


_HBM_SPEC = pl.BlockSpec(memory_space=pltpu.HBM)
_SEM_SPEC = pl.BlockSpec(memory_space=pltpu.SEMAPHORE)
_EFFECT = pltpu.SideEffectType.DATAFLOW_SIDE_EFFECTING


def _split_start(name, kind, xs, after=None):
    n = len(xs)
    n_sem = n * len(_MASKS[kind])
    extra = [] if after is None else [after]
    lands = [lax.empty(x.shape if kind == "scatter" else (N_DEV,) + x.shape, x.dtype) for x in xs]

    def body(*refs):
        x_refs, land_refs = refs[:n], refs[n:2 * n]
        send_sems, recv_sems = refs[2 * n + len(extra)], refs[2 * n + len(extra) + 1]
        token = refs[-1]
        for cp in _split_copies(kind, True, x_refs, land_refs, send_sems, recv_sems):
            cp.start()
        token[...] = jnp.zeros_like(token)

    hbm = [pltpu.HBM(a.shape, a.dtype) for a in list(xs) + lands]
    res = pl.pallas_call(
        body, name=name,
        out_shape=[pltpu.SemaphoreType.DMA((n_sem,)), pltpu.SemaphoreType.DMA((n_sem,))] + hbm + [_sds((8, 128), F32)],
        in_specs=[_HBM_SPEC] * (2 * n) + [pl.BlockSpec(memory_space=pl.ANY)] * len(extra),
        out_specs=[_SEM_SPEC, _SEM_SPEC] + [_HBM_SPEC] * (2 * n) + [pl.BlockSpec(memory_space=pltpu.VMEM)],
        input_output_aliases={j: 2 + j for j in range(2 * n)}, compiler_params=pltpu.CompilerParams(has_side_effects=_EFFECT),
    )(*[pltpu.with_memory_space_constraint(a, pltpu.HBM) for a in list(xs) + lands], *extra)
    return (kind, n, res[0], res[1], res[2:2 + 2 * n]), res[-1]


def _split_pass(name, state, after):
    kind, n, send_sems_in, recv_sems_in, thru = state
    n_sem = n * len(_MASKS["pass"])

    def body(*refs):
        x_refs, land_refs = refs[:n], refs[n:2 * n]
        send_sems, recv_sems = refs[2 * n], refs[2 * n + 1]
        next_send, next_recv, token = refs[-3:]
        for cp in _split_copies(kind, True, x_refs, land_refs, send_sems, recv_sems):
            cp.wait_send()
        for cp in _split_copies(kind, False, x_refs, land_refs, send_sems, recv_sems):
            cp.wait_recv()
        for cp in _split_copies("pass", True, land_refs, land_refs, next_send, next_recv):
            cp.start()
        token[...] = jnp.zeros_like(token)

    res = pl.pallas_call(
        body, name=name,
        out_shape=[pltpu.HBM(a.shape, a.dtype) for a in thru] + [pltpu.SemaphoreType.DMA((n_sem,)), pltpu.SemaphoreType.DMA((n_sem,)),
                                                                 _sds((8, 128), F32)],
        in_specs=[_HBM_SPEC] * (2 * n) + [_SEM_SPEC, _SEM_SPEC, pl.BlockSpec(memory_space=pl.ANY)],
        out_specs=[_HBM_SPEC] * (2 * n) + [_SEM_SPEC, _SEM_SPEC, pl.BlockSpec(memory_space=pltpu.VMEM)],
        input_output_aliases={j: j for j in range(2 * n)}, compiler_params=pltpu.CompilerParams(has_side_effects=_EFFECT),
    )(*thru, send_sems_in, recv_sems_in, after)
    return ("pass", n, res[2 * n], res[2 * n + 1], res[:2 * n]), res[-1]


def _split_wait(name, state, after):
    kind, n, send_sems_in, recv_sems_in, thru = state

    def body(*refs):
        x_refs, land_refs = refs[:n], refs[n:2 * n]
        send_sems, recv_sems = refs[2 * n], refs[2 * n + 1]
        for cp in _split_copies(kind, True, x_refs, land_refs, send_sems, recv_sems):
            cp.wait_send()
        for cp in _split_copies(kind, False, x_refs, land_refs, send_sems, recv_sems):
            cp.wait_recv()

    res = pl.pallas_call(
        body, name=name, out_shape=[pltpu.HBM(a.shape, a.dtype) for a in thru],
        in_specs=[_HBM_SPEC] * (2 * n) + [_SEM_SPEC, _SEM_SPEC, pl.BlockSpec(memory_space=pl.ANY)], out_specs=[_HBM_SPEC] * (2 * n),
        input_output_aliases={j: j for j in range(2 * n)}, compiler_params=pltpu.CompilerParams(has_side_effects=_EFFECT),
    )(*thru, send_sems_in, recv_sems_in, after)
    me = _index(_mesh_pos())
    out = []
    for x, land in zip(res[:n], res[n:]):
        own = lax.dynamic_index_in_dim(x, me, 0, keepdims=False) if kind == "scatter" else x
        out.append(lax.dynamic_update_slice(land, own[None], (me,) + (0,) * own.ndim))
    return out


def _rope_tables(s):
    inv = 1.0 / (ROPE_THETA ** (jnp.arange(0, QK_ROPE, 2, dtype=F32) / QK_ROPE))
    ang = jnp.arange(s, dtype=F32)[:, None] * inv[None, :]
    pad = jnp.zeros((s, KV_LORA - QK_ROPE), F32)
    return (jnp.concatenate([jnp.cos(ang), jnp.cos(ang), pad], axis=1), jnp.concatenate([jnp.sin(ang), jnp.sin(ang), pad], axis=1))


def kernel(x, c, ada_w, ada_b, norm_g, ffn_w_in, ffn_w_out, pool_w, pool_b, pool_scale, mla_w_in, mla_q_norm, mla_kv_norm, mla_w_uq, mla_w_uk, mla_w_uv, mla_w_o, loss_target, m_ada_w, m_ada_b, m_norm_g, m_ffn_w_in, m_ffn_w_out, m_pool_w, m_pool_b, m_pool_scale, m_mla_w_in, m_mla_q_norm, m_mla_kv_norm, m_mla_w_uq, m_mla_w_uk, m_mla_w_uv, m_mla_w_o, v_ada_w, v_ada_b, v_norm_g, v_ffn_w_in, v_ffn_w_out, v_pool_w, v_pool_b, v_pool_scale, v_mla_w_in, v_mla_q_norm, v_mla_kv_norm, v_mla_w_uq, v_mla_w_uk, v_mla_w_uv, v_mla_w_o):
    s, d = x.shape[1], x.shape[2]
    tm = min(512, s)
    tr = min(512, s)
    tf = min(1024, s)
    tw = min(2048, s)
    me = 4 * lax.axis_index("x") + 2 * lax.axis_index("y") + lax.axis_index("c")
    x0 = x.reshape(s, d)
    target = loss_target.reshape(s, d)
    n_mod = ada_w.shape[2] * N_DEV // d
    mod_blk = ada_w.shape[2]

    small = jnp.concatenate([c.reshape(-1), norm_g.reshape(-1), pool_b.reshape(-1), mla_q_norm.reshape(-1)]).reshape(1, -1)
    w_in_t, m_in_t, v_in_t = (jnp.swapaxes(a, 2, 3) for a in (ffn_w_in, m_ffn_w_in, v_ffn_w_in))
    w_in_loc = [w_in_t[i, f].astype(BF16) for i in range(2) for f in range(2)]
    w_out_loc = [ffn_w_out[i, f].astype(BF16) for i in range(2) for f in range(2)]
    (small_all,) = _all_gather("gather_small", [small])
    small_all = small_all.reshape(N_DEV, -1)
    c_all = small_all[:, :d]
    off = d
    g_all = small_all[:, off:off + 12 * (d // N_DEV)].reshape(N_DEV, 2, 6, d // N_DEV).transpose(1, 2, 0, 3).reshape(2, 6, d)
    off += 12 * (d // N_DEV)
    pool_b_all = small_all[:, off:off + 4 * 32].reshape(N_DEV, 4, 32).transpose(1, 0, 2).reshape(1, d)
    off += 4 * 32
    q_norm_all = small_all[:, off:off + 32].reshape(1, Q_LORA)
    kv_norm_row = mla_kv_norm.reshape(1, KV_LORA)
    pscale_row = pool_scale.reshape(1, d)

    even = (jnp.arange(N_HEADS) % 2 == 0)[:, None, None]
    cos_k, sin_k = _rope_tables(s)

    def mla_weights(mla_w_in_all, mla_w_uq_all, mla_w_o_all):
        uq = mla_w_uq_all.reshape(Q_LORA, N_HEADS, QK_NOPE + QK_ROPE).transpose(1, 0, 2)
        zq = jnp.zeros((N_HEADS, Q_LORA, QK_NOPE), BF16)
        wq = jnp.concatenate(
            [uq[:, :, :QK_NOPE], zq, uq[:, :, QK_NOPE:], jnp.zeros((N_HEADS, Q_LORA, QK_PAD - KV_LORA - QK_ROPE), BF16)], axis=2)
        wukp = jnp.pad(mla_w_uk[0].transpose(1, 2, 0).astype(BF16), ((0, 0), (0, QK_PAD - QK_NOPE), (0, QK_PAD - KV_LORA)))
        uv = mla_w_uv[0].transpose(1, 0, 2).astype(BF16)
        wuv2 = jnp.where(even, jnp.concatenate([uv, jnp.zeros_like(uv)], axis=2), jnp.concatenate([jnp.zeros_like(uv), uv], axis=2))
        return dict(w_in=jnp.pad(mla_w_in_all.reshape(d, -1), ((0, 0), (0, LAT_PAD - mla_w_in.shape[2]))), wq=wq, wukp=wukp,
                    wuv2=wuv2, w_o=mla_w_o_all.reshape(d, d), q_norm=q_norm_all, kv_norm=kv_norm_row)

    (sc_all,), _ = _rowmap("ada_silu", lambda cv: ((cv * jax.nn.sigmoid(cv),), ()), [(c_all, (N_DEV, d), lambda i: (0, 0))],
                           [(_sds((N_DEV, d), F32), (N_DEV, d), lambda i: (0, 0))], [], (1,))
    ada_b_loc = lax.dynamic_slice_in_dim(ada_b, me * mod_blk, mod_blk, axis=1).reshape(2, 1, mod_blk)
    m_pad = 2 * N_DEV
    modp = _matmul("ada_mod", jnp.pad(sc_all, ((0, m_pad - N_DEV), (0, 0))), ada_w, a_blk=(m_pad, d), a_map=lambda i, k: (0, 0),
                   b_blk=(None, d, mod_blk), b_map=lambda i, k: (i, 0, 0), o_shape=(2, m_pad, mod_blk), o_blk=(None, m_pad, mod_blk),
                   o_map=lambda i, k: (i, 0, 0), grid=(2, 1), contract=NN, out_dtype=F32, bias=ada_b_loc, bias_blk=(None, 1, mod_blk),
                   bias_map=lambda i, k: (i, 0, 0))[:, :N_DEV]
    (modp_all,) = _all_gather("gather_mod", [modp.reshape(2 * N_DEV, mod_blk)])
    groups = [[w_in_loc[0], w_out_loc[0], pool_w.reshape(-1, POOL_GROUP).astype(BF16)], [w_in_loc[1], w_out_loc[1]],
              [w_in_loc[2], w_out_loc[2], mla_w_in[0].astype(BF16), mla_w_uq.reshape(mla_w_uq.shape[1], -1).astype(BF16),
               mla_w_o[0].astype(BF16)], [w_in_loc[3], w_out_loc[3]]]
    states, token = [], modp_all
    for name, group in zip("0abc", groups):
        state, token = _split_start(f"gather_start_{name}", "own", group, after=token)
        states.append(state)
    w_in8 = [None] * 4
    w_out4 = [None] * 4
    mod = lax.dynamic_index_in_dim(modp_all.reshape(N_DEV, 2, N_DEV, mod_blk), me, axis=2, keepdims=False)
    mod = mod.transpose(1, 0, 2).reshape(2, n_mod, d) + token[0, 0]
    mod_rows = mod.reshape(2 * n_mod, 1, d)
    g_rows = g_all.reshape(12, 1, d)
    weights_of = (0.5, 1.0, 0.5, 0.5, 1.0, 0.5)

    def pre_rows(k, with_shift):
        rows = [_rowk(g_rows, 2 * k), _rowk(mod_rows, 3 * k + 1)]
        return rows + [_rowk(mod_rows, 3 * k)] if with_shift else rows

    def post_rows(k):
        return [_rowk(g_rows, 2 * k + 1), _rowk(mod_rows, 3 * k + 2)]

    def act_dtype(k):
        return F32 if k == 1 else BF16

    saved = []
    xs = x0
    mla_wts = None
    h = _prenorm("prenorm_0", xs, pre_rows(0, True), act_dtype(0), tr)
    state, token = _split_pass("gather_pass_0", states[0], h)
    lands = _split_wait("gather_wait_0", state, token)
    w_in8[0], w_out4[0] = lands[0], lands[1].reshape(4, FF_BLK, d)
    w4 = lands[2].reshape(N_DEV, 4, 32, POOL_GROUP).transpose(1, 0, 2, 3).reshape(4, POOL_GROUP, POOL_GROUP)
    token = None
    for k in range(6):
        i, sub = divmod(k, 3)
        tag = f"{i}{sub}"
        if k == 1:
            states[1], token = _split_pass("gather_pass_a", states[1], xs)
        if k == 2:
            lands = _split_wait("gather_wait_a", states[1], xs)
            w_in8[1], w_out4[1] = lands[0], lands[1].reshape(4, FF_BLK, d)
            states[2], token = _split_pass("gather_pass_b", states[2], lands[0])
        if k == 3:
            lands = _split_wait("gather_wait_b", states[2], xs)
            w_in8[2], w_out4[2] = lands[0], lands[1].reshape(4, FF_BLK, d)
            mla_wts = mla_weights(*lands[2:])
            states[3], token = _split_pass("gather_pass_c", states[3], lands[0])
        if k == 5:
            lands = _split_wait("gather_wait_c", states[3], xs)
            w_in8[3], w_out4[3] = lands[0], lands[1].reshape(4, FF_BLK, d)
        if sub != 1:
            u, extra = _ffn_fwd(tag, h, w_in8[2 * i + sub // 2], w_out4[2 * i + sub // 2], tf, tf // 2, after=token)
            token = None
        elif i == 0:
            u, z, v = _pool_fwd(h, w4, pool_b_all + token[0, 0], pscale_row, min(4 * tm, s))
            extra = (z, v)
            token = None
        else:
            u, extra = _mla_fwd(h, mla_wts, cos_k, sin_k, tm)
        saved.append((xs, h, u, extra))
        if k < 5:
            xs, h = _norm_link(f"norm_link_{k + 1}", xs, u, post_rows(k), weights_of[k], pre_rows(k + 1, True), act_dtype(k + 1), tr)

    dx, du, (sq, dgate, dgpost) = _norm_loss("norm_loss", xs, u, target, post_rows(5), weights_of[5], tr)
    loss_part = (0.5 * jnp.sum(sq) / d).reshape(1, 1)

    d_mod = [None] * (2 * n_mod)
    d_g = [None] * 12
    d_mod[3 * 5 + 2], d_g[2 * 5 + 1] = dgate, dgpost
    sent = {}
    pool_grads = mla_grads = None

    def start_scatter(key, arrays):
        state, token = _split_start(f"scatter_start_{key}", "scatter", arrays)
        sent[key] = state
        return token

    for k in (5, 4, 3, 2, 1, 0):
        i, sub = divmod(k, 3)
        tag = f"{i}{sub}"
        xin, h, u, extra = saved[k]
        token = None
        if sub != 1:
            f = 2 * i + sub // 2
            dh, dgu, act = _ffn_bwd_act(tag, du, extra, w_in8[f], w_out4[f], tf, tf // 4)
            dw_out = _ffn_dw(f"ffn_dwout_{tag}", act, du, tw).reshape(N_DEV, FF_BLK // 2, d)
            if k == 0:
                d_pool_w = pool_grads[0].reshape(4, N_DEV, 32, POOL_GROUP).transpose(1, 0, 2, 3).reshape(N_DEV, 4 * 32, POOL_GROUP)
                token = start_scatter(tag + "_out", [dw_out, d_pool_w])
                token = start_scatter(tag + "_in", [_ffn_dw(f"ffn_dwin_{tag}", dgu, h, tw, after=token)])
            else:
                token = start_scatter(tag, [_ffn_dw(f"ffn_dwin_{tag}", dgu, h, tw), dw_out])
        elif i == 0:
            dh, dw4, (dpscale, dpb) = _pool_bwd(du, extra[0], extra[1], w4, pscale_row, min(4 * tm, s))
            pool_grads = (dw4, dpscale, dpb)
        else:
            dh, mla_grads = _mla_bwd(du, h, extra, mla_wts, cos_k, sin_k, tm)
            dwq = mla_grads["wq"]
            d_uq = jnp.concatenate([dwq[:, :, :QK_NOPE], dwq[:, :, KV_LORA:KV_LORA + QK_ROPE]], axis=2).transpose(1, 0, 2)
            token = start_scatter("mla", [mla_grads["w_in"][:, :mla_w_in.shape[2]].reshape(N_DEV, d // N_DEV, -1),
                                          d_uq.reshape(N_DEV, Q_LORA // N_DEV, -1), mla_grads["w_o"].reshape(N_DEV, d // N_DEV, d)])
            dwukp, dwuv2 = mla_grads["wukp"], mla_grads["wuv2"]
            d_uk = dwukp[:, :QK_NOPE, :KV_LORA].transpose(2, 0, 1).reshape(KV_LORA, -1)
            d_uv = jnp.where(even, dwuv2[:, :, :V_HEAD], dwuv2[:, :, V_HEAD:]).transpose(1, 0, 2).reshape(KV_LORA, -1)
            state_ukv, token = _split_start("gather_start_ukv", "gather", [d_uk, d_uv], after=token)
        if k > 0:
            dx, du, reds = _norm_link_bwd(f"norm_link_bwd_{k}", dh, xin, dx, saved[k - 1][2], pre_rows(k, False), post_rows(k - 1),
                                          weights_of[k - 1], act_dtype(k - 1), tr, after=token)
            d_mod[3 * (k - 1) + 2], d_g[2 * (k - 1) + 1] = reds[3], reds[4]
        else:
            dx, reds = _prenorm_bwd("prenorm_bwd_0", dh, xin, dx, pre_rows(0, False), tr, after=token)
        d_mod[3 * k], d_mod[3 * k + 1], d_g[2 * k] = reds[0], reds[1], reds[2]
    grad_x = dx.reshape(x.shape)

    def upd(name, parts, w, m, v):
        shape = w.shape
        r, cdim = parts.shape[1], parts.shape[2]
        return [o.reshape(shape) for o in _adamw(name, parts, w.reshape(r, cdim), m.reshape(r, cdim), v.reshape(r, cdim))]

    def landed(key, after):
        return _split_wait(f"scatter_wait_{key}", sent[key], after)

    res = {}
    w_in_s, m_in_s, v_in_s = (a.reshape(4, FF_BLK, d) for a in (w_in_t, m_in_t, v_in_t))
    w_out_s, m_out_s, v_out_s = (a.reshape(4, FF_BLK // 2, d) for a in (ffn_w_out, m_ffn_w_out, v_ffn_w_out))
    bufs_in = [lax.empty(w_in_s.shape, F32) for _ in range(4)]
    bufs_out = [lax.empty(w_out_s.shape, F32) for _ in range(4)]
    for key, k in (("12", 3), ("mla", None), ("10", 2), ("02", 1)):
        parts = landed(key, grad_x)
        if k is None:
            res["mla_w_in"] = upd("adam_mla_w_in", parts[0], mla_w_in, m_mla_w_in, v_mla_w_in)
            res["mla_w_uq"] = upd("adam_mla_w_uq", parts[1], mla_w_uq, m_mla_w_uq, v_mla_w_uq)
            res["mla_w_o"] = upd("adam_mla_w_o", parts[2], mla_w_o, m_mla_w_o, v_mla_w_o)
            uk_all, uv_all = _split_wait("gather_wait_ukv", state_ukv, grad_x)
            res["mla_w_uk"] = upd("adam_mla_w_uk", uk_all, mla_w_uk, m_mla_w_uk, v_mla_w_uk)
            res["mla_w_uv"] = upd("adam_mla_w_uv", uv_all, mla_w_uv, m_mla_w_uv, v_mla_w_uv)
            continue
        bufs_in = _adamw_slab(f"adam_ffn_w_in_{key}", parts[0], w_in_s, m_in_s, v_in_s, bufs_in, k)
        bufs_out = _adamw_slab(f"adam_ffn_w_out_{key}", parts[1], w_out_s, m_out_s, v_out_s, bufs_out, k)

    dw4, dpscale, dpb = pool_grads
    small_g = jnp.concatenate(d_mod + d_g + [dpb, dpscale, mla_grads["q_norm"], mla_grads["kv_norm"], loss_part], axis=1)
    done = [bufs_in[0], bufs_out[0], res["mla_w_o"][0], res["mla_w_uv"][0]]
    (small_g_all,) = _all_gather("gather_small_grads", [small_g], after=sum(a.reshape(-1)[:1] for a in done))
    small_g_all = small_g_all.reshape(N_DEV, -1)
    n_m = 2 * n_mod * d
    d_mod_all = small_g_all[:, :n_m].reshape(N_DEV, 2, n_mod * d)
    rest = small_g_all[:, n_m:]
    p_norm_g = lax.dynamic_slice_in_dim(rest[:, :12 * d].reshape(N_DEV, 12, d), me * (d // N_DEV), d // N_DEV, axis=2)
    p_pool_b = lax.dynamic_slice_in_dim(rest[:, 12 * d:13 * d].reshape(N_DEV, 4, POOL_GROUP), me * 32, 32, axis=2)
    p_pool_scale = rest[:, 13 * d:14 * d].reshape(N_DEV, 1, d)
    p_q_norm = lax.dynamic_slice_in_dim(rest[:, 14 * d:14 * d + Q_LORA], me * 32, 32, axis=1).reshape(N_DEV, 1, 32)
    p_kv_norm = rest[:, 14 * d + Q_LORA:14 * d + Q_LORA + KV_LORA].reshape(N_DEV, 1, KV_LORA)
    loss = jnp.sum(rest[:, -1])

    d_mod_loc = lax.dynamic_slice_in_dim(d_mod_all, me * mod_blk, mod_blk, axis=2).transpose(1, 0, 2)
    k_pad = 128
    sc_t = jnp.pad(sc_all.T, ((0, 0), (0, k_pad - N_DEV)))
    d_ada_w = _matmul("ada_dw", sc_t, jnp.pad(d_mod_loc, ((0, 0), (0, k_pad - N_DEV), (0, 0))), a_blk=(d, k_pad),
                      a_map=lambda i, k: (0, 0), b_blk=(None, k_pad, mod_blk), b_map=lambda i, k: (i, 0, 0), o_shape=(2, d, mod_blk),
                      o_blk=(None, d, mod_blk), o_map=lambda i, k: (i, 0, 0), grid=(2, 1), contract=NN, out_dtype=F32)
    res["ada_w"] = upd("adam_ada_w", d_ada_w.reshape(1, 2 * d, mod_blk), ada_w, m_ada_w, v_ada_w)
    res["ada_b"] = upd("adam_ada_b", d_mod_all.reshape(N_DEV, 2, n_mod * d), ada_b, m_ada_b, v_ada_b)
    res["norm_g"] = upd("adam_norm_g", p_norm_g, norm_g, m_norm_g, v_norm_g)
    res["pool_b"] = upd("adam_pool_b", p_pool_b, pool_b, m_pool_b, v_pool_b)
    res["pool_scale"] = upd("adam_pool_scale", p_pool_scale, pool_scale, m_pool_scale, v_pool_scale)
    res["mla_q_norm"] = upd("adam_mla_q_norm", p_q_norm, mla_q_norm, m_mla_q_norm, v_mla_q_norm)
    res["mla_kv_norm"] = upd("adam_mla_kv_norm", p_kv_norm, mla_kv_norm, m_mla_kv_norm, v_mla_kv_norm)

    p_out, p_pool_w = landed("00_out", res["ada_w"][1])
    bufs_out = _adamw_slab("adam_ffn_w_out_00", p_out, w_out_s, m_out_s, v_out_s, bufs_out, 0)
    res["pool_w"] = upd("adam_pool_w", p_pool_w, pool_w, m_pool_w, v_pool_w)
    (p_in,) = landed("00_in", res["pool_w"][1])
    bufs_in = _adamw_slab("adam_ffn_w_in_00", p_in, w_in_s, m_in_s, v_in_s, bufs_in, 0)
    res["ffn_w_in"] = [jnp.swapaxes(b.reshape(w_in_t.shape), 2, 3) for b in bufs_in]
    res["ffn_w_out"] = [b.reshape(ffn_w_out.shape) for b in bufs_out]

    order = ["ada_w", "ada_b", "norm_g", "ffn_w_in", "ffn_w_out", "pool_w", "pool_b", "pool_scale", "mla_w_in", "mla_q_norm",
             "mla_kv_norm", "mla_w_uq", "mla_w_uk", "mla_w_uv", "mla_w_o"]
    outs = [loss, grad_x]
    for j in range(4):
        outs += [res[name][j] for name in order]
    return tuple(outs)
```

```python
import jax
import jax.numpy as jnp
from jax import lax
from jax.experimental import pallas as pl
from jax.experimental.pallas import tpu as pltpu

F32 = jnp.float32
BF16 = jnp.bfloat16
N_DEV = 8
MESH = pl.DeviceIdType.MESH

D_MODEL = 1024
N_HEADS = 16
QK_NOPE = 64
QK_ROPE = 32
V_HEAD = 64
Q_LORA = 256
KV_LORA = 128
LAT_PAD = 512
QK_PAD = 256
ONE_COL = 160
D_FF = 2816
FF_BLK = 2 * D_FF // N_DEV
POOL_WINDOWS = (2, 4, 8, 16)
POOL_GROUP = 256
POOL_SHARD = POOL_GROUP // N_DEV
Q_SHARD = Q_LORA // N_DEV
ROPE_THETA = 10000.0
EPS = 1e-6
ATTN_SCALE = (QK_NOPE + QK_ROPE) ** -0.5
LOG2_E = 1.4426950408889634
ADAM_LR, ADAM_B1, ADAM_B2, ADAM_EPS, ADAM_WD, ADAM_STEP = 0.001, 0.9, 0.999, 1e-08, 0.01, 10
VMEM_LIMIT = 56 * 1024 * 1024

NN = ((1,), (0,))
NT = ((1,), (1,))
TN = ((0,), (0,))


def _params(**kw):
    return pltpu.CompilerParams(vmem_limit_bytes=VMEM_LIMIT, **kw)


def _dot(a, b, contract):
    return lax.dot_general(a, b, (contract, ((), ())), preferred_element_type=F32)


def _matmul(name, a, b, *, a_blk, a_map, b_blk, b_map, o_shape, o_blk, o_map, grid, contract, out_dtype,
            bias=None, bias_blk=None, bias_map=None, after=None):
    n_k = grid[-1]
    k_axis = len(grid) - 1
    acc_shape = tuple(d for d in o_blk if d is not None)

    def body(*refs):
        a_ref, b_ref = refs[:2]
        bias_ref = refs[2] if bias is not None else None
        if n_k == 1:
            r = _dot(a_ref[...].astype(BF16), b_ref[...].astype(BF16), contract)
            if bias is not None:
                r = r + bias_ref[...]
            refs[-1][...] = r.astype(refs[-1].dtype)
            return
        o_ref, acc = refs[-2:]
        k = pl.program_id(k_axis)

        @pl.when(k == 0)
        def _():
            acc[...] = jnp.zeros_like(acc)

        acc[...] += _dot(a_ref[...].astype(BF16), b_ref[...].astype(BF16), contract)

        @pl.when(k == n_k - 1)
        def _():
            r = acc[...]
            if bias is not None:
                r = r + bias_ref[...]
            o_ref[...] = r.astype(o_ref.dtype)

    in_specs = [pl.BlockSpec(a_blk, a_map), pl.BlockSpec(b_blk, b_map)]
    args = [a, b]
    if bias is not None:
        in_specs.append(pl.BlockSpec(bias_blk, bias_map))
        args.append(bias)
    if after is not None:
        in_specs.append(pl.BlockSpec(memory_space=pl.ANY))
        args.append(after)
    return pl.pallas_call(
        body, name=name, grid=grid, in_specs=in_specs, out_specs=pl.BlockSpec(o_blk, o_map),
        out_shape=jax.ShapeDtypeStruct(o_shape, out_dtype), scratch_shapes=[pltpu.VMEM(acc_shape, F32)] if n_k > 1 else [],
        compiler_params=_params(),
    )(*args)


def _rowmap(name, fn, ins, outs, reds, grid, after=None):
    n_in, n_out, n_red = len(ins), len(outs), len(reds)
    extra = [] if after is None else [after]

    def body(*refs):
        in_refs = refs[:n_in]
        out_refs = refs[n_in + len(extra):n_in + len(extra) + n_out]
        red_refs = refs[n_in + len(extra) + n_out:]
        out_vals, red_vals = fn(*[r[...] for r in in_refs])
        for r, v in zip(out_refs, out_vals):
            r[...] = v.astype(r.dtype)
        if n_red:
            first = pl.program_id(0) == 0
            for ax in range(1, len(grid)):
                first = jnp.logical_and(first, pl.program_id(ax) == 0)

            @pl.when(first)
            def _():
                for r in red_refs:
                    r[...] = jnp.zeros_like(r)

            for r, v in zip(red_refs, red_vals):
                r[...] += v

    res = pl.pallas_call(
        body, name=name, grid=grid,
        in_specs=[pl.BlockSpec(blk, imap) for _, blk, imap in ins] + [pl.BlockSpec(memory_space=pl.ANY)] * len(extra),
        out_specs=[pl.BlockSpec(blk, imap) for _, blk, imap in list(outs) + list(reds)],
        out_shape=[sds for sds, _, _ in list(outs) + list(reds)],
        compiler_params=_params(),
    )(*[a for a, _, _ in ins], *extra)
    return res[:n_out], res[n_out:]


def _sds(shape, dtype):
    return jax.ShapeDtypeStruct(shape, dtype)


def _tile(a, tm):
    return (a, (tm, a.shape[1]), lambda i: (i, 0))


def _row(a):
    return (a, (1, a.shape[1]), lambda i: (0, 0))


def _otile(n, c, dtype, tm):
    return (_sds((n, c), dtype), (tm, c), lambda i: (i, 0))


def _ored(c):
    return (_sds((1, c), F32), (1, c), lambda i: (0, 0))


def _colsum(v):
    return jnp.sum(v, axis=0, keepdims=True)


def _rstd(v):
    return lax.rsqrt(jnp.mean(v * v, axis=-1, keepdims=True) + EPS)


def _pre(xv, g, sc, sh):
    return xv * _rstd(xv) * g * (1.0 + sc) + sh


def _post(xv, uv, g, gt, weight):
    return xv + weight * (1.0 + gt) * (uv * _rstd(uv) * g)


def _post_bwd(dv, uv, g, gt, weight):
    r = _rstd(uv)
    un = uv * r
    dy = dv * (weight * (1.0 + gt))
    a = dy * g
    du = r * (a - un * jnp.mean(a * un, axis=-1, keepdims=True))
    return du, (_colsum(dv * (weight * (un * g))), _colsum(dy * un))


def _pre_bwd(dhv, xv, dv, g, sc):
    dhv = dhv.astype(F32)
    r = _rstd(xv)
    xn = xv * r
    b = dhv * (g * (1.0 + sc))
    dx = dv + r * (b - xn * jnp.mean(b * xn, axis=-1, keepdims=True))
    return dx, (_colsum(dhv), _colsum(dhv * (xn * g)), _colsum(dhv * ((1.0 + sc) * xn)))


def _rowk(rows, k):
    return (rows, (None, 1, rows.shape[2]), lambda i: (k, 0, 0))


def _prenorm(name, x, pre, out_dtype, tm):
    n, d = x.shape
    (h,), _ = _rowmap(name, lambda xv, g, sc, sh: ((_pre(xv, g, sc, sh),), ()), [_tile(x, tm), *pre], [_otile(n, d, out_dtype, tm)],
                      [], (n // tm,))
    return h


def _norm_link(name, x, u, post, weight, pre, out_dtype, tm):
    n, d = x.shape

    def fn(xv, uv, g, gt, g2, sc, sh):
        xn = _post(xv, uv, g, gt, weight)
        return (xn, _pre(xn, g2, sc, sh)), ()

    (xn, h), _ = _rowmap(name, fn, [_tile(x, tm), _tile(u, tm), *post, *pre], [_otile(n, d, F32, tm), _otile(n, d, out_dtype, tm)],
                         [], (n // tm,))
    return xn, h


def _norm_loss(name, x, u, target, post, weight, tm):
    n, d = x.shape

    def fn(xv, uv, tv, g, gt):
        e = _post(xv, uv, g, gt, weight) - tv
        dv = e * (1.0 / d)
        du, reds = _post_bwd(dv, uv, g, gt, weight)
        return (dv, du), (_colsum(e * e), *reds)

    (dx, du), reds = _rowmap(name, fn, [_tile(x, tm), _tile(u, tm), _tile(target, tm), *post],
                             [_otile(n, d, F32, tm), _otile(n, d, BF16, tm)], [_ored(d)] * 3, (n // tm,))
    return dx, du, reds


def _norm_link_bwd(name, dh, x, dout, u_prev, pre, post_prev, weight_prev, out_dtype, tm, after=None):
    n, d = x.shape

    def fn(dhv, xv, dv, uv, g, sc, g2, gt):
        dx, reds = _pre_bwd(dhv, xv, dv, g, sc)
        du, reds_prev = _post_bwd(dx, uv, g2, gt, weight_prev)
        return (dx, du), (*reds, *reds_prev)

    (dx, du), reds = _rowmap(name, fn, [_tile(dh, tm), _tile(x, tm), _tile(dout, tm), _tile(u_prev, tm), *pre, *post_prev],
                             [_otile(n, d, F32, tm), _otile(n, d, out_dtype, tm)], [_ored(d)] * 5, (n // tm,), after=after)
    return dx, du, reds


def _prenorm_bwd(name, dh, x, dout, pre, tm, after=None):
    n, d = x.shape

    def fn(dhv, xv, dv, g, sc):
        dx, reds = _pre_bwd(dhv, xv, dv, g, sc)
        return (dx,), reds

    (dx,), reds = _rowmap(name, fn, [_tile(dh, tm), _tile(x, tm), _tile(dout, tm), *pre], [_otile(n, d, F32, tm)], [_ored(d)] * 3,
                          (n // tm,), after=after)
    return dx, reds


def _ffn_fwd(tag, h, w_in8, w_out4, tm, sub, after=None):
    s, d = h.shape

    extra = [] if after is None else [after]

    def body(h_ref, wg_ref, wu_ref, wo_ref, *rest):
        u_ref, gu_ref = rest[-2:]

        @pl.when(pl.program_id(1) == 0)
        def _():
            u_ref[...] = jnp.zeros_like(u_ref)

        for r in range(tm // sub):
            rows = pl.ds(r * sub, sub)
            hv = h_ref[rows, :]
            gate = _dot(hv, wg_ref[...], NT)
            up = _dot(hv, wu_ref[...], NT)
            gu_ref[0, rows, :] = gate.astype(BF16)
            gu_ref[1, rows, :] = up.astype(BF16)
            u_ref[rows, :] += _dot((gate * jax.nn.sigmoid(gate) * up).astype(BF16), wo_ref[...], NN)

    w_blk = (None, FF_BLK, d)
    return pl.pallas_call(
        body, name=f"ffn_fwd_{tag}", grid=(s // tm, 4),
        in_specs=[pl.BlockSpec((tm, d), lambda i, j: (i, 0)), pl.BlockSpec(w_blk, lambda i, j: (j, 0, 0)),
                  pl.BlockSpec(w_blk, lambda i, j: (j + 4, 0, 0)), pl.BlockSpec((None, FF_BLK, d), lambda i, j: (j, 0, 0))]
        + [pl.BlockSpec(memory_space=pl.ANY)] * len(extra),
        out_specs=[pl.BlockSpec((tm, d), lambda i, j: (i, 0)), pl.BlockSpec((2, None, tm, FF_BLK), lambda i, j: (0, j, i, 0))],
        out_shape=[_sds((s, d), F32), _sds((2, 4, s, FF_BLK), BF16)], compiler_params=_params(),
    )(h, w_in8, w_in8, w_out4, *extra)


def _ffn_bwd_act(tag, du, gu, w_in8, w_out4, tm, sub):
    s, d = du.shape

    def body(du_ref, gu_ref, wg_ref, wu_ref, wo_ref, dh_ref, dgu_ref, act_ref):
        @pl.when(pl.program_id(1) == 0)
        def _():
            dh_ref[...] = jnp.zeros_like(dh_ref)

        for r in range(tm // sub):
            rows = pl.ds(r * sub, sub)
            dact = _dot(du_ref[rows, :], wo_ref[...], NT)
            gate, up = gu_ref[0, rows, :].astype(F32), gu_ref[1, rows, :].astype(F32)
            sg = jax.nn.sigmoid(gate)
            silu = gate * sg
            dg = dact * up * (sg * (1.0 + gate * (1.0 - sg)))
            dup = dact * silu
            dgu_ref[0, :, rows] = dg.T.astype(BF16)
            dgu_ref[1, :, rows] = dup.T.astype(BF16)
            act_ref[:, rows] = (silu * up).T.astype(BF16)
            dh_ref[rows, :] += _dot(dg.astype(BF16), wg_ref[...], NN) + _dot(dup.astype(BF16), wu_ref[...], NN)

    w_blk = (None, FF_BLK, d)
    dh, dgu_t, act_t = pl.pallas_call(
        body, name=f"ffn_bwd_{tag}", grid=(s // tm, 4),
        in_specs=[pl.BlockSpec((tm, d), lambda i, j: (i, 0)), pl.BlockSpec((2, None, tm, FF_BLK), lambda i, j: (0, j, i, 0)),
                  pl.BlockSpec(w_blk, lambda i, j: (j, 0, 0)), pl.BlockSpec(w_blk, lambda i, j: (j + 4, 0, 0)),
                  pl.BlockSpec((None, FF_BLK, d), lambda i, j: (j, 0, 0))],
        out_specs=[pl.BlockSpec((tm, d), lambda i, j: (i, 0)), pl.BlockSpec((2, None, FF_BLK, tm), lambda i, j: (0, j, 0, i)),
                   pl.BlockSpec((None, FF_BLK, tm), lambda i, j: (j, 0, i))],
        out_shape=[_sds((s, d), F32), _sds((2, 4, FF_BLK, s), BF16), _sds((4, FF_BLK, s), BF16)], compiler_params=_params(),
    )(du, gu, w_in8, w_in8, w_out4)
    return dh, dgu_t.reshape(8, FF_BLK, s), act_t


def _ffn_dw(name, lhs_t, rhs, tk, after=None):
    n_g, _, s = lhs_t.shape
    d = rhs.shape[1]
    return _matmul(name, lhs_t, rhs, a_blk=(None, FF_BLK, tk), a_map=lambda g, k: (g, 0, k), b_blk=(tk, d), b_map=lambda g, k: (k, 0),
                   o_shape=(n_g, FF_BLK, d), o_blk=(None, FF_BLK, d), o_map=lambda g, k: (g, 0, 0), grid=(n_g, s // tk),
                   contract=NN, out_dtype=BF16, after=after)


def _window_sum(x, window, transpose):
    s = x.shape[0]
    t = lax.broadcasted_iota(jnp.int32, (s, 1), 0)
    half = window // 2
    cnt = jnp.minimum(t + half, s) - jnp.maximum(t - half, 0)
    inv = 1.0 / cnt.astype(F32)
    if transpose:
        x = x * inv
        offsets = range(-half + 1, half + 1)
    else:
        offsets = range(-half, half)
    acc = jnp.zeros_like(x)
    for o in offsets:
        shifted = x if o == 0 else pltpu.roll(x, (-o) % s, 0)
        valid = jnp.logical_and(t + o >= 0, t + o < s)
        acc = acc + jnp.where(valid, shifted, 0.0)
    return acc if transpose else acc * inv


def _pool_mix(name, x, transpose, out_dtype):
    s, d = x.shape

    def body(x_ref, o_ref):
        g = pl.program_id(0)
        for gi, window in enumerate(POOL_WINDOWS):
            @pl.when(g == gi)
            def _(window=window):
                xv = x_ref[...].astype(F32)
                o_ref[...] = (_window_sum(xv, window, transpose) - xv).astype(o_ref.dtype)

    return pl.pallas_call(
        body, name=name, grid=(len(POOL_WINDOWS),), in_specs=[pl.BlockSpec((s, POOL_GROUP), lambda g: (0, g))],
        out_specs=pl.BlockSpec((s, POOL_GROUP), lambda g: (0, g)), out_shape=_sds((s, d), out_dtype), compiler_params=_params(),
    )(x)


def _pool_fwd(h, w4, bias, pscale, tm):
    s, d = h.shape
    nt = s // tm
    z = _pool_mix("pool_mix", h, False, BF16)
    v = _matmul("pool_proj", z, w4, a_blk=(tm, POOL_GROUP), a_map=lambda i, g, k: (i, g), b_blk=(None, POOL_GROUP, POOL_GROUP),
                b_map=lambda i, g, k: (g, 0, 0), o_shape=(s, d), o_blk=(tm, POOL_GROUP), o_map=lambda i, g, k: (i, g),
                grid=(nt, 4, 1), contract=NN, out_dtype=F32, bias=bias, bias_blk=(1, POOL_GROUP), bias_map=lambda i, g, k: (0, g))
    tr = min(512, s)
    (u,), _ = _rowmap("pool_scale", lambda vv, ps: ((vv * ps,), ()), [_tile(v, tr), _row(pscale)], [_otile(s, d, F32, tr)], [],
                      (s // tr,))
    return u, z, v


def _pool_bwd(du, z, v, w4, pscale, tm):
    s, d = du.shape
    nt = s // tm

    def fn(duv, vv, ps):
        dv = duv * ps
        return (dv,), (_colsum(duv * vv), _colsum(dv))

    tr = min(512, s)
    (dv,), reds = _rowmap("pool_dscale", fn, [_tile(du, tr), _tile(v, tr), _row(pscale)], [_otile(s, d, BF16, tr)],
                          [_ored(d), _ored(d)], (s // tr,))
    dw4 = _matmul("pool_dw", z, dv, a_blk=(tm, POOL_GROUP), a_map=lambda g, k: (k, g), b_blk=(tm, POOL_GROUP),
                  b_map=lambda g, k: (k, g), o_shape=(4, POOL_GROUP, POOL_GROUP), o_blk=(None, POOL_GROUP, POOL_GROUP),
                  o_map=lambda g, k: (g, 0, 0), grid=(4, nt), contract=TN, out_dtype=F32)
    dz = _matmul("pool_dz", dv, w4, a_blk=(tm, POOL_GROUP), a_map=lambda i, g, k: (i, g), b_blk=(None, POOL_GROUP, POOL_GROUP),
                 b_map=lambda i, g, k: (g, 0, 0), o_shape=(s, d), o_blk=(tm, POOL_GROUP), o_map=lambda i, g, k: (i, g),
                 grid=(nt, 4, 1), contract=NT, out_dtype=F32)
    dh = _pool_mix("pool_mix_t", dz, True, F32)
    return dh, dw4, reds


def _lane(shape):
    return lax.broadcasted_iota(jnp.int32, shape, 1)


def _rope_swap(v, transpose):
    half = QK_ROPE // 2
    lane = _lane(v.shape)
    up = pltpu.roll(v, v.shape[1] - half, 1)
    down = pltpu.roll(v, half, 1)
    if transpose:
        return jnp.where(lane < half, up, jnp.where(lane < QK_ROPE, -down, 0.0))
    return jnp.where(lane < half, -up, jnp.where(lane < QK_ROPE, down, 0.0))


def _rope(v, cos, sin):
    return v * cos + _rope_swap(v, False) * sin


def _rope_t(g, cos, sin):
    return g * cos + _rope_swap(g * sin, True)


def _mla_mid(lat, q_norm, kv_norm, cos_k, sin_k, tm):
    s = lat.shape[0]

    def fn(lv, qn, kn, cs, sn):
        cq = lv[:, :Q_LORA]
        ckv = lv[:, Q_LORA:Q_LORA + KV_LORA]
        kr = lv[:, Q_LORA + KV_LORA:]
        cq = cq * _rstd(cq) * qn
        ckv = ckv * _rstd(ckv) * kn
        k_rope = jnp.where(_lane(kr.shape) == ONE_COL - KV_LORA, 1.0, _rope(kr, cs, sn))
        return (cq, jnp.concatenate([ckv, k_rope], axis=1)), ()

    (cq, kcat), _ = _rowmap("mla_mid", fn, [_tile(lat, tm), _row(q_norm), _row(kv_norm), _tile(cos_k, tm), _tile(sin_k, tm)],
                            [_otile(s, Q_LORA, BF16, tm), _otile(s, QK_PAD, BF16, tm)], [], (s // tm,))
    return cq, kcat


def _mla_mid_bwd(lat, dcq, dkcat, dv, q_norm, kv_norm, cos_k, sin_k, tm):
    s = lat.shape[0]

    def fn(lv, dq, dk, dvv, qn, kn, cs, sn):
        dk = dk * (1.0 / LOG2_E)
        cq = lv[:, :Q_LORA]
        ckv = lv[:, Q_LORA:Q_LORA + KV_LORA]
        rq, rk = _rstd(cq), _rstd(ckv)
        cqn, ckn = cq * rq, ckv * rk
        a = dq * qn
        d_cq = rq * (a - cqn * jnp.mean(a * cqn, axis=-1, keepdims=True))
        dckv = dk[:, :KV_LORA] + dvv
        a2 = dckv * kn
        d_ckv = rk * (a2 - ckn * jnp.mean(a2 * ckn, axis=-1, keepdims=True))
        d_kr = _rope_t(dk[:, KV_LORA:], cs, sn)
        return (jnp.concatenate([d_cq, d_ckv, d_kr], axis=1),), (_colsum(dq * cqn), _colsum(dckv * ckn))

    (dlat,), reds = _rowmap(
        "mla_mid_bwd", fn,
        [_tile(lat, tm), _tile(dcq, tm), _tile(dkcat, tm), _tile(dv, tm), _row(q_norm), _row(kv_norm), _tile(cos_k, tm),
         _tile(sin_k, tm)],
        [_otile(s, LAT_PAD, BF16, tm)], [_ored(Q_LORA), _ored(KV_LORA)], (s // tm,))
    return dlat, reds


def _mla_q(cq, wq, wukp, cos_k, sin_k, tm):
    s = cq.shape[0]

    def body(cq_ref, wq_ref, wuk_ref, cos_ref, sin_ref, o_ref):
        cqv, cs, sn = cq_ref[...], cos_ref[...], sin_ref[...]
        for h in range(N_HEADS):
            aq = _dot(cqv, wq_ref[h], NN)
            qlat = _dot(aq.astype(BF16), wuk_ref[h], NN)
            roped = _rope(aq[:, KV_LORA:], cs, sn)
            o_ref[h] = (jnp.concatenate([qlat[:, :KV_LORA], roped], axis=1) * (ATTN_SCALE * LOG2_E)).astype(o_ref.dtype)

    wblk = pl.BlockSpec((N_HEADS, QK_PAD, QK_PAD), lambda i: (0, 0, 0))
    tblk = pl.BlockSpec((tm, KV_LORA), lambda i: (i, 0))
    return pl.pallas_call(
        body, name="mla_q", grid=(s // tm,),
        in_specs=[pl.BlockSpec((tm, Q_LORA), lambda i: (i, 0)), wblk, wblk, tblk, tblk],
        out_specs=pl.BlockSpec((N_HEADS, tm, QK_PAD), lambda i: (0, i, 0)), out_shape=_sds((N_HEADS, s, QK_PAD), BF16),
        compiler_params=_params(),
    )(cq, wq, wukp, cos_k, sin_k)


def _mla_q_bwd(cq, wq, wukp, cos_k, sin_k, dqcat, tm):
    s = cq.shape[0]

    def body(cq_ref, wq_ref, wuk_ref, cos_ref, sin_ref, dq_ref, dcq_ref, dwq_ref, dwuk_ref):
        @pl.when(pl.program_id(0) == 0)
        def _():
            dwq_ref[...] = jnp.zeros_like(dwq_ref)
            dwuk_ref[...] = jnp.zeros_like(dwuk_ref)

        cqv, cs, sn = cq_ref[...], cos_ref[...], sin_ref[...]
        d_cq = jnp.zeros((tm, Q_LORA), F32)
        for h in range(N_HEADS):
            aq = _dot(cqv, wq_ref[h], NN).astype(BF16)
            g = dq_ref[h].astype(F32) * ATTN_SCALE
            gl, gr = g[:, :KV_LORA], g[:, KV_LORA:]
            dqlat = jnp.concatenate([gl, jnp.zeros_like(gl)], axis=1).astype(BF16)
            d_rope = _rope_t(gr, cs, sn)
            daq = _dot(dqlat, wuk_ref[h], NT) + jnp.concatenate([jnp.zeros_like(d_rope), d_rope], axis=1)
            daq_b = daq.astype(BF16)
            dwuk_ref[h] += _dot(aq, dqlat, TN)
            dwq_ref[h] += _dot(cqv, daq_b, TN)
            d_cq = d_cq + _dot(daq_b, wq_ref[h], NT)
        dcq_ref[...] = d_cq

    wblk = pl.BlockSpec((N_HEADS, QK_PAD, QK_PAD), lambda i: (0, 0, 0))
    tblk = pl.BlockSpec((tm, KV_LORA), lambda i: (i, 0))
    return pl.pallas_call(
        body, name="mla_q_bwd", grid=(s // tm,),
        in_specs=[pl.BlockSpec((tm, Q_LORA), lambda i: (i, 0)), wblk, wblk, tblk, tblk,
                  pl.BlockSpec((N_HEADS, tm, QK_PAD), lambda i: (0, i, 0))],
        out_specs=[pl.BlockSpec((tm, Q_LORA), lambda i: (i, 0)), wblk, wblk],
        out_shape=[_sds((s, Q_LORA), F32), _sds((N_HEADS, QK_PAD, QK_PAD), F32), _sds((N_HEADS, QK_PAD, QK_PAD), F32)],
        compiler_params=_params(),
    )(cq, wq, wukp, cos_k, sin_k, dqcat)


def _flash_fwd(qcat, kcat, tq, tk):
    n_h, s, _ = qcat.shape
    n_k = s // tk

    def body(q_ref, k_ref, o_ref, lse_ref):
        q = q_ref[...]
        m = jnp.full((tq, 1), -1e30, F32)
        acc = jnp.zeros((tq, QK_PAD), F32)
        for kk in range(n_k):
            k = k_ref[pl.ds(kk * tk, tk), :]
            sc = _dot(q, k, NT)
            m_new = jnp.maximum(m, jnp.max(sc, axis=1, keepdims=True))
            p = jnp.exp2(sc - m_new).astype(BF16)
            acc = jnp.exp2(m - m_new) * acc + _dot(p, k, NN)
            m = m_new
        l = jnp.sum(jnp.where(_lane(acc.shape) == ONE_COL, acc, 0.0), axis=1, keepdims=True)
        o_ref[...] = (acc[:, :KV_LORA] / l).astype(o_ref.dtype)
        lse_ref[...] = m + jnp.log2(l)

    return pl.pallas_call(
        body, name="mla_attn", grid=(n_h, s // tq),
        in_specs=[pl.BlockSpec((None, tq, QK_PAD), lambda h, i: (h, i, 0)), pl.BlockSpec((s, QK_PAD), lambda h, i: (0, 0))],
        out_specs=[pl.BlockSpec((None, tq, KV_LORA), lambda h, i: (h, i, 0)), pl.BlockSpec((None, tq, 1), lambda h, i: (h, i, 0))],
        out_shape=[_sds((n_h, s, KV_LORA), BF16), _sds((n_h, s, 1), F32)], compiler_params=_params(),
    )(qcat, kcat)


def _flash_bwd(qcat, kcat, o, do, lse, tq, tk):
    n_h, s, _ = qcat.shape
    n_k = s // tk

    def body(q_ref, k_ref, v_ref, o_ref, do_ref, lse_ref, dq_ref, dk_ref, dv_ref, dq_acc):
        h, i = pl.program_id(0), pl.program_id(1)

        @pl.when(jnp.logical_and(h == 0, i == 0))
        def _():
            dk_ref[...] = jnp.zeros_like(dk_ref)
            dv_ref[...] = jnp.zeros_like(dv_ref)

        q = q_ref[...]
        dov = do_ref[...]
        dov_t = dov.T
        lse_v = lse_ref[...]
        delta = jnp.sum(dov.astype(F32) * o_ref[...].astype(F32), axis=1, keepdims=True)
        dq_acc[...] = jnp.zeros_like(dq_acc)

        for kk in range(n_k):
            rows = pl.ds(kk * tk, tk)
            k = k_ref[rows, :]
            p = jnp.exp2(_dot(q, k, NT) - lse_v)
            dp = _dot(dov, v_ref[rows, :], NT)
            ds = (p * (dp - delta)).astype(BF16)
            dq_acc[...] += _dot(ds, k, NN)
            dv_ref[:, rows] += _dot(dov_t, p.astype(BF16), NN)
            dk_ref[rows, :] += _dot(ds, q, TN)
        dq_ref[...] = dq_acc[...].astype(dq_ref.dtype)

    qblk = pl.BlockSpec((None, tq, QK_PAD), lambda h, i: (h, i, 0))
    oblk = pl.BlockSpec((None, tq, KV_LORA), lambda h, i: (h, i, 0))
    return pl.pallas_call(
        body, name="mla_attn_bwd", grid=(n_h, s // tq),
        in_specs=[qblk, pl.BlockSpec((s, QK_PAD), lambda h, i: (0, 0)), pl.BlockSpec((s, KV_LORA), lambda h, i: (0, 0)), oblk, oblk,
                  pl.BlockSpec((None, tq, 1), lambda h, i: (h, i, 0))],
        out_specs=[qblk, pl.BlockSpec((s, QK_PAD), lambda h, i: (0, 0)), pl.BlockSpec((KV_LORA, s), lambda h, i: (0, 0))],
        out_shape=[_sds((n_h, s, QK_PAD), BF16), _sds((s, QK_PAD), F32), _sds((KV_LORA, s), F32)],
        scratch_shapes=[pltpu.VMEM((tq, QK_PAD), F32)], compiler_params=_params(),
    )(qcat, kcat, kcat, o, do, lse)


def _mla_uv(o_lat, wuv2, do, tm):
    n_h, s, _ = o_lat.shape
    d = n_h * V_HEAD
    pair = 2 * V_HEAD
    lat_blk = pl.BlockSpec((n_h, tm, KV_LORA), lambda i: (0, i, 0))
    w_blk = pl.BlockSpec((n_h, KV_LORA, pair), lambda i: (0, 0, 0))
    row_blk = pl.BlockSpec((tm, d), lambda i: (i, 0))

    if do is None:
        def body(a_ref, w_ref, o_ref):
            for p in range(n_h // 2):
                o_ref[:, p * pair:(p + 1) * pair] = (
                    _dot(a_ref[2 * p], w_ref[2 * p], NN) + _dot(a_ref[2 * p + 1], w_ref[2 * p + 1], NN)).astype(o_ref.dtype)

        return pl.pallas_call(body, name="mla_uv", grid=(s // tm,), in_specs=[lat_blk, w_blk], out_specs=row_blk,
                              out_shape=_sds((s, d), BF16), compiler_params=_params())(o_lat, wuv2)

    def body(a_ref, w_ref, do_ref, dlat_ref, dw_ref):
        @pl.when(pl.program_id(0) == 0)
        def _():
            dw_ref[...] = jnp.zeros_like(dw_ref)

        for h in range(n_h):
            dov = do_ref[:, (h // 2) * pair:(h // 2 + 1) * pair]
            dlat_ref[h] = _dot(dov, w_ref[h], NT).astype(dlat_ref.dtype)
            dw_ref[h] += _dot(a_ref[h], dov, TN)

    return pl.pallas_call(body, name="mla_uv_bwd", grid=(s // tm,), in_specs=[lat_blk, w_blk, row_blk], out_specs=[lat_blk, w_blk],
                          out_shape=[_sds((n_h, s, KV_LORA), BF16), _sds((n_h, KV_LORA, pair), F32)], compiler_params=_params(),
                          )(o_lat, wuv2, do)


def _mla_fwd(h, wts, cos_k, sin_k, tm):
    s, d = h.shape
    nt = s // tm
    lat = _matmul("mla_lat", h, wts["w_in"], a_blk=(tm, d), a_map=lambda i, k: (i, 0), b_blk=(d, LAT_PAD), b_map=lambda i, k: (0, 0),
                  o_shape=(s, LAT_PAD), o_blk=(tm, LAT_PAD), o_map=lambda i, k: (i, 0), grid=(nt, 1), contract=NN, out_dtype=F32)
    cq, kcat = _mla_mid(lat, wts["q_norm"], wts["kv_norm"], cos_k, sin_k, tm)
    qcat = _mla_q(cq, wts["wq"], wts["wukp"], cos_k, sin_k, tm)
    o_lat, lse = _flash_fwd(qcat, kcat, min(2 * tm, s), tm)
    o = _mla_uv(o_lat, wts["wuv2"], None, tm)
    u = _matmul("mla_out", o, wts["w_o"], a_blk=(tm, d), a_map=lambda i, k: (i, 0), b_blk=(d, d), b_map=lambda i, k: (0, 0),
                o_shape=(s, d), o_blk=(tm, d), o_map=lambda i, k: (i, 0), grid=(nt, 1), contract=NN, out_dtype=F32)
    return u, (lat, cq, kcat, qcat, o_lat, lse, o)


def _mla_bwd(du, h, saved, wts, cos_k, sin_k, tm):
    lat, cq, kcat, qcat, o_lat, lse, o = saved
    s, d = h.shape
    nt = s // tm
    do = _matmul("mla_do", du, wts["w_o"], a_blk=(tm, d), a_map=lambda i, k: (i, 0), b_blk=(d, d), b_map=lambda i, k: (0, 0),
                 o_shape=(s, d), o_blk=(tm, d), o_map=lambda i, k: (i, 0), grid=(nt, 1), contract=NT, out_dtype=BF16)
    dw_o = _matmul("mla_dwo", o, du, a_blk=(tm, d), a_map=lambda k: (k, 0), b_blk=(tm, d), b_map=lambda k: (k, 0),
                   o_shape=(d, d), o_blk=(d, d), o_map=lambda k: (0, 0), grid=(nt,), contract=TN, out_dtype=F32)
    do_lat, dwuv2 = _mla_uv(o_lat, wts["wuv2"], do, tm)
    dqcat, dkcat, dv_t = _flash_bwd(qcat, kcat, o_lat, do_lat, lse, min(2 * tm, s), tm)
    dv = dv_t.T
    dcq, dwq, dwukp = _mla_q_bwd(cq, wts["wq"], wts["wukp"], cos_k, sin_k, dqcat, tm)
    dlat, (dqn, dkn) = _mla_mid_bwd(lat, dcq, dkcat, dv, wts["q_norm"], wts["kv_norm"], cos_k, sin_k, tm)
    dh = _matmul("mla_dh", dlat, wts["w_in"], a_blk=(tm, LAT_PAD), a_map=lambda i, k: (i, 0), b_blk=(d, LAT_PAD),
                 b_map=lambda i, k: (0, 0), o_shape=(s, d), o_blk=(tm, d), o_map=lambda i, k: (i, 0), grid=(nt, 1), contract=NT,
                 out_dtype=F32)
    dw_in = _matmul("mla_dwin", h, dlat, a_blk=(tm, d), a_map=lambda k: (k, 0), b_blk=(tm, LAT_PAD), b_map=lambda k: (k, 0),
                    o_shape=(d, LAT_PAD), o_blk=(d, LAT_PAD), o_map=lambda k: (0, 0), grid=(nt,), contract=TN, out_dtype=F32)
    return dh, dict(w_in=dw_in, wq=dwq, wukp=dwukp, wuv2=dwuv2, w_o=dw_o, q_norm=dqn, kv_norm=dkn)


def _adamw(name, parts, w, m, v):
    n_parts, r, c = parts.shape
    tr = r
    for cand in (256, 128, 64, 32, 16, 8):
        if r > cand and r % cand == 0:
            tr = cand
            break

    def body(p_ref, w_ref, m_ref, v_ref, g_ref, d_ref, nm_ref, nv_ref):
        g = p_ref[0].astype(F32)
        for k in range(1, n_parts):
            g = g + p_ref[k].astype(F32)
        nm = ADAM_B1 * m_ref[...] + (1.0 - ADAM_B1) * g
        nv = ADAM_B2 * v_ref[...] + (1.0 - ADAM_B2) * (g * g)
        m_hat = nm / (1.0 - ADAM_B1 ** ADAM_STEP)
        v_hat = nv / (1.0 - ADAM_B2 ** ADAM_STEP)
        g_ref[...] = g
        d_ref[...] = -ADAM_LR * (m_hat / (jnp.sqrt(v_hat) + ADAM_EPS) + ADAM_WD * w_ref[...])
        nm_ref[...] = nm
        nv_ref[...] = nv

    blk = pl.BlockSpec((tr, c), lambda i: (i, 0))
    return pl.pallas_call(
        body, name=name, grid=(r // tr,), in_specs=[pl.BlockSpec((n_parts, tr, c), lambda i: (0, i, 0)), blk, blk, blk],
        out_specs=[blk] * 4, out_shape=[_sds((r, c), F32)] * 4, compiler_params=_params(),
    )(parts, w, m, v)


def _adamw_slab(name, parts, w, m, v, bufs, f):
    n_parts, r, c = parts.shape
    tr = max(t for t in range(8, 257, 8) if r % t == 0)

    def body(p_ref, w_ref, m_ref, v_ref, *rest):
        g_ref, d_ref, nm_ref, nv_ref = rest[4:]
        g = p_ref[0].astype(F32)
        for k in range(1, n_parts):
            g = g + p_ref[k].astype(F32)
        nm = ADAM_B1 * m_ref[...] + (1.0 - ADAM_B1) * g
        nv = ADAM_B2 * v_ref[...] + (1.0 - ADAM_B2) * (g * g)
        m_hat = nm / (1.0 - ADAM_B1 ** ADAM_STEP)
        v_hat = nv / (1.0 - ADAM_B2 ** ADAM_STEP)
        g_ref[...] = g
        d_ref[...] = -ADAM_LR * (m_hat / (jnp.sqrt(v_hat) + ADAM_EPS) + ADAM_WD * w_ref[...])
        nm_ref[...] = nm
        nv_ref[...] = nv

    blk = pl.BlockSpec((None, tr, c), lambda i: (f, i, 0))
    return pl.pallas_call(
        body, name=name, grid=(r // tr,),
        in_specs=[pl.BlockSpec((n_parts, tr, c), lambda i: (0, i, 0)), blk, blk, blk] + [pl.BlockSpec(memory_space=pl.ANY)] * 4,
        out_specs=[blk] * 4, out_shape=[_sds(w.shape, F32)] * 4, input_output_aliases={4 + j: j for j in range(4)},
        compiler_params=_params(),
    )(parts, w, m, v, *bufs)


def _mesh_pos():
    return lax.axis_index("x"), lax.axis_index("y"), lax.axis_index("c")


def _flip(pos, mask):
    return tuple(1 - p if (mask >> (2 - b)) & 1 else p for b, p in enumerate(pos))


def _index(pos):
    return 4 * pos[0] + 2 * pos[1] + pos[2]


def _all_gather(name, xs, after=None):
    n = len(xs)
    extra = [] if after is None else [after]

    def body(*refs):
        x_refs, o_refs = refs[:n], refs[n + len(extra):2 * n + len(extra)]
        send_sems, recv_sems, local_sems = refs[2 * n + len(extra):]
        me = _mesh_pos()
        sibling = _flip(me, 1)
        others = [_flip(me, 4), _flip(me, 2), _flip(me, 6)]

        def copy(k, j, block, to, src=None):
            dst = o_refs[k].at[_index(block)]
            return pltpu.make_async_remote_copy(
                src_ref=dst if src is None else src, dst_ref=dst, send_sem=send_sems.at[k, j], recv_sem=recv_sems.at[k, j],
                device_id=to, device_id_type=MESH)

        local = [pltpu.make_async_copy(x_refs[k], o_refs[k].at[_index(me)], local_sems.at[k]) for k in range(n)]
        for cp in local:
            cp.start()
        first = []
        for k in range(n):
            first.append(copy(k, 0, me, sibling, src=x_refs[k]))
            first += [copy(k, 1 + j, me, other, src=x_refs[k]) for j, other in enumerate(others)]
        for cp in first:
            cp.start()
        passed = []
        for j, other in enumerate(others):
            for k in range(n):
                copy(k, 1 + j, other, me).wait_recv()
                cp = copy(k, 4 + j, other, sibling)
                cp.start()
                passed.append(cp)
        for k in range(n):
            copy(k, 0, sibling, me).wait_recv()
        for j, other in enumerate(others):
            for k in range(n):
                copy(k, 4 + j, _flip(other, 1), me).wait_recv()
        for cp in first + passed:
            cp.wait_send()
        for cp in local:
            cp.wait()

    any_spec = pl.BlockSpec(memory_space=pl.ANY)
    return pl.pallas_call(
        body, name=name, in_specs=[any_spec] * (n + len(extra)), out_specs=[any_spec] * n,
        out_shape=[_sds((N_DEV,) + x.shape, x.dtype) for x in xs],
        scratch_shapes=[pltpu.SemaphoreType.DMA((n, 7)), pltpu.SemaphoreType.DMA((n, 7)), pltpu.SemaphoreType.DMA((n,))],
    )(*xs, *extra)


_MASKS = {
    "gather": tuple(range(1, N_DEV)),
    "scatter": tuple(range(1, N_DEV)),
    "own": (1, 4, 2, 6),
    "pass": (4, 2, 6),
}


def _split_copies(kind, outgoing, x_refs, land_refs, send_sems, recv_sems):
    me = _mesh_pos()
    masks = _MASKS[kind]
    copies = []
    for k, (x_ref, land_ref) in enumerate(zip(x_refs, land_refs)):
        for j, mask in enumerate(masks):
            if kind == "pass":
                peer = _flip(me, 1)
                src = land_ref.at[_index(_flip(me, mask))]
                dst = land_ref.at[_index(_flip(me, mask if outgoing else mask | 1))]
            else:
                peer = _flip(me, mask)
                src = x_ref.at[_index(peer)] if kind == "scatter" else x_ref
                dst = land_ref.at[_index(me if outgoing else peer)]
            sem = k * len(masks) + j
            copies.append(pltpu.make_async_remote_copy(src_ref=src, dst_ref=dst, send_sem=send_sems.at[sem], recv_sem=recv_sems.at[sem],
                                                       device_id=peer, device_id_type=MESH))
    return copies


_HBM_SPEC = pl.BlockSpec(memory_space=pltpu.HBM)
_SEM_SPEC = pl.BlockSpec(memory_space=pltpu.SEMAPHORE)
_EFFECT = pltpu.SideEffectType.DATAFLOW_SIDE_EFFECTING


def _split_start(name, kind, xs, after=None):
    n = len(xs)
    n_sem = n * len(_MASKS[kind])
    extra = [] if after is None else [after]
    lands = [lax.empty(x.shape if kind == "scatter" else (N_DEV,) + x.shape, x.dtype) for x in xs]

    def body(*refs):
        x_refs, land_refs = refs[:n], refs[n:2 * n]
        send_sems, recv_sems = refs[2 * n + len(extra)], refs[2 * n + len(extra) + 1]
        token = refs[-1]
        for cp in _split_copies(kind, True, x_refs, land_refs, send_sems, recv_sems):
            cp.start()
        token[...] = jnp.zeros_like(token)

    hbm = [pltpu.HBM(a.shape, a.dtype) for a in list(xs) + lands]
    res = pl.pallas_call(
        body, name=name,
        out_shape=[pltpu.SemaphoreType.DMA((n_sem,)), pltpu.SemaphoreType.DMA((n_sem,))] + hbm + [_sds((8, 128), F32)],
        in_specs=[_HBM_SPEC] * (2 * n) + [pl.BlockSpec(memory_space=pl.ANY)] * len(extra),
        out_specs=[_SEM_SPEC, _SEM_SPEC] + [_HBM_SPEC] * (2 * n) + [pl.BlockSpec(memory_space=pltpu.VMEM)],
        input_output_aliases={j: 2 + j for j in range(2 * n)}, compiler_params=pltpu.CompilerParams(has_side_effects=_EFFECT),
    )(*[pltpu.with_memory_space_constraint(a, pltpu.HBM) for a in list(xs) + lands], *extra)
    return (kind, n, res[0], res[1], res[2:2 + 2 * n]), res[-1]


def _split_pass(name, state, after):
    kind, n, send_sems_in, recv_sems_in, thru = state
    n_sem = n * len(_MASKS["pass"])

    def body(*refs):
        x_refs, land_refs = refs[:n], refs[n:2 * n]
        send_sems, recv_sems = refs[2 * n], refs[2 * n + 1]
        next_send, next_recv, token = refs[-3:]
        for cp in _split_copies(kind, True, x_refs, land_refs, send_sems, recv_sems):
            cp.wait_send()
        for cp in _split_copies(kind, False, x_refs, land_refs, send_sems, recv_sems):
            cp.wait_recv()
        for cp in _split_copies("pass", True, land_refs, land_refs, next_send, next_recv):
            cp.start()
        token[...] = jnp.zeros_like(token)

    res = pl.pallas_call(
        body, name=name,
        out_shape=[pltpu.HBM(a.shape, a.dtype) for a in thru] + [pltpu.SemaphoreType.DMA((n_sem,)), pltpu.SemaphoreType.DMA((n_sem,)),
                                                                 _sds((8, 128), F32)],
        in_specs=[_HBM_SPEC] * (2 * n) + [_SEM_SPEC, _SEM_SPEC, pl.BlockSpec(memory_space=pl.ANY)],
        out_specs=[_HBM_SPEC] * (2 * n) + [_SEM_SPEC, _SEM_SPEC, pl.BlockSpec(memory_space=pltpu.VMEM)],
        input_output_aliases={j: j for j in range(2 * n)}, compiler_params=pltpu.CompilerParams(has_side_effects=_EFFECT),
    )(*thru, send_sems_in, recv_sems_in, after)
    return ("pass", n, res[2 * n], res[2 * n + 1], res[:2 * n]), res[-1]


def _split_wait(name, state, after):
    kind, n, send_sems_in, recv_sems_in, thru = state

    def body(*refs):
        x_refs, land_refs = refs[:n], refs[n:2 * n]
        send_sems, recv_sems = refs[2 * n], refs[2 * n + 1]
        for cp in _split_copies(kind, True, x_refs, land_refs, send_sems, recv_sems):
            cp.wait_send()
        for cp in _split_copies(kind, False, x_refs, land_refs, send_sems, recv_sems):
            cp.wait_recv()

    res = pl.pallas_call(
        body, name=name, out_shape=[pltpu.HBM(a.shape, a.dtype) for a in thru],
        in_specs=[_HBM_SPEC] * (2 * n) + [_SEM_SPEC, _SEM_SPEC, pl.BlockSpec(memory_space=pl.ANY)], out_specs=[_HBM_SPEC] * (2 * n),
        input_output_aliases={j: j for j in range(2 * n)}, compiler_params=pltpu.CompilerParams(has_side_effects=_EFFECT),
    )(*thru, send_sems_in, recv_sems_in, after)
    me = _index(_mesh_pos())
    out = []
    for x, land in zip(res[:n], res[n:]):
        own = lax.dynamic_index_in_dim(x, me, 0, keepdims=False) if kind == "scatter" else x
        out.append(lax.dynamic_update_slice(land, own[None], (me,) + (0,) * own.ndim))
    return out


def _rope_tables(s):
    inv = 1.0 / (ROPE_THETA ** (jnp.arange(0, QK_ROPE, 2, dtype=F32) / QK_ROPE))
    ang = jnp.arange(s, dtype=F32)[:, None] * inv[None, :]
    pad = jnp.zeros((s, KV_LORA - QK_ROPE), F32)
    return (jnp.concatenate([jnp.cos(ang), jnp.cos(ang), pad], axis=1), jnp.concatenate([jnp.sin(ang), jnp.sin(ang), pad], axis=1))


def kernel(x, c, ada_w, ada_b, norm_g, ffn_w_in, ffn_w_out, pool_w, pool_b, pool_scale, mla_w_in, mla_q_norm, mla_kv_norm, mla_w_uq, mla_w_uk, mla_w_uv, mla_w_o, loss_target, m_ada_w, m_ada_b, m_norm_g, m_ffn_w_in, m_ffn_w_out, m_pool_w, m_pool_b, m_pool_scale, m_mla_w_in, m_mla_q_norm, m_mla_kv_norm, m_mla_w_uq, m_mla_w_uk, m_mla_w_uv, m_mla_w_o, v_ada_w, v_ada_b, v_norm_g, v_ffn_w_in, v_ffn_w_out, v_pool_w, v_pool_b, v_pool_scale, v_mla_w_in, v_mla_q_norm, v_mla_kv_norm, v_mla_w_uq, v_mla_w_uk, v_mla_w_uv, v_mla_w_o):
    s, d = x.shape[1], x.shape[2]
    tm = min(512, s)
    tr = min(512, s)
    tf = min(1024, s)
    tw = min(2048, s)
    me = 4 * lax.axis_index("x") + 2 * lax.axis_index("y") + lax.axis_index("c")
    x0 = x.reshape(s, d)
    target = loss_target.reshape(s, d)
    n_mod = ada_w.shape[2] * N_DEV // d
    mod_blk = ada_w.shape[2]

    small = jnp.concatenate([c.reshape(-1), norm_g.reshape(-1), pool_b.reshape(-1), mla_q_norm.reshape(-1)]).reshape(1, -1)
    w_in_t, m_in_t, v_in_t = (jnp.swapaxes(a, 2, 3) for a in (ffn_w_in, m_ffn_w_in, v_ffn_w_in))
    w_in_loc = [w_in_t[i, f].astype(BF16) for i in range(2) for f in range(2)]
    w_out_loc = [ffn_w_out[i, f].astype(BF16) for i in range(2) for f in range(2)]
    (small_all,) = _all_gather("gather_small", [small])
    small_all = small_all.reshape(N_DEV, -1)
    c_all = small_all[:, :d]
    off = d
    g_all = small_all[:, off:off + 12 * (d // N_DEV)].reshape(N_DEV, 2, 6, d // N_DEV).transpose(1, 2, 0, 3).reshape(2, 6, d)
    off += 12 * (d // N_DEV)
    pool_b_all = small_all[:, off:off + 4 * POOL_SHARD].reshape(N_DEV, 4, POOL_SHARD).transpose(1, 0, 2).reshape(1, d)
    off += 4 * POOL_SHARD
    q_norm_all = small_all[:, off:off + Q_SHARD].reshape(1, Q_LORA)
    kv_norm_row = mla_kv_norm.reshape(1, KV_LORA)
    pscale_row = pool_scale.reshape(1, d)

    even = (jnp.arange(N_HEADS) % 2 == 0)[:, None, None]
    cos_k, sin_k = _rope_tables(s)

    def mla_weights(mla_w_in_all, mla_w_uq_all, mla_w_o_all):
        uq = mla_w_uq_all.reshape(Q_LORA, N_HEADS, QK_NOPE + QK_ROPE).transpose(1, 0, 2)
        zq = jnp.zeros((N_HEADS, Q_LORA, QK_NOPE), BF16)
        wq = jnp.concatenate(
            [uq[:, :, :QK_NOPE], zq, uq[:, :, QK_NOPE:], jnp.zeros((N_HEADS, Q_LORA, QK_PAD - KV_LORA - QK_ROPE), BF16)], axis=2)
        wukp = jnp.pad(mla_w_uk[0].transpose(1, 2, 0).astype(BF16), ((0, 0), (0, QK_PAD - QK_NOPE), (0, QK_PAD - KV_LORA)))
        uv = mla_w_uv[0].transpose(1, 0, 2).astype(BF16)
        wuv2 = jnp.where(even, jnp.concatenate([uv, jnp.zeros_like(uv)], axis=2), jnp.concatenate([jnp.zeros_like(uv), uv], axis=2))
        return dict(w_in=jnp.pad(mla_w_in_all.reshape(d, -1), ((0, 0), (0, LAT_PAD - mla_w_in.shape[2]))), wq=wq, wukp=wukp,
                    wuv2=wuv2, w_o=mla_w_o_all.reshape(d, d), q_norm=q_norm_all, kv_norm=kv_norm_row)

    (sc_all,), _ = _rowmap("ada_silu", lambda cv: ((cv * jax.nn.sigmoid(cv),), ()), [(c_all, (N_DEV, d), lambda i: (0, 0))],
                           [(_sds((N_DEV, d), F32), (N_DEV, d), lambda i: (0, 0))], [], (1,))
    ada_b_loc = lax.dynamic_slice_in_dim(ada_b, me * mod_blk, mod_blk, axis=1).reshape(2, 1, mod_blk)
    m_pad = 2 * N_DEV
    modp = _matmul("ada_mod", jnp.pad(sc_all, ((0, m_pad - N_DEV), (0, 0))), ada_w, a_blk=(m_pad, d), a_map=lambda i, k: (0, 0),
                   b_blk=(None, d, mod_blk), b_map=lambda i, k: (i, 0, 0), o_shape=(2, m_pad, mod_blk), o_blk=(None, m_pad, mod_blk),
                   o_map=lambda i, k: (i, 0, 0), grid=(2, 1), contract=NN, out_dtype=F32, bias=ada_b_loc, bias_blk=(None, 1, mod_blk),
                   bias_map=lambda i, k: (i, 0, 0))[:, :N_DEV]
    (modp_all,) = _all_gather("gather_mod", [modp.reshape(2 * N_DEV, mod_blk)])
    groups = [[w_in_loc[0], w_out_loc[0], pool_w.reshape(-1, POOL_GROUP).astype(BF16)], [w_in_loc[1], w_out_loc[1]],
              [w_in_loc[2], w_out_loc[2], mla_w_in[0].astype(BF16), mla_w_uq.reshape(mla_w_uq.shape[1], -1).astype(BF16),
               mla_w_o[0].astype(BF16)], [w_in_loc[3], w_out_loc[3]]]
    states, token = [], modp_all
    for name, group in zip("0abc", groups):
        state, token = _split_start(f"gather_start_{name}", "own", group, after=token)
        states.append(state)
    w_in8 = [None] * 4
    w_out4 = [None] * 4
    mod = lax.dynamic_index_in_dim(modp_all.reshape(N_DEV, 2, N_DEV, mod_blk), me, axis=2, keepdims=False)
    mod = mod.transpose(1, 0, 2).reshape(2, n_mod, d) + token[0, 0]
    mod_rows = mod.reshape(2 * n_mod, 1, d)
    g_rows = g_all.reshape(12, 1, d)
    weights_of = (0.5, 1.0, 0.5, 0.5, 1.0, 0.5)

    def pre_rows(k, with_shift):
        rows = [_rowk(g_rows, 2 * k), _rowk(mod_rows, 3 * k + 1)]
        return rows + [_rowk(mod_rows, 3 * k)] if with_shift else rows

    def post_rows(k):
        return [_rowk(g_rows, 2 * k + 1), _rowk(mod_rows, 3 * k + 2)]

    def act_dtype(k):
        return F32 if k == 1 else BF16

    saved = []
    xs = x0
    mla_wts = None
    h = _prenorm("prenorm_0", xs, pre_rows(0, True), act_dtype(0), tr)
    state, token = _split_pass("gather_pass_0", states[0], h)
    lands = _split_wait("gather_wait_0", state, token)
    w_in8[0], w_out4[0] = lands[0], lands[1].reshape(4, FF_BLK, d)
    w4 = lands[2].reshape(N_DEV, 4, POOL_SHARD, POOL_GROUP).transpose(1, 0, 2, 3).reshape(4, POOL_GROUP, POOL_GROUP)
    token = None
    for k in range(6):
        i, sub = divmod(k, 3)
        tag = f"{i}{sub}"
        if k == 1:
            states[1], token = _split_pass("gather_pass_a", states[1], xs)
        if k == 2:
            lands = _split_wait("gather_wait_a", states[1], xs)
            w_in8[1], w_out4[1] = lands[0], lands[1].reshape(4, FF_BLK, d)
            states[2], token = _split_pass("gather_pass_b", states[2], lands[0])
        if k == 3:
            lands = _split_wait("gather_wait_b", states[2], xs)
            w_in8[2], w_out4[2] = lands[0], lands[1].reshape(4, FF_BLK, d)
            mla_wts = mla_weights(*lands[2:])
            states[3], token = _split_pass("gather_pass_c", states[3], lands[0])
        if k == 5:
            lands = _split_wait("gather_wait_c", states[3], xs)
            w_in8[3], w_out4[3] = lands[0], lands[1].reshape(4, FF_BLK, d)
        if sub != 1:
            u, extra = _ffn_fwd(tag, h, w_in8[2 * i + sub // 2], w_out4[2 * i + sub // 2], tf, tf // 2, after=token)
            token = None
        elif i == 0:
            u, z, v = _pool_fwd(h, w4, pool_b_all + token[0, 0], pscale_row, min(4 * tm, s))
            extra = (z, v)
            token = None
        else:
            u, extra = _mla_fwd(h, mla_wts, cos_k, sin_k, tm)
        saved.append((xs, h, u, extra))
        if k < 5:
            xs, h = _norm_link(f"norm_link_{k + 1}", xs, u, post_rows(k), weights_of[k], pre_rows(k + 1, True), act_dtype(k + 1), tr)

    dx, du, (sq, dgate, dgpost) = _norm_loss("norm_loss", xs, u, target, post_rows(5), weights_of[5], tr)
    loss_part = (0.5 * jnp.sum(sq) / d).reshape(1, 1)

    d_mod = [None] * (2 * n_mod)
    d_g = [None] * 12
    d_mod[3 * 5 + 2], d_g[2 * 5 + 1] = dgate, dgpost
    sent = {}
    pool_grads = mla_grads = None

    def start_scatter(key, arrays):
        state, token = _split_start(f"scatter_start_{key}", "scatter", arrays)
        sent[key] = state
        return token

    for k in (5, 4, 3, 2, 1, 0):
        i, sub = divmod(k, 3)
        tag = f"{i}{sub}"
        xin, h, u, extra = saved[k]
        token = None
        if sub != 1:
            f = 2 * i + sub // 2
            dh, dgu, act = _ffn_bwd_act(tag, du, extra, w_in8[f], w_out4[f], tf, tf // 4)
            dw_out = _ffn_dw(f"ffn_dwout_{tag}", act, du, tw).reshape(N_DEV, FF_BLK // 2, d)
            if k == 0:
                d_pool_w = pool_grads[0].reshape(4, N_DEV, POOL_SHARD, POOL_GROUP).transpose(1, 0, 2, 3).reshape(N_DEV, -1, POOL_GROUP)
                token = start_scatter(tag + "_out", [dw_out, d_pool_w])
                token = start_scatter(tag + "_in", [_ffn_dw(f"ffn_dwin_{tag}", dgu, h, tw, after=token)])
            else:
                token = start_scatter(tag, [_ffn_dw(f"ffn_dwin_{tag}", dgu, h, tw), dw_out])
        elif i == 0:
            dh, dw4, (dpscale, dpb) = _pool_bwd(du, extra[0], extra[1], w4, pscale_row, min(4 * tm, s))
            pool_grads = (dw4, dpscale, dpb)
        else:
            dh, mla_grads = _mla_bwd(du, h, extra, mla_wts, cos_k, sin_k, tm)
            dwq = mla_grads["wq"]
            d_uq = jnp.concatenate([dwq[:, :, :QK_NOPE], dwq[:, :, KV_LORA:KV_LORA + QK_ROPE]], axis=2).transpose(1, 0, 2)
            token = start_scatter("mla", [mla_grads["w_in"][:, :mla_w_in.shape[2]].reshape(N_DEV, d // N_DEV, -1),
                                          d_uq.reshape(N_DEV, Q_LORA // N_DEV, -1), mla_grads["w_o"].reshape(N_DEV, d // N_DEV, d)])
            dwukp, dwuv2 = mla_grads["wukp"], mla_grads["wuv2"]
            d_uk = dwukp[:, :QK_NOPE, :KV_LORA].transpose(2, 0, 1).reshape(KV_LORA, -1)
            d_uv = jnp.where(even, dwuv2[:, :, :V_HEAD], dwuv2[:, :, V_HEAD:]).transpose(1, 0, 2).reshape(KV_LORA, -1)
            state_ukv, token = _split_start("gather_start_ukv", "gather", [d_uk, d_uv], after=token)
        if k > 0:
            dx, du, reds = _norm_link_bwd(f"norm_link_bwd_{k}", dh, xin, dx, saved[k - 1][2], pre_rows(k, False), post_rows(k - 1),
                                          weights_of[k - 1], act_dtype(k - 1), tr, after=token)
            d_mod[3 * (k - 1) + 2], d_g[2 * (k - 1) + 1] = reds[3], reds[4]
        else:
            dx, reds = _prenorm_bwd("prenorm_bwd_0", dh, xin, dx, pre_rows(0, False), tr, after=token)
        d_mod[3 * k], d_mod[3 * k + 1], d_g[2 * k] = reds[0], reds[1], reds[2]
    grad_x = dx.reshape(x.shape)

    def upd(name, parts, w, m, v):
        shape = w.shape
        r, cdim = parts.shape[1], parts.shape[2]
        return [o.reshape(shape) for o in _adamw(name, parts, w.reshape(r, cdim), m.reshape(r, cdim), v.reshape(r, cdim))]

    def landed(key, after):
        return _split_wait(f"scatter_wait_{key}", sent[key], after)

    res = {}
    w_in_s, m_in_s, v_in_s = (a.reshape(4, FF_BLK, d) for a in (w_in_t, m_in_t, v_in_t))
    w_out_s, m_out_s, v_out_s = (a.reshape(4, FF_BLK // 2, d) for a in (ffn_w_out, m_ffn_w_out, v_ffn_w_out))
    bufs_in = [lax.empty(w_in_s.shape, F32) for _ in range(4)]
    bufs_out = [lax.empty(w_out_s.shape, F32) for _ in range(4)]
    for key, k in (("12", 3), ("mla", None), ("10", 2), ("02", 1)):
        parts = landed(key, grad_x)
        if k is None:
            res["mla_w_in"] = upd("adam_mla_w_in", parts[0], mla_w_in, m_mla_w_in, v_mla_w_in)
            res["mla_w_uq"] = upd("adam_mla_w_uq", parts[1], mla_w_uq, m_mla_w_uq, v_mla_w_uq)
            res["mla_w_o"] = upd("adam_mla_w_o", parts[2], mla_w_o, m_mla_w_o, v_mla_w_o)
            uk_all, uv_all = _split_wait("gather_wait_ukv", state_ukv, grad_x)
            res["mla_w_uk"] = upd("adam_mla_w_uk", uk_all, mla_w_uk, m_mla_w_uk, v_mla_w_uk)
            res["mla_w_uv"] = upd("adam_mla_w_uv", uv_all, mla_w_uv, m_mla_w_uv, v_mla_w_uv)
            continue
        bufs_in = _adamw_slab(f"adam_ffn_w_in_{key}", parts[0], w_in_s, m_in_s, v_in_s, bufs_in, k)
        bufs_out = _adamw_slab(f"adam_ffn_w_out_{key}", parts[1], w_out_s, m_out_s, v_out_s, bufs_out, k)

    dw4, dpscale, dpb = pool_grads
    small_g = jnp.concatenate(d_mod + d_g + [dpb, dpscale, mla_grads["q_norm"], mla_grads["kv_norm"], loss_part], axis=1)
    done = [bufs_in[0], bufs_out[0], res["mla_w_o"][0], res["mla_w_uv"][0]]
    (small_g_all,) = _all_gather("gather_small_grads", [small_g], after=sum(a.reshape(-1)[:1] for a in done))
    small_g_all = small_g_all.reshape(N_DEV, -1)
    n_m = 2 * n_mod * d
    d_mod_all = small_g_all[:, :n_m].reshape(N_DEV, 2, n_mod * d)
    rest = small_g_all[:, n_m:]
    p_norm_g = lax.dynamic_slice_in_dim(rest[:, :12 * d].reshape(N_DEV, 12, d), me * (d // N_DEV), d // N_DEV, axis=2)
    p_pool_b = lax.dynamic_slice_in_dim(rest[:, 12 * d:13 * d].reshape(N_DEV, 4, POOL_GROUP), me * POOL_SHARD, POOL_SHARD, axis=2)
    p_pool_scale = rest[:, 13 * d:14 * d].reshape(N_DEV, 1, d)
    p_q_norm = lax.dynamic_slice_in_dim(rest[:, 14 * d:14 * d + Q_LORA], me * Q_SHARD, Q_SHARD, axis=1).reshape(N_DEV, 1, Q_SHARD)
    p_kv_norm = rest[:, 14 * d + Q_LORA:14 * d + Q_LORA + KV_LORA].reshape(N_DEV, 1, KV_LORA)
    loss = jnp.sum(rest[:, -1])

    d_mod_loc = lax.dynamic_slice_in_dim(d_mod_all, me * mod_blk, mod_blk, axis=2).transpose(1, 0, 2)
    k_pad = 128
    sc_t = jnp.pad(sc_all.T, ((0, 0), (0, k_pad - N_DEV)))
    d_ada_w = _matmul("ada_dw", sc_t, jnp.pad(d_mod_loc, ((0, 0), (0, k_pad - N_DEV), (0, 0))), a_blk=(d, k_pad),
                      a_map=lambda i, k: (0, 0), b_blk=(None, k_pad, mod_blk), b_map=lambda i, k: (i, 0, 0), o_shape=(2, d, mod_blk),
                      o_blk=(None, d, mod_blk), o_map=lambda i, k: (i, 0, 0), grid=(2, 1), contract=NN, out_dtype=F32)
    res["ada_w"] = upd("adam_ada_w", d_ada_w.reshape(1, 2 * d, mod_blk), ada_w, m_ada_w, v_ada_w)
    res["ada_b"] = upd("adam_ada_b", d_mod_all.reshape(N_DEV, 2, n_mod * d), ada_b, m_ada_b, v_ada_b)
    res["norm_g"] = upd("adam_norm_g", p_norm_g, norm_g, m_norm_g, v_norm_g)
    res["pool_b"] = upd("adam_pool_b", p_pool_b, pool_b, m_pool_b, v_pool_b)
    res["pool_scale"] = upd("adam_pool_scale", p_pool_scale, pool_scale, m_pool_scale, v_pool_scale)
    res["mla_q_norm"] = upd("adam_mla_q_norm", p_q_norm, mla_q_norm, m_mla_q_norm, v_mla_q_norm)
    res["mla_kv_norm"] = upd("adam_mla_kv_norm", p_kv_norm, mla_kv_norm, m_mla_kv_norm, v_mla_kv_norm)

    p_out, p_pool_w = landed("00_out", res["ada_w"][1])
    bufs_out = _adamw_slab("adam_ffn_w_out_00", p_out, w_out_s, m_out_s, v_out_s, bufs_out, 0)
    res["pool_w"] = upd("adam_pool_w", p_pool_w, pool_w, m_pool_w, v_pool_w)
    (p_in,) = landed("00_in", res["pool_w"][1])
    bufs_in = _adamw_slab("adam_ffn_w_in_00", p_in, w_in_s, m_in_s, v_in_s, bufs_in, 0)
    res["ffn_w_in"] = [jnp.swapaxes(b.reshape(w_in_t.shape), 2, 3) for b in bufs_in]
    res["ffn_w_out"] = [b.reshape(ffn_w_out.shape) for b in bufs_out]

    order = ["ada_w", "ada_b", "norm_g", "ffn_w_in", "ffn_w_out", "pool_w", "pool_b", "pool_scale", "mla_w_in", "mla_q_norm",
             "mla_kv_norm", "mla_w_uq", "mla_w_uk", "mla_w_uv", "mla_w_o"]
    outs = [loss, grad_x]
    for j in range(4):
        outs += [res[name][j] for name in order]
    return tuple(outs)
```

```python
import jax
import jax.numpy as jnp
from jax import lax
from jax.experimental import pallas as pl
from jax.experimental.pallas import tpu as pltpu

F32 = jnp.float32
BF16 = jnp.bfloat16
N_DEV = 8
MESH = pl.DeviceIdType.MESH

D_MODEL = 1024
N_HEADS = 16
QK_NOPE = 64
QK_ROPE = 32
V_HEAD = 64
Q_LORA = 256
KV_LORA = 128
LAT_PAD = 512
QK_PAD = 256
ONE_COL = 160
D_FF = 2816
FF_BLK = 2 * D_FF // N_DEV
POOL_WINDOWS = (2, 4, 8, 16)
POOL_GROUP = 256
POOL_SHARD = POOL_GROUP // N_DEV
Q_SHARD = Q_LORA // N_DEV
ROPE_THETA = 10000.0
EPS = 1e-6
ATTN_SCALE = (QK_NOPE + QK_ROPE) ** -0.5
LOG2_E = 1.4426950408889634
ADAM_LR, ADAM_B1, ADAM_B2, ADAM_EPS, ADAM_WD, ADAM_STEP = 0.001, 0.9, 0.999, 1e-08, 0.01, 10
VMEM_LIMIT = 56 * 1024 * 1024

NN = ((1,), (0,))
NT = ((1,), (1,))
TN = ((0,), (0,))


def _params(**kw):
    return pltpu.CompilerParams(vmem_limit_bytes=VMEM_LIMIT, **kw)


def _dot(a, b, contract):
    return lax.dot_general(a, b, (contract, ((), ())), preferred_element_type=F32)


def _matmul(name, a, b, *, a_blk, a_map, b_blk, b_map, o_shape, o_blk, o_map, grid, contract, out_dtype,
            bias=None, bias_blk=None, bias_map=None, after=None):
    n_k = grid[-1]
    k_axis = len(grid) - 1
    acc_shape = tuple(d for d in o_blk if d is not None)

    def body(*refs):
        a_ref, b_ref = refs[:2]
        bias_ref = refs[2] if bias is not None else None
        if n_k == 1:
            r = _dot(a_ref[...].astype(BF16), b_ref[...].astype(BF16), contract)
            if bias is not None:
                r = r + bias_ref[...]
            refs[-1][...] = r.astype(refs[-1].dtype)
            return
        o_ref, acc = refs[-2:]
        k = pl.program_id(k_axis)

        @pl.when(k == 0)
        def _():
            acc[...] = jnp.zeros_like(acc)

        acc[...] += _dot(a_ref[...].astype(BF16), b_ref[...].astype(BF16), contract)

        @pl.when(k == n_k - 1)
        def _():
            r = acc[...]
            if bias is not None:
                r = r + bias_ref[...]
            o_ref[...] = r.astype(o_ref.dtype)

    in_specs = [pl.BlockSpec(a_blk, a_map), pl.BlockSpec(b_blk, b_map)]
    args = [a, b]
    if bias is not None:
        in_specs.append(pl.BlockSpec(bias_blk, bias_map))
        args.append(bias)
    if after is not None:
        in_specs.append(pl.BlockSpec(memory_space=pl.ANY))
        args.append(after)
    return pl.pallas_call(
        body, name=name, grid=grid, in_specs=in_specs, out_specs=pl.BlockSpec(o_blk, o_map),
        out_shape=jax.ShapeDtypeStruct(o_shape, out_dtype), scratch_shapes=[pltpu.VMEM(acc_shape, F32)] if n_k > 1 else [],
        compiler_params=_params(),
    )(*args)


def _rowmap(name, fn, ins, outs, reds, grid, after=None):
    n_in, n_out, n_red = len(ins), len(outs), len(reds)
    extra = [] if after is None else [after]

    def body(*refs):
        in_refs = refs[:n_in]
        out_refs = refs[n_in + len(extra):n_in + len(extra) + n_out]
        red_refs = refs[n_in + len(extra) + n_out:]
        out_vals, red_vals = fn(*[r[...] for r in in_refs])
        for r, v in zip(out_refs, out_vals):
            r[...] = v.astype(r.dtype)
        if n_red:
            first = pl.program_id(0) == 0
            for ax in range(1, len(grid)):
                first = jnp.logical_and(first, pl.program_id(ax) == 0)

            @pl.when(first)
            def _():
                for r in red_refs:
                    r[...] = jnp.zeros_like(r)

            for r, v in zip(red_refs, red_vals):
                r[...] += v

    res = pl.pallas_call(
        body, name=name, grid=grid,
        in_specs=[pl.BlockSpec(blk, imap) for _, blk, imap in ins] + [pl.BlockSpec(memory_space=pl.ANY)] * len(extra),
        out_specs=[pl.BlockSpec(blk, imap) for _, blk, imap in list(outs) + list(reds)],
        out_shape=[sds for sds, _, _ in list(outs) + list(reds)],
        compiler_params=_params(),
    )(*[a for a, _, _ in ins], *extra)
    return res[:n_out], res[n_out:]


def _sds(shape, dtype):
    return jax.ShapeDtypeStruct(shape, dtype)


def _tile(a, tm):
    return (a, (tm, a.shape[1]), lambda i: (i, 0))


def _row(a):
    return (a, (1, a.shape[1]), lambda i: (0, 0))


def _otile(n, c, dtype, tm):
    return (_sds((n, c), dtype), (tm, c), lambda i: (i, 0))


def _ored(c):
    return (_sds((1, c), F32), (1, c), lambda i: (0, 0))


def _colsum(v):
    return jnp.sum(v, axis=0, keepdims=True)


def _rstd(v):
    return lax.rsqrt(jnp.mean(v * v, axis=-1, keepdims=True) + EPS)


def _pre(xv, g, sc, sh):
    return xv * _rstd(xv) * g * (1.0 + sc) + sh


def _post(xv, uv, g, gt, weight):
    return xv + weight * (1.0 + gt) * (uv * _rstd(uv) * g)


def _post_bwd(dv, uv, g, gt, weight):
    r = _rstd(uv)
    un = uv * r
    dy = dv * (weight * (1.0 + gt))
    a = dy * g
    du = r * (a - un * jnp.mean(a * un, axis=-1, keepdims=True))
    return du, (_colsum(dv * (weight * (un * g))), _colsum(dy * un))


def _pre_bwd(dhv, xv, dv, g, sc):
    dhv = dhv.astype(F32)
    r = _rstd(xv)
    xn = xv * r
    b = dhv * (g * (1.0 + sc))
    dx = dv + r * (b - xn * jnp.mean(b * xn, axis=-1, keepdims=True))
    return dx, (_colsum(dhv), _colsum(dhv * (xn * g)), _colsum(dhv * ((1.0 + sc) * xn)))


def _rowk(rows, k):
    return (rows, (None, 1, rows.shape[2]), lambda i: (k, 0, 0))


def _prenorm(name, x, pre, out_dtype, tm):
    n, d = x.shape
    (h,), _ = _rowmap(name, lambda xv, g, sc, sh: ((_pre(xv, g, sc, sh),), ()), [_tile(x, tm), *pre], [_otile(n, d, out_dtype, tm)],
                      [], (n // tm,))
    return h


def _norm_link(name, x, u, post, weight, pre, out_dtype, tm):
    n, d = x.shape

    def fn(xv, uv, g, gt, g2, sc, sh):
        xn = _post(xv, uv, g, gt, weight)
        return (xn, _pre(xn, g2, sc, sh)), ()

    (xn, h), _ = _rowmap(name, fn, [_tile(x, tm), _tile(u, tm), *post, *pre], [_otile(n, d, F32, tm), _otile(n, d, out_dtype, tm)],
                         [], (n // tm,))
    return xn, h


def _norm_loss(name, x, u, target, post, weight, tm):
    n, d = x.shape

    def fn(xv, uv, tv, g, gt):
        e = _post(xv, uv, g, gt, weight) - tv
        dv = e * (1.0 / d)
        du, reds = _post_bwd(dv, uv, g, gt, weight)
        return (dv, du), (_colsum(e * e), *reds)

    (dx, du), reds = _rowmap(name, fn, [_tile(x, tm), _tile(u, tm), _tile(target, tm), *post],
                             [_otile(n, d, F32, tm), _otile(n, d, BF16, tm)], [_ored(d)] * 3, (n // tm,))
    return dx, du, reds


def _norm_link_bwd(name, dh, x, dout, u_prev, pre, post_prev, weight_prev, out_dtype, tm, after=None):
    n, d = x.shape

    def fn(dhv, xv, dv, uv, g, sc, g2, gt):
        dx, reds = _pre_bwd(dhv, xv, dv, g, sc)
        du, reds_prev = _post_bwd(dx, uv, g2, gt, weight_prev)
        return (dx, du), (*reds, *reds_prev)

    (dx, du), reds = _rowmap(name, fn, [_tile(dh, tm), _tile(x, tm), _tile(dout, tm), _tile(u_prev, tm), *pre, *post_prev],
                             [_otile(n, d, F32, tm), _otile(n, d, out_dtype, tm)], [_ored(d)] * 5, (n // tm,), after=after)
    return dx, du, reds


def _prenorm_bwd(name, dh, x, dout, pre, tm, after=None):
    n, d = x.shape

    def fn(dhv, xv, dv, g, sc):
        dx, reds = _pre_bwd(dhv, xv, dv, g, sc)
        return (dx,), reds

    (dx,), reds = _rowmap(name, fn, [_tile(dh, tm), _tile(x, tm), _tile(dout, tm), *pre], [_otile(n, d, F32, tm)], [_ored(d)] * 3,
                          (n // tm,), after=after)
    return dx, reds


def _ffn_fwd(tag, h, w_in8, w_out4, tm, sub, after=None):
    s, d = h.shape

    extra = [] if after is None else [after]

    def body(h_ref, wg_ref, wu_ref, wo_ref, *rest):
        u_ref, gu_ref = rest[-2:]

        @pl.when(pl.program_id(1) == 0)
        def _():
            u_ref[...] = jnp.zeros_like(u_ref)

        for r in range(tm // sub):
            rows = pl.ds(r * sub, sub)
            hv = h_ref[rows, :]
            gate = _dot(hv, wg_ref[...], NT)
            up = _dot(hv, wu_ref[...], NT)
            gu_ref[0, rows, :] = gate.astype(BF16)
            gu_ref[1, rows, :] = up.astype(BF16)
            u_ref[rows, :] += _dot((gate * jax.nn.sigmoid(gate) * up).astype(BF16), wo_ref[...], NN)

    w_blk = (None, FF_BLK, d)
    return pl.pallas_call(
        body, name=f"ffn_fwd_{tag}", grid=(s // tm, 4),
        in_specs=[pl.BlockSpec((tm, d), lambda i, j: (i, 0)), pl.BlockSpec(w_blk, lambda i, j: (j, 0, 0)),
                  pl.BlockSpec(w_blk, lambda i, j: (j + 4, 0, 0)), pl.BlockSpec((None, FF_BLK, d), lambda i, j: (j, 0, 0))]
        + [pl.BlockSpec(memory_space=pl.ANY)] * len(extra),
        out_specs=[pl.BlockSpec((tm, d), lambda i, j: (i, 0)), pl.BlockSpec((2, None, tm, FF_BLK), lambda i, j: (0, j, i, 0))],
        out_shape=[_sds((s, d), F32), _sds((2, 4, s, FF_BLK), BF16)], compiler_params=_params(),
    )(h, w_in8, w_in8, w_out4, *extra)


def _ffn_bwd_act(tag, du, gu, w_in8, w_out4, tm, sub):
    s, d = du.shape

    def body(du_ref, gu_ref, wg_ref, wu_ref, wo_ref, dh_ref, dgu_ref, act_ref):
        @pl.when(pl.program_id(1) == 0)
        def _():
            dh_ref[...] = jnp.zeros_like(dh_ref)

        n_sub = tm // sub
        dact_next = _dot(du_ref[pl.ds(0, sub), :], wo_ref[...], NT)
        for r in range(n_sub):
            rows = pl.ds(r * sub, sub)
            dact = dact_next
            if r + 1 < n_sub:
                dact_next = _dot(du_ref[pl.ds((r + 1) * sub, sub), :], wo_ref[...], NT)
            gate, up = gu_ref[0, rows, :].astype(F32), gu_ref[1, rows, :].astype(F32)
            sg = jax.nn.sigmoid(gate)
            silu = gate * sg
            dg = dact * up * (sg * (1.0 + gate * (1.0 - sg)))
            dup = dact * silu
            dgu_ref[0, :, rows] = dg.T.astype(BF16)
            dgu_ref[1, :, rows] = dup.T.astype(BF16)
            act_ref[:, rows] = (silu * up).T.astype(BF16)
            dh_ref[rows, :] += _dot(dg.astype(BF16), wg_ref[...], NN) + _dot(dup.astype(BF16), wu_ref[...], NN)

    w_blk = (None, FF_BLK, d)
    dh, dgu_t, act_t = pl.pallas_call(
        body, name=f"ffn_bwd_{tag}", grid=(s // tm, 4),
        in_specs=[pl.BlockSpec((tm, d), lambda i, j: (i, 0)), pl.BlockSpec((2, None, tm, FF_BLK), lambda i, j: (0, j, i, 0)),
                  pl.BlockSpec(w_blk, lambda i, j: (j, 0, 0)), pl.BlockSpec(w_blk, lambda i, j: (j + 4, 0, 0)),
                  pl.BlockSpec((None, FF_BLK, d), lambda i, j: (j, 0, 0))],
        out_specs=[pl.BlockSpec((tm, d), lambda i, j: (i, 0)), pl.BlockSpec((2, None, FF_BLK, tm), lambda i, j: (0, j, 0, i)),
                   pl.BlockSpec((None, FF_BLK, tm), lambda i, j: (j, 0, i))],
        out_shape=[_sds((s, d), F32), _sds((2, 4, FF_BLK, s), BF16), _sds((4, FF_BLK, s), BF16)], compiler_params=_params(),
    )(du, gu, w_in8, w_in8, w_out4)
    return dh, dgu_t.reshape(8, FF_BLK, s), act_t


def _ffn_dw(name, lhs_t, rhs, tk, after=None):
    n_g, _, s = lhs_t.shape
    d = rhs.shape[1]
    return _matmul(name, lhs_t, rhs, a_blk=(None, FF_BLK, tk), a_map=lambda g, k: (g, 0, k), b_blk=(tk, d), b_map=lambda g, k: (k, 0),
                   o_shape=(n_g, FF_BLK, d), o_blk=(None, FF_BLK, d), o_map=lambda g, k: (g, 0, 0), grid=(n_g, s // tk),
                   contract=NN, out_dtype=BF16, after=after)


def _window_sum(x, window, transpose):
    s = x.shape[0]
    t = lax.broadcasted_iota(jnp.int32, (s, 1), 0)
    half = window // 2
    cnt = jnp.minimum(t + half, s) - jnp.maximum(t - half, 0)
    inv = 1.0 / cnt.astype(F32)
    if transpose:
        x = x * inv
        offsets = range(-half + 1, half + 1)
    else:
        offsets = range(-half, half)
    acc = jnp.zeros_like(x)
    for o in offsets:
        shifted = x if o == 0 else pltpu.roll(x, (-o) % s, 0)
        valid = jnp.logical_and(t + o >= 0, t + o < s)
        acc = acc + jnp.where(valid, shifted, 0.0)
    return acc if transpose else acc * inv


def _pool_mix(name, x, transpose, out_dtype):
    s, d = x.shape

    def body(x_ref, o_ref):
        g = pl.program_id(0)
        for gi, window in enumerate(POOL_WINDOWS):
            @pl.when(g == gi)
            def _(window=window):
                xv = x_ref[...].astype(F32)
                o_ref[...] = (_window_sum(xv, window, transpose) - xv).astype(o_ref.dtype)

    return pl.pallas_call(
        body, name=name, grid=(len(POOL_WINDOWS),), in_specs=[pl.BlockSpec((s, POOL_GROUP), lambda g: (0, g))],
        out_specs=pl.BlockSpec((s, POOL_GROUP), lambda g: (0, g)), out_shape=_sds((s, d), out_dtype), compiler_params=_params(),
    )(x)


def _pool_fwd(h, w4, bias, pscale, tm):
    s, d = h.shape
    nt = s // tm
    z = _pool_mix("pool_mix", h, False, BF16)
    v = _matmul("pool_proj", z, w4, a_blk=(tm, POOL_GROUP), a_map=lambda i, g, k: (i, g), b_blk=(None, POOL_GROUP, POOL_GROUP),
                b_map=lambda i, g, k: (g, 0, 0), o_shape=(s, d), o_blk=(tm, POOL_GROUP), o_map=lambda i, g, k: (i, g),
                grid=(nt, 4, 1), contract=NN, out_dtype=F32, bias=bias, bias_blk=(1, POOL_GROUP), bias_map=lambda i, g, k: (0, g))
    tr = min(512, s)
    (u,), _ = _rowmap("pool_scale", lambda vv, ps: ((vv * ps,), ()), [_tile(v, tr), _row(pscale)], [_otile(s, d, F32, tr)], [],
                      (s // tr,))
    return u, z, v


def _pool_bwd(du, z, v, w4, pscale, tm):
    s, d = du.shape
    nt = s // tm

    def fn(duv, vv, ps):
        dv = duv * ps
        return (dv,), (_colsum(duv * vv), _colsum(dv))

    tr = min(512, s)
    (dv,), reds = _rowmap("pool_dscale", fn, [_tile(du, tr), _tile(v, tr), _row(pscale)], [_otile(s, d, BF16, tr)],
                          [_ored(d), _ored(d)], (s // tr,))
    dw4 = _matmul("pool_dw", z, dv, a_blk=(tm, POOL_GROUP), a_map=lambda g, k: (k, g), b_blk=(tm, POOL_GROUP),
                  b_map=lambda g, k: (k, g), o_shape=(4, POOL_GROUP, POOL_GROUP), o_blk=(None, POOL_GROUP, POOL_GROUP),
                  o_map=lambda g, k: (g, 0, 0), grid=(4, nt), contract=TN, out_dtype=F32)
    dz = _matmul("pool_dz", dv, w4, a_blk=(tm, POOL_GROUP), a_map=lambda i, g, k: (i, g), b_blk=(None, POOL_GROUP, POOL_GROUP),
                 b_map=lambda i, g, k: (g, 0, 0), o_shape=(s, d), o_blk=(tm, POOL_GROUP), o_map=lambda i, g, k: (i, g),
                 grid=(nt, 4, 1), contract=NT, out_dtype=F32)
    dh = _pool_mix("pool_mix_t", dz, True, F32)
    return dh, dw4, reds


def _lane(shape):
    return lax.broadcasted_iota(jnp.int32, shape, 1)


def _rope_swap(v, transpose):
    half = QK_ROPE // 2
    lane = _lane(v.shape)
    up = pltpu.roll(v, v.shape[1] - half, 1)
    down = pltpu.roll(v, half, 1)
    if transpose:
        return jnp.where(lane < half, up, jnp.where(lane < QK_ROPE, -down, 0.0))
    return jnp.where(lane < half, -up, jnp.where(lane < QK_ROPE, down, 0.0))


def _rope(v, cos, sin):
    return v * cos + _rope_swap(v, False) * sin


def _rope_t(g, cos, sin):
    return g * cos + _rope_swap(g * sin, True)


def _mla_mid(lat, q_norm, kv_norm, cos_k, sin_k, tm):
    s = lat.shape[0]

    def fn(lv, qn, kn, cs, sn):
        cq = lv[:, :Q_LORA]
        ckv = lv[:, Q_LORA:Q_LORA + KV_LORA]
        kr = lv[:, Q_LORA + KV_LORA:]
        cq = cq * _rstd(cq) * qn
        ckv = ckv * _rstd(ckv) * kn
        k_rope = jnp.where(_lane(kr.shape) == ONE_COL - KV_LORA, 1.0, _rope(kr, cs, sn))
        return (cq, jnp.concatenate([ckv, k_rope], axis=1)), ()

    (cq, kcat), _ = _rowmap("mla_mid", fn, [_tile(lat, tm), _row(q_norm), _row(kv_norm), _tile(cos_k, tm), _tile(sin_k, tm)],
                            [_otile(s, Q_LORA, BF16, tm), _otile(s, QK_PAD, BF16, tm)], [], (s // tm,))
    return cq, kcat


def _mla_mid_bwd(lat, dcq, dkcat, dv, q_norm, kv_norm, cos_k, sin_k, tm):
    s = lat.shape[0]

    def fn(lv, dq, dk, dvv, qn, kn, cs, sn):
        dk = dk * (1.0 / LOG2_E)
        cq = lv[:, :Q_LORA]
        ckv = lv[:, Q_LORA:Q_LORA + KV_LORA]
        rq, rk = _rstd(cq), _rstd(ckv)
        cqn, ckn = cq * rq, ckv * rk
        a = dq * qn
        d_cq = rq * (a - cqn * jnp.mean(a * cqn, axis=-1, keepdims=True))
        dckv = dk[:, :KV_LORA] + dvv
        a2 = dckv * kn
        d_ckv = rk * (a2 - ckn * jnp.mean(a2 * ckn, axis=-1, keepdims=True))
        d_kr = _rope_t(dk[:, KV_LORA:], cs, sn)
        return (jnp.concatenate([d_cq, d_ckv, d_kr], axis=1),), (_colsum(dq * cqn), _colsum(dckv * ckn))

    (dlat,), reds = _rowmap(
        "mla_mid_bwd", fn,
        [_tile(lat, tm), _tile(dcq, tm), _tile(dkcat, tm), _tile(dv, tm), _row(q_norm), _row(kv_norm), _tile(cos_k, tm),
         _tile(sin_k, tm)],
        [_otile(s, LAT_PAD, BF16, tm)], [_ored(Q_LORA), _ored(KV_LORA)], (s // tm,))
    return dlat, reds


def _mla_q(cq, wq, wukp, cos_k, sin_k, tm):
    s = cq.shape[0]

    def body(cq_ref, wq_ref, wuk_ref, cos_ref, sin_ref, o_ref):
        cqv, cs, sn = cq_ref[...], cos_ref[...], sin_ref[...]
        for h in range(N_HEADS):
            aq = _dot(cqv, wq_ref[h], NN)
            qlat = _dot(aq.astype(BF16), wuk_ref[h], NN)
            roped = _rope(aq[:, KV_LORA:], cs, sn)
            o_ref[h] = (jnp.concatenate([qlat[:, :KV_LORA], roped], axis=1) * (ATTN_SCALE * LOG2_E)).astype(o_ref.dtype)

    wblk = pl.BlockSpec((N_HEADS, QK_PAD, QK_PAD), lambda i: (0, 0, 0))
    tblk = pl.BlockSpec((tm, KV_LORA), lambda i: (i, 0))
    return pl.pallas_call(
        body, name="mla_q", grid=(s // tm,),
        in_specs=[pl.BlockSpec((tm, Q_LORA), lambda i: (i, 0)), wblk, wblk, tblk, tblk],
        out_specs=pl.BlockSpec((N_HEADS, tm, QK_PAD), lambda i: (0, i, 0)), out_shape=_sds((N_HEADS, s, QK_PAD), BF16),
        compiler_params=_params(),
    )(cq, wq, wukp, cos_k, sin_k)


def _mla_q_bwd(cq, wq, wukp, cos_k, sin_k, dqcat, tm):
    s = cq.shape[0]

    def body(cq_ref, wq_ref, wuk_ref, cos_ref, sin_ref, dq_ref, dcq_ref, dwq_ref, dwuk_ref):
        @pl.when(pl.program_id(0) == 0)
        def _():
            dwq_ref[...] = jnp.zeros_like(dwq_ref)
            dwuk_ref[...] = jnp.zeros_like(dwuk_ref)

        cqv, cs, sn = cq_ref[...], cos_ref[...], sin_ref[...]
        d_cq = jnp.zeros((tm, Q_LORA), F32)
        for h in range(N_HEADS):
            aq = _dot(cqv, wq_ref[h], NN).astype(BF16)
            g = dq_ref[h].astype(F32) * ATTN_SCALE
            gl, gr = g[:, :KV_LORA], g[:, KV_LORA:]
            dqlat = jnp.concatenate([gl, jnp.zeros_like(gl)], axis=1).astype(BF16)
            d_rope = _rope_t(gr, cs, sn)
            daq = _dot(dqlat, wuk_ref[h], NT) + jnp.concatenate([jnp.zeros_like(d_rope), d_rope], axis=1)
            daq_b = daq.astype(BF16)
            dwuk_ref[h] += _dot(aq, dqlat, TN)
            dwq_ref[h] += _dot(cqv, daq_b, TN)
            d_cq = d_cq + _dot(daq_b, wq_ref[h], NT)
        dcq_ref[...] = d_cq

    wblk = pl.BlockSpec((N_HEADS, QK_PAD, QK_PAD), lambda i: (0, 0, 0))
    tblk = pl.BlockSpec((tm, KV_LORA), lambda i: (i, 0))
    return pl.pallas_call(
        body, name="mla_q_bwd", grid=(s // tm,),
        in_specs=[pl.BlockSpec((tm, Q_LORA), lambda i: (i, 0)), wblk, wblk, tblk, tblk,
                  pl.BlockSpec((N_HEADS, tm, QK_PAD), lambda i: (0, i, 0))],
        out_specs=[pl.BlockSpec((tm, Q_LORA), lambda i: (i, 0)), wblk, wblk],
        out_shape=[_sds((s, Q_LORA), F32), _sds((N_HEADS, QK_PAD, QK_PAD), F32), _sds((N_HEADS, QK_PAD, QK_PAD), F32)],
        compiler_params=_params(),
    )(cq, wq, wukp, cos_k, sin_k, dqcat)


def _flash_fwd(qcat, kcat, tq, tk):
    n_h, s, _ = qcat.shape
    n_k = s // tk

    def body(q_ref, k_ref, o_ref, lse_ref):
        q = q_ref[...]
        m = jnp.full((tq, 1), -1e30, F32)
        acc = jnp.zeros((tq, QK_PAD), F32)
        for kk in range(n_k):
            k = k_ref[pl.ds(kk * tk, tk), :]
            sc = _dot(q, k, NT)
            m_new = jnp.maximum(m, jnp.max(sc, axis=1, keepdims=True))
            p = jnp.exp2(sc - m_new).astype(BF16)
            acc = jnp.exp2(m - m_new) * acc + _dot(p, k, NN)
            m = m_new
        l = jnp.sum(jnp.where(_lane(acc.shape) == ONE_COL, acc, 0.0), axis=1, keepdims=True)
        o_ref[...] = (acc[:, :KV_LORA] / l).astype(o_ref.dtype)
        lse_ref[...] = m + jnp.log2(l)

    return pl.pallas_call(
        body, name="mla_attn", grid=(n_h, s // tq),
        in_specs=[pl.BlockSpec((None, tq, QK_PAD), lambda h, i: (h, i, 0)), pl.BlockSpec((s, QK_PAD), lambda h, i: (0, 0))],
        out_specs=[pl.BlockSpec((None, tq, KV_LORA), lambda h, i: (h, i, 0)), pl.BlockSpec((None, tq, 1), lambda h, i: (h, i, 0))],
        out_shape=[_sds((n_h, s, KV_LORA), BF16), _sds((n_h, s, 1), F32)], compiler_params=_params(),
    )(qcat, kcat)


def _flash_bwd(qcat, kcat, o, do, lse, tq, tk):
    n_h, s, _ = qcat.shape
    n_k = s // tk

    def body(q_ref, k_ref, v_ref, o_ref, do_ref, lse_ref, dq_ref, dk_ref, dv_ref, dq_acc):
        h, i = pl.program_id(0), pl.program_id(1)

        @pl.when(jnp.logical_and(h == 0, i == 0))
        def _():
            dk_ref[...] = jnp.zeros_like(dk_ref)
            dv_ref[...] = jnp.zeros_like(dv_ref)

        q = q_ref[...]
        dov = do_ref[...]
        dov_t = dov.T
        lse_v = lse_ref[...]
        delta = jnp.sum(dov.astype(F32) * o_ref[...].astype(F32), axis=1, keepdims=True)
        dq_acc[...] = jnp.zeros_like(dq_acc)

        for kk in range(n_k):
            rows = pl.ds(kk * tk, tk)
            k = k_ref[rows, :]
            p = jnp.exp2(_dot(q, k, NT) - lse_v)
            dp = _dot(dov, v_ref[rows, :], NT)
            ds = (p * (dp - delta)).astype(BF16)
            dq_acc[...] += _dot(ds, k, NN)
            dv_ref[:, rows] += _dot(dov_t, p.astype(BF16), NN)
            dk_ref[rows, :] += _dot(ds, q, TN)
        dq_ref[...] = dq_acc[...].astype(dq_ref.dtype)

    qblk = pl.BlockSpec((None, tq, QK_PAD), lambda h, i: (h, i, 0))
    oblk = pl.BlockSpec((None, tq, KV_LORA), lambda h, i: (h, i, 0))
    return pl.pallas_call(
        body, name="mla_attn_bwd", grid=(n_h, s // tq),
        in_specs=[qblk, pl.BlockSpec((s, QK_PAD), lambda h, i: (0, 0)), pl.BlockSpec((s, KV_LORA), lambda h, i: (0, 0)), oblk, oblk,
                  pl.BlockSpec((None, tq, 1), lambda h, i: (h, i, 0))],
        out_specs=[qblk, pl.BlockSpec((s, QK_PAD), lambda h, i: (0, 0)), pl.BlockSpec((KV_LORA, s), lambda h, i: (0, 0))],
        out_shape=[_sds((n_h, s, QK_PAD), BF16), _sds((s, QK_PAD), F32), _sds((KV_LORA, s), F32)],
        scratch_shapes=[pltpu.VMEM((tq, QK_PAD), F32)], compiler_params=_params(),
    )(qcat, kcat, kcat, o, do, lse)


def _mla_uv(o_lat, wuv2, do, tm):
    n_h, s, _ = o_lat.shape
    d = n_h * V_HEAD
    pair = 2 * V_HEAD
    lat_blk = pl.BlockSpec((n_h, tm, KV_LORA), lambda i: (0, i, 0))
    w_blk = pl.BlockSpec((n_h, KV_LORA, pair), lambda i: (0, 0, 0))
    row_blk = pl.BlockSpec((tm, d), lambda i: (i, 0))

    if do is None:
        def body(a_ref, w_ref, o_ref):
            for p in range(n_h // 2):
                o_ref[:, p * pair:(p + 1) * pair] = (
                    _dot(a_ref[2 * p], w_ref[2 * p], NN) + _dot(a_ref[2 * p + 1], w_ref[2 * p + 1], NN)).astype(o_ref.dtype)

        return pl.pallas_call(body, name="mla_uv", grid=(s // tm,), in_specs=[lat_blk, w_blk], out_specs=row_blk,
                              out_shape=_sds((s, d), BF16), compiler_params=_params())(o_lat, wuv2)

    def body(a_ref, w_ref, do_ref, dlat_ref, dw_ref):
        @pl.when(pl.program_id(0) == 0)
        def _():
            dw_ref[...] = jnp.zeros_like(dw_ref)

        for h in range(n_h):
            dov = do_ref[:, (h // 2) * pair:(h // 2 + 1) * pair]
            dlat_ref[h] = _dot(dov, w_ref[h], NT).astype(dlat_ref.dtype)
            dw_ref[h] += _dot(a_ref[h], dov, TN)

    return pl.pallas_call(body, name="mla_uv_bwd", grid=(s // tm,), in_specs=[lat_blk, w_blk, row_blk], out_specs=[lat_blk, w_blk],
                          out_shape=[_sds((n_h, s, KV_LORA), BF16), _sds((n_h, KV_LORA, pair), F32)], compiler_params=_params(),
                          )(o_lat, wuv2, do)


def _mla_fwd(h, wts, cos_k, sin_k, tm):
    s, d = h.shape
    nt = s // tm
    lat = _matmul("mla_lat", h, wts["w_in"], a_blk=(tm, d), a_map=lambda i, k: (i, 0), b_blk=(d, LAT_PAD), b_map=lambda i, k: (0, 0),
                  o_shape=(s, LAT_PAD), o_blk=(tm, LAT_PAD), o_map=lambda i, k: (i, 0), grid=(nt, 1), contract=NN, out_dtype=F32)
    cq, kcat = _mla_mid(lat, wts["q_norm"], wts["kv_norm"], cos_k, sin_k, tm)
    qcat = _mla_q(cq, wts["wq"], wts["wukp"], cos_k, sin_k, tm)
    o_lat, lse = _flash_fwd(qcat, kcat, min(2 * tm, s), tm)
    o = _mla_uv(o_lat, wts["wuv2"], None, tm)
    u = _matmul("mla_out", o, wts["w_o"], a_blk=(tm, d), a_map=lambda i, k: (i, 0), b_blk=(d, d), b_map=lambda i, k: (0, 0),
                o_shape=(s, d), o_blk=(tm, d), o_map=lambda i, k: (i, 0), grid=(nt, 1), contract=NN, out_dtype=F32)
    return u, (lat, cq, kcat, qcat, o_lat, lse, o)


def _mla_bwd(du, h, saved, wts, cos_k, sin_k, tm):
    lat, cq, kcat, qcat, o_lat, lse, o = saved
    s, d = h.shape
    nt = s // tm
    do = _matmul("mla_do", du, wts["w_o"], a_blk=(tm, d), a_map=lambda i, k: (i, 0), b_blk=(d, d), b_map=lambda i, k: (0, 0),
                 o_shape=(s, d), o_blk=(tm, d), o_map=lambda i, k: (i, 0), grid=(nt, 1), contract=NT, out_dtype=BF16)
    dw_o = _matmul("mla_dwo", o, du, a_blk=(tm, d), a_map=lambda k: (k, 0), b_blk=(tm, d), b_map=lambda k: (k, 0),
                   o_shape=(d, d), o_blk=(d, d), o_map=lambda k: (0, 0), grid=(nt,), contract=TN, out_dtype=F32)
    do_lat, dwuv2 = _mla_uv(o_lat, wts["wuv2"], do, tm)
    dqcat, dkcat, dv_t = _flash_bwd(qcat, kcat, o_lat, do_lat, lse, min(2 * tm, s), tm)
    dv = dv_t.T
    dcq, dwq, dwukp = _mla_q_bwd(cq, wts["wq"], wts["wukp"], cos_k, sin_k, dqcat, tm)
    dlat, (dqn, dkn) = _mla_mid_bwd(lat, dcq, dkcat, dv, wts["q_norm"], wts["kv_norm"], cos_k, sin_k, tm)
    dh = _matmul("mla_dh", dlat, wts["w_in"], a_blk=(tm, LAT_PAD), a_map=lambda i, k: (i, 0), b_blk=(d, LAT_PAD),
                 b_map=lambda i, k: (0, 0), o_shape=(s, d), o_blk=(tm, d), o_map=lambda i, k: (i, 0), grid=(nt, 1), contract=NT,
                 out_dtype=F32)
    dw_in = _matmul("mla_dwin", h, dlat, a_blk=(tm, d), a_map=lambda k: (k, 0), b_blk=(tm, LAT_PAD), b_map=lambda k: (k, 0),
                    o_shape=(d, LAT_PAD), o_blk=(d, LAT_PAD), o_map=lambda k: (0, 0), grid=(nt,), contract=TN, out_dtype=F32)
    return dh, dict(w_in=dw_in, wq=dwq, wukp=dwukp, wuv2=dwuv2, w_o=dw_o, q_norm=dqn, kv_norm=dkn)


def _adamw(name, parts, w, m, v):
    n_parts, r, c = parts.shape
    tr = r
    for cand in (256, 128, 64, 32, 16, 8):
        if r > cand and r % cand == 0:
            tr = cand
            break

    def body(p_ref, w_ref, m_ref, v_ref, g_ref, d_ref, nm_ref, nv_ref):
        g = p_ref[0].astype(F32)
        for k in range(1, n_parts):
            g = g + p_ref[k].astype(F32)
        nm = ADAM_B1 * m_ref[...] + (1.0 - ADAM_B1) * g
        nv = ADAM_B2 * v_ref[...] + (1.0 - ADAM_B2) * (g * g)
        m_hat = nm / (1.0 - ADAM_B1 ** ADAM_STEP)
        v_hat = nv / (1.0 - ADAM_B2 ** ADAM_STEP)
        g_ref[...] = g
        d_ref[...] = -ADAM_LR * (m_hat / (jnp.sqrt(v_hat) + ADAM_EPS) + ADAM_WD * w_ref[...])
        nm_ref[...] = nm
        nv_ref[...] = nv

    blk = pl.BlockSpec((tr, c), lambda i: (i, 0))
    return pl.pallas_call(
        body, name=name, grid=(r // tr,), in_specs=[pl.BlockSpec((n_parts, tr, c), lambda i: (0, i, 0)), blk, blk, blk],
        out_specs=[blk] * 4, out_shape=[_sds((r, c), F32)] * 4, compiler_params=_params(),
    )(parts, w, m, v)


def _adamw_slab(name, parts, w, m, v, bufs, f):
    n_parts, r, c = parts.shape
    tr = max(t for t in range(8, 257, 8) if r % t == 0)

    def body(p_ref, w_ref, m_ref, v_ref, *rest):
        g_ref, d_ref, nm_ref, nv_ref = rest[4:]
        g = p_ref[0].astype(F32)
        for k in range(1, n_parts):
            g = g + p_ref[k].astype(F32)
        nm = ADAM_B1 * m_ref[...] + (1.0 - ADAM_B1) * g
        nv = ADAM_B2 * v_ref[...] + (1.0 - ADAM_B2) * (g * g)
        m_hat = nm / (1.0 - ADAM_B1 ** ADAM_STEP)
        v_hat = nv / (1.0 - ADAM_B2 ** ADAM_STEP)
        g_ref[...] = g
        d_ref[...] = -ADAM_LR * (m_hat / (jnp.sqrt(v_hat) + ADAM_EPS) + ADAM_WD * w_ref[...])
        nm_ref[...] = nm
        nv_ref[...] = nv

    blk = pl.BlockSpec((None, tr, c), lambda i: (f, i, 0))
    return pl.pallas_call(
        body, name=name, grid=(r // tr,),
        in_specs=[pl.BlockSpec((n_parts, tr, c), lambda i: (0, i, 0)), blk, blk, blk] + [pl.BlockSpec(memory_space=pl.ANY)] * 4,
        out_specs=[blk] * 4, out_shape=[_sds(w.shape, F32)] * 4, input_output_aliases={4 + j: j for j in range(4)},
        compiler_params=_params(),
    )(parts, w, m, v, *bufs)


def _mesh_pos():
    return lax.axis_index("x"), lax.axis_index("y"), lax.axis_index("c")


def _flip(pos, mask):
    return tuple(1 - p if (mask >> (2 - b)) & 1 else p for b, p in enumerate(pos))


def _index(pos):
    return 4 * pos[0] + 2 * pos[1] + pos[2]


def _all_gather(name, xs, after=None):
    n = len(xs)
    extra = [] if after is None else [after]

    def body(*refs):
        x_refs, o_refs = refs[:n], refs[n + len(extra):2 * n + len(extra)]
        send_sems, recv_sems, local_sems = refs[2 * n + len(extra):]
        me = _mesh_pos()
        sibling = _flip(me, 1)
        others = [_flip(me, 4), _flip(me, 2), _flip(me, 6)]

        def copy(k, j, block, to, src=None):
            dst = o_refs[k].at[_index(block)]
            return pltpu.make_async_remote_copy(
                src_ref=dst if src is None else src, dst_ref=dst, send_sem=send_sems.at[k, j], recv_sem=recv_sems.at[k, j],
                device_id=to, device_id_type=MESH)

        local = [pltpu.make_async_copy(x_refs[k], o_refs[k].at[_index(me)], local_sems.at[k]) for k in range(n)]
        for cp in local:
            cp.start()
        first = []
        for k in range(n):
            first.append(copy(k, 0, me, sibling, src=x_refs[k]))
            first += [copy(k, 1 + j, me, other, src=x_refs[k]) for j, other in enumerate(others)]
        for cp in first:
            cp.start()
        passed = []
        for j, other in enumerate(others):
            for k in range(n):
                copy(k, 1 + j, other, me).wait_recv()
                cp = copy(k, 4 + j, other, sibling)
                cp.start()
                passed.append(cp)
        for k in range(n):
            copy(k, 0, sibling, me).wait_recv()
        for j, other in enumerate(others):
            for k in range(n):
                copy(k, 4 + j, _flip(other, 1), me).wait_recv()
        for cp in first + passed:
            cp.wait_send()
        for cp in local:
            cp.wait()

    any_spec = pl.BlockSpec(memory_space=pl.ANY)
    return pl.pallas_call(
        body, name=name, in_specs=[any_spec] * (n + len(extra)), out_specs=[any_spec] * n,
        out_shape=[_sds((N_DEV,) + x.shape, x.dtype) for x in xs],
        scratch_shapes=[pltpu.SemaphoreType.DMA((n, 7)), pltpu.SemaphoreType.DMA((n, 7)), pltpu.SemaphoreType.DMA((n,))],
    )(*xs, *extra)


_MASKS = {
    "gather": tuple(range(1, N_DEV)),
    "scatter": tuple(range(1, N_DEV)),
    "own": (1, 4, 2, 6),
    "pass": (4, 2, 6),
}


def _split_copies(kind, outgoing, x_refs, land_refs, send_sems, recv_sems):
    me = _mesh_pos()
    masks = _MASKS[kind]
    copies = []
    for k, (x_ref, land_ref) in enumerate(zip(x_refs, land_refs)):
        for j, mask in enumerate(masks):
            if kind == "pass":
                peer = _flip(me, 1)
                src = land_ref.at[_index(_flip(me, mask))]
                dst = land_ref.at[_index(_flip(me, mask if outgoing else mask | 1))]
            else:
                peer = _flip(me, mask)
                src = x_ref.at[_index(peer)] if kind == "scatter" else x_ref
                dst = land_ref.at[_index(me if outgoing else peer)]
            sem = k * len(masks) + j
            copies.append(pltpu.make_async_remote_copy(src_ref=src, dst_ref=dst, send_sem=send_sems.at[sem], recv_sem=recv_sems.at[sem],
                                                       device_id=peer, device_id_type=MESH))
    return copies


_HBM_SPEC = pl.BlockSpec(memory_space=pltpu.HBM)
_SEM_SPEC = pl.BlockSpec(memory_space=pltpu.SEMAPHORE)
_EFFECT = pltpu.SideEffectType.DATAFLOW_SIDE_EFFECTING


def _split_start(name, kind, xs, after=None):
    n = len(xs)
    n_sem = n * len(_MASKS[kind])
    extra = [] if after is None else [after]
    lands = [lax.empty(x.shape if kind == "scatter" else (N_DEV,) + x.shape, x.dtype) for x in xs]

    def body(*refs):
        x_refs, land_refs = refs[:n], refs[n:2 * n]
        send_sems, recv_sems = refs[2 * n + len(extra)], refs[2 * n + len(extra) + 1]
        token = refs[-1]
        for cp in _split_copies(kind, True, x_refs, land_refs, send_sems, recv_sems):
            cp.start()
        token[...] = jnp.zeros_like(token)

    hbm = [pltpu.HBM(a.shape, a.dtype) for a in list(xs) + lands]
    res = pl.pallas_call(
        body, name=name,
        out_shape=[pltpu.SemaphoreType.DMA((n_sem,)), pltpu.SemaphoreType.DMA((n_sem,))] + hbm + [_sds((8, 128), F32)],
        in_specs=[_HBM_SPEC] * (2 * n) + [pl.BlockSpec(memory_space=pl.ANY)] * len(extra),
        out_specs=[_SEM_SPEC, _SEM_SPEC] + [_HBM_SPEC] * (2 * n) + [pl.BlockSpec(memory_space=pltpu.VMEM)],
        input_output_aliases={j: 2 + j for j in range(2 * n)}, compiler_params=pltpu.CompilerParams(has_side_effects=_EFFECT),
    )(*[pltpu.with_memory_space_constraint(a, pltpu.HBM) for a in list(xs) + lands], *extra)
    return (kind, n, res[0], res[1], res[2:2 + 2 * n]), res[-1]


def _split_pass(name, state, after):
    kind, n, send_sems_in, recv_sems_in, thru = state
    n_sem = n * len(_MASKS["pass"])

    def body(*refs):
        x_refs, land_refs = refs[:n], refs[n:2 * n]
        send_sems, recv_sems = refs[2 * n], refs[2 * n + 1]
        next_send, next_recv, token = refs[-3:]
        for cp in _split_copies(kind, True, x_refs, land_refs, send_sems, recv_sems):
            cp.wait_send()
        for cp in _split_copies(kind, False, x_refs, land_refs, send_sems, recv_sems):
            cp.wait_recv()
        for cp in _split_copies("pass", True, land_refs, land_refs, next_send, next_recv):
            cp.start()
        token[...] = jnp.zeros_like(token)

    res = pl.pallas_call(
        body, name=name,
        out_shape=[pltpu.HBM(a.shape, a.dtype) for a in thru] + [pltpu.SemaphoreType.DMA((n_sem,)), pltpu.SemaphoreType.DMA((n_sem,)),
                                                                 _sds((8, 128), F32)],
        in_specs=[_HBM_SPEC] * (2 * n) + [_SEM_SPEC, _SEM_SPEC, pl.BlockSpec(memory_space=pl.ANY)],
        out_specs=[_HBM_SPEC] * (2 * n) + [_SEM_SPEC, _SEM_SPEC, pl.BlockSpec(memory_space=pltpu.VMEM)],
        input_output_aliases={j: j for j in range(2 * n)}, compiler_params=pltpu.CompilerParams(has_side_effects=_EFFECT),
    )(*thru, send_sems_in, recv_sems_in, after)
    return ("pass", n, res[2 * n], res[2 * n + 1], res[:2 * n]), res[-1]


def _split_wait(name, state, after):
    kind, n, send_sems_in, recv_sems_in, thru = state

    def body(*refs):
        x_refs, land_refs = refs[:n], refs[n:2 * n]
        send_sems, recv_sems = refs[2 * n], refs[2 * n + 1]
        for cp in _split_copies(kind, True, x_refs, land_refs, send_sems, recv_sems):
            cp.wait_send()
        for cp in _split_copies(kind, False, x_refs, land_refs, send_sems, recv_sems):
            cp.wait_recv()

    res = pl.pallas_call(
        body, name=name, out_shape=[pltpu.HBM(a.shape, a.dtype) for a in thru],
        in_specs=[_HBM_SPEC] * (2 * n) + [_SEM_SPEC, _SEM_SPEC, pl.BlockSpec(memory_space=pl.ANY)], out_specs=[_HBM_SPEC] * (2 * n),
        input_output_aliases={j: j for j in range(2 * n)}, compiler_params=pltpu.CompilerParams(has_side_effects=_EFFECT),
    )(*thru, send_sems_in, recv_sems_in, after)
    me = _index(_mesh_pos())
    out = []
    for x, land in zip(res[:n], res[n:]):
        own = lax.dynamic_index_in_dim(x, me, 0, keepdims=False) if kind == "scatter" else x
        out.append(lax.dynamic_update_slice(land, own[None], (me,) + (0,) * own.ndim))
    return out


def _rope_tables(s):
    inv = 1.0 / (ROPE_THETA ** (jnp.arange(0, QK_ROPE, 2, dtype=F32) / QK_ROPE))
    ang = jnp.arange(s, dtype=F32)[:, None] * inv[None, :]
    pad = jnp.zeros((s, KV_LORA - QK_ROPE), F32)
    return (jnp.concatenate([jnp.cos(ang), jnp.cos(ang), pad], axis=1), jnp.concatenate([jnp.sin(ang), jnp.sin(ang), pad], axis=1))


def kernel(x, c, ada_w, ada_b, norm_g, ffn_w_in, ffn_w_out, pool_w, pool_b, pool_scale, mla_w_in, mla_q_norm, mla_kv_norm, mla_w_uq, mla_w_uk, mla_w_uv, mla_w_o, loss_target, m_ada_w, m_ada_b, m_norm_g, m_ffn_w_in, m_ffn_w_out, m_pool_w, m_pool_b, m_pool_scale, m_mla_w_in, m_mla_q_norm, m_mla_kv_norm, m_mla_w_uq, m_mla_w_uk, m_mla_w_uv, m_mla_w_o, v_ada_w, v_ada_b, v_norm_g, v_ffn_w_in, v_ffn_w_out, v_pool_w, v_pool_b, v_pool_scale, v_mla_w_in, v_mla_q_norm, v_mla_kv_norm, v_mla_w_uq, v_mla_w_uk, v_mla_w_uv, v_mla_w_o):
    s, d = x.shape[1], x.shape[2]
    tm = min(512, s)
    tr = min(512, s)
    tf = min(1024, s)
    tw = min(2048, s)
    me = 4 * lax.axis_index("x") + 2 * lax.axis_index("y") + lax.axis_index("c")
    x0 = x.reshape(s, d)
    target = loss_target.reshape(s, d)
    n_mod = ada_w.shape[2] * N_DEV // d
    mod_blk = ada_w.shape[2]

    small = jnp.concatenate([c.reshape(-1), norm_g.reshape(-1), pool_b.reshape(-1), mla_q_norm.reshape(-1)]).reshape(1, -1)
    w_in_t, m_in_t, v_in_t = (jnp.swapaxes(a, 2, 3) for a in (ffn_w_in, m_ffn_w_in, v_ffn_w_in))
    w_in_loc = [w_in_t[i, f].astype(BF16) for i in range(2) for f in range(2)]
    w_out_loc = [ffn_w_out[i, f].astype(BF16) for i in range(2) for f in range(2)]
    (small_all,) = _all_gather("gather_small", [small])
    small_all = small_all.reshape(N_DEV, -1)
    c_all = small_all[:, :d]
    off = d
    g_all = small_all[:, off:off + 12 * (d // N_DEV)].reshape(N_DEV, 2, 6, d // N_DEV).transpose(1, 2, 0, 3).reshape(2, 6, d)
    off += 12 * (d // N_DEV)
    pool_b_all = small_all[:, off:off + 4 * POOL_SHARD].reshape(N_DEV, 4, POOL_SHARD).transpose(1, 0, 2).reshape(1, d)
    off += 4 * POOL_SHARD
    q_norm_all = small_all[:, off:off + Q_SHARD].reshape(1, Q_LORA)
    kv_norm_row = mla_kv_norm.reshape(1, KV_LORA)
    pscale_row = pool_scale.reshape(1, d)

    even = (jnp.arange(N_HEADS) % 2 == 0)[:, None, None]
    cos_k, sin_k = _rope_tables(s)

    def mla_weights(mla_w_in_all, mla_w_uq_all, mla_w_o_all):
        uq = mla_w_uq_all.reshape(Q_LORA, N_HEADS, QK_NOPE + QK_ROPE).transpose(1, 0, 2)
        zq = jnp.zeros((N_HEADS, Q_LORA, QK_NOPE), BF16)
        wq = jnp.concatenate(
            [uq[:, :, :QK_NOPE], zq, uq[:, :, QK_NOPE:], jnp.zeros((N_HEADS, Q_LORA, QK_PAD - KV_LORA - QK_ROPE), BF16)], axis=2)
        wukp = jnp.pad(mla_w_uk[0].transpose(1, 2, 0).astype(BF16), ((0, 0), (0, QK_PAD - QK_NOPE), (0, QK_PAD - KV_LORA)))
        uv = mla_w_uv[0].transpose(1, 0, 2).astype(BF16)
        wuv2 = jnp.where(even, jnp.concatenate([uv, jnp.zeros_like(uv)], axis=2), jnp.concatenate([jnp.zeros_like(uv), uv], axis=2))
        return dict(w_in=jnp.pad(mla_w_in_all.reshape(d, -1), ((0, 0), (0, LAT_PAD - mla_w_in.shape[2]))), wq=wq, wukp=wukp,
                    wuv2=wuv2, w_o=mla_w_o_all.reshape(d, d), q_norm=q_norm_all, kv_norm=kv_norm_row)

    (sc_all,), _ = _rowmap("ada_silu", lambda cv: ((cv * jax.nn.sigmoid(cv),), ()), [(c_all, (N_DEV, d), lambda i: (0, 0))],
                           [(_sds((N_DEV, d), F32), (N_DEV, d), lambda i: (0, 0))], [], (1,))
    ada_b_loc = lax.dynamic_slice_in_dim(ada_b, me * mod_blk, mod_blk, axis=1).reshape(2, 1, mod_blk)
    m_pad = 2 * N_DEV
    modp = _matmul("ada_mod", jnp.pad(sc_all, ((0, m_pad - N_DEV), (0, 0))), ada_w, a_blk=(m_pad, d), a_map=lambda i, k: (0, 0),
                   b_blk=(None, d, mod_blk), b_map=lambda i, k: (i, 0, 0), o_shape=(2, m_pad, mod_blk), o_blk=(None, m_pad, mod_blk),
                   o_map=lambda i, k: (i, 0, 0), grid=(2, 1), contract=NN, out_dtype=F32, bias=ada_b_loc, bias_blk=(None, 1, mod_blk),
                   bias_map=lambda i, k: (i, 0, 0))[:, :N_DEV]
    (modp_all,) = _all_gather("gather_mod", [modp.reshape(2 * N_DEV, mod_blk)])
    groups = [[w_in_loc[0], w_out_loc[0], pool_w.reshape(-1, POOL_GROUP).astype(BF16)], [w_in_loc[1], w_out_loc[1]],
              [w_in_loc[2], w_out_loc[2], mla_w_in[0].astype(BF16), mla_w_uq.reshape(mla_w_uq.shape[1], -1).astype(BF16),
               mla_w_o[0].astype(BF16)], [w_in_loc[3], w_out_loc[3]]]
    states, token = [], modp_all
    for name, group in zip("0abc", groups):
        state, token = _split_start(f"gather_start_{name}", "own", group, after=token)
        states.append(state)
    w_in8 = [None] * 4
    w_out4 = [None] * 4
    mod = lax.dynamic_index_in_dim(modp_all.reshape(N_DEV, 2, N_DEV, mod_blk), me, axis=2, keepdims=False)
    mod = mod.transpose(1, 0, 2).reshape(2, n_mod, d) + token[0, 0]
    mod_rows = mod.reshape(2 * n_mod, 1, d)
    g_rows = g_all.reshape(12, 1, d)
    weights_of = (0.5, 1.0, 0.5, 0.5, 1.0, 0.5)

    def pre_rows(k, with_shift):
        rows = [_rowk(g_rows, 2 * k), _rowk(mod_rows, 3 * k + 1)]
        return rows + [_rowk(mod_rows, 3 * k)] if with_shift else rows

    def post_rows(k):
        return [_rowk(g_rows, 2 * k + 1), _rowk(mod_rows, 3 * k + 2)]

    def act_dtype(k):
        return F32 if k == 1 else BF16

    saved = []
    xs = x0
    mla_wts = None
    h = _prenorm("prenorm_0", xs, pre_rows(0, True), act_dtype(0), tr)
    state, token = _split_pass("gather_pass_0", states[0], h)
    lands = _split_wait("gather_wait_0", state, token)
    w_in8[0], w_out4[0] = lands[0], lands[1].reshape(4, FF_BLK, d)
    w4 = lands[2].reshape(N_DEV, 4, POOL_SHARD, POOL_GROUP).transpose(1, 0, 2, 3).reshape(4, POOL_GROUP, POOL_GROUP)
    token = None
    for k in range(6):
        i, sub = divmod(k, 3)
        tag = f"{i}{sub}"
        if k == 1:
            states[1], token = _split_pass("gather_pass_a", states[1], xs)
        if k == 2:
            lands = _split_wait("gather_wait_a", states[1], xs)
            w_in8[1], w_out4[1] = lands[0], lands[1].reshape(4, FF_BLK, d)
            states[2], token = _split_pass("gather_pass_b", states[2], lands[0])
        if k == 3:
            lands = _split_wait("gather_wait_b", states[2], xs)
            w_in8[2], w_out4[2] = lands[0], lands[1].reshape(4, FF_BLK, d)
            mla_wts = mla_weights(*lands[2:])
            states[3], token = _split_pass("gather_pass_c", states[3], lands[0])
        if k == 5:
            lands = _split_wait("gather_wait_c", states[3], xs)
            w_in8[3], w_out4[3] = lands[0], lands[1].reshape(4, FF_BLK, d)
        if sub != 1:
            u, extra = _ffn_fwd(tag, h, w_in8[2 * i + sub // 2], w_out4[2 * i + sub // 2], tf, tf // 2, after=token)
            token = None
        elif i == 0:
            u, z, v = _pool_fwd(h, w4, pool_b_all + token[0, 0], pscale_row, min(4 * tm, s))
            extra = (z, v)
            token = None
        else:
            u, extra = _mla_fwd(h, mla_wts, cos_k, sin_k, tm)
        saved.append((xs, h, u, extra))
        if k < 5:
            xs, h = _norm_link(f"norm_link_{k + 1}", xs, u, post_rows(k), weights_of[k], pre_rows(k + 1, True), act_dtype(k + 1), tr)

    dx, du, (sq, dgate, dgpost) = _norm_loss("norm_loss", xs, u, target, post_rows(5), weights_of[5], tr)
    loss_part = (0.5 * jnp.sum(sq) / d).reshape(1, 1)

    d_mod = [None] * (2 * n_mod)
    d_g = [None] * 12
    d_mod[3 * 5 + 2], d_g[2 * 5 + 1] = dgate, dgpost
    sent = {}
    pool_grads = mla_grads = None

    def start_scatter(key, arrays):
        state, token = _split_start(f"scatter_start_{key}", "scatter", arrays)
        sent[key] = state
        return token

    for k in (5, 4, 3, 2, 1, 0):
        i, sub = divmod(k, 3)
        tag = f"{i}{sub}"
        xin, h, u, extra = saved[k]
        token = None
        if sub != 1:
            f = 2 * i + sub // 2
            dh, dgu, act = _ffn_bwd_act(tag, du, extra, w_in8[f], w_out4[f], tf, tf // 4)
            dw_out = _ffn_dw(f"ffn_dwout_{tag}", act, du, tw).reshape(N_DEV, FF_BLK // 2, d)
            if k == 0:
                d_pool_w = pool_grads[0].reshape(4, N_DEV, POOL_SHARD, POOL_GROUP).transpose(1, 0, 2, 3).reshape(N_DEV, -1, POOL_GROUP)
                token = start_scatter(tag + "_out", [dw_out, d_pool_w])
                token = start_scatter(tag + "_in", [_ffn_dw(f"ffn_dwin_{tag}", dgu, h, tw, after=token)])
            else:
                token = start_scatter(tag, [_ffn_dw(f"ffn_dwin_{tag}", dgu, h, tw), dw_out])
        elif i == 0:
            dh, dw4, (dpscale, dpb) = _pool_bwd(du, extra[0], extra[1], w4, pscale_row, min(4 * tm, s))
            pool_grads = (dw4, dpscale, dpb)
        else:
            dh, mla_grads = _mla_bwd(du, h, extra, mla_wts, cos_k, sin_k, tm)
            dwq = mla_grads["wq"]
            d_uq = jnp.concatenate([dwq[:, :, :QK_NOPE], dwq[:, :, KV_LORA:KV_LORA + QK_ROPE]], axis=2).transpose(1, 0, 2)
            token = start_scatter("mla", [mla_grads["w_in"][:, :mla_w_in.shape[2]].reshape(N_DEV, d // N_DEV, -1),
                                          d_uq.reshape(N_DEV, Q_LORA // N_DEV, -1), mla_grads["w_o"].reshape(N_DEV, d // N_DEV, d)])
            dwukp, dwuv2 = mla_grads["wukp"], mla_grads["wuv2"]
            d_uk = dwukp[:, :QK_NOPE, :KV_LORA].transpose(2, 0, 1).reshape(KV_LORA, -1)
            d_uv = jnp.where(even, dwuv2[:, :, :V_HEAD], dwuv2[:, :, V_HEAD:]).transpose(1, 0, 2).reshape(KV_LORA, -1)
            state_ukv, token = _split_start("gather_start_ukv", "gather", [d_uk, d_uv], after=token)
        if k > 0:
            dx, du, reds = _norm_link_bwd(f"norm_link_bwd_{k}", dh, xin, dx, saved[k - 1][2], pre_rows(k, False), post_rows(k - 1),
                                          weights_of[k - 1], act_dtype(k - 1), tr, after=token)
            d_mod[3 * (k - 1) + 2], d_g[2 * (k - 1) + 1] = reds[3], reds[4]
        else:
            dx, reds = _prenorm_bwd("prenorm_bwd_0", dh, xin, dx, pre_rows(0, False), tr, after=token)
        d_mod[3 * k], d_mod[3 * k + 1], d_g[2 * k] = reds[0], reds[1], reds[2]
    grad_x = dx.reshape(x.shape)

    def upd(name, parts, w, m, v):
        shape = w.shape
        r, cdim = parts.shape[1], parts.shape[2]
        return [o.reshape(shape) for o in _adamw(name, parts, w.reshape(r, cdim), m.reshape(r, cdim), v.reshape(r, cdim))]

    def landed(key, after):
        return _split_wait(f"scatter_wait_{key}", sent[key], after)

    res = {}
    w_in_s, m_in_s, v_in_s = (a.reshape(4, FF_BLK, d) for a in (w_in_t, m_in_t, v_in_t))
    w_out_s, m_out_s, v_out_s = (a.reshape(4, FF_BLK // 2, d) for a in (ffn_w_out, m_ffn_w_out, v_ffn_w_out))
    bufs_in = [lax.empty(w_in_s.shape, F32) for _ in range(4)]
    bufs_out = [lax.empty(w_out_s.shape, F32) for _ in range(4)]
    for key, k in (("12", 3), ("mla", None), ("10", 2), ("02", 1)):
        parts = landed(key, grad_x)
        if k is None:
            res["mla_w_in"] = upd("adam_mla_w_in", parts[0], mla_w_in, m_mla_w_in, v_mla_w_in)
            res["mla_w_uq"] = upd("adam_mla_w_uq", parts[1], mla_w_uq, m_mla_w_uq, v_mla_w_uq)
            res["mla_w_o"] = upd("adam_mla_w_o", parts[2], mla_w_o, m_mla_w_o, v_mla_w_o)
            uk_all, uv_all = _split_wait("gather_wait_ukv", state_ukv, grad_x)
            res["mla_w_uk"] = upd("adam_mla_w_uk", uk_all, mla_w_uk, m_mla_w_uk, v_mla_w_uk)
            res["mla_w_uv"] = upd("adam_mla_w_uv", uv_all, mla_w_uv, m_mla_w_uv, v_mla_w_uv)
            continue
        bufs_in = _adamw_slab(f"adam_ffn_w_in_{key}", parts[0], w_in_s, m_in_s, v_in_s, bufs_in, k)
        bufs_out = _adamw_slab(f"adam_ffn_w_out_{key}", parts[1], w_out_s, m_out_s, v_out_s, bufs_out, k)

    dw4, dpscale, dpb = pool_grads
    small_g = jnp.concatenate(d_mod + d_g + [dpb, dpscale, mla_grads["q_norm"], mla_grads["kv_norm"], loss_part], axis=1)
    done = [bufs_in[0], bufs_out[0], res["mla_w_o"][0], res["mla_w_uv"][0]]
    (small_g_all,) = _all_gather("gather_small_grads", [small_g], after=sum(a.reshape(-1)[:1] for a in done))
    small_g_all = small_g_all.reshape(N_DEV, -1)
    n_m = 2 * n_mod * d
    d_mod_all = small_g_all[:, :n_m].reshape(N_DEV, 2, n_mod * d)
    rest = small_g_all[:, n_m:]
    p_norm_g = lax.dynamic_slice_in_dim(rest[:, :12 * d].reshape(N_DEV, 12, d), me * (d // N_DEV), d // N_DEV, axis=2)
    p_pool_b = lax.dynamic_slice_in_dim(rest[:, 12 * d:13 * d].reshape(N_DEV, 4, POOL_GROUP), me * POOL_SHARD, POOL_SHARD, axis=2)
    p_pool_scale = rest[:, 13 * d:14 * d].reshape(N_DEV, 1, d)
    p_q_norm = lax.dynamic_slice_in_dim(rest[:, 14 * d:14 * d + Q_LORA], me * Q_SHARD, Q_SHARD, axis=1).reshape(N_DEV, 1, Q_SHARD)
    p_kv_norm = rest[:, 14 * d + Q_LORA:14 * d + Q_LORA + KV_LORA].reshape(N_DEV, 1, KV_LORA)
    loss = jnp.sum(rest[:, -1])

    d_mod_loc = lax.dynamic_slice_in_dim(d_mod_all, me * mod_blk, mod_blk, axis=2).transpose(1, 0, 2)
    k_pad = 128
    sc_t = jnp.pad(sc_all.T, ((0, 0), (0, k_pad - N_DEV)))
    d_ada_w = _matmul("ada_dw", sc_t, jnp.pad(d_mod_loc, ((0, 0), (0, k_pad - N_DEV), (0, 0))), a_blk=(d, k_pad),
                      a_map=lambda i, k: (0, 0), b_blk=(None, k_pad, mod_blk), b_map=lambda i, k: (i, 0, 0), o_shape=(2, d, mod_blk),
                      o_blk=(None, d, mod_blk), o_map=lambda i, k: (i, 0, 0), grid=(2, 1), contract=NN, out_dtype=F32)
    res["ada_w"] = upd("adam_ada_w", d_ada_w.reshape(1, 2 * d, mod_blk), ada_w, m_ada_w, v_ada_w)
    res["ada_b"] = upd("adam_ada_b", d_mod_all.reshape(N_DEV, 2, n_mod * d), ada_b, m_ada_b, v_ada_b)
    res["norm_g"] = upd("adam_norm_g", p_norm_g, norm_g, m_norm_g, v_norm_g)
    res["pool_b"] = upd("adam_pool_b", p_pool_b, pool_b, m_pool_b, v_pool_b)
    res["pool_scale"] = upd("adam_pool_scale", p_pool_scale, pool_scale, m_pool_scale, v_pool_scale)
    res["mla_q_norm"] = upd("adam_mla_q_norm", p_q_norm, mla_q_norm, m_mla_q_norm, v_mla_q_norm)
    res["mla_kv_norm"] = upd("adam_mla_kv_norm", p_kv_norm, mla_kv_norm, m_mla_kv_norm, v_mla_kv_norm)

    p_out, p_pool_w = landed("00_out", res["ada_w"][1])
    bufs_out = _adamw_slab("adam_ffn_w_out_00", p_out, w_out_s, m_out_s, v_out_s, bufs_out, 0)
    res["pool_w"] = upd("adam_pool_w", p_pool_w, pool_w, m_pool_w, v_pool_w)
    (p_in,) = landed("00_in", res["pool_w"][1])
    bufs_in = _adamw_slab("adam_ffn_w_in_00", p_in, w_in_s, m_in_s, v_in_s, bufs_in, 0)
    res["ffn_w_in"] = [jnp.swapaxes(b.reshape(w_in_t.shape), 2, 3) for b in bufs_in]
    res["ffn_w_out"] = [b.reshape(ffn_w_out.shape) for b in bufs_out]

    order = ["ada_w", "ada_b", "norm_g", "ffn_w_in", "ffn_w_out", "pool_w", "pool_b", "pool_scale", "mla_w_in", "mla_q_norm",
             "mla_kv_norm", "mla_w_uq", "mla_w_uk", "mla_w_uv", "mla_w_o"]
    outs = [loss, grad_x]
    for j in range(4):
        outs += [res[name][j] for name in order]
    return tuple(outs)
```

```python
import jax
import jax.numpy as jnp
from jax import lax
from jax.experimental import pallas as pl
from jax.experimental.pallas import tpu as pltpu

F32 = jnp.float32
BF16 = jnp.bfloat16
N_DEV = 8
MESH = pl.DeviceIdType.MESH

D_MODEL = 1024
N_HEADS = 16
QK_NOPE = 64
QK_ROPE = 32
V_HEAD = 64
Q_LORA = 256
KV_LORA = 128
LAT_PAD = 512
QK_PAD = 256
ONE_COL = 160
D_FF = 2816
FF_BLK = 2 * D_FF // N_DEV
POOL_WINDOWS = (2, 4, 8, 16)
POOL_GROUP = 256
POOL_SHARD = POOL_GROUP // N_DEV
Q_SHARD = Q_LORA // N_DEV
ROPE_THETA = 10000.0
EPS = 1e-6
ATTN_SCALE = (QK_NOPE + QK_ROPE) ** -0.5
LOG2_E = 1.4426950408889634
ADAM_LR, ADAM_B1, ADAM_B2, ADAM_EPS, ADAM_WD, ADAM_STEP = 0.001, 0.9, 0.999, 1e-08, 0.01, 10
VMEM_LIMIT = 56 * 1024 * 1024

NN = ((1,), (0,))
NT = ((1,), (1,))
TN = ((0,), (0,))


def _params(**kw):
    return pltpu.CompilerParams(vmem_limit_bytes=VMEM_LIMIT, **kw)


def _dot(a, b, contract):
    return lax.dot_general(a, b, (contract, ((), ())), preferred_element_type=F32)


def _matmul(name, a, b, *, a_blk, a_map, b_blk, b_map, o_shape, o_blk, o_map, grid, contract, out_dtype,
            bias=None, bias_blk=None, bias_map=None, after=None):
    n_k = grid[-1]
    k_axis = len(grid) - 1
    acc_shape = tuple(d for d in o_blk if d is not None)

    def body(*refs):
        a_ref, b_ref = refs[:2]
        bias_ref = refs[2] if bias is not None else None
        if n_k == 1:
            r = _dot(a_ref[...].astype(BF16), b_ref[...].astype(BF16), contract)
            if bias is not None:
                r = r + bias_ref[...]
            refs[-1][...] = r.astype(refs[-1].dtype)
            return
        o_ref, acc = refs[-2:]
        k = pl.program_id(k_axis)

        @pl.when(k == 0)
        def _():
            acc[...] = jnp.zeros_like(acc)

        acc[...] += _dot(a_ref[...].astype(BF16), b_ref[...].astype(BF16), contract)

        @pl.when(k == n_k - 1)
        def _():
            r = acc[...]
            if bias is not None:
                r = r + bias_ref[...]
            o_ref[...] = r.astype(o_ref.dtype)

    in_specs = [pl.BlockSpec(a_blk, a_map), pl.BlockSpec(b_blk, b_map)]
    args = [a, b]
    if bias is not None:
        in_specs.append(pl.BlockSpec(bias_blk, bias_map))
        args.append(bias)
    if after is not None:
        in_specs.append(pl.BlockSpec(memory_space=pl.ANY))
        args.append(after)
    return pl.pallas_call(
        body, name=name, grid=grid, in_specs=in_specs, out_specs=pl.BlockSpec(o_blk, o_map),
        out_shape=jax.ShapeDtypeStruct(o_shape, out_dtype), scratch_shapes=[pltpu.VMEM(acc_shape, F32)] if n_k > 1 else [],
        compiler_params=_params(),
    )(*args)


def _rowmap(name, fn, ins, outs, reds, grid, after=None):
    n_in, n_out, n_red = len(ins), len(outs), len(reds)
    extra = [] if after is None else [after]

    def body(*refs):
        in_refs = refs[:n_in]
        out_refs = refs[n_in + len(extra):n_in + len(extra) + n_out]
        red_refs = refs[n_in + len(extra) + n_out:]
        out_vals, red_vals = fn(*[r[...] for r in in_refs])
        for r, v in zip(out_refs, out_vals):
            r[...] = v.astype(r.dtype)
        if n_red:
            first = pl.program_id(0) == 0
            for ax in range(1, len(grid)):
                first = jnp.logical_and(first, pl.program_id(ax) == 0)

            @pl.when(first)
            def _():
                for r in red_refs:
                    r[...] = jnp.zeros_like(r)

            for r, v in zip(red_refs, red_vals):
                r[...] += v

    res = pl.pallas_call(
        body, name=name, grid=grid,
        in_specs=[pl.BlockSpec(blk, imap) for _, blk, imap in ins] + [pl.BlockSpec(memory_space=pl.ANY)] * len(extra),
        out_specs=[pl.BlockSpec(blk, imap) for _, blk, imap in list(outs) + list(reds)],
        out_shape=[sds for sds, _, _ in list(outs) + list(reds)],
        compiler_params=_params(),
    )(*[a for a, _, _ in ins], *extra)
    return res[:n_out], res[n_out:]


def _sds(shape, dtype):
    return jax.ShapeDtypeStruct(shape, dtype)


def _tile(a, tm):
    return (a, (tm, a.shape[1]), lambda i: (i, 0))


def _row(a):
    return (a, (1, a.shape[1]), lambda i: (0, 0))


def _otile(n, c, dtype, tm):
    return (_sds((n, c), dtype), (tm, c), lambda i: (i, 0))


def _ored(c):
    return (_sds((1, c), F32), (1, c), lambda i: (0, 0))


def _colsum(v):
    return jnp.sum(v, axis=0, keepdims=True)


def _rstd(v):
    return lax.rsqrt(jnp.mean(v * v, axis=-1, keepdims=True) + EPS)


def _pre(xv, g, sc, sh):
    return xv * _rstd(xv) * g * (1.0 + sc) + sh


def _post(xv, uv, g, gt, weight):
    uv = uv.astype(F32)
    return xv + weight * (1.0 + gt) * (uv * _rstd(uv) * g)


def _post_bwd(dv, uv, g, gt, weight):
    uv = uv.astype(F32)
    r = _rstd(uv)
    un = uv * r
    dy = dv * (weight * (1.0 + gt))
    a = dy * g
    du = r * (a - un * jnp.mean(a * un, axis=-1, keepdims=True))
    return du, (_colsum(dv * (weight * (un * g))), _colsum(dy * un))


def _pre_bwd(dhv, xv, dv, g, sc):
    dhv = dhv.astype(F32)
    r = _rstd(xv)
    xn = xv * r
    b = dhv * (g * (1.0 + sc))
    dx = dv + r * (b - xn * jnp.mean(b * xn, axis=-1, keepdims=True))
    return dx, (_colsum(dhv), _colsum(dhv * (xn * g)), _colsum(dhv * ((1.0 + sc) * xn)))


def _rowk(rows, k):
    return (rows, (None, 1, rows.shape[2]), lambda i: (k, 0, 0))


def _prenorm(name, x, pre, out_dtype, tm):
    n, d = x.shape
    (h,), _ = _rowmap(name, lambda xv, g, sc, sh: ((_pre(xv, g, sc, sh),), ()), [_tile(x, tm), *pre], [_otile(n, d, out_dtype, tm)],
                      [], (n // tm,))
    return h


def _norm_link(name, x, u, post, weight, pre, out_dtype, tm):
    n, d = x.shape

    def fn(xv, uv, g, gt, g2, sc, sh):
        xn = _post(xv, uv, g, gt, weight)
        return (xn, _pre(xn, g2, sc, sh)), ()

    (xn, h), _ = _rowmap(name, fn, [_tile(x, tm), _tile(u, tm), *post, *pre], [_otile(n, d, F32, tm), _otile(n, d, out_dtype, tm)],
                         [], (n // tm,))
    return xn, h


def _norm_loss(name, x, u, target, post, weight, tm):
    n, d = x.shape

    def fn(xv, uv, tv, g, gt):
        e = _post(xv, uv, g, gt, weight) - tv
        dv = e * (1.0 / d)
        du, reds = _post_bwd(dv, uv, g, gt, weight)
        return (dv, du), (_colsum(e * e), *reds)

    (dx, du), reds = _rowmap(name, fn, [_tile(x, tm), _tile(u, tm), _tile(target, tm), *post],
                             [_otile(n, d, F32, tm), _otile(n, d, BF16, tm)], [_ored(d)] * 3, (n // tm,))
    return dx, du, reds


def _norm_link_bwd(name, dh, x, dout, u_prev, pre, post_prev, weight_prev, out_dtype, tm, after=None):
    n, d = x.shape

    def fn(dhv, xv, dv, uv, g, sc, g2, gt):
        dx, reds = _pre_bwd(dhv, xv, dv, g, sc)
        du, reds_prev = _post_bwd(dx, uv, g2, gt, weight_prev)
        return (dx, du), (*reds, *reds_prev)

    (dx, du), reds = _rowmap(name, fn, [_tile(dh, tm), _tile(x, tm), _tile(dout, tm), _tile(u_prev, tm), *pre, *post_prev],
                             [_otile(n, d, F32, tm), _otile(n, d, out_dtype, tm)], [_ored(d)] * 5, (n // tm,), after=after)
    return dx, du, reds


def _prenorm_bwd(name, dh, x, dout, pre, tm, after=None):
    n, d = x.shape

    def fn(dhv, xv, dv, g, sc):
        dx, reds = _pre_bwd(dhv, xv, dv, g, sc)
        return (dx,), reds

    (dx,), reds = _rowmap(name, fn, [_tile(dh, tm), _tile(x, tm), _tile(dout, tm), *pre], [_otile(n, d, F32, tm)], [_ored(d)] * 3,
                          (n // tm,), after=after)
    return dx, reds


def _ffn_fwd(tag, h, w_in8, w_out4, tm, sub, after=None):
    s, d = h.shape

    extra = [] if after is None else [after]

    def body(h_ref, wg_ref, wu_ref, wo_ref, *rest):
        u_ref, gu_ref, acc = rest[-3:]
        j = pl.program_id(1)

        @pl.when(j == 0)
        def _():
            acc[...] = jnp.zeros_like(acc)

        for r in range(tm // sub):
            rows = pl.ds(r * sub, sub)
            hv = h_ref[rows, :]
            gate = _dot(hv, wg_ref[...], NT)
            up = _dot(hv, wu_ref[...], NT)
            gu_ref[0, rows, :] = gate.astype(BF16)
            gu_ref[1, rows, :] = up.astype(BF16)
            acc[rows, :] += _dot((gate * jax.nn.sigmoid(gate) * up).astype(BF16), wo_ref[...], NN)

        @pl.when(j == w_out4.shape[0] - 1)
        def _():
            u_ref[...] = acc[...].astype(u_ref.dtype)

    w_blk = (None, FF_BLK, d)
    return pl.pallas_call(
        body, name=f"ffn_fwd_{tag}", grid=(s // tm, 4),
        in_specs=[pl.BlockSpec((tm, d), lambda i, j: (i, 0)), pl.BlockSpec(w_blk, lambda i, j: (j, 0, 0)),
                  pl.BlockSpec(w_blk, lambda i, j: (j + 4, 0, 0)), pl.BlockSpec((None, FF_BLK, d), lambda i, j: (j, 0, 0))]
        + [pl.BlockSpec(memory_space=pl.ANY)] * len(extra),
        out_specs=[pl.BlockSpec((tm, d), lambda i, j: (i, 0)), pl.BlockSpec((2, None, tm, FF_BLK), lambda i, j: (0, j, i, 0))],
        out_shape=[_sds((s, d), BF16), _sds((2, 4, s, FF_BLK), BF16)], scratch_shapes=[pltpu.VMEM((tm, d), F32)],
        compiler_params=_params(),
    )(h, w_in8, w_in8, w_out4, *extra)


def _ffn_bwd_act(tag, du, gu, w_in8, w_out4, tm, sub):
    s, d = du.shape

    def body(du_ref, gu_ref, wg_ref, wu_ref, wo_ref, dh_ref, dgu_ref, act_ref, acc):
        j = pl.program_id(1)

        @pl.when(j == 0)
        def _():
            acc[...] = jnp.zeros_like(acc)

        n_sub = tm // sub
        dact_next = _dot(du_ref[pl.ds(0, sub), :], wo_ref[...], NT)
        for r in range(n_sub):
            rows = pl.ds(r * sub, sub)
            dact = dact_next
            if r + 1 < n_sub:
                dact_next = _dot(du_ref[pl.ds((r + 1) * sub, sub), :], wo_ref[...], NT)
            gate, up = gu_ref[0, rows, :].astype(F32), gu_ref[1, rows, :].astype(F32)
            sg = jax.nn.sigmoid(gate)
            silu = gate * sg
            dg = dact * up * (sg * (1.0 + gate * (1.0 - sg)))
            dup = dact * silu
            dgu_ref[0, :, rows] = dg.T.astype(BF16)
            dgu_ref[1, :, rows] = dup.T.astype(BF16)
            act_ref[:, rows] = (silu * up).T.astype(BF16)
            acc[rows, :] += _dot(dg.astype(BF16), wg_ref[...], NN) + _dot(dup.astype(BF16), wu_ref[...], NN)

        @pl.when(j == w_out4.shape[0] - 1)
        def _():
            dh_ref[...] = acc[...].astype(dh_ref.dtype)

    w_blk = (None, FF_BLK, d)
    dh, dgu_t, act_t = pl.pallas_call(
        body, name=f"ffn_bwd_{tag}", grid=(s // tm, 4),
        in_specs=[pl.BlockSpec((tm, d), lambda i, j: (i, 0)), pl.BlockSpec((2, None, tm, FF_BLK), lambda i, j: (0, j, i, 0)),
                  pl.BlockSpec(w_blk, lambda i, j: (j, 0, 0)), pl.BlockSpec(w_blk, lambda i, j: (j + 4, 0, 0)),
                  pl.BlockSpec((None, FF_BLK, d), lambda i, j: (j, 0, 0))],
        out_specs=[pl.BlockSpec((tm, d), lambda i, j: (i, 0)), pl.BlockSpec((2, None, FF_BLK, tm), lambda i, j: (0, j, 0, i)),
                   pl.BlockSpec((None, FF_BLK, tm), lambda i, j: (j, 0, i))],
        out_shape=[_sds((s, d), BF16), _sds((2, 4, FF_BLK, s), BF16), _sds((4, FF_BLK, s), BF16)],
        scratch_shapes=[pltpu.VMEM((tm, d), F32)], compiler_params=_params(),
    )(du, gu, w_in8, w_in8, w_out4)
    return dh, dgu_t.reshape(8, FF_BLK, s), act_t


def _ffn_dw(name, lhs_t, rhs, tk, after=None):
    n_g, _, s = lhs_t.shape
    d = rhs.shape[1]
    return _matmul(name, lhs_t, rhs, a_blk=(None, FF_BLK, tk), a_map=lambda g, k: (g, 0, k), b_blk=(tk, d), b_map=lambda g, k: (k, 0),
                   o_shape=(n_g, FF_BLK, d), o_blk=(None, FF_BLK, d), o_map=lambda g, k: (g, 0, 0), grid=(n_g, s // tk),
                   contract=NN, out_dtype=BF16, after=after)


def _window_sum(x, window, transpose):
    s = x.shape[0]
    t = lax.broadcasted_iota(jnp.int32, (s, 1), 0)
    half = window // 2
    cnt = jnp.minimum(t + half, s) - jnp.maximum(t - half, 0)
    inv = 1.0 / cnt.astype(F32)
    if transpose:
        x = x * inv
        offsets = range(-half + 1, half + 1)
    else:
        offsets = range(-half, half)
    acc = jnp.zeros_like(x)
    for o in offsets:
        shifted = x if o == 0 else pltpu.roll(x, (-o) % s, 0)
        valid = jnp.logical_and(t + o >= 0, t + o < s)
        acc = acc + jnp.where(valid, shifted, 0.0)
    return acc if transpose else acc * inv


def _pool_mix(name, x, transpose, out_dtype):
    s, d = x.shape

    def body(x_ref, o_ref):
        g = pl.program_id(0)
        for gi, window in enumerate(POOL_WINDOWS):
            @pl.when(g == gi)
            def _(window=window):
                xv = x_ref[...].astype(F32)
                o_ref[...] = (_window_sum(xv, window, transpose) - xv).astype(o_ref.dtype)

    return pl.pallas_call(
        body, name=name, grid=(len(POOL_WINDOWS),), in_specs=[pl.BlockSpec((s, POOL_GROUP), lambda g: (0, g))],
        out_specs=pl.BlockSpec((s, POOL_GROUP), lambda g: (0, g)), out_shape=_sds((s, d), out_dtype), compiler_params=_params(),
    )(x)


def _pool_fwd(h, w4, bias, pscale, tm):
    s, d = h.shape
    nt = s // tm
    z = _pool_mix("pool_mix", h, False, BF16)
    v = _matmul("pool_proj", z, w4, a_blk=(tm, POOL_GROUP), a_map=lambda i, g, k: (i, g), b_blk=(None, POOL_GROUP, POOL_GROUP),
                b_map=lambda i, g, k: (g, 0, 0), o_shape=(s, d), o_blk=(tm, POOL_GROUP), o_map=lambda i, g, k: (i, g),
                grid=(nt, 4, 1), contract=NN, out_dtype=F32, bias=bias, bias_blk=(1, POOL_GROUP), bias_map=lambda i, g, k: (0, g))
    tr = min(512, s)
    (u,), _ = _rowmap("pool_scale", lambda vv, ps: ((vv * ps,), ()), [_tile(v, tr), _row(pscale)], [_otile(s, d, BF16, tr)], [],
                      (s // tr,))
    return u, z, v


def _pool_bwd(du, z, v, w4, pscale, tm):
    s, d = du.shape
    nt = s // tm

    def fn(duv, vv, ps):
        dv = duv * ps
        return (dv,), (_colsum(duv * vv), _colsum(dv))

    tr = min(512, s)
    (dv,), reds = _rowmap("pool_dscale", fn, [_tile(du, tr), _tile(v, tr), _row(pscale)], [_otile(s, d, BF16, tr)],
                          [_ored(d), _ored(d)], (s // tr,))
    dw4 = _matmul("pool_dw", z, dv, a_blk=(tm, POOL_GROUP), a_map=lambda g, k: (k, g), b_blk=(tm, POOL_GROUP),
                  b_map=lambda g, k: (k, g), o_shape=(4, POOL_GROUP, POOL_GROUP), o_blk=(None, POOL_GROUP, POOL_GROUP),
                  o_map=lambda g, k: (g, 0, 0), grid=(4, nt), contract=TN, out_dtype=F32)
    dz = _matmul("pool_dz", dv, w4, a_blk=(tm, POOL_GROUP), a_map=lambda i, g, k: (i, g), b_blk=(None, POOL_GROUP, POOL_GROUP),
                 b_map=lambda i, g, k: (g, 0, 0), o_shape=(s, d), o_blk=(tm, POOL_GROUP), o_map=lambda i, g, k: (i, g),
                 grid=(nt, 4, 1), contract=NT, out_dtype=F32)
    dh = _pool_mix("pool_mix_t", dz, True, BF16)
    return dh, dw4, reds


def _lane(shape):
    return lax.broadcasted_iota(jnp.int32, shape, 1)


def _rope_swap(v, transpose):
    half = QK_ROPE // 2
    lane = _lane(v.shape)
    up = pltpu.roll(v, v.shape[1] - half, 1)
    down = pltpu.roll(v, half, 1)
    if transpose:
        return jnp.where(lane < half, up, jnp.where(lane < QK_ROPE, -down, 0.0))
    return jnp.where(lane < half, -up, jnp.where(lane < QK_ROPE, down, 0.0))


def _rope(v, cos, sin):
    return v * cos + _rope_swap(v, False) * sin


def _rope_t(g, cos, sin):
    return g * cos + _rope_swap(g * sin, True)


def _mla_mid(lat, q_norm, kv_norm, cos_k, sin_k, tm):
    s = lat.shape[0]

    def fn(lv, qn, kn, cs, sn):
        cq = lv[:, :Q_LORA]
        ckv = lv[:, Q_LORA:Q_LORA + KV_LORA]
        kr = lv[:, Q_LORA + KV_LORA:]
        cq = cq * _rstd(cq) * qn
        ckv = ckv * _rstd(ckv) * kn
        k_rope = jnp.where(_lane(kr.shape) == ONE_COL - KV_LORA, 1.0, _rope(kr, cs, sn))
        return (cq, jnp.concatenate([ckv, k_rope], axis=1)), ()

    (cq, kcat), _ = _rowmap("mla_mid", fn, [_tile(lat, tm), _row(q_norm), _row(kv_norm), _tile(cos_k, tm), _tile(sin_k, tm)],
                            [_otile(s, Q_LORA, BF16, tm), _otile(s, QK_PAD, BF16, tm)], [], (s // tm,))
    return cq, kcat


def _mla_mid_bwd(lat, dcq, dkcat, dv, q_norm, kv_norm, cos_k, sin_k, tm):
    s = lat.shape[0]

    def fn(lv, dq, dk, dvv, qn, kn, cs, sn):
        dk = dk * (1.0 / LOG2_E)
        cq = lv[:, :Q_LORA]
        ckv = lv[:, Q_LORA:Q_LORA + KV_LORA]
        rq, rk = _rstd(cq), _rstd(ckv)
        cqn, ckn = cq * rq, ckv * rk
        a = dq * qn
        d_cq = rq * (a - cqn * jnp.mean(a * cqn, axis=-1, keepdims=True))
        dckv = dk[:, :KV_LORA] + dvv
        a2 = dckv * kn
        d_ckv = rk * (a2 - ckn * jnp.mean(a2 * ckn, axis=-1, keepdims=True))
        d_kr = _rope_t(dk[:, KV_LORA:], cs, sn)
        return (jnp.concatenate([d_cq, d_ckv, d_kr], axis=1),), (_colsum(dq * cqn), _colsum(dckv * ckn))

    (dlat,), reds = _rowmap(
        "mla_mid_bwd", fn,
        [_tile(lat, tm), _tile(dcq, tm), _tile(dkcat, tm), _tile(dv, tm), _row(q_norm), _row(kv_norm), _tile(cos_k, tm),
         _tile(sin_k, tm)],
        [_otile(s, LAT_PAD, BF16, tm)], [_ored(Q_LORA), _ored(KV_LORA)], (s // tm,))
    return dlat, reds


def _mla_q(cq, wq, wukp, cos_k, sin_k, tm):
    s = cq.shape[0]

    def body(cq_ref, wq_ref, wuk_ref, cos_ref, sin_ref, o_ref):
        cqv, cs, sn = cq_ref[...], cos_ref[...], sin_ref[...]
        for h in range(N_HEADS):
            aq = _dot(cqv, wq_ref[h], NN)
            qlat = _dot(aq.astype(BF16), wuk_ref[h], NN)
            roped = _rope(aq[:, KV_LORA:], cs, sn)
            o_ref[h] = (jnp.concatenate([qlat[:, :KV_LORA], roped], axis=1) * (ATTN_SCALE * LOG2_E)).astype(o_ref.dtype)

    wblk = pl.BlockSpec((N_HEADS, QK_PAD, QK_PAD), lambda i: (0, 0, 0))
    tblk = pl.BlockSpec((tm, KV_LORA), lambda i: (i, 0))
    return pl.pallas_call(
        body, name="mla_q", grid=(s // tm,),
        in_specs=[pl.BlockSpec((tm, Q_LORA), lambda i: (i, 0)), wblk, wblk, tblk, tblk],
        out_specs=pl.BlockSpec((N_HEADS, tm, QK_PAD), lambda i: (0, i, 0)), out_shape=_sds((N_HEADS, s, QK_PAD), BF16),
        compiler_params=_params(),
    )(cq, wq, wukp, cos_k, sin_k)


def _mla_q_bwd(cq, wq, wukp, cos_k, sin_k, dqcat, tm):
    s = cq.shape[0]

    def body(cq_ref, wq_ref, wuk_ref, cos_ref, sin_ref, dq_ref, dcq_ref, dwq_ref, dwuk_ref):
        @pl.when(pl.program_id(0) == 0)
        def _():
            dwq_ref[...] = jnp.zeros_like(dwq_ref)
            dwuk_ref[...] = jnp.zeros_like(dwuk_ref)

        cqv, cs, sn = cq_ref[...], cos_ref[...], sin_ref[...]
        d_cq = jnp.zeros((tm, Q_LORA), F32)
        for h in range(N_HEADS):
            aq = _dot(cqv, wq_ref[h], NN).astype(BF16)
            g = dq_ref[h].astype(F32) * ATTN_SCALE
            gl, gr = g[:, :KV_LORA], g[:, KV_LORA:]
            dqlat = jnp.concatenate([gl, jnp.zeros_like(gl)], axis=1).astype(BF16)
            d_rope = _rope_t(gr, cs, sn)
            daq = _dot(dqlat, wuk_ref[h], NT) + jnp.concatenate([jnp.zeros_like(d_rope), d_rope], axis=1)
            daq_b = daq.astype(BF16)
            dwuk_ref[h] += _dot(aq, dqlat, TN)
            dwq_ref[h] += _dot(cqv, daq_b, TN)
            d_cq = d_cq + _dot(daq_b, wq_ref[h], NT)
        dcq_ref[...] = d_cq

    wblk = pl.BlockSpec((N_HEADS, QK_PAD, QK_PAD), lambda i: (0, 0, 0))
    tblk = pl.BlockSpec((tm, KV_LORA), lambda i: (i, 0))
    return pl.pallas_call(
        body, name="mla_q_bwd", grid=(s // tm,),
        in_specs=[pl.BlockSpec((tm, Q_LORA), lambda i: (i, 0)), wblk, wblk, tblk, tblk,
                  pl.BlockSpec((N_HEADS, tm, QK_PAD), lambda i: (0, i, 0))],
        out_specs=[pl.BlockSpec((tm, Q_LORA), lambda i: (i, 0)), wblk, wblk],
        out_shape=[_sds((s, Q_LORA), F32), _sds((N_HEADS, QK_PAD, QK_PAD), F32), _sds((N_HEADS, QK_PAD, QK_PAD), F32)],
        compiler_params=_params(),
    )(cq, wq, wukp, cos_k, sin_k, dqcat)


def _flash_fwd(qcat, kcat, tq, tk):
    n_h, s, _ = qcat.shape
    n_k = s // tk

    def body(q_ref, k_ref, o_ref, lse_ref):
        q = q_ref[...]
        m = jnp.full((tq, 1), -1e30, F32)
        acc = jnp.zeros((tq, QK_PAD), F32)
        for kk in range(n_k):
            k = k_ref[pl.ds(kk * tk, tk), :]
            sc = _dot(q, k, NT)
            m_new = jnp.maximum(m, jnp.max(sc, axis=1, keepdims=True))
            p = jnp.exp2(sc - m_new).astype(BF16)
            acc = jnp.exp2(m - m_new) * acc + _dot(p, k, NN)
            m = m_new
        l = jnp.sum(jnp.where(_lane(acc.shape) == ONE_COL, acc, 0.0), axis=1, keepdims=True)
        o_ref[...] = (acc[:, :KV_LORA] / l).astype(o_ref.dtype)
        lse_ref[...] = m + jnp.log2(l)

    return pl.pallas_call(
        body, name="mla_attn", grid=(n_h, s // tq),
        in_specs=[pl.BlockSpec((None, tq, QK_PAD), lambda h, i: (h, i, 0)), pl.BlockSpec((s, QK_PAD), lambda h, i: (0, 0))],
        out_specs=[pl.BlockSpec((None, tq, KV_LORA), lambda h, i: (h, i, 0)), pl.BlockSpec((None, tq, 1), lambda h, i: (h, i, 0))],
        out_shape=[_sds((n_h, s, KV_LORA), BF16), _sds((n_h, s, 1), F32)], compiler_params=_params(),
    )(qcat, kcat)


def _flash_bwd(qcat, kcat, o, do, lse, tq, tk):
    n_h, s, _ = qcat.shape
    n_k = s // tk

    def body(q_ref, k_ref, v_ref, o_ref, do_ref, lse_ref, dq_ref, dk_ref, dv_ref, dq_acc):
        h, i = pl.program_id(0), pl.program_id(1)

        @pl.when(jnp.logical_and(h == 0, i == 0))
        def _():
            dk_ref[...] = jnp.zeros_like(dk_ref)
            dv_ref[...] = jnp.zeros_like(dv_ref)

        q = q_ref[...]
        dov = do_ref[...]
        dov_t = dov.T
        q_t = q.T
        lse_v = lse_ref[...]
        delta = jnp.sum(dov.astype(F32) * o_ref[...].astype(F32), axis=1, keepdims=True)
        dq_acc[...] = jnp.zeros_like(dq_acc)

        for kk in range(n_k):
            rows = pl.ds(kk * tk, tk)
            k = k_ref[rows, :]
            p = jnp.exp2(_dot(q, k, NT) - lse_v)
            dp = _dot(dov, v_ref[rows, :], NT)
            ds = (p * (dp - delta)).astype(BF16)
            dq_acc[...] += _dot(ds, k, NN)
            dv_ref[:, rows] += _dot(dov_t, p.astype(BF16), NN)
            dk_ref[:, rows] += _dot(q_t, ds, NN)
        dq_ref[...] = dq_acc[...].astype(dq_ref.dtype)

    qblk = pl.BlockSpec((None, tq, QK_PAD), lambda h, i: (h, i, 0))
    oblk = pl.BlockSpec((None, tq, KV_LORA), lambda h, i: (h, i, 0))
    return pl.pallas_call(
        body, name="mla_attn_bwd", grid=(n_h, s // tq),
        in_specs=[qblk, pl.BlockSpec((s, QK_PAD), lambda h, i: (0, 0)), pl.BlockSpec((s, KV_LORA), lambda h, i: (0, 0)), oblk, oblk,
                  pl.BlockSpec((None, tq, 1), lambda h, i: (h, i, 0))],
        out_specs=[qblk, pl.BlockSpec((QK_PAD, s), lambda h, i: (0, 0)), pl.BlockSpec((KV_LORA, s), lambda h, i: (0, 0))],
        out_shape=[_sds((n_h, s, QK_PAD), BF16), _sds((QK_PAD, s), F32), _sds((KV_LORA, s), F32)],
        scratch_shapes=[pltpu.VMEM((tq, QK_PAD), F32)], compiler_params=_params(),
    )(qcat, kcat, kcat, o, do, lse)


def _mla_uv(o_lat, wuv2, do, tm):
    n_h, s, _ = o_lat.shape
    d = n_h * V_HEAD
    pair = 2 * V_HEAD
    lat_blk = pl.BlockSpec((n_h, tm, KV_LORA), lambda i: (0, i, 0))
    w_blk = pl.BlockSpec((n_h, KV_LORA, pair), lambda i: (0, 0, 0))
    row_blk = pl.BlockSpec((tm, d), lambda i: (i, 0))

    if do is None:
        def body(a_ref, w_ref, o_ref):
            for p in range(n_h // 2):
                o_ref[:, p * pair:(p + 1) * pair] = (
                    _dot(a_ref[2 * p], w_ref[2 * p], NN) + _dot(a_ref[2 * p + 1], w_ref[2 * p + 1], NN)).astype(o_ref.dtype)

        return pl.pallas_call(body, name="mla_uv", grid=(s // tm,), in_specs=[lat_blk, w_blk], out_specs=row_blk,
                              out_shape=_sds((s, d), BF16), compiler_params=_params())(o_lat, wuv2)

    def body(a_ref, w_ref, do_ref, dlat_ref, dw_ref):
        @pl.when(pl.program_id(0) == 0)
        def _():
            dw_ref[...] = jnp.zeros_like(dw_ref)

        for h in range(n_h):
            dov = do_ref[:, (h // 2) * pair:(h // 2 + 1) * pair]
            dlat_ref[h] = _dot(dov, w_ref[h], NT).astype(dlat_ref.dtype)
            dw_ref[h] += _dot(a_ref[h], dov, TN)

    return pl.pallas_call(body, name="mla_uv_bwd", grid=(s // tm,), in_specs=[lat_blk, w_blk, row_blk], out_specs=[lat_blk, w_blk],
                          out_shape=[_sds((n_h, s, KV_LORA), BF16), _sds((n_h, KV_LORA, pair), F32)], compiler_params=_params(),
                          )(o_lat, wuv2, do)


def _mla_fwd(h, wts, cos_k, sin_k, tm):
    s, d = h.shape
    nt = s // tm
    lat = _matmul("mla_lat", h, wts["w_in"], a_blk=(tm, d), a_map=lambda i, k: (i, 0), b_blk=(d, LAT_PAD), b_map=lambda i, k: (0, 0),
                  o_shape=(s, LAT_PAD), o_blk=(tm, LAT_PAD), o_map=lambda i, k: (i, 0), grid=(nt, 1), contract=NN, out_dtype=F32)
    cq, kcat = _mla_mid(lat, wts["q_norm"], wts["kv_norm"], cos_k, sin_k, tm)
    qcat = _mla_q(cq, wts["wq"], wts["wukp"], cos_k, sin_k, tm)
    o_lat, lse = _flash_fwd(qcat, kcat, min(2 * tm, s), tm)
    o = _mla_uv(o_lat, wts["wuv2"], None, tm)
    u = _matmul("mla_out", o, wts["w_o"], a_blk=(tm, d), a_map=lambda i, k: (i, 0), b_blk=(d, d), b_map=lambda i, k: (0, 0),
                o_shape=(s, d), o_blk=(tm, d), o_map=lambda i, k: (i, 0), grid=(nt, 1), contract=NN, out_dtype=BF16)
    return u, (lat, cq, kcat, qcat, o_lat, lse, o)


def _mla_bwd(du, h, saved, wts, cos_k, sin_k, tm):
    lat, cq, kcat, qcat, o_lat, lse, o = saved
    s, d = h.shape
    nt = s // tm
    do = _matmul("mla_do", du, wts["w_o"], a_blk=(tm, d), a_map=lambda i, k: (i, 0), b_blk=(d, d), b_map=lambda i, k: (0, 0),
                 o_shape=(s, d), o_blk=(tm, d), o_map=lambda i, k: (i, 0), grid=(nt, 1), contract=NT, out_dtype=BF16)
    dw_o = _matmul("mla_dwo", o, du, a_blk=(tm, d), a_map=lambda k: (k, 0), b_blk=(tm, d), b_map=lambda k: (k, 0),
                   o_shape=(d, d), o_blk=(d, d), o_map=lambda k: (0, 0), grid=(nt,), contract=TN, out_dtype=F32)
    do_lat, dwuv2 = _mla_uv(o_lat, wts["wuv2"], do, tm)
    dqcat, dkcat_t, dv_t = _flash_bwd(qcat, kcat, o_lat, do_lat, lse, min(2 * tm, s), tm)
    dkcat, dv = dkcat_t.T, dv_t.T
    dcq, dwq, dwukp = _mla_q_bwd(cq, wts["wq"], wts["wukp"], cos_k, sin_k, dqcat, tm)
    dlat, (dqn, dkn) = _mla_mid_bwd(lat, dcq, dkcat, dv, wts["q_norm"], wts["kv_norm"], cos_k, sin_k, tm)
    dh = _matmul("mla_dh", dlat, wts["w_in"], a_blk=(tm, LAT_PAD), a_map=lambda i, k: (i, 0), b_blk=(d, LAT_PAD),
                 b_map=lambda i, k: (0, 0), o_shape=(s, d), o_blk=(tm, d), o_map=lambda i, k: (i, 0), grid=(nt, 1), contract=NT,
                 out_dtype=BF16)
    dw_in = _matmul("mla_dwin", h, dlat, a_blk=(tm, d), a_map=lambda k: (k, 0), b_blk=(tm, LAT_PAD), b_map=lambda k: (k, 0),
                    o_shape=(d, LAT_PAD), o_blk=(d, LAT_PAD), o_map=lambda k: (0, 0), grid=(nt,), contract=TN, out_dtype=F32)
    return dh, dict(w_in=dw_in, wq=dwq, wukp=dwukp, wuv2=dwuv2, w_o=dw_o, q_norm=dqn, kv_norm=dkn)


def _adamw(name, parts, w, m, v):
    n_parts, r, c = parts.shape
    tr = r
    for cand in (256, 128, 64, 32, 16, 8):
        if r > cand and r % cand == 0:
            tr = cand
            break

    def body(p_ref, w_ref, m_ref, v_ref, g_ref, d_ref, nm_ref, nv_ref):
        g = p_ref[0].astype(F32)
        for k in range(1, n_parts):
            g = g + p_ref[k].astype(F32)
        nm = ADAM_B1 * m_ref[...] + (1.0 - ADAM_B1) * g
        nv = ADAM_B2 * v_ref[...] + (1.0 - ADAM_B2) * (g * g)
        m_hat = nm / (1.0 - ADAM_B1 ** ADAM_STEP)
        v_hat = nv / (1.0 - ADAM_B2 ** ADAM_STEP)
        g_ref[...] = g
        d_ref[...] = -ADAM_LR * (m_hat / (jnp.sqrt(v_hat) + ADAM_EPS) + ADAM_WD * w_ref[...])
        nm_ref[...] = nm
        nv_ref[...] = nv

    blk = pl.BlockSpec((tr, c), lambda i: (i, 0))
    return pl.pallas_call(
        body, name=name, grid=(r // tr,), in_specs=[pl.BlockSpec((n_parts, tr, c), lambda i: (0, i, 0)), blk, blk, blk],
        out_specs=[blk] * 4, out_shape=[_sds((r, c), F32)] * 4, compiler_params=_params(),
    )(parts, w, m, v)


def _adamw_slab(name, parts, w, m, v, bufs, f):
    n_parts, r, c = parts.shape
    tr = max(t for t in range(8, 257, 8) if r % t == 0)

    def body(p_ref, w_ref, m_ref, v_ref, *rest):
        g_ref, d_ref, nm_ref, nv_ref = rest[4:]
        g = p_ref[0].astype(F32)
        for k in range(1, n_parts):
            g = g + p_ref[k].astype(F32)
        nm = ADAM_B1 * m_ref[...] + (1.0 - ADAM_B1) * g
        nv = ADAM_B2 * v_ref[...] + (1.0 - ADAM_B2) * (g * g)
        m_hat = nm / (1.0 - ADAM_B1 ** ADAM_STEP)
        v_hat = nv / (1.0 - ADAM_B2 ** ADAM_STEP)
        g_ref[...] = g
        d_ref[...] = -ADAM_LR * (m_hat / (jnp.sqrt(v_hat) + ADAM_EPS) + ADAM_WD * w_ref[...])
        nm_ref[...] = nm
        nv_ref[...] = nv

    blk = pl.BlockSpec((None, tr, c), lambda i: (f, i, 0))
    return pl.pallas_call(
        body, name=name, grid=(r // tr,),
        in_specs=[pl.BlockSpec((n_parts, tr, c), lambda i: (0, i, 0)), blk, blk, blk] + [pl.BlockSpec(memory_space=pl.ANY)] * 4,
        out_specs=[blk] * 4, out_shape=[_sds(w.shape, F32)] * 4, input_output_aliases={4 + j: j for j in range(4)},
        compiler_params=_params(),
    )(parts, w, m, v, *bufs)


def _mesh_pos():
    return lax.axis_index("x"), lax.axis_index("y"), lax.axis_index("c")


def _flip(pos, mask):
    return tuple(1 - p if (mask >> (2 - b)) & 1 else p for b, p in enumerate(pos))


def _index(pos):
    return 4 * pos[0] + 2 * pos[1] + pos[2]


def _all_gather(name, xs, after=None):
    n = len(xs)
    extra = [] if after is None else [after]

    def body(*refs):
        x_refs, o_refs = refs[:n], refs[n + len(extra):2 * n + len(extra)]
        send_sems, recv_sems, local_sems = refs[2 * n + len(extra):]
        me = _mesh_pos()
        sibling = _flip(me, 1)
        others = [_flip(me, 4), _flip(me, 2), _flip(me, 6)]

        def copy(k, j, block, to, src=None):
            dst = o_refs[k].at[_index(block)]
            return pltpu.make_async_remote_copy(
                src_ref=dst if src is None else src, dst_ref=dst, send_sem=send_sems.at[k, j], recv_sem=recv_sems.at[k, j],
                device_id=to, device_id_type=MESH)

        local = [pltpu.make_async_copy(x_refs[k], o_refs[k].at[_index(me)], local_sems.at[k]) for k in range(n)]
        for cp in local:
            cp.start()
        first = []
        for k in range(n):
            first.append(copy(k, 0, me, sibling, src=x_refs[k]))
            first += [copy(k, 1 + j, me, other, src=x_refs[k]) for j, other in enumerate(others)]
        for cp in first:
            cp.start()
        passed = []
        for j, other in enumerate(others):
            for k in range(n):
                copy(k, 1 + j, other, me).wait_recv()
                cp = copy(k, 4 + j, other, sibling)
                cp.start()
                passed.append(cp)
        for k in range(n):
            copy(k, 0, sibling, me).wait_recv()
        for j, other in enumerate(others):
            for k in range(n):
                copy(k, 4 + j, _flip(other, 1), me).wait_recv()
        for cp in first + passed:
            cp.wait_send()
        for cp in local:
            cp.wait()

    any_spec = pl.BlockSpec(memory_space=pl.ANY)
    return pl.pallas_call(
        body, name=name, in_specs=[any_spec] * (n + len(extra)), out_specs=[any_spec] * n,
        out_shape=[_sds((N_DEV,) + x.shape, x.dtype) for x in xs],
        scratch_shapes=[pltpu.SemaphoreType.DMA((n, 7)), pltpu.SemaphoreType.DMA((n, 7)), pltpu.SemaphoreType.DMA((n,))],
    )(*xs, *extra)


_MASKS = {
    "gather": tuple(range(1, N_DEV)),
    "scatter": tuple(range(1, N_DEV)),
    "own": (1, 4, 2, 6),
    "pass": (4, 2, 6),
}


def _split_copies(kind, outgoing, x_refs, land_refs, send_sems, recv_sems):
    me = _mesh_pos()
    masks = _MASKS[kind]
    copies = []
    for k, (x_ref, land_ref) in enumerate(zip(x_refs, land_refs)):
        for j, mask in enumerate(masks):
            if kind == "pass":
                peer = _flip(me, 1)
                src = land_ref.at[_index(_flip(me, mask))]
                dst = land_ref.at[_index(_flip(me, mask if outgoing else mask | 1))]
            else:
                peer = _flip(me, mask)
                src = x_ref.at[_index(peer)] if kind == "scatter" else x_ref
                dst = land_ref.at[_index(me if outgoing else peer)]
            sem = k * len(masks) + j
            copies.append(pltpu.make_async_remote_copy(src_ref=src, dst_ref=dst, send_sem=send_sems.at[sem], recv_sem=recv_sems.at[sem],
                                                       device_id=peer, device_id_type=MESH))
    return copies


_HBM_SPEC = pl.BlockSpec(memory_space=pltpu.HBM)
_SEM_SPEC = pl.BlockSpec(memory_space=pltpu.SEMAPHORE)
_EFFECT = pltpu.SideEffectType.DATAFLOW_SIDE_EFFECTING


def _split_start(name, kind, xs, after=None):
    n = len(xs)
    n_sem = n * len(_MASKS[kind])
    extra = [] if after is None else [after]
    lands = [lax.empty(x.shape if kind == "scatter" else (N_DEV,) + x.shape, x.dtype) for x in xs]

    def body(*refs):
        x_refs, land_refs = refs[:n], refs[n:2 * n]
        send_sems, recv_sems = refs[2 * n + len(extra)], refs[2 * n + len(extra) + 1]
        token = refs[-1]
        for cp in _split_copies(kind, True, x_refs, land_refs, send_sems, recv_sems):
            cp.start()
        token[...] = jnp.zeros_like(token)

    hbm = [pltpu.HBM(a.shape, a.dtype) for a in list(xs) + lands]
    res = pl.pallas_call(
        body, name=name,
        out_shape=[pltpu.SemaphoreType.DMA((n_sem,)), pltpu.SemaphoreType.DMA((n_sem,))] + hbm + [_sds((8, 128), F32)],
        in_specs=[_HBM_SPEC] * (2 * n) + [pl.BlockSpec(memory_space=pl.ANY)] * len(extra),
        out_specs=[_SEM_SPEC, _SEM_SPEC] + [_HBM_SPEC] * (2 * n) + [pl.BlockSpec(memory_space=pltpu.VMEM)],
        input_output_aliases={j: 2 + j for j in range(2 * n)}, compiler_params=pltpu.CompilerParams(has_side_effects=_EFFECT),
    )(*[pltpu.with_memory_space_constraint(a, pltpu.HBM) for a in list(xs) + lands], *extra)
    return (kind, n, res[0], res[1], res[2:2 + 2 * n]), res[-1]


def _split_pass(name, state, after):
    kind, n, send_sems_in, recv_sems_in, thru = state
    n_sem = n * len(_MASKS["pass"])

    def body(*refs):
        x_refs, land_refs = refs[:n], refs[n:2 * n]
        send_sems, recv_sems = refs[2 * n], refs[2 * n + 1]
        next_send, next_recv, token = refs[-3:]
        for cp in _split_copies(kind, True, x_refs, land_refs, send_sems, recv_sems):
            cp.wait_send()
        for cp in _split_copies(kind, False, x_refs, land_refs, send_sems, recv_sems):
            cp.wait_recv()
        for cp in _split_copies("pass", True, land_refs, land_refs, next_send, next_recv):
            cp.start()
        token[...] = jnp.zeros_like(token)

    res = pl.pallas_call(
        body, name=name,
        out_shape=[pltpu.HBM(a.shape, a.dtype) for a in thru] + [pltpu.SemaphoreType.DMA((n_sem,)), pltpu.SemaphoreType.DMA((n_sem,)),
                                                                 _sds((8, 128), F32)],
        in_specs=[_HBM_SPEC] * (2 * n) + [_SEM_SPEC, _SEM_SPEC, pl.BlockSpec(memory_space=pl.ANY)],
        out_specs=[_HBM_SPEC] * (2 * n) + [_SEM_SPEC, _SEM_SPEC, pl.BlockSpec(memory_space=pltpu.VMEM)],
        input_output_aliases={j: j for j in range(2 * n)}, compiler_params=pltpu.CompilerParams(has_side_effects=_EFFECT),
    )(*thru, send_sems_in, recv_sems_in, after)
    return ("pass", n, res[2 * n], res[2 * n + 1], res[:2 * n]), res[-1]


def _split_wait(name, state, after):
    kind, n, send_sems_in, recv_sems_in, thru = state

    def body(*refs):
        x_refs, land_refs = refs[:n], refs[n:2 * n]
        send_sems, recv_sems = refs[2 * n], refs[2 * n + 1]
        for cp in _split_copies(kind, True, x_refs, land_refs, send_sems, recv_sems):
            cp.wait_send()
        for cp in _split_copies(kind, False, x_refs, land_refs, send_sems, recv_sems):
            cp.wait_recv()

    res = pl.pallas_call(
        body, name=name, out_shape=[pltpu.HBM(a.shape, a.dtype) for a in thru],
        in_specs=[_HBM_SPEC] * (2 * n) + [_SEM_SPEC, _SEM_SPEC, pl.BlockSpec(memory_space=pl.ANY)], out_specs=[_HBM_SPEC] * (2 * n),
        input_output_aliases={j: j for j in range(2 * n)}, compiler_params=pltpu.CompilerParams(has_side_effects=_EFFECT),
    )(*thru, send_sems_in, recv_sems_in, after)
    me = _index(_mesh_pos())
    out = []
    for x, land in zip(res[:n], res[n:]):
        own = lax.dynamic_index_in_dim(x, me, 0, keepdims=False) if kind == "scatter" else x
        out.append(lax.dynamic_update_slice(land, own[None], (me,) + (0,) * own.ndim))
    return out


def _rope_tables(s):
    inv = 1.0 / (ROPE_THETA ** (jnp.arange(0, QK_ROPE, 2, dtype=F32) / QK_ROPE))
    ang = jnp.arange(s, dtype=F32)[:, None] * inv[None, :]
    pad = jnp.zeros((s, KV_LORA - QK_ROPE), F32)
    return (jnp.concatenate([jnp.cos(ang), jnp.cos(ang), pad], axis=1), jnp.concatenate([jnp.sin(ang), jnp.sin(ang), pad], axis=1))


def kernel(x, c, ada_w, ada_b, norm_g, ffn_w_in, ffn_w_out, pool_w, pool_b, pool_scale, mla_w_in, mla_q_norm, mla_kv_norm, mla_w_uq, mla_w_uk, mla_w_uv, mla_w_o, loss_target, m_ada_w, m_ada_b, m_norm_g, m_ffn_w_in, m_ffn_w_out, m_pool_w, m_pool_b, m_pool_scale, m_mla_w_in, m_mla_q_norm, m_mla_kv_norm, m_mla_w_uq, m_mla_w_uk, m_mla_w_uv, m_mla_w_o, v_ada_w, v_ada_b, v_norm_g, v_ffn_w_in, v_ffn_w_out, v_pool_w, v_pool_b, v_pool_scale, v_mla_w_in, v_mla_q_norm, v_mla_kv_norm, v_mla_w_uq, v_mla_w_uk, v_mla_w_uv, v_mla_w_o):
    s, d = x.shape[1], x.shape[2]
    tm = min(512, s)
    tr = min(512, s)
    tf = min(1024, s)
    tw = min(2048, s)
    me = 4 * lax.axis_index("x") + 2 * lax.axis_index("y") + lax.axis_index("c")
    x0 = x.reshape(s, d)
    target = loss_target.reshape(s, d)
    n_mod = ada_w.shape[2] * N_DEV // d
    mod_blk = ada_w.shape[2]

    small = jnp.concatenate([c.reshape(-1), norm_g.reshape(-1), pool_b.reshape(-1), mla_q_norm.reshape(-1)]).reshape(1, -1)
    w_in_t, m_in_t, v_in_t = (jnp.swapaxes(a, 2, 3) for a in (ffn_w_in, m_ffn_w_in, v_ffn_w_in))
    w_in_loc = [w_in_t[i, f].astype(BF16) for i in range(2) for f in range(2)]
    w_out_loc = [ffn_w_out[i, f].astype(BF16) for i in range(2) for f in range(2)]
    (small_all,) = _all_gather("gather_small", [small])
    small_all = small_all.reshape(N_DEV, -1)
    c_all = small_all[:, :d]
    off = d
    g_all = small_all[:, off:off + 12 * (d // N_DEV)].reshape(N_DEV, 2, 6, d // N_DEV).transpose(1, 2, 0, 3).reshape(2, 6, d)
    off += 12 * (d // N_DEV)
    pool_b_all = small_all[:, off:off + 4 * POOL_SHARD].reshape(N_DEV, 4, POOL_SHARD).transpose(1, 0, 2).reshape(1, d)
    off += 4 * POOL_SHARD
    q_norm_all = small_all[:, off:off + Q_SHARD].reshape(1, Q_LORA)
    kv_norm_row = mla_kv_norm.reshape(1, KV_LORA)
    pscale_row = pool_scale.reshape(1, d)

    even = (jnp.arange(N_HEADS) % 2 == 0)[:, None, None]
    cos_k, sin_k = _rope_tables(s)

    def mla_weights(mla_w_in_all, mla_w_uq_all, mla_w_o_all):
        uq = mla_w_uq_all.reshape(Q_LORA, N_HEADS, QK_NOPE + QK_ROPE).transpose(1, 0, 2)
        zq = jnp.zeros((N_HEADS, Q_LORA, QK_NOPE), BF16)
        wq = jnp.concatenate(
            [uq[:, :, :QK_NOPE], zq, uq[:, :, QK_NOPE:], jnp.zeros((N_HEADS, Q_LORA, QK_PAD - KV_LORA - QK_ROPE), BF16)], axis=2)
        wukp = jnp.pad(mla_w_uk[0].transpose(1, 2, 0).astype(BF16), ((0, 0), (0, QK_PAD - QK_NOPE), (0, QK_PAD - KV_LORA)))
        uv = mla_w_uv[0].transpose(1, 0, 2).astype(BF16)
        wuv2 = jnp.where(even, jnp.concatenate([uv, jnp.zeros_like(uv)], axis=2), jnp.concatenate([jnp.zeros_like(uv), uv], axis=2))
        return dict(w_in=jnp.pad(mla_w_in_all.reshape(d, -1), ((0, 0), (0, LAT_PAD - mla_w_in.shape[2]))), wq=wq, wukp=wukp,
                    wuv2=wuv2, w_o=mla_w_o_all.reshape(d, d), q_norm=q_norm_all, kv_norm=kv_norm_row)

    (sc_all,), _ = _rowmap("ada_silu", lambda cv: ((cv * jax.nn.sigmoid(cv),), ()), [(c_all, (N_DEV, d), lambda i: (0, 0))],
                           [(_sds((N_DEV, d), F32), (N_DEV, d), lambda i: (0, 0))], [], (1,))
    ada_b_loc = lax.dynamic_slice_in_dim(ada_b, me * mod_blk, mod_blk, axis=1).reshape(2, 1, mod_blk)
    m_pad = 2 * N_DEV
    modp = _matmul("ada_mod", jnp.pad(sc_all, ((0, m_pad - N_DEV), (0, 0))), ada_w, a_blk=(m_pad, d), a_map=lambda i, k: (0, 0),
                   b_blk=(None, d, mod_blk), b_map=lambda i, k: (i, 0, 0), o_shape=(2, m_pad, mod_blk), o_blk=(None, m_pad, mod_blk),
                   o_map=lambda i, k: (i, 0, 0), grid=(2, 1), contract=NN, out_dtype=F32, bias=ada_b_loc, bias_blk=(None, 1, mod_blk),
                   bias_map=lambda i, k: (i, 0, 0))[:, :N_DEV]
    (modp_all,) = _all_gather("gather_mod", [modp.reshape(2 * N_DEV, mod_blk)])
    groups = [[w_in_loc[0], w_out_loc[0], pool_w.reshape(-1, POOL_GROUP).astype(BF16)], [w_in_loc[1], w_out_loc[1]],
              [w_in_loc[2], w_out_loc[2], mla_w_in[0].astype(BF16), mla_w_uq.reshape(mla_w_uq.shape[1], -1).astype(BF16),
               mla_w_o[0].astype(BF16)], [w_in_loc[3], w_out_loc[3]]]
    states, token = [], modp_all
    for name, group in zip("0abc", groups):
        state, token = _split_start(f"gather_start_{name}", "own", group, after=token)
        states.append(state)
    w_in8 = [None] * 4
    w_out4 = [None] * 4
    mod = lax.dynamic_index_in_dim(modp_all.reshape(N_DEV, 2, N_DEV, mod_blk), me, axis=2, keepdims=False)
    mod = mod.transpose(1, 0, 2).reshape(2, n_mod, d) + token[0, 0]
    mod_rows = mod.reshape(2 * n_mod, 1, d)
    g_rows = g_all.reshape(12, 1, d)
    weights_of = (0.5, 1.0, 0.5, 0.5, 1.0, 0.5)

    def pre_rows(k, with_shift):
        rows = [_rowk(g_rows, 2 * k), _rowk(mod_rows, 3 * k + 1)]
        return rows + [_rowk(mod_rows, 3 * k)] if with_shift else rows

    def post_rows(k):
        return [_rowk(g_rows, 2 * k + 1), _rowk(mod_rows, 3 * k + 2)]

    def act_dtype(k):
        return F32 if k == 1 else BF16

    saved = []
    xs = x0
    mla_wts = None
    h = _prenorm("prenorm_0", xs, pre_rows(0, True), act_dtype(0), tr)
    state, token = _split_pass("gather_pass_0", states[0], h)
    lands = _split_wait("gather_wait_0", state, token)
    w_in8[0], w_out4[0] = lands[0], lands[1].reshape(4, FF_BLK, d)
    w4 = lands[2].reshape(N_DEV, 4, POOL_SHARD, POOL_GROUP).transpose(1, 0, 2, 3).reshape(4, POOL_GROUP, POOL_GROUP)
    token = None
    for k in range(6):
        i, sub = divmod(k, 3)
        tag = f"{i}{sub}"
        if k == 1:
            states[1], token = _split_pass("gather_pass_a", states[1], xs)
        if k == 2:
            lands = _split_wait("gather_wait_a", states[1], xs)
            w_in8[1], w_out4[1] = lands[0], lands[1].reshape(4, FF_BLK, d)
            states[2], token = _split_pass("gather_pass_b", states[2], lands[0])
        if k == 3:
            lands = _split_wait("gather_wait_b", states[2], xs)
            w_in8[2], w_out4[2] = lands[0], lands[1].reshape(4, FF_BLK, d)
            mla_wts = mla_weights(*lands[2:])
            states[3], token = _split_pass("gather_pass_c", states[3], lands[0])
        if k == 5:
            lands = _split_wait("gather_wait_c", states[3], xs)
            w_in8[3], w_out4[3] = lands[0], lands[1].reshape(4, FF_BLK, d)
        if sub != 1:
            u, extra = _ffn_fwd(tag, h, w_in8[2 * i + sub // 2], w_out4[2 * i + sub // 2], tf, tf // 2, after=token)
            token = None
        elif i == 0:
            u, z, v = _pool_fwd(h, w4, pool_b_all + token[0, 0], pscale_row, min(4 * tm, s))
            extra = (z, v)
            token = None
        else:
            u, extra = _mla_fwd(h, mla_wts, cos_k, sin_k, tm)
        saved.append((xs, h, u, extra))
        if k < 5:
            xs, h = _norm_link(f"norm_link_{k + 1}", xs, u, post_rows(k), weights_of[k], pre_rows(k + 1, True), act_dtype(k + 1), tr)

    dx, du, (sq, dgate, dgpost) = _norm_loss("norm_loss", xs, u, target, post_rows(5), weights_of[5], tr)
    loss_part = (0.5 * jnp.sum(sq) / d).reshape(1, 1)

    d_mod = [None] * (2 * n_mod)
    d_g = [None] * 12
    d_mod[3 * 5 + 2], d_g[2 * 5 + 1] = dgate, dgpost
    sent = {}
    pool_grads = mla_grads = None

    def start_scatter(key, arrays):
        state, token = _split_start(f"scatter_start_{key}", "scatter", arrays)
        sent[key] = state
        return token

    for k in (5, 4, 3, 2, 1, 0):
        i, sub = divmod(k, 3)
        tag = f"{i}{sub}"
        xin, h, u, extra = saved[k]
        token = None
        if sub != 1:
            f = 2 * i + sub // 2
            dh, dgu, act = _ffn_bwd_act(tag, du, extra, w_in8[f], w_out4[f], tf, tf // 4)
            dw_out = _ffn_dw(f"ffn_dwout_{tag}", act, du, tw).reshape(N_DEV, FF_BLK // 2, d)
            if k == 0:
                d_pool_w = pool_grads[0].reshape(4, N_DEV, POOL_SHARD, POOL_GROUP).transpose(1, 0, 2, 3).reshape(N_DEV, -1, POOL_GROUP)
                token = start_scatter(tag + "_out", [dw_out, d_pool_w])
                token = start_scatter(tag + "_in", [_ffn_dw(f"ffn_dwin_{tag}", dgu, h, tw, after=token)])
            else:
                token = start_scatter(tag, [_ffn_dw(f"ffn_dwin_{tag}", dgu, h, tw), dw_out])
        elif i == 0:
            dh, dw4, (dpscale, dpb) = _pool_bwd(du, extra[0], extra[1], w4, pscale_row, min(4 * tm, s))
            pool_grads = (dw4, dpscale, dpb)
        else:
            dh, mla_grads = _mla_bwd(du, h, extra, mla_wts, cos_k, sin_k, tm)
            dwq = mla_grads["wq"]
            d_uq = jnp.concatenate([dwq[:, :, :QK_NOPE], dwq[:, :, KV_LORA:KV_LORA + QK_ROPE]], axis=2).transpose(1, 0, 2)
            token = start_scatter("mla", [mla_grads["w_in"][:, :mla_w_in.shape[2]].reshape(N_DEV, d // N_DEV, -1),
                                          d_uq.reshape(N_DEV, Q_LORA // N_DEV, -1), mla_grads["w_o"].reshape(N_DEV, d // N_DEV, d)])
            dwukp, dwuv2 = mla_grads["wukp"], mla_grads["wuv2"]
            d_uk = dwukp[:, :QK_NOPE, :KV_LORA].transpose(2, 0, 1).reshape(KV_LORA, -1)
            d_uv = jnp.where(even, dwuv2[:, :, :V_HEAD], dwuv2[:, :, V_HEAD:]).transpose(1, 0, 2).reshape(KV_LORA, -1)
            state_ukv, token = _split_start("gather_start_ukv", "gather", [d_uk, d_uv], after=token)
        if k > 0:
            dx, du, reds = _norm_link_bwd(f"norm_link_bwd_{k}", dh, xin, dx, saved[k - 1][2], pre_rows(k, False), post_rows(k - 1),
                                          weights_of[k - 1], act_dtype(k - 1), tr, after=token)
            d_mod[3 * (k - 1) + 2], d_g[2 * (k - 1) + 1] = reds[3], reds[4]
        else:
            dx, reds = _prenorm_bwd("prenorm_bwd_0", dh, xin, dx, pre_rows(0, False), tr, after=token)
        d_mod[3 * k], d_mod[3 * k + 1], d_g[2 * k] = reds[0], reds[1], reds[2]
    grad_x = dx.reshape(x.shape)

    def upd(name, parts, w, m, v):
        shape = w.shape
        r, cdim = parts.shape[1], parts.shape[2]
        return [o.reshape(shape) for o in _adamw(name, parts, w.reshape(r, cdim), m.reshape(r, cdim), v.reshape(r, cdim))]

    def landed(key, after):
        return _split_wait(f"scatter_wait_{key}", sent[key], after)

    res = {}
    w_in_s, m_in_s, v_in_s = (a.reshape(4, FF_BLK, d) for a in (w_in_t, m_in_t, v_in_t))
    w_out_s, m_out_s, v_out_s = (a.reshape(4, FF_BLK // 2, d) for a in (ffn_w_out, m_ffn_w_out, v_ffn_w_out))
    bufs_in = [lax.empty(w_in_s.shape, F32) for _ in range(4)]
    bufs_out = [lax.empty(w_out_s.shape, F32) for _ in range(4)]
    for key, k in (("12", 3), ("mla", None), ("10", 2), ("02", 1)):
        parts = landed(key, grad_x)
        if k is None:
            res["mla_w_in"] = upd("adam_mla_w_in", parts[0], mla_w_in, m_mla_w_in, v_mla_w_in)
            res["mla_w_uq"] = upd("adam_mla_w_uq", parts[1], mla_w_uq, m_mla_w_uq, v_mla_w_uq)
            res["mla_w_o"] = upd("adam_mla_w_o", parts[2], mla_w_o, m_mla_w_o, v_mla_w_o)
            uk_all, uv_all = _split_wait("gather_wait_ukv", state_ukv, grad_x)
            res["mla_w_uk"] = upd("adam_mla_w_uk", uk_all, mla_w_uk, m_mla_w_uk, v_mla_w_uk)
            res["mla_w_uv"] = upd("adam_mla_w_uv", uv_all, mla_w_uv, m_mla_w_uv, v_mla_w_uv)
            continue
        bufs_in = _adamw_slab(f"adam_ffn_w_in_{key}", parts[0], w_in_s, m_in_s, v_in_s, bufs_in, k)
        bufs_out = _adamw_slab(f"adam_ffn_w_out_{key}", parts[1], w_out_s, m_out_s, v_out_s, bufs_out, k)

    dw4, dpscale, dpb = pool_grads
    small_g = jnp.concatenate(d_mod + d_g + [dpb, dpscale, mla_grads["q_norm"], mla_grads["kv_norm"], loss_part], axis=1)
    done = [bufs_in[0], bufs_out[0], res["mla_w_o"][0], res["mla_w_uv"][0]]
    (small_g_all,) = _all_gather("gather_small_grads", [small_g], after=sum(a.reshape(-1)[:1] for a in done))
    small_g_all = small_g_all.reshape(N_DEV, -1)
    n_m = 2 * n_mod * d
    d_mod_all = small_g_all[:, :n_m].reshape(N_DEV, 2, n_mod * d)
    rest = small_g_all[:, n_m:]
    p_norm_g = lax.dynamic_slice_in_dim(rest[:, :12 * d].reshape(N_DEV, 12, d), me * (d // N_DEV), d // N_DEV, axis=2)
    p_pool_b = lax.dynamic_slice_in_dim(rest[:, 12 * d:13 * d].reshape(N_DEV, 4, POOL_GROUP), me * POOL_SHARD, POOL_SHARD, axis=2)
    p_pool_scale = rest[:, 13 * d:14 * d].reshape(N_DEV, 1, d)
    p_q_norm = lax.dynamic_slice_in_dim(rest[:, 14 * d:14 * d + Q_LORA], me * Q_SHARD, Q_SHARD, axis=1).reshape(N_DEV, 1, Q_SHARD)
    p_kv_norm = rest[:, 14 * d + Q_LORA:14 * d + Q_LORA + KV_LORA].reshape(N_DEV, 1, KV_LORA)
    loss = jnp.sum(rest[:, -1])

    d_mod_loc = lax.dynamic_slice_in_dim(d_mod_all, me * mod_blk, mod_blk, axis=2).transpose(1, 0, 2)
    k_pad = 128
    sc_t = jnp.pad(sc_all.T, ((0, 0), (0, k_pad - N_DEV)))
    d_ada_w = _matmul("ada_dw", sc_t, jnp.pad(d_mod_loc, ((0, 0), (0, k_pad - N_DEV), (0, 0))), a_blk=(d, k_pad),
                      a_map=lambda i, k: (0, 0), b_blk=(None, k_pad, mod_blk), b_map=lambda i, k: (i, 0, 0), o_shape=(2, d, mod_blk),
                      o_blk=(None, d, mod_blk), o_map=lambda i, k: (i, 0, 0), grid=(2, 1), contract=NN, out_dtype=F32)
    res["ada_w"] = upd("adam_ada_w", d_ada_w.reshape(1, 2 * d, mod_blk), ada_w, m_ada_w, v_ada_w)
    res["ada_b"] = upd("adam_ada_b", d_mod_all.reshape(N_DEV, 2, n_mod * d), ada_b, m_ada_b, v_ada_b)
    res["norm_g"] = upd("adam_norm_g", p_norm_g, norm_g, m_norm_g, v_norm_g)
    res["pool_b"] = upd("adam_pool_b", p_pool_b, pool_b, m_pool_b, v_pool_b)
    res["pool_scale"] = upd("adam_pool_scale", p_pool_scale, pool_scale, m_pool_scale, v_pool_scale)
    res["mla_q_norm"] = upd("adam_mla_q_norm", p_q_norm, mla_q_norm, m_mla_q_norm, v_mla_q_norm)
    res["mla_kv_norm"] = upd("adam_mla_kv_norm", p_kv_norm, mla_kv_norm, m_mla_kv_norm, v_mla_kv_norm)

    p_out, p_pool_w = landed("00_out", res["ada_w"][1])
    bufs_out = _adamw_slab("adam_ffn_w_out_00", p_out, w_out_s, m_out_s, v_out_s, bufs_out, 0)
    res["pool_w"] = upd("adam_pool_w", p_pool_w, pool_w, m_pool_w, v_pool_w)
    (p_in,) = landed("00_in", res["pool_w"][1])
    bufs_in = _adamw_slab("adam_ffn_w_in_00", p_in, w_in_s, m_in_s, v_in_s, bufs_in, 0)
    res["ffn_w_in"] = [jnp.swapaxes(b.reshape(w_in_t.shape), 2, 3) for b in bufs_in]
    res["ffn_w_out"] = [b.reshape(ffn_w_out.shape) for b in bufs_out]

    order = ["ada_w", "ada_b", "norm_g", "ffn_w_in", "ffn_w_out", "pool_w", "pool_b", "pool_scale", "mla_w_in", "mla_q_norm",
             "mla_kv_norm", "mla_w_uq", "mla_w_uk", "mla_w_uv", "mla_w_o"]
    outs = [loss, grad_x]
    for j in range(4):
        outs += [res[name][j] for name in order]
    return tuple(outs)
```

```python
import jax
import jax.numpy as jnp
from jax import lax
from jax.experimental import pallas as pl
from jax.experimental.pallas import tpu as pltpu

F32 = jnp.float32
BF16 = jnp.bfloat16
N_DEV = 8
MESH = pl.DeviceIdType.MESH

D_MODEL = 1024
N_HEADS = 16
QK_NOPE = 64
QK_ROPE = 32
V_HEAD = 64
Q_LORA = 256
KV_LORA = 128
LAT_PAD = 512
QK_PAD = 256
ONE_COL = 160
D_FF = 2816
FF_BLK = 2 * D_FF // N_DEV
POOL_WINDOWS = (2, 4, 8, 16)
POOL_GROUP = 256
POOL_SHARD = POOL_GROUP // N_DEV
Q_SHARD = Q_LORA // N_DEV
ROPE_THETA = 10000.0
EPS = 1e-6
ATTN_SCALE = (QK_NOPE + QK_ROPE) ** -0.5
LOG2_E = 1.4426950408889634
ADAM_LR, ADAM_B1, ADAM_B2, ADAM_EPS, ADAM_WD, ADAM_STEP = 0.001, 0.9, 0.999, 1e-08, 0.01, 10
VMEM_LIMIT = 56 * 1024 * 1024

NN = ((1,), (0,))
NT = ((1,), (1,))
TN = ((0,), (0,))


def _params(**kw):
    return pltpu.CompilerParams(vmem_limit_bytes=VMEM_LIMIT, **kw)


def _dot(a, b, contract):
    return lax.dot_general(a, b, (contract, ((), ())), preferred_element_type=F32)


def _matmul(name, a, b, *, a_blk, a_map, b_blk, b_map, o_shape, o_blk, o_map, grid, contract, out_dtype,
            bias=None, bias_blk=None, bias_map=None, after=None):
    n_k = grid[-1]
    k_axis = len(grid) - 1
    acc_shape = tuple(d for d in o_blk if d is not None)

    def body(*refs):
        a_ref, b_ref = refs[:2]
        bias_ref = refs[2] if bias is not None else None
        if n_k == 1:
            r = _dot(a_ref[...].astype(BF16), b_ref[...].astype(BF16), contract)
            if bias is not None:
                r = r + bias_ref[...]
            refs[-1][...] = r.astype(refs[-1].dtype)
            return
        o_ref, acc = refs[-2:]
        k = pl.program_id(k_axis)

        @pl.when(k == 0)
        def _():
            acc[...] = jnp.zeros_like(acc)

        acc[...] += _dot(a_ref[...].astype(BF16), b_ref[...].astype(BF16), contract)

        @pl.when(k == n_k - 1)
        def _():
            r = acc[...]
            if bias is not None:
                r = r + bias_ref[...]
            o_ref[...] = r.astype(o_ref.dtype)

    in_specs = [pl.BlockSpec(a_blk, a_map), pl.BlockSpec(b_blk, b_map)]
    args = [a, b]
    if bias is not None:
        in_specs.append(pl.BlockSpec(bias_blk, bias_map))
        args.append(bias)
    if after is not None:
        in_specs.append(pl.BlockSpec(memory_space=pl.ANY))
        args.append(after)
    return pl.pallas_call(
        body, name=name, grid=grid, in_specs=in_specs, out_specs=pl.BlockSpec(o_blk, o_map),
        out_shape=jax.ShapeDtypeStruct(o_shape, out_dtype), scratch_shapes=[pltpu.VMEM(acc_shape, F32)] if n_k > 1 else [],
        compiler_params=_params(),
    )(*args)


def _rowmap(name, fn, ins, outs, reds, grid, after=None):
    n_in, n_out, n_red = len(ins), len(outs), len(reds)
    extra = [] if after is None else [after]

    def body(*refs):
        in_refs = refs[:n_in]
        out_refs = refs[n_in + len(extra):n_in + len(extra) + n_out]
        red_refs = refs[n_in + len(extra) + n_out:]
        out_vals, red_vals = fn(*[r[...] for r in in_refs])
        for r, v in zip(out_refs, out_vals):
            r[...] = v.astype(r.dtype)
        if n_red:
            first = pl.program_id(0) == 0
            for ax in range(1, len(grid)):
                first = jnp.logical_and(first, pl.program_id(ax) == 0)

            @pl.when(first)
            def _():
                for r in red_refs:
                    r[...] = jnp.zeros_like(r)

            for r, v in zip(red_refs, red_vals):
                r[...] += v

    res = pl.pallas_call(
        body, name=name, grid=grid,
        in_specs=[pl.BlockSpec(blk, imap) for _, blk, imap in ins] + [pl.BlockSpec(memory_space=pl.ANY)] * len(extra),
        out_specs=[pl.BlockSpec(blk, imap) for _, blk, imap in list(outs) + list(reds)],
        out_shape=[sds for sds, _, _ in list(outs) + list(reds)],
        compiler_params=_params(),
    )(*[a for a, _, _ in ins], *extra)
    return res[:n_out], res[n_out:]


def _sds(shape, dtype):
    return jax.ShapeDtypeStruct(shape, dtype)


def _tile(a, tm):
    return (a, (tm, a.shape[1]), lambda i: (i, 0))


def _row(a):
    return (a, (1, a.shape[1]), lambda i: (0, 0))


def _otile(n, c, dtype, tm):
    return (_sds((n, c), dtype), (tm, c), lambda i: (i, 0))


def _ored(c):
    return (_sds((1, c), F32), (1, c), lambda i: (0, 0))


def _colsum(v):
    return jnp.sum(v, axis=0, keepdims=True)


def _rstd(v):
    return lax.rsqrt(jnp.mean(v * v, axis=-1, keepdims=True) + EPS)


def _pre(xv, g, sc, sh):
    return xv * _rstd(xv) * g * (1.0 + sc) + sh


def _post(xv, uv, g, gt, weight):
    uv = uv.astype(F32)
    return xv + weight * (1.0 + gt) * (uv * _rstd(uv) * g)


def _post_bwd(dv, uv, g, gt, weight):
    uv = uv.astype(F32)
    r = _rstd(uv)
    un = uv * r
    dy = dv * (weight * (1.0 + gt))
    a = dy * g
    du = r * (a - un * jnp.mean(a * un, axis=-1, keepdims=True))
    return du, (_colsum(dv * (weight * (un * g))), _colsum(dy * un))


def _pre_bwd(dhv, xv, dv, g, sc):
    dhv = dhv.astype(F32)
    r = _rstd(xv)
    xn = xv * r
    b = dhv * (g * (1.0 + sc))
    dx = dv + r * (b - xn * jnp.mean(b * xn, axis=-1, keepdims=True))
    return dx, (_colsum(dhv), _colsum(dhv * (xn * g)), _colsum(dhv * ((1.0 + sc) * xn)))


def _rowk(rows, k):
    return (rows, (None, 1, rows.shape[2]), lambda i: (k, 0, 0))


def _prenorm(name, x, pre, out_dtype, tm):
    n, d = x.shape
    (h,), _ = _rowmap(name, lambda xv, g, sc, sh: ((_pre(xv, g, sc, sh),), ()), [_tile(x, tm), *pre], [_otile(n, d, out_dtype, tm)],
                      [], (n // tm,))
    return h


def _norm_link(name, x, u, post, weight, pre, out_dtype, tm):
    n, d = x.shape

    def fn(xv, uv, g, gt, g2, sc, sh):
        xn = _post(xv, uv, g, gt, weight)
        return (xn, _pre(xn, g2, sc, sh)), ()

    (xn, h), _ = _rowmap(name, fn, [_tile(x, tm), _tile(u, tm), *post, *pre], [_otile(n, d, F32, tm), _otile(n, d, out_dtype, tm)],
                         [], (n // tm,))
    return xn, h


def _norm_loss(name, x, u, target, post, weight, tm):
    n, d = x.shape

    def fn(xv, uv, tv, g, gt):
        e = _post(xv, uv, g, gt, weight) - tv
        dv = e * (1.0 / d)
        du, reds = _post_bwd(dv, uv, g, gt, weight)
        return (dv, du), (_colsum(e * e), *reds)

    (dx, du), reds = _rowmap(name, fn, [_tile(x, tm), _tile(u, tm), _tile(target, tm), *post],
                             [_otile(n, d, F32, tm), _otile(n, d, BF16, tm)], [_ored(d)] * 3, (n // tm,))
    return dx, du, reds


def _norm_link_bwd(name, dh, x, dout, u_prev, pre, post_prev, weight_prev, out_dtype, tm, after=None):
    n, d = x.shape

    def fn(dhv, xv, dv, uv, g, sc, g2, gt):
        dx, reds = _pre_bwd(dhv, xv, dv, g, sc)
        du, reds_prev = _post_bwd(dx, uv, g2, gt, weight_prev)
        return (dx, du), (*reds, *reds_prev)

    (dx, du), reds = _rowmap(name, fn, [_tile(dh, tm), _tile(x, tm), _tile(dout, tm), _tile(u_prev, tm), *pre, *post_prev],
                             [_otile(n, d, F32, tm), _otile(n, d, out_dtype, tm)], [_ored(d)] * 5, (n // tm,), after=after)
    return dx, du, reds


def _prenorm_bwd(name, dh, x, dout, pre, tm, after=None):
    n, d = x.shape

    def fn(dhv, xv, dv, g, sc):
        dx, reds = _pre_bwd(dhv, xv, dv, g, sc)
        return (dx,), reds

    (dx,), reds = _rowmap(name, fn, [_tile(dh, tm), _tile(x, tm), _tile(dout, tm), *pre], [_otile(n, d, F32, tm)], [_ored(d)] * 3,
                          (n // tm,), after=after)
    return dx, reds


def _ffn_fwd(tag, h, w_in8, w_out4, tm, sub, after=None):
    s, d = h.shape

    extra = [] if after is None else [after]

    def body(h_ref, wg_ref, wu_ref, wo_ref, *rest):
        u_ref, gu_ref, acc = rest[-3:]
        j = pl.program_id(1)

        @pl.when(j == 0)
        def _():
            acc[...] = jnp.zeros_like(acc)

        for r in range(tm // sub):
            rows = pl.ds(r * sub, sub)
            hv = h_ref[rows, :]
            gate = _dot(hv, wg_ref[...], NT)
            up = _dot(hv, wu_ref[...], NT)
            gu_ref[0, rows, :] = gate.astype(BF16)
            gu_ref[1, rows, :] = up.astype(BF16)
            acc[rows, :] += _dot((gate * jax.nn.sigmoid(gate) * up).astype(BF16), wo_ref[...], NN)

        @pl.when(j == w_out4.shape[0] - 1)
        def _():
            u_ref[...] = acc[...].astype(u_ref.dtype)

    w_blk = (None, FF_BLK, d)
    return pl.pallas_call(
        body, name=f"ffn_fwd_{tag}", grid=(s // tm, 4),
        in_specs=[pl.BlockSpec((tm, d), lambda i, j: (i, 0)), pl.BlockSpec(w_blk, lambda i, j: (j, 0, 0)),
                  pl.BlockSpec(w_blk, lambda i, j: (j + 4, 0, 0)), pl.BlockSpec((None, FF_BLK, d), lambda i, j: (j, 0, 0))]
        + [pl.BlockSpec(memory_space=pl.ANY)] * len(extra),
        out_specs=[pl.BlockSpec((tm, d), lambda i, j: (i, 0)), pl.BlockSpec((2, None, tm, FF_BLK), lambda i, j: (0, j, i, 0))],
        out_shape=[_sds((s, d), BF16), _sds((2, 4, s, FF_BLK), BF16)], scratch_shapes=[pltpu.VMEM((tm, d), F32)],
        compiler_params=_params(),
    )(h, w_in8, w_in8, w_out4, *extra)


def _ffn_bwd_act(tag, du, gu, w_in8, w_out4, tm, sub):
    s, d = du.shape

    def body(du_ref, gu_ref, wg_ref, wu_ref, wo_ref, dh_ref, dgu_ref, act_ref, acc):
        j = pl.program_id(1)

        @pl.when(j == 0)
        def _():
            acc[...] = jnp.zeros_like(acc)

        n_sub = tm // sub
        dact_next = _dot(du_ref[pl.ds(0, sub), :], wo_ref[...], NT)
        for r in range(n_sub):
            rows = pl.ds(r * sub, sub)
            dact = dact_next
            if r + 1 < n_sub:
                dact_next = _dot(du_ref[pl.ds((r + 1) * sub, sub), :], wo_ref[...], NT)
            gate, up = gu_ref[0, rows, :].astype(F32), gu_ref[1, rows, :].astype(F32)
            sg = jax.nn.sigmoid(gate)
            silu = gate * sg
            dg = dact * up * (sg * (1.0 + gate * (1.0 - sg)))
            dup = dact * silu
            dgu_ref[0, :, rows] = dg.T.astype(BF16)
            dgu_ref[1, :, rows] = dup.T.astype(BF16)
            act_ref[:, rows] = (silu * up).T.astype(BF16)
            acc[rows, :] += _dot(dg.astype(BF16), wg_ref[...], NN) + _dot(dup.astype(BF16), wu_ref[...], NN)

        @pl.when(j == w_out4.shape[0] - 1)
        def _():
            dh_ref[...] = acc[...].astype(dh_ref.dtype)

    w_blk = (None, FF_BLK, d)
    dh, dgu_t, act_t = pl.pallas_call(
        body, name=f"ffn_bwd_{tag}", grid=(s // tm, 4),
        in_specs=[pl.BlockSpec((tm, d), lambda i, j: (i, 0)), pl.BlockSpec((2, None, tm, FF_BLK), lambda i, j: (0, j, i, 0)),
                  pl.BlockSpec(w_blk, lambda i, j: (j, 0, 0)), pl.BlockSpec(w_blk, lambda i, j: (j + 4, 0, 0)),
                  pl.BlockSpec((None, FF_BLK, d), lambda i, j: (j, 0, 0))],
        out_specs=[pl.BlockSpec((tm, d), lambda i, j: (i, 0)), pl.BlockSpec((2, None, FF_BLK, tm), lambda i, j: (0, j, 0, i)),
                   pl.BlockSpec((None, FF_BLK, tm), lambda i, j: (j, 0, i))],
        out_shape=[_sds((s, d), BF16), _sds((2, 4, FF_BLK, s), BF16), _sds((4, FF_BLK, s), BF16)],
        scratch_shapes=[pltpu.VMEM((tm, d), F32)], compiler_params=_params(),
    )(du, gu, w_in8, w_in8, w_out4)
    return dh, dgu_t.reshape(8, FF_BLK, s), act_t


def _ffn_dw(name, lhs_t, rhs, tk, after=None):
    n_g, _, s = lhs_t.shape
    d = rhs.shape[1]
    return _matmul(name, lhs_t, rhs, a_blk=(None, FF_BLK, tk), a_map=lambda g, k: (g, 0, k), b_blk=(tk, d), b_map=lambda g, k: (k, 0),
                   o_shape=(n_g, FF_BLK, d), o_blk=(None, FF_BLK, d), o_map=lambda g, k: (g, 0, 0), grid=(n_g, s // tk),
                   contract=NN, out_dtype=BF16, after=after)


def _window_sum(x, window, transpose):
    s = x.shape[0]
    t = lax.broadcasted_iota(jnp.int32, (s, 1), 0)
    half = window // 2
    cnt = jnp.minimum(t + half, s) - jnp.maximum(t - half, 0)
    inv = 1.0 / cnt.astype(F32)
    if transpose:
        x = x * inv
        offsets = range(-half + 1, half + 1)
    else:
        offsets = range(-half, half)
    acc = jnp.zeros_like(x)
    for o in offsets:
        shifted = x if o == 0 else pltpu.roll(x, (-o) % s, 0)
        valid = jnp.logical_and(t + o >= 0, t + o < s)
        acc = acc + jnp.where(valid, shifted, 0.0)
    return acc if transpose else acc * inv


def _pool_mix(name, x, transpose, out_dtype):
    s, d = x.shape

    def body(x_ref, o_ref):
        g = pl.program_id(0)
        for gi, window in enumerate(POOL_WINDOWS):
            @pl.when(g == gi)
            def _(window=window):
                xv = x_ref[...].astype(F32)
                o_ref[...] = (_window_sum(xv, window, transpose) - xv).astype(o_ref.dtype)

    return pl.pallas_call(
        body, name=name, grid=(len(POOL_WINDOWS),), in_specs=[pl.BlockSpec((s, POOL_GROUP), lambda g: (0, g))],
        out_specs=pl.BlockSpec((s, POOL_GROUP), lambda g: (0, g)), out_shape=_sds((s, d), out_dtype), compiler_params=_params(),
    )(x)


def _pool_fwd(h, w4, bias, pscale, tm):
    s, d = h.shape
    nt = s // tm
    z = _pool_mix("pool_mix", h, False, BF16)
    v = _matmul("pool_proj", z, w4, a_blk=(tm, POOL_GROUP), a_map=lambda i, g, k: (i, g), b_blk=(None, POOL_GROUP, POOL_GROUP),
                b_map=lambda i, g, k: (g, 0, 0), o_shape=(s, d), o_blk=(tm, POOL_GROUP), o_map=lambda i, g, k: (i, g),
                grid=(nt, 4, 1), contract=NN, out_dtype=F32, bias=bias, bias_blk=(1, POOL_GROUP), bias_map=lambda i, g, k: (0, g))
    tr = min(512, s)
    (u,), _ = _rowmap("pool_scale", lambda vv, ps: ((vv * ps,), ()), [_tile(v, tr), _row(pscale)], [_otile(s, d, BF16, tr)], [],
                      (s // tr,))
    return u, z, v


def _pool_bwd(du, z, v, w4, pscale, tm):
    s, d = du.shape
    nt = s // tm

    def fn(duv, vv, ps):
        dv = duv * ps
        return (dv,), (_colsum(duv * vv), _colsum(dv))

    tr = min(512, s)
    (dv,), reds = _rowmap("pool_dscale", fn, [_tile(du, tr), _tile(v, tr), _row(pscale)], [_otile(s, d, BF16, tr)],
                          [_ored(d), _ored(d)], (s // tr,))
    dw4 = _matmul("pool_dw", z, dv, a_blk=(tm, POOL_GROUP), a_map=lambda g, k: (k, g), b_blk=(tm, POOL_GROUP),
                  b_map=lambda g, k: (k, g), o_shape=(4, POOL_GROUP, POOL_GROUP), o_blk=(None, POOL_GROUP, POOL_GROUP),
                  o_map=lambda g, k: (g, 0, 0), grid=(4, nt), contract=TN, out_dtype=F32)
    dz = _matmul("pool_dz", dv, w4, a_blk=(tm, POOL_GROUP), a_map=lambda i, g, k: (i, g), b_blk=(None, POOL_GROUP, POOL_GROUP),
                 b_map=lambda i, g, k: (g, 0, 0), o_shape=(s, d), o_blk=(tm, POOL_GROUP), o_map=lambda i, g, k: (i, g),
                 grid=(nt, 4, 1), contract=NT, out_dtype=F32)
    dh = _pool_mix("pool_mix_t", dz, True, BF16)
    return dh, dw4, reds


def _lane(shape):
    return lax.broadcasted_iota(jnp.int32, shape, 1)


def _rope_swap(v, transpose):
    half = QK_ROPE // 2
    lane = _lane(v.shape)
    up = pltpu.roll(v, v.shape[1] - half, 1)
    down = pltpu.roll(v, half, 1)
    if transpose:
        return jnp.where(lane < half, up, jnp.where(lane < QK_ROPE, -down, 0.0))
    return jnp.where(lane < half, -up, jnp.where(lane < QK_ROPE, down, 0.0))


def _rope(v, cos, sin):
    return v * cos + _rope_swap(v, False) * sin


def _rope_t(g, cos, sin):
    return g * cos + _rope_swap(g * sin, True)


def _mla_mid(lat, q_norm, kv_norm, cos_k, sin_k, tm):
    s = lat.shape[0]

    def fn(lv, qn, kn, cs, sn):
        cq = lv[:, :Q_LORA]
        ckv = lv[:, Q_LORA:Q_LORA + KV_LORA]
        kr = lv[:, Q_LORA + KV_LORA:]
        cq = cq * _rstd(cq) * qn
        ckv = ckv * _rstd(ckv) * kn
        k_rope = jnp.where(_lane(kr.shape) == ONE_COL - KV_LORA, 1.0, _rope(kr, cs, sn))
        return (cq, jnp.concatenate([ckv, k_rope], axis=1)), ()

    (cq, kcat), _ = _rowmap("mla_mid", fn, [_tile(lat, tm), _row(q_norm), _row(kv_norm), _tile(cos_k, tm), _tile(sin_k, tm)],
                            [_otile(s, Q_LORA, BF16, tm), _otile(s, QK_PAD, BF16, tm)], [], (s // tm,))
    return cq, kcat


def _mla_mid_bwd(lat, dcq, dkcat, dv, q_norm, kv_norm, cos_k, sin_k, tm):
    s = lat.shape[0]

    def fn(lv, dq, dk, dvv, qn, kn, cs, sn):
        dk = dk * (1.0 / LOG2_E)
        cq = lv[:, :Q_LORA]
        ckv = lv[:, Q_LORA:Q_LORA + KV_LORA]
        rq, rk = _rstd(cq), _rstd(ckv)
        cqn, ckn = cq * rq, ckv * rk
        a = dq * qn
        d_cq = rq * (a - cqn * jnp.mean(a * cqn, axis=-1, keepdims=True))
        dckv = dk[:, :KV_LORA] + dvv
        a2 = dckv * kn
        d_ckv = rk * (a2 - ckn * jnp.mean(a2 * ckn, axis=-1, keepdims=True))
        d_kr = _rope_t(dk[:, KV_LORA:], cs, sn)
        return (jnp.concatenate([d_cq, d_ckv, d_kr], axis=1),), (_colsum(dq * cqn), _colsum(dckv * ckn))

    (dlat,), reds = _rowmap(
        "mla_mid_bwd", fn,
        [_tile(lat, tm), _tile(dcq, tm), _tile(dkcat, tm), _tile(dv, tm), _row(q_norm), _row(kv_norm), _tile(cos_k, tm),
         _tile(sin_k, tm)],
        [_otile(s, LAT_PAD, BF16, tm)], [_ored(Q_LORA), _ored(KV_LORA)], (s // tm,))
    return dlat, reds


def _mla_q(cq, wq, wukp, cos_k, sin_k, tm):
    s = cq.shape[0]

    def body(cq_ref, wq_ref, wuk_ref, cos_ref, sin_ref, o_ref):
        cqv, cs, sn = cq_ref[...], cos_ref[...], sin_ref[...]
        for h in range(N_HEADS):
            aq = _dot(cqv, wq_ref[h], NN)
            qlat = _dot(aq.astype(BF16), wuk_ref[h], NN)
            roped = _rope(aq[:, KV_LORA:], cs, sn)
            o_ref[h] = (jnp.concatenate([qlat[:, :KV_LORA], roped], axis=1) * (ATTN_SCALE * LOG2_E)).astype(o_ref.dtype)

    wblk = pl.BlockSpec((N_HEADS, QK_PAD, QK_PAD), lambda i: (0, 0, 0))
    tblk = pl.BlockSpec((tm, KV_LORA), lambda i: (i, 0))
    return pl.pallas_call(
        body, name="mla_q", grid=(s // tm,),
        in_specs=[pl.BlockSpec((tm, Q_LORA), lambda i: (i, 0)), wblk, wblk, tblk, tblk],
        out_specs=pl.BlockSpec((N_HEADS, tm, QK_PAD), lambda i: (0, i, 0)), out_shape=_sds((N_HEADS, s, QK_PAD), BF16),
        compiler_params=_params(),
    )(cq, wq, wukp, cos_k, sin_k)


def _mla_q_bwd(cq, wq, wukp, cos_k, sin_k, dqcat, tm):
    s = cq.shape[0]

    def body(cq_ref, wq_ref, wuk_ref, cos_ref, sin_ref, dq_ref, dcq_ref, dwq_ref, dwuk_ref):
        @pl.when(pl.program_id(0) == 0)
        def _():
            dwq_ref[...] = jnp.zeros_like(dwq_ref)
            dwuk_ref[...] = jnp.zeros_like(dwuk_ref)

        cqv, cs, sn = cq_ref[...], cos_ref[...], sin_ref[...]
        d_cq = jnp.zeros((tm, Q_LORA), F32)
        for h in range(N_HEADS):
            aq = _dot(cqv, wq_ref[h], NN).astype(BF16)
            g = dq_ref[h].astype(F32) * ATTN_SCALE
            gl, gr = g[:, :KV_LORA], g[:, KV_LORA:]
            dqlat = jnp.concatenate([gl, jnp.zeros_like(gl)], axis=1).astype(BF16)
            d_rope = _rope_t(gr, cs, sn)
            daq = _dot(dqlat, wuk_ref[h], NT) + jnp.concatenate([jnp.zeros_like(d_rope), d_rope], axis=1)
            daq_b = daq.astype(BF16)
            dwuk_ref[h] += _dot(aq, dqlat, TN)
            dwq_ref[h] += _dot(cqv, daq_b, TN)
            d_cq = d_cq + _dot(daq_b, wq_ref[h], NT)
        dcq_ref[...] = d_cq

    wblk = pl.BlockSpec((N_HEADS, QK_PAD, QK_PAD), lambda i: (0, 0, 0))
    tblk = pl.BlockSpec((tm, KV_LORA), lambda i: (i, 0))
    return pl.pallas_call(
        body, name="mla_q_bwd", grid=(s // tm,),
        in_specs=[pl.BlockSpec((tm, Q_LORA), lambda i: (i, 0)), wblk, wblk, tblk, tblk,
                  pl.BlockSpec((N_HEADS, tm, QK_PAD), lambda i: (0, i, 0))],
        out_specs=[pl.BlockSpec((tm, Q_LORA), lambda i: (i, 0)), wblk, wblk],
        out_shape=[_sds((s, Q_LORA), F32), _sds((N_HEADS, QK_PAD, QK_PAD), F32), _sds((N_HEADS, QK_PAD, QK_PAD), F32)],
        compiler_params=_params(),
    )(cq, wq, wukp, cos_k, sin_k, dqcat)


def _flash_fwd(qcat, kcat, tq, tk):
    n_h, s, _ = qcat.shape
    n_k = s // tk

    def body(q_ref, k_ref, o_ref, lse_ref):
        q = q_ref[...]
        m = jnp.full((tq, 1), -1e30, F32)
        acc = jnp.zeros((tq, QK_PAD), F32)
        for kk in range(n_k):
            k = k_ref[pl.ds(kk * tk, tk), :]
            sc = _dot(q, k, NT)
            m_new = jnp.maximum(m, jnp.max(sc, axis=1, keepdims=True))
            p = jnp.exp2(sc - m_new).astype(BF16)
            acc = jnp.exp2(m - m_new) * acc + _dot(p, k, NN)
            m = m_new
        l = jnp.sum(jnp.where(_lane(acc.shape) == ONE_COL, acc, 0.0), axis=1, keepdims=True)
        o_ref[...] = (acc[:, :KV_LORA] / l).astype(o_ref.dtype)
        lse_ref[...] = m + jnp.log2(l)

    return pl.pallas_call(
        body, name="mla_attn", grid=(n_h, s // tq),
        in_specs=[pl.BlockSpec((None, tq, QK_PAD), lambda h, i: (h, i, 0)), pl.BlockSpec((s, QK_PAD), lambda h, i: (0, 0))],
        out_specs=[pl.BlockSpec((None, tq, KV_LORA), lambda h, i: (h, i, 0)), pl.BlockSpec((None, tq, 1), lambda h, i: (h, i, 0))],
        out_shape=[_sds((n_h, s, KV_LORA), BF16), _sds((n_h, s, 1), F32)], compiler_params=_params(),
    )(qcat, kcat)


def _flash_bwd(qcat, kcat, o, do, lse, tq, tk):
    n_h, s, _ = qcat.shape
    n_k = s // tk

    def body(q_ref, k_ref, v_ref, o_ref, do_ref, lse_ref, dq_ref, dk_ref, dv_ref, dq_acc):
        h, i = pl.program_id(0), pl.program_id(1)

        @pl.when(jnp.logical_and(h == 0, i == 0))
        def _():
            dk_ref[...] = jnp.zeros_like(dk_ref)
            dv_ref[...] = jnp.zeros_like(dv_ref)

        q = q_ref[...]
        dov = do_ref[...]
        dov_t = dov.T
        q_t = q.T
        lse_v = lse_ref[...]
        delta = jnp.sum(dov.astype(F32) * o_ref[...].astype(F32), axis=1, keepdims=True)
        dq_acc[...] = jnp.zeros_like(dq_acc)

        for kk in range(n_k):
            rows = pl.ds(kk * tk, tk)
            k = k_ref[rows, :]
            p = jnp.exp2(_dot(q, k, NT) - lse_v)
            dp = _dot(dov, v_ref[rows, :], NT)
            ds = (p * (dp - delta)).astype(BF16)
            dq_acc[...] += _dot(ds, k, NN)
            dv_ref[:, rows] += _dot(dov_t, p.astype(BF16), NN)
            dk_ref[:, rows] += _dot(q_t, ds, NN)
        dq_ref[...] = dq_acc[...].astype(dq_ref.dtype)

    qblk = pl.BlockSpec((None, tq, QK_PAD), lambda h, i: (h, i, 0))
    oblk = pl.BlockSpec((None, tq, KV_LORA), lambda h, i: (h, i, 0))
    return pl.pallas_call(
        body, name="mla_attn_bwd", grid=(n_h, s // tq),
        in_specs=[qblk, pl.BlockSpec((s, QK_PAD), lambda h, i: (0, 0)), pl.BlockSpec((s, KV_LORA), lambda h, i: (0, 0)), oblk, oblk,
                  pl.BlockSpec((None, tq, 1), lambda h, i: (h, i, 0))],
        out_specs=[qblk, pl.BlockSpec((QK_PAD, s), lambda h, i: (0, 0)), pl.BlockSpec((KV_LORA, s), lambda h, i: (0, 0))],
        out_shape=[_sds((n_h, s, QK_PAD), BF16), _sds((QK_PAD, s), F32), _sds((KV_LORA, s), F32)],
        scratch_shapes=[pltpu.VMEM((tq, QK_PAD), F32)], compiler_params=_params(),
    )(qcat, kcat, kcat, o, do, lse)


def _mla_uv(o_lat, wuv2, do, tm):
    n_h, s, _ = o_lat.shape
    d = n_h * V_HEAD
    pair = 2 * V_HEAD
    lat_blk = pl.BlockSpec((n_h, tm, KV_LORA), lambda i: (0, i, 0))
    w_blk = pl.BlockSpec((n_h, KV_LORA, pair), lambda i: (0, 0, 0))
    row_blk = pl.BlockSpec((tm, d), lambda i: (i, 0))

    if do is None:
        def body(a_ref, w_ref, o_ref):
            for p in range(n_h // 2):
                o_ref[:, p * pair:(p + 1) * pair] = (
                    _dot(a_ref[2 * p], w_ref[2 * p], NN) + _dot(a_ref[2 * p + 1], w_ref[2 * p + 1], NN)).astype(o_ref.dtype)

        return pl.pallas_call(body, name="mla_uv", grid=(s // tm,), in_specs=[lat_blk, w_blk], out_specs=row_blk,
                              out_shape=_sds((s, d), BF16), compiler_params=_params())(o_lat, wuv2)

    def body(a_ref, w_ref, do_ref, dlat_ref, dw_ref):
        @pl.when(pl.program_id(0) == 0)
        def _():
            dw_ref[...] = jnp.zeros_like(dw_ref)

        for h in range(n_h):
            dov = do_ref[:, (h // 2) * pair:(h // 2 + 1) * pair]
            dlat_ref[h] = _dot(dov, w_ref[h], NT).astype(dlat_ref.dtype)
            dw_ref[h] += _dot(a_ref[h], dov, TN)

    return pl.pallas_call(body, name="mla_uv_bwd", grid=(s // tm,), in_specs=[lat_blk, w_blk, row_blk], out_specs=[lat_blk, w_blk],
                          out_shape=[_sds((n_h, s, KV_LORA), BF16), _sds((n_h, KV_LORA, pair), F32)], compiler_params=_params(),
                          )(o_lat, wuv2, do)


def _mla_fwd(h, wts, cos_k, sin_k, tm):
    s, d = h.shape
    nt = s // tm
    lat = _matmul("mla_lat", h, wts["w_in"], a_blk=(tm, d), a_map=lambda i, k: (i, 0), b_blk=(d, LAT_PAD), b_map=lambda i, k: (0, 0),
                  o_shape=(s, LAT_PAD), o_blk=(tm, LAT_PAD), o_map=lambda i, k: (i, 0), grid=(nt, 1), contract=NN, out_dtype=F32)
    cq, kcat = _mla_mid(lat, wts["q_norm"], wts["kv_norm"], cos_k, sin_k, tm)
    qcat = _mla_q(cq, wts["wq"], wts["wukp"], cos_k, sin_k, tm)
    o_lat, lse = _flash_fwd(qcat, kcat, min(4 * tm, s), tm)
    o = _mla_uv(o_lat, wts["wuv2"], None, tm)
    u = _matmul("mla_out", o, wts["w_o"], a_blk=(tm, d), a_map=lambda i, k: (i, 0), b_blk=(d, d), b_map=lambda i, k: (0, 0),
                o_shape=(s, d), o_blk=(tm, d), o_map=lambda i, k: (i, 0), grid=(nt, 1), contract=NN, out_dtype=BF16)
    return u, (lat, cq, kcat, qcat, o_lat, lse, o)


def _mla_bwd(du, h, saved, wts, cos_k, sin_k, tm):
    lat, cq, kcat, qcat, o_lat, lse, o = saved
    s, d = h.shape
    nt = s // tm
    do = _matmul("mla_do", du, wts["w_o"], a_blk=(tm, d), a_map=lambda i, k: (i, 0), b_blk=(d, d), b_map=lambda i, k: (0, 0),
                 o_shape=(s, d), o_blk=(tm, d), o_map=lambda i, k: (i, 0), grid=(nt, 1), contract=NT, out_dtype=BF16)
    dw_o = _matmul("mla_dwo", o, du, a_blk=(tm, d), a_map=lambda k: (k, 0), b_blk=(tm, d), b_map=lambda k: (k, 0),
                   o_shape=(d, d), o_blk=(d, d), o_map=lambda k: (0, 0), grid=(nt,), contract=TN, out_dtype=F32)
    do_lat, dwuv2 = _mla_uv(o_lat, wts["wuv2"], do, tm)
    dqcat, dkcat_t, dv_t = _flash_bwd(qcat, kcat, o_lat, do_lat, lse, min(2 * tm, s), tm)
    dkcat, dv = dkcat_t.T, dv_t.T
    dcq, dwq, dwukp = _mla_q_bwd(cq, wts["wq"], wts["wukp"], cos_k, sin_k, dqcat, tm)
    dlat, (dqn, dkn) = _mla_mid_bwd(lat, dcq, dkcat, dv, wts["q_norm"], wts["kv_norm"], cos_k, sin_k, tm)
    dh = _matmul("mla_dh", dlat, wts["w_in"], a_blk=(tm, LAT_PAD), a_map=lambda i, k: (i, 0), b_blk=(d, LAT_PAD),
                 b_map=lambda i, k: (0, 0), o_shape=(s, d), o_blk=(tm, d), o_map=lambda i, k: (i, 0), grid=(nt, 1), contract=NT,
                 out_dtype=BF16)
    dw_in = _matmul("mla_dwin", h, dlat, a_blk=(tm, d), a_map=lambda k: (k, 0), b_blk=(tm, LAT_PAD), b_map=lambda k: (k, 0),
                    o_shape=(d, LAT_PAD), o_blk=(d, LAT_PAD), o_map=lambda k: (0, 0), grid=(nt,), contract=TN, out_dtype=F32)
    return dh, dict(w_in=dw_in, wq=dwq, wukp=dwukp, wuv2=dwuv2, w_o=dw_o, q_norm=dqn, kv_norm=dkn)


def _adamw(name, parts, w, m, v):
    n_parts, r, c = parts.shape
    tr = r
    for cand in (256, 128, 64, 32, 16, 8):
        if r > cand and r % cand == 0:
            tr = cand
            break

    def body(p_ref, w_ref, m_ref, v_ref, g_ref, d_ref, nm_ref, nv_ref):
        g = p_ref[0].astype(F32)
        for k in range(1, n_parts):
            g = g + p_ref[k].astype(F32)
        nm = ADAM_B1 * m_ref[...] + (1.0 - ADAM_B1) * g
        nv = ADAM_B2 * v_ref[...] + (1.0 - ADAM_B2) * (g * g)
        m_hat = nm / (1.0 - ADAM_B1 ** ADAM_STEP)
        v_hat = nv / (1.0 - ADAM_B2 ** ADAM_STEP)
        g_ref[...] = g
        d_ref[...] = -ADAM_LR * (m_hat / (jnp.sqrt(v_hat) + ADAM_EPS) + ADAM_WD * w_ref[...])
        nm_ref[...] = nm
        nv_ref[...] = nv

    blk = pl.BlockSpec((tr, c), lambda i: (i, 0))
    return pl.pallas_call(
        body, name=name, grid=(r // tr,), in_specs=[pl.BlockSpec((n_parts, tr, c), lambda i: (0, i, 0)), blk, blk, blk],
        out_specs=[blk] * 4, out_shape=[_sds((r, c), F32)] * 4, compiler_params=_params(),
    )(parts, w, m, v)


def _adamw_slab(name, parts, w, m, v, bufs, f):
    n_parts, r, c = parts.shape
    tr = max(t for t in range(8, 257, 8) if r % t == 0)

    def body(p_ref, w_ref, m_ref, v_ref, *rest):
        g_ref, d_ref, nm_ref, nv_ref = rest[4:]
        g = p_ref[0].astype(F32)
        for k in range(1, n_parts):
            g = g + p_ref[k].astype(F32)
        nm = ADAM_B1 * m_ref[...] + (1.0 - ADAM_B1) * g
        nv = ADAM_B2 * v_ref[...] + (1.0 - ADAM_B2) * (g * g)
        m_hat = nm / (1.0 - ADAM_B1 ** ADAM_STEP)
        v_hat = nv / (1.0 - ADAM_B2 ** ADAM_STEP)
        g_ref[...] = g
        d_ref[...] = -ADAM_LR * (m_hat / (jnp.sqrt(v_hat) + ADAM_EPS) + ADAM_WD * w_ref[...])
        nm_ref[...] = nm
        nv_ref[...] = nv

    blk = pl.BlockSpec((None, tr, c), lambda i: (f, i, 0))
    return pl.pallas_call(
        body, name=name, grid=(r // tr,),
        in_specs=[pl.BlockSpec((n_parts, tr, c), lambda i: (0, i, 0)), blk, blk, blk] + [pl.BlockSpec(memory_space=pl.ANY)] * 4,
        out_specs=[blk] * 4, out_shape=[_sds(w.shape, F32)] * 4, input_output_aliases={4 + j: j for j in range(4)},
        compiler_params=_params(),
    )(parts, w, m, v, *bufs)


def _mesh_pos():
    return lax.axis_index("x"), lax.axis_index("y"), lax.axis_index("c")


def _flip(pos, mask):
    return tuple(1 - p if (mask >> (2 - b)) & 1 else p for b, p in enumerate(pos))


def _index(pos):
    return 4 * pos[0] + 2 * pos[1] + pos[2]


def _all_gather(name, xs, after=None):
    n = len(xs)
    extra = [] if after is None else [after]

    def body(*refs):
        x_refs, o_refs = refs[:n], refs[n + len(extra):2 * n + len(extra)]
        send_sems, recv_sems, local_sems = refs[2 * n + len(extra):]
        me = _mesh_pos()
        sibling = _flip(me, 1)
        others = [_flip(me, 4), _flip(me, 2), _flip(me, 6)]

        def copy(k, j, block, to, src=None):
            dst = o_refs[k].at[_index(block)]
            return pltpu.make_async_remote_copy(
                src_ref=dst if src is None else src, dst_ref=dst, send_sem=send_sems.at[k, j], recv_sem=recv_sems.at[k, j],
                device_id=to, device_id_type=MESH)

        local = [pltpu.make_async_copy(x_refs[k], o_refs[k].at[_index(me)], local_sems.at[k]) for k in range(n)]
        for cp in local:
            cp.start()
        first = []
        for k in range(n):
            first.append(copy(k, 0, me, sibling, src=x_refs[k]))
            first += [copy(k, 1 + j, me, other, src=x_refs[k]) for j, other in enumerate(others)]
        for cp in first:
            cp.start()
        passed = []
        for j, other in enumerate(others):
            for k in range(n):
                copy(k, 1 + j, other, me).wait_recv()
                cp = copy(k, 4 + j, other, sibling)
                cp.start()
                passed.append(cp)
        for k in range(n):
            copy(k, 0, sibling, me).wait_recv()
        for j, other in enumerate(others):
            for k in range(n):
                copy(k, 4 + j, _flip(other, 1), me).wait_recv()
        for cp in first + passed:
            cp.wait_send()
        for cp in local:
            cp.wait()

    any_spec = pl.BlockSpec(memory_space=pl.ANY)
    return pl.pallas_call(
        body, name=name, in_specs=[any_spec] * (n + len(extra)), out_specs=[any_spec] * n,
        out_shape=[_sds((N_DEV,) + x.shape, x.dtype) for x in xs],
        scratch_shapes=[pltpu.SemaphoreType.DMA((n, 7)), pltpu.SemaphoreType.DMA((n, 7)), pltpu.SemaphoreType.DMA((n,))],
    )(*xs, *extra)


_MASKS = {
    "gather": tuple(range(1, N_DEV)),
    "scatter": tuple(range(1, N_DEV)),
    "own": (1, 4, 2, 6),
    "pass": (4, 2, 6),
}


def _split_copies(kind, outgoing, x_refs, land_refs, send_sems, recv_sems):
    me = _mesh_pos()
    masks = _MASKS[kind]
    copies = []
    for k, (x_ref, land_ref) in enumerate(zip(x_refs, land_refs)):
        for j, mask in enumerate(masks):
            if kind == "pass":
                peer = _flip(me, 1)
                src = land_ref.at[_index(_flip(me, mask))]
                dst = land_ref.at[_index(_flip(me, mask if outgoing else mask | 1))]
            else:
                peer = _flip(me, mask)
                src = x_ref.at[_index(peer)] if kind == "scatter" else x_ref
                dst = land_ref.at[_index(me if outgoing else peer)]
            sem = k * len(masks) + j
            copies.append(pltpu.make_async_remote_copy(src_ref=src, dst_ref=dst, send_sem=send_sems.at[sem], recv_sem=recv_sems.at[sem],
                                                       device_id=peer, device_id_type=MESH))
    return copies


_HBM_SPEC = pl.BlockSpec(memory_space=pltpu.HBM)
_SEM_SPEC = pl.BlockSpec(memory_space=pltpu.SEMAPHORE)
_EFFECT = pltpu.SideEffectType.DATAFLOW_SIDE_EFFECTING


def _split_start(name, kind, xs, after=None):
    n = len(xs)
    n_sem = n * len(_MASKS[kind])
    extra = [] if after is None else [after]
    lands = [lax.empty(x.shape if kind == "scatter" else (N_DEV,) + x.shape, x.dtype) for x in xs]

    def body(*refs):
        x_refs, land_refs = refs[:n], refs[n:2 * n]
        send_sems, recv_sems = refs[2 * n + len(extra)], refs[2 * n + len(extra) + 1]
        token = refs[-1]
        for cp in _split_copies(kind, True, x_refs, land_refs, send_sems, recv_sems):
            cp.start()
        token[...] = jnp.zeros_like(token)

    hbm = [pltpu.HBM(a.shape, a.dtype) for a in list(xs) + lands]
    res = pl.pallas_call(
        body, name=name,
        out_shape=[pltpu.SemaphoreType.DMA((n_sem,)), pltpu.SemaphoreType.DMA((n_sem,))] + hbm + [_sds((8, 128), F32)],
        in_specs=[_HBM_SPEC] * (2 * n) + [pl.BlockSpec(memory_space=pl.ANY)] * len(extra),
        out_specs=[_SEM_SPEC, _SEM_SPEC] + [_HBM_SPEC] * (2 * n) + [pl.BlockSpec(memory_space=pltpu.VMEM)],
        input_output_aliases={j: 2 + j for j in range(2 * n)}, compiler_params=pltpu.CompilerParams(has_side_effects=_EFFECT),
    )(*[pltpu.with_memory_space_constraint(a, pltpu.HBM) for a in list(xs) + lands], *extra)
    return (kind, n, res[0], res[1], res[2:2 + 2 * n]), res[-1]


def _split_pass(name, state, after):
    kind, n, send_sems_in, recv_sems_in, thru = state
    n_sem = n * len(_MASKS["pass"])

    def body(*refs):
        x_refs, land_refs = refs[:n], refs[n:2 * n]
        send_sems, recv_sems = refs[2 * n], refs[2 * n + 1]
        next_send, next_recv, token = refs[-3:]
        for cp in _split_copies(kind, True, x_refs, land_refs, send_sems, recv_sems):
            cp.wait_send()
        for cp in _split_copies(kind, False, x_refs, land_refs, send_sems, recv_sems):
            cp.wait_recv()
        for cp in _split_copies("pass", True, land_refs, land_refs, next_send, next_recv):
            cp.start()
        token[...] = jnp.zeros_like(token)

    res = pl.pallas_call(
        body, name=name,
        out_shape=[pltpu.HBM(a.shape, a.dtype) for a in thru] + [pltpu.SemaphoreType.DMA((n_sem,)), pltpu.SemaphoreType.DMA((n_sem,)),
                                                                 _sds((8, 128), F32)],
        in_specs=[_HBM_SPEC] * (2 * n) + [_SEM_SPEC, _SEM_SPEC, pl.BlockSpec(memory_space=pl.ANY)],
        out_specs=[_HBM_SPEC] * (2 * n) + [_SEM_SPEC, _SEM_SPEC, pl.BlockSpec(memory_space=pltpu.VMEM)],
        input_output_aliases={j: j for j in range(2 * n)}, compiler_params=pltpu.CompilerParams(has_side_effects=_EFFECT),
    )(*thru, send_sems_in, recv_sems_in, after)
    return ("pass", n, res[2 * n], res[2 * n + 1], res[:2 * n]), res[-1]


def _split_wait(name, state, after):
    kind, n, send_sems_in, recv_sems_in, thru = state

    def body(*refs):
        x_refs, land_refs = refs[:n], refs[n:2 * n]
        send_sems, recv_sems = refs[2 * n], refs[2 * n + 1]
        for cp in _split_copies(kind, True, x_refs, land_refs, send_sems, recv_sems):
            cp.wait_send()
        for cp in _split_copies(kind, False, x_refs, land_refs, send_sems, recv_sems):
            cp.wait_recv()

    res = pl.pallas_call(
        body, name=name, out_shape=[pltpu.HBM(a.shape, a.dtype) for a in thru],
        in_specs=[_HBM_SPEC] * (2 * n) + [_SEM_SPEC, _SEM_SPEC, pl.BlockSpec(memory_space=pl.ANY)], out_specs=[_HBM_SPEC] * (2 * n),
        input_output_aliases={j: j for j in range(2 * n)}, compiler_params=pltpu.CompilerParams(has_side_effects=_EFFECT),
    )(*thru, send_sems_in, recv_sems_in, after)
    me = _index(_mesh_pos())
    out = []
    for x, land in zip(res[:n], res[n:]):
        own = lax.dynamic_index_in_dim(x, me, 0, keepdims=False) if kind == "scatter" else x
        out.append(lax.dynamic_update_slice(land, own[None], (me,) + (0,) * own.ndim))
    return out


def _rope_tables(s):
    inv = 1.0 / (ROPE_THETA ** (jnp.arange(0, QK_ROPE, 2, dtype=F32) / QK_ROPE))
    ang = jnp.arange(s, dtype=F32)[:, None] * inv[None, :]
    pad = jnp.zeros((s, KV_LORA - QK_ROPE), F32)
    return (jnp.concatenate([jnp.cos(ang), jnp.cos(ang), pad], axis=1), jnp.concatenate([jnp.sin(ang), jnp.sin(ang), pad], axis=1))


def kernel(x, c, ada_w, ada_b, norm_g, ffn_w_in, ffn_w_out, pool_w, pool_b, pool_scale, mla_w_in, mla_q_norm, mla_kv_norm, mla_w_uq, mla_w_uk, mla_w_uv, mla_w_o, loss_target, m_ada_w, m_ada_b, m_norm_g, m_ffn_w_in, m_ffn_w_out, m_pool_w, m_pool_b, m_pool_scale, m_mla_w_in, m_mla_q_norm, m_mla_kv_norm, m_mla_w_uq, m_mla_w_uk, m_mla_w_uv, m_mla_w_o, v_ada_w, v_ada_b, v_norm_g, v_ffn_w_in, v_ffn_w_out, v_pool_w, v_pool_b, v_pool_scale, v_mla_w_in, v_mla_q_norm, v_mla_kv_norm, v_mla_w_uq, v_mla_w_uk, v_mla_w_uv, v_mla_w_o):
    s, d = x.shape[1], x.shape[2]
    tm = min(512, s)
    tr = min(512, s)
    tf = min(1024, s)
    tw = min(2048, s)
    me = 4 * lax.axis_index("x") + 2 * lax.axis_index("y") + lax.axis_index("c")
    x0 = x.reshape(s, d)
    target = loss_target.reshape(s, d)
    n_mod = ada_w.shape[2] * N_DEV // d
    mod_blk = ada_w.shape[2]

    small = jnp.concatenate([c.reshape(-1), norm_g.reshape(-1), pool_b.reshape(-1), mla_q_norm.reshape(-1)]).reshape(1, -1)
    w_in_t, m_in_t, v_in_t = (jnp.swapaxes(a, 2, 3) for a in (ffn_w_in, m_ffn_w_in, v_ffn_w_in))
    w_in_loc = [w_in_t[i, f].astype(BF16) for i in range(2) for f in range(2)]
    w_out_loc = [ffn_w_out[i, f].astype(BF16) for i in range(2) for f in range(2)]
    (small_all,) = _all_gather("gather_small", [small])
    small_all = small_all.reshape(N_DEV, -1)
    c_all = small_all[:, :d]
    off = d
    g_all = small_all[:, off:off + 12 * (d // N_DEV)].reshape(N_DEV, 2, 6, d // N_DEV).transpose(1, 2, 0, 3).reshape(2, 6, d)
    off += 12 * (d // N_DEV)
    pool_b_all = small_all[:, off:off + 4 * POOL_SHARD].reshape(N_DEV, 4, POOL_SHARD).transpose(1, 0, 2).reshape(1, d)
    off += 4 * POOL_SHARD
    q_norm_all = small_all[:, off:off + Q_SHARD].reshape(1, Q_LORA)
    kv_norm_row = mla_kv_norm.reshape(1, KV_LORA)
    pscale_row = pool_scale.reshape(1, d)

    even = (jnp.arange(N_HEADS) % 2 == 0)[:, None, None]
    cos_k, sin_k = _rope_tables(s)

    def mla_weights(mla_w_in_all, mla_w_uq_all, mla_w_o_all):
        uq = mla_w_uq_all.reshape(Q_LORA, N_HEADS, QK_NOPE + QK_ROPE).transpose(1, 0, 2)
        zq = jnp.zeros((N_HEADS, Q_LORA, QK_NOPE), BF16)
        wq = jnp.concatenate(
            [uq[:, :, :QK_NOPE], zq, uq[:, :, QK_NOPE:], jnp.zeros((N_HEADS, Q_LORA, QK_PAD - KV_LORA - QK_ROPE), BF16)], axis=2)
        wukp = jnp.pad(mla_w_uk[0].transpose(1, 2, 0).astype(BF16), ((0, 0), (0, QK_PAD - QK_NOPE), (0, QK_PAD - KV_LORA)))
        uv = mla_w_uv[0].transpose(1, 0, 2).astype(BF16)
        wuv2 = jnp.where(even, jnp.concatenate([uv, jnp.zeros_like(uv)], axis=2), jnp.concatenate([jnp.zeros_like(uv), uv], axis=2))
        return dict(w_in=jnp.pad(mla_w_in_all.reshape(d, -1), ((0, 0), (0, LAT_PAD - mla_w_in.shape[2]))), wq=wq, wukp=wukp,
                    wuv2=wuv2, w_o=mla_w_o_all.reshape(d, d), q_norm=q_norm_all, kv_norm=kv_norm_row)

    (sc_all,), _ = _rowmap("ada_silu", lambda cv: ((cv * jax.nn.sigmoid(cv),), ()), [(c_all, (N_DEV, d), lambda i: (0, 0))],
                           [(_sds((N_DEV, d), F32), (N_DEV, d), lambda i: (0, 0))], [], (1,))
    ada_b_loc = lax.dynamic_slice_in_dim(ada_b, me * mod_blk, mod_blk, axis=1).reshape(2, 1, mod_blk)
    m_pad = 2 * N_DEV
    modp = _matmul("ada_mod", jnp.pad(sc_all, ((0, m_pad - N_DEV), (0, 0))), ada_w, a_blk=(m_pad, d), a_map=lambda i, k: (0, 0),
                   b_blk=(None, d, mod_blk), b_map=lambda i, k: (i, 0, 0), o_shape=(2, m_pad, mod_blk), o_blk=(None, m_pad, mod_blk),
                   o_map=lambda i, k: (i, 0, 0), grid=(2, 1), contract=NN, out_dtype=F32, bias=ada_b_loc, bias_blk=(None, 1, mod_blk),
                   bias_map=lambda i, k: (i, 0, 0))[:, :N_DEV]
    (modp_all,) = _all_gather("gather_mod", [modp.reshape(2 * N_DEV, mod_blk)])
    groups = [[w_in_loc[0], w_out_loc[0], pool_w.reshape(-1, POOL_GROUP).astype(BF16)], [w_in_loc[1], w_out_loc[1]],
              [w_in_loc[2], w_out_loc[2], mla_w_in[0].astype(BF16), mla_w_uq.reshape(mla_w_uq.shape[1], -1).astype(BF16),
               mla_w_o[0].astype(BF16)], [w_in_loc[3], w_out_loc[3]]]
    state, token = _split_start("gather_start_0", "own", groups[0], after=modp_all)
    states = [state]
    w_in8 = [None] * 4
    w_out4 = [None] * 4
    mod = lax.dynamic_index_in_dim(modp_all.reshape(N_DEV, 2, N_DEV, mod_blk), me, axis=2, keepdims=False)
    mod = mod.transpose(1, 0, 2).reshape(2, n_mod, d) + token[0, 0]
    mod_rows = mod.reshape(2 * n_mod, 1, d)
    g_rows = g_all.reshape(12, 1, d)
    weights_of = (0.5, 1.0, 0.5, 0.5, 1.0, 0.5)

    def pre_rows(k, with_shift):
        rows = [_rowk(g_rows, 2 * k), _rowk(mod_rows, 3 * k + 1)]
        return rows + [_rowk(mod_rows, 3 * k)] if with_shift else rows

    def post_rows(k):
        return [_rowk(g_rows, 2 * k + 1), _rowk(mod_rows, 3 * k + 2)]

    def act_dtype(k):
        return F32 if k == 1 else BF16

    saved = []
    xs = x0
    mla_wts = None
    h = _prenorm("prenorm_0", xs, pre_rows(0, True), act_dtype(0), tr)
    token = h
    for name, group in zip("abc", groups[1:]):
        state, token = _split_start(f"gather_start_{name}", "own", group, after=token)
        states.append(state)
    state, token = _split_pass("gather_pass_0", states[0], token)
    lands = _split_wait("gather_wait_0", state, token)
    w_in8[0], w_out4[0] = lands[0], lands[1].reshape(4, FF_BLK, d)
    w4 = lands[2].reshape(N_DEV, 4, POOL_SHARD, POOL_GROUP).transpose(1, 0, 2, 3).reshape(4, POOL_GROUP, POOL_GROUP)
    token = None
    for k in range(6):
        i, sub = divmod(k, 3)
        tag = f"{i}{sub}"
        if k == 1:
            states[1], token = _split_pass("gather_pass_a", states[1], xs)
        if k == 2:
            lands = _split_wait("gather_wait_a", states[1], xs)
            w_in8[1], w_out4[1] = lands[0], lands[1].reshape(4, FF_BLK, d)
            states[2], token = _split_pass("gather_pass_b", states[2], lands[0])
        if k == 3:
            lands = _split_wait("gather_wait_b", states[2], xs)
            w_in8[2], w_out4[2] = lands[0], lands[1].reshape(4, FF_BLK, d)
            mla_wts = mla_weights(*lands[2:])
            states[3], token = _split_pass("gather_pass_c", states[3], lands[0])
        if k == 5:
            lands = _split_wait("gather_wait_c", states[3], xs)
            w_in8[3], w_out4[3] = lands[0], lands[1].reshape(4, FF_BLK, d)
        if sub != 1:
            u, extra = _ffn_fwd(tag, h, w_in8[2 * i + sub // 2], w_out4[2 * i + sub // 2], tf, tf // 2, after=token)
            token = None
        elif i == 0:
            u, z, v = _pool_fwd(h, w4, pool_b_all + token[0, 0], pscale_row, min(4 * tm, s))
            extra = (z, v)
            token = None
        else:
            u, extra = _mla_fwd(h, mla_wts, cos_k, sin_k, tm)
        saved.append((xs, h, u, extra))
        if k < 5:
            xs, h = _norm_link(f"norm_link_{k + 1}", xs, u, post_rows(k), weights_of[k], pre_rows(k + 1, True), act_dtype(k + 1), tr)

    dx, du, (sq, dgate, dgpost) = _norm_loss("norm_loss", xs, u, target, post_rows(5), weights_of[5], tr)
    loss_part = (0.5 * jnp.sum(sq) / d).reshape(1, 1)

    d_mod = [None] * (2 * n_mod)
    d_g = [None] * 12
    d_mod[3 * 5 + 2], d_g[2 * 5 + 1] = dgate, dgpost
    sent = {}
    pool_grads = mla_grads = None

    def start_scatter(key, arrays):
        state, token = _split_start(f"scatter_start_{key}", "scatter", arrays)
        sent[key] = state
        return token

    for k in (5, 4, 3, 2, 1, 0):
        i, sub = divmod(k, 3)
        tag = f"{i}{sub}"
        xin, h, u, extra = saved[k]
        token = None
        if sub != 1:
            f = 2 * i + sub // 2
            dh, dgu, act = _ffn_bwd_act(tag, du, extra, w_in8[f], w_out4[f], tf, tf // 4)
            dw_out = _ffn_dw(f"ffn_dwout_{tag}", act, du, tw).reshape(N_DEV, FF_BLK // 2, d)
            if k == 0:
                d_pool_w = pool_grads[0].reshape(4, N_DEV, POOL_SHARD, POOL_GROUP).transpose(1, 0, 2, 3).reshape(N_DEV, -1, POOL_GROUP)
                token = start_scatter(tag + "_out", [dw_out, d_pool_w])
                token = start_scatter(tag + "_in", [_ffn_dw(f"ffn_dwin_{tag}", dgu, h, tw, after=token)])
            else:
                token = start_scatter(tag, [_ffn_dw(f"ffn_dwin_{tag}", dgu, h, tw), dw_out])
        elif i == 0:
            dh, dw4, (dpscale, dpb) = _pool_bwd(du, extra[0], extra[1], w4, pscale_row, min(4 * tm, s))
            pool_grads = (dw4, dpscale, dpb)
        else:
            dh, mla_grads = _mla_bwd(du, h, extra, mla_wts, cos_k, sin_k, tm)
            dwq = mla_grads["wq"]
            d_uq = jnp.concatenate([dwq[:, :, :QK_NOPE], dwq[:, :, KV_LORA:KV_LORA + QK_ROPE]], axis=2).transpose(1, 0, 2)
            token = start_scatter("mla", [mla_grads["w_in"][:, :mla_w_in.shape[2]].reshape(N_DEV, d // N_DEV, -1),
                                          d_uq.reshape(N_DEV, Q_LORA // N_DEV, -1), mla_grads["w_o"].reshape(N_DEV, d // N_DEV, d)])
            dwukp, dwuv2 = mla_grads["wukp"], mla_grads["wuv2"]
            d_uk = dwukp[:, :QK_NOPE, :KV_LORA].transpose(2, 0, 1).reshape(KV_LORA, -1)
            d_uv = jnp.where(even, dwuv2[:, :, :V_HEAD], dwuv2[:, :, V_HEAD:]).transpose(1, 0, 2).reshape(KV_LORA, -1)
            state_ukv, token = _split_start("gather_start_ukv", "gather", [d_uk, d_uv], after=token)
        if k > 0:
            dx, du, reds = _norm_link_bwd(f"norm_link_bwd_{k}", dh, xin, dx, saved[k - 1][2], pre_rows(k, False), post_rows(k - 1),
                                          weights_of[k - 1], act_dtype(k - 1), tr, after=token)
            d_mod[3 * (k - 1) + 2], d_g[2 * (k - 1) + 1] = reds[3], reds[4]
        else:
            dx, reds = _prenorm_bwd("prenorm_bwd_0", dh, xin, dx, pre_rows(0, False), tr, after=token)
        d_mod[3 * k], d_mod[3 * k + 1], d_g[2 * k] = reds[0], reds[1], reds[2]
    grad_x = dx.reshape(x.shape)

    def upd(name, parts, w, m, v):
        shape = w.shape
        r, cdim = parts.shape[1], parts.shape[2]
        return [o.reshape(shape) for o in _adamw(name, parts, w.reshape(r, cdim), m.reshape(r, cdim), v.reshape(r, cdim))]

    def landed(key, after):
        return _split_wait(f"scatter_wait_{key}", sent[key], after)

    res = {}
    w_in_s, m_in_s, v_in_s = (a.reshape(4, FF_BLK, d) for a in (w_in_t, m_in_t, v_in_t))
    w_out_s, m_out_s, v_out_s = (a.reshape(4, FF_BLK // 2, d) for a in (ffn_w_out, m_ffn_w_out, v_ffn_w_out))
    bufs_in = [lax.empty(w_in_s.shape, F32) for _ in range(4)]
    bufs_out = [lax.empty(w_out_s.shape, F32) for _ in range(4)]
    for key, k in (("12", 3), ("mla", None), ("10", 2), ("02", 1)):
        parts = landed(key, grad_x)
        if k is None:
            res["mla_w_in"] = upd("adam_mla_w_in", parts[0], mla_w_in, m_mla_w_in, v_mla_w_in)
            res["mla_w_uq"] = upd("adam_mla_w_uq", parts[1], mla_w_uq, m_mla_w_uq, v_mla_w_uq)
            res["mla_w_o"] = upd("adam_mla_w_o", parts[2], mla_w_o, m_mla_w_o, v_mla_w_o)
            uk_all, uv_all = _split_wait("gather_wait_ukv", state_ukv, grad_x)
            res["mla_w_uk"] = upd("adam_mla_w_uk", uk_all, mla_w_uk, m_mla_w_uk, v_mla_w_uk)
            res["mla_w_uv"] = upd("adam_mla_w_uv", uv_all, mla_w_uv, m_mla_w_uv, v_mla_w_uv)
            continue
        bufs_in = _adamw_slab(f"adam_ffn_w_in_{key}", parts[0], w_in_s, m_in_s, v_in_s, bufs_in, k)
        bufs_out = _adamw_slab(f"adam_ffn_w_out_{key}", parts[1], w_out_s, m_out_s, v_out_s, bufs_out, k)

    dw4, dpscale, dpb = pool_grads
    small_g = jnp.concatenate(d_mod + d_g + [dpb, dpscale, mla_grads["q_norm"], mla_grads["kv_norm"], loss_part], axis=1)
    done = [bufs_in[0], bufs_out[0], res["mla_w_o"][0], res["mla_w_uv"][0]]
    (small_g_all,) = _all_gather("gather_small_grads", [small_g], after=sum(a.reshape(-1)[:1] for a in done))
    small_g_all = small_g_all.reshape(N_DEV, -1)
    n_m = 2 * n_mod * d
    d_mod_all = small_g_all[:, :n_m].reshape(N_DEV, 2, n_mod * d)
    rest = small_g_all[:, n_m:]
    p_norm_g = lax.dynamic_slice_in_dim(rest[:, :12 * d].reshape(N_DEV, 12, d), me * (d // N_DEV), d // N_DEV, axis=2)
    p_pool_b = lax.dynamic_slice_in_dim(rest[:, 12 * d:13 * d].reshape(N_DEV, 4, POOL_GROUP), me * POOL_SHARD, POOL_SHARD, axis=2)
    p_pool_scale = rest[:, 13 * d:14 * d].reshape(N_DEV, 1, d)
    p_q_norm = lax.dynamic_slice_in_dim(rest[:, 14 * d:14 * d + Q_LORA], me * Q_SHARD, Q_SHARD, axis=1).reshape(N_DEV, 1, Q_SHARD)
    p_kv_norm = rest[:, 14 * d + Q_LORA:14 * d + Q_LORA + KV_LORA].reshape(N_DEV, 1, KV_LORA)
    loss = jnp.sum(rest[:, -1])

    d_mod_loc = lax.dynamic_slice_in_dim(d_mod_all, me * mod_blk, mod_blk, axis=2).transpose(1, 0, 2)
    k_pad = 128
    sc_t = jnp.pad(sc_all.T, ((0, 0), (0, k_pad - N_DEV)))
    d_ada_w = _matmul("ada_dw", sc_t, jnp.pad(d_mod_loc, ((0, 0), (0, k_pad - N_DEV), (0, 0))), a_blk=(d, k_pad),
                      a_map=lambda i, k: (0, 0), b_blk=(None, k_pad, mod_blk), b_map=lambda i, k: (i, 0, 0), o_shape=(2, d, mod_blk),
                      o_blk=(None, d, mod_blk), o_map=lambda i, k: (i, 0, 0), grid=(2, 1), contract=NN, out_dtype=F32)
    res["ada_w"] = upd("adam_ada_w", d_ada_w.reshape(1, 2 * d, mod_blk), ada_w, m_ada_w, v_ada_w)
    res["ada_b"] = upd("adam_ada_b", d_mod_all.reshape(N_DEV, 2, n_mod * d), ada_b, m_ada_b, v_ada_b)
    res["norm_g"] = upd("adam_norm_g", p_norm_g, norm_g, m_norm_g, v_norm_g)
    res["pool_b"] = upd("adam_pool_b", p_pool_b, pool_b, m_pool_b, v_pool_b)
    res["pool_scale"] = upd("adam_pool_scale", p_pool_scale, pool_scale, m_pool_scale, v_pool_scale)
    res["mla_q_norm"] = upd("adam_mla_q_norm", p_q_norm, mla_q_norm, m_mla_q_norm, v_mla_q_norm)
    res["mla_kv_norm"] = upd("adam_mla_kv_norm", p_kv_norm, mla_kv_norm, m_mla_kv_norm, v_mla_kv_norm)

    p_out, p_pool_w = landed("00_out", res["ada_w"][1])
    bufs_out = _adamw_slab("adam_ffn_w_out_00", p_out, w_out_s, m_out_s, v_out_s, bufs_out, 0)
    res["pool_w"] = upd("adam_pool_w", p_pool_w, pool_w, m_pool_w, v_pool_w)
    (p_in,) = landed("00_in", res["pool_w"][1])
    bufs_in = _adamw_slab("adam_ffn_w_in_00", p_in, w_in_s, m_in_s, v_in_s, bufs_in, 0)
    res["ffn_w_in"] = [jnp.swapaxes(b.reshape(w_in_t.shape), 2, 3) for b in bufs_in]
    res["ffn_w_out"] = [b.reshape(ffn_w_out.shape) for b in bufs_out]

    order = ["ada_w", "ada_b", "norm_g", "ffn_w_in", "ffn_w_out", "pool_w", "pool_b", "pool_scale", "mla_w_in", "mla_q_norm",
             "mla_kv_norm", "mla_w_uq", "mla_w_uk", "mla_w_uv", "mla_w_o"]
    outs = [loss, grad_x]
    for j in range(4):
        outs += [res[name][j] for name in order]
    return tuple(outs)
```

```python
import jax
import jax.numpy as jnp
from jax import lax
from jax.experimental import pallas as pl
from jax.experimental.pallas import tpu as pltpu

F32 = jnp.float32
BF16 = jnp.bfloat16
N_DEV = 8
MESH = pl.DeviceIdType.MESH

D_MODEL = 1024
N_HEADS = 16
QK_NOPE = 64
QK_ROPE = 32
V_HEAD = 64
Q_LORA = 256
KV_LORA = 128
LAT_PAD = 512
QK_PAD = 256
ONE_COL = 160
D_FF = 2816
FF_BLK = 2 * D_FF // N_DEV
POOL_WINDOWS = (2, 4, 8, 16)
POOL_GROUP = 256
POOL_SHARD = POOL_GROUP // N_DEV
Q_SHARD = Q_LORA // N_DEV
ROPE_THETA = 10000.0
EPS = 1e-6
ATTN_SCALE = (QK_NOPE + QK_ROPE) ** -0.5
LOG2_E = 1.4426950408889634
ADAM_LR, ADAM_B1, ADAM_B2, ADAM_EPS, ADAM_WD, ADAM_STEP = 0.001, 0.9, 0.999, 1e-08, 0.01, 10
VMEM_LIMIT = 56 * 1024 * 1024

NN = ((1,), (0,))
NT = ((1,), (1,))
TN = ((0,), (0,))


def _params(**kw):
    return pltpu.CompilerParams(vmem_limit_bytes=VMEM_LIMIT, **kw)


def _dot(a, b, contract):
    return lax.dot_general(a, b, (contract, ((), ())), preferred_element_type=F32)


def _matmul(name, a, b, *, a_blk, a_map, b_blk, b_map, o_shape, o_blk, o_map, grid, contract, out_dtype,
            bias=None, bias_blk=None, bias_map=None, after=None):
    n_k = grid[-1]
    k_axis = len(grid) - 1
    acc_shape = tuple(d for d in o_blk if d is not None)

    def body(*refs):
        a_ref, b_ref = refs[:2]
        bias_ref = refs[2] if bias is not None else None
        if n_k == 1:
            r = _dot(a_ref[...].astype(BF16), b_ref[...].astype(BF16), contract)
            if bias is not None:
                r = r + bias_ref[...]
            refs[-1][...] = r.astype(refs[-1].dtype)
            return
        o_ref, acc = refs[-2:]
        k = pl.program_id(k_axis)

        @pl.when(k == 0)
        def _():
            acc[...] = jnp.zeros_like(acc)

        acc[...] += _dot(a_ref[...].astype(BF16), b_ref[...].astype(BF16), contract)

        @pl.when(k == n_k - 1)
        def _():
            r = acc[...]
            if bias is not None:
                r = r + bias_ref[...]
            o_ref[...] = r.astype(o_ref.dtype)

    in_specs = [pl.BlockSpec(a_blk, a_map), pl.BlockSpec(b_blk, b_map)]
    args = [a, b]
    if bias is not None:
        in_specs.append(pl.BlockSpec(bias_blk, bias_map))
        args.append(bias)
    if after is not None:
        in_specs.append(pl.BlockSpec(memory_space=pl.ANY))
        args.append(after)
    return pl.pallas_call(
        body, name=name, grid=grid, in_specs=in_specs, out_specs=pl.BlockSpec(o_blk, o_map),
        out_shape=jax.ShapeDtypeStruct(o_shape, out_dtype), scratch_shapes=[pltpu.VMEM(acc_shape, F32)] if n_k > 1 else [],
        compiler_params=_params(),
    )(*args)


def _rowmap(name, fn, ins, outs, reds, grid, after=None):
    n_in, n_out, n_red = len(ins), len(outs), len(reds)
    extra = [] if after is None else [after]

    def body(*refs):
        in_refs = refs[:n_in]
        out_refs = refs[n_in + len(extra):n_in + len(extra) + n_out]
        red_refs = refs[n_in + len(extra) + n_out:]
        out_vals, red_vals = fn(*[r[...] for r in in_refs])
        for r, v in zip(out_refs, out_vals):
            r[...] = v.astype(r.dtype)
        if n_red:
            first = pl.program_id(0) == 0
            for ax in range(1, len(grid)):
                first = jnp.logical_and(first, pl.program_id(ax) == 0)

            @pl.when(first)
            def _():
                for r in red_refs:
                    r[...] = jnp.zeros_like(r)

            for r, v in zip(red_refs, red_vals):
                r[...] += v

    res = pl.pallas_call(
        body, name=name, grid=grid,
        in_specs=[pl.BlockSpec(blk, imap) for _, blk, imap in ins] + [pl.BlockSpec(memory_space=pl.ANY)] * len(extra),
        out_specs=[pl.BlockSpec(blk, imap) for _, blk, imap in list(outs) + list(reds)],
        out_shape=[sds for sds, _, _ in list(outs) + list(reds)],
        compiler_params=_params(),
    )(*[a for a, _, _ in ins], *extra)
    return res[:n_out], res[n_out:]


def _sds(shape, dtype):
    return jax.ShapeDtypeStruct(shape, dtype)


def _tile(a, tm):
    return (a, (tm, a.shape[1]), lambda i: (i, 0))


def _row(a):
    return (a, (1, a.shape[1]), lambda i: (0, 0))


def _otile(n, c, dtype, tm):
    return (_sds((n, c), dtype), (tm, c), lambda i: (i, 0))


def _ored(c):
    return (_sds((1, c), F32), (1, c), lambda i: (0, 0))


def _colsum(v):
    return jnp.sum(v, axis=0, keepdims=True)


def _rstd(v):
    return lax.rsqrt(jnp.mean(v * v, axis=-1, keepdims=True) + EPS)


def _pre(xv, g, sc, sh):
    return xv * _rstd(xv) * g * (1.0 + sc) + sh


def _post(xv, uv, g, gt, weight):
    uv = uv.astype(F32)
    return xv + weight * (1.0 + gt) * (uv * _rstd(uv) * g)


def _post_bwd(dv, uv, g, gt, weight):
    uv = uv.astype(F32)
    r = _rstd(uv)
    un = uv * r
    dy = dv * (weight * (1.0 + gt))
    a = dy * g
    du = r * (a - un * jnp.mean(a * un, axis=-1, keepdims=True))
    return du, (_colsum(dv * (weight * (un * g))), _colsum(dy * un))


def _pre_bwd(dhv, xv, dv, g, sc):
    dhv = dhv.astype(F32)
    r = _rstd(xv)
    xn = xv * r
    b = dhv * (g * (1.0 + sc))
    dx = dv + r * (b - xn * jnp.mean(b * xn, axis=-1, keepdims=True))
    return dx, (_colsum(dhv), _colsum(dhv * (xn * g)), _colsum(dhv * ((1.0 + sc) * xn)))


def _rowk(rows, k):
    return (rows, (None, 1, rows.shape[2]), lambda i: (k, 0, 0))


def _prenorm(name, x, pre, out_dtype, tm):
    n, d = x.shape
    (h,), _ = _rowmap(name, lambda xv, g, sc, sh: ((_pre(xv, g, sc, sh),), ()), [_tile(x, tm), *pre], [_otile(n, d, out_dtype, tm)],
                      [], (n // tm,))
    return h


def _norm_link(name, x, u, post, weight, pre, out_dtype, tm):
    n, d = x.shape

    def fn(xv, uv, g, gt, g2, sc, sh):
        xn = _post(xv, uv, g, gt, weight)
        return (xn, _pre(xn, g2, sc, sh)), ()

    (xn, h), _ = _rowmap(name, fn, [_tile(x, tm), _tile(u, tm), *post, *pre], [_otile(n, d, F32, tm), _otile(n, d, out_dtype, tm)],
                         [], (n // tm,))
    return xn, h


def _norm_loss(name, x, u, target, post, weight, tm):
    n, d = x.shape

    def fn(xv, uv, tv, g, gt):
        e = _post(xv, uv, g, gt, weight) - tv
        dv = e * (1.0 / d)
        du, reds = _post_bwd(dv, uv, g, gt, weight)
        return (dv, du), (_colsum(e * e), *reds)

    (dx, du), reds = _rowmap(name, fn, [_tile(x, tm), _tile(u, tm), _tile(target, tm), *post],
                             [_otile(n, d, F32, tm), _otile(n, d, BF16, tm)], [_ored(d)] * 3, (n // tm,))
    return dx, du, reds


def _norm_link_bwd(name, dh, x, dout, u_prev, pre, post_prev, weight_prev, out_dtype, tm, after=None):
    n, d = x.shape

    def fn(dhv, xv, dv, uv, g, sc, g2, gt):
        dx, reds = _pre_bwd(dhv, xv, dv, g, sc)
        du, reds_prev = _post_bwd(dx, uv, g2, gt, weight_prev)
        return (dx, du), (*reds, *reds_prev)

    (dx, du), reds = _rowmap(name, fn, [_tile(dh, tm), _tile(x, tm), _tile(dout, tm), _tile(u_prev, tm), *pre, *post_prev],
                             [_otile(n, d, F32, tm), _otile(n, d, out_dtype, tm)], [_ored(d)] * 5, (n // tm,), after=after)
    return dx, du, reds


def _prenorm_bwd(name, dh, x, dout, pre, tm, after=None):
    n, d = x.shape

    def fn(dhv, xv, dv, g, sc):
        dx, reds = _pre_bwd(dhv, xv, dv, g, sc)
        return (dx,), reds

    (dx,), reds = _rowmap(name, fn, [_tile(dh, tm), _tile(x, tm), _tile(dout, tm), *pre], [_otile(n, d, F32, tm)], [_ored(d)] * 3,
                          (n // tm,), after=after)
    return dx, reds


def _ffn_fwd(tag, h, w_in8, w_out4, tm, sub, after=None):
    s, d = h.shape

    extra = [] if after is None else [after]

    def body(h_ref, wg_ref, wu_ref, wo_ref, *rest):
        u_ref, gu_ref, acc = rest[-3:]
        j = pl.program_id(1)

        @pl.when(j == 0)
        def _():
            acc[...] = jnp.zeros_like(acc)

        for r in range(tm // sub):
            rows = pl.ds(r * sub, sub)
            hv = h_ref[rows, :]
            gate = _dot(hv, wg_ref[...], NT)
            up = _dot(hv, wu_ref[...], NT)
            gu_ref[0, rows, :] = gate.astype(BF16)
            gu_ref[1, rows, :] = up.astype(BF16)
            acc[rows, :] += _dot((gate * jax.nn.sigmoid(gate) * up).astype(BF16), wo_ref[...], NN)

        @pl.when(j == w_out4.shape[0] - 1)
        def _():
            u_ref[...] = acc[...].astype(u_ref.dtype)

    w_blk = (None, FF_BLK, d)
    return pl.pallas_call(
        body, name=f"ffn_fwd_{tag}", grid=(s // tm, 4),
        in_specs=[pl.BlockSpec((tm, d), lambda i, j: (i, 0)), pl.BlockSpec(w_blk, lambda i, j: (j, 0, 0)),
                  pl.BlockSpec(w_blk, lambda i, j: (j + 4, 0, 0)), pl.BlockSpec((None, FF_BLK, d), lambda i, j: (j, 0, 0))]
        + [pl.BlockSpec(memory_space=pl.ANY)] * len(extra),
        out_specs=[pl.BlockSpec((tm, d), lambda i, j: (i, 0)), pl.BlockSpec((2, None, tm, FF_BLK), lambda i, j: (0, j, i, 0))],
        out_shape=[_sds((s, d), BF16), _sds((2, 4, s, FF_BLK), BF16)], scratch_shapes=[pltpu.VMEM((tm, d), F32)],
        compiler_params=_params(),
    )(h, w_in8, w_in8, w_out4, *extra)


def _ffn_bwd_act(tag, du, gu, w_in8, w_out4, tm, sub):
    s, d = du.shape

    def body(du_ref, gu_ref, wg_ref, wu_ref, wo_ref, dh_ref, dgu_ref, act_ref, acc):
        j = pl.program_id(1)

        @pl.when(j == 0)
        def _():
            acc[...] = jnp.zeros_like(acc)

        n_sub = tm // sub
        dact_next = _dot(du_ref[pl.ds(0, sub), :], wo_ref[...], NT)
        for r in range(n_sub):
            rows = pl.ds(r * sub, sub)
            dact = dact_next
            if r + 1 < n_sub:
                dact_next = _dot(du_ref[pl.ds((r + 1) * sub, sub), :], wo_ref[...], NT)
            gate, up = gu_ref[0, rows, :].astype(F32), gu_ref[1, rows, :].astype(F32)
            sg = jax.nn.sigmoid(gate)
            silu = gate * sg
            dg = dact * up * (sg * (1.0 + gate * (1.0 - sg)))
            dup = dact * silu
            dgu_ref[0, :, rows] = dg.T.astype(BF16)
            dgu_ref[1, :, rows] = dup.T.astype(BF16)
            act_ref[:, rows] = (silu * up).T.astype(BF16)
            acc[rows, :] += _dot(dg.astype(BF16), wg_ref[...], NN) + _dot(dup.astype(BF16), wu_ref[...], NN)

        @pl.when(j == w_out4.shape[0] - 1)
        def _():
            dh_ref[...] = acc[...].astype(dh_ref.dtype)

    w_blk = (None, FF_BLK, d)
    dh, dgu_t, act_t = pl.pallas_call(
        body, name=f"ffn_bwd_{tag}", grid=(s // tm, 4),
        in_specs=[pl.BlockSpec((tm, d), lambda i, j: (i, 0)), pl.BlockSpec((2, None, tm, FF_BLK), lambda i, j: (0, j, i, 0)),
                  pl.BlockSpec(w_blk, lambda i, j: (j, 0, 0)), pl.BlockSpec(w_blk, lambda i, j: (j + 4, 0, 0)),
                  pl.BlockSpec((None, FF_BLK, d), lambda i, j: (j, 0, 0))],
        out_specs=[pl.BlockSpec((tm, d), lambda i, j: (i, 0)), pl.BlockSpec((2, None, FF_BLK, tm), lambda i, j: (0, j, 0, i)),
                   pl.BlockSpec((None, FF_BLK, tm), lambda i, j: (j, 0, i))],
        out_shape=[_sds((s, d), BF16), _sds((2, 4, FF_BLK, s), BF16), _sds((4, FF_BLK, s), BF16)],
        scratch_shapes=[pltpu.VMEM((tm, d), F32)], compiler_params=_params(),
    )(du, gu, w_in8, w_in8, w_out4)
    return dh, dgu_t.reshape(8, FF_BLK, s), act_t


def _ffn_dw(name, lhs_t, rhs, tk, after=None):
    n_g, _, s = lhs_t.shape
    d = rhs.shape[1]
    return _matmul(name, lhs_t, rhs, a_blk=(None, FF_BLK, tk), a_map=lambda g, k: (g, 0, k), b_blk=(tk, d), b_map=lambda g, k: (k, 0),
                   o_shape=(n_g, FF_BLK, d), o_blk=(None, FF_BLK, d), o_map=lambda g, k: (g, 0, 0), grid=(n_g, s // tk),
                   contract=NN, out_dtype=BF16, after=after)


def _window_sum(x, window, transpose):
    s = x.shape[0]
    t = lax.broadcasted_iota(jnp.int32, (s, 1), 0)
    half = window // 2
    cnt = jnp.minimum(t + half, s) - jnp.maximum(t - half, 0)
    inv = 1.0 / cnt.astype(F32)
    if transpose:
        x = x * inv
        offsets = range(-half + 1, half + 1)
    else:
        offsets = range(-half, half)
    acc = jnp.zeros_like(x)
    for o in offsets:
        shifted = x if o == 0 else pltpu.roll(x, (-o) % s, 0)
        valid = jnp.logical_and(t + o >= 0, t + o < s)
        acc = acc + jnp.where(valid, shifted, 0.0)
    return acc if transpose else acc * inv


def _pool_mix(name, x, transpose, out_dtype):
    s, d = x.shape

    def body(x_ref, o_ref):
        g = pl.program_id(0)
        for gi, window in enumerate(POOL_WINDOWS):
            @pl.when(g == gi)
            def _(window=window):
                xv = x_ref[...].astype(F32)
                o_ref[...] = (_window_sum(xv, window, transpose) - xv).astype(o_ref.dtype)

    return pl.pallas_call(
        body, name=name, grid=(len(POOL_WINDOWS),), in_specs=[pl.BlockSpec((s, POOL_GROUP), lambda g: (0, g))],
        out_specs=pl.BlockSpec((s, POOL_GROUP), lambda g: (0, g)), out_shape=_sds((s, d), out_dtype), compiler_params=_params(),
    )(x)


def _pool_fwd(h, w4, bias, pscale, tm):
    s, d = h.shape
    nt = s // tm
    z = _pool_mix("pool_mix", h, False, BF16)
    v = _matmul("pool_proj", z, w4, a_blk=(tm, POOL_GROUP), a_map=lambda i, g, k: (i, g), b_blk=(None, POOL_GROUP, POOL_GROUP),
                b_map=lambda i, g, k: (g, 0, 0), o_shape=(s, d), o_blk=(tm, POOL_GROUP), o_map=lambda i, g, k: (i, g),
                grid=(nt, 4, 1), contract=NN, out_dtype=F32, bias=bias, bias_blk=(1, POOL_GROUP), bias_map=lambda i, g, k: (0, g))
    tr = min(512, s)
    (u,), _ = _rowmap("pool_scale", lambda vv, ps: ((vv * ps,), ()), [_tile(v, tr), _row(pscale)], [_otile(s, d, BF16, tr)], [],
                      (s // tr,))
    return u, z, v


def _pool_bwd(du, z, v, w4, pscale, tm):
    s, d = du.shape
    nt = s // tm

    def fn(duv, vv, ps):
        dv = duv * ps
        return (dv,), (_colsum(duv * vv), _colsum(dv))

    tr = min(512, s)
    (dv,), reds = _rowmap("pool_dscale", fn, [_tile(du, tr), _tile(v, tr), _row(pscale)], [_otile(s, d, BF16, tr)],
                          [_ored(d), _ored(d)], (s // tr,))
    dw4 = _matmul("pool_dw", z, dv, a_blk=(tm, POOL_GROUP), a_map=lambda g, k: (k, g), b_blk=(tm, POOL_GROUP),
                  b_map=lambda g, k: (k, g), o_shape=(4, POOL_GROUP, POOL_GROUP), o_blk=(None, POOL_GROUP, POOL_GROUP),
                  o_map=lambda g, k: (g, 0, 0), grid=(4, nt), contract=TN, out_dtype=F32)
    dz = _matmul("pool_dz", dv, w4, a_blk=(tm, POOL_GROUP), a_map=lambda i, g, k: (i, g), b_blk=(None, POOL_GROUP, POOL_GROUP),
                 b_map=lambda i, g, k: (g, 0, 0), o_shape=(s, d), o_blk=(tm, POOL_GROUP), o_map=lambda i, g, k: (i, g),
                 grid=(nt, 4, 1), contract=NT, out_dtype=F32)
    dh = _pool_mix("pool_mix_t", dz, True, BF16)
    return dh, dw4, reds


def _lane(shape):
    return lax.broadcasted_iota(jnp.int32, shape, 1)


def _rope_swap(v, transpose):
    half = QK_ROPE // 2
    lane = _lane(v.shape)
    up = pltpu.roll(v, v.shape[1] - half, 1)
    down = pltpu.roll(v, half, 1)
    if transpose:
        return jnp.where(lane < half, up, jnp.where(lane < QK_ROPE, -down, 0.0))
    return jnp.where(lane < half, -up, jnp.where(lane < QK_ROPE, down, 0.0))


def _rope(v, cos, sin):
    return v * cos + _rope_swap(v, False) * sin


def _rope_t(g, cos, sin):
    return g * cos + _rope_swap(g * sin, True)


def _mla_mid(lat, q_norm, kv_norm, cos_k, sin_k, tm):
    s = lat.shape[0]

    def fn(lv, qn, kn, cs, sn):
        cq = lv[:, :Q_LORA]
        ckv = lv[:, Q_LORA:Q_LORA + KV_LORA]
        kr = lv[:, Q_LORA + KV_LORA:]
        cq = cq * _rstd(cq) * qn
        ckv = ckv * _rstd(ckv) * kn
        k_rope = jnp.where(_lane(kr.shape) == ONE_COL - KV_LORA, 1.0, _rope(kr, cs, sn))
        return (cq, jnp.concatenate([ckv, k_rope], axis=1)), ()

    (cq, kcat), _ = _rowmap("mla_mid", fn, [_tile(lat, tm), _row(q_norm), _row(kv_norm), _tile(cos_k, tm), _tile(sin_k, tm)],
                            [_otile(s, Q_LORA, BF16, tm), _otile(s, QK_PAD, BF16, tm)], [], (s // tm,))
    return cq, kcat


def _mla_mid_bwd(lat, dcq, dkcat, dv, q_norm, kv_norm, cos_k, sin_k, tm):
    s = lat.shape[0]

    def fn(lv, dq, dk, dvv, qn, kn, cs, sn):
        dk = dk * (1.0 / LOG2_E)
        cq = lv[:, :Q_LORA]
        ckv = lv[:, Q_LORA:Q_LORA + KV_LORA]
        rq, rk = _rstd(cq), _rstd(ckv)
        cqn, ckn = cq * rq, ckv * rk
        a = dq * qn
        d_cq = rq * (a - cqn * jnp.mean(a * cqn, axis=-1, keepdims=True))
        dckv = dk[:, :KV_LORA] + dvv
        a2 = dckv * kn
        d_ckv = rk * (a2 - ckn * jnp.mean(a2 * ckn, axis=-1, keepdims=True))
        d_kr = _rope_t(dk[:, KV_LORA:], cs, sn)
        return (jnp.concatenate([d_cq, d_ckv, d_kr], axis=1),), (_colsum(dq * cqn), _colsum(dckv * ckn))

    (dlat,), reds = _rowmap(
        "mla_mid_bwd", fn,
        [_tile(lat, tm), _tile(dcq, tm), _tile(dkcat, tm), _tile(dv, tm), _row(q_norm), _row(kv_norm), _tile(cos_k, tm),
         _tile(sin_k, tm)],
        [_otile(s, LAT_PAD, BF16, tm)], [_ored(Q_LORA), _ored(KV_LORA)], (s // tm,))
    return dlat, reds


def _mla_q(cq, wq, wukp, cos_k, sin_k, tm):
    s = cq.shape[0]

    def body(cq_ref, wq_ref, wuk_ref, cos_ref, sin_ref, o_ref):
        cqv, cs, sn = cq_ref[...], cos_ref[...], sin_ref[...]
        for h in range(N_HEADS):
            aq = _dot(cqv, wq_ref[h], NN)
            qlat = _dot(aq.astype(BF16), wuk_ref[h], NN)
            roped = _rope(aq[:, KV_LORA:], cs, sn)
            o_ref[h] = (jnp.concatenate([qlat[:, :KV_LORA], roped], axis=1) * (ATTN_SCALE * LOG2_E)).astype(o_ref.dtype)

    wblk = pl.BlockSpec((N_HEADS, QK_PAD, QK_PAD), lambda i: (0, 0, 0))
    tblk = pl.BlockSpec((tm, KV_LORA), lambda i: (i, 0))
    return pl.pallas_call(
        body, name="mla_q", grid=(s // tm,),
        in_specs=[pl.BlockSpec((tm, Q_LORA), lambda i: (i, 0)), wblk, wblk, tblk, tblk],
        out_specs=pl.BlockSpec((N_HEADS, tm, QK_PAD), lambda i: (0, i, 0)), out_shape=_sds((N_HEADS, s, QK_PAD), BF16),
        compiler_params=_params(),
    )(cq, wq, wukp, cos_k, sin_k)


def _mla_q_bwd(cq, wq, wukp, cos_k, sin_k, dqcat, tm):
    s = cq.shape[0]

    def body(cq_ref, wq_ref, wuk_ref, cos_ref, sin_ref, dq_ref, dcq_ref, dwq_ref, dwuk_ref):
        @pl.when(pl.program_id(0) == 0)
        def _():
            dwq_ref[...] = jnp.zeros_like(dwq_ref)
            dwuk_ref[...] = jnp.zeros_like(dwuk_ref)

        cqv, cs, sn = cq_ref[...], cos_ref[...], sin_ref[...]
        d_cq = jnp.zeros((tm, Q_LORA), F32)
        for h in range(N_HEADS):
            aq = _dot(cqv, wq_ref[h], NN).astype(BF16)
            g = dq_ref[h].astype(F32) * ATTN_SCALE
            gl, gr = g[:, :KV_LORA], g[:, KV_LORA:]
            dqlat = jnp.concatenate([gl, jnp.zeros_like(gl)], axis=1).astype(BF16)
            d_rope = _rope_t(gr, cs, sn)
            daq = _dot(dqlat, wuk_ref[h], NT) + jnp.concatenate([jnp.zeros_like(d_rope), d_rope], axis=1)
            daq_b = daq.astype(BF16)
            dwuk_ref[h] += _dot(aq, dqlat, TN)
            dwq_ref[h] += _dot(cqv, daq_b, TN)
            d_cq = d_cq + _dot(daq_b, wq_ref[h], NT)
        dcq_ref[...] = d_cq

    wblk = pl.BlockSpec((N_HEADS, QK_PAD, QK_PAD), lambda i: (0, 0, 0))
    tblk = pl.BlockSpec((tm, KV_LORA), lambda i: (i, 0))
    return pl.pallas_call(
        body, name="mla_q_bwd", grid=(s // tm,),
        in_specs=[pl.BlockSpec((tm, Q_LORA), lambda i: (i, 0)), wblk, wblk, tblk, tblk,
                  pl.BlockSpec((N_HEADS, tm, QK_PAD), lambda i: (0, i, 0))],
        out_specs=[pl.BlockSpec((tm, Q_LORA), lambda i: (i, 0)), wblk, wblk],
        out_shape=[_sds((s, Q_LORA), F32), _sds((N_HEADS, QK_PAD, QK_PAD), F32), _sds((N_HEADS, QK_PAD, QK_PAD), F32)],
        compiler_params=_params(),
    )(cq, wq, wukp, cos_k, sin_k, dqcat)


def _flash_fwd(qcat, kcat, tq, tk):
    n_h, s, _ = qcat.shape
    n_k = s // tk

    def body(q_ref, k_ref, o_ref, lse_ref):
        q = q_ref[...]
        m = jnp.full((tq, 1), -1e30, F32)
        acc = jnp.zeros((tq, QK_PAD), F32)
        for kk in range(n_k):
            k = k_ref[pl.ds(kk * tk, tk), :]
            sc = _dot(q, k, NT)
            m_new = jnp.maximum(m, jnp.max(sc, axis=1, keepdims=True))
            p = jnp.exp2(sc - m_new).astype(BF16)
            acc = jnp.exp2(m - m_new) * acc + _dot(p, k, NN)
            m = m_new
        l = jnp.sum(jnp.where(_lane(acc.shape) == ONE_COL, acc, 0.0), axis=1, keepdims=True)
        o_ref[...] = (acc[:, :KV_LORA] / l).astype(o_ref.dtype)
        lse_ref[...] = m + jnp.log2(l)

    return pl.pallas_call(
        body, name="mla_attn", grid=(n_h, s // tq),
        in_specs=[pl.BlockSpec((None, tq, QK_PAD), lambda h, i: (h, i, 0)), pl.BlockSpec((s, QK_PAD), lambda h, i: (0, 0))],
        out_specs=[pl.BlockSpec((None, tq, KV_LORA), lambda h, i: (h, i, 0)), pl.BlockSpec((None, tq, 1), lambda h, i: (h, i, 0))],
        out_shape=[_sds((n_h, s, KV_LORA), BF16), _sds((n_h, s, 1), F32)], compiler_params=_params(),
    )(qcat, kcat)


def _flash_bwd(qcat, kcat, o, do, lse, tq, tk):
    n_h, s, _ = qcat.shape
    n_k = s // tk

    def body(q_ref, k_ref, v_ref, o_ref, do_ref, lse_ref, dq_ref, dk_ref, dv_ref, dq_acc):
        h, i = pl.program_id(0), pl.program_id(1)

        @pl.when(jnp.logical_and(h == 0, i == 0))
        def _():
            dk_ref[...] = jnp.zeros_like(dk_ref)
            dv_ref[...] = jnp.zeros_like(dv_ref)

        q = q_ref[...]
        dov = do_ref[...]
        dov_t = dov.T
        q_t = q.T
        lse_v = lse_ref[...]
        delta = jnp.sum(dov.astype(F32) * o_ref[...].astype(F32), axis=1, keepdims=True)
        dq_acc[...] = jnp.zeros_like(dq_acc)

        for kk in range(n_k):
            rows = pl.ds(kk * tk, tk)
            k = k_ref[rows, :]
            p = jnp.exp2(_dot(q, k, NT) - lse_v)
            dp = _dot(dov, v_ref[rows, :], NT)
            ds = (p * (dp - delta)).astype(BF16)
            dq_acc[...] += _dot(ds, k, NN)
            dv_ref[:, rows] += _dot(dov_t, p.astype(BF16), NN)
            dk_ref[:, rows] += _dot(q_t, ds, NN)
        dq_ref[...] = dq_acc[...].astype(dq_ref.dtype)

    qblk = pl.BlockSpec((None, tq, QK_PAD), lambda h, i: (h, i, 0))
    oblk = pl.BlockSpec((None, tq, KV_LORA), lambda h, i: (h, i, 0))
    return pl.pallas_call(
        body, name="mla_attn_bwd", grid=(n_h, s // tq),
        in_specs=[qblk, pl.BlockSpec((s, QK_PAD), lambda h, i: (0, 0)), pl.BlockSpec((s, KV_LORA), lambda h, i: (0, 0)), oblk, oblk,
                  pl.BlockSpec((None, tq, 1), lambda h, i: (h, i, 0))],
        out_specs=[qblk, pl.BlockSpec((QK_PAD, s), lambda h, i: (0, 0)), pl.BlockSpec((KV_LORA, s), lambda h, i: (0, 0))],
        out_shape=[_sds((n_h, s, QK_PAD), BF16), _sds((QK_PAD, s), F32), _sds((KV_LORA, s), F32)],
        scratch_shapes=[pltpu.VMEM((tq, QK_PAD), F32)], compiler_params=_params(),
    )(qcat, kcat, kcat, o, do, lse)


def _mla_uv(o_lat, wuv2, do, tm):
    n_h, s, _ = o_lat.shape
    d = n_h * V_HEAD
    pair = 2 * V_HEAD
    lat_blk = pl.BlockSpec((n_h, tm, KV_LORA), lambda i: (0, i, 0))
    w_blk = pl.BlockSpec((n_h, KV_LORA, pair), lambda i: (0, 0, 0))
    row_blk = pl.BlockSpec((tm, d), lambda i: (i, 0))

    if do is None:
        def body(a_ref, w_ref, o_ref):
            for p in range(n_h // 2):
                o_ref[:, p * pair:(p + 1) * pair] = (
                    _dot(a_ref[2 * p], w_ref[2 * p], NN) + _dot(a_ref[2 * p + 1], w_ref[2 * p + 1], NN)).astype(o_ref.dtype)

        return pl.pallas_call(body, name="mla_uv", grid=(s // tm,), in_specs=[lat_blk, w_blk], out_specs=row_blk,
                              out_shape=_sds((s, d), BF16), compiler_params=_params())(o_lat, wuv2)

    def body(a_ref, w_ref, do_ref, dlat_ref, dw_ref):
        @pl.when(pl.program_id(0) == 0)
        def _():
            dw_ref[...] = jnp.zeros_like(dw_ref)

        for h in range(n_h):
            dov = do_ref[:, (h // 2) * pair:(h // 2 + 1) * pair]
            dlat_ref[h] = _dot(dov, w_ref[h], NT).astype(dlat_ref.dtype)
            dw_ref[h] += _dot(a_ref[h], dov, TN)

    return pl.pallas_call(body, name="mla_uv_bwd", grid=(s // tm,), in_specs=[lat_blk, w_blk, row_blk], out_specs=[lat_blk, w_blk],
                          out_shape=[_sds((n_h, s, KV_LORA), BF16), _sds((n_h, KV_LORA, pair), F32)], compiler_params=_params(),
                          )(o_lat, wuv2, do)


def _mla_fwd(h, wts, cos_k, sin_k, tm):
    s, d = h.shape
    nt = s // tm
    lat = _matmul("mla_lat", h, wts["w_in"], a_blk=(tm, d), a_map=lambda i, k: (i, 0), b_blk=(d, LAT_PAD), b_map=lambda i, k: (0, 0),
                  o_shape=(s, LAT_PAD), o_blk=(tm, LAT_PAD), o_map=lambda i, k: (i, 0), grid=(nt, 1), contract=NN, out_dtype=F32)
    cq, kcat = _mla_mid(lat, wts["q_norm"], wts["kv_norm"], cos_k, sin_k, tm)
    qcat = _mla_q(cq, wts["wq"], wts["wukp"], cos_k, sin_k, tm)
    o_lat, lse = _flash_fwd(qcat, kcat, min(4 * tm, s), tm)
    o = _mla_uv(o_lat, wts["wuv2"], None, tm)
    u = _matmul("mla_out", o, wts["w_o"], a_blk=(tm, d), a_map=lambda i, k: (i, 0), b_blk=(d, d), b_map=lambda i, k: (0, 0),
                o_shape=(s, d), o_blk=(tm, d), o_map=lambda i, k: (i, 0), grid=(nt, 1), contract=NN, out_dtype=BF16)
    return u, (lat, cq, kcat, qcat, o_lat, lse, o)


def _mla_bwd(du, h, saved, wts, cos_k, sin_k, tm):
    lat, cq, kcat, qcat, o_lat, lse, o = saved
    s, d = h.shape
    nt = s // tm
    do = _matmul("mla_do", du, wts["w_o"], a_blk=(tm, d), a_map=lambda i, k: (i, 0), b_blk=(d, d), b_map=lambda i, k: (0, 0),
                 o_shape=(s, d), o_blk=(tm, d), o_map=lambda i, k: (i, 0), grid=(nt, 1), contract=NT, out_dtype=BF16)
    dw_o = _matmul("mla_dwo", o, du, a_blk=(tm, d), a_map=lambda k: (k, 0), b_blk=(tm, d), b_map=lambda k: (k, 0),
                   o_shape=(d, d), o_blk=(d, d), o_map=lambda k: (0, 0), grid=(nt,), contract=TN, out_dtype=F32)
    do_lat, dwuv2 = _mla_uv(o_lat, wts["wuv2"], do, tm)
    dqcat, dkcat_t, dv_t = _flash_bwd(qcat, kcat, o_lat, do_lat, lse, min(4 * tm, s), tm)
    dkcat, dv = dkcat_t.T, dv_t.T
    dcq, dwq, dwukp = _mla_q_bwd(cq, wts["wq"], wts["wukp"], cos_k, sin_k, dqcat, tm)
    dlat, (dqn, dkn) = _mla_mid_bwd(lat, dcq, dkcat, dv, wts["q_norm"], wts["kv_norm"], cos_k, sin_k, tm)
    dh = _matmul("mla_dh", dlat, wts["w_in"], a_blk=(tm, LAT_PAD), a_map=lambda i, k: (i, 0), b_blk=(d, LAT_PAD),
                 b_map=lambda i, k: (0, 0), o_shape=(s, d), o_blk=(tm, d), o_map=lambda i, k: (i, 0), grid=(nt, 1), contract=NT,
                 out_dtype=BF16)
    dw_in = _matmul("mla_dwin", h, dlat, a_blk=(tm, d), a_map=lambda k: (k, 0), b_blk=(tm, LAT_PAD), b_map=lambda k: (k, 0),
                    o_shape=(d, LAT_PAD), o_blk=(d, LAT_PAD), o_map=lambda k: (0, 0), grid=(nt,), contract=TN, out_dtype=F32)
    return dh, dict(w_in=dw_in, wq=dwq, wukp=dwukp, wuv2=dwuv2, w_o=dw_o, q_norm=dqn, kv_norm=dkn)


def _adamw(name, parts, w, m, v):
    n_parts, r, c = parts.shape
    tr = r
    for cand in (256, 128, 64, 32, 16, 8):
        if r > cand and r % cand == 0:
            tr = cand
            break

    def body(p_ref, w_ref, m_ref, v_ref, g_ref, d_ref, nm_ref, nv_ref):
        g = p_ref[0].astype(F32)
        for k in range(1, n_parts):
            g = g + p_ref[k].astype(F32)
        nm = ADAM_B1 * m_ref[...] + (1.0 - ADAM_B1) * g
        nv = ADAM_B2 * v_ref[...] + (1.0 - ADAM_B2) * (g * g)
        m_hat = nm / (1.0 - ADAM_B1 ** ADAM_STEP)
        v_hat = nv / (1.0 - ADAM_B2 ** ADAM_STEP)
        g_ref[...] = g
        d_ref[...] = -ADAM_LR * (m_hat / (jnp.sqrt(v_hat) + ADAM_EPS) + ADAM_WD * w_ref[...])
        nm_ref[...] = nm
        nv_ref[...] = nv

    blk = pl.BlockSpec((tr, c), lambda i: (i, 0))
    return pl.pallas_call(
        body, name=name, grid=(r // tr,), in_specs=[pl.BlockSpec((n_parts, tr, c), lambda i: (0, i, 0)), blk, blk, blk],
        out_specs=[blk] * 4, out_shape=[_sds((r, c), F32)] * 4, compiler_params=_params(),
    )(parts, w, m, v)


def _adamw_slab(name, parts, w, m, v, bufs, f):
    n_parts, r, c = parts.shape
    tr = max(t for t in range(8, 257, 8) if r % t == 0)

    def body(p_ref, w_ref, m_ref, v_ref, *rest):
        g_ref, d_ref, nm_ref, nv_ref = rest[4:]
        g = p_ref[0].astype(F32)
        for k in range(1, n_parts):
            g = g + p_ref[k].astype(F32)
        nm = ADAM_B1 * m_ref[...] + (1.0 - ADAM_B1) * g
        nv = ADAM_B2 * v_ref[...] + (1.0 - ADAM_B2) * (g * g)
        m_hat = nm / (1.0 - ADAM_B1 ** ADAM_STEP)
        v_hat = nv / (1.0 - ADAM_B2 ** ADAM_STEP)
        g_ref[...] = g
        d_ref[...] = -ADAM_LR * (m_hat / (jnp.sqrt(v_hat) + ADAM_EPS) + ADAM_WD * w_ref[...])
        nm_ref[...] = nm
        nv_ref[...] = nv

    blk = pl.BlockSpec((None, tr, c), lambda i: (f, i, 0))
    return pl.pallas_call(
        body, name=name, grid=(r // tr,),
        in_specs=[pl.BlockSpec((n_parts, tr, c), lambda i: (0, i, 0)), blk, blk, blk] + [pl.BlockSpec(memory_space=pl.ANY)] * 4,
        out_specs=[blk] * 4, out_shape=[_sds(w.shape, F32)] * 4, input_output_aliases={4 + j: j for j in range(4)},
        compiler_params=_params(),
    )(parts, w, m, v, *bufs)


def _mesh_pos():
    return lax.axis_index("x"), lax.axis_index("y"), lax.axis_index("c")


def _flip(pos, mask):
    return tuple(1 - p if (mask >> (2 - b)) & 1 else p for b, p in enumerate(pos))


def _index(pos):
    return 4 * pos[0] + 2 * pos[1] + pos[2]


def _all_gather(name, xs, after=None):
    n = len(xs)
    extra = [] if after is None else [after]

    def body(*refs):
        x_refs, o_refs = refs[:n], refs[n + len(extra):2 * n + len(extra)]
        send_sems, recv_sems, local_sems = refs[2 * n + len(extra):]
        me = _mesh_pos()
        sibling = _flip(me, 1)
        others = [_flip(me, 4), _flip(me, 2), _flip(me, 6)]

        def copy(k, j, block, to, src=None):
            dst = o_refs[k].at[_index(block)]
            return pltpu.make_async_remote_copy(
                src_ref=dst if src is None else src, dst_ref=dst, send_sem=send_sems.at[k, j], recv_sem=recv_sems.at[k, j],
                device_id=to, device_id_type=MESH)

        local = [pltpu.make_async_copy(x_refs[k], o_refs[k].at[_index(me)], local_sems.at[k]) for k in range(n)]
        for cp in local:
            cp.start()
        first = []
        for k in range(n):
            first.append(copy(k, 0, me, sibling, src=x_refs[k]))
            first += [copy(k, 1 + j, me, other, src=x_refs[k]) for j, other in enumerate(others)]
        for cp in first:
            cp.start()
        passed = []
        for j, other in enumerate(others):
            for k in range(n):
                copy(k, 1 + j, other, me).wait_recv()
                cp = copy(k, 4 + j, other, sibling)
                cp.start()
                passed.append(cp)
        for k in range(n):
            copy(k, 0, sibling, me).wait_recv()
        for j, other in enumerate(others):
            for k in range(n):
                copy(k, 4 + j, _flip(other, 1), me).wait_recv()
        for cp in first + passed:
            cp.wait_send()
        for cp in local:
            cp.wait()

    any_spec = pl.BlockSpec(memory_space=pl.ANY)
    return pl.pallas_call(
        body, name=name, in_specs=[any_spec] * (n + len(extra)), out_specs=[any_spec] * n,
        out_shape=[_sds((N_DEV,) + x.shape, x.dtype) for x in xs],
        scratch_shapes=[pltpu.SemaphoreType.DMA((n, 7)), pltpu.SemaphoreType.DMA((n, 7)), pltpu.SemaphoreType.DMA((n,))],
    )(*xs, *extra)


_MASKS = {
    "gather": tuple(range(1, N_DEV)),
    "scatter": tuple(range(1, N_DEV)),
    "own": (1, 4, 2, 6),
    "pass": (4, 2, 6),
}


def _split_copies(kind, outgoing, x_refs, land_refs, send_sems, recv_sems):
    me = _mesh_pos()
    masks = _MASKS[kind]
    copies = []
    for k, (x_ref, land_ref) in enumerate(zip(x_refs, land_refs)):
        for j, mask in enumerate(masks):
            if kind == "pass":
                peer = _flip(me, 1)
                src = land_ref.at[_index(_flip(me, mask))]
                dst = land_ref.at[_index(_flip(me, mask if outgoing else mask | 1))]
            else:
                peer = _flip(me, mask)
                src = x_ref.at[_index(peer)] if kind == "scatter" else x_ref
                dst = land_ref.at[_index(me if outgoing else peer)]
            sem = k * len(masks) + j
            copies.append(pltpu.make_async_remote_copy(src_ref=src, dst_ref=dst, send_sem=send_sems.at[sem], recv_sem=recv_sems.at[sem],
                                                       device_id=peer, device_id_type=MESH))
    return copies


_HBM_SPEC = pl.BlockSpec(memory_space=pltpu.HBM)
_SEM_SPEC = pl.BlockSpec(memory_space=pltpu.SEMAPHORE)
_EFFECT = pltpu.SideEffectType.DATAFLOW_SIDE_EFFECTING


def _split_start(name, kind, xs, after=None):
    n = len(xs)
    n_sem = n * len(_MASKS[kind])
    extra = [] if after is None else [after]
    lands = [lax.empty(x.shape if kind == "scatter" else (N_DEV,) + x.shape, x.dtype) for x in xs]

    def body(*refs):
        x_refs, land_refs = refs[:n], refs[n:2 * n]
        send_sems, recv_sems = refs[2 * n + len(extra)], refs[2 * n + len(extra) + 1]
        token = refs[-1]
        for cp in _split_copies(kind, True, x_refs, land_refs, send_sems, recv_sems):
            cp.start()
        token[...] = jnp.zeros_like(token)

    hbm = [pltpu.HBM(a.shape, a.dtype) for a in list(xs) + lands]
    res = pl.pallas_call(
        body, name=name,
        out_shape=[pltpu.SemaphoreType.DMA((n_sem,)), pltpu.SemaphoreType.DMA((n_sem,))] + hbm + [_sds((8, 128), F32)],
        in_specs=[_HBM_SPEC] * (2 * n) + [pl.BlockSpec(memory_space=pl.ANY)] * len(extra),
        out_specs=[_SEM_SPEC, _SEM_SPEC] + [_HBM_SPEC] * (2 * n) + [pl.BlockSpec(memory_space=pltpu.VMEM)],
        input_output_aliases={j: 2 + j for j in range(2 * n)}, compiler_params=pltpu.CompilerParams(has_side_effects=_EFFECT),
    )(*[pltpu.with_memory_space_constraint(a, pltpu.HBM) for a in list(xs) + lands], *extra)
    return (kind, n, res[0], res[1], res[2:2 + 2 * n]), res[-1]


def _split_pass(name, state, after):
    kind, n, send_sems_in, recv_sems_in, thru = state
    n_sem = n * len(_MASKS["pass"])

    def body(*refs):
        x_refs, land_refs = refs[:n], refs[n:2 * n]
        send_sems, recv_sems = refs[2 * n], refs[2 * n + 1]
        next_send, next_recv, token = refs[-3:]
        for cp in _split_copies(kind, True, x_refs, land_refs, send_sems, recv_sems):
            cp.wait_send()
        for cp in _split_copies(kind, False, x_refs, land_refs, send_sems, recv_sems):
            cp.wait_recv()
        for cp in _split_copies("pass", True, land_refs, land_refs, next_send, next_recv):
            cp.start()
        token[...] = jnp.zeros_like(token)

    res = pl.pallas_call(
        body, name=name,
        out_shape=[pltpu.HBM(a.shape, a.dtype) for a in thru] + [pltpu.SemaphoreType.DMA((n_sem,)), pltpu.SemaphoreType.DMA((n_sem,)),
                                                                 _sds((8, 128), F32)],
        in_specs=[_HBM_SPEC] * (2 * n) + [_SEM_SPEC, _SEM_SPEC, pl.BlockSpec(memory_space=pl.ANY)],
        out_specs=[_HBM_SPEC] * (2 * n) + [_SEM_SPEC, _SEM_SPEC, pl.BlockSpec(memory_space=pltpu.VMEM)],
        input_output_aliases={j: j for j in range(2 * n)}, compiler_params=pltpu.CompilerParams(has_side_effects=_EFFECT),
    )(*thru, send_sems_in, recv_sems_in, after)
    return ("pass", n, res[2 * n], res[2 * n + 1], res[:2 * n]), res[-1]


def _split_wait(name, state, after):
    kind, n, send_sems_in, recv_sems_in, thru = state

    def body(*refs):
        x_refs, land_refs = refs[:n], refs[n:2 * n]
        send_sems, recv_sems = refs[2 * n], refs[2 * n + 1]
        for cp in _split_copies(kind, True, x_refs, land_refs, send_sems, recv_sems):
            cp.wait_send()
        for cp in _split_copies(kind, False, x_refs, land_refs, send_sems, recv_sems):
            cp.wait_recv()

    res = pl.pallas_call(
        body, name=name, out_shape=[pltpu.HBM(a.shape, a.dtype) for a in thru],
        in_specs=[_HBM_SPEC] * (2 * n) + [_SEM_SPEC, _SEM_SPEC, pl.BlockSpec(memory_space=pl.ANY)], out_specs=[_HBM_SPEC] * (2 * n),
        input_output_aliases={j: j for j in range(2 * n)}, compiler_params=pltpu.CompilerParams(has_side_effects=_EFFECT),
    )(*thru, send_sems_in, recv_sems_in, after)
    me = _index(_mesh_pos())
    out = []
    for x, land in zip(res[:n], res[n:]):
        own = lax.dynamic_index_in_dim(x, me, 0, keepdims=False) if kind == "scatter" else x
        out.append(lax.dynamic_update_slice(land, own[None], (me,) + (0,) * own.ndim))
    return out


def _rope_tables(s):
    inv = 1.0 / (ROPE_THETA ** (jnp.arange(0, QK_ROPE, 2, dtype=F32) / QK_ROPE))
    ang = jnp.arange(s, dtype=F32)[:, None] * inv[None, :]
    pad = jnp.zeros((s, KV_LORA - QK_ROPE), F32)
    return (jnp.concatenate([jnp.cos(ang), jnp.cos(ang), pad], axis=1), jnp.concatenate([jnp.sin(ang), jnp.sin(ang), pad], axis=1))


def kernel(x, c, ada_w, ada_b, norm_g, ffn_w_in, ffn_w_out, pool_w, pool_b, pool_scale, mla_w_in, mla_q_norm, mla_kv_norm, mla_w_uq, mla_w_uk, mla_w_uv, mla_w_o, loss_target, m_ada_w, m_ada_b, m_norm_g, m_ffn_w_in, m_ffn_w_out, m_pool_w, m_pool_b, m_pool_scale, m_mla_w_in, m_mla_q_norm, m_mla_kv_norm, m_mla_w_uq, m_mla_w_uk, m_mla_w_uv, m_mla_w_o, v_ada_w, v_ada_b, v_norm_g, v_ffn_w_in, v_ffn_w_out, v_pool_w, v_pool_b, v_pool_scale, v_mla_w_in, v_mla_q_norm, v_mla_kv_norm, v_mla_w_uq, v_mla_w_uk, v_mla_w_uv, v_mla_w_o):
    s, d = x.shape[1], x.shape[2]
    tm = min(512, s)
    tr = min(512, s)
    tf = min(1024, s)
    tw = min(2048, s)
    me = 4 * lax.axis_index("x") + 2 * lax.axis_index("y") + lax.axis_index("c")
    x0 = x.reshape(s, d)
    target = loss_target.reshape(s, d)
    n_mod = ada_w.shape[2] * N_DEV // d
    mod_blk = ada_w.shape[2]

    small = jnp.concatenate([c.reshape(-1), norm_g.reshape(-1), pool_b.reshape(-1), mla_q_norm.reshape(-1)]).reshape(1, -1)
    w_in_t, m_in_t, v_in_t = (jnp.swapaxes(a, 2, 3) for a in (ffn_w_in, m_ffn_w_in, v_ffn_w_in))
    w_in_loc = [w_in_t[i, f].astype(BF16) for i in range(2) for f in range(2)]
    w_out_loc = [ffn_w_out[i, f].astype(BF16) for i in range(2) for f in range(2)]
    (small_all,) = _all_gather("gather_small", [small])
    small_all = small_all.reshape(N_DEV, -1)
    c_all = small_all[:, :d]
    off = d
    g_all = small_all[:, off:off + 12 * (d // N_DEV)].reshape(N_DEV, 2, 6, d // N_DEV).transpose(1, 2, 0, 3).reshape(2, 6, d)
    off += 12 * (d // N_DEV)
    pool_b_all = small_all[:, off:off + 4 * POOL_SHARD].reshape(N_DEV, 4, POOL_SHARD).transpose(1, 0, 2).reshape(1, d)
    off += 4 * POOL_SHARD
    q_norm_all = small_all[:, off:off + Q_SHARD].reshape(1, Q_LORA)
    kv_norm_row = mla_kv_norm.reshape(1, KV_LORA)
    pscale_row = pool_scale.reshape(1, d)

    even = (jnp.arange(N_HEADS) % 2 == 0)[:, None, None]
    cos_k, sin_k = _rope_tables(s)

    def mla_weights(mla_w_in_all, mla_w_uq_all, mla_w_o_all):
        uq = mla_w_uq_all.reshape(Q_LORA, N_HEADS, QK_NOPE + QK_ROPE).transpose(1, 0, 2)
        zq = jnp.zeros((N_HEADS, Q_LORA, QK_NOPE), BF16)
        wq = jnp.concatenate(
            [uq[:, :, :QK_NOPE], zq, uq[:, :, QK_NOPE:], jnp.zeros((N_HEADS, Q_LORA, QK_PAD - KV_LORA - QK_ROPE), BF16)], axis=2)
        wukp = jnp.pad(mla_w_uk[0].transpose(1, 2, 0).astype(BF16), ((0, 0), (0, QK_PAD - QK_NOPE), (0, QK_PAD - KV_LORA)))
        uv = mla_w_uv[0].transpose(1, 0, 2).astype(BF16)
        wuv2 = jnp.where(even, jnp.concatenate([uv, jnp.zeros_like(uv)], axis=2), jnp.concatenate([jnp.zeros_like(uv), uv], axis=2))
        return dict(w_in=jnp.pad(mla_w_in_all.reshape(d, -1), ((0, 0), (0, LAT_PAD - mla_w_in.shape[2]))), wq=wq, wukp=wukp,
                    wuv2=wuv2, w_o=mla_w_o_all.reshape(d, d), q_norm=q_norm_all, kv_norm=kv_norm_row)

    (sc_all,), _ = _rowmap("ada_silu", lambda cv: ((cv * jax.nn.sigmoid(cv),), ()), [(c_all, (N_DEV, d), lambda i: (0, 0))],
                           [(_sds((N_DEV, d), F32), (N_DEV, d), lambda i: (0, 0))], [], (1,))
    ada_b_loc = lax.dynamic_slice_in_dim(ada_b, me * mod_blk, mod_blk, axis=1).reshape(2, 1, mod_blk)
    m_pad = 2 * N_DEV
    modp = _matmul("ada_mod", jnp.pad(sc_all, ((0, m_pad - N_DEV), (0, 0))), ada_w, a_blk=(m_pad, d), a_map=lambda i, k: (0, 0),
                   b_blk=(None, d, mod_blk), b_map=lambda i, k: (i, 0, 0), o_shape=(2, m_pad, mod_blk), o_blk=(None, m_pad, mod_blk),
                   o_map=lambda i, k: (i, 0, 0), grid=(2, 1), contract=NN, out_dtype=F32, bias=ada_b_loc, bias_blk=(None, 1, mod_blk),
                   bias_map=lambda i, k: (i, 0, 0))[:, :N_DEV]
    (modp_all,) = _all_gather("gather_mod", [modp.reshape(2 * N_DEV, mod_blk)])
    groups = [[w_in_loc[0], w_out_loc[0], pool_w.reshape(-1, POOL_GROUP).astype(BF16)], [w_in_loc[1], w_out_loc[1]],
              [w_in_loc[2], w_out_loc[2], mla_w_in[0].astype(BF16), mla_w_uq.reshape(mla_w_uq.shape[1], -1).astype(BF16),
               mla_w_o[0].astype(BF16)], [w_in_loc[3], w_out_loc[3]]]
    state, token = _split_start("gather_start_0", "own", groups[0], after=modp_all)
    states = [state]
    w_in8 = [None] * 4
    w_out4 = [None] * 4
    mod = lax.dynamic_index_in_dim(modp_all.reshape(N_DEV, 2, N_DEV, mod_blk), me, axis=2, keepdims=False)
    mod = mod.transpose(1, 0, 2).reshape(2, n_mod, d) + token[0, 0]
    mod_rows = mod.reshape(2 * n_mod, 1, d)
    g_rows = g_all.reshape(12, 1, d)
    weights_of = (0.5, 1.0, 0.5, 0.5, 1.0, 0.5)

    def pre_rows(k, with_shift):
        rows = [_rowk(g_rows, 2 * k), _rowk(mod_rows, 3 * k + 1)]
        return rows + [_rowk(mod_rows, 3 * k)] if with_shift else rows

    def post_rows(k):
        return [_rowk(g_rows, 2 * k + 1), _rowk(mod_rows, 3 * k + 2)]

    def act_dtype(k):
        return F32 if k == 1 else BF16

    saved = []
    xs = x0
    mla_wts = None
    h = _prenorm("prenorm_0", xs, pre_rows(0, True), act_dtype(0), tr)
    token = h
    for name, group in zip("abc", groups[1:]):
        state, token = _split_start(f"gather_start_{name}", "own", group, after=token)
        states.append(state)
    state, token = _split_pass("gather_pass_0", states[0], token)
    lands = _split_wait("gather_wait_0", state, token)
    w_in8[0], w_out4[0] = lands[0], lands[1].reshape(4, FF_BLK, d)
    w4 = lands[2].reshape(N_DEV, 4, POOL_SHARD, POOL_GROUP).transpose(1, 0, 2, 3).reshape(4, POOL_GROUP, POOL_GROUP)
    token = None
    for k in range(6):
        i, sub = divmod(k, 3)
        tag = f"{i}{sub}"
        if k == 1:
            states[1], token = _split_pass("gather_pass_a", states[1], xs)
        if k == 2:
            lands = _split_wait("gather_wait_a", states[1], xs)
            w_in8[1], w_out4[1] = lands[0], lands[1].reshape(4, FF_BLK, d)
            states[2], token = _split_pass("gather_pass_b", states[2], lands[0])
        if k == 3:
            lands = _split_wait("gather_wait_b", states[2], xs)
            w_in8[2], w_out4[2] = lands[0], lands[1].reshape(4, FF_BLK, d)
            mla_wts = mla_weights(*lands[2:])
            states[3], token = _split_pass("gather_pass_c", states[3], lands[0])
        if k == 5:
            lands = _split_wait("gather_wait_c", states[3], xs)
            w_in8[3], w_out4[3] = lands[0], lands[1].reshape(4, FF_BLK, d)
        if sub != 1:
            u, extra = _ffn_fwd(tag, h, w_in8[2 * i + sub // 2], w_out4[2 * i + sub // 2], tf, tf // 2, after=token)
            token = None
        elif i == 0:
            u, z, v = _pool_fwd(h, w4, pool_b_all + token[0, 0], pscale_row, min(4 * tm, s))
            extra = (z, v)
            token = None
        else:
            u, extra = _mla_fwd(h, mla_wts, cos_k, sin_k, tm)
        saved.append((xs, h, u, extra))
        if k < 5:
            xs, h = _norm_link(f"norm_link_{k + 1}", xs, u, post_rows(k), weights_of[k], pre_rows(k + 1, True), act_dtype(k + 1), tr)

    dx, du, (sq, dgate, dgpost) = _norm_loss("norm_loss", xs, u, target, post_rows(5), weights_of[5], tr)
    loss_part = (0.5 * jnp.sum(sq) / d).reshape(1, 1)

    d_mod = [None] * (2 * n_mod)
    d_g = [None] * 12
    d_mod[3 * 5 + 2], d_g[2 * 5 + 1] = dgate, dgpost
    sent = {}
    pool_grads = mla_grads = None

    def start_scatter(key, arrays):
        state, token = _split_start(f"scatter_start_{key}", "scatter", arrays)
        sent[key] = state
        return token

    for k in (5, 4, 3, 2, 1, 0):
        i, sub = divmod(k, 3)
        tag = f"{i}{sub}"
        xin, h, u, extra = saved[k]
        token = None
        if sub != 1:
            f = 2 * i + sub // 2
            dh, dgu, act = _ffn_bwd_act(tag, du, extra, w_in8[f], w_out4[f], tf, tf // 4)
            dw_out = _ffn_dw(f"ffn_dwout_{tag}", act, du, tw).reshape(N_DEV, FF_BLK // 2, d)
            if k == 0:
                d_pool_w = pool_grads[0].reshape(4, N_DEV, POOL_SHARD, POOL_GROUP).transpose(1, 0, 2, 3).reshape(N_DEV, -1, POOL_GROUP)
                token = start_scatter(tag + "_out", [dw_out, d_pool_w])
                token = start_scatter(tag + "_in", [_ffn_dw(f"ffn_dwin_{tag}", dgu, h, tw, after=token)])
            else:
                token = start_scatter(tag, [_ffn_dw(f"ffn_dwin_{tag}", dgu, h, tw), dw_out])
        elif i == 0:
            dh, dw4, (dpscale, dpb) = _pool_bwd(du, extra[0], extra[1], w4, pscale_row, min(4 * tm, s))
            pool_grads = (dw4, dpscale, dpb)
        else:
            dh, mla_grads = _mla_bwd(du, h, extra, mla_wts, cos_k, sin_k, tm)
            dwq = mla_grads["wq"]
            d_uq = jnp.concatenate([dwq[:, :, :QK_NOPE], dwq[:, :, KV_LORA:KV_LORA + QK_ROPE]], axis=2).transpose(1, 0, 2)
            token = start_scatter("mla", [mla_grads["w_in"][:, :mla_w_in.shape[2]].reshape(N_DEV, d // N_DEV, -1),
                                          d_uq.reshape(N_DEV, Q_LORA // N_DEV, -1), mla_grads["w_o"].reshape(N_DEV, d // N_DEV, d)])
            dwukp, dwuv2 = mla_grads["wukp"], mla_grads["wuv2"]
            d_uk = dwukp[:, :QK_NOPE, :KV_LORA].transpose(2, 0, 1).reshape(KV_LORA, -1)
            d_uv = jnp.where(even, dwuv2[:, :, :V_HEAD], dwuv2[:, :, V_HEAD:]).transpose(1, 0, 2).reshape(KV_LORA, -1)
            state_ukv, token = _split_start("gather_start_ukv", "gather", [d_uk, d_uv], after=token)
        if k > 0:
            dx, du, reds = _norm_link_bwd(f"norm_link_bwd_{k}", dh, xin, dx, saved[k - 1][2], pre_rows(k, False), post_rows(k - 1),
                                          weights_of[k - 1], act_dtype(k - 1), tr, after=token)
            d_mod[3 * (k - 1) + 2], d_g[2 * (k - 1) + 1] = reds[3], reds[4]
        else:
            dx, reds = _prenorm_bwd("prenorm_bwd_0", dh, xin, dx, pre_rows(0, False), tr, after=token)
        d_mod[3 * k], d_mod[3 * k + 1], d_g[2 * k] = reds[0], reds[1], reds[2]
    grad_x = dx.reshape(x.shape)

    def upd(name, parts, w, m, v):
        shape = w.shape
        r, cdim = parts.shape[1], parts.shape[2]
        return [o.reshape(shape) for o in _adamw(name, parts, w.reshape(r, cdim), m.reshape(r, cdim), v.reshape(r, cdim))]

    def landed(key, after):
        return _split_wait(f"scatter_wait_{key}", sent[key], after)

    res = {}
    w_in_s, m_in_s, v_in_s = (a.reshape(4, FF_BLK, d) for a in (w_in_t, m_in_t, v_in_t))
    w_out_s, m_out_s, v_out_s = (a.reshape(4, FF_BLK // 2, d) for a in (ffn_w_out, m_ffn_w_out, v_ffn_w_out))
    bufs_in = [lax.empty(w_in_s.shape, F32) for _ in range(4)]
    bufs_out = [lax.empty(w_out_s.shape, F32) for _ in range(4)]
    for key, k in (("12", 3), ("mla", None), ("10", 2), ("02", 1)):
        parts = landed(key, grad_x)
        if k is None:
            res["mla_w_in"] = upd("adam_mla_w_in", parts[0], mla_w_in, m_mla_w_in, v_mla_w_in)
            res["mla_w_uq"] = upd("adam_mla_w_uq", parts[1], mla_w_uq, m_mla_w_uq, v_mla_w_uq)
            res["mla_w_o"] = upd("adam_mla_w_o", parts[2], mla_w_o, m_mla_w_o, v_mla_w_o)
            uk_all, uv_all = _split_wait("gather_wait_ukv", state_ukv, grad_x)
            res["mla_w_uk"] = upd("adam_mla_w_uk", uk_all, mla_w_uk, m_mla_w_uk, v_mla_w_uk)
            res["mla_w_uv"] = upd("adam_mla_w_uv", uv_all, mla_w_uv, m_mla_w_uv, v_mla_w_uv)
            continue
        bufs_in = _adamw_slab(f"adam_ffn_w_in_{key}", parts[0], w_in_s, m_in_s, v_in_s, bufs_in, k)
        bufs_out = _adamw_slab(f"adam_ffn_w_out_{key}", parts[1], w_out_s, m_out_s, v_out_s, bufs_out, k)

    dw4, dpscale, dpb = pool_grads
    small_g = jnp.concatenate(d_mod + d_g + [dpb, dpscale, mla_grads["q_norm"], mla_grads["kv_norm"], loss_part], axis=1)
    done = [bufs_in[0], bufs_out[0], res["mla_w_o"][0], res["mla_w_uv"][0]]
    (small_g_all,) = _all_gather("gather_small_grads", [small_g], after=sum(a.reshape(-1)[:1] for a in done))
    small_g_all = small_g_all.reshape(N_DEV, -1)
    n_m = 2 * n_mod * d
    d_mod_all = small_g_all[:, :n_m].reshape(N_DEV, 2, n_mod * d)
    rest = small_g_all[:, n_m:]
    p_norm_g = lax.dynamic_slice_in_dim(rest[:, :12 * d].reshape(N_DEV, 12, d), me * (d // N_DEV), d // N_DEV, axis=2)
    p_pool_b = lax.dynamic_slice_in_dim(rest[:, 12 * d:13 * d].reshape(N_DEV, 4, POOL_GROUP), me * POOL_SHARD, POOL_SHARD, axis=2)
    p_pool_scale = rest[:, 13 * d:14 * d].reshape(N_DEV, 1, d)
    p_q_norm = lax.dynamic_slice_in_dim(rest[:, 14 * d:14 * d + Q_LORA], me * Q_SHARD, Q_SHARD, axis=1).reshape(N_DEV, 1, Q_SHARD)
    p_kv_norm = rest[:, 14 * d + Q_LORA:14 * d + Q_LORA + KV_LORA].reshape(N_DEV, 1, KV_LORA)
    loss = jnp.sum(rest[:, -1])

    d_mod_loc = lax.dynamic_slice_in_dim(d_mod_all, me * mod_blk, mod_blk, axis=2).transpose(1, 0, 2)
    k_pad = 128
    sc_t = jnp.pad(sc_all.T, ((0, 0), (0, k_pad - N_DEV)))
    d_ada_w = _matmul("ada_dw", sc_t, jnp.pad(d_mod_loc, ((0, 0), (0, k_pad - N_DEV), (0, 0))), a_blk=(d, k_pad),
                      a_map=lambda i, k: (0, 0), b_blk=(None, k_pad, mod_blk), b_map=lambda i, k: (i, 0, 0), o_shape=(2, d, mod_blk),
                      o_blk=(None, d, mod_blk), o_map=lambda i, k: (i, 0, 0), grid=(2, 1), contract=NN, out_dtype=F32)
    res["ada_w"] = upd("adam_ada_w", d_ada_w.reshape(1, 2 * d, mod_blk), ada_w, m_ada_w, v_ada_w)
    res["ada_b"] = upd("adam_ada_b", d_mod_all.reshape(N_DEV, 2, n_mod * d), ada_b, m_ada_b, v_ada_b)
    res["norm_g"] = upd("adam_norm_g", p_norm_g, norm_g, m_norm_g, v_norm_g)
    res["pool_b"] = upd("adam_pool_b", p_pool_b, pool_b, m_pool_b, v_pool_b)
    res["pool_scale"] = upd("adam_pool_scale", p_pool_scale, pool_scale, m_pool_scale, v_pool_scale)
    res["mla_q_norm"] = upd("adam_mla_q_norm", p_q_norm, mla_q_norm, m_mla_q_norm, v_mla_q_norm)
    res["mla_kv_norm"] = upd("adam_mla_kv_norm", p_kv_norm, mla_kv_norm, m_mla_kv_norm, v_mla_kv_norm)

    p_out, p_pool_w = landed("00_out", res["ada_w"][1])
    bufs_out = _adamw_slab("adam_ffn_w_out_00", p_out, w_out_s, m_out_s, v_out_s, bufs_out, 0)
    res["pool_w"] = upd("adam_pool_w", p_pool_w, pool_w, m_pool_w, v_pool_w)
    (p_in,) = landed("00_in", res["pool_w"][1])
    bufs_in = _adamw_slab("adam_ffn_w_in_00", p_in, w_in_s, m_in_s, v_in_s, bufs_in, 0)
    res["ffn_w_in"] = [jnp.swapaxes(b.reshape(w_in_t.shape), 2, 3) for b in bufs_in]
    res["ffn_w_out"] = [b.reshape(ffn_w_out.shape) for b in bufs_out]

    order = ["ada_w", "ada_b", "norm_g", "ffn_w_in", "ffn_w_out", "pool_w", "pool_b", "pool_scale", "mla_w_in", "mla_q_norm",
             "mla_kv_norm", "mla_w_uq", "mla_w_uk", "mla_w_uv", "mla_w_o"]
    outs = [loss, grad_x]
    for j in range(4):
        outs += [res[name][j] for name in order]
    return tuple(outs)
```

```python
import jax
import jax.numpy as jnp
from jax import lax
from jax.experimental import pallas as pl
from jax.experimental.pallas import tpu as pltpu

F32 = jnp.float32
BF16 = jnp.bfloat16
N_DEV = 8
MESH = pl.DeviceIdType.MESH

D_MODEL = 1024
N_HEADS = 16
QK_NOPE = 64
QK_ROPE = 32
V_HEAD = 64
Q_LORA = 256
KV_LORA = 128
LAT_PAD = 512
QK_PAD = 256
ONE_COL = 160
D_FF = 2816
FF_BLK = 2 * D_FF // N_DEV
POOL_WINDOWS = (2, 4, 8, 16)
POOL_GROUP = 256
POOL_SHARD = POOL_GROUP // N_DEV
Q_SHARD = Q_LORA // N_DEV
ROPE_THETA = 10000.0
EPS = 1e-6
ATTN_SCALE = (QK_NOPE + QK_ROPE) ** -0.5
LOG2_E = 1.4426950408889634
ADAM_LR, ADAM_B1, ADAM_B2, ADAM_EPS, ADAM_WD, ADAM_STEP = 0.001, 0.9, 0.999, 1e-08, 0.01, 10
VMEM_LIMIT = 56 * 1024 * 1024

NN = ((1,), (0,))
NT = ((1,), (1,))
TN = ((0,), (0,))


def _params(**kw):
    return pltpu.CompilerParams(vmem_limit_bytes=VMEM_LIMIT, **kw)


def _dot(a, b, contract):
    return lax.dot_general(a, b, (contract, ((), ())), preferred_element_type=F32)


def _matmul(name, a, b, *, a_blk, a_map, b_blk, b_map, o_shape, o_blk, o_map, grid, contract, out_dtype,
            bias=None, bias_blk=None, bias_map=None, after=None):
    n_k = grid[-1]
    k_axis = len(grid) - 1
    acc_shape = tuple(d for d in o_blk if d is not None)

    def body(*refs):
        a_ref, b_ref = refs[:2]
        bias_ref = refs[2] if bias is not None else None
        if n_k == 1:
            r = _dot(a_ref[...].astype(BF16), b_ref[...].astype(BF16), contract)
            if bias is not None:
                r = r + bias_ref[...]
            refs[-1][...] = r.astype(refs[-1].dtype)
            return
        o_ref, acc = refs[-2:]
        k = pl.program_id(k_axis)

        @pl.when(k == 0)
        def _():
            acc[...] = jnp.zeros_like(acc)

        acc[...] += _dot(a_ref[...].astype(BF16), b_ref[...].astype(BF16), contract)

        @pl.when(k == n_k - 1)
        def _():
            r = acc[...]
            if bias is not None:
                r = r + bias_ref[...]
            o_ref[...] = r.astype(o_ref.dtype)

    in_specs = [pl.BlockSpec(a_blk, a_map), pl.BlockSpec(b_blk, b_map)]
    args = [a, b]
    if bias is not None:
        in_specs.append(pl.BlockSpec(bias_blk, bias_map))
        args.append(bias)
    if after is not None:
        in_specs.append(pl.BlockSpec(memory_space=pl.ANY))
        args.append(after)
    return pl.pallas_call(
        body, name=name, grid=grid, in_specs=in_specs, out_specs=pl.BlockSpec(o_blk, o_map),
        out_shape=jax.ShapeDtypeStruct(o_shape, out_dtype), scratch_shapes=[pltpu.VMEM(acc_shape, F32)] if n_k > 1 else [],
        compiler_params=_params(),
    )(*args)


def _rowmap(name, fn, ins, outs, reds, grid, after=None):
    n_in, n_out, n_red = len(ins), len(outs), len(reds)
    extra = [] if after is None else [after]

    def body(*refs):
        in_refs = refs[:n_in]
        out_refs = refs[n_in + len(extra):n_in + len(extra) + n_out]
        red_refs = refs[n_in + len(extra) + n_out:]
        out_vals, red_vals = fn(*[r[...] for r in in_refs])
        for r, v in zip(out_refs, out_vals):
            r[...] = v.astype(r.dtype)
        if n_red:
            first = pl.program_id(0) == 0
            for ax in range(1, len(grid)):
                first = jnp.logical_and(first, pl.program_id(ax) == 0)

            @pl.when(first)
            def _():
                for r in red_refs:
                    r[...] = jnp.zeros_like(r)

            for r, v in zip(red_refs, red_vals):
                r[...] += v

    res = pl.pallas_call(
        body, name=name, grid=grid,
        in_specs=[pl.BlockSpec(blk, imap) for _, blk, imap in ins] + [pl.BlockSpec(memory_space=pl.ANY)] * len(extra),
        out_specs=[pl.BlockSpec(blk, imap) for _, blk, imap in list(outs) + list(reds)],
        out_shape=[sds for sds, _, _ in list(outs) + list(reds)],
        compiler_params=_params(),
    )(*[a for a, _, _ in ins], *extra)
    return res[:n_out], res[n_out:]


def _sds(shape, dtype):
    return jax.ShapeDtypeStruct(shape, dtype)


def _tile(a, tm):
    return (a, (tm, a.shape[1]), lambda i: (i, 0))


def _row(a):
    return (a, (1, a.shape[1]), lambda i: (0, 0))


def _otile(n, c, dtype, tm):
    return (_sds((n, c), dtype), (tm, c), lambda i: (i, 0))


def _ored(c):
    return (_sds((1, c), F32), (1, c), lambda i: (0, 0))


def _colsum(v):
    return jnp.sum(v, axis=0, keepdims=True)


def _rstd(v):
    return lax.rsqrt(jnp.mean(v * v, axis=-1, keepdims=True) + EPS)


def _pre(xv, g, sc, sh):
    return xv * _rstd(xv) * g * (1.0 + sc) + sh


def _post(xv, uv, g, gt, weight):
    uv = uv.astype(F32)
    return xv + weight * (1.0 + gt) * (uv * _rstd(uv) * g)


def _post_bwd(dv, uv, g, gt, weight):
    uv = uv.astype(F32)
    r = _rstd(uv)
    un = uv * r
    dy = dv * (weight * (1.0 + gt))
    a = dy * g
    du = r * (a - un * jnp.mean(a * un, axis=-1, keepdims=True))
    return du, (_colsum(dv * (weight * (un * g))), _colsum(dy * un))


def _pre_bwd(dhv, xv, dv, g, sc):
    dhv = dhv.astype(F32)
    r = _rstd(xv)
    xn = xv * r
    b = dhv * (g * (1.0 + sc))
    dx = dv + r * (b - xn * jnp.mean(b * xn, axis=-1, keepdims=True))
    return dx, (_colsum(dhv), _colsum(dhv * (xn * g)), _colsum(dhv * ((1.0 + sc) * xn)))


def _rowk(rows, k):
    return (rows, (None, 1, rows.shape[2]), lambda i: (k, 0, 0))


def _prenorm(name, x, pre, out_dtype, tm):
    n, d = x.shape
    (h,), _ = _rowmap(name, lambda xv, g, sc, sh: ((_pre(xv, g, sc, sh),), ()), [_tile(x, tm), *pre], [_otile(n, d, out_dtype, tm)],
                      [], (n // tm,))
    return h


def _norm_link(name, x, u, post, weight, pre, out_dtype, tm, u_scale=None):
    n, d = x.shape
    rows = [] if u_scale is None else [_row(u_scale)]

    def fn(xv, uv, g, gt, g2, sc, sh, *ps):
        xn = _post(xv, uv * ps[0] if ps else uv, g, gt, weight)
        return (xn, _pre(xn, g2, sc, sh)), ()

    (xn, h), _ = _rowmap(name, fn, [_tile(x, tm), _tile(u, tm), *post, *pre, *rows],
                         [_otile(n, d, F32, tm), _otile(n, d, out_dtype, tm)], [], (n // tm,))
    return xn, h


def _norm_loss(name, x, u, target, post, weight, tm):
    n, d = x.shape

    def fn(xv, uv, tv, g, gt):
        e = _post(xv, uv, g, gt, weight) - tv
        dv = e * (1.0 / d)
        du, reds = _post_bwd(dv, uv, g, gt, weight)
        return (dv, du), (_colsum(e * e), *reds)

    (dx, du), reds = _rowmap(name, fn, [_tile(x, tm), _tile(u, tm), _tile(target, tm), *post],
                             [_otile(n, d, F32, tm), _otile(n, d, BF16, tm)], [_ored(d)] * 3, (n // tm,))
    return dx, du, reds


def _norm_link_bwd(name, dh, x, dout, u_prev, pre, post_prev, weight_prev, out_dtype, tm, after=None, u_scale=None):
    n, d = x.shape
    rows = [] if u_scale is None else [_row(u_scale)]

    def fn(dhv, xv, dv, uv, g, sc, g2, gt, *ps):
        dx, reds = _pre_bwd(dhv, xv, dv, g, sc)
        du, reds_prev = _post_bwd(dx, uv * ps[0] if ps else uv, g2, gt, weight_prev)
        if not ps:
            return (dx, du), (*reds, *reds_prev)
        d_unscaled = du * ps[0]
        return (dx, d_unscaled), (*reds, *reds_prev, _colsum(du * uv), _colsum(d_unscaled))

    (dx, du), reds = _rowmap(name, fn, [_tile(dh, tm), _tile(x, tm), _tile(dout, tm), _tile(u_prev, tm), *pre, *post_prev, *rows],
                             [_otile(n, d, F32, tm), _otile(n, d, out_dtype, tm)], [_ored(d)] * (5 + 2 * len(rows)), (n // tm,),
                             after=after)
    return dx, du, reds


def _prenorm_bwd(name, dh, x, dout, pre, tm, after=None):
    n, d = x.shape

    def fn(dhv, xv, dv, g, sc):
        dx, reds = _pre_bwd(dhv, xv, dv, g, sc)
        return (dx,), reds

    (dx,), reds = _rowmap(name, fn, [_tile(dh, tm), _tile(x, tm), _tile(dout, tm), *pre], [_otile(n, d, F32, tm)], [_ored(d)] * 3,
                          (n // tm,), after=after)
    return dx, reds


def _ffn_fwd(tag, h, w_in8, w_out4, tm, sub, after=None):
    s, d = h.shape

    extra = [] if after is None else [after]

    def body(h_ref, wg_ref, wu_ref, wo_ref, *rest):
        u_ref, gu_ref, acc = rest[-3:]
        j = pl.program_id(1)

        @pl.when(j == 0)
        def _():
            acc[...] = jnp.zeros_like(acc)

        for r in range(tm // sub):
            rows = pl.ds(r * sub, sub)
            hv = h_ref[rows, :]
            gate = _dot(hv, wg_ref[...], NT)
            up = _dot(hv, wu_ref[...], NT)
            gu_ref[0, rows, :] = gate.astype(BF16)
            gu_ref[1, rows, :] = up.astype(BF16)
            acc[rows, :] += _dot((gate * jax.nn.sigmoid(gate) * up).astype(BF16), wo_ref[...], NN)

        @pl.when(j == w_out4.shape[0] - 1)
        def _():
            u_ref[...] = acc[...].astype(u_ref.dtype)

    w_blk = (None, FF_BLK, d)
    return pl.pallas_call(
        body, name=f"ffn_fwd_{tag}", grid=(s // tm, 4),
        in_specs=[pl.BlockSpec((tm, d), lambda i, j: (i, 0)), pl.BlockSpec(w_blk, lambda i, j: (j, 0, 0)),
                  pl.BlockSpec(w_blk, lambda i, j: (j + 4, 0, 0)), pl.BlockSpec((None, FF_BLK, d), lambda i, j: (j, 0, 0))]
        + [pl.BlockSpec(memory_space=pl.ANY)] * len(extra),
        out_specs=[pl.BlockSpec((tm, d), lambda i, j: (i, 0)), pl.BlockSpec((2, None, tm, FF_BLK), lambda i, j: (0, j, i, 0))],
        out_shape=[_sds((s, d), BF16), _sds((2, 4, s, FF_BLK), BF16)], scratch_shapes=[pltpu.VMEM((tm, d), F32)],
        compiler_params=_params(),
    )(h, w_in8, w_in8, w_out4, *extra)


def _ffn_bwd_act(tag, du, gu, w_in8, w_out4, tm, sub):
    s, d = du.shape

    def body(du_ref, gu_ref, wg_ref, wu_ref, wo_ref, dh_ref, dgu_ref, act_ref, acc):
        j = pl.program_id(1)

        @pl.when(j == 0)
        def _():
            acc[...] = jnp.zeros_like(acc)

        n_sub = tm // sub
        dact_next = _dot(du_ref[pl.ds(0, sub), :], wo_ref[...], NT)
        for r in range(n_sub):
            rows = pl.ds(r * sub, sub)
            dact = dact_next
            if r + 1 < n_sub:
                dact_next = _dot(du_ref[pl.ds((r + 1) * sub, sub), :], wo_ref[...], NT)
            gate, up = gu_ref[0, rows, :].astype(F32), gu_ref[1, rows, :].astype(F32)
            sg = jax.nn.sigmoid(gate)
            silu = gate * sg
            dg = dact * up * (sg * (1.0 + gate * (1.0 - sg)))
            dup = dact * silu
            dgu_ref[0, :, rows] = dg.T.astype(BF16)
            dgu_ref[1, :, rows] = dup.T.astype(BF16)
            act_ref[:, rows] = (silu * up).T.astype(BF16)
            acc[rows, :] += _dot(dg.astype(BF16), wg_ref[...], NN) + _dot(dup.astype(BF16), wu_ref[...], NN)

        @pl.when(j == w_out4.shape[0] - 1)
        def _():
            dh_ref[...] = acc[...].astype(dh_ref.dtype)

    w_blk = (None, FF_BLK, d)
    dh, dgu_t, act_t = pl.pallas_call(
        body, name=f"ffn_bwd_{tag}", grid=(s // tm, 4),
        in_specs=[pl.BlockSpec((tm, d), lambda i, j: (i, 0)), pl.BlockSpec((2, None, tm, FF_BLK), lambda i, j: (0, j, i, 0)),
                  pl.BlockSpec(w_blk, lambda i, j: (j, 0, 0)), pl.BlockSpec(w_blk, lambda i, j: (j + 4, 0, 0)),
                  pl.BlockSpec((None, FF_BLK, d), lambda i, j: (j, 0, 0))],
        out_specs=[pl.BlockSpec((tm, d), lambda i, j: (i, 0)), pl.BlockSpec((2, None, FF_BLK, tm), lambda i, j: (0, j, 0, i)),
                   pl.BlockSpec((None, FF_BLK, tm), lambda i, j: (j, 0, i))],
        out_shape=[_sds((s, d), BF16), _sds((2, 4, FF_BLK, s), BF16), _sds((4, FF_BLK, s), BF16)],
        scratch_shapes=[pltpu.VMEM((tm, d), F32)], compiler_params=_params(),
    )(du, gu, w_in8, w_in8, w_out4)
    return dh, dgu_t.reshape(8, FF_BLK, s), act_t


def _ffn_dw(name, lhs_t, rhs, tk, after=None):
    n_g, _, s = lhs_t.shape
    d = rhs.shape[1]
    return _matmul(name, lhs_t, rhs, a_blk=(None, FF_BLK, tk), a_map=lambda g, k: (g, 0, k), b_blk=(tk, d), b_map=lambda g, k: (k, 0),
                   o_shape=(n_g, FF_BLK, d), o_blk=(None, FF_BLK, d), o_map=lambda g, k: (g, 0, 0), grid=(n_g, s // tk),
                   contract=NN, out_dtype=BF16, after=after)


def _window_sum(x, window, transpose):
    s = x.shape[0]
    t = lax.broadcasted_iota(jnp.int32, (s, 1), 0)
    half = window // 2
    cnt = jnp.minimum(t + half, s) - jnp.maximum(t - half, 0)
    inv = 1.0 / cnt.astype(F32)
    if transpose:
        x = x * inv
        offsets = range(-half + 1, half + 1)
    else:
        offsets = range(-half, half)
    acc = jnp.zeros_like(x)
    for o in offsets:
        shifted = x if o == 0 else pltpu.roll(x, (-o) % s, 0)
        valid = jnp.logical_and(t + o >= 0, t + o < s)
        acc = acc + jnp.where(valid, shifted, 0.0)
    return acc if transpose else acc * inv


def _pool_mix(name, x, transpose, out_dtype):
    s, d = x.shape

    def body(x_ref, o_ref):
        g = pl.program_id(0)
        for gi, window in enumerate(POOL_WINDOWS):
            @pl.when(g == gi)
            def _(window=window):
                xv = x_ref[...].astype(F32)
                o_ref[...] = (_window_sum(xv, window, transpose) - xv).astype(o_ref.dtype)

    return pl.pallas_call(
        body, name=name, grid=(len(POOL_WINDOWS),), in_specs=[pl.BlockSpec((s, POOL_GROUP), lambda g: (0, g))],
        out_specs=pl.BlockSpec((s, POOL_GROUP), lambda g: (0, g)), out_shape=_sds((s, d), out_dtype), compiler_params=_params(),
    )(x)


def _pool_fwd(h, w4, bias, tm):
    s, d = h.shape
    nt = s // tm
    z = _pool_mix("pool_mix", h, False, BF16)
    v = _matmul("pool_proj", z, w4, a_blk=(tm, POOL_GROUP), a_map=lambda i, g, k: (i, g), b_blk=(None, POOL_GROUP, POOL_GROUP),
                b_map=lambda i, g, k: (g, 0, 0), o_shape=(s, d), o_blk=(tm, POOL_GROUP), o_map=lambda i, g, k: (i, g),
                grid=(nt, 4, 1), contract=NN, out_dtype=F32, bias=bias, bias_blk=(1, POOL_GROUP), bias_map=lambda i, g, k: (0, g))
    return v, z


def _pool_bwd(dv, z, w4, tm):
    s, d = dv.shape
    nt = s // tm
    dw4 = _matmul("pool_dw", z, dv, a_blk=(tm, POOL_GROUP), a_map=lambda g, k: (k, g), b_blk=(tm, POOL_GROUP),
                  b_map=lambda g, k: (k, g), o_shape=(4, POOL_GROUP, POOL_GROUP), o_blk=(None, POOL_GROUP, POOL_GROUP),
                  o_map=lambda g, k: (g, 0, 0), grid=(4, nt), contract=TN, out_dtype=F32)
    dz = _matmul("pool_dz", dv, w4, a_blk=(tm, POOL_GROUP), a_map=lambda i, g, k: (i, g), b_blk=(None, POOL_GROUP, POOL_GROUP),
                 b_map=lambda i, g, k: (g, 0, 0), o_shape=(s, d), o_blk=(tm, POOL_GROUP), o_map=lambda i, g, k: (i, g),
                 grid=(nt, 4, 1), contract=NT, out_dtype=F32)
    dh = _pool_mix("pool_mix_t", dz, True, BF16)
    return dh, dw4


def _lane(shape):
    return lax.broadcasted_iota(jnp.int32, shape, 1)


def _rope_swap(v, transpose):
    half = QK_ROPE // 2
    lane = _lane(v.shape)
    up = pltpu.roll(v, v.shape[1] - half, 1)
    down = pltpu.roll(v, half, 1)
    if transpose:
        return jnp.where(lane < half, up, jnp.where(lane < QK_ROPE, -down, 0.0))
    return jnp.where(lane < half, -up, jnp.where(lane < QK_ROPE, down, 0.0))


def _rope(v, cos, sin):
    return v * cos + _rope_swap(v, False) * sin


def _rope_t(g, cos, sin):
    return g * cos + _rope_swap(g * sin, True)


def _mla_mid(lat, q_norm, kv_norm, cos_k, sin_k, tm):
    s = lat.shape[0]

    def fn(lv, qn, kn, cs, sn):
        cq = lv[:, :Q_LORA]
        ckv = lv[:, Q_LORA:Q_LORA + KV_LORA]
        kr = lv[:, Q_LORA + KV_LORA:]
        cq = cq * _rstd(cq) * qn
        ckv = ckv * _rstd(ckv) * kn
        k_rope = jnp.where(_lane(kr.shape) == ONE_COL - KV_LORA, 1.0, _rope(kr, cs, sn))
        return (cq, jnp.concatenate([ckv, k_rope], axis=1)), ()

    (cq, kcat), _ = _rowmap("mla_mid", fn, [_tile(lat, tm), _row(q_norm), _row(kv_norm), _tile(cos_k, tm), _tile(sin_k, tm)],
                            [_otile(s, Q_LORA, BF16, tm), _otile(s, QK_PAD, BF16, tm)], [], (s // tm,))
    return cq, kcat


def _mla_mid_bwd(lat, dcq, dkcat, dv, q_norm, kv_norm, cos_k, sin_k, tm):
    s = lat.shape[0]

    def fn(lv, dq, dk, dvv, qn, kn, cs, sn):
        dk = dk * (1.0 / LOG2_E)
        cq = lv[:, :Q_LORA]
        ckv = lv[:, Q_LORA:Q_LORA + KV_LORA]
        rq, rk = _rstd(cq), _rstd(ckv)
        cqn, ckn = cq * rq, ckv * rk
        a = dq * qn
        d_cq = rq * (a - cqn * jnp.mean(a * cqn, axis=-1, keepdims=True))
        dckv = dk[:, :KV_LORA] + dvv
        a2 = dckv * kn
        d_ckv = rk * (a2 - ckn * jnp.mean(a2 * ckn, axis=-1, keepdims=True))
        d_kr = _rope_t(dk[:, KV_LORA:], cs, sn)
        return (jnp.concatenate([d_cq, d_ckv, d_kr], axis=1),), (_colsum(dq * cqn), _colsum(dckv * ckn))

    (dlat,), reds = _rowmap(
        "mla_mid_bwd", fn,
        [_tile(lat, tm), _tile(dcq, tm), _tile(dkcat, tm), _tile(dv, tm), _row(q_norm), _row(kv_norm), _tile(cos_k, tm),
         _tile(sin_k, tm)],
        [_otile(s, LAT_PAD, BF16, tm)], [_ored(Q_LORA), _ored(KV_LORA)], (s // tm,))
    return dlat, reds


def _mla_q(cq, wq, wukp, cos_k, sin_k, tm):
    s = cq.shape[0]

    def body(cq_ref, wq_ref, wuk_ref, cos_ref, sin_ref, o_ref):
        cqv, cs, sn = cq_ref[...], cos_ref[...], sin_ref[...]
        for h in range(N_HEADS):
            aq = _dot(cqv, wq_ref[h], NN)
            qlat = _dot(aq.astype(BF16), wuk_ref[h], NN)
            roped = _rope(aq[:, KV_LORA:], cs, sn)
            o_ref[h] = (jnp.concatenate([qlat[:, :KV_LORA], roped], axis=1) * (ATTN_SCALE * LOG2_E)).astype(o_ref.dtype)

    wblk = pl.BlockSpec((N_HEADS, QK_PAD, QK_PAD), lambda i: (0, 0, 0))
    tblk = pl.BlockSpec((tm, KV_LORA), lambda i: (i, 0))
    return pl.pallas_call(
        body, name="mla_q", grid=(s // tm,),
        in_specs=[pl.BlockSpec((tm, Q_LORA), lambda i: (i, 0)), wblk, wblk, tblk, tblk],
        out_specs=pl.BlockSpec((N_HEADS, tm, QK_PAD), lambda i: (0, i, 0)), out_shape=_sds((N_HEADS, s, QK_PAD), BF16),
        compiler_params=_params(),
    )(cq, wq, wukp, cos_k, sin_k)


def _mla_q_bwd(cq, wq, wukp, cos_k, sin_k, dqcat, tm):
    s = cq.shape[0]

    def body(cq_ref, wq_ref, wuk_ref, cos_ref, sin_ref, dq_ref, dcq_ref, dwq_ref, dwuk_ref):
        @pl.when(pl.program_id(0) == 0)
        def _():
            dwq_ref[...] = jnp.zeros_like(dwq_ref)
            dwuk_ref[...] = jnp.zeros_like(dwuk_ref)

        cqv, cs, sn = cq_ref[...], cos_ref[...], sin_ref[...]
        d_cq = jnp.zeros((tm, Q_LORA), F32)
        for h in range(N_HEADS):
            aq = _dot(cqv, wq_ref[h], NN).astype(BF16)
            g = dq_ref[h].astype(F32) * ATTN_SCALE
            gl, gr = g[:, :KV_LORA], g[:, KV_LORA:]
            dqlat = jnp.concatenate([gl, jnp.zeros_like(gl)], axis=1).astype(BF16)
            d_rope = _rope_t(gr, cs, sn)
            daq = _dot(dqlat, wuk_ref[h], NT) + jnp.concatenate([jnp.zeros_like(d_rope), d_rope], axis=1)
            daq_b = daq.astype(BF16)
            dwuk_ref[h] += _dot(aq, dqlat, TN)
            dwq_ref[h] += _dot(cqv, daq_b, TN)
            d_cq = d_cq + _dot(daq_b, wq_ref[h], NT)
        dcq_ref[...] = d_cq

    wblk = pl.BlockSpec((N_HEADS, QK_PAD, QK_PAD), lambda i: (0, 0, 0))
    tblk = pl.BlockSpec((tm, KV_LORA), lambda i: (i, 0))
    return pl.pallas_call(
        body, name="mla_q_bwd", grid=(s // tm,),
        in_specs=[pl.BlockSpec((tm, Q_LORA), lambda i: (i, 0)), wblk, wblk, tblk, tblk,
                  pl.BlockSpec((N_HEADS, tm, QK_PAD), lambda i: (0, i, 0))],
        out_specs=[pl.BlockSpec((tm, Q_LORA), lambda i: (i, 0)), wblk, wblk],
        out_shape=[_sds((s, Q_LORA), F32), _sds((N_HEADS, QK_PAD, QK_PAD), F32), _sds((N_HEADS, QK_PAD, QK_PAD), F32)],
        compiler_params=_params(),
    )(cq, wq, wukp, cos_k, sin_k, dqcat)


def _flash_fwd(qcat, kcat, tq, tk):
    n_h, s, _ = qcat.shape
    n_k = s // tk

    def body(q_ref, k_ref, o_ref, lse_ref):
        q = q_ref[...]
        m = jnp.full((tq, 1), -1e30, F32)
        acc = jnp.zeros((tq, QK_PAD), F32)
        for kk in range(n_k):
            k = k_ref[pl.ds(kk * tk, tk), :]
            sc = _dot(q, k, NT)
            m_new = jnp.maximum(m, jnp.max(sc, axis=1, keepdims=True))
            p = jnp.exp2(sc - m_new).astype(BF16)
            acc = jnp.exp2(m - m_new) * acc + _dot(p, k, NN)
            m = m_new
        l = jnp.sum(jnp.where(_lane(acc.shape) == ONE_COL, acc, 0.0), axis=1, keepdims=True)
        o_ref[...] = (acc[:, :KV_LORA] / l).astype(o_ref.dtype)
        lse_ref[...] = m + jnp.log2(l)

    return pl.pallas_call(
        body, name="mla_attn", grid=(n_h, s // tq),
        in_specs=[pl.BlockSpec((None, tq, QK_PAD), lambda h, i: (h, i, 0)), pl.BlockSpec((s, QK_PAD), lambda h, i: (0, 0))],
        out_specs=[pl.BlockSpec((None, tq, KV_LORA), lambda h, i: (h, i, 0)), pl.BlockSpec((None, tq, 1), lambda h, i: (h, i, 0))],
        out_shape=[_sds((n_h, s, KV_LORA), BF16), _sds((n_h, s, 1), F32)], compiler_params=_params(),
    )(qcat, kcat)


def _flash_bwd(qcat, kcat, o, do, lse, tq, tk):
    n_h, s, _ = qcat.shape
    n_k = s // tk

    def body(q_ref, k_ref, v_ref, o_ref, do_ref, lse_ref, dq_ref, dk_ref, dv_ref, dq_acc):
        h, i = pl.program_id(0), pl.program_id(1)

        @pl.when(jnp.logical_and(h == 0, i == 0))
        def _():
            dk_ref[...] = jnp.zeros_like(dk_ref)
            dv_ref[...] = jnp.zeros_like(dv_ref)

        q = q_ref[...]
        dov = do_ref[...]
        dov_t = dov.T
        q_t = q.T
        lse_v = lse_ref[...]
        delta = jnp.sum(dov.astype(F32) * o_ref[...].astype(F32), axis=1, keepdims=True)
        dq_acc[...] = jnp.zeros_like(dq_acc)

        for kk in range(n_k):
            rows = pl.ds(kk * tk, tk)
            k = k_ref[rows, :]
            p = jnp.exp2(_dot(q, k, NT) - lse_v)
            dp = _dot(dov, v_ref[rows, :], NT)
            ds = (p * (dp - delta)).astype(BF16)
            dq_acc[...] += _dot(ds, k, NN)
            dv_ref[:, rows] += _dot(dov_t, p.astype(BF16), NN)
            dk_ref[:, rows] += _dot(q_t, ds, NN)
        dq_ref[...] = dq_acc[...].astype(dq_ref.dtype)

    qblk = pl.BlockSpec((None, tq, QK_PAD), lambda h, i: (h, i, 0))
    oblk = pl.BlockSpec((None, tq, KV_LORA), lambda h, i: (h, i, 0))
    return pl.pallas_call(
        body, name="mla_attn_bwd", grid=(n_h, s // tq),
        in_specs=[qblk, pl.BlockSpec((s, QK_PAD), lambda h, i: (0, 0)), pl.BlockSpec((s, KV_LORA), lambda h, i: (0, 0)), oblk, oblk,
                  pl.BlockSpec((None, tq, 1), lambda h, i: (h, i, 0))],
        out_specs=[qblk, pl.BlockSpec((QK_PAD, s), lambda h, i: (0, 0)), pl.BlockSpec((KV_LORA, s), lambda h, i: (0, 0))],
        out_shape=[_sds((n_h, s, QK_PAD), BF16), _sds((QK_PAD, s), F32), _sds((KV_LORA, s), F32)],
        scratch_shapes=[pltpu.VMEM((tq, QK_PAD), F32)], compiler_params=_params(),
    )(qcat, kcat, kcat, o, do, lse)


def _mla_uv(o_lat, wuv2, do, tm):
    n_h, s, _ = o_lat.shape
    d = n_h * V_HEAD
    pair = 2 * V_HEAD
    lat_blk = pl.BlockSpec((n_h, tm, KV_LORA), lambda i: (0, i, 0))
    w_blk = pl.BlockSpec((n_h, KV_LORA, pair), lambda i: (0, 0, 0))
    row_blk = pl.BlockSpec((tm, d), lambda i: (i, 0))

    if do is None:
        def body(a_ref, w_ref, o_ref):
            for p in range(n_h // 2):
                o_ref[:, p * pair:(p + 1) * pair] = (
                    _dot(a_ref[2 * p], w_ref[2 * p], NN) + _dot(a_ref[2 * p + 1], w_ref[2 * p + 1], NN)).astype(o_ref.dtype)

        return pl.pallas_call(body, name="mla_uv", grid=(s // tm,), in_specs=[lat_blk, w_blk], out_specs=row_blk,
                              out_shape=_sds((s, d), BF16), compiler_params=_params())(o_lat, wuv2)

    def body(a_ref, w_ref, do_ref, dlat_ref, dw_ref):
        @pl.when(pl.program_id(0) == 0)
        def _():
            dw_ref[...] = jnp.zeros_like(dw_ref)

        for h in range(n_h):
            dov = do_ref[:, (h // 2) * pair:(h // 2 + 1) * pair]
            dlat_ref[h] = _dot(dov, w_ref[h], NT).astype(dlat_ref.dtype)
            dw_ref[h] += _dot(a_ref[h], dov, TN)

    return pl.pallas_call(body, name="mla_uv_bwd", grid=(s // tm,), in_specs=[lat_blk, w_blk, row_blk], out_specs=[lat_blk, w_blk],
                          out_shape=[_sds((n_h, s, KV_LORA), BF16), _sds((n_h, KV_LORA, pair), F32)], compiler_params=_params(),
                          )(o_lat, wuv2, do)


def _mla_fwd(h, wts, cos_k, sin_k, tm):
    s, d = h.shape
    nt = s // tm
    lat = _matmul("mla_lat", h, wts["w_in"], a_blk=(tm, d), a_map=lambda i, k: (i, 0), b_blk=(d, LAT_PAD), b_map=lambda i, k: (0, 0),
                  o_shape=(s, LAT_PAD), o_blk=(tm, LAT_PAD), o_map=lambda i, k: (i, 0), grid=(nt, 1), contract=NN, out_dtype=F32)
    cq, kcat = _mla_mid(lat, wts["q_norm"], wts["kv_norm"], cos_k, sin_k, tm)
    qcat = _mla_q(cq, wts["wq"], wts["wukp"], cos_k, sin_k, tm)
    o_lat, lse = _flash_fwd(qcat, kcat, min(4 * tm, s), tm)
    o = _mla_uv(o_lat, wts["wuv2"], None, tm)
    u = _matmul("mla_out", o, wts["w_o"], a_blk=(tm, d), a_map=lambda i, k: (i, 0), b_blk=(d, d), b_map=lambda i, k: (0, 0),
                o_shape=(s, d), o_blk=(tm, d), o_map=lambda i, k: (i, 0), grid=(nt, 1), contract=NN, out_dtype=BF16)
    return u, (lat, cq, kcat, qcat, o_lat, lse, o)


def _mla_bwd(du, h, saved, wts, cos_k, sin_k, tm):
    lat, cq, kcat, qcat, o_lat, lse, o = saved
    s, d = h.shape
    nt = s // tm
    do = _matmul("mla_do", du, wts["w_o"], a_blk=(tm, d), a_map=lambda i, k: (i, 0), b_blk=(d, d), b_map=lambda i, k: (0, 0),
                 o_shape=(s, d), o_blk=(tm, d), o_map=lambda i, k: (i, 0), grid=(nt, 1), contract=NT, out_dtype=BF16)
    dw_o = _matmul("mla_dwo", o, du, a_blk=(tm, d), a_map=lambda k: (k, 0), b_blk=(tm, d), b_map=lambda k: (k, 0),
                   o_shape=(d, d), o_blk=(d, d), o_map=lambda k: (0, 0), grid=(nt,), contract=TN, out_dtype=F32)
    do_lat, dwuv2 = _mla_uv(o_lat, wts["wuv2"], do, tm)
    dqcat, dkcat_t, dv_t = _flash_bwd(qcat, kcat, o_lat, do_lat, lse, min(2 * tm, s), tm)
    dkcat, dv = dkcat_t.T, dv_t.T
    dcq, dwq, dwukp = _mla_q_bwd(cq, wts["wq"], wts["wukp"], cos_k, sin_k, dqcat, tm)
    dlat, (dqn, dkn) = _mla_mid_bwd(lat, dcq, dkcat, dv, wts["q_norm"], wts["kv_norm"], cos_k, sin_k, tm)
    dh = _matmul("mla_dh", dlat, wts["w_in"], a_blk=(tm, LAT_PAD), a_map=lambda i, k: (i, 0), b_blk=(d, LAT_PAD),
                 b_map=lambda i, k: (0, 0), o_shape=(s, d), o_blk=(tm, d), o_map=lambda i, k: (i, 0), grid=(nt, 1), contract=NT,
                 out_dtype=BF16)
    dw_in = _matmul("mla_dwin", h, dlat, a_blk=(tm, d), a_map=lambda k: (k, 0), b_blk=(tm, LAT_PAD), b_map=lambda k: (k, 0),
                    o_shape=(d, LAT_PAD), o_blk=(d, LAT_PAD), o_map=lambda k: (0, 0), grid=(nt,), contract=TN, out_dtype=F32)
    return dh, dict(w_in=dw_in, wq=dwq, wukp=dwukp, wuv2=dwuv2, w_o=dw_o, q_norm=dqn, kv_norm=dkn)


def _adamw(name, parts, w, m, v):
    n_parts, r, c = parts.shape
    tr = r
    for cand in (256, 128, 64, 32, 16, 8):
        if r > cand and r % cand == 0:
            tr = cand
            break

    def body(p_ref, w_ref, m_ref, v_ref, g_ref, d_ref, nm_ref, nv_ref):
        g = p_ref[0].astype(F32)
        for k in range(1, n_parts):
            g = g + p_ref[k].astype(F32)
        nm = ADAM_B1 * m_ref[...] + (1.0 - ADAM_B1) * g
        nv = ADAM_B2 * v_ref[...] + (1.0 - ADAM_B2) * (g * g)
        m_hat = nm / (1.0 - ADAM_B1 ** ADAM_STEP)
        v_hat = nv / (1.0 - ADAM_B2 ** ADAM_STEP)
        g_ref[...] = g
        d_ref[...] = -ADAM_LR * (m_hat / (jnp.sqrt(v_hat) + ADAM_EPS) + ADAM_WD * w_ref[...])
        nm_ref[...] = nm
        nv_ref[...] = nv

    blk = pl.BlockSpec((tr, c), lambda i: (i, 0))
    return pl.pallas_call(
        body, name=name, grid=(r // tr,), in_specs=[pl.BlockSpec((n_parts, tr, c), lambda i: (0, i, 0)), blk, blk, blk],
        out_specs=[blk] * 4, out_shape=[_sds((r, c), F32)] * 4, compiler_params=_params(),
    )(parts, w, m, v)


def _adamw_slab(name, parts, w, m, v, bufs, f):
    n_parts, r, c = parts.shape
    tr = max(t for t in range(8, 257, 8) if r % t == 0)

    def body(p_ref, w_ref, m_ref, v_ref, *rest):
        g_ref, d_ref, nm_ref, nv_ref = rest[4:]
        g = p_ref[0].astype(F32)
        for k in range(1, n_parts):
            g = g + p_ref[k].astype(F32)
        nm = ADAM_B1 * m_ref[...] + (1.0 - ADAM_B1) * g
        nv = ADAM_B2 * v_ref[...] + (1.0 - ADAM_B2) * (g * g)
        m_hat = nm / (1.0 - ADAM_B1 ** ADAM_STEP)
        v_hat = nv / (1.0 - ADAM_B2 ** ADAM_STEP)
        g_ref[...] = g
        d_ref[...] = -ADAM_LR * (m_hat / (jnp.sqrt(v_hat) + ADAM_EPS) + ADAM_WD * w_ref[...])
        nm_ref[...] = nm
        nv_ref[...] = nv

    blk = pl.BlockSpec((None, tr, c), lambda i: (f, i, 0))
    return pl.pallas_call(
        body, name=name, grid=(r // tr,),
        in_specs=[pl.BlockSpec((n_parts, tr, c), lambda i: (0, i, 0)), blk, blk, blk] + [pl.BlockSpec(memory_space=pl.ANY)] * 4,
        out_specs=[blk] * 4, out_shape=[_sds(w.shape, F32)] * 4, input_output_aliases={4 + j: j for j in range(4)},
        compiler_params=_params(),
    )(parts, w, m, v, *bufs)


def _mesh_pos():
    return lax.axis_index("x"), lax.axis_index("y"), lax.axis_index("c")


def _flip(pos, mask):
    return tuple(1 - p if (mask >> (2 - b)) & 1 else p for b, p in enumerate(pos))


def _index(pos):
    return 4 * pos[0] + 2 * pos[1] + pos[2]


def _all_gather(name, xs, after=None):
    n = len(xs)
    extra = [] if after is None else [after]

    def body(*refs):
        x_refs, o_refs = refs[:n], refs[n + len(extra):2 * n + len(extra)]
        send_sems, recv_sems, local_sems = refs[2 * n + len(extra):]
        me = _mesh_pos()
        sibling = _flip(me, 1)
        others = [_flip(me, 4), _flip(me, 2), _flip(me, 6)]

        def copy(k, j, block, to, src=None):
            dst = o_refs[k].at[_index(block)]
            return pltpu.make_async_remote_copy(
                src_ref=dst if src is None else src, dst_ref=dst, send_sem=send_sems.at[k, j], recv_sem=recv_sems.at[k, j],
                device_id=to, device_id_type=MESH)

        local = [pltpu.make_async_copy(x_refs[k], o_refs[k].at[_index(me)], local_sems.at[k]) for k in range(n)]
        for cp in local:
            cp.start()
        first = []
        for k in range(n):
            first.append(copy(k, 0, me, sibling, src=x_refs[k]))
            first += [copy(k, 1 + j, me, other, src=x_refs[k]) for j, other in enumerate(others)]
        for cp in first:
            cp.start()
        passed = []
        for j, other in enumerate(others):
            for k in range(n):
                copy(k, 1 + j, other, me).wait_recv()
                cp = copy(k, 4 + j, other, sibling)
                cp.start()
                passed.append(cp)
        for k in range(n):
            copy(k, 0, sibling, me).wait_recv()
        for j, other in enumerate(others):
            for k in range(n):
                copy(k, 4 + j, _flip(other, 1), me).wait_recv()
        for cp in first + passed:
            cp.wait_send()
        for cp in local:
            cp.wait()

    any_spec = pl.BlockSpec(memory_space=pl.ANY)
    return pl.pallas_call(
        body, name=name, in_specs=[any_spec] * (n + len(extra)), out_specs=[any_spec] * n,
        out_shape=[_sds((N_DEV,) + x.shape, x.dtype) for x in xs],
        scratch_shapes=[pltpu.SemaphoreType.DMA((n, 7)), pltpu.SemaphoreType.DMA((n, 7)), pltpu.SemaphoreType.DMA((n,))],
    )(*xs, *extra)


_MASKS = {
    "gather": tuple(range(1, N_DEV)),
    "scatter": tuple(range(1, N_DEV)),
    "own": (1, 4, 2, 6),
    "pass": (4, 2, 6),
}


def _split_copies(kind, outgoing, x_refs, land_refs, send_sems, recv_sems):
    me = _mesh_pos()
    masks = _MASKS[kind]
    copies = []
    for k, (x_ref, land_ref) in enumerate(zip(x_refs, land_refs)):
        for j, mask in enumerate(masks):
            if kind == "pass":
                peer = _flip(me, 1)
                src = land_ref.at[_index(_flip(me, mask))]
                dst = land_ref.at[_index(_flip(me, mask if outgoing else mask | 1))]
            else:
                peer = _flip(me, mask)
                src = x_ref.at[_index(peer)] if kind == "scatter" else x_ref
                dst = land_ref.at[_index(me if outgoing else peer)]
            sem = k * len(masks) + j
            copies.append(pltpu.make_async_remote_copy(src_ref=src, dst_ref=dst, send_sem=send_sems.at[sem], recv_sem=recv_sems.at[sem],
                                                       device_id=peer, device_id_type=MESH))
    return copies


_HBM_SPEC = pl.BlockSpec(memory_space=pltpu.HBM)
_SEM_SPEC = pl.BlockSpec(memory_space=pltpu.SEMAPHORE)
_EFFECT = pltpu.SideEffectType.DATAFLOW_SIDE_EFFECTING


def _split_start(name, kind, xs, after=None):
    n = len(xs)
    n_sem = n * len(_MASKS[kind])
    extra = [] if after is None else [after]
    lands = [lax.empty(x.shape if kind == "scatter" else (N_DEV,) + x.shape, x.dtype) for x in xs]

    def body(*refs):
        x_refs, land_refs = refs[:n], refs[n:2 * n]
        send_sems, recv_sems = refs[2 * n + len(extra)], refs[2 * n + len(extra) + 1]
        token = refs[-1]
        for cp in _split_copies(kind, True, x_refs, land_refs, send_sems, recv_sems):
            cp.start()
        token[...] = jnp.zeros_like(token)

    hbm = [pltpu.HBM(a.shape, a.dtype) for a in list(xs) + lands]
    res = pl.pallas_call(
        body, name=name,
        out_shape=[pltpu.SemaphoreType.DMA((n_sem,)), pltpu.SemaphoreType.DMA((n_sem,))] + hbm + [_sds((8, 128), F32)],
        in_specs=[_HBM_SPEC] * (2 * n) + [pl.BlockSpec(memory_space=pl.ANY)] * len(extra),
        out_specs=[_SEM_SPEC, _SEM_SPEC] + [_HBM_SPEC] * (2 * n) + [pl.BlockSpec(memory_space=pltpu.VMEM)],
        input_output_aliases={j: 2 + j for j in range(2 * n)}, compiler_params=pltpu.CompilerParams(has_side_effects=_EFFECT),
    )(*[pltpu.with_memory_space_constraint(a, pltpu.HBM) for a in list(xs) + lands], *extra)
    return (kind, n, res[0], res[1], res[2:2 + 2 * n]), res[-1]


def _split_pass(name, state, after):
    kind, n, send_sems_in, recv_sems_in, thru = state
    n_sem = n * len(_MASKS["pass"])

    def body(*refs):
        x_refs, land_refs = refs[:n], refs[n:2 * n]
        send_sems, recv_sems = refs[2 * n], refs[2 * n + 1]
        next_send, next_recv, token = refs[-3:]
        for cp in _split_copies(kind, True, x_refs, land_refs, send_sems, recv_sems):
            cp.wait_send()
        for cp in _split_copies(kind, False, x_refs, land_refs, send_sems, recv_sems):
            cp.wait_recv()
        for cp in _split_copies("pass", True, land_refs, land_refs, next_send, next_recv):
            cp.start()
        token[...] = jnp.zeros_like(token)

    res = pl.pallas_call(
        body, name=name,
        out_shape=[pltpu.HBM(a.shape, a.dtype) for a in thru] + [pltpu.SemaphoreType.DMA((n_sem,)), pltpu.SemaphoreType.DMA((n_sem,)),
                                                                 _sds((8, 128), F32)],
        in_specs=[_HBM_SPEC] * (2 * n) + [_SEM_SPEC, _SEM_SPEC, pl.BlockSpec(memory_space=pl.ANY)],
        out_specs=[_HBM_SPEC] * (2 * n) + [_SEM_SPEC, _SEM_SPEC, pl.BlockSpec(memory_space=pltpu.VMEM)],
        input_output_aliases={j: j for j in range(2 * n)}, compiler_params=pltpu.CompilerParams(has_side_effects=_EFFECT),
    )(*thru, send_sems_in, recv_sems_in, after)
    return ("pass", n, res[2 * n], res[2 * n + 1], res[:2 * n]), res[-1]


def _split_wait(name, state, after):
    kind, n, send_sems_in, recv_sems_in, thru = state

    def body(*refs):
        x_refs, land_refs = refs[:n], refs[n:2 * n]
        send_sems, recv_sems = refs[2 * n], refs[2 * n + 1]
        for cp in _split_copies(kind, True, x_refs, land_refs, send_sems, recv_sems):
            cp.wait_send()
        for cp in _split_copies(kind, False, x_refs, land_refs, send_sems, recv_sems):
            cp.wait_recv()

    res = pl.pallas_call(
        body, name=name, out_shape=[pltpu.HBM(a.shape, a.dtype) for a in thru],
        in_specs=[_HBM_SPEC] * (2 * n) + [_SEM_SPEC, _SEM_SPEC, pl.BlockSpec(memory_space=pl.ANY)], out_specs=[_HBM_SPEC] * (2 * n),
        input_output_aliases={j: j for j in range(2 * n)}, compiler_params=pltpu.CompilerParams(has_side_effects=_EFFECT),
    )(*thru, send_sems_in, recv_sems_in, after)
    me = _index(_mesh_pos())
    out = []
    for x, land in zip(res[:n], res[n:]):
        own = lax.dynamic_index_in_dim(x, me, 0, keepdims=False) if kind == "scatter" else x
        out.append(lax.dynamic_update_slice(land, own[None], (me,) + (0,) * own.ndim))
    return out


def _rope_tables(s):
    inv = 1.0 / (ROPE_THETA ** (jnp.arange(0, QK_ROPE, 2, dtype=F32) / QK_ROPE))
    ang = jnp.arange(s, dtype=F32)[:, None] * inv[None, :]
    pad = jnp.zeros((s, KV_LORA - QK_ROPE), F32)
    return (jnp.concatenate([jnp.cos(ang), jnp.cos(ang), pad], axis=1), jnp.concatenate([jnp.sin(ang), jnp.sin(ang), pad], axis=1))


def kernel(x, c, ada_w, ada_b, norm_g, ffn_w_in, ffn_w_out, pool_w, pool_b, pool_scale, mla_w_in, mla_q_norm, mla_kv_norm, mla_w_uq, mla_w_uk, mla_w_uv, mla_w_o, loss_target, m_ada_w, m_ada_b, m_norm_g, m_ffn_w_in, m_ffn_w_out, m_pool_w, m_pool_b, m_pool_scale, m_mla_w_in, m_mla_q_norm, m_mla_kv_norm, m_mla_w_uq, m_mla_w_uk, m_mla_w_uv, m_mla_w_o, v_ada_w, v_ada_b, v_norm_g, v_ffn_w_in, v_ffn_w_out, v_pool_w, v_pool_b, v_pool_scale, v_mla_w_in, v_mla_q_norm, v_mla_kv_norm, v_mla_w_uq, v_mla_w_uk, v_mla_w_uv, v_mla_w_o):
    s, d = x.shape[1], x.shape[2]
    tm = min(512, s)
    tr = min(512, s)
    tf = min(1024, s)
    tw = min(2048, s)
    me = 4 * lax.axis_index("x") + 2 * lax.axis_index("y") + lax.axis_index("c")
    x0 = x.reshape(s, d)
    target = loss_target.reshape(s, d)
    n_mod = ada_w.shape[2] * N_DEV // d
    mod_blk = ada_w.shape[2]

    small = jnp.concatenate([c.reshape(-1), norm_g.reshape(-1), pool_b.reshape(-1), mla_q_norm.reshape(-1)]).reshape(1, -1)
    w_in_t, m_in_t, v_in_t = (jnp.swapaxes(a, 2, 3) for a in (ffn_w_in, m_ffn_w_in, v_ffn_w_in))
    w_in_loc = [w_in_t[i, f].astype(BF16) for i in range(2) for f in range(2)]
    w_out_loc = [ffn_w_out[i, f].astype(BF16) for i in range(2) for f in range(2)]
    (small_all,) = _all_gather("gather_small", [small])
    small_all = small_all.reshape(N_DEV, -1)
    c_all = small_all[:, :d]
    off = d
    g_all = small_all[:, off:off + 12 * (d // N_DEV)].reshape(N_DEV, 2, 6, d // N_DEV).transpose(1, 2, 0, 3).reshape(2, 6, d)
    off += 12 * (d // N_DEV)
    pool_b_all = small_all[:, off:off + 4 * POOL_SHARD].reshape(N_DEV, 4, POOL_SHARD).transpose(1, 0, 2).reshape(1, d)
    off += 4 * POOL_SHARD
    q_norm_all = small_all[:, off:off + Q_SHARD].reshape(1, Q_LORA)
    kv_norm_row = mla_kv_norm.reshape(1, KV_LORA)
    pscale_row = pool_scale.reshape(1, d)

    even = (jnp.arange(N_HEADS) % 2 == 0)[:, None, None]
    cos_k, sin_k = _rope_tables(s)

    def mla_weights(mla_w_in_all, mla_w_uq_all, mla_w_o_all):
        uq = mla_w_uq_all.reshape(Q_LORA, N_HEADS, QK_NOPE + QK_ROPE).transpose(1, 0, 2)
        zq = jnp.zeros((N_HEADS, Q_LORA, QK_NOPE), BF16)
        wq = jnp.concatenate(
            [uq[:, :, :QK_NOPE], zq, uq[:, :, QK_NOPE:], jnp.zeros((N_HEADS, Q_LORA, QK_PAD - KV_LORA - QK_ROPE), BF16)], axis=2)
        wukp = jnp.pad(mla_w_uk[0].transpose(1, 2, 0).astype(BF16), ((0, 0), (0, QK_PAD - QK_NOPE), (0, QK_PAD - KV_LORA)))
        uv = mla_w_uv[0].transpose(1, 0, 2).astype(BF16)
        wuv2 = jnp.where(even, jnp.concatenate([uv, jnp.zeros_like(uv)], axis=2), jnp.concatenate([jnp.zeros_like(uv), uv], axis=2))
        return dict(w_in=jnp.pad(mla_w_in_all.reshape(d, -1), ((0, 0), (0, LAT_PAD - mla_w_in.shape[2]))), wq=wq, wukp=wukp,
                    wuv2=wuv2, w_o=mla_w_o_all.reshape(d, d), q_norm=q_norm_all, kv_norm=kv_norm_row)

    (sc_all,), _ = _rowmap("ada_silu", lambda cv: ((cv * jax.nn.sigmoid(cv),), ()), [(c_all, (N_DEV, d), lambda i: (0, 0))],
                           [(_sds((N_DEV, d), F32), (N_DEV, d), lambda i: (0, 0))], [], (1,))
    ada_b_loc = lax.dynamic_slice_in_dim(ada_b, me * mod_blk, mod_blk, axis=1).reshape(2, 1, mod_blk)
    m_pad = 2 * N_DEV
    modp = _matmul("ada_mod", jnp.pad(sc_all, ((0, m_pad - N_DEV), (0, 0))), ada_w, a_blk=(m_pad, d), a_map=lambda i, k: (0, 0),
                   b_blk=(None, d, mod_blk), b_map=lambda i, k: (i, 0, 0), o_shape=(2, m_pad, mod_blk), o_blk=(None, m_pad, mod_blk),
                   o_map=lambda i, k: (i, 0, 0), grid=(2, 1), contract=NN, out_dtype=F32, bias=ada_b_loc, bias_blk=(None, 1, mod_blk),
                   bias_map=lambda i, k: (i, 0, 0))[:, :N_DEV]
    (modp_all,) = _all_gather("gather_mod", [modp.reshape(2 * N_DEV, mod_blk)])
    groups = [[w_in_loc[0], w_out_loc[0], pool_w.reshape(-1, POOL_GROUP).astype(BF16)], [w_in_loc[1], w_out_loc[1]],
              [w_in_loc[2], w_out_loc[2], mla_w_in[0].astype(BF16), mla_w_uq.reshape(mla_w_uq.shape[1], -1).astype(BF16),
               mla_w_o[0].astype(BF16)], [w_in_loc[3], w_out_loc[3]]]
    state, token = _split_start("gather_start_0", "own", groups[0], after=modp_all)
    states = [state]
    w_in8 = [None] * 4
    w_out4 = [None] * 4
    mod = lax.dynamic_index_in_dim(modp_all.reshape(N_DEV, 2, N_DEV, mod_blk), me, axis=2, keepdims=False)
    mod = mod.transpose(1, 0, 2).reshape(2, n_mod, d) + token[0, 0]
    mod_rows = mod.reshape(2 * n_mod, 1, d)
    g_rows = g_all.reshape(12, 1, d)
    weights_of = (0.5, 1.0, 0.5, 0.5, 1.0, 0.5)

    def pre_rows(k, with_shift):
        rows = [_rowk(g_rows, 2 * k), _rowk(mod_rows, 3 * k + 1)]
        return rows + [_rowk(mod_rows, 3 * k)] if with_shift else rows

    def post_rows(k):
        return [_rowk(g_rows, 2 * k + 1), _rowk(mod_rows, 3 * k + 2)]

    def act_dtype(k):
        return F32 if k == 1 else BF16

    saved = []
    xs = x0
    mla_wts = None
    h = _prenorm("prenorm_0", xs, pre_rows(0, True), act_dtype(0), tr)
    token = h
    for name, group in zip("abc", groups[1:]):
        state, token = _split_start(f"gather_start_{name}", "own", group, after=token)
        states.append(state)
    state, token = _split_pass("gather_pass_0", states[0], token)
    lands = _split_wait("gather_wait_0", state, token)
    w_in8[0], w_out4[0] = lands[0], lands[1].reshape(4, FF_BLK, d)
    w4 = lands[2].reshape(N_DEV, 4, POOL_SHARD, POOL_GROUP).transpose(1, 0, 2, 3).reshape(4, POOL_GROUP, POOL_GROUP)
    token = None
    for k in range(6):
        i, sub = divmod(k, 3)
        tag = f"{i}{sub}"
        if k == 1:
            states[1], token = _split_pass("gather_pass_a", states[1], xs)
        if k == 2:
            lands = _split_wait("gather_wait_a", states[1], xs)
            w_in8[1], w_out4[1] = lands[0], lands[1].reshape(4, FF_BLK, d)
            states[2], token = _split_pass("gather_pass_b", states[2], lands[0])
        if k == 3:
            lands = _split_wait("gather_wait_b", states[2], xs)
            w_in8[2], w_out4[2] = lands[0], lands[1].reshape(4, FF_BLK, d)
            mla_wts = mla_weights(*lands[2:])
            states[3], token = _split_pass("gather_pass_c", states[3], lands[0])
        if k == 5:
            lands = _split_wait("gather_wait_c", states[3], xs)
            w_in8[3], w_out4[3] = lands[0], lands[1].reshape(4, FF_BLK, d)
        if sub != 1:
            u, extra = _ffn_fwd(tag, h, w_in8[2 * i + sub // 2], w_out4[2 * i + sub // 2], tf, tf // 2, after=token)
            token = None
        elif i == 0:
            u, extra = _pool_fwd(h, w4, pool_b_all + token[0, 0], min(4 * tm, s))
            token = None
        else:
            u, extra = _mla_fwd(h, mla_wts, cos_k, sin_k, tm)
        saved.append((xs, h, u, extra))
        if k < 5:
            xs, h = _norm_link(f"norm_link_{k + 1}", xs, u, post_rows(k), weights_of[k], pre_rows(k + 1, True), act_dtype(k + 1), tr,
                               u_scale=pscale_row if k == 1 else None)

    dx, du, (sq, dgate, dgpost) = _norm_loss("norm_loss", xs, u, target, post_rows(5), weights_of[5], tr)
    loss_part = (0.5 * jnp.sum(sq) / d).reshape(1, 1)

    d_mod = [None] * (2 * n_mod)
    d_g = [None] * 12
    d_mod[3 * 5 + 2], d_g[2 * 5 + 1] = dgate, dgpost
    sent = {}
    pool_grads = mla_grads = None

    def start_scatter(key, arrays):
        state, token = _split_start(f"scatter_start_{key}", "scatter", arrays)
        sent[key] = state
        return token

    for k in (5, 4, 3, 2, 1, 0):
        i, sub = divmod(k, 3)
        tag = f"{i}{sub}"
        xin, h, u, extra = saved[k]
        token = None
        if sub != 1:
            f = 2 * i + sub // 2
            dh, dgu, act = _ffn_bwd_act(tag, du, extra, w_in8[f], w_out4[f], tf, tf // 4)
            dw_out = _ffn_dw(f"ffn_dwout_{tag}", act, du, tw).reshape(N_DEV, FF_BLK // 2, d)
            if k == 0:
                d_pool_w = pool_grads[0].reshape(4, N_DEV, POOL_SHARD, POOL_GROUP).transpose(1, 0, 2, 3).reshape(N_DEV, -1, POOL_GROUP)
                token = start_scatter(tag + "_out", [dw_out, d_pool_w])
                token = start_scatter(tag + "_in", [_ffn_dw(f"ffn_dwin_{tag}", dgu, h, tw, after=token)])
            else:
                token = start_scatter(tag, [_ffn_dw(f"ffn_dwin_{tag}", dgu, h, tw), dw_out])
        elif i == 0:
            dh, dw4 = _pool_bwd(du, extra, w4, min(4 * tm, s))
            pool_grads = (dw4, *pool_reds)
        else:
            dh, mla_grads = _mla_bwd(du, h, extra, mla_wts, cos_k, sin_k, tm)
            dwq = mla_grads["wq"]
            d_uq = jnp.concatenate([dwq[:, :, :QK_NOPE], dwq[:, :, KV_LORA:KV_LORA + QK_ROPE]], axis=2).transpose(1, 0, 2)
            token = start_scatter("mla", [mla_grads["w_in"][:, :mla_w_in.shape[2]].reshape(N_DEV, d // N_DEV, -1),
                                          d_uq.reshape(N_DEV, Q_LORA // N_DEV, -1), mla_grads["w_o"].reshape(N_DEV, d // N_DEV, d)])
            dwukp, dwuv2 = mla_grads["wukp"], mla_grads["wuv2"]
            d_uk = dwukp[:, :QK_NOPE, :KV_LORA].transpose(2, 0, 1).reshape(KV_LORA, -1)
            d_uv = jnp.where(even, dwuv2[:, :, :V_HEAD], dwuv2[:, :, V_HEAD:]).transpose(1, 0, 2).reshape(KV_LORA, -1)
            state_ukv, token = _split_start("gather_start_ukv", "gather", [d_uk, d_uv], after=token)
        if k > 0:
            dx, du, reds = _norm_link_bwd(f"norm_link_bwd_{k}", dh, xin, dx, saved[k - 1][2], pre_rows(k, False), post_rows(k - 1),
                                          weights_of[k - 1], BF16, tr, after=token, u_scale=pscale_row if k == 2 else None)
            d_mod[3 * (k - 1) + 2], d_g[2 * (k - 1) + 1] = reds[3], reds[4]
            if k == 2:
                pool_reds = reds[5:]
        else:
            dx, reds = _prenorm_bwd("prenorm_bwd_0", dh, xin, dx, pre_rows(0, False), tr, after=token)
        d_mod[3 * k], d_mod[3 * k + 1], d_g[2 * k] = reds[0], reds[1], reds[2]
    grad_x = dx.reshape(x.shape)

    def upd(name, parts, w, m, v):
        shape = w.shape
        r, cdim = parts.shape[1], parts.shape[2]
        return [o.reshape(shape) for o in _adamw(name, parts, w.reshape(r, cdim), m.reshape(r, cdim), v.reshape(r, cdim))]

    def landed(key, after):
        return _split_wait(f"scatter_wait_{key}", sent[key], after)

    res = {}
    w_in_s, m_in_s, v_in_s = (a.reshape(4, FF_BLK, d) for a in (w_in_t, m_in_t, v_in_t))
    w_out_s, m_out_s, v_out_s = (a.reshape(4, FF_BLK // 2, d) for a in (ffn_w_out, m_ffn_w_out, v_ffn_w_out))
    bufs_in = [lax.empty(w_in_s.shape, F32) for _ in range(4)]
    bufs_out = [lax.empty(w_out_s.shape, F32) for _ in range(4)]
    for key, k in (("12", 3), ("mla", None), ("10", 2), ("02", 1)):
        parts = landed(key, grad_x)
        if k is None:
            res["mla_w_in"] = upd("adam_mla_w_in", parts[0], mla_w_in, m_mla_w_in, v_mla_w_in)
            res["mla_w_uq"] = upd("adam_mla_w_uq", parts[1], mla_w_uq, m_mla_w_uq, v_mla_w_uq)
            res["mla_w_o"] = upd("adam_mla_w_o", parts[2], mla_w_o, m_mla_w_o, v_mla_w_o)
            uk_all, uv_all = _split_wait("gather_wait_ukv", state_ukv, grad_x)
            res["mla_w_uk"] = upd("adam_mla_w_uk", uk_all, mla_w_uk, m_mla_w_uk, v_mla_w_uk)
            res["mla_w_uv"] = upd("adam_mla_w_uv", uv_all, mla_w_uv, m_mla_w_uv, v_mla_w_uv)
            continue
        bufs_in = _adamw_slab(f"adam_ffn_w_in_{key}", parts[0], w_in_s, m_in_s, v_in_s, bufs_in, k)
        bufs_out = _adamw_slab(f"adam_ffn_w_out_{key}", parts[1], w_out_s, m_out_s, v_out_s, bufs_out, k)

    dw4, dpscale, dpb = pool_grads
    small_g = jnp.concatenate(d_mod + d_g + [dpb, dpscale, mla_grads["q_norm"], mla_grads["kv_norm"], loss_part], axis=1)
    done = [bufs_in[0], bufs_out[0], res["mla_w_o"][0], res["mla_w_uv"][0]]
    (small_g_all,) = _all_gather("gather_small_grads", [small_g], after=sum(a.reshape(-1)[:1] for a in done))
    small_g_all = small_g_all.reshape(N_DEV, -1)
    n_m = 2 * n_mod * d
    d_mod_all = small_g_all[:, :n_m].reshape(N_DEV, 2, n_mod * d)
    rest = small_g_all[:, n_m:]
    p_norm_g = lax.dynamic_slice_in_dim(rest[:, :12 * d].reshape(N_DEV, 12, d), me * (d // N_DEV), d // N_DEV, axis=2)
    p_pool_b = lax.dynamic_slice_in_dim(rest[:, 12 * d:13 * d].reshape(N_DEV, 4, POOL_GROUP), me * POOL_SHARD, POOL_SHARD, axis=2)
    p_pool_scale = rest[:, 13 * d:14 * d].reshape(N_DEV, 1, d)
    p_q_norm = lax.dynamic_slice_in_dim(rest[:, 14 * d:14 * d + Q_LORA], me * Q_SHARD, Q_SHARD, axis=1).reshape(N_DEV, 1, Q_SHARD)
    p_kv_norm = rest[:, 14 * d + Q_LORA:14 * d + Q_LORA + KV_LORA].reshape(N_DEV, 1, KV_LORA)
    loss = jnp.sum(rest[:, -1])

    d_mod_loc = lax.dynamic_slice_in_dim(d_mod_all, me * mod_blk, mod_blk, axis=2).transpose(1, 0, 2)
    k_pad = 128
    sc_t = jnp.pad(sc_all.T, ((0, 0), (0, k_pad - N_DEV)))
    d_ada_w = _matmul("ada_dw", sc_t, jnp.pad(d_mod_loc, ((0, 0), (0, k_pad - N_DEV), (0, 0))), a_blk=(d, k_pad),
                      a_map=lambda i, k: (0, 0), b_blk=(None, k_pad, mod_blk), b_map=lambda i, k: (i, 0, 0), o_shape=(2, d, mod_blk),
                      o_blk=(None, d, mod_blk), o_map=lambda i, k: (i, 0, 0), grid=(2, 1), contract=NN, out_dtype=F32)
    res["ada_w"] = upd("adam_ada_w", d_ada_w.reshape(1, 2 * d, mod_blk), ada_w, m_ada_w, v_ada_w)
    res["ada_b"] = upd("adam_ada_b", d_mod_all.reshape(N_DEV, 2, n_mod * d), ada_b, m_ada_b, v_ada_b)
    res["norm_g"] = upd("adam_norm_g", p_norm_g, norm_g, m_norm_g, v_norm_g)
    res["pool_b"] = upd("adam_pool_b", p_pool_b, pool_b, m_pool_b, v_pool_b)
    res["pool_scale"] = upd("adam_pool_scale", p_pool_scale, pool_scale, m_pool_scale, v_pool_scale)
    res["mla_q_norm"] = upd("adam_mla_q_norm", p_q_norm, mla_q_norm, m_mla_q_norm, v_mla_q_norm)
    res["mla_kv_norm"] = upd("adam_mla_kv_norm", p_kv_norm, mla_kv_norm, m_mla_kv_norm, v_mla_kv_norm)

    p_out, p_pool_w = landed("00_out", res["ada_w"][1])
    bufs_out = _adamw_slab("adam_ffn_w_out_00", p_out, w_out_s, m_out_s, v_out_s, bufs_out, 0)
    res["pool_w"] = upd("adam_pool_w", p_pool_w, pool_w, m_pool_w, v_pool_w)
    (p_in,) = landed("00_in", res["pool_w"][1])
    bufs_in = _adamw_slab("adam_ffn_w_in_00", p_in, w_in_s, m_in_s, v_in_s, bufs_in, 0)
    res["ffn_w_in"] = [jnp.swapaxes(b.reshape(w_in_t.shape), 2, 3) for b in bufs_in]
    res["ffn_w_out"] = [b.reshape(ffn_w_out.shape) for b in bufs_out]

    order = ["ada_w", "ada_b", "norm_g", "ffn_w_in", "ffn_w_out", "pool_w", "pool_b", "pool_scale", "mla_w_in", "mla_q_norm",
             "mla_kv_norm", "mla_w_uq", "mla_w_uk", "mla_w_uv", "mla_w_o"]
    outs = [loss, grad_x]
    for j in range(4):
        outs += [res[name][j] for name in order]
    return tuple(outs)
```

```python
import jax
import jax.numpy as jnp
from jax import lax
from jax.experimental import pallas as pl
from jax.experimental.pallas import tpu as pltpu

F32 = jnp.float32
BF16 = jnp.bfloat16
N_DEV = 8
MESH = pl.DeviceIdType.MESH

D_MODEL = 1024
N_HEADS = 16
QK_NOPE = 64
QK_ROPE = 32
V_HEAD = 64
Q_LORA = 256
KV_LORA = 128
LAT_PAD = 512
QK_PAD = 256
ONE_COL = 160
D_FF = 2816
FF_BLK = 2 * D_FF // N_DEV
POOL_WINDOWS = (2, 4, 8, 16)
POOL_GROUP = 256
POOL_SHARD = POOL_GROUP // N_DEV
Q_SHARD = Q_LORA // N_DEV
ROPE_THETA = 10000.0
EPS = 1e-6
ATTN_SCALE = (QK_NOPE + QK_ROPE) ** -0.5
LOG2_E = 1.4426950408889634
ADAM_LR, ADAM_B1, ADAM_B2, ADAM_EPS, ADAM_WD, ADAM_STEP = 0.001, 0.9, 0.999, 1e-08, 0.01, 10
VMEM_LIMIT = 56 * 1024 * 1024

NN = ((1,), (0,))
NT = ((1,), (1,))
TN = ((0,), (0,))


def _params(**kw):
    return pltpu.CompilerParams(vmem_limit_bytes=VMEM_LIMIT, **kw)


def _dot(a, b, contract):
    return lax.dot_general(a, b, (contract, ((), ())), preferred_element_type=F32)


def _matmul(name, a, b, *, a_blk, a_map, b_blk, b_map, o_shape, o_blk, o_map, grid, contract, out_dtype,
            bias=None, bias_blk=None, bias_map=None, after=None):
    n_k = grid[-1]
    k_axis = len(grid) - 1
    acc_shape = tuple(d for d in o_blk if d is not None)

    def body(*refs):
        a_ref, b_ref = refs[:2]
        bias_ref = refs[2] if bias is not None else None
        if n_k == 1:
            r = _dot(a_ref[...].astype(BF16), b_ref[...].astype(BF16), contract)
            if bias is not None:
                r = r + bias_ref[...]
            refs[-1][...] = r.astype(refs[-1].dtype)
            return
        o_ref, acc = refs[-2:]
        k = pl.program_id(k_axis)

        @pl.when(k == 0)
        def _():
            acc[...] = jnp.zeros_like(acc)

        acc[...] += _dot(a_ref[...].astype(BF16), b_ref[...].astype(BF16), contract)

        @pl.when(k == n_k - 1)
        def _():
            r = acc[...]
            if bias is not None:
                r = r + bias_ref[...]
            o_ref[...] = r.astype(o_ref.dtype)

    in_specs = [pl.BlockSpec(a_blk, a_map), pl.BlockSpec(b_blk, b_map)]
    args = [a, b]
    if bias is not None:
        in_specs.append(pl.BlockSpec(bias_blk, bias_map))
        args.append(bias)
    if after is not None:
        in_specs.append(pl.BlockSpec(memory_space=pl.ANY))
        args.append(after)
    return pl.pallas_call(
        body, name=name, grid=grid, in_specs=in_specs, out_specs=pl.BlockSpec(o_blk, o_map),
        out_shape=jax.ShapeDtypeStruct(o_shape, out_dtype), scratch_shapes=[pltpu.VMEM(acc_shape, F32)] if n_k > 1 else [],
        compiler_params=_params(),
    )(*args)


def _rowmap(name, fn, ins, outs, reds, grid, after=None):
    n_in, n_out, n_red = len(ins), len(outs), len(reds)
    extra = [] if after is None else [after]

    def body(*refs):
        in_refs = refs[:n_in]
        out_refs = refs[n_in + len(extra):n_in + len(extra) + n_out]
        red_refs = refs[n_in + len(extra) + n_out:]
        out_vals, red_vals = fn(*[r[...] for r in in_refs])
        for r, v in zip(out_refs, out_vals):
            r[...] = v.astype(r.dtype)
        if n_red:
            first = pl.program_id(0) == 0
            for ax in range(1, len(grid)):
                first = jnp.logical_and(first, pl.program_id(ax) == 0)

            @pl.when(first)
            def _():
                for r in red_refs:
                    r[...] = jnp.zeros_like(r)

            for r, v in zip(red_refs, red_vals):
                r[...] += v

    res = pl.pallas_call(
        body, name=name, grid=grid,
        in_specs=[pl.BlockSpec(blk, imap) for _, blk, imap in ins] + [pl.BlockSpec(memory_space=pl.ANY)] * len(extra),
        out_specs=[pl.BlockSpec(blk, imap) for _, blk, imap in list(outs) + list(reds)],
        out_shape=[sds for sds, _, _ in list(outs) + list(reds)],
        compiler_params=_params(),
    )(*[a for a, _, _ in ins], *extra)
    return res[:n_out], res[n_out:]


def _sds(shape, dtype):
    return jax.ShapeDtypeStruct(shape, dtype)


def _tile(a, tm):
    return (a, (tm, a.shape[1]), lambda i: (i, 0))


def _row(a):
    return (a, (1, a.shape[1]), lambda i: (0, 0))


def _otile(n, c, dtype, tm):
    return (_sds((n, c), dtype), (tm, c), lambda i: (i, 0))


def _ored(c):
    return (_sds((1, c), F32), (1, c), lambda i: (0, 0))


def _colsum(v):
    return jnp.sum(v, axis=0, keepdims=True)


def _rstd(v):
    return lax.rsqrt(jnp.mean(v * v, axis=-1, keepdims=True) + EPS)


def _pre(xv, g, sc, sh):
    return xv * _rstd(xv) * g * (1.0 + sc) + sh


def _post(xv, uv, g, gt, weight):
    uv = uv.astype(F32)
    return xv + weight * (1.0 + gt) * (uv * _rstd(uv) * g)


def _post_bwd(dv, uv, g, gt, weight):
    uv = uv.astype(F32)
    r = _rstd(uv)
    un = uv * r
    dy = dv * (weight * (1.0 + gt))
    a = dy * g
    du = r * (a - un * jnp.mean(a * un, axis=-1, keepdims=True))
    return du, (_colsum(dv * (weight * (un * g))), _colsum(dy * un))


def _pre_bwd(dhv, xv, dv, g, sc):
    dhv = dhv.astype(F32)
    r = _rstd(xv)
    xn = xv * r
    b = dhv * (g * (1.0 + sc))
    dx = dv + r * (b - xn * jnp.mean(b * xn, axis=-1, keepdims=True))
    return dx, (_colsum(dhv), _colsum(dhv * (xn * g)), _colsum(dhv * ((1.0 + sc) * xn)))


def _rowk(rows, k):
    return (rows, (None, 1, rows.shape[2]), lambda i: (k, 0, 0))


def _prenorm(name, x, pre, out_dtype, tm):
    n, d = x.shape
    (h,), _ = _rowmap(name, lambda xv, g, sc, sh: ((_pre(xv, g, sc, sh),), ()), [_tile(x, tm), *pre], [_otile(n, d, out_dtype, tm)],
                      [], (n // tm,))
    return h


def _norm_link(name, x, u, post, weight, pre, out_dtype, tm, u_scale=None):
    n, d = x.shape
    rows = [] if u_scale is None else [_row(u_scale)]

    def fn(xv, uv, g, gt, g2, sc, sh, *ps):
        xn = _post(xv, uv * ps[0] if ps else uv, g, gt, weight)
        return (xn, _pre(xn, g2, sc, sh)), ()

    (xn, h), _ = _rowmap(name, fn, [_tile(x, tm), _tile(u, tm), *post, *pre, *rows],
                         [_otile(n, d, F32, tm), _otile(n, d, out_dtype, tm)], [], (n // tm,))
    return xn, h


def _norm_loss(name, x, u, target, post, weight, tm):
    n, d = x.shape

    def fn(xv, uv, tv, g, gt):
        e = _post(xv, uv, g, gt, weight) - tv
        dv = e * (1.0 / d)
        du, reds = _post_bwd(dv, uv, g, gt, weight)
        return (dv, du), (_colsum(e * e), *reds)

    (dx, du), reds = _rowmap(name, fn, [_tile(x, tm), _tile(u, tm), _tile(target, tm), *post],
                             [_otile(n, d, F32, tm), _otile(n, d, BF16, tm)], [_ored(d)] * 3, (n // tm,))
    return dx, du, reds


def _norm_link_bwd(name, dh, x, dout, u_prev, pre, post_prev, weight_prev, out_dtype, tm, after=None, u_scale=None):
    n, d = x.shape
    rows = [] if u_scale is None else [_row(u_scale)]

    def fn(dhv, xv, dv, uv, g, sc, g2, gt, *ps):
        dx, reds = _pre_bwd(dhv, xv, dv, g, sc)
        du, reds_prev = _post_bwd(dx, uv * ps[0] if ps else uv, g2, gt, weight_prev)
        if not ps:
            return (dx, du), (*reds, *reds_prev)
        d_unscaled = du * ps[0]
        return (dx, d_unscaled), (*reds, *reds_prev, _colsum(du * uv), _colsum(d_unscaled))

    (dx, du), reds = _rowmap(name, fn, [_tile(dh, tm), _tile(x, tm), _tile(dout, tm), _tile(u_prev, tm), *pre, *post_prev, *rows],
                             [_otile(n, d, F32, tm), _otile(n, d, out_dtype, tm)], [_ored(d)] * (5 + 2 * len(rows)), (n // tm,),
                             after=after)
    return dx, du, reds


def _prenorm_bwd(name, dh, x, dout, pre, tm, after=None):
    n, d = x.shape

    def fn(dhv, xv, dv, g, sc):
        dx, reds = _pre_bwd(dhv, xv, dv, g, sc)
        return (dx,), reds

    (dx,), reds = _rowmap(name, fn, [_tile(dh, tm), _tile(x, tm), _tile(dout, tm), *pre], [_otile(n, d, F32, tm)], [_ored(d)] * 3,
                          (n // tm,), after=after)
    return dx, reds


def _ffn_fwd(tag, h, w_in8, w_out4, tm, sub, after=None):
    s, d = h.shape

    extra = [] if after is None else [after]

    def body(h_ref, wg_ref, wu_ref, wo_ref, *rest):
        u_ref, gu_ref, acc = rest[-3:]
        j = pl.program_id(1)

        @pl.when(j == 0)
        def _():
            acc[...] = jnp.zeros_like(acc)

        for r in range(tm // sub):
            rows = pl.ds(r * sub, sub)
            hv = h_ref[rows, :]
            gate = _dot(hv, wg_ref[...], NT)
            up = _dot(hv, wu_ref[...], NT)
            gu_ref[0, rows, :] = gate.astype(BF16)
            gu_ref[1, rows, :] = up.astype(BF16)
            acc[rows, :] += _dot((gate * jax.nn.sigmoid(gate) * up).astype(BF16), wo_ref[...], NN)

        @pl.when(j == w_out4.shape[0] - 1)
        def _():
            u_ref[...] = acc[...].astype(u_ref.dtype)

    w_blk = (None, FF_BLK, d)
    return pl.pallas_call(
        body, name=f"ffn_fwd_{tag}", grid=(s // tm, 4),
        in_specs=[pl.BlockSpec((tm, d), lambda i, j: (i, 0)), pl.BlockSpec(w_blk, lambda i, j: (j, 0, 0)),
                  pl.BlockSpec(w_blk, lambda i, j: (j + 4, 0, 0)), pl.BlockSpec((None, FF_BLK, d), lambda i, j: (j, 0, 0))]
        + [pl.BlockSpec(memory_space=pl.ANY)] * len(extra),
        out_specs=[pl.BlockSpec((tm, d), lambda i, j: (i, 0)), pl.BlockSpec((2, None, tm, FF_BLK), lambda i, j: (0, j, i, 0))],
        out_shape=[_sds((s, d), BF16), _sds((2, 4, s, FF_BLK), BF16)], scratch_shapes=[pltpu.VMEM((tm, d), F32)],
        compiler_params=_params(),
    )(h, w_in8, w_in8, w_out4, *extra)


def _ffn_bwd_act(tag, du, gu, w_in8, w_out4, tm, sub):
    s, d = du.shape

    def body(du_ref, gu_ref, wg_ref, wu_ref, wo_ref, dh_ref, dgu_ref, act_ref, acc):
        j = pl.program_id(1)

        @pl.when(j == 0)
        def _():
            acc[...] = jnp.zeros_like(acc)

        n_sub = tm // sub
        dact_next = _dot(du_ref[pl.ds(0, sub), :], wo_ref[...], NT)
        for r in range(n_sub):
            rows = pl.ds(r * sub, sub)
            dact = dact_next
            if r + 1 < n_sub:
                dact_next = _dot(du_ref[pl.ds((r + 1) * sub, sub), :], wo_ref[...], NT)
            gate, up = gu_ref[0, rows, :].astype(F32), gu_ref[1, rows, :].astype(F32)
            sg = jax.nn.sigmoid(gate)
            silu = gate * sg
            dg = dact * up * (sg * (1.0 + gate * (1.0 - sg)))
            dup = dact * silu
            dgu_ref[0, :, rows] = dg.T.astype(BF16)
            dgu_ref[1, :, rows] = dup.T.astype(BF16)
            act_ref[:, rows] = (silu * up).T.astype(BF16)
            acc[rows, :] += _dot(dg.astype(BF16), wg_ref[...], NN) + _dot(dup.astype(BF16), wu_ref[...], NN)

        @pl.when(j == w_out4.shape[0] - 1)
        def _():
            dh_ref[...] = acc[...].astype(dh_ref.dtype)

    w_blk = (None, FF_BLK, d)
    dh, dgu_t, act_t = pl.pallas_call(
        body, name=f"ffn_bwd_{tag}", grid=(s // tm, 4),
        in_specs=[pl.BlockSpec((tm, d), lambda i, j: (i, 0)), pl.BlockSpec((2, None, tm, FF_BLK), lambda i, j: (0, j, i, 0)),
                  pl.BlockSpec(w_blk, lambda i, j: (j, 0, 0)), pl.BlockSpec(w_blk, lambda i, j: (j + 4, 0, 0)),
                  pl.BlockSpec((None, FF_BLK, d), lambda i, j: (j, 0, 0))],
        out_specs=[pl.BlockSpec((tm, d), lambda i, j: (i, 0)), pl.BlockSpec((2, None, FF_BLK, tm), lambda i, j: (0, j, 0, i)),
                   pl.BlockSpec((None, FF_BLK, tm), lambda i, j: (j, 0, i))],
        out_shape=[_sds((s, d), BF16), _sds((2, 4, FF_BLK, s), BF16), _sds((4, FF_BLK, s), BF16)],
        scratch_shapes=[pltpu.VMEM((tm, d), F32)], compiler_params=_params(),
    )(du, gu, w_in8, w_in8, w_out4)
    return dh, dgu_t.reshape(8, FF_BLK, s), act_t


def _ffn_dw(name, lhs_t, rhs, tk, after=None):
    n_g, _, s = lhs_t.shape
    d = rhs.shape[1]
    return _matmul(name, lhs_t, rhs, a_blk=(None, FF_BLK, tk), a_map=lambda g, k: (g, 0, k), b_blk=(tk, d), b_map=lambda g, k: (k, 0),
                   o_shape=(n_g, FF_BLK, d), o_blk=(None, FF_BLK, d), o_map=lambda g, k: (g, 0, 0), grid=(n_g, s // tk),
                   contract=NN, out_dtype=BF16, after=after)


def _window_sum(x, window, transpose):
    s = x.shape[0]
    t = lax.broadcasted_iota(jnp.int32, (s, 1), 0)
    half = window // 2
    cnt = jnp.minimum(t + half, s) - jnp.maximum(t - half, 0)
    inv = 1.0 / cnt.astype(F32)
    if transpose:
        x = x * inv
        offsets = range(-half + 1, half + 1)
    else:
        offsets = range(-half, half)
    acc = jnp.zeros_like(x)
    for o in offsets:
        shifted = x if o == 0 else pltpu.roll(x, (-o) % s, 0)
        valid = jnp.logical_and(t + o >= 0, t + o < s)
        acc = acc + jnp.where(valid, shifted, 0.0)
    return acc if transpose else acc * inv


def _pool_mix(name, x, transpose, out_dtype):
    s, d = x.shape

    def body(x_ref, o_ref):
        g = pl.program_id(0)
        for gi, window in enumerate(POOL_WINDOWS):
            @pl.when(g == gi)
            def _(window=window):
                xv = x_ref[...].astype(F32)
                o_ref[...] = (_window_sum(xv, window, transpose) - xv).astype(o_ref.dtype)

    return pl.pallas_call(
        body, name=name, grid=(len(POOL_WINDOWS),), in_specs=[pl.BlockSpec((s, POOL_GROUP), lambda g: (0, g))],
        out_specs=pl.BlockSpec((s, POOL_GROUP), lambda g: (0, g)), out_shape=_sds((s, d), out_dtype), compiler_params=_params(),
    )(x)


def _pool_fwd(h, w4, bias, tm):
    s, d = h.shape
    nt = s // tm
    z = _pool_mix("pool_mix", h, False, BF16)
    v = _matmul("pool_proj", z, w4, a_blk=(tm, POOL_GROUP), a_map=lambda i, g, k: (i, g), b_blk=(None, POOL_GROUP, POOL_GROUP),
                b_map=lambda i, g, k: (g, 0, 0), o_shape=(s, d), o_blk=(tm, POOL_GROUP), o_map=lambda i, g, k: (i, g),
                grid=(nt, 4, 1), contract=NN, out_dtype=F32, bias=bias, bias_blk=(1, POOL_GROUP), bias_map=lambda i, g, k: (0, g))
    return v, z


def _pool_bwd(dv, z, w4, tm):
    s, d = dv.shape
    nt = s // tm
    dw4 = _matmul("pool_dw", z, dv, a_blk=(tm, POOL_GROUP), a_map=lambda g, k: (k, g), b_blk=(tm, POOL_GROUP),
                  b_map=lambda g, k: (k, g), o_shape=(4, POOL_GROUP, POOL_GROUP), o_blk=(None, POOL_GROUP, POOL_GROUP),
                  o_map=lambda g, k: (g, 0, 0), grid=(4, nt), contract=TN, out_dtype=F32)
    dz = _matmul("pool_dz", dv, w4, a_blk=(tm, POOL_GROUP), a_map=lambda i, g, k: (i, g), b_blk=(None, POOL_GROUP, POOL_GROUP),
                 b_map=lambda i, g, k: (g, 0, 0), o_shape=(s, d), o_blk=(tm, POOL_GROUP), o_map=lambda i, g, k: (i, g),
                 grid=(nt, 4, 1), contract=NT, out_dtype=F32)
    dh = _pool_mix("pool_mix_t", dz, True, BF16)
    return dh, dw4


def _lane(shape):
    return lax.broadcasted_iota(jnp.int32, shape, 1)


def _rope_swap(v, transpose):
    half = QK_ROPE // 2
    lane = _lane(v.shape)
    up = pltpu.roll(v, v.shape[1] - half, 1)
    down = pltpu.roll(v, half, 1)
    if transpose:
        return jnp.where(lane < half, up, jnp.where(lane < QK_ROPE, -down, 0.0))
    return jnp.where(lane < half, -up, jnp.where(lane < QK_ROPE, down, 0.0))


def _rope(v, cos, sin):
    return v * cos + _rope_swap(v, False) * sin


def _rope_t(g, cos, sin):
    return g * cos + _rope_swap(g * sin, True)


def _mla_mid(lat, q_norm, kv_norm, cos_k, sin_k, tm):
    s = lat.shape[0]

    def fn(lv, qn, kn, cs, sn):
        cq = lv[:, :Q_LORA]
        ckv = lv[:, Q_LORA:Q_LORA + KV_LORA]
        kr = lv[:, Q_LORA + KV_LORA:]
        cq = cq * _rstd(cq) * qn
        ckv = ckv * _rstd(ckv) * kn
        k_rope = jnp.where(_lane(kr.shape) == ONE_COL - KV_LORA, 1.0, _rope(kr, cs, sn))
        return (cq, jnp.concatenate([ckv, k_rope], axis=1)), ()

    (cq, kcat), _ = _rowmap("mla_mid", fn, [_tile(lat, tm), _row(q_norm), _row(kv_norm), _tile(cos_k, tm), _tile(sin_k, tm)],
                            [_otile(s, Q_LORA, BF16, tm), _otile(s, QK_PAD, BF16, tm)], [], (s // tm,))
    return cq, kcat


def _mla_mid_bwd(lat, dcq, dkcat, dv, q_norm, kv_norm, cos_k, sin_k, tm):
    s = lat.shape[0]

    def fn(lv, dq, dk, dvv, qn, kn, cs, sn):
        dk = dk * (1.0 / LOG2_E)
        cq = lv[:, :Q_LORA]
        ckv = lv[:, Q_LORA:Q_LORA + KV_LORA]
        rq, rk = _rstd(cq), _rstd(ckv)
        cqn, ckn = cq * rq, ckv * rk
        a = dq * qn
        d_cq = rq * (a - cqn * jnp.mean(a * cqn, axis=-1, keepdims=True))
        dckv = dk[:, :KV_LORA] + dvv
        a2 = dckv * kn
        d_ckv = rk * (a2 - ckn * jnp.mean(a2 * ckn, axis=-1, keepdims=True))
        d_kr = _rope_t(dk[:, KV_LORA:], cs, sn)
        return (jnp.concatenate([d_cq, d_ckv, d_kr], axis=1),), (_colsum(dq * cqn), _colsum(dckv * ckn))

    (dlat,), reds = _rowmap(
        "mla_mid_bwd", fn,
        [_tile(lat, tm), _tile(dcq, tm), _tile(dkcat, tm), _tile(dv, tm), _row(q_norm), _row(kv_norm), _tile(cos_k, tm),
         _tile(sin_k, tm)],
        [_otile(s, LAT_PAD, BF16, tm)], [_ored(Q_LORA), _ored(KV_LORA)], (s // tm,))
    return dlat, reds


def _mla_q(cq, wq, wukp, cos_k, sin_k, tm):
    s = cq.shape[0]

    def body(cq_ref, wq_ref, wuk_ref, cos_ref, sin_ref, o_ref):
        cqv, cs, sn = cq_ref[...], cos_ref[...], sin_ref[...]
        for h in range(N_HEADS):
            aq = _dot(cqv, wq_ref[h], NN)
            qlat = _dot(aq.astype(BF16), wuk_ref[h], NN)
            roped = _rope(aq[:, KV_LORA:], cs, sn)
            o_ref[h] = (jnp.concatenate([qlat[:, :KV_LORA], roped], axis=1) * (ATTN_SCALE * LOG2_E)).astype(o_ref.dtype)

    wblk = pl.BlockSpec((N_HEADS, QK_PAD, QK_PAD), lambda i: (0, 0, 0))
    tblk = pl.BlockSpec((tm, KV_LORA), lambda i: (i, 0))
    return pl.pallas_call(
        body, name="mla_q", grid=(s // tm,),
        in_specs=[pl.BlockSpec((tm, Q_LORA), lambda i: (i, 0)), wblk, wblk, tblk, tblk],
        out_specs=pl.BlockSpec((N_HEADS, tm, QK_PAD), lambda i: (0, i, 0)), out_shape=_sds((N_HEADS, s, QK_PAD), BF16),
        compiler_params=_params(),
    )(cq, wq, wukp, cos_k, sin_k)


def _mla_q_bwd(cq, wq, wukp, cos_k, sin_k, dqcat, tm):
    s = cq.shape[0]

    def body(cq_ref, wq_ref, wuk_ref, cos_ref, sin_ref, dq_ref, dcq_ref, dwq_ref, dwuk_ref):
        @pl.when(pl.program_id(0) == 0)
        def _():
            dwq_ref[...] = jnp.zeros_like(dwq_ref)
            dwuk_ref[...] = jnp.zeros_like(dwuk_ref)

        cqv, cs, sn = cq_ref[...], cos_ref[...], sin_ref[...]
        d_cq = jnp.zeros((tm, Q_LORA), F32)
        for h in range(N_HEADS):
            aq = _dot(cqv, wq_ref[h], NN).astype(BF16)
            g = dq_ref[h].astype(F32) * ATTN_SCALE
            gl, gr = g[:, :KV_LORA], g[:, KV_LORA:]
            dqlat = jnp.concatenate([gl, jnp.zeros_like(gl)], axis=1).astype(BF16)
            d_rope = _rope_t(gr, cs, sn)
            daq = _dot(dqlat, wuk_ref[h], NT) + jnp.concatenate([jnp.zeros_like(d_rope), d_rope], axis=1)
            daq_b = daq.astype(BF16)
            dwuk_ref[h] += _dot(aq, dqlat, TN)
            dwq_ref[h] += _dot(cqv, daq_b, TN)
            d_cq = d_cq + _dot(daq_b, wq_ref[h], NT)
        dcq_ref[...] = d_cq

    wblk = pl.BlockSpec((N_HEADS, QK_PAD, QK_PAD), lambda i: (0, 0, 0))
    tblk = pl.BlockSpec((tm, KV_LORA), lambda i: (i, 0))
    return pl.pallas_call(
        body, name="mla_q_bwd", grid=(s // tm,),
        in_specs=[pl.BlockSpec((tm, Q_LORA), lambda i: (i, 0)), wblk, wblk, tblk, tblk,
                  pl.BlockSpec((N_HEADS, tm, QK_PAD), lambda i: (0, i, 0))],
        out_specs=[pl.BlockSpec((tm, Q_LORA), lambda i: (i, 0)), wblk, wblk],
        out_shape=[_sds((s, Q_LORA), F32), _sds((N_HEADS, QK_PAD, QK_PAD), F32), _sds((N_HEADS, QK_PAD, QK_PAD), F32)],
        compiler_params=_params(),
    )(cq, wq, wukp, cos_k, sin_k, dqcat)


def _flash_fwd(qcat, kcat, tq, tk):
    n_h, s, _ = qcat.shape
    n_k = s // tk

    def body(q_ref, k_ref, o_ref, lse_ref):
        q = q_ref[...]
        m = jnp.full((tq, 1), -1e30, F32)
        acc = jnp.zeros((tq, QK_PAD), F32)
        for kk in range(n_k):
            k = k_ref[pl.ds(kk * tk, tk), :]
            sc = _dot(q, k, NT)
            m_new = jnp.maximum(m, jnp.max(sc, axis=1, keepdims=True))
            p = jnp.exp2(sc - m_new).astype(BF16)
            acc = jnp.exp2(m - m_new) * acc + _dot(p, k, NN)
            m = m_new
        l = jnp.sum(jnp.where(_lane(acc.shape) == ONE_COL, acc, 0.0), axis=1, keepdims=True)
        o_ref[...] = (acc[:, :KV_LORA] / l).astype(o_ref.dtype)
        lse_ref[...] = m + jnp.log2(l)

    return pl.pallas_call(
        body, name="mla_attn", grid=(n_h, s // tq),
        in_specs=[pl.BlockSpec((None, tq, QK_PAD), lambda h, i: (h, i, 0)), pl.BlockSpec((s, QK_PAD), lambda h, i: (0, 0))],
        out_specs=[pl.BlockSpec((None, tq, KV_LORA), lambda h, i: (h, i, 0)), pl.BlockSpec((None, tq, 1), lambda h, i: (h, i, 0))],
        out_shape=[_sds((n_h, s, KV_LORA), BF16), _sds((n_h, s, 1), F32)], compiler_params=_params(),
    )(qcat, kcat)


def _flash_bwd(qcat, kcat, o, do, lse, tq, tk):
    n_h, s, _ = qcat.shape
    n_k = s // tk

    def body(q_ref, k_ref, v_ref, o_ref, do_ref, lse_ref, dq_ref, dk_ref, dv_ref, dq_acc):
        h, i = pl.program_id(0), pl.program_id(1)

        @pl.when(jnp.logical_and(h == 0, i == 0))
        def _():
            dk_ref[...] = jnp.zeros_like(dk_ref)
            dv_ref[...] = jnp.zeros_like(dv_ref)

        q = q_ref[...]
        dov = do_ref[...]
        dov_t = dov.T
        q_t = q.T
        lse_v = lse_ref[...]
        delta = jnp.sum(dov.astype(F32) * o_ref[...].astype(F32), axis=1, keepdims=True)
        dq_acc[...] = jnp.zeros_like(dq_acc)

        for kk in range(n_k):
            rows = pl.ds(kk * tk, tk)
            k = k_ref[rows, :]
            p = jnp.exp2(_dot(q, k, NT) - lse_v)
            dp = _dot(dov, v_ref[rows, :], NT)
            ds = (p * (dp - delta)).astype(BF16)
            dq_acc[...] += _dot(ds, k, NN)
            dv_ref[:, rows] += _dot(dov_t, p.astype(BF16), NN)
            dk_ref[:, rows] += _dot(q_t, ds, NN)
        dq_ref[...] = dq_acc[...].astype(dq_ref.dtype)

    qblk = pl.BlockSpec((None, tq, QK_PAD), lambda h, i: (h, i, 0))
    oblk = pl.BlockSpec((None, tq, KV_LORA), lambda h, i: (h, i, 0))
    return pl.pallas_call(
        body, name="mla_attn_bwd", grid=(n_h, s // tq),
        in_specs=[qblk, pl.BlockSpec((s, QK_PAD), lambda h, i: (0, 0)), pl.BlockSpec((s, KV_LORA), lambda h, i: (0, 0)), oblk, oblk,
                  pl.BlockSpec((None, tq, 1), lambda h, i: (h, i, 0))],
        out_specs=[qblk, pl.BlockSpec((QK_PAD, s), lambda h, i: (0, 0)), pl.BlockSpec((KV_LORA, s), lambda h, i: (0, 0))],
        out_shape=[_sds((n_h, s, QK_PAD), BF16), _sds((QK_PAD, s), F32), _sds((KV_LORA, s), F32)],
        scratch_shapes=[pltpu.VMEM((tq, QK_PAD), F32)], compiler_params=_params(),
    )(qcat, kcat, kcat, o, do, lse)


def _mla_uv(o_lat, wuv2, do, tm):
    n_h, s, _ = o_lat.shape
    d = n_h * V_HEAD
    pair = 2 * V_HEAD
    lat_blk = pl.BlockSpec((n_h, tm, KV_LORA), lambda i: (0, i, 0))
    w_blk = pl.BlockSpec((n_h, KV_LORA, pair), lambda i: (0, 0, 0))
    row_blk = pl.BlockSpec((tm, d), lambda i: (i, 0))

    if do is None:
        def body(a_ref, w_ref, o_ref):
            for p in range(n_h // 2):
                o_ref[:, p * pair:(p + 1) * pair] = (
                    _dot(a_ref[2 * p], w_ref[2 * p], NN) + _dot(a_ref[2 * p + 1], w_ref[2 * p + 1], NN)).astype(o_ref.dtype)

        return pl.pallas_call(body, name="mla_uv", grid=(s // tm,), in_specs=[lat_blk, w_blk], out_specs=row_blk,
                              out_shape=_sds((s, d), BF16), compiler_params=_params())(o_lat, wuv2)

    def body(a_ref, w_ref, do_ref, dlat_ref, dw_ref):
        @pl.when(pl.program_id(0) == 0)
        def _():
            dw_ref[...] = jnp.zeros_like(dw_ref)

        for h in range(n_h):
            dov = do_ref[:, (h // 2) * pair:(h // 2 + 1) * pair]
            dlat_ref[h] = _dot(dov, w_ref[h], NT).astype(dlat_ref.dtype)
            dw_ref[h] += _dot(a_ref[h], dov, TN)

    return pl.pallas_call(body, name="mla_uv_bwd", grid=(s // tm,), in_specs=[lat_blk, w_blk, row_blk], out_specs=[lat_blk, w_blk],
                          out_shape=[_sds((n_h, s, KV_LORA), BF16), _sds((n_h, KV_LORA, pair), F32)], compiler_params=_params(),
                          )(o_lat, wuv2, do)


def _mla_fwd(h, wts, cos_k, sin_k, tm):
    s, d = h.shape
    nt = s // tm
    lat = _matmul("mla_lat", h, wts["w_in"], a_blk=(tm, d), a_map=lambda i, k: (i, 0), b_blk=(d, LAT_PAD), b_map=lambda i, k: (0, 0),
                  o_shape=(s, LAT_PAD), o_blk=(tm, LAT_PAD), o_map=lambda i, k: (i, 0), grid=(nt, 1), contract=NN, out_dtype=F32)
    cq, kcat = _mla_mid(lat, wts["q_norm"], wts["kv_norm"], cos_k, sin_k, tm)
    qcat = _mla_q(cq, wts["wq"], wts["wukp"], cos_k, sin_k, tm)
    o_lat, lse = _flash_fwd(qcat, kcat, min(4 * tm, s), tm)
    o = _mla_uv(o_lat, wts["wuv2"], None, tm)
    u = _matmul("mla_out", o, wts["w_o"], a_blk=(tm, d), a_map=lambda i, k: (i, 0), b_blk=(d, d), b_map=lambda i, k: (0, 0),
                o_shape=(s, d), o_blk=(tm, d), o_map=lambda i, k: (i, 0), grid=(nt, 1), contract=NN, out_dtype=BF16)
    return u, (lat, cq, kcat, qcat, o_lat, lse, o)


def _mla_bwd(du, h, saved, wts, cos_k, sin_k, tm):
    lat, cq, kcat, qcat, o_lat, lse, o = saved
    s, d = h.shape
    nt = s // tm
    do = _matmul("mla_do", du, wts["w_o"], a_blk=(tm, d), a_map=lambda i, k: (i, 0), b_blk=(d, d), b_map=lambda i, k: (0, 0),
                 o_shape=(s, d), o_blk=(tm, d), o_map=lambda i, k: (i, 0), grid=(nt, 1), contract=NT, out_dtype=BF16)
    dw_o = _matmul("mla_dwo", o, du, a_blk=(tm, d), a_map=lambda k: (k, 0), b_blk=(tm, d), b_map=lambda k: (k, 0),
                   o_shape=(d, d), o_blk=(d, d), o_map=lambda k: (0, 0), grid=(nt,), contract=TN, out_dtype=F32)
    do_lat, dwuv2 = _mla_uv(o_lat, wts["wuv2"], do, tm)
    dqcat, dkcat_t, dv_t = _flash_bwd(qcat, kcat, o_lat, do_lat, lse, min(2 * tm, s), tm)
    dkcat, dv = dkcat_t.T, dv_t.T
    dcq, dwq, dwukp = _mla_q_bwd(cq, wts["wq"], wts["wukp"], cos_k, sin_k, dqcat, tm)
    dlat, (dqn, dkn) = _mla_mid_bwd(lat, dcq, dkcat, dv, wts["q_norm"], wts["kv_norm"], cos_k, sin_k, tm)
    dh = _matmul("mla_dh", dlat, wts["w_in"], a_blk=(tm, LAT_PAD), a_map=lambda i, k: (i, 0), b_blk=(d, LAT_PAD),
                 b_map=lambda i, k: (0, 0), o_shape=(s, d), o_blk=(tm, d), o_map=lambda i, k: (i, 0), grid=(nt, 1), contract=NT,
                 out_dtype=BF16)
    dw_in = _matmul("mla_dwin", h, dlat, a_blk=(tm, d), a_map=lambda k: (k, 0), b_blk=(tm, LAT_PAD), b_map=lambda k: (k, 0),
                    o_shape=(d, LAT_PAD), o_blk=(d, LAT_PAD), o_map=lambda k: (0, 0), grid=(nt,), contract=TN, out_dtype=F32)
    return dh, dict(w_in=dw_in, wq=dwq, wukp=dwukp, wuv2=dwuv2, w_o=dw_o, q_norm=dqn, kv_norm=dkn)


def _adamw(name, parts, w, m, v):
    n_parts, r, c = parts.shape
    tr = r
    for cand in (256, 128, 64, 32, 16, 8):
        if r > cand and r % cand == 0:
            tr = cand
            break

    def body(p_ref, w_ref, m_ref, v_ref, g_ref, d_ref, nm_ref, nv_ref):
        g = p_ref[0].astype(F32)
        for k in range(1, n_parts):
            g = g + p_ref[k].astype(F32)
        nm = ADAM_B1 * m_ref[...] + (1.0 - ADAM_B1) * g
        nv = ADAM_B2 * v_ref[...] + (1.0 - ADAM_B2) * (g * g)
        m_hat = nm / (1.0 - ADAM_B1 ** ADAM_STEP)
        v_hat = nv / (1.0 - ADAM_B2 ** ADAM_STEP)
        g_ref[...] = g
        d_ref[...] = -ADAM_LR * (m_hat / (jnp.sqrt(v_hat) + ADAM_EPS) + ADAM_WD * w_ref[...])
        nm_ref[...] = nm
        nv_ref[...] = nv

    blk = pl.BlockSpec((tr, c), lambda i: (i, 0))
    return pl.pallas_call(
        body, name=name, grid=(r // tr,), in_specs=[pl.BlockSpec((n_parts, tr, c), lambda i: (0, i, 0)), blk, blk, blk],
        out_specs=[blk] * 4, out_shape=[_sds((r, c), F32)] * 4, compiler_params=_params(),
    )(parts, w, m, v)


def _adamw_slab(name, parts, w, m, v, bufs, f):
    n_parts, r, c = parts.shape
    tr = max(t for t in range(8, 257, 8) if r % t == 0)

    def body(p_ref, w_ref, m_ref, v_ref, *rest):
        g_ref, d_ref, nm_ref, nv_ref = rest[4:]
        g = p_ref[0].astype(F32)
        for k in range(1, n_parts):
            g = g + p_ref[k].astype(F32)
        nm = ADAM_B1 * m_ref[...] + (1.0 - ADAM_B1) * g
        nv = ADAM_B2 * v_ref[...] + (1.0 - ADAM_B2) * (g * g)
        m_hat = nm / (1.0 - ADAM_B1 ** ADAM_STEP)
        v_hat = nv / (1.0 - ADAM_B2 ** ADAM_STEP)
        g_ref[...] = g
        d_ref[...] = -ADAM_LR * (m_hat / (jnp.sqrt(v_hat) + ADAM_EPS) + ADAM_WD * w_ref[...])
        nm_ref[...] = nm
        nv_ref[...] = nv

    blk = pl.BlockSpec((None, tr, c), lambda i: (f, i, 0))
    return pl.pallas_call(
        body, name=name, grid=(r // tr,),
        in_specs=[pl.BlockSpec((n_parts, tr, c), lambda i: (0, i, 0)), blk, blk, blk] + [pl.BlockSpec(memory_space=pl.ANY)] * 4,
        out_specs=[blk] * 4, out_shape=[_sds(w.shape, F32)] * 4, input_output_aliases={4 + j: j for j in range(4)},
        compiler_params=_params(),
    )(parts, w, m, v, *bufs)


def _mesh_pos():
    return lax.axis_index("x"), lax.axis_index("y"), lax.axis_index("c")


def _flip(pos, mask):
    return tuple(1 - p if (mask >> (2 - b)) & 1 else p for b, p in enumerate(pos))


def _index(pos):
    return 4 * pos[0] + 2 * pos[1] + pos[2]


def _all_gather(name, xs, after=None):
    n = len(xs)
    extra = [] if after is None else [after]

    def body(*refs):
        x_refs, o_refs = refs[:n], refs[n + len(extra):2 * n + len(extra)]
        send_sems, recv_sems, local_sems = refs[2 * n + len(extra):]
        me = _mesh_pos()
        sibling = _flip(me, 1)
        others = [_flip(me, 4), _flip(me, 2), _flip(me, 6)]

        def copy(k, j, block, to, src=None):
            dst = o_refs[k].at[_index(block)]
            return pltpu.make_async_remote_copy(
                src_ref=dst if src is None else src, dst_ref=dst, send_sem=send_sems.at[k, j], recv_sem=recv_sems.at[k, j],
                device_id=to, device_id_type=MESH)

        local = [pltpu.make_async_copy(x_refs[k], o_refs[k].at[_index(me)], local_sems.at[k]) for k in range(n)]
        for cp in local:
            cp.start()
        first = []
        for k in range(n):
            first.append(copy(k, 0, me, sibling, src=x_refs[k]))
            first += [copy(k, 1 + j, me, other, src=x_refs[k]) for j, other in enumerate(others)]
        for cp in first:
            cp.start()
        passed = []
        for j, other in enumerate(others):
            for k in range(n):
                copy(k, 1 + j, other, me).wait_recv()
                cp = copy(k, 4 + j, other, sibling)
                cp.start()
                passed.append(cp)
        for k in range(n):
            copy(k, 0, sibling, me).wait_recv()
        for j, other in enumerate(others):
            for k in range(n):
                copy(k, 4 + j, _flip(other, 1), me).wait_recv()
        for cp in first + passed:
            cp.wait_send()
        for cp in local:
            cp.wait()

    any_spec = pl.BlockSpec(memory_space=pl.ANY)
    return pl.pallas_call(
        body, name=name, in_specs=[any_spec] * (n + len(extra)), out_specs=[any_spec] * n,
        out_shape=[_sds((N_DEV,) + x.shape, x.dtype) for x in xs],
        scratch_shapes=[pltpu.SemaphoreType.DMA((n, 7)), pltpu.SemaphoreType.DMA((n, 7)), pltpu.SemaphoreType.DMA((n,))],
    )(*xs, *extra)


_MASKS = {
    "gather": tuple(range(1, N_DEV)),
    "scatter": tuple(range(1, N_DEV)),
    "own": (1, 4, 2, 6),
    "pass": (4, 2, 6),
}


def _split_copies(kind, outgoing, x_refs, land_refs, send_sems, recv_sems):
    me = _mesh_pos()
    masks = _MASKS[kind]
    copies = []
    for k, (x_ref, land_ref) in enumerate(zip(x_refs, land_refs)):
        for j, mask in enumerate(masks):
            if kind == "pass":
                peer = _flip(me, 1)
                src = land_ref.at[_index(_flip(me, mask))]
                dst = land_ref.at[_index(_flip(me, mask if outgoing else mask | 1))]
            else:
                peer = _flip(me, mask)
                src = x_ref.at[_index(peer)] if kind == "scatter" else x_ref
                dst = land_ref.at[_index(me if outgoing else peer)]
            sem = k * len(masks) + j
            copies.append(pltpu.make_async_remote_copy(src_ref=src, dst_ref=dst, send_sem=send_sems.at[sem], recv_sem=recv_sems.at[sem],
                                                       device_id=peer, device_id_type=MESH))
    return copies


_HBM_SPEC = pl.BlockSpec(memory_space=pltpu.HBM)
_SEM_SPEC = pl.BlockSpec(memory_space=pltpu.SEMAPHORE)
_EFFECT = pltpu.SideEffectType.DATAFLOW_SIDE_EFFECTING


def _split_start(name, kind, xs, after=None):
    n = len(xs)
    n_sem = n * len(_MASKS[kind])
    extra = [] if after is None else [after]
    lands = [lax.empty(x.shape if kind == "scatter" else (N_DEV,) + x.shape, x.dtype) for x in xs]

    def body(*refs):
        x_refs, land_refs = refs[:n], refs[n:2 * n]
        send_sems, recv_sems = refs[2 * n + len(extra)], refs[2 * n + len(extra) + 1]
        token = refs[-1]
        for cp in _split_copies(kind, True, x_refs, land_refs, send_sems, recv_sems):
            cp.start()
        token[...] = jnp.zeros_like(token)

    hbm = [pltpu.HBM(a.shape, a.dtype) for a in list(xs) + lands]
    res = pl.pallas_call(
        body, name=name,
        out_shape=[pltpu.SemaphoreType.DMA((n_sem,)), pltpu.SemaphoreType.DMA((n_sem,))] + hbm + [_sds((8, 128), F32)],
        in_specs=[_HBM_SPEC] * (2 * n) + [pl.BlockSpec(memory_space=pl.ANY)] * len(extra),
        out_specs=[_SEM_SPEC, _SEM_SPEC] + [_HBM_SPEC] * (2 * n) + [pl.BlockSpec(memory_space=pltpu.VMEM)],
        input_output_aliases={j: 2 + j for j in range(2 * n)}, compiler_params=pltpu.CompilerParams(has_side_effects=_EFFECT),
    )(*[pltpu.with_memory_space_constraint(a, pltpu.HBM) for a in list(xs) + lands], *extra)
    return (kind, n, res[0], res[1], res[2:2 + 2 * n]), res[-1]


def _split_pass(name, state, after):
    kind, n, send_sems_in, recv_sems_in, thru = state
    n_sem = n * len(_MASKS["pass"])

    def body(*refs):
        x_refs, land_refs = refs[:n], refs[n:2 * n]
        send_sems, recv_sems = refs[2 * n], refs[2 * n + 1]
        next_send, next_recv, token = refs[-3:]
        for cp in _split_copies(kind, True, x_refs, land_refs, send_sems, recv_sems):
            cp.wait_send()
        for cp in _split_copies(kind, False, x_refs, land_refs, send_sems, recv_sems):
            cp.wait_recv()
        for cp in _split_copies("pass", True, land_refs, land_refs, next_send, next_recv):
            cp.start()
        token[...] = jnp.zeros_like(token)

    res = pl.pallas_call(
        body, name=name,
        out_shape=[pltpu.HBM(a.shape, a.dtype) for a in thru] + [pltpu.SemaphoreType.DMA((n_sem,)), pltpu.SemaphoreType.DMA((n_sem,)),
                                                                 _sds((8, 128), F32)],
        in_specs=[_HBM_SPEC] * (2 * n) + [_SEM_SPEC, _SEM_SPEC, pl.BlockSpec(memory_space=pl.ANY)],
        out_specs=[_HBM_SPEC] * (2 * n) + [_SEM_SPEC, _SEM_SPEC, pl.BlockSpec(memory_space=pltpu.VMEM)],
        input_output_aliases={j: j for j in range(2 * n)}, compiler_params=pltpu.CompilerParams(has_side_effects=_EFFECT),
    )(*thru, send_sems_in, recv_sems_in, after)
    return ("pass", n, res[2 * n], res[2 * n + 1], res[:2 * n]), res[-1]


def _split_wait(name, state, after):
    kind, n, send_sems_in, recv_sems_in, thru = state

    def body(*refs):
        x_refs, land_refs = refs[:n], refs[n:2 * n]
        send_sems, recv_sems = refs[2 * n], refs[2 * n + 1]
        for cp in _split_copies(kind, True, x_refs, land_refs, send_sems, recv_sems):
            cp.wait_send()
        for cp in _split_copies(kind, False, x_refs, land_refs, send_sems, recv_sems):
            cp.wait_recv()

    res = pl.pallas_call(
        body, name=name, out_shape=[pltpu.HBM(a.shape, a.dtype) for a in thru],
        in_specs=[_HBM_SPEC] * (2 * n) + [_SEM_SPEC, _SEM_SPEC, pl.BlockSpec(memory_space=pl.ANY)], out_specs=[_HBM_SPEC] * (2 * n),
        input_output_aliases={j: j for j in range(2 * n)}, compiler_params=pltpu.CompilerParams(has_side_effects=_EFFECT),
    )(*thru, send_sems_in, recv_sems_in, after)
    me = _index(_mesh_pos())
    out = []
    for x, land in zip(res[:n], res[n:]):
        own = lax.dynamic_index_in_dim(x, me, 0, keepdims=False) if kind == "scatter" else x
        out.append(lax.dynamic_update_slice(land, own[None], (me,) + (0,) * own.ndim))
    return out


def _rope_tables(s):
    inv = 1.0 / (ROPE_THETA ** (jnp.arange(0, QK_ROPE, 2, dtype=F32) / QK_ROPE))
    ang = jnp.arange(s, dtype=F32)[:, None] * inv[None, :]
    pad = jnp.zeros((s, KV_LORA - QK_ROPE), F32)
    return (jnp.concatenate([jnp.cos(ang), jnp.cos(ang), pad], axis=1), jnp.concatenate([jnp.sin(ang), jnp.sin(ang), pad], axis=1))


def kernel(x, c, ada_w, ada_b, norm_g, ffn_w_in, ffn_w_out, pool_w, pool_b, pool_scale, mla_w_in, mla_q_norm, mla_kv_norm, mla_w_uq, mla_w_uk, mla_w_uv, mla_w_o, loss_target, m_ada_w, m_ada_b, m_norm_g, m_ffn_w_in, m_ffn_w_out, m_pool_w, m_pool_b, m_pool_scale, m_mla_w_in, m_mla_q_norm, m_mla_kv_norm, m_mla_w_uq, m_mla_w_uk, m_mla_w_uv, m_mla_w_o, v_ada_w, v_ada_b, v_norm_g, v_ffn_w_in, v_ffn_w_out, v_pool_w, v_pool_b, v_pool_scale, v_mla_w_in, v_mla_q_norm, v_mla_kv_norm, v_mla_w_uq, v_mla_w_uk, v_mla_w_uv, v_mla_w_o):
    s, d = x.shape[1], x.shape[2]
    tm = min(512, s)
    tr = min(512, s)
    tf = min(1024, s)
    tw = min(4096, s)
    me = 4 * lax.axis_index("x") + 2 * lax.axis_index("y") + lax.axis_index("c")
    x0 = x.reshape(s, d)
    target = loss_target.reshape(s, d)
    n_mod = ada_w.shape[2] * N_DEV // d
    mod_blk = ada_w.shape[2]

    small = jnp.concatenate([c.reshape(-1), norm_g.reshape(-1), pool_b.reshape(-1), mla_q_norm.reshape(-1)]).reshape(1, -1)
    w_in_t, m_in_t, v_in_t = (jnp.swapaxes(a, 2, 3) for a in (ffn_w_in, m_ffn_w_in, v_ffn_w_in))
    w_in_loc = [w_in_t[i, f].astype(BF16) for i in range(2) for f in range(2)]
    w_out_loc = [ffn_w_out[i, f].astype(BF16) for i in range(2) for f in range(2)]
    (small_all,) = _all_gather("gather_small", [small])
    small_all = small_all.reshape(N_DEV, -1)
    c_all = small_all[:, :d]
    off = d
    g_all = small_all[:, off:off + 12 * (d // N_DEV)].reshape(N_DEV, 2, 6, d // N_DEV).transpose(1, 2, 0, 3).reshape(2, 6, d)
    off += 12 * (d // N_DEV)
    pool_b_all = small_all[:, off:off + 4 * POOL_SHARD].reshape(N_DEV, 4, POOL_SHARD).transpose(1, 0, 2).reshape(1, d)
    off += 4 * POOL_SHARD
    q_norm_all = small_all[:, off:off + Q_SHARD].reshape(1, Q_LORA)
    kv_norm_row = mla_kv_norm.reshape(1, KV_LORA)
    pscale_row = pool_scale.reshape(1, d)

    even = (jnp.arange(N_HEADS) % 2 == 0)[:, None, None]
    cos_k, sin_k = _rope_tables(s)

    def mla_weights(mla_w_in_all, mla_w_uq_all, mla_w_o_all):
        uq = mla_w_uq_all.reshape(Q_LORA, N_HEADS, QK_NOPE + QK_ROPE).transpose(1, 0, 2)
        zq = jnp.zeros((N_HEADS, Q_LORA, QK_NOPE), BF16)
        wq = jnp.concatenate(
            [uq[:, :, :QK_NOPE], zq, uq[:, :, QK_NOPE:], jnp.zeros((N_HEADS, Q_LORA, QK_PAD - KV_LORA - QK_ROPE), BF16)], axis=2)
        wukp = jnp.pad(mla_w_uk[0].transpose(1, 2, 0).astype(BF16), ((0, 0), (0, QK_PAD - QK_NOPE), (0, QK_PAD - KV_LORA)))
        uv = mla_w_uv[0].transpose(1, 0, 2).astype(BF16)
        wuv2 = jnp.where(even, jnp.concatenate([uv, jnp.zeros_like(uv)], axis=2), jnp.concatenate([jnp.zeros_like(uv), uv], axis=2))
        return dict(w_in=jnp.pad(mla_w_in_all.reshape(d, -1), ((0, 0), (0, LAT_PAD - mla_w_in.shape[2]))), wq=wq, wukp=wukp,
                    wuv2=wuv2, w_o=mla_w_o_all.reshape(d, d), q_norm=q_norm_all, kv_norm=kv_norm_row)

    (sc_all,), _ = _rowmap("ada_silu", lambda cv: ((cv * jax.nn.sigmoid(cv),), ()), [(c_all, (N_DEV, d), lambda i: (0, 0))],
                           [(_sds((N_DEV, d), F32), (N_DEV, d), lambda i: (0, 0))], [], (1,))
    ada_b_loc = lax.dynamic_slice_in_dim(ada_b, me * mod_blk, mod_blk, axis=1).reshape(2, 1, mod_blk)
    m_pad = 2 * N_DEV
    modp = _matmul("ada_mod", jnp.pad(sc_all, ((0, m_pad - N_DEV), (0, 0))), ada_w, a_blk=(m_pad, d), a_map=lambda i, k: (0, 0),
                   b_blk=(None, d, mod_blk), b_map=lambda i, k: (i, 0, 0), o_shape=(2, m_pad, mod_blk), o_blk=(None, m_pad, mod_blk),
                   o_map=lambda i, k: (i, 0, 0), grid=(2, 1), contract=NN, out_dtype=F32, bias=ada_b_loc, bias_blk=(None, 1, mod_blk),
                   bias_map=lambda i, k: (i, 0, 0))[:, :N_DEV]
    (modp_all,) = _all_gather("gather_mod", [modp.reshape(2 * N_DEV, mod_blk)])
    groups = [[w_in_loc[0], w_out_loc[0], pool_w.reshape(-1, POOL_GROUP).astype(BF16)], [w_in_loc[1], w_out_loc[1]],
              [w_in_loc[2], w_out_loc[2], mla_w_in[0].astype(BF16), mla_w_uq.reshape(mla_w_uq.shape[1], -1).astype(BF16),
               mla_w_o[0].astype(BF16)], [w_in_loc[3], w_out_loc[3]]]
    state, token = _split_start("gather_start_0", "own", groups[0], after=modp_all)
    states = [state]
    w_in8 = [None] * 4
    w_out4 = [None] * 4
    mod = lax.dynamic_index_in_dim(modp_all.reshape(N_DEV, 2, N_DEV, mod_blk), me, axis=2, keepdims=False)
    mod = mod.transpose(1, 0, 2).reshape(2, n_mod, d) + token[0, 0]
    mod_rows = mod.reshape(2 * n_mod, 1, d)
    g_rows = g_all.reshape(12, 1, d)
    weights_of = (0.5, 1.0, 0.5, 0.5, 1.0, 0.5)

    def pre_rows(k, with_shift):
        rows = [_rowk(g_rows, 2 * k), _rowk(mod_rows, 3 * k + 1)]
        return rows + [_rowk(mod_rows, 3 * k)] if with_shift else rows

    def post_rows(k):
        return [_rowk(g_rows, 2 * k + 1), _rowk(mod_rows, 3 * k + 2)]

    def act_dtype(k):
        return F32 if k == 1 else BF16

    saved = []
    xs = x0
    mla_wts = None
    h = _prenorm("prenorm_0", xs, pre_rows(0, True), act_dtype(0), tr)
    token = h
    for name, group in zip("abc", groups[1:]):
        state, token = _split_start(f"gather_start_{name}", "own", group, after=token)
        states.append(state)
    state, token = _split_pass("gather_pass_0", states[0], token)
    lands = _split_wait("gather_wait_0", state, token)
    w_in8[0], w_out4[0] = lands[0], lands[1].reshape(4, FF_BLK, d)
    w4 = lands[2].reshape(N_DEV, 4, POOL_SHARD, POOL_GROUP).transpose(1, 0, 2, 3).reshape(4, POOL_GROUP, POOL_GROUP)
    token = None
    for k in range(6):
        i, sub = divmod(k, 3)
        tag = f"{i}{sub}"
        if k == 1:
            states[1], token = _split_pass("gather_pass_a", states[1], xs)
        if k == 2:
            lands = _split_wait("gather_wait_a", states[1], xs)
            w_in8[1], w_out4[1] = lands[0], lands[1].reshape(4, FF_BLK, d)
            states[2], token = _split_pass("gather_pass_b", states[2], lands[0])
        if k == 3:
            lands = _split_wait("gather_wait_b", states[2], xs)
            w_in8[2], w_out4[2] = lands[0], lands[1].reshape(4, FF_BLK, d)
            mla_wts = mla_weights(*lands[2:])
            states[3], token = _split_pass("gather_pass_c", states[3], lands[0])
        if k == 5:
            lands = _split_wait("gather_wait_c", states[3], xs)
            w_in8[3], w_out4[3] = lands[0], lands[1].reshape(4, FF_BLK, d)
        if sub != 1:
            u, extra = _ffn_fwd(tag, h, w_in8[2 * i + sub // 2], w_out4[2 * i + sub // 2], tf, tf // 2, after=token)
            token = None
        elif i == 0:
            u, extra = _pool_fwd(h, w4, pool_b_all + token[0, 0], min(4 * tm, s))
            token = None
        else:
            u, extra = _mla_fwd(h, mla_wts, cos_k, sin_k, tm)
        saved.append((xs, h, u, extra))
        if k < 5:
            xs, h = _norm_link(f"norm_link_{k + 1}", xs, u, post_rows(k), weights_of[k], pre_rows(k + 1, True), act_dtype(k + 1), tr,
                               u_scale=pscale_row if k == 1 else None)

    dx, du, (sq, dgate, dgpost) = _norm_loss("norm_loss", xs, u, target, post_rows(5), weights_of[5], tr)
    loss_part = (0.5 * jnp.sum(sq) / d).reshape(1, 1)

    d_mod = [None] * (2 * n_mod)
    d_g = [None] * 12
    d_mod[3 * 5 + 2], d_g[2 * 5 + 1] = dgate, dgpost
    sent = {}
    pool_grads = mla_grads = None

    def start_scatter(key, arrays):
        state, token = _split_start(f"scatter_start_{key}", "scatter", arrays)
        sent[key] = state
        return token

    for k in (5, 4, 3, 2, 1, 0):
        i, sub = divmod(k, 3)
        tag = f"{i}{sub}"
        xin, h, u, extra = saved[k]
        token = None
        if sub != 1:
            f = 2 * i + sub // 2
            dh, dgu, act = _ffn_bwd_act(tag, du, extra, w_in8[f], w_out4[f], tf, tf // 4)
            dw_out = _ffn_dw(f"ffn_dwout_{tag}", act, du, tw).reshape(N_DEV, FF_BLK // 2, d)
            if k == 0:
                d_pool_w = pool_grads[0].reshape(4, N_DEV, POOL_SHARD, POOL_GROUP).transpose(1, 0, 2, 3).reshape(N_DEV, -1, POOL_GROUP)
                token = start_scatter(tag + "_out", [dw_out, d_pool_w])
                token = start_scatter(tag + "_in", [_ffn_dw(f"ffn_dwin_{tag}", dgu, h, tw, after=token)])
            else:
                token = start_scatter(tag, [_ffn_dw(f"ffn_dwin_{tag}", dgu, h, tw), dw_out])
        elif i == 0:
            dh, dw4 = _pool_bwd(du, extra, w4, min(4 * tm, s))
            pool_grads = (dw4, *pool_reds)
        else:
            dh, mla_grads = _mla_bwd(du, h, extra, mla_wts, cos_k, sin_k, tm)
            dwq = mla_grads["wq"]
            d_uq = jnp.concatenate([dwq[:, :, :QK_NOPE], dwq[:, :, KV_LORA:KV_LORA + QK_ROPE]], axis=2).transpose(1, 0, 2)
            token = start_scatter("mla", [mla_grads["w_in"][:, :mla_w_in.shape[2]].reshape(N_DEV, d // N_DEV, -1),
                                          d_uq.reshape(N_DEV, Q_LORA // N_DEV, -1), mla_grads["w_o"].reshape(N_DEV, d // N_DEV, d)])
            dwukp, dwuv2 = mla_grads["wukp"], mla_grads["wuv2"]
            d_uk = dwukp[:, :QK_NOPE, :KV_LORA].transpose(2, 0, 1).reshape(KV_LORA, -1)
            d_uv = jnp.where(even, dwuv2[:, :, :V_HEAD], dwuv2[:, :, V_HEAD:]).transpose(1, 0, 2).reshape(KV_LORA, -1)
            state_ukv, token = _split_start("gather_start_ukv", "gather", [d_uk, d_uv], after=token)
        if k > 0:
            dx, du, reds = _norm_link_bwd(f"norm_link_bwd_{k}", dh, xin, dx, saved[k - 1][2], pre_rows(k, False), post_rows(k - 1),
                                          weights_of[k - 1], BF16, tr, after=token, u_scale=pscale_row if k == 2 else None)
            d_mod[3 * (k - 1) + 2], d_g[2 * (k - 1) + 1] = reds[3], reds[4]
            if k == 2:
                pool_reds = reds[5:]
        else:
            dx, reds = _prenorm_bwd("prenorm_bwd_0", dh, xin, dx, pre_rows(0, False), tr, after=token)
        d_mod[3 * k], d_mod[3 * k + 1], d_g[2 * k] = reds[0], reds[1], reds[2]
    grad_x = dx.reshape(x.shape)

    def upd(name, parts, w, m, v):
        shape = w.shape
        r, cdim = parts.shape[1], parts.shape[2]
        return [o.reshape(shape) for o in _adamw(name, parts, w.reshape(r, cdim), m.reshape(r, cdim), v.reshape(r, cdim))]

    def landed(key, after):
        return _split_wait(f"scatter_wait_{key}", sent[key], after)

    res = {}
    w_in_s, m_in_s, v_in_s = (a.reshape(4, FF_BLK, d) for a in (w_in_t, m_in_t, v_in_t))
    w_out_s, m_out_s, v_out_s = (a.reshape(4, FF_BLK // 2, d) for a in (ffn_w_out, m_ffn_w_out, v_ffn_w_out))
    bufs_in = [lax.empty(w_in_s.shape, F32) for _ in range(4)]
    bufs_out = [lax.empty(w_out_s.shape, F32) for _ in range(4)]
    for key, k in (("12", 3), ("mla", None), ("10", 2), ("02", 1)):
        parts = landed(key, grad_x)
        if k is None:
            res["mla_w_in"] = upd("adam_mla_w_in", parts[0], mla_w_in, m_mla_w_in, v_mla_w_in)
            res["mla_w_uq"] = upd("adam_mla_w_uq", parts[1], mla_w_uq, m_mla_w_uq, v_mla_w_uq)
            res["mla_w_o"] = upd("adam_mla_w_o", parts[2], mla_w_o, m_mla_w_o, v_mla_w_o)
            uk_all, uv_all = _split_wait("gather_wait_ukv", state_ukv, grad_x)
            res["mla_w_uk"] = upd("adam_mla_w_uk", uk_all, mla_w_uk, m_mla_w_uk, v_mla_w_uk)
            res["mla_w_uv"] = upd("adam_mla_w_uv", uv_all, mla_w_uv, m_mla_w_uv, v_mla_w_uv)
            continue
        bufs_in = _adamw_slab(f"adam_ffn_w_in_{key}", parts[0], w_in_s, m_in_s, v_in_s, bufs_in, k)
        bufs_out = _adamw_slab(f"adam_ffn_w_out_{key}", parts[1], w_out_s, m_out_s, v_out_s, bufs_out, k)

    dw4, dpscale, dpb = pool_grads
    small_g = jnp.concatenate(d_mod + d_g + [dpb, dpscale, mla_grads["q_norm"], mla_grads["kv_norm"], loss_part], axis=1)
    done = [bufs_in[0], bufs_out[0], res["mla_w_o"][0], res["mla_w_uv"][0]]
    (small_g_all,) = _all_gather("gather_small_grads", [small_g], after=sum(a.reshape(-1)[:1] for a in done))
    small_g_all = small_g_all.reshape(N_DEV, -1)
    n_m = 2 * n_mod * d
    d_mod_all = small_g_all[:, :n_m].reshape(N_DEV, 2, n_mod * d)
    rest = small_g_all[:, n_m:]
    p_norm_g = lax.dynamic_slice_in_dim(rest[:, :12 * d].reshape(N_DEV, 12, d), me * (d // N_DEV), d // N_DEV, axis=2)
    p_pool_b = lax.dynamic_slice_in_dim(rest[:, 12 * d:13 * d].reshape(N_DEV, 4, POOL_GROUP), me * POOL_SHARD, POOL_SHARD, axis=2)
    p_pool_scale = rest[:, 13 * d:14 * d].reshape(N_DEV, 1, d)
    p_q_norm = lax.dynamic_slice_in_dim(rest[:, 14 * d:14 * d + Q_LORA], me * Q_SHARD, Q_SHARD, axis=1).reshape(N_DEV, 1, Q_SHARD)
    p_kv_norm = rest[:, 14 * d + Q_LORA:14 * d + Q_LORA + KV_LORA].reshape(N_DEV, 1, KV_LORA)
    loss = jnp.sum(rest[:, -1])

    d_mod_loc = lax.dynamic_slice_in_dim(d_mod_all, me * mod_blk, mod_blk, axis=2).transpose(1, 0, 2)
    k_pad = 128
    sc_t = jnp.pad(sc_all.T, ((0, 0), (0, k_pad - N_DEV)))
    d_ada_w = _matmul("ada_dw", sc_t, jnp.pad(d_mod_loc, ((0, 0), (0, k_pad - N_DEV), (0, 0))), a_blk=(d, k_pad),
                      a_map=lambda i, k: (0, 0), b_blk=(None, k_pad, mod_blk), b_map=lambda i, k: (i, 0, 0), o_shape=(2, d, mod_blk),
                      o_blk=(None, d, mod_blk), o_map=lambda i, k: (i, 0, 0), grid=(2, 1), contract=NN, out_dtype=F32)
    res["ada_w"] = upd("adam_ada_w", d_ada_w.reshape(1, 2 * d, mod_blk), ada_w, m_ada_w, v_ada_w)
    res["ada_b"] = upd("adam_ada_b", d_mod_all.reshape(N_DEV, 2, n_mod * d), ada_b, m_ada_b, v_ada_b)
    res["norm_g"] = upd("adam_norm_g", p_norm_g, norm_g, m_norm_g, v_norm_g)
    res["pool_b"] = upd("adam_pool_b", p_pool_b, pool_b, m_pool_b, v_pool_b)
    res["pool_scale"] = upd("adam_pool_scale", p_pool_scale, pool_scale, m_pool_scale, v_pool_scale)
    res["mla_q_norm"] = upd("adam_mla_q_norm", p_q_norm, mla_q_norm, m_mla_q_norm, v_mla_q_norm)
    res["mla_kv_norm"] = upd("adam_mla_kv_norm", p_kv_norm, mla_kv_norm, m_mla_kv_norm, v_mla_kv_norm)

    p_out, p_pool_w = landed("00_out", res["ada_w"][1])
    bufs_out = _adamw_slab("adam_ffn_w_out_00", p_out, w_out_s, m_out_s, v_out_s, bufs_out, 0)
    res["pool_w"] = upd("adam_pool_w", p_pool_w, pool_w, m_pool_w, v_pool_w)
    (p_in,) = landed("00_in", res["pool_w"][1])
    bufs_in = _adamw_slab("adam_ffn_w_in_00", p_in, w_in_s, m_in_s, v_in_s, bufs_in, 0)
    res["ffn_w_in"] = [jnp.swapaxes(b.reshape(w_in_t.shape), 2, 3) for b in bufs_in]
    res["ffn_w_out"] = [b.reshape(ffn_w_out.shape) for b in bufs_out]

    order = ["ada_w", "ada_b", "norm_g", "ffn_w_in", "ffn_w_out", "pool_w", "pool_b", "pool_scale", "mla_w_in", "mla_q_norm",
             "mla_kv_norm", "mla_w_uq", "mla_w_uk", "mla_w_uv", "mla_w_o"]
    outs = [loss, grad_x]
    for j in range(4):
        outs += [res[name][j] for name in order]
    return tuple(outs)
```

```python
import jax
import jax.numpy as jnp
from jax import lax
from jax.experimental import pallas as pl
from jax.experimental.pallas import tpu as pltpu

F32 = jnp.float32
BF16 = jnp.bfloat16
N_DEV = 8
MESH = pl.DeviceIdType.MESH

D_MODEL = 1024
N_HEADS = 16
QK_NOPE = 64
QK_ROPE = 32
V_HEAD = 64
Q_LORA = 256
KV_LORA = 128
LAT_PAD = 512
QK_PAD = 256
ONE_COL = 160
D_FF = 2816
FF_BLK = 2 * D_FF // N_DEV
POOL_WINDOWS = (2, 4, 8, 16)
POOL_GROUP = 256
POOL_SHARD = POOL_GROUP // N_DEV
Q_SHARD = Q_LORA // N_DEV
ROPE_THETA = 10000.0
EPS = 1e-6
ATTN_SCALE = (QK_NOPE + QK_ROPE) ** -0.5
LOG2_E = 1.4426950408889634
ADAM_LR, ADAM_B1, ADAM_B2, ADAM_EPS, ADAM_WD, ADAM_STEP = 0.001, 0.9, 0.999, 1e-08, 0.01, 10
VMEM_LIMIT = 56 * 1024 * 1024

NN = ((1,), (0,))
NT = ((1,), (1,))
TN = ((0,), (0,))


def _params(**kw):
    return pltpu.CompilerParams(vmem_limit_bytes=VMEM_LIMIT, **kw)


def _dot(a, b, contract):
    return lax.dot_general(a, b, (contract, ((), ())), preferred_element_type=F32)


def _matmul(name, a, b, *, a_blk, a_map, b_blk, b_map, o_shape, o_blk, o_map, grid, contract, out_dtype,
            bias=None, bias_blk=None, bias_map=None, after=None):
    n_k = grid[-1]
    k_axis = len(grid) - 1
    acc_shape = tuple(d for d in o_blk if d is not None)

    def body(*refs):
        a_ref, b_ref = refs[:2]
        bias_ref = refs[2] if bias is not None else None
        if n_k == 1:
            r = _dot(a_ref[...].astype(BF16), b_ref[...].astype(BF16), contract)
            if bias is not None:
                r = r + bias_ref[...]
            refs[-1][...] = r.astype(refs[-1].dtype)
            return
        o_ref, acc = refs[-2:]
        k = pl.program_id(k_axis)

        @pl.when(k == 0)
        def _():
            acc[...] = jnp.zeros_like(acc)

        acc[...] += _dot(a_ref[...].astype(BF16), b_ref[...].astype(BF16), contract)

        @pl.when(k == n_k - 1)
        def _():
            r = acc[...]
            if bias is not None:
                r = r + bias_ref[...]
            o_ref[...] = r.astype(o_ref.dtype)

    in_specs = [pl.BlockSpec(a_blk, a_map), pl.BlockSpec(b_blk, b_map)]
    args = [a, b]
    if bias is not None:
        in_specs.append(pl.BlockSpec(bias_blk, bias_map))
        args.append(bias)
    if after is not None:
        in_specs.append(pl.BlockSpec(memory_space=pl.ANY))
        args.append(after)
    return pl.pallas_call(
        body, name=name, grid=grid, in_specs=in_specs, out_specs=pl.BlockSpec(o_blk, o_map),
        out_shape=jax.ShapeDtypeStruct(o_shape, out_dtype), scratch_shapes=[pltpu.VMEM(acc_shape, F32)] if n_k > 1 else [],
        compiler_params=_params(),
    )(*args)


def _rowmap(name, fn, ins, outs, reds, grid, after=None):
    n_in, n_out, n_red = len(ins), len(outs), len(reds)
    extra = [] if after is None else [after]

    def body(*refs):
        in_refs = refs[:n_in]
        out_refs = refs[n_in + len(extra):n_in + len(extra) + n_out]
        red_refs = refs[n_in + len(extra) + n_out:]
        out_vals, red_vals = fn(*[r[...] for r in in_refs])
        for r, v in zip(out_refs, out_vals):
            r[...] = v.astype(r.dtype)
        if n_red:
            first = pl.program_id(0) == 0
            for ax in range(1, len(grid)):
                first = jnp.logical_and(first, pl.program_id(ax) == 0)

            @pl.when(first)
            def _():
                for r in red_refs:
                    r[...] = jnp.zeros_like(r)

            for r, v in zip(red_refs, red_vals):
                r[...] += v

    res = pl.pallas_call(
        body, name=name, grid=grid,
        in_specs=[pl.BlockSpec(blk, imap) for _, blk, imap in ins] + [pl.BlockSpec(memory_space=pl.ANY)] * len(extra),
        out_specs=[pl.BlockSpec(blk, imap) for _, blk, imap in list(outs) + list(reds)],
        out_shape=[sds for sds, _, _ in list(outs) + list(reds)],
        compiler_params=_params(),
    )(*[a for a, _, _ in ins], *extra)
    return res[:n_out], res[n_out:]


def _sds(shape, dtype):
    return jax.ShapeDtypeStruct(shape, dtype)


def _tile(a, tm):
    return (a, (tm, a.shape[1]), lambda i: (i, 0))


def _row(a):
    return (a, (1, a.shape[1]), lambda i: (0, 0))


def _otile(n, c, dtype, tm):
    return (_sds((n, c), dtype), (tm, c), lambda i: (i, 0))


def _ored(c):
    return (_sds((1, c), F32), (1, c), lambda i: (0, 0))


def _colsum(v):
    return jnp.sum(v, axis=0, keepdims=True)


def _rstd(v):
    return lax.rsqrt(jnp.mean(v * v, axis=-1, keepdims=True) + EPS)


def _pre(xv, g, sc, sh):
    return xv * _rstd(xv) * g * (1.0 + sc) + sh


def _post(xv, uv, g, gt, weight):
    uv = uv.astype(F32)
    return xv + weight * (1.0 + gt) * (uv * _rstd(uv) * g)


def _post_bwd(dv, uv, g, gt, weight):
    uv = uv.astype(F32)
    r = _rstd(uv)
    un = uv * r
    dy = dv * (weight * (1.0 + gt))
    a = dy * g
    du = r * (a - un * jnp.mean(a * un, axis=-1, keepdims=True))
    return du, (_colsum(dv * (weight * (un * g))), _colsum(dy * un))


def _pre_bwd(dhv, xv, dv, g, sc):
    dhv = dhv.astype(F32)
    r = _rstd(xv)
    xn = xv * r
    b = dhv * (g * (1.0 + sc))
    dx = dv + r * (b - xn * jnp.mean(b * xn, axis=-1, keepdims=True))
    return dx, (_colsum(dhv), _colsum(dhv * (xn * g)), _colsum(dhv * ((1.0 + sc) * xn)))


def _rowk(rows, k):
    return (rows, (None, 1, rows.shape[2]), lambda i: (k, 0, 0))


def _prenorm(name, x, pre, out_dtype, tm):
    n, d = x.shape
    (h,), _ = _rowmap(name, lambda xv, g, sc, sh: ((_pre(xv, g, sc, sh),), ()), [_tile(x, tm), *pre], [_otile(n, d, out_dtype, tm)],
                      [], (n // tm,))
    return h


def _norm_link(name, x, u, post, weight, pre, out_dtype, tm, u_scale=None):
    n, d = x.shape
    rows = [] if u_scale is None else [_row(u_scale)]

    def fn(xv, uv, g, gt, g2, sc, sh, *ps):
        xn = _post(xv, uv * ps[0] if ps else uv, g, gt, weight)
        return (xn, _pre(xn, g2, sc, sh)), ()

    (xn, h), _ = _rowmap(name, fn, [_tile(x, tm), _tile(u, tm), *post, *pre, *rows],
                         [_otile(n, d, F32, tm), _otile(n, d, out_dtype, tm)], [], (n // tm,))
    return xn, h


def _norm_loss(name, x, u, target, post, weight, tm):
    n, d = x.shape

    def fn(xv, uv, tv, g, gt):
        e = _post(xv, uv, g, gt, weight) - tv
        dv = e * (1.0 / d)
        du, reds = _post_bwd(dv, uv, g, gt, weight)
        return (dv, du), (_colsum(e * e), *reds)

    (dx, du), reds = _rowmap(name, fn, [_tile(x, tm), _tile(u, tm), _tile(target, tm), *post],
                             [_otile(n, d, F32, tm), _otile(n, d, BF16, tm)], [_ored(d)] * 3, (n // tm,))
    return dx, du, reds


def _norm_link_bwd(name, dh, x, dout, u_prev, pre, post_prev, weight_prev, out_dtype, tm, after=None, u_scale=None):
    n, d = x.shape
    rows = [] if u_scale is None else [_row(u_scale)]

    def fn(dhv, xv, dv, uv, g, sc, g2, gt, *ps):
        dx, reds = _pre_bwd(dhv, xv, dv, g, sc)
        du, reds_prev = _post_bwd(dx, uv * ps[0] if ps else uv, g2, gt, weight_prev)
        if not ps:
            return (dx, du), (*reds, *reds_prev)
        d_unscaled = du * ps[0]
        return (dx, d_unscaled), (*reds, *reds_prev, _colsum(du * uv), _colsum(d_unscaled))

    (dx, du), reds = _rowmap(name, fn, [_tile(dh, tm), _tile(x, tm), _tile(dout, tm), _tile(u_prev, tm), *pre, *post_prev, *rows],
                             [_otile(n, d, F32, tm), _otile(n, d, out_dtype, tm)], [_ored(d)] * (5 + 2 * len(rows)), (n // tm,),
                             after=after)
    return dx, du, reds


def _prenorm_bwd(name, dh, x, dout, pre, tm, after=None):
    n, d = x.shape

    def fn(dhv, xv, dv, g, sc):
        dx, reds = _pre_bwd(dhv, xv, dv, g, sc)
        return (dx,), reds

    (dx,), reds = _rowmap(name, fn, [_tile(dh, tm), _tile(x, tm), _tile(dout, tm), *pre], [_otile(n, d, F32, tm)], [_ored(d)] * 3,
                          (n // tm,), after=after)
    return dx, reds


def _ffn_fwd(tag, h, w_in8, w_out4, tm, sub, after=None):
    s, d = h.shape

    extra = [] if after is None else [after]

    def body(h_ref, wg_ref, wu_ref, wo_ref, *rest):
        u_ref, gu_ref, acc = rest[-3:]
        j = pl.program_id(1)

        @pl.when(j == 0)
        def _():
            acc[...] = jnp.zeros_like(acc)

        for r in range(tm // sub):
            rows = pl.ds(r * sub, sub)
            hv = h_ref[rows, :]
            gate = _dot(hv, wg_ref[...], NT)
            up = _dot(hv, wu_ref[...], NT)
            gu_ref[0, rows, :] = gate.astype(BF16)
            gu_ref[1, rows, :] = up.astype(BF16)
            acc[rows, :] += _dot((gate * jax.nn.sigmoid(gate) * up).astype(BF16), wo_ref[...], NN)

        @pl.when(j == w_out4.shape[0] - 1)
        def _():
            u_ref[...] = acc[...].astype(u_ref.dtype)

    w_blk = (None, FF_BLK, d)
    return pl.pallas_call(
        body, name=f"ffn_fwd_{tag}", grid=(s // tm, 4),
        in_specs=[pl.BlockSpec((tm, d), lambda i, j: (i, 0)), pl.BlockSpec(w_blk, lambda i, j: (j, 0, 0)),
                  pl.BlockSpec(w_blk, lambda i, j: (j + 4, 0, 0)), pl.BlockSpec((None, FF_BLK, d), lambda i, j: (j, 0, 0))]
        + [pl.BlockSpec(memory_space=pl.ANY)] * len(extra),
        out_specs=[pl.BlockSpec((tm, d), lambda i, j: (i, 0)), pl.BlockSpec((2, None, tm, FF_BLK), lambda i, j: (0, j, i, 0))],
        out_shape=[_sds((s, d), BF16), _sds((2, 4, s, FF_BLK), BF16)], scratch_shapes=[pltpu.VMEM((tm, d), F32)],
        compiler_params=_params(),
    )(h, w_in8, w_in8, w_out4, *extra)


def _ffn_bwd_act(tag, du, gu, w_in8, w_out4, tm, sub):
    s, d = du.shape

    def body(du_ref, gu_ref, wg_ref, wu_ref, wo_ref, dh_ref, dgu_ref, act_ref, acc):
        j = pl.program_id(1)

        @pl.when(j == 0)
        def _():
            acc[...] = jnp.zeros_like(acc)

        n_sub = tm // sub
        dact_next = _dot(du_ref[pl.ds(0, sub), :], wo_ref[...], NT)
        for r in range(n_sub):
            rows = pl.ds(r * sub, sub)
            dact = dact_next
            if r + 1 < n_sub:
                dact_next = _dot(du_ref[pl.ds((r + 1) * sub, sub), :], wo_ref[...], NT)
            gate, up = gu_ref[0, rows, :].astype(F32), gu_ref[1, rows, :].astype(F32)
            sg = jax.nn.sigmoid(gate)
            silu = gate * sg
            dg = dact * up * (sg * (1.0 + gate * (1.0 - sg)))
            dup = dact * silu
            dgu_ref[0, :, rows] = dg.T.astype(BF16)
            dgu_ref[1, :, rows] = dup.T.astype(BF16)
            act_ref[:, rows] = (silu * up).T.astype(BF16)
            acc[rows, :] += _dot(dg.astype(BF16), wg_ref[...], NN) + _dot(dup.astype(BF16), wu_ref[...], NN)

        @pl.when(j == w_out4.shape[0] - 1)
        def _():
            dh_ref[...] = acc[...].astype(dh_ref.dtype)

    w_blk = (None, FF_BLK, d)
    dh, dgu_t, act_t = pl.pallas_call(
        body, name=f"ffn_bwd_{tag}", grid=(s // tm, 4),
        in_specs=[pl.BlockSpec((tm, d), lambda i, j: (i, 0)), pl.BlockSpec((2, None, tm, FF_BLK), lambda i, j: (0, j, i, 0)),
                  pl.BlockSpec(w_blk, lambda i, j: (j, 0, 0)), pl.BlockSpec(w_blk, lambda i, j: (j + 4, 0, 0)),
                  pl.BlockSpec((None, FF_BLK, d), lambda i, j: (j, 0, 0))],
        out_specs=[pl.BlockSpec((tm, d), lambda i, j: (i, 0)), pl.BlockSpec((2, None, FF_BLK, tm), lambda i, j: (0, j, 0, i)),
                   pl.BlockSpec((None, FF_BLK, tm), lambda i, j: (j, 0, i))],
        out_shape=[_sds((s, d), BF16), _sds((2, 4, FF_BLK, s), BF16), _sds((4, FF_BLK, s), BF16)],
        scratch_shapes=[pltpu.VMEM((tm, d), F32)], compiler_params=_params(),
    )(du, gu, w_in8, w_in8, w_out4)
    return dh, dgu_t.reshape(8, FF_BLK, s), act_t


def _ffn_dw(name, lhs_t, rhs, tk, after=None):
    n_g, _, s = lhs_t.shape
    d = rhs.shape[1]
    return _matmul(name, lhs_t, rhs, a_blk=(None, FF_BLK, tk), a_map=lambda g, k: (g, 0, k), b_blk=(tk, d), b_map=lambda g, k: (k, 0),
                   o_shape=(n_g, FF_BLK, d), o_blk=(None, FF_BLK, d), o_map=lambda g, k: (g, 0, 0), grid=(n_g, s // tk),
                   contract=NN, out_dtype=BF16, after=after)


def _window_sum(x, window, transpose):
    s = x.shape[0]
    t = lax.broadcasted_iota(jnp.int32, (s, 1), 0)
    half = window // 2
    cnt = jnp.minimum(t + half, s) - jnp.maximum(t - half, 0)
    inv = 1.0 / cnt.astype(F32)
    if transpose:
        x = x * inv
        offsets = range(-half + 1, half + 1)
    else:
        offsets = range(-half, half)
    acc = jnp.zeros_like(x)
    for o in offsets:
        shifted = x if o == 0 else pltpu.roll(x, (-o) % s, 0)
        valid = jnp.logical_and(t + o >= 0, t + o < s)
        acc = acc + jnp.where(valid, shifted, 0.0)
    return acc if transpose else acc * inv


def _pool_mix(name, x, transpose, out_dtype):
    s, d = x.shape

    def body(x_ref, o_ref):
        g = pl.program_id(0)
        for gi, window in enumerate(POOL_WINDOWS):
            @pl.when(g == gi)
            def _(window=window):
                xv = x_ref[...].astype(F32)
                o_ref[...] = (_window_sum(xv, window, transpose) - xv).astype(o_ref.dtype)

    return pl.pallas_call(
        body, name=name, grid=(len(POOL_WINDOWS),), in_specs=[pl.BlockSpec((s, POOL_GROUP), lambda g: (0, g))],
        out_specs=pl.BlockSpec((s, POOL_GROUP), lambda g: (0, g)), out_shape=_sds((s, d), out_dtype), compiler_params=_params(),
    )(x)


def _pool_fwd(h, w4, bias, tm):
    s, d = h.shape
    nt = s // tm
    z = _pool_mix("pool_mix", h, False, BF16)
    v = _matmul("pool_proj", z, w4, a_blk=(tm, POOL_GROUP), a_map=lambda i, g, k: (i, g), b_blk=(None, POOL_GROUP, POOL_GROUP),
                b_map=lambda i, g, k: (g, 0, 0), o_shape=(s, d), o_blk=(tm, POOL_GROUP), o_map=lambda i, g, k: (i, g),
                grid=(nt, 4, 1), contract=NN, out_dtype=F32, bias=bias, bias_blk=(1, POOL_GROUP), bias_map=lambda i, g, k: (0, g))
    return v, z


def _pool_bwd(dv, z, w4, tm):
    s, d = dv.shape
    nt = s // tm
    dw4 = _matmul("pool_dw", z, dv, a_blk=(tm, POOL_GROUP), a_map=lambda g, k: (k, g), b_blk=(tm, POOL_GROUP),
                  b_map=lambda g, k: (k, g), o_shape=(4, POOL_GROUP, POOL_GROUP), o_blk=(None, POOL_GROUP, POOL_GROUP),
                  o_map=lambda g, k: (g, 0, 0), grid=(4, nt), contract=TN, out_dtype=F32)
    dz = _matmul("pool_dz", dv, w4, a_blk=(tm, POOL_GROUP), a_map=lambda i, g, k: (i, g), b_blk=(None, POOL_GROUP, POOL_GROUP),
                 b_map=lambda i, g, k: (g, 0, 0), o_shape=(s, d), o_blk=(tm, POOL_GROUP), o_map=lambda i, g, k: (i, g),
                 grid=(nt, 4, 1), contract=NT, out_dtype=F32)
    dh = _pool_mix("pool_mix_t", dz, True, BF16)
    return dh, dw4


def _lane(shape):
    return lax.broadcasted_iota(jnp.int32, shape, 1)


def _rope_swap(v, transpose):
    half = QK_ROPE // 2
    lane = _lane(v.shape)
    up = pltpu.roll(v, v.shape[1] - half, 1)
    down = pltpu.roll(v, half, 1)
    if transpose:
        return jnp.where(lane < half, up, jnp.where(lane < QK_ROPE, -down, 0.0))
    return jnp.where(lane < half, -up, jnp.where(lane < QK_ROPE, down, 0.0))


def _rope(v, cos, sin):
    return v * cos + _rope_swap(v, False) * sin


def _rope_t(g, cos, sin):
    return g * cos + _rope_swap(g * sin, True)


def _mla_mid(lat, q_norm, kv_norm, cos_k, sin_k, tm):
    s = lat.shape[0]

    def fn(lv, qn, kn, cs, sn):
        cq = lv[:, :Q_LORA]
        ckv = lv[:, Q_LORA:Q_LORA + KV_LORA]
        kr = lv[:, Q_LORA + KV_LORA:]
        cq = cq * _rstd(cq) * qn
        ckv = ckv * _rstd(ckv) * kn
        k_rope = jnp.where(_lane(kr.shape) == ONE_COL - KV_LORA, 1.0, _rope(kr, cs, sn))
        return (cq, jnp.concatenate([ckv, k_rope], axis=1)), ()

    (cq, kcat), _ = _rowmap("mla_mid", fn, [_tile(lat, tm), _row(q_norm), _row(kv_norm), _tile(cos_k, tm), _tile(sin_k, tm)],
                            [_otile(s, Q_LORA, BF16, tm), _otile(s, QK_PAD, BF16, tm)], [], (s // tm,))
    return cq, kcat


def _mla_mid_bwd(lat, dcq, dkcat, dv, q_norm, kv_norm, cos_k, sin_k, tm):
    s = lat.shape[0]

    def fn(lv, dq, dk, dvv, qn, kn, cs, sn):
        dk = dk * (1.0 / LOG2_E)
        cq = lv[:, :Q_LORA]
        ckv = lv[:, Q_LORA:Q_LORA + KV_LORA]
        rq, rk = _rstd(cq), _rstd(ckv)
        cqn, ckn = cq * rq, ckv * rk
        a = dq * qn
        d_cq = rq * (a - cqn * jnp.mean(a * cqn, axis=-1, keepdims=True))
        dckv = dk[:, :KV_LORA] + dvv
        a2 = dckv * kn
        d_ckv = rk * (a2 - ckn * jnp.mean(a2 * ckn, axis=-1, keepdims=True))
        d_kr = _rope_t(dk[:, KV_LORA:], cs, sn)
        return (jnp.concatenate([d_cq, d_ckv, d_kr], axis=1),), (_colsum(dq * cqn), _colsum(dckv * ckn))

    (dlat,), reds = _rowmap(
        "mla_mid_bwd", fn,
        [_tile(lat, tm), _tile(dcq, tm), _tile(dkcat, tm), _tile(dv, tm), _row(q_norm), _row(kv_norm), _tile(cos_k, tm),
         _tile(sin_k, tm)],
        [_otile(s, LAT_PAD, BF16, tm)], [_ored(Q_LORA), _ored(KV_LORA)], (s // tm,))
    return dlat, reds


def _mla_q(cq, wq, wukp, cos_k, sin_k, tm):
    s = cq.shape[0]

    def body(cq_ref, wq_ref, wuk_ref, cos_ref, sin_ref, o_ref):
        cqv, cs, sn = cq_ref[...], cos_ref[...], sin_ref[...]
        for h in range(N_HEADS):
            aq = _dot(cqv, wq_ref[h], NN)
            qlat = _dot(aq.astype(BF16), wuk_ref[h], NN)
            roped = _rope(aq[:, KV_LORA:], cs, sn)
            o_ref[h] = (jnp.concatenate([qlat[:, :KV_LORA], roped], axis=1) * (ATTN_SCALE * LOG2_E)).astype(o_ref.dtype)

    wblk = pl.BlockSpec((N_HEADS, QK_PAD, QK_PAD), lambda i: (0, 0, 0))
    tblk = pl.BlockSpec((tm, KV_LORA), lambda i: (i, 0))
    return pl.pallas_call(
        body, name="mla_q", grid=(s // tm,),
        in_specs=[pl.BlockSpec((tm, Q_LORA), lambda i: (i, 0)), wblk, wblk, tblk, tblk],
        out_specs=pl.BlockSpec((N_HEADS, tm, QK_PAD), lambda i: (0, i, 0)), out_shape=_sds((N_HEADS, s, QK_PAD), BF16),
        compiler_params=_params(),
    )(cq, wq, wukp, cos_k, sin_k)


def _mla_q_bwd(cq, wq, wukp, cos_k, sin_k, dqcat, tm):
    s = cq.shape[0]

    def body(cq_ref, wq_ref, wuk_ref, cos_ref, sin_ref, dq_ref, dcq_ref, dwq_ref, dwuk_ref):
        @pl.when(pl.program_id(0) == 0)
        def _():
            dwq_ref[...] = jnp.zeros_like(dwq_ref)
            dwuk_ref[...] = jnp.zeros_like(dwuk_ref)

        cqv, cs, sn = cq_ref[...], cos_ref[...], sin_ref[...]
        d_cq = jnp.zeros((tm, Q_LORA), F32)
        for h in range(N_HEADS):
            aq = _dot(cqv, wq_ref[h], NN).astype(BF16)
            g = dq_ref[h].astype(F32) * ATTN_SCALE
            gl, gr = g[:, :KV_LORA], g[:, KV_LORA:]
            dqlat = jnp.concatenate([gl, jnp.zeros_like(gl)], axis=1).astype(BF16)
            d_rope = _rope_t(gr, cs, sn)
            daq = _dot(dqlat, wuk_ref[h], NT) + jnp.concatenate([jnp.zeros_like(d_rope), d_rope], axis=1)
            daq_b = daq.astype(BF16)
            dwuk_ref[h] += _dot(aq, dqlat, TN)
            dwq_ref[h] += _dot(cqv, daq_b, TN)
            d_cq = d_cq + _dot(daq_b, wq_ref[h], NT)
        dcq_ref[...] = d_cq

    wblk = pl.BlockSpec((N_HEADS, QK_PAD, QK_PAD), lambda i: (0, 0, 0))
    tblk = pl.BlockSpec((tm, KV_LORA), lambda i: (i, 0))
    return pl.pallas_call(
        body, name="mla_q_bwd", grid=(s // tm,),
        in_specs=[pl.BlockSpec((tm, Q_LORA), lambda i: (i, 0)), wblk, wblk, tblk, tblk,
                  pl.BlockSpec((N_HEADS, tm, QK_PAD), lambda i: (0, i, 0))],
        out_specs=[pl.BlockSpec((tm, Q_LORA), lambda i: (i, 0)), wblk, wblk],
        out_shape=[_sds((s, Q_LORA), F32), _sds((N_HEADS, QK_PAD, QK_PAD), F32), _sds((N_HEADS, QK_PAD, QK_PAD), F32)],
        compiler_params=_params(),
    )(cq, wq, wukp, cos_k, sin_k, dqcat)


def _flash_fwd(qcat, kcat, tq, tk):
    n_h, s, _ = qcat.shape
    n_k = s // tk

    def body(q_ref, k_ref, o_ref, lse_ref):
        q = q_ref[...]
        m = jnp.full((tq, 1), -1e30, F32)
        acc = jnp.zeros((tq, QK_PAD), F32)
        for kk in range(n_k):
            k = k_ref[pl.ds(kk * tk, tk), :]
            sc = _dot(q, k, NT)
            m_new = jnp.maximum(m, jnp.max(sc, axis=1, keepdims=True))
            p = jnp.exp2(sc - m_new).astype(BF16)
            acc = jnp.exp2(m - m_new) * acc + _dot(p, k, NN)
            m = m_new
        l = jnp.sum(jnp.where(_lane(acc.shape) == ONE_COL, acc, 0.0), axis=1, keepdims=True)
        o_ref[...] = (acc[:, :KV_LORA] / l).astype(o_ref.dtype)
        lse_ref[...] = m + jnp.log2(l)

    return pl.pallas_call(
        body, name="mla_attn", grid=(n_h, s // tq),
        in_specs=[pl.BlockSpec((None, tq, QK_PAD), lambda h, i: (h, i, 0)), pl.BlockSpec((s, QK_PAD), lambda h, i: (0, 0))],
        out_specs=[pl.BlockSpec((None, tq, KV_LORA), lambda h, i: (h, i, 0)), pl.BlockSpec((None, tq, 1), lambda h, i: (h, i, 0))],
        out_shape=[_sds((n_h, s, KV_LORA), BF16), _sds((n_h, s, 1), F32)], compiler_params=_params(),
    )(qcat, kcat)


def _flash_bwd(qcat, kcat, o, do, lse, tq, tk):
    n_h, s, _ = qcat.shape
    n_k = s // tk

    def body(q_ref, k_ref, v_ref, o_ref, do_ref, lse_ref, dq_ref, dk_ref, dv_ref, dq_acc):
        h, i = pl.program_id(0), pl.program_id(1)

        @pl.when(jnp.logical_and(h == 0, i == 0))
        def _():
            dk_ref[...] = jnp.zeros_like(dk_ref)
            dv_ref[...] = jnp.zeros_like(dv_ref)

        q = q_ref[...]
        dov = do_ref[...]
        dov_t = dov.T
        q_t = q.T
        lse_v = lse_ref[...]
        delta = jnp.sum(dov.astype(F32) * o_ref[...].astype(F32), axis=1, keepdims=True)
        dq_acc[...] = jnp.zeros_like(dq_acc)

        for kk in range(n_k):
            rows = pl.ds(kk * tk, tk)
            k = k_ref[rows, :]
            p = jnp.exp2(_dot(q, k, NT) - lse_v)
            dp = _dot(dov, v_ref[rows, :], NT)
            ds = (p * (dp - delta)).astype(BF16)
            dq_acc[...] += _dot(ds, k, NN)
            dv_ref[:, rows] += _dot(dov_t, p.astype(BF16), NN)
            dk_ref[:, rows] += _dot(q_t, ds, NN)
        dq_ref[...] = dq_acc[...].astype(dq_ref.dtype)

    qblk = pl.BlockSpec((None, tq, QK_PAD), lambda h, i: (h, i, 0))
    oblk = pl.BlockSpec((None, tq, KV_LORA), lambda h, i: (h, i, 0))
    return pl.pallas_call(
        body, name="mla_attn_bwd", grid=(n_h, s // tq),
        in_specs=[qblk, pl.BlockSpec((s, QK_PAD), lambda h, i: (0, 0)), pl.BlockSpec((s, KV_LORA), lambda h, i: (0, 0)), oblk, oblk,
                  pl.BlockSpec((None, tq, 1), lambda h, i: (h, i, 0))],
        out_specs=[qblk, pl.BlockSpec((QK_PAD, s), lambda h, i: (0, 0)), pl.BlockSpec((KV_LORA, s), lambda h, i: (0, 0))],
        out_shape=[_sds((n_h, s, QK_PAD), BF16), _sds((QK_PAD, s), F32), _sds((KV_LORA, s), F32)],
        scratch_shapes=[pltpu.VMEM((tq, QK_PAD), F32)], compiler_params=_params(),
    )(qcat, kcat, kcat, o, do, lse)


def _mla_uv(o_lat, wuv2, do, tm):
    n_h, s, _ = o_lat.shape
    d = n_h * V_HEAD
    pair = 2 * V_HEAD
    lat_blk = pl.BlockSpec((n_h, tm, KV_LORA), lambda i: (0, i, 0))
    w_blk = pl.BlockSpec((n_h, KV_LORA, pair), lambda i: (0, 0, 0))
    row_blk = pl.BlockSpec((tm, d), lambda i: (i, 0))

    if do is None:
        def body(a_ref, w_ref, o_ref):
            for p in range(n_h // 2):
                o_ref[:, p * pair:(p + 1) * pair] = (
                    _dot(a_ref[2 * p], w_ref[2 * p], NN) + _dot(a_ref[2 * p + 1], w_ref[2 * p + 1], NN)).astype(o_ref.dtype)

        return pl.pallas_call(body, name="mla_uv", grid=(s // tm,), in_specs=[lat_blk, w_blk], out_specs=row_blk,
                              out_shape=_sds((s, d), BF16), compiler_params=_params())(o_lat, wuv2)

    def body(a_ref, w_ref, do_ref, dlat_ref, dw_ref):
        @pl.when(pl.program_id(0) == 0)
        def _():
            dw_ref[...] = jnp.zeros_like(dw_ref)

        for h in range(n_h):
            dov = do_ref[:, (h // 2) * pair:(h // 2 + 1) * pair]
            dlat_ref[h] = _dot(dov, w_ref[h], NT).astype(dlat_ref.dtype)
            dw_ref[h] += _dot(a_ref[h], dov, TN)

    return pl.pallas_call(body, name="mla_uv_bwd", grid=(s // tm,), in_specs=[lat_blk, w_blk, row_blk], out_specs=[lat_blk, w_blk],
                          out_shape=[_sds((n_h, s, KV_LORA), BF16), _sds((n_h, KV_LORA, pair), F32)], compiler_params=_params(),
                          )(o_lat, wuv2, do)


def _mla_fwd(h, wts, cos_k, sin_k, tm):
    s, d = h.shape
    nt = s // tm
    lat = _matmul("mla_lat", h, wts["w_in"], a_blk=(tm, d), a_map=lambda i, k: (i, 0), b_blk=(d, LAT_PAD), b_map=lambda i, k: (0, 0),
                  o_shape=(s, LAT_PAD), o_blk=(tm, LAT_PAD), o_map=lambda i, k: (i, 0), grid=(nt, 1), contract=NN, out_dtype=F32)
    cq, kcat = _mla_mid(lat, wts["q_norm"], wts["kv_norm"], cos_k, sin_k, tm)
    qcat = _mla_q(cq, wts["wq"], wts["wukp"], cos_k, sin_k, tm)
    o_lat, lse = _flash_fwd(qcat, kcat, min(4 * tm, s), tm // 2)
    o = _mla_uv(o_lat, wts["wuv2"], None, tm)
    u = _matmul("mla_out", o, wts["w_o"], a_blk=(tm, d), a_map=lambda i, k: (i, 0), b_blk=(d, d), b_map=lambda i, k: (0, 0),
                o_shape=(s, d), o_blk=(tm, d), o_map=lambda i, k: (i, 0), grid=(nt, 1), contract=NN, out_dtype=BF16)
    return u, (lat, cq, kcat, qcat, o_lat, lse, o)


def _mla_bwd(du, h, saved, wts, cos_k, sin_k, tm):
    lat, cq, kcat, qcat, o_lat, lse, o = saved
    s, d = h.shape
    nt = s // tm
    do = _matmul("mla_do", du, wts["w_o"], a_blk=(tm, d), a_map=lambda i, k: (i, 0), b_blk=(d, d), b_map=lambda i, k: (0, 0),
                 o_shape=(s, d), o_blk=(tm, d), o_map=lambda i, k: (i, 0), grid=(nt, 1), contract=NT, out_dtype=BF16)
    dw_o = _matmul("mla_dwo", o, du, a_blk=(tm, d), a_map=lambda k: (k, 0), b_blk=(tm, d), b_map=lambda k: (k, 0),
                   o_shape=(d, d), o_blk=(d, d), o_map=lambda k: (0, 0), grid=(nt,), contract=TN, out_dtype=F32)
    do_lat, dwuv2 = _mla_uv(o_lat, wts["wuv2"], do, tm)
    dqcat, dkcat_t, dv_t = _flash_bwd(qcat, kcat, o_lat, do_lat, lse, min(2 * tm, s), tm)
    dkcat, dv = dkcat_t.T, dv_t.T
    dcq, dwq, dwukp = _mla_q_bwd(cq, wts["wq"], wts["wukp"], cos_k, sin_k, dqcat, tm)
    dlat, (dqn, dkn) = _mla_mid_bwd(lat, dcq, dkcat, dv, wts["q_norm"], wts["kv_norm"], cos_k, sin_k, tm)
    dh = _matmul("mla_dh", dlat, wts["w_in"], a_blk=(tm, LAT_PAD), a_map=lambda i, k: (i, 0), b_blk=(d, LAT_PAD),
                 b_map=lambda i, k: (0, 0), o_shape=(s, d), o_blk=(tm, d), o_map=lambda i, k: (i, 0), grid=(nt, 1), contract=NT,
                 out_dtype=BF16)
    dw_in = _matmul("mla_dwin", h, dlat, a_blk=(tm, d), a_map=lambda k: (k, 0), b_blk=(tm, LAT_PAD), b_map=lambda k: (k, 0),
                    o_shape=(d, LAT_PAD), o_blk=(d, LAT_PAD), o_map=lambda k: (0, 0), grid=(nt,), contract=TN, out_dtype=F32)
    return dh, dict(w_in=dw_in, wq=dwq, wukp=dwukp, wuv2=dwuv2, w_o=dw_o, q_norm=dqn, kv_norm=dkn)


def _adamw(name, parts, w, m, v):
    n_parts, r, c = parts.shape
    tr = r
    for cand in (256, 128, 64, 32, 16, 8):
        if r > cand and r % cand == 0:
            tr = cand
            break

    def body(p_ref, w_ref, m_ref, v_ref, g_ref, d_ref, nm_ref, nv_ref):
        g = p_ref[0].astype(F32)
        for k in range(1, n_parts):
            g = g + p_ref[k].astype(F32)
        nm = ADAM_B1 * m_ref[...] + (1.0 - ADAM_B1) * g
        nv = ADAM_B2 * v_ref[...] + (1.0 - ADAM_B2) * (g * g)
        m_hat = nm / (1.0 - ADAM_B1 ** ADAM_STEP)
        v_hat = nv / (1.0 - ADAM_B2 ** ADAM_STEP)
        g_ref[...] = g
        d_ref[...] = -ADAM_LR * (m_hat / (jnp.sqrt(v_hat) + ADAM_EPS) + ADAM_WD * w_ref[...])
        nm_ref[...] = nm
        nv_ref[...] = nv

    blk = pl.BlockSpec((tr, c), lambda i: (i, 0))
    return pl.pallas_call(
        body, name=name, grid=(r // tr,), in_specs=[pl.BlockSpec((n_parts, tr, c), lambda i: (0, i, 0)), blk, blk, blk],
        out_specs=[blk] * 4, out_shape=[_sds((r, c), F32)] * 4, compiler_params=_params(),
    )(parts, w, m, v)


def _adamw_slab(name, parts, w, m, v, bufs, f):
    n_parts, r, c = parts.shape
    tr = max(t for t in range(8, 257, 8) if r % t == 0)

    def body(p_ref, w_ref, m_ref, v_ref, *rest):
        g_ref, d_ref, nm_ref, nv_ref = rest[4:]
        g = p_ref[0].astype(F32)
        for k in range(1, n_parts):
            g = g + p_ref[k].astype(F32)
        nm = ADAM_B1 * m_ref[...] + (1.0 - ADAM_B1) * g
        nv = ADAM_B2 * v_ref[...] + (1.0 - ADAM_B2) * (g * g)
        m_hat = nm / (1.0 - ADAM_B1 ** ADAM_STEP)
        v_hat = nv / (1.0 - ADAM_B2 ** ADAM_STEP)
        g_ref[...] = g
        d_ref[...] = -ADAM_LR * (m_hat / (jnp.sqrt(v_hat) + ADAM_EPS) + ADAM_WD * w_ref[...])
        nm_ref[...] = nm
        nv_ref[...] = nv

    blk = pl.BlockSpec((None, tr, c), lambda i: (f, i, 0))
    return pl.pallas_call(
        body, name=name, grid=(r // tr,),
        in_specs=[pl.BlockSpec((n_parts, tr, c), lambda i: (0, i, 0)), blk, blk, blk] + [pl.BlockSpec(memory_space=pl.ANY)] * 4,
        out_specs=[blk] * 4, out_shape=[_sds(w.shape, F32)] * 4, input_output_aliases={4 + j: j for j in range(4)},
        compiler_params=_params(),
    )(parts, w, m, v, *bufs)


def _mesh_pos():
    return lax.axis_index("x"), lax.axis_index("y"), lax.axis_index("c")


def _flip(pos, mask):
    return tuple(1 - p if (mask >> (2 - b)) & 1 else p for b, p in enumerate(pos))


def _index(pos):
    return 4 * pos[0] + 2 * pos[1] + pos[2]


def _all_gather(name, xs, after=None):
    n = len(xs)
    extra = [] if after is None else [after]

    def body(*refs):
        x_refs, o_refs = refs[:n], refs[n + len(extra):2 * n + len(extra)]
        send_sems, recv_sems, local_sems = refs[2 * n + len(extra):]
        me = _mesh_pos()
        sibling = _flip(me, 1)
        others = [_flip(me, 4), _flip(me, 2), _flip(me, 6)]

        def copy(k, j, block, to, src=None):
            dst = o_refs[k].at[_index(block)]
            return pltpu.make_async_remote_copy(
                src_ref=dst if src is None else src, dst_ref=dst, send_sem=send_sems.at[k, j], recv_sem=recv_sems.at[k, j],
                device_id=to, device_id_type=MESH)

        local = [pltpu.make_async_copy(x_refs[k], o_refs[k].at[_index(me)], local_sems.at[k]) for k in range(n)]
        for cp in local:
            cp.start()
        first = []
        for k in range(n):
            first.append(copy(k, 0, me, sibling, src=x_refs[k]))
            first += [copy(k, 1 + j, me, other, src=x_refs[k]) for j, other in enumerate(others)]
        for cp in first:
            cp.start()
        passed = []
        for j, other in enumerate(others):
            for k in range(n):
                copy(k, 1 + j, other, me).wait_recv()
                cp = copy(k, 4 + j, other, sibling)
                cp.start()
                passed.append(cp)
        for k in range(n):
            copy(k, 0, sibling, me).wait_recv()
        for j, other in enumerate(others):
            for k in range(n):
                copy(k, 4 + j, _flip(other, 1), me).wait_recv()
        for cp in first + passed:
            cp.wait_send()
        for cp in local:
            cp.wait()

    any_spec = pl.BlockSpec(memory_space=pl.ANY)
    return pl.pallas_call(
        body, name=name, in_specs=[any_spec] * (n + len(extra)), out_specs=[any_spec] * n,
        out_shape=[_sds((N_DEV,) + x.shape, x.dtype) for x in xs],
        scratch_shapes=[pltpu.SemaphoreType.DMA((n, 7)), pltpu.SemaphoreType.DMA((n, 7)), pltpu.SemaphoreType.DMA((n,))],
    )(*xs, *extra)


_MASKS = {
    "gather": tuple(range(1, N_DEV)),
    "scatter": tuple(range(1, N_DEV)),
    "own": (1, 4, 2, 6),
    "pass": (4, 2, 6),
}


def _split_copies(kind, outgoing, x_refs, land_refs, send_sems, recv_sems):
    me = _mesh_pos()
    masks = _MASKS[kind]
    copies = []
    for k, (x_ref, land_ref) in enumerate(zip(x_refs, land_refs)):
        for j, mask in enumerate(masks):
            if kind == "pass":
                peer = _flip(me, 1)
                src = land_ref.at[_index(_flip(me, mask))]
                dst = land_ref.at[_index(_flip(me, mask if outgoing else mask | 1))]
            else:
                peer = _flip(me, mask)
                src = x_ref.at[_index(peer)] if kind == "scatter" else x_ref
                dst = land_ref.at[_index(me if outgoing else peer)]
            sem = k * len(masks) + j
            copies.append(pltpu.make_async_remote_copy(src_ref=src, dst_ref=dst, send_sem=send_sems.at[sem], recv_sem=recv_sems.at[sem],
                                                       device_id=peer, device_id_type=MESH))
    return copies


_HBM_SPEC = pl.BlockSpec(memory_space=pltpu.HBM)
_SEM_SPEC = pl.BlockSpec(memory_space=pltpu.SEMAPHORE)
_EFFECT = pltpu.SideEffectType.DATAFLOW_SIDE_EFFECTING


def _split_start(name, kind, xs, after=None):
    n = len(xs)
    n_sem = n * len(_MASKS[kind])
    extra = [] if after is None else [after]
    lands = [lax.empty(x.shape if kind == "scatter" else (N_DEV,) + x.shape, x.dtype) for x in xs]

    def body(*refs):
        x_refs, land_refs = refs[:n], refs[n:2 * n]
        send_sems, recv_sems = refs[2 * n + len(extra)], refs[2 * n + len(extra) + 1]
        token = refs[-1]
        for cp in _split_copies(kind, True, x_refs, land_refs, send_sems, recv_sems):
            cp.start()
        token[...] = jnp.zeros_like(token)

    hbm = [pltpu.HBM(a.shape, a.dtype) for a in list(xs) + lands]
    res = pl.pallas_call(
        body, name=name,
        out_shape=[pltpu.SemaphoreType.DMA((n_sem,)), pltpu.SemaphoreType.DMA((n_sem,))] + hbm + [_sds((8, 128), F32)],
        in_specs=[_HBM_SPEC] * (2 * n) + [pl.BlockSpec(memory_space=pl.ANY)] * len(extra),
        out_specs=[_SEM_SPEC, _SEM_SPEC] + [_HBM_SPEC] * (2 * n) + [pl.BlockSpec(memory_space=pltpu.VMEM)],
        input_output_aliases={j: 2 + j for j in range(2 * n)}, compiler_params=pltpu.CompilerParams(has_side_effects=_EFFECT),
    )(*[pltpu.with_memory_space_constraint(a, pltpu.HBM) for a in list(xs) + lands], *extra)
    return (kind, n, res[0], res[1], res[2:2 + 2 * n]), res[-1]


def _split_pass(name, state, after):
    kind, n, send_sems_in, recv_sems_in, thru = state
    n_sem = n * len(_MASKS["pass"])

    def body(*refs):
        x_refs, land_refs = refs[:n], refs[n:2 * n]
        send_sems, recv_sems = refs[2 * n], refs[2 * n + 1]
        next_send, next_recv, token = refs[-3:]
        for cp in _split_copies(kind, True, x_refs, land_refs, send_sems, recv_sems):
            cp.wait_send()
        for cp in _split_copies(kind, False, x_refs, land_refs, send_sems, recv_sems):
            cp.wait_recv()
        for cp in _split_copies("pass", True, land_refs, land_refs, next_send, next_recv):
            cp.start()
        token[...] = jnp.zeros_like(token)

    res = pl.pallas_call(
        body, name=name,
        out_shape=[pltpu.HBM(a.shape, a.dtype) for a in thru] + [pltpu.SemaphoreType.DMA((n_sem,)), pltpu.SemaphoreType.DMA((n_sem,)),
                                                                 _sds((8, 128), F32)],
        in_specs=[_HBM_SPEC] * (2 * n) + [_SEM_SPEC, _SEM_SPEC, pl.BlockSpec(memory_space=pl.ANY)],
        out_specs=[_HBM_SPEC] * (2 * n) + [_SEM_SPEC, _SEM_SPEC, pl.BlockSpec(memory_space=pltpu.VMEM)],
        input_output_aliases={j: j for j in range(2 * n)}, compiler_params=pltpu.CompilerParams(has_side_effects=_EFFECT),
    )(*thru, send_sems_in, recv_sems_in, after)
    return ("pass", n, res[2 * n], res[2 * n + 1], res[:2 * n]), res[-1]


def _split_wait(name, state, after):
    kind, n, send_sems_in, recv_sems_in, thru = state

    def body(*refs):
        x_refs, land_refs = refs[:n], refs[n:2 * n]
        send_sems, recv_sems = refs[2 * n], refs[2 * n + 1]
        for cp in _split_copies(kind, True, x_refs, land_refs, send_sems, recv_sems):
            cp.wait_send()
        for cp in _split_copies(kind, False, x_refs, land_refs, send_sems, recv_sems):
            cp.wait_recv()

    res = pl.pallas_call(
        body, name=name, out_shape=[pltpu.HBM(a.shape, a.dtype) for a in thru],
        in_specs=[_HBM_SPEC] * (2 * n) + [_SEM_SPEC, _SEM_SPEC, pl.BlockSpec(memory_space=pl.ANY)], out_specs=[_HBM_SPEC] * (2 * n),
        input_output_aliases={j: j for j in range(2 * n)}, compiler_params=pltpu.CompilerParams(has_side_effects=_EFFECT),
    )(*thru, send_sems_in, recv_sems_in, after)
    me = _index(_mesh_pos())
    out = []
    for x, land in zip(res[:n], res[n:]):
        own = lax.dynamic_index_in_dim(x, me, 0, keepdims=False) if kind == "scatter" else x
        out.append(lax.dynamic_update_slice(land, own[None], (me,) + (0,) * own.ndim))
    return out


def _rope_tables(s):
    inv = 1.0 / (ROPE_THETA ** (jnp.arange(0, QK_ROPE, 2, dtype=F32) / QK_ROPE))
    ang = jnp.arange(s, dtype=F32)[:, None] * inv[None, :]
    pad = jnp.zeros((s, KV_LORA - QK_ROPE), F32)
    return (jnp.concatenate([jnp.cos(ang), jnp.cos(ang), pad], axis=1), jnp.concatenate([jnp.sin(ang), jnp.sin(ang), pad], axis=1))


def kernel(x, c, ada_w, ada_b, norm_g, ffn_w_in, ffn_w_out, pool_w, pool_b, pool_scale, mla_w_in, mla_q_norm, mla_kv_norm, mla_w_uq, mla_w_uk, mla_w_uv, mla_w_o, loss_target, m_ada_w, m_ada_b, m_norm_g, m_ffn_w_in, m_ffn_w_out, m_pool_w, m_pool_b, m_pool_scale, m_mla_w_in, m_mla_q_norm, m_mla_kv_norm, m_mla_w_uq, m_mla_w_uk, m_mla_w_uv, m_mla_w_o, v_ada_w, v_ada_b, v_norm_g, v_ffn_w_in, v_ffn_w_out, v_pool_w, v_pool_b, v_pool_scale, v_mla_w_in, v_mla_q_norm, v_mla_kv_norm, v_mla_w_uq, v_mla_w_uk, v_mla_w_uv, v_mla_w_o):
    s, d = x.shape[1], x.shape[2]
    tm = min(512, s)
    tr = min(512, s)
    tf = min(1024, s)
    tw = min(4096, s)
    me = 4 * lax.axis_index("x") + 2 * lax.axis_index("y") + lax.axis_index("c")
    x0 = x.reshape(s, d)
    target = loss_target.reshape(s, d)
    n_mod = ada_w.shape[2] * N_DEV // d
    mod_blk = ada_w.shape[2]

    small = jnp.concatenate([c.reshape(-1), norm_g.reshape(-1), pool_b.reshape(-1), mla_q_norm.reshape(-1)]).reshape(1, -1)
    w_in_t, m_in_t, v_in_t = (jnp.swapaxes(a, 2, 3) for a in (ffn_w_in, m_ffn_w_in, v_ffn_w_in))
    w_in_loc = [w_in_t[i, f].astype(BF16) for i in range(2) for f in range(2)]
    w_out_loc = [ffn_w_out[i, f].astype(BF16) for i in range(2) for f in range(2)]
    (small_all,) = _all_gather("gather_small", [small])
    small_all = small_all.reshape(N_DEV, -1)
    c_all = small_all[:, :d]
    off = d
    g_all = small_all[:, off:off + 12 * (d // N_DEV)].reshape(N_DEV, 2, 6, d // N_DEV).transpose(1, 2, 0, 3).reshape(2, 6, d)
    off += 12 * (d // N_DEV)
    pool_b_all = small_all[:, off:off + 4 * POOL_SHARD].reshape(N_DEV, 4, POOL_SHARD).transpose(1, 0, 2).reshape(1, d)
    off += 4 * POOL_SHARD
    q_norm_all = small_all[:, off:off + Q_SHARD].reshape(1, Q_LORA)
    kv_norm_row = mla_kv_norm.reshape(1, KV_LORA)
    pscale_row = pool_scale.reshape(1, d)

    even = (jnp.arange(N_HEADS) % 2 == 0)[:, None, None]
    cos_k, sin_k = _rope_tables(s)

    def mla_weights(mla_w_in_all, mla_w_uq_all, mla_w_o_all):
        uq = mla_w_uq_all.reshape(Q_LORA, N_HEADS, QK_NOPE + QK_ROPE).transpose(1, 0, 2)
        zq = jnp.zeros((N_HEADS, Q_LORA, QK_NOPE), BF16)
        wq = jnp.concatenate(
            [uq[:, :, :QK_NOPE], zq, uq[:, :, QK_NOPE:], jnp.zeros((N_HEADS, Q_LORA, QK_PAD - KV_LORA - QK_ROPE), BF16)], axis=2)
        wukp = jnp.pad(mla_w_uk[0].transpose(1, 2, 0).astype(BF16), ((0, 0), (0, QK_PAD - QK_NOPE), (0, QK_PAD - KV_LORA)))
        uv = mla_w_uv[0].transpose(1, 0, 2).astype(BF16)
        wuv2 = jnp.where(even, jnp.concatenate([uv, jnp.zeros_like(uv)], axis=2), jnp.concatenate([jnp.zeros_like(uv), uv], axis=2))
        return dict(w_in=jnp.pad(mla_w_in_all.reshape(d, -1), ((0, 0), (0, LAT_PAD - mla_w_in.shape[2]))), wq=wq, wukp=wukp,
                    wuv2=wuv2, w_o=mla_w_o_all.reshape(d, d), q_norm=q_norm_all, kv_norm=kv_norm_row)

    (sc_all,), _ = _rowmap("ada_silu", lambda cv: ((cv * jax.nn.sigmoid(cv),), ()), [(c_all, (N_DEV, d), lambda i: (0, 0))],
                           [(_sds((N_DEV, d), F32), (N_DEV, d), lambda i: (0, 0))], [], (1,))
    ada_b_loc = lax.dynamic_slice_in_dim(ada_b, me * mod_blk, mod_blk, axis=1).reshape(2, 1, mod_blk)
    m_pad = 2 * N_DEV
    modp = _matmul("ada_mod", jnp.pad(sc_all, ((0, m_pad - N_DEV), (0, 0))), ada_w, a_blk=(m_pad, d), a_map=lambda i, k: (0, 0),
                   b_blk=(None, d, mod_blk), b_map=lambda i, k: (i, 0, 0), o_shape=(2, m_pad, mod_blk), o_blk=(None, m_pad, mod_blk),
                   o_map=lambda i, k: (i, 0, 0), grid=(2, 1), contract=NN, out_dtype=F32, bias=ada_b_loc, bias_blk=(None, 1, mod_blk),
                   bias_map=lambda i, k: (i, 0, 0))[:, :N_DEV]
    (modp_all,) = _all_gather("gather_mod", [modp.reshape(2 * N_DEV, mod_blk)])
    groups = [[w_in_loc[0], w_out_loc[0], pool_w.reshape(-1, POOL_GROUP).astype(BF16)], [w_in_loc[1], w_out_loc[1]],
              [w_in_loc[2], w_out_loc[2], mla_w_in[0].astype(BF16), mla_w_uq.reshape(mla_w_uq.shape[1], -1).astype(BF16),
               mla_w_o[0].astype(BF16)], [w_in_loc[3], w_out_loc[3]]]
    state, token = _split_start("gather_start_0", "own", groups[0], after=modp_all)
    states = [state]
    w_in8 = [None] * 4
    w_out4 = [None] * 4
    mod = lax.dynamic_index_in_dim(modp_all.reshape(N_DEV, 2, N_DEV, mod_blk), me, axis=2, keepdims=False)
    mod = mod.transpose(1, 0, 2).reshape(2, n_mod, d) + token[0, 0]
    mod_rows = mod.reshape(2 * n_mod, 1, d)
    g_rows = g_all.reshape(12, 1, d)
    weights_of = (0.5, 1.0, 0.5, 0.5, 1.0, 0.5)

    def pre_rows(k, with_shift):
        rows = [_rowk(g_rows, 2 * k), _rowk(mod_rows, 3 * k + 1)]
        return rows + [_rowk(mod_rows, 3 * k)] if with_shift else rows

    def post_rows(k):
        return [_rowk(g_rows, 2 * k + 1), _rowk(mod_rows, 3 * k + 2)]

    def act_dtype(k):
        return F32 if k == 1 else BF16

    saved = []
    xs = x0
    mla_wts = None
    h = _prenorm("prenorm_0", xs, pre_rows(0, True), act_dtype(0), tr)
    token = h
    for name, group in zip("abc", groups[1:]):
        state, token = _split_start(f"gather_start_{name}", "own", group, after=token)
        states.append(state)
    state, token = _split_pass("gather_pass_0", states[0], token)
    lands = _split_wait("gather_wait_0", state, token)
    w_in8[0], w_out4[0] = lands[0], lands[1].reshape(4, FF_BLK, d)
    w4 = lands[2].reshape(N_DEV, 4, POOL_SHARD, POOL_GROUP).transpose(1, 0, 2, 3).reshape(4, POOL_GROUP, POOL_GROUP)
    token = None
    for k in range(6):
        i, sub = divmod(k, 3)
        tag = f"{i}{sub}"
        if k == 1:
            states[1], token = _split_pass("gather_pass_a", states[1], xs)
        if k == 2:
            lands = _split_wait("gather_wait_a", states[1], xs)
            w_in8[1], w_out4[1] = lands[0], lands[1].reshape(4, FF_BLK, d)
            states[2], token = _split_pass("gather_pass_b", states[2], lands[0])
        if k == 3:
            lands = _split_wait("gather_wait_b", states[2], xs)
            w_in8[2], w_out4[2] = lands[0], lands[1].reshape(4, FF_BLK, d)
            mla_wts = mla_weights(*lands[2:])
            states[3], token = _split_pass("gather_pass_c", states[3], lands[0])
        if k == 5:
            lands = _split_wait("gather_wait_c", states[3], xs)
            w_in8[3], w_out4[3] = lands[0], lands[1].reshape(4, FF_BLK, d)
        if sub != 1:
            u, extra = _ffn_fwd(tag, h, w_in8[2 * i + sub // 2], w_out4[2 * i + sub // 2], tf, tf // 2, after=token)
            token = None
        elif i == 0:
            u, extra = _pool_fwd(h, w4, pool_b_all + token[0, 0], min(4 * tm, s))
            token = None
        else:
            u, extra = _mla_fwd(h, mla_wts, cos_k, sin_k, tm)
        saved.append((xs, h, u, extra))
        if k < 5:
            xs, h = _norm_link(f"norm_link_{k + 1}", xs, u, post_rows(k), weights_of[k], pre_rows(k + 1, True), act_dtype(k + 1), tr,
                               u_scale=pscale_row if k == 1 else None)

    dx, du, (sq, dgate, dgpost) = _norm_loss("norm_loss", xs, u, target, post_rows(5), weights_of[5], tr)
    loss_part = (0.5 * jnp.sum(sq) / d).reshape(1, 1)

    d_mod = [None] * (2 * n_mod)
    d_g = [None] * 12
    d_mod[3 * 5 + 2], d_g[2 * 5 + 1] = dgate, dgpost
    sent = {}
    pool_grads = mla_grads = None

    def start_scatter(key, arrays):
        state, token = _split_start(f"scatter_start_{key}", "scatter", arrays)
        sent[key] = state
        return token

    for k in (5, 4, 3, 2, 1, 0):
        i, sub = divmod(k, 3)
        tag = f"{i}{sub}"
        xin, h, u, extra = saved[k]
        token = None
        if sub != 1:
            f = 2 * i + sub // 2
            dh, dgu, act = _ffn_bwd_act(tag, du, extra, w_in8[f], w_out4[f], tf, tf // 4)
            dw_out = _ffn_dw(f"ffn_dwout_{tag}", act, du, tw).reshape(N_DEV, FF_BLK // 2, d)
            if k == 0:
                d_pool_w = pool_grads[0].reshape(4, N_DEV, POOL_SHARD, POOL_GROUP).transpose(1, 0, 2, 3).reshape(N_DEV, -1, POOL_GROUP)
                token = start_scatter(tag + "_out", [dw_out, d_pool_w])
                token = start_scatter(tag + "_in", [_ffn_dw(f"ffn_dwin_{tag}", dgu, h, tw, after=token)])
            else:
                token = start_scatter(tag, [_ffn_dw(f"ffn_dwin_{tag}", dgu, h, tw), dw_out])
        elif i == 0:
            dh, dw4 = _pool_bwd(du, extra, w4, min(4 * tm, s))
            pool_grads = (dw4, *pool_reds)
        else:
            dh, mla_grads = _mla_bwd(du, h, extra, mla_wts, cos_k, sin_k, tm)
            dwq = mla_grads["wq"]
            d_uq = jnp.concatenate([dwq[:, :, :QK_NOPE], dwq[:, :, KV_LORA:KV_LORA + QK_ROPE]], axis=2).transpose(1, 0, 2)
            token = start_scatter("mla", [mla_grads["w_in"][:, :mla_w_in.shape[2]].reshape(N_DEV, d // N_DEV, -1),
                                          d_uq.reshape(N_DEV, Q_LORA // N_DEV, -1), mla_grads["w_o"].reshape(N_DEV, d // N_DEV, d)])
            dwukp, dwuv2 = mla_grads["wukp"], mla_grads["wuv2"]
            d_uk = dwukp[:, :QK_NOPE, :KV_LORA].transpose(2, 0, 1).reshape(KV_LORA, -1)
            d_uv = jnp.where(even, dwuv2[:, :, :V_HEAD], dwuv2[:, :, V_HEAD:]).transpose(1, 0, 2).reshape(KV_LORA, -1)
            state_ukv, token = _split_start("gather_start_ukv", "gather", [d_uk, d_uv], after=token)
        if k > 0:
            dx, du, reds = _norm_link_bwd(f"norm_link_bwd_{k}", dh, xin, dx, saved[k - 1][2], pre_rows(k, False), post_rows(k - 1),
                                          weights_of[k - 1], BF16, tr, after=token, u_scale=pscale_row if k == 2 else None)
            d_mod[3 * (k - 1) + 2], d_g[2 * (k - 1) + 1] = reds[3], reds[4]
            if k == 2:
                pool_reds = reds[5:]
        else:
            dx, reds = _prenorm_bwd("prenorm_bwd_0", dh, xin, dx, pre_rows(0, False), tr, after=token)
        d_mod[3 * k], d_mod[3 * k + 1], d_g[2 * k] = reds[0], reds[1], reds[2]
    grad_x = dx.reshape(x.shape)

    def upd(name, parts, w, m, v):
        shape = w.shape
        r, cdim = parts.shape[1], parts.shape[2]
        return [o.reshape(shape) for o in _adamw(name, parts, w.reshape(r, cdim), m.reshape(r, cdim), v.reshape(r, cdim))]

    def landed(key, after):
        return _split_wait(f"scatter_wait_{key}", sent[key], after)

    res = {}
    w_in_s, m_in_s, v_in_s = (a.reshape(4, FF_BLK, d) for a in (w_in_t, m_in_t, v_in_t))
    w_out_s, m_out_s, v_out_s = (a.reshape(4, FF_BLK // 2, d) for a in (ffn_w_out, m_ffn_w_out, v_ffn_w_out))
    bufs_in = [lax.empty(w_in_s.shape, F32) for _ in range(4)]
    bufs_out = [lax.empty(w_out_s.shape, F32) for _ in range(4)]
    for key, k in (("12", 3), ("mla", None), ("10", 2), ("02", 1)):
        parts = landed(key, grad_x)
        if k is None:
            res["mla_w_in"] = upd("adam_mla_w_in", parts[0], mla_w_in, m_mla_w_in, v_mla_w_in)
            res["mla_w_uq"] = upd("adam_mla_w_uq", parts[1], mla_w_uq, m_mla_w_uq, v_mla_w_uq)
            res["mla_w_o"] = upd("adam_mla_w_o", parts[2], mla_w_o, m_mla_w_o, v_mla_w_o)
            uk_all, uv_all = _split_wait("gather_wait_ukv", state_ukv, grad_x)
            res["mla_w_uk"] = upd("adam_mla_w_uk", uk_all, mla_w_uk, m_mla_w_uk, v_mla_w_uk)
            res["mla_w_uv"] = upd("adam_mla_w_uv", uv_all, mla_w_uv, m_mla_w_uv, v_mla_w_uv)
            continue
        bufs_in = _adamw_slab(f"adam_ffn_w_in_{key}", parts[0], w_in_s, m_in_s, v_in_s, bufs_in, k)
        bufs_out = _adamw_slab(f"adam_ffn_w_out_{key}", parts[1], w_out_s, m_out_s, v_out_s, bufs_out, k)

    dw4, dpscale, dpb = pool_grads
    small_g = jnp.concatenate(d_mod + d_g + [dpb, dpscale, mla_grads["q_norm"], mla_grads["kv_norm"], loss_part], axis=1)
    done = [bufs_in[0], bufs_out[0], res["mla_w_o"][0], res["mla_w_uv"][0]]
    (small_g_all,) = _all_gather("gather_small_grads", [small_g], after=sum(a.reshape(-1)[:1] for a in done))
    small_g_all = small_g_all.reshape(N_DEV, -1)
    n_m = 2 * n_mod * d
    d_mod_all = small_g_all[:, :n_m].reshape(N_DEV, 2, n_mod * d)
    rest = small_g_all[:, n_m:]
    p_norm_g = lax.dynamic_slice_in_dim(rest[:, :12 * d].reshape(N_DEV, 12, d), me * (d // N_DEV), d // N_DEV, axis=2)
    p_pool_b = lax.dynamic_slice_in_dim(rest[:, 12 * d:13 * d].reshape(N_DEV, 4, POOL_GROUP), me * POOL_SHARD, POOL_SHARD, axis=2)
    p_pool_scale = rest[:, 13 * d:14 * d].reshape(N_DEV, 1, d)
    p_q_norm = lax.dynamic_slice_in_dim(rest[:, 14 * d:14 * d + Q_LORA], me * Q_SHARD, Q_SHARD, axis=1).reshape(N_DEV, 1, Q_SHARD)
    p_kv_norm = rest[:, 14 * d + Q_LORA:14 * d + Q_LORA + KV_LORA].reshape(N_DEV, 1, KV_LORA)
    loss = jnp.sum(rest[:, -1])

    d_mod_loc = lax.dynamic_slice_in_dim(d_mod_all, me * mod_blk, mod_blk, axis=2).transpose(1, 0, 2)
    k_pad = 128
    sc_t = jnp.pad(sc_all.T, ((0, 0), (0, k_pad - N_DEV)))
    d_ada_w = _matmul("ada_dw", sc_t, jnp.pad(d_mod_loc, ((0, 0), (0, k_pad - N_DEV), (0, 0))), a_blk=(d, k_pad),
                      a_map=lambda i, k: (0, 0), b_blk=(None, k_pad, mod_blk), b_map=lambda i, k: (i, 0, 0), o_shape=(2, d, mod_blk),
                      o_blk=(None, d, mod_blk), o_map=lambda i, k: (i, 0, 0), grid=(2, 1), contract=NN, out_dtype=F32)
    res["ada_w"] = upd("adam_ada_w", d_ada_w.reshape(1, 2 * d, mod_blk), ada_w, m_ada_w, v_ada_w)
    res["ada_b"] = upd("adam_ada_b", d_mod_all.reshape(N_DEV, 2, n_mod * d), ada_b, m_ada_b, v_ada_b)
    res["norm_g"] = upd("adam_norm_g", p_norm_g, norm_g, m_norm_g, v_norm_g)
    res["pool_b"] = upd("adam_pool_b", p_pool_b, pool_b, m_pool_b, v_pool_b)
    res["pool_scale"] = upd("adam_pool_scale", p_pool_scale, pool_scale, m_pool_scale, v_pool_scale)
    res["mla_q_norm"] = upd("adam_mla_q_norm", p_q_norm, mla_q_norm, m_mla_q_norm, v_mla_q_norm)
    res["mla_kv_norm"] = upd("adam_mla_kv_norm", p_kv_norm, mla_kv_norm, m_mla_kv_norm, v_mla_kv_norm)

    p_out, p_pool_w = landed("00_out", res["ada_w"][1])
    bufs_out = _adamw_slab("adam_ffn_w_out_00", p_out, w_out_s, m_out_s, v_out_s, bufs_out, 0)
    res["pool_w"] = upd("adam_pool_w", p_pool_w, pool_w, m_pool_w, v_pool_w)
    (p_in,) = landed("00_in", res["pool_w"][1])
    bufs_in = _adamw_slab("adam_ffn_w_in_00", p_in, w_in_s, m_in_s, v_in_s, bufs_in, 0)
    res["ffn_w_in"] = [jnp.swapaxes(b.reshape(w_in_t.shape), 2, 3) for b in bufs_in]
    res["ffn_w_out"] = [b.reshape(ffn_w_out.shape) for b in bufs_out]

    order = ["ada_w", "ada_b", "norm_g", "ffn_w_in", "ffn_w_out", "pool_w", "pool_b", "pool_scale", "mla_w_in", "mla_q_norm",
             "mla_kv_norm", "mla_w_uq", "mla_w_uk", "mla_w_uv", "mla_w_o"]
    outs = [loss, grad_x]
    for j in range(4):
        outs += [res[name][j] for name in order]
    return tuple(outs)
```
